```python
import math
import jax, jax.numpy as jnp
from jax import lax
import numpy as np

D_MODEL = 1024
BATCH = 32
SEQ = 2048
DEPTH = 4

ATTN_WIDTH = D_MODEL // 2
SSM_WIDTH = D_MODEL - ATTN_WIDTH
HEAD_DIM = 64
N_HEADS = ATTN_WIDTH // HEAD_DIM
DILATION_PAIRS = ((128, 1), (512, 4), (2048, 16))
ROPE_THETA = 10000.0
SSM_GROUP_DIM = 16
SSM_GROUPS = SSM_WIDTH // SSM_GROUP_DIM
SSM_STATE = 64
DT_MIN = 0.001
DT_MAX = 0.1
D_FF = 4 * D_MODEL
IN_WIDTH = 3 * ATTN_WIDTH + SSM_WIDTH
LN_EPS = 1e-5
RMS_EPS = 1e-6
NEG_INF = -1e30
DEEPNORM_ALPHA = (2.0 * DEPTH) ** 0.25
DEEPNORM_BETA = (8.0 * DEPTH) ** -0.25

kernel_name = "hymba_s5_dilated_attn_deepnorm_trunk"


def layer_norm(x, g, b):
    xf = x.astype(jnp.float32)
    mu = jnp.mean(xf, axis=-1, keepdims=True)
    var = jnp.mean(jnp.square(xf - mu), axis=-1, keepdims=True)
    y = (xf - mu) * lax.rsqrt(var + LN_EPS) * g.astype(jnp.float32) + b.astype(jnp.float32)
    return y.astype(x.dtype)


def rms_norm(x, g):
    xf = x.astype(jnp.float32)
    y = xf * lax.rsqrt(jnp.mean(jnp.square(xf), axis=-1, keepdims=True) + RMS_EPS)
    return y * g.astype(jnp.float32)


def rope(t, positions):
    half = t.shape[-1] // 2
    inv_freq = ROPE_THETA ** (-jnp.arange(half, dtype=jnp.float32) * 2.0 / t.shape[-1])
    ang = positions.astype(jnp.float32)[..., None] * inv_freq
    cos = jnp.cos(ang)[:, :, None, :]
    sin = jnp.sin(ang)[:, :, None, :]
    t1, t2 = t[..., :half], t[..., half:]
    return jnp.concatenate([t1 * cos - t2 * sin, t1 * sin + t2 * cos], axis=-1)


def dilated_branch(q, k, v, window, dilation):
    bsz, s, h, e = q.shape
    nk = window // dilation
    qb_size = nk
    sub_len = s // dilation
    nb = -(-sub_len // qb_size)
    padded = nb * qb_size
    pad = padded - sub_len

    def to_sub(t):
        return t.reshape(bsz, sub_len, dilation, h, e).transpose(0, 2, 3, 1, 4)

    qs = jnp.pad(to_sub(q), ((0, 0), (0, 0), (0, 0), (0, pad), (0, 0)))
    ks = jnp.pad(to_sub(k), ((0, 0), (0, 0), (0, 0), (qb_size, pad), (0, 0)))
    vs = jnp.pad(to_sub(v), ((0, 0), (0, 0), (0, 0), (qb_size, pad), (0, 0)))
    qblk = qs.reshape(bsz, dilation, h, nb, qb_size, e)
    kb = ks.reshape(bsz, dilation, h, nb + 1, qb_size, e)
    vb = vs.reshape(bsz, dilation, h, nb + 1, qb_size, e)
    keys = jnp.concatenate([kb[:, :, :, :-1], kb[:, :, :, 1:]], axis=4)
    vals = jnp.concatenate([vb[:, :, :, :-1], vb[:, :, :, 1:]], axis=4)

    scores = jnp.einsum('bdhnqe,bdhnke->bdhnqk', qblk, keys)
    qi = jnp.arange(qb_size)[:, None]
    kj = jnp.arange(2 * qb_size)[None, :]
    dist = qi + qb_size - kj
    key_idx = jnp.arange(nb)[:, None, None] * qb_size - qb_size + kj[None]
    valid = (dist >= 0)[None] & (dist <= nk)[None] & (key_idx >= 0)
    scores = jnp.where(valid, scores, NEG_INF)
    m = jnp.max(scores, axis=-1, keepdims=True)
    p = jnp.exp(scores - m)
    den = jnp.sum(p, axis=-1, keepdims=True)
    o = jnp.einsum('bdhnqk,bdhnke->bdhnqe', p, vals) / den
    lse = (m + jnp.log(den))[..., 0]

    o = o.reshape(bsz, dilation, h, padded, e)[:, :, :, :sub_len]
    o = o.transpose(0, 3, 1, 2, 4).reshape(bsz, s, h, e)
    lse = lse.reshape(bsz, dilation, h, padded)[:, :, :, :sub_len]
    lse = lse.transpose(0, 3, 1, 2).reshape(bsz, s, h)
    return o, lse


def dilated_attention(q, k, v, positions):
    bsz, s, _ = q.shape
    q = rope(q.astype(jnp.float32).reshape(bsz, s, N_HEADS, HEAD_DIM), positions)
    k = rope(k.astype(jnp.float32).reshape(bsz, s, N_HEADS, HEAD_DIM), positions)
    v = v.astype(jnp.float32).reshape(bsz, s, N_HEADS, HEAD_DIM)
    q = q * (HEAD_DIM ** -0.5)
    outs, lses = [], []
    for window, dilation in DILATION_PAIRS:
        o, lse = dilated_branch(q, k, v, window, dilation)
        outs.append(o)
        lses.append(lse)
    w = jax.nn.softmax(jnp.stack(lses, axis=0), axis=0)
    o = jnp.sum(w[..., None] * jnp.stack(outs, axis=0), axis=0)
    return o.reshape(bsz, s, ATTN_WIDTH)


def _ssm_combine(e1, e2):
    a1r, a1i, b1r, b1i = e1
    a2r, a2i, b2r, b2i = e2
    ar = a2r * a1r - a2i * a1i
    ai = a2r * a1i + a2i * a1r
    br = a2r * b1r - a2i * b1i + b2r
    bi = a2r * b1i + a2i * b1r + b2i
    return (ar, ai, br, bi)


def s5_mixer(u, a_re, a_im, log_dt, b_re, b_im, c_re, c_im, d_skip, w_glu, b_glu):
    bsz, s, _ = u.shape
    f32 = jnp.float32
    uf = u.astype(f32).reshape(bsz, s, SSM_GROUPS, SSM_GROUP_DIM)
    a_re = a_re.astype(f32)
    a_im = a_im.astype(f32)
    dt = jnp.exp(log_dt.astype(f32))[:, None]
    mag = jnp.exp(a_re * dt)
    ang = a_im * dt
    lb_re = mag * jnp.cos(ang)
    lb_im = mag * jnp.sin(ang)
    den = a_re * a_re + a_im * a_im
    nr = lb_re - 1.0
    ni = lb_im
    cr = (nr * a_re + ni * a_im) / den
    ci = (ni * a_re - nr * a_im) / den
    b_re = b_re.astype(f32)
    b_im = b_im.astype(f32)
    bb_re = cr[..., None] * b_re - ci[..., None] * b_im
    bb_im = cr[..., None] * b_im + ci[..., None] * b_re
    bu_re = jnp.einsum('bsgn,gpn->bsgp', uf, bb_re)
    bu_im = jnp.einsum('bsgn,gpn->bsgp', uf, bb_im)
    lam_re = jnp.broadcast_to(lb_re[None, None], (1, s, SSM_GROUPS, SSM_STATE))
    lam_im = jnp.broadcast_to(lb_im[None, None], (1, s, SSM_GROUPS, SSM_STATE))
    _, _, xr, xi = lax.associative_scan(_ssm_combine, (lam_re, lam_im, bu_re, bu_im), axis=1)
    y = (jnp.einsum('bsgp,gnp->bsgn', xr, c_re.astype(f32))
         - jnp.einsum('bsgp,gnp->bsgn', xi, c_im.astype(f32))
         + d_skip.astype(f32) * uf)
    y = jax.nn.gelu(y.reshape(bsz, s, SSM_WIDTH))
    y = y * jax.nn.sigmoid(y @ w_glu.astype(f32) + b_glu.astype(f32))
    return y


def _fwd_setup_inputs(seed: int = 0) -> dict:
    key = jax.random.key(seed)
    ks = jax.random.split(key, 32)
    L = DEPTH
    nrm = jax.random.normal
    f32 = jnp.float32
    x = nrm(ks[0], (BATCH, SEQ, D_MODEL), f32)
    positions = (jax.random.randint(ks[1], (BATCH, 1), 0, 1024, dtype=jnp.int32)
                 + jnp.arange(SEQ, dtype=jnp.int32)[None, :])
    w_in = nrm(ks[2], (L, D_MODEL, IN_WIDTH), f32) * D_MODEL ** -0.5
    attn_gain = 1.0 + 0.02 * nrm(ks[3], (L, ATTN_WIDTH), f32)
    ssm_gain = 1.0 + 0.02 * nrm(ks[4], (L, SSM_WIDTH), f32)
    ssm_a_re = -0.5 + 0.01 * nrm(ks[5], (L, SSM_GROUPS, SSM_STATE), f32)
    ssm_a_im = (math.pi * jnp.arange(SSM_STATE, dtype=f32)[None, None, :]
                + 0.01 * nrm(ks[6], (L, SSM_GROUPS, SSM_STATE), f32))
    ssm_log_dt = jax.random.uniform(ks[7], (L, SSM_GROUPS), f32,
                                    math.log(DT_MIN), math.log(DT_MAX))
    bs = (2.0 * SSM_GROUP_DIM) ** -0.5
    cs = (2.0 * SSM_STATE) ** -0.5
    ssm_b_re = nrm(ks[8], (L, SSM_GROUPS, SSM_STATE, SSM_GROUP_DIM), f32) * bs
    ssm_b_im = nrm(ks[9], (L, SSM_GROUPS, SSM_STATE, SSM_GROUP_DIM), f32) * bs
    ssm_c_re = nrm(ks[10], (L, SSM_GROUPS, SSM_GROUP_DIM, SSM_STATE), f32) * cs
    ssm_c_im = nrm(ks[11], (L, SSM_GROUPS, SSM_GROUP_DIM, SSM_STATE), f32) * cs
    ssm_d = nrm(ks[12], (L, SSM_GROUPS, SSM_GROUP_DIM), f32)
    w_glu = nrm(ks[13], (L, SSM_WIDTH, SSM_WIDTH), f32) * SSM_WIDTH ** -0.5
    b_glu = 0.01 * nrm(ks[14], (L, SSM_WIDTH), f32)
    w_out = nrm(ks[15], (L, D_MODEL, D_MODEL), f32) * D_MODEL ** -0.5 * DEEPNORM_BETA
    b_out = 0.01 * nrm(ks[16], (L, D_MODEL), f32)
    ln1_g = 1.0 + 0.02 * nrm(ks[17], (L, D_MODEL), f32)
    ln1_b = 0.01 * nrm(ks[18], (L, D_MODEL), f32)
    w_ff1 = nrm(ks[19], (L, D_MODEL, D_FF), f32) * D_MODEL ** -0.5
    b_ff1 = 0.01 * nrm(ks[20], (L, D_FF), f32)
    w_ff2 = nrm(ks[21], (L, D_FF, D_MODEL), f32) * D_FF ** -0.5 * DEEPNORM_BETA
    b_ff2 = 0.01 * nrm(ks[22], (L, D_MODEL), f32)
    ln2_g = 1.0 + 0.02 * nrm(ks[23], (L, D_MODEL), f32)
    ln2_b = 0.01 * nrm(ks[24], (L, D_MODEL), f32)
    return {"x": x, "positions": positions, "w_in": w_in, "attn_gain": attn_gain,
            "ssm_gain": ssm_gain, "ssm_a_re": ssm_a_re, "ssm_a_im": ssm_a_im,
            "ssm_log_dt": ssm_log_dt, "ssm_b_re": ssm_b_re, "ssm_b_im": ssm_b_im,
            "ssm_c_re": ssm_c_re, "ssm_c_im": ssm_c_im, "ssm_d": ssm_d,
            "w_glu": w_glu, "b_glu": b_glu, "w_out": w_out, "b_out": b_out,
            "ln1_g": ln1_g, "ln1_b": ln1_b, "w_ff1": w_ff1, "b_ff1": b_ff1,
            "w_ff2": w_ff2, "b_ff2": b_ff2, "ln2_g": ln2_g, "ln2_b": ln2_b}


def _fwd_reference(x, positions, w_in, attn_gain, ssm_gain, ssm_a_re, ssm_a_im, ssm_log_dt,
              ssm_b_re, ssm_b_im, ssm_c_re, ssm_c_im, ssm_d, w_glu, b_glu, w_out, b_out,
              ln1_g, ln1_b, w_ff1, b_ff1, w_ff2, b_ff2, ln2_g, ln2_b):
    h = x
    for l in range(DEPTH):
        proj = h @ w_in[l]
        q = proj[..., :ATTN_WIDTH]
        k = proj[..., ATTN_WIDTH:2 * ATTN_WIDTH]
        v = proj[..., 2 * ATTN_WIDTH:3 * ATTN_WIDTH]
        u = proj[..., 3 * ATTN_WIDTH:]
        attn = dilated_attention(q, k, v, positions)
        ssm = s5_mixer(u, ssm_a_re[l], ssm_a_im[l], ssm_log_dt[l], ssm_b_re[l], ssm_b_im[l],
                       ssm_c_re[l], ssm_c_im[l], ssm_d[l], w_glu[l], b_glu[l])
        mixed = jnp.concatenate([rms_norm(attn, attn_gain[l]), rms_norm(ssm, ssm_gain[l])],
                                axis=-1).astype(h.dtype)
        mix_out = mixed @ w_out[l] + b_out[l]
        h = layer_norm(DEEPNORM_ALPHA * h + mix_out, ln1_g[l], ln1_b[l])
        ff = jnp.square(jax.nn.relu(h @ w_ff1[l] + b_ff1[l])) @ w_ff2[l] + b_ff2[l]
        h = layer_norm(DEEPNORM_ALPHA * h + ff, ln2_g[l], ln2_b[l])
    return h


import jax as _jax
import jax.numpy as _jnp

TWIN_FORMAT = 'train_step'
FWD_PARAMS = ['x', 'positions', 'w_in', 'attn_gain', 'ssm_gain', 'ssm_a_re', 'ssm_a_im', 'ssm_log_dt', 'ssm_b_re', 'ssm_b_im', 'ssm_c_re', 'ssm_c_im', 'ssm_d', 'w_glu', 'b_glu', 'w_out', 'b_out', 'ln1_g', 'ln1_b', 'w_ff1', 'b_ff1', 'w_ff2', 'b_ff2', 'ln2_g', 'ln2_b']
TWIN_WEIGHTS = ['w_in', 'attn_gain', 'ssm_gain', 'ssm_a_re', 'ssm_a_im', 'ssm_log_dt', 'ssm_b_re', 'ssm_b_im', 'ssm_c_re', 'ssm_c_im', 'ssm_d', 'w_glu', 'b_glu', 'w_out', 'b_out', 'ln1_g', 'ln1_b', 'w_ff1', 'b_ff1', 'w_ff2', 'b_ff2', 'ln2_g', 'ln2_b']
TWIN_DIFF_INPUT = 'x'
TWIN_INPUTS = ['x', 'positions', 'w_in', 'attn_gain', 'ssm_gain', 'ssm_a_re', 'ssm_a_im', 'ssm_log_dt', 'ssm_b_re', 'ssm_b_im', 'ssm_c_re', 'ssm_c_im', 'ssm_d', 'w_glu', 'b_glu', 'w_out', 'b_out', 'ln1_g', 'ln1_b', 'w_ff1', 'b_ff1', 'w_ff2', 'b_ff2', 'ln2_g', 'ln2_b', 'loss_target', 'm_w_in', 'm_attn_gain', 'm_ssm_gain', 'm_ssm_a_re', 'm_ssm_a_im', 'm_ssm_log_dt', 'm_ssm_b_re', 'm_ssm_b_im', 'm_ssm_c_re', 'm_ssm_c_im', 'm_ssm_d', 'm_w_glu', 'm_b_glu', 'm_w_out', 'm_b_out', 'm_ln1_g', 'm_ln1_b', 'm_w_ff1', 'm_b_ff1', 'm_w_ff2', 'm_b_ff2', 'm_ln2_g', 'm_ln2_b', 'v_w_in', 'v_attn_gain', 'v_ssm_gain', 'v_ssm_a_re', 'v_ssm_a_im', 'v_ssm_log_dt', 'v_ssm_b_re', 'v_ssm_b_im', 'v_ssm_c_re', 'v_ssm_c_im', 'v_ssm_d', 'v_w_glu', 'v_b_glu', 'v_w_out', 'v_b_out', 'v_ln1_g', 'v_ln1_b', 'v_w_ff1', 'v_b_ff1', 'v_w_ff2', 'v_b_ff2', 'v_ln2_g', 'v_ln2_b']
TWIN_OUTPUTS = ['loss', 'grad_x', 'grad_w_in', 'grad_attn_gain', 'grad_ssm_gain', 'grad_ssm_a_re', 'grad_ssm_a_im', 'grad_ssm_log_dt', 'grad_ssm_b_re', 'grad_ssm_b_im', 'grad_ssm_c_re', 'grad_ssm_c_im', 'grad_ssm_d', 'grad_w_glu', 'grad_b_glu', 'grad_w_out', 'grad_b_out', 'grad_ln1_g', 'grad_ln1_b', 'grad_w_ff1', 'grad_b_ff1', 'grad_w_ff2', 'grad_b_ff2', 'grad_ln2_g', 'grad_ln2_b', 'delta_w_in', 'delta_attn_gain', 'delta_ssm_gain', 'delta_ssm_a_re', 'delta_ssm_a_im', 'delta_ssm_log_dt', 'delta_ssm_b_re', 'delta_ssm_b_im', 'delta_ssm_c_re', 'delta_ssm_c_im', 'delta_ssm_d', 'delta_w_glu', 'delta_b_glu', 'delta_w_out', 'delta_b_out', 'delta_ln1_g', 'delta_ln1_b', 'delta_w_ff1', 'delta_b_ff1', 'delta_w_ff2', 'delta_b_ff2', 'delta_ln2_g', 'delta_ln2_b', 'new_m_w_in', 'new_m_attn_gain', 'new_m_ssm_gain', 'new_m_ssm_a_re', 'new_m_ssm_a_im', 'new_m_ssm_log_dt', 'new_m_ssm_b_re', 'new_m_ssm_b_im', 'new_m_ssm_c_re', 'new_m_ssm_c_im', 'new_m_ssm_d', 'new_m_w_glu', 'new_m_b_glu', 'new_m_w_out', 'new_m_b_out', 'new_m_ln1_g', 'new_m_ln1_b', 'new_m_w_ff1', 'new_m_b_ff1', 'new_m_w_ff2', 'new_m_b_ff2', 'new_m_ln2_g', 'new_m_ln2_b', 'new_v_w_in', 'new_v_attn_gain', 'new_v_ssm_gain', 'new_v_ssm_a_re', 'new_v_ssm_a_im', 'new_v_ssm_log_dt', 'new_v_ssm_b_re', 'new_v_ssm_b_im', 'new_v_ssm_c_re', 'new_v_ssm_c_im', 'new_v_ssm_d', 'new_v_w_glu', 'new_v_b_glu', 'new_v_w_out', 'new_v_b_out', 'new_v_ln1_g', 'new_v_ln1_b', 'new_v_w_ff1', 'new_v_b_ff1', 'new_v_w_ff2', 'new_v_b_ff2', 'new_v_ln2_g', 'new_v_ln2_b']
TWIN_LEAF_KINDS = {'loss': 'loss', 'grad_x': 'grad_x', 'grad_w_in': 'grad_w', 'grad_attn_gain': 'grad_w', 'grad_ssm_gain': 'grad_w', 'grad_ssm_a_re': 'grad_w', 'grad_ssm_a_im': 'grad_w', 'grad_ssm_log_dt': 'grad_w', 'grad_ssm_b_re': 'grad_w', 'grad_ssm_b_im': 'grad_w', 'grad_ssm_c_re': 'grad_w', 'grad_ssm_c_im': 'grad_w', 'grad_ssm_d': 'grad_w', 'grad_w_glu': 'grad_w', 'grad_b_glu': 'grad_w', 'grad_w_out': 'grad_w', 'grad_b_out': 'grad_w', 'grad_ln1_g': 'grad_w', 'grad_ln1_b': 'grad_w', 'grad_w_ff1': 'grad_w', 'grad_b_ff1': 'grad_w', 'grad_w_ff2': 'grad_w', 'grad_b_ff2': 'grad_w', 'grad_ln2_g': 'grad_w', 'grad_ln2_b': 'grad_w', 'delta_w_in': 'delta_w', 'delta_attn_gain': 'delta_w', 'delta_ssm_gain': 'delta_w', 'delta_ssm_a_re': 'delta_w', 'delta_ssm_a_im': 'delta_w', 'delta_ssm_log_dt': 'delta_w', 'delta_ssm_b_re': 'delta_w', 'delta_ssm_b_im': 'delta_w', 'delta_ssm_c_re': 'delta_w', 'delta_ssm_c_im': 'delta_w', 'delta_ssm_d': 'delta_w', 'delta_w_glu': 'delta_w', 'delta_b_glu': 'delta_w', 'delta_w_out': 'delta_w', 'delta_b_out': 'delta_w', 'delta_ln1_g': 'delta_w', 'delta_ln1_b': 'delta_w', 'delta_w_ff1': 'delta_w', 'delta_b_ff1': 'delta_w', 'delta_w_ff2': 'delta_w', 'delta_b_ff2': 'delta_w', 'delta_ln2_g': 'delta_w', 'delta_ln2_b': 'delta_w', 'new_m_w_in': 'new_m', 'new_m_attn_gain': 'new_m', 'new_m_ssm_gain': 'new_m', 'new_m_ssm_a_re': 'new_m', 'new_m_ssm_a_im': 'new_m', 'new_m_ssm_log_dt': 'new_m', 'new_m_ssm_b_re': 'new_m', 'new_m_ssm_b_im': 'new_m', 'new_m_ssm_c_re': 'new_m', 'new_m_ssm_c_im': 'new_m', 'new_m_ssm_d': 'new_m', 'new_m_w_glu': 'new_m', 'new_m_b_glu': 'new_m', 'new_m_w_out': 'new_m', 'new_m_b_out': 'new_m', 'new_m_ln1_g': 'new_m', 'new_m_ln1_b': 'new_m', 'new_m_w_ff1': 'new_m', 'new_m_b_ff1': 'new_m', 'new_m_w_ff2': 'new_m', 'new_m_b_ff2': 'new_m', 'new_m_ln2_g': 'new_m', 'new_m_ln2_b': 'new_m', 'new_v_w_in': 'new_v', 'new_v_attn_gain': 'new_v', 'new_v_ssm_gain': 'new_v', 'new_v_ssm_a_re': 'new_v', 'new_v_ssm_a_im': 'new_v', 'new_v_ssm_log_dt': 'new_v', 'new_v_ssm_b_re': 'new_v', 'new_v_ssm_b_im': 'new_v', 'new_v_ssm_c_re': 'new_v', 'new_v_ssm_c_im': 'new_v', 'new_v_ssm_d': 'new_v', 'new_v_w_glu': 'new_v', 'new_v_b_glu': 'new_v', 'new_v_w_out': 'new_v', 'new_v_b_out': 'new_v', 'new_v_ln1_g': 'new_v', 'new_v_ln1_b': 'new_v', 'new_v_w_ff1': 'new_v', 'new_v_b_ff1': 'new_v', 'new_v_w_ff2': 'new_v', 'new_v_b_ff2': 'new_v', 'new_v_ln2_g': 'new_v', 'new_v_ln2_b': 'new_v'}


def _forward(args):
    return _fwd_reference(*[args[k] for k in FWD_PARAMS])


def _output_shape():
    out = _jax.eval_shape(lambda: _forward(_fwd_setup_inputs(0)))
    return out.shape, out.dtype

N_MICROBATCH = 1
ADAM_LR = 0.001
ADAM_B1 = 0.9
ADAM_B2 = 0.999
ADAM_EPS = 1e-08
ADAM_WD = 0.01
ADAM_STEP = 10
PER_EXAMPLE_BATCH_AXIS = {'x': 0, 'positions': 0, 'loss_target': 0}
SHARED_INPUTS = []
_WEIGHT_DTYPES = {'w_in': _jnp.float32, 'attn_gain': _jnp.float32, 'ssm_gain': _jnp.float32, 'ssm_a_re': _jnp.float32, 'ssm_a_im': _jnp.float32, 'ssm_log_dt': _jnp.float32, 'ssm_b_re': _jnp.float32, 'ssm_b_im': _jnp.float32, 'ssm_c_re': _jnp.float32, 'ssm_c_im': _jnp.float32, 'ssm_d': _jnp.float32, 'w_glu': _jnp.float32, 'b_glu': _jnp.float32, 'w_out': _jnp.float32, 'b_out': _jnp.float32, 'ln1_g': _jnp.float32, 'ln1_b': _jnp.float32, 'w_ff1': _jnp.float32, 'b_ff1': _jnp.float32, 'w_ff2': _jnp.float32, 'b_ff2': _jnp.float32, 'ln2_g': _jnp.float32, 'ln2_b': _jnp.float32}
MOMENT_SCALE = {'w_in': 6.970628e-02, 'attn_gain': 1.196687e-01, 'ssm_gain': 1.112743e-01, 'ssm_a_re': 4.676311e-03, 'ssm_a_im': 4.292138e-03, 'ssm_log_dt': 4.249715e+00, 'ssm_b_re': 3.006472e-03, 'ssm_b_im': 3.194461e-03, 'ssm_c_re': 5.808011e-03, 'ssm_c_im': 6.159007e-03, 'ssm_d': 1.143410e-01, 'w_glu': 2.132866e-02, 'b_glu': 5.046270e-02, 'w_out': 2.659886e-01, 'b_out': 5.157359e-01, 'ln1_g': 1.357443e+00, 'ln1_b': 8.944111e-01, 'w_ff1': 4.439937e-02, 'b_ff1': 8.525465e-02, 'w_ff2': 2.731114e-01, 'b_ff2': 5.289308e-01, 'ln2_g': 3.271534e+01, 'ln2_b': 1.127139e+01}


def _to_microbatches(a, axis):
    t = _jnp.moveaxis(a, axis, 0)
    t = t.reshape((N_MICROBATCH, t.shape[0] // N_MICROBATCH) + t.shape[1:])
    return _jnp.moveaxis(t, 1, axis + 1)


def setup_inputs(seed: int = 0) -> dict:
    inp = _fwd_setup_inputs(seed)
    key = _jax.random.fold_in(_jax.random.key(seed), 7919)
    shape, _ = _output_shape()
    out = dict(inp)
    out["loss_target"] = _jax.random.normal(_jax.random.fold_in(key, 0), shape, _jnp.float32)
    for i, name in enumerate(TWIN_WEIGHTS):
        w = inp[name].astype(_jnp.float32)
        if MOMENT_SCALE is None:
            s = _jnp.sqrt(_jnp.mean(_jnp.square(w)) + 1e-30)
        else:
            s = MOMENT_SCALE[name]
        km, kv = _jax.random.split(_jax.random.fold_in(key, i + 1))
        out[name] = w
        out["m_" + name] = s * _jax.random.normal(km, w.shape, _jnp.float32)
        out["v_" + name] = (s * s) * _jax.random.uniform(kv, w.shape, _jnp.float32, 0.5, 1.5)
    if N_MICROBATCH > 1:
        for name, axis in PER_EXAMPLE_BATCH_AXIS.items():
            out[name] = _to_microbatches(out[name], axis)
    return {'x': out['x'], 'positions': out['positions'], 'w_in': out['w_in'], 'attn_gain': out['attn_gain'], 'ssm_gain': out['ssm_gain'], 'ssm_a_re': out['ssm_a_re'], 'ssm_a_im': out['ssm_a_im'], 'ssm_log_dt': out['ssm_log_dt'], 'ssm_b_re': out['ssm_b_re'], 'ssm_b_im': out['ssm_b_im'], 'ssm_c_re': out['ssm_c_re'], 'ssm_c_im': out['ssm_c_im'], 'ssm_d': out['ssm_d'], 'w_glu': out['w_glu'], 'b_glu': out['b_glu'], 'w_out': out['w_out'], 'b_out': out['b_out'], 'ln1_g': out['ln1_g'], 'ln1_b': out['ln1_b'], 'w_ff1': out['w_ff1'], 'b_ff1': out['b_ff1'], 'w_ff2': out['w_ff2'], 'b_ff2': out['b_ff2'], 'ln2_g': out['ln2_g'], 'ln2_b': out['ln2_b'], 'loss_target': out['loss_target'], 'm_w_in': out['m_w_in'], 'm_attn_gain': out['m_attn_gain'], 'm_ssm_gain': out['m_ssm_gain'], 'm_ssm_a_re': out['m_ssm_a_re'], 'm_ssm_a_im': out['m_ssm_a_im'], 'm_ssm_log_dt': out['m_ssm_log_dt'], 'm_ssm_b_re': out['m_ssm_b_re'], 'm_ssm_b_im': out['m_ssm_b_im'], 'm_ssm_c_re': out['m_ssm_c_re'], 'm_ssm_c_im': out['m_ssm_c_im'], 'm_ssm_d': out['m_ssm_d'], 'm_w_glu': out['m_w_glu'], 'm_b_glu': out['m_b_glu'], 'm_w_out': out['m_w_out'], 'm_b_out': out['m_b_out'], 'm_ln1_g': out['m_ln1_g'], 'm_ln1_b': out['m_ln1_b'], 'm_w_ff1': out['m_w_ff1'], 'm_b_ff1': out['m_b_ff1'], 'm_w_ff2': out['m_w_ff2'], 'm_b_ff2': out['m_b_ff2'], 'm_ln2_g': out['m_ln2_g'], 'm_ln2_b': out['m_ln2_b'], 'v_w_in': out['v_w_in'], 'v_attn_gain': out['v_attn_gain'], 'v_ssm_gain': out['v_ssm_gain'], 'v_ssm_a_re': out['v_ssm_a_re'], 'v_ssm_a_im': out['v_ssm_a_im'], 'v_ssm_log_dt': out['v_ssm_log_dt'], 'v_ssm_b_re': out['v_ssm_b_re'], 'v_ssm_b_im': out['v_ssm_b_im'], 'v_ssm_c_re': out['v_ssm_c_re'], 'v_ssm_c_im': out['v_ssm_c_im'], 'v_ssm_d': out['v_ssm_d'], 'v_w_glu': out['v_w_glu'], 'v_b_glu': out['v_b_glu'], 'v_w_out': out['v_w_out'], 'v_b_out': out['v_b_out'], 'v_ln1_g': out['v_ln1_g'], 'v_ln1_b': out['v_ln1_b'], 'v_w_ff1': out['v_w_ff1'], 'v_b_ff1': out['v_b_ff1'], 'v_w_ff2': out['v_w_ff2'], 'v_b_ff2': out['v_b_ff2'], 'v_ln2_g': out['v_ln2_g'], 'v_ln2_b': out['v_ln2_b']}


def _loss(weights, diff, rest, loss_target):
    with _jax.named_scope("forward"):
        args = {**rest, TWIN_DIFF_INPUT: diff, **{k: w.astype(_WEIGHT_DTYPES[k]) for k, w in weights.items()}}
        y = _forward(args)
    with _jax.named_scope("loss_head"):
        err = _jnp.square(y.astype(_jnp.float32) - loss_target)
        return 0.5 * _jnp.sum(_jnp.mean(err, axis=-1)) if err.ndim else 0.5 * err


def _adamw(w, g, m, v):
    m = ADAM_B1 * m + (1.0 - ADAM_B1) * g
    v = ADAM_B2 * v + (1.0 - ADAM_B2) * _jnp.square(g)
    m_hat = m / (1.0 - ADAM_B1 ** ADAM_STEP)
    v_hat = v / (1.0 - ADAM_B2 ** ADAM_STEP)
    delta = -ADAM_LR * (m_hat / (_jnp.sqrt(v_hat) + ADAM_EPS) + ADAM_WD * w)
    return delta, m, v


def reference(x, positions, w_in, attn_gain, ssm_gain, ssm_a_re, ssm_a_im, ssm_log_dt, ssm_b_re, ssm_b_im, ssm_c_re, ssm_c_im, ssm_d, w_glu, b_glu, w_out, b_out, ln1_g, ln1_b, w_ff1, b_ff1, w_ff2, b_ff2, ln2_g, ln2_b, loss_target, m_w_in, m_attn_gain, m_ssm_gain, m_ssm_a_re, m_ssm_a_im, m_ssm_log_dt, m_ssm_b_re, m_ssm_b_im, m_ssm_c_re, m_ssm_c_im, m_ssm_d, m_w_glu, m_b_glu, m_w_out, m_b_out, m_ln1_g, m_ln1_b, m_w_ff1, m_b_ff1, m_w_ff2, m_b_ff2, m_ln2_g, m_ln2_b, v_w_in, v_attn_gain, v_ssm_gain, v_ssm_a_re, v_ssm_a_im, v_ssm_log_dt, v_ssm_b_re, v_ssm_b_im, v_ssm_c_re, v_ssm_c_im, v_ssm_d, v_w_glu, v_b_glu, v_w_out, v_b_out, v_ln1_g, v_ln1_b, v_w_ff1, v_b_ff1, v_w_ff2, v_b_ff2, v_ln2_g, v_ln2_b):
    given = dict(x=x, positions=positions, w_in=w_in, attn_gain=attn_gain, ssm_gain=ssm_gain, ssm_a_re=ssm_a_re, ssm_a_im=ssm_a_im, ssm_log_dt=ssm_log_dt, ssm_b_re=ssm_b_re, ssm_b_im=ssm_b_im, ssm_c_re=ssm_c_re, ssm_c_im=ssm_c_im, ssm_d=ssm_d, w_glu=w_glu, b_glu=b_glu, w_out=w_out, b_out=b_out, ln1_g=ln1_g, ln1_b=ln1_b, w_ff1=w_ff1, b_ff1=b_ff1, w_ff2=w_ff2, b_ff2=b_ff2, ln2_g=ln2_g, ln2_b=ln2_b, loss_target=loss_target, m_w_in=m_w_in, m_attn_gain=m_attn_gain, m_ssm_gain=m_ssm_gain, m_ssm_a_re=m_ssm_a_re, m_ssm_a_im=m_ssm_a_im, m_ssm_log_dt=m_ssm_log_dt, m_ssm_b_re=m_ssm_b_re, m_ssm_b_im=m_ssm_b_im, m_ssm_c_re=m_ssm_c_re, m_ssm_c_im=m_ssm_c_im, m_ssm_d=m_ssm_d, m_w_glu=m_w_glu, m_b_glu=m_b_glu, m_w_out=m_w_out, m_b_out=m_b_out, m_ln1_g=m_ln1_g, m_ln1_b=m_ln1_b, m_w_ff1=m_w_ff1, m_b_ff1=m_b_ff1, m_w_ff2=m_w_ff2, m_b_ff2=m_b_ff2, m_ln2_g=m_ln2_g, m_ln2_b=m_ln2_b, v_w_in=v_w_in, v_attn_gain=v_attn_gain, v_ssm_gain=v_ssm_gain, v_ssm_a_re=v_ssm_a_re, v_ssm_a_im=v_ssm_a_im, v_ssm_log_dt=v_ssm_log_dt, v_ssm_b_re=v_ssm_b_re, v_ssm_b_im=v_ssm_b_im, v_ssm_c_re=v_ssm_c_re, v_ssm_c_im=v_ssm_c_im, v_ssm_d=v_ssm_d, v_w_glu=v_w_glu, v_b_glu=v_b_glu, v_w_out=v_w_out, v_b_out=v_b_out, v_ln1_g=v_ln1_g, v_ln1_b=v_ln1_b, v_w_ff1=v_w_ff1, v_b_ff1=v_b_ff1, v_w_ff2=v_w_ff2, v_b_ff2=v_b_ff2, v_ln2_g=v_ln2_g, v_ln2_b=v_ln2_b)
    weights = {n: given[n] for n in TWIN_WEIGHTS}
    shared = {n: given[n] for n in SHARED_INPUTS}
    per_example = {n: given[n] for n in ['x', 'positions']}
    grad_fn = _jax.value_and_grad(_loss, argnums=(0, 1))

    def one_microbatch(ex, loss_target):
        ex = dict(ex)
        diff = ex.pop(TWIN_DIFF_INPUT)
        return grad_fn(weights, diff, {**shared, **ex}, loss_target)

    if N_MICROBATCH == 1:
        loss, (grad_w, grad_x) = one_microbatch(per_example, given["loss_target"])
    else:
        def body(carry, xs):
            loss_sum, grad_sum = carry
            l_k, (gw_k, gx_k) = one_microbatch(xs[0], xs[1])
            with _jax.named_scope("update"):
                return (loss_sum + l_k, _jax.tree.map(_jnp.add, grad_sum, gw_k)), gx_k

        init = (_jnp.zeros((), _jnp.float32), _jax.tree.map(_jnp.zeros_like, weights))
        (loss, grad_w), grad_x = _jax.lax.scan(body, init, (per_example, given["loss_target"]))
    with _jax.named_scope("update"):
        delta_w, new_m, new_v = {}, {}, {}
        for n in TWIN_WEIGHTS:
            delta_w[n], new_m[n], new_v[n] = _adamw(weights[n], grad_w[n], given["m_" + n], given["v_" + n])
    return (loss, grad_x, *[grad_w[n] for n in TWIN_WEIGHTS], *[delta_w[n] for n in TWIN_WEIGHTS],
            *[new_m[n] for n in TWIN_WEIGHTS], *[new_v[n] for n in TWIN_WEIGHTS])
```

```python
import functools
import math

import jax
import jax.numpy as jnp
import numpy as np
from jax import lax
from jax.experimental import pallas as pl
from jax.experimental.pallas import tpu as pltpu

F32 = jnp.float32
MXU_DTYPE = jnp.bfloat16

HEAD_DIM = 64
DILATION_PAIRS = ((128, 1), (512, 4), (2048, 16))
ROPE_THETA = 10000.0
SSM_GROUP_DIM = 16
SSM_STATE = 64
LN_EPS = 1e-5
RMS_EPS = 1e-6
NEG_INF = -1e30
ADAM_LR, ADAM_B1, ADAM_B2, ADAM_EPS, ADAM_WD, ADAM_STEP = 0.001, 0.9, 0.999, 1e-08, 0.01, 10

LANES = 128
SUBLANES = 8
QBLK = 128
VMEM_LIMIT = 56 * 2**20
MESH_AXES = ("x", "y", "c")
NDEV = 8

SMALL_NAMES = ("attn_gain", "ssm_gain", "ssm_a_re", "ssm_a_im", "ssm_log_dt", "ssm_b_re", "ssm_b_im", "ssm_c_re",
               "ssm_c_im", "ssm_d", "b_glu", "b_out", "ln1_g", "ln1_b", "b_ff1", "b_ff2", "ln2_g", "ln2_b")
BIG_NAMES = ("w_in", "w_glu", "w_out", "w_ff1", "w_ff2")
WEIGHT_ORDER = ("w_in", "attn_gain", "ssm_gain", "ssm_a_re", "ssm_a_im", "ssm_log_dt", "ssm_b_re", "ssm_b_im", "ssm_c_re",
                "ssm_c_im", "ssm_d", "w_glu", "b_glu", "w_out", "b_out", "ln1_g", "ln1_b", "w_ff1", "b_ff1", "w_ff2",
                "b_ff2", "ln2_g", "ln2_b")


def _cparams(sem=None):
    return pltpu.CompilerParams(dimension_semantics=sem, vmem_limit_bytes=VMEM_LIMIT)


def _mm(a, b):
    return jnp.dot(a.astype(MXU_DTYPE), b.astype(MXU_DTYPE), preferred_element_type=F32)


def _mm_nt(a, b):
    return lax.dot_general(a.astype(MXU_DTYPE), b.astype(MXU_DTYPE), (((1,), (1,)), ((), ())), preferred_element_type=F32)


def _mm_tn(a, b):
    return lax.dot_general(a.astype(MXU_DTYPE), b.astype(MXU_DTYPE), (((0,), (0,)), ((), ())), preferred_element_type=F32)


def _colsum(x):
    return jnp.sum(x, axis=0, keepdims=True)


def _layer_norm(x, g, b):
    mu = jnp.mean(x, axis=-1, keepdims=True)
    xc = x - mu
    var = jnp.mean(xc * xc, axis=-1, keepdims=True)
    return xc * lax.rsqrt(var + LN_EPS) * g + b


def _layer_norm_bwd(dy, pre, g):
    mu = jnp.mean(pre, axis=-1, keepdims=True)
    xc = pre - mu
    var = jnp.mean(xc * xc, axis=-1, keepdims=True)
    r = lax.rsqrt(var + LN_EPS)
    xhat = xc * r
    dyg = dy * g
    dpre = r * (dyg - jnp.mean(dyg, axis=-1, keepdims=True) - xhat * jnp.mean(dyg * xhat, axis=-1, keepdims=True))
    return dpre, _colsum(dy * xhat), _colsum(dy)


def _rms_norm(x, g):
    return x * lax.rsqrt(jnp.mean(x * x, axis=-1, keepdims=True) + RMS_EPS) * g


def _rms_norm_bwd(dy, x, g):
    r = lax.rsqrt(jnp.mean(x * x, axis=-1, keepdims=True) + RMS_EPS)
    dyg = dy * g
    dx = dyg * r - x * (r * r * r) * jnp.mean(dyg * x, axis=-1, keepdims=True)
    return dx, _colsum(dy * x * r)


_GELU_C = math.sqrt(2.0 / math.pi)


def _gelu(x):
    return 0.5 * x * (1.0 + jnp.tanh(_GELU_C * (x + 0.044715 * (x * x * x))))


def _gelu_grad(x):
    t = jnp.tanh(_GELU_C * (x + 0.044715 * (x * x * x)))
    return 0.5 * (1.0 + t) + 0.5 * x * (1.0 - t * t) * (_GELU_C * (1.0 + 3.0 * 0.044715 * x * x))


def _rows_call(name, fn, rows, consts, out_rows, out_accs, tm):
    rows = [r if isinstance(r, tuple) else (r, r.shape[1], 0) for r in rows]
    m = rows[0][0].shape[0]
    assert m % tm == 0
    nr, nc, no, na = len(rows), len(consts), len(out_rows), len(out_accs)

    def body(*refs):
        rr, cr = refs[:nr], refs[nr:nr + nc]
        orr, ar = refs[nr + nc:nr + nc + no], refs[nr + nc + no:]
        outs, accs = fn([r[...] for r in rr], cr)
        for o, v in zip(orr, outs, strict=True):
            o[...] = v.astype(o.dtype)
        if na:
            first = pl.program_id(0) == 0

            @pl.when(first)
            def _():
                for a, v in zip(ar, accs, strict=True):
                    a[...] = v

            @pl.when(jnp.logical_not(first))
            def _():
                for a, v in zip(ar, accs, strict=True):
                    a[...] += v

    def whole(shape):
        return pl.BlockSpec(shape, lambda i, n=len(shape): (0,) * n)

    in_specs = [pl.BlockSpec((tm, w), lambda i, cb=cb: (i, cb)) for _, w, cb in rows] + [whole(c.shape) for c in consts]
    out_specs = [pl.BlockSpec((tm, w), lambda i: (i, 0)) for w, _ in out_rows] + [whole(s) for s in out_accs]
    out_shape = [jax.ShapeDtypeStruct((m, w), dt) for w, dt in out_rows] + [jax.ShapeDtypeStruct(s, F32) for s in out_accs]
    res = pl.pallas_call(body, grid=(m // tm,), in_specs=in_specs, out_specs=out_specs, out_shape=out_shape, name=name,
                         compiler_params=_cparams(("arbitrary",)))(*[r[0] for r in rows], *consts)
    return res[:no], res[no:]


def _wgrad_call(name, x, dy, split, nblk, jb, blk_shape, tm, prologue=None):
    m = x.shape[0]
    kk, nn = blk_shape
    assert m % tm == 0 and nblk % jb == 0
    xw = kk * jb if split == "x" else x.shape[1]
    yw = nn * jb if split == "y" else dy.shape[1]
    nrow = m // tm

    def body(x_ref, dy_ref, o_ref, acc_ref):
        i = pl.program_id(1)

        @pl.when(i == 0)
        def _():
            acc_ref[...] = jnp.zeros_like(acc_ref)

        xv = x_ref[...]
        if prologue is not None:
            xv = prologue(xv)
        xv = xv.astype(MXU_DTYPE)
        dv = dy_ref[...].astype(MXU_DTYPE)
        for j in range(jb):
            xa = xv[:, j * kk:(j + 1) * kk] if split == "x" else xv
            da = dv[:, j * nn:(j + 1) * nn] if split == "y" else dv
            acc_ref[j] += _mm_tn(xa, da)

        @pl.when(i == nrow - 1)
        def _():
            o_ref[...] = acc_ref[...].astype(o_ref.dtype)

    in_specs = [pl.BlockSpec((tm, xw), (lambda j, i: (i, j)) if split == "x" else (lambda j, i: (i, 0))),
                pl.BlockSpec((tm, yw), (lambda j, i: (i, j)) if split == "y" else (lambda j, i: (i, 0)))]
    return pl.pallas_call(
        body, grid=(nblk // jb, nrow), in_specs=in_specs,
        out_specs=pl.BlockSpec((jb, kk, nn), lambda j, i: (j, 0, 0)),
        out_shape=jax.ShapeDtypeStruct((nblk, kk, nn), MXU_DTYPE),
        scratch_shapes=[pltpu.VMEM((jb, kk, nn), F32)], name=name,
        compiler_params=_cparams(("arbitrary", "arbitrary")))(x, dy)


def _exchange(name, arrays, modes):
    n = len(arrays)
    out_shape = [jax.ShapeDtypeStruct((NDEV,) + (a.shape if md == "gather" else a.shape[1:]), a.dtype)
                 for a, md in zip(arrays, modes, strict=True)]

    def body(*refs):
        ins, outs = refs[:n], refs[n:2 * n]
        send_sems, recv_sems, local_sems = refs[2 * n:]
        x, y, c = lax.axis_index("x"), lax.axis_index("y"), lax.axis_index("c")
        me = 4 * x + 2 * y + c
        peers = []
        for k in range(1, NDEV):
            fx, fy, fc = (k >> 2) & 1, (k >> 1) & 1, k & 1
            px, py, pc = (x + fx) % 2, (y + fy) % 2, (c + fc) % 2
            peers.append(((px, py, pc), 4 * px + 2 * py + pc))

        def piece(a, idx):
            return ins[a] if modes[a] == "gather" else ins[a].at[idx]

        def copy(a, k):
            dev, idx = peers[k]
            return pltpu.make_async_remote_copy(
                src_ref=piece(a, idx), dst_ref=outs[a].at[me], send_sem=send_sems.at[a, k], recv_sem=recv_sems.at[a, k],
                device_id=dev, device_id_type=pl.DeviceIdType.MESH)

        def arrival(a, k):
            dev, idx = peers[k]
            return pltpu.make_async_remote_copy(
                src_ref=piece(a, idx), dst_ref=outs[a].at[idx], send_sem=send_sems.at[a, k], recv_sem=recv_sems.at[a, k],
                device_id=dev, device_id_type=pl.DeviceIdType.MESH)

        mine = [pltpu.make_async_copy(piece(a, me), outs[a].at[me], local_sems.at[a]) for a in range(n)]
        for cp in mine:
            cp.start()
        sends = [copy(a, k) for k in range(NDEV - 1) for a in range(n)]
        for cp in sends:
            cp.start()
        for k in range(NDEV - 1):
            for a in range(n):
                arrival(a, k).wait_recv()
        for cp in sends:
            cp.wait_send()
        for cp in mine:
            cp.wait()

    any_spec = pl.BlockSpec(memory_space=pl.ANY)
    return pl.pallas_call(
        body, in_specs=[any_spec] * n, out_specs=[any_spec] * n, out_shape=out_shape, name=name,
        scratch_shapes=[pltpu.SemaphoreType.DMA((n, NDEV - 1)), pltpu.SemaphoreType.DMA((n, NDEV - 1)),
                        pltpu.SemaphoreType.DMA((n,))],
    )(*arrays)


def _rope_tables(positions3, inv_freq_row, sign_row):
    b, s, _ = positions3.shape

    def body(pos_ref, f_ref, sg_ref, c_ref, s_ref):
        ang = pos_ref[...].astype(F32) * f_ref[...]
        c_ref[...] = jnp.cos(ang)
        s_ref[...] = jnp.sin(ang) * sg_ref[...]

    row = pl.BlockSpec((1, LANES), lambda i: (0, 0))
    blk = pl.BlockSpec((None, s, LANES), lambda i: (i, 0, 0))
    return pl.pallas_call(
        body, grid=(b,), in_specs=[pl.BlockSpec((None, s, 1), lambda i: (i, 0, 0)), row, row], out_specs=[blk, blk],
        out_shape=[jax.ShapeDtypeStruct((b, s, LANES), F32)] * 2, name="rope_tables",
        compiler_params=_cparams(("arbitrary",)))(positions3, inv_freq_row, sign_row)


def _swap_halves(t):
    lane = lax.broadcasted_iota(jnp.int32, t.shape, 1)
    half = HEAD_DIM // 2
    return jnp.where((lane % HEAD_DIM) < half, pltpu.roll(t, LANES - half, 1), pltpu.roll(t, half, 1))


def _band_masks():
    qi = lax.broadcasted_iota(jnp.int32, (QBLK, 2 * QBLK), 0)
    kj = lax.broadcasted_iota(jnp.int32, (QBLK, 2 * QBLK), 1)
    band = (kj >= qi) & (kj <= qi + QBLK)
    q1 = lax.broadcasted_iota(jnp.int32, (QBLK, QBLK), 0)
    k1 = lax.broadcasted_iota(jnp.int32, (QBLK, QBLK), 1)
    return band, k1 <= q1


def _permute_in(dst, src, d, s):
    n = s // d
    for r in range(d):
        dst[pl.ds(r * n, n), :] = src[pl.ds(r, n, stride=d), :]


def _attn_fwd(proj3, ctab, stab, aw):
    b, s, _ = proj3.shape
    npair = aw // LANES
    scale = HEAD_DIM ** -0.5
    nbr = len(DILATION_PAIRS)

    def body(q_ref, k_ref, v_ref, c_ref, s_ref, o_ref, l_ref, qs, ks, vs, qp, kp, vp, op, lp, ob, lb):
        band, causal = _band_masks()
        cc, ss = c_ref[...], s_ref[...]
        q2 = q_ref[...]
        k2 = k_ref[...]
        q2 = (q2 * cc + _swap_halves(q2) * ss) * scale
        k2 = k2 * cc + _swap_halves(k2) * ss
        v2 = v_ref[...]
        for hh in range(LANES // HEAD_DIM):
            lo = hh * HEAD_DIM
            qs[...] = q2[:, lo:lo + HEAD_DIM]
            ks[...] = k2[:, lo:lo + HEAD_DIM]
            vs[...] = v2[:, lo:lo + HEAD_DIM]
            for br, (window, d) in enumerate(DILATION_PAIRS):
                if d == 1:
                    qr, kr, vr = qs, ks, vs
                else:
                    _permute_in(qp, qs, d, s)
                    _permute_in(kp, ks, d, s)
                    _permute_in(vp, vs, d, s)
                    qr, kr, vr = qp, kp, vp
                nb = (s // d) // QBLK
                for i in range(s // QBLK):
                    r0 = i * QBLK
                    first = (i % nb) == 0
                    k0, nk = (r0, QBLK) if first else (r0 - QBLK, 2 * QBLK)
                    qb = qr[pl.ds(r0, QBLK), :]
                    kb = kr[pl.ds(k0, nk), :]
                    vb = vr[pl.ds(k0, nk), :]
                    sc = jnp.where(causal if first else band, _mm_nt(qb, kb), NEG_INF)
                    mx = jnp.max(sc, axis=-1, keepdims=True)
                    p = jnp.exp(sc - mx)
                    den = jnp.sum(p, axis=-1, keepdims=True)
                    op[pl.ds(r0, QBLK), :] = _mm(p, vb) / den
                    lp[pl.ds(r0, QBLK), :] = mx + jnp.log(den)
                if d == 1:
                    ob[br] = op[...]
                    lb[br] = lp[...]
                else:
                    n = s // d
                    for r in range(d):
                        ob[br, pl.ds(r, n, stride=d), :] = op[pl.ds(r * n, n), :]
                        lb[br, pl.ds(r, n, stride=d), :] = lp[pl.ds(r * n, n), :]
            ls = [lb[br] for br in range(nbr)]
            mx = functools.reduce(jnp.maximum, ls)
            ws = [jnp.exp(l - mx) for l in ls]
            tot = functools.reduce(lambda a, b_: a + b_, ws)
            out = functools.reduce(lambda a, b_: a + b_, [(w / tot) * ob[br] for br, w in enumerate(ws)])
            o_ref[:, lo:lo + HEAD_DIM] = out
            l_ref[:, lo:lo + HEAD_DIM] = jnp.broadcast_to(mx + jnp.log(tot), (s, HEAD_DIM))

    def col(off):
        return pl.BlockSpec((None, s, LANES), lambda bi, hp, off=off: (bi, 0, off + hp))

    tab = pl.BlockSpec((None, s, LANES), lambda bi, hp: (bi, 0, 0))
    hd = pltpu.VMEM((s, HEAD_DIM), F32)
    one = pltpu.VMEM((s, 1), F32)
    return pl.pallas_call(
        body, grid=(b, npair), in_specs=[col(0), col(npair), col(2 * npair), tab, tab], out_specs=[col(0), col(0)],
        out_shape=[jax.ShapeDtypeStruct((b, s, aw), F32)] * 2,
        scratch_shapes=[hd] * 7 + [one, pltpu.VMEM((nbr, s, HEAD_DIM), F32), pltpu.VMEM((nbr, s, 1), F32)],
        name="attn_fwd", compiler_params=_cparams(("arbitrary", "arbitrary")))(proj3, proj3, proj3, ctab, stab)


def _attn_bwd(proj3, ctab, stab, dout3, out3, lse3, aw):
    b, s, _ = proj3.shape
    npair = aw // LANES
    scale = HEAD_DIM ** -0.5

    def body(q_ref, k_ref, v_ref, c_ref, s_ref, do_ref, o_ref, l_ref, dq_ref, dk_ref, dv_ref,
             qs, ks, vs, dos, qp, kp, vp, dop, dqp, dkp, dvp, dqn, dkn, dvn, ls, dl, lp, dlp, dq2, dk2):
        band, causal = _band_masks()
        cc, ss = c_ref[...], s_ref[...]
        q2 = q_ref[...]
        k2 = k_ref[...]
        q2 = (q2 * cc + _swap_halves(q2) * ss) * scale
        k2 = k2 * cc + _swap_halves(k2) * ss
        v2 = v_ref[...]
        do2 = do_ref[...]
        dd2 = do2 * o_ref[...]
        l2 = l_ref[...]
        for hh in range(LANES // HEAD_DIM):
            lo = hh * HEAD_DIM
            qs[...] = q2[:, lo:lo + HEAD_DIM]
            ks[...] = k2[:, lo:lo + HEAD_DIM]
            vs[...] = v2[:, lo:lo + HEAD_DIM]
            dos[...] = do2[:, lo:lo + HEAD_DIM]
            ls[...] = l2[:, lo:lo + 1]
            dl[...] = jnp.sum(dd2[:, lo:lo + HEAD_DIM], axis=-1, keepdims=True)
            dqn[...] = jnp.zeros_like(dqn)
            dkn[...] = jnp.zeros_like(dkn)
            dvn[...] = jnp.zeros_like(dvn)
            for window, d in DILATION_PAIRS:
                if d == 1:
                    qr, kr, vr, dor, lr, dlr = qs, ks, vs, dos, ls, dl
                else:
                    for dst, src in ((qp, qs), (kp, ks), (vp, vs), (dop, dos), (lp, ls), (dlp, dl)):
                        _permute_in(dst, src, d, s)
                    qr, kr, vr, dor, lr, dlr = qp, kp, vp, dop, lp, dlp
                dkp[...] = jnp.zeros_like(dkp)
                dvp[...] = jnp.zeros_like(dvp)
                nb = (s // d) // QBLK
                for i in range(s // QBLK):
                    r0 = i * QBLK
                    first = (i % nb) == 0
                    k0, nk = (r0, QBLK) if first else (r0 - QBLK, 2 * QBLK)
                    qb = qr[pl.ds(r0, QBLK), :]
                    dob = dor[pl.ds(r0, QBLK), :]
                    kb = kr[pl.ds(k0, nk), :]
                    vb = vr[pl.ds(k0, nk), :]
                    sc = jnp.where(causal if first else band, _mm_nt(qb, kb), NEG_INF)
                    p = jnp.exp(sc - lr[pl.ds(r0, QBLK), :])
                    ds_ = p * (_mm_nt(dob, vb) - dlr[pl.ds(r0, QBLK), :])
                    dqp[pl.ds(r0, QBLK), :] = _mm(ds_, kb)
                    dkp[pl.ds(k0, nk), :] += _mm_tn(ds_, qb)
                    dvp[pl.ds(k0, nk), :] += _mm_tn(p, dob)
                n = s // d
                for r in range(d):
                    if d == 1:
                        src, dst = pl.ds(0, s), pl.ds(0, s)
                    else:
                        src, dst = pl.ds(r * n, n), pl.ds(r, n, stride=d)
                    dqn[dst, :] += dqp[src, :]
                    dkn[dst, :] += dkp[src, :]
                    dvn[dst, :] += dvp[src, :]
            dq2[:, lo:lo + HEAD_DIM] = dqn[...] * scale
            dk2[:, lo:lo + HEAD_DIM] = dkn[...]
            dv_ref[:, lo:lo + HEAD_DIM] = dvn[...]
        g = dq2[...]
        dq_ref[...] = g * cc + _swap_halves(g * ss)
        g = dk2[...]
        dk_ref[...] = g * cc + _swap_halves(g * ss)

    def col(off):
        return pl.BlockSpec((None, s, LANES), lambda bi, hp, off=off: (bi, 0, off + hp))

    tab = pl.BlockSpec((None, s, LANES), lambda bi, hp: (bi, 0, 0))
    hd = pltpu.VMEM((s, HEAD_DIM), F32)
    one = pltpu.VMEM((s, 1), F32)
    pair = pltpu.VMEM((s, LANES), F32)
    return pl.pallas_call(
        body, grid=(b, npair), in_specs=[col(0), col(npair), col(2 * npair), tab, tab, col(0), col(0), col(0)],
        out_specs=[col(0)] * 3, out_shape=[jax.ShapeDtypeStruct((b, s, aw), F32)] * 3,
        scratch_shapes=[hd] * 14 + [one] * 4 + [pair] * 2,
        name="attn_bwd", compiler_params=_cparams(("arbitrary", "arbitrary")))(
            proj3, proj3, proj3, ctab, stab, dout3, out3, lse3)


def _s5_discretise(a_re, a_im, log_dt, bt_re, bt_im):
    dt = jnp.exp(log_dt)
    mag = jnp.exp(a_re * dt)
    ang = a_im * dt
    lb_re = mag * jnp.cos(ang)
    lb_im = mag * jnp.sin(ang)
    den = a_re * a_re + a_im * a_im
    nr = lb_re - 1.0
    ni = lb_im
    cr = (nr * a_re + ni * a_im) / den
    ci = (ni * a_re - nr * a_im) / den
    return lb_re, lb_im, cr * bt_re - ci * bt_im, cr * bt_im + ci * bt_re


def _s5_params(a_re, a_im, log_dt, bt_re, bt_im):
    def body(ar, ai, ld, br, bi, o1, o2, o3, o4):
        r = _s5_discretise(ar[...], ai[...], ld[...], br[...], bi[...])
        for o, v in zip((o1, o2, o3, o4), r, strict=True):
            o[...] = v

    sd = jax.ShapeDtypeStruct
    return pl.pallas_call(body, out_shape=[sd(a_re.shape, F32)] * 2 + [sd(bt_re.shape, F32)] * 2, name="s5_params")(
        a_re, a_im, log_dt, bt_re, bt_im)


def _s5_params_bwd(a_re, a_im, log_dt, bt_re, bt_im, cts):
    def body(ar, ai, ld, br, bi, c1, c2, c3, c4, o1, o2, o3, o4, o5):
        _, vjp = jax.vjp(_s5_discretise, ar[...], ai[...], ld[...], br[...], bi[...])
        r = vjp((c1[...], c2[...], c3[...], c4[...]))
        for o, v in zip((o1, o2, o3, o4, o5), r, strict=True):
            o[...] = v

    sd = jax.ShapeDtypeStruct
    return pl.pallas_call(
        body, out_shape=[sd(a_re.shape, F32)] * 2 + [sd(log_dt.shape, F32)] + [sd(bt_re.shape, F32)] * 2, name="s5_params_bwd")(
            a_re, a_im, log_dt, bt_re, bt_im, *cts)


S5_TC = 128


def _s5_fwd(proj3, ucol, lam_re, lam_im, bre, bim, cre, cim, dvec, sw):
    b, s, _ = proj3.shape
    nt = lam_re.shape[0]
    ncb = sw // LANES
    tpc = nt // ncb
    tc = S5_TC

    def body(u_ref, lr_ref, li_ref, bre_ref, bim_ref, cre_ref, cim_ref, d_ref, y_ref, xr_ref, xi_ref, sr, si):
        @pl.when(pl.program_id(0) == 0)
        def _():
            sr[...] = jnp.zeros_like(sr)
            si[...] = jnp.zeros_like(si)

        for bi in range(b):
            for cb in range(ncb):
                ucb = u_ref[bi, :, cb * LANES:(cb + 1) * LANES]
                for t in range(cb * tpc, (cb + 1) * tpc):
                    xr_ref[bi, pl.ds(t, tc, stride=nt), :] = _mm(ucb, bre_ref[t])
                    xi_ref[bi, pl.ds(t, tc, stride=nt), :] = _mm(ucb, bim_ref[t])
        lr, li = lr_ref[...], li_ref[...]

        def step(j, carry):
            off = pl.multiple_of(j * nt, nt)
            new = []
            for bi in range(b):
                pr, pi = carry[2 * bi], carry[2 * bi + 1]
                nr = lr * pr - li * pi + xr_ref[bi, pl.ds(off, nt), :]
                ni = lr * pi + li * pr + xi_ref[bi, pl.ds(off, nt), :]
                xr_ref[bi, pl.ds(off, nt), :] = nr
                xi_ref[bi, pl.ds(off, nt), :] = ni
                new += [nr, ni]
            return tuple(new)

        init = tuple(v for bi in range(b) for v in (sr[bi], si[bi]))
        fin = lax.fori_loop(0, tc, step, init, unroll=4)
        for bi in range(b):
            sr[bi] = fin[2 * bi]
            si[bi] = fin[2 * bi + 1]
        for bi in range(b):
            for cb in range(ncb):
                cols = slice(cb * LANES, (cb + 1) * LANES)
                acc = None
                for t in range(cb * tpc, (cb + 1) * tpc):
                    term = _mm(xr_ref[bi, pl.ds(t, tc, stride=nt), :], cre_ref[t]) - _mm(xi_ref[bi, pl.ds(t, tc, stride=nt), :], cim_ref[t])
                    acc = term if acc is None else acc + term
                y_ref[bi, :, cols] = acc + d_ref[:, cols] * u_ref[bi, :, cols]

    def whole(a):
        return pl.BlockSpec(a.shape, lambda c, n=a.ndim: (0,) * n)

    xblk = pl.BlockSpec((b, tc * nt, LANES), lambda c: (0, c, 0))
    return pl.pallas_call(
        body, grid=(s // tc,),
        in_specs=[pl.BlockSpec((b, tc, sw), lambda c: (0, c, ucol))] + [whole(a) for a in (lam_re, lam_im, bre, bim, cre, cim, dvec)],
        out_specs=[pl.BlockSpec((b, tc, sw), lambda c: (0, c, 0)), xblk, xblk],
        out_shape=[jax.ShapeDtypeStruct((b, s, sw), F32)] + [jax.ShapeDtypeStruct((b, s * nt, LANES), F32)] * 2,
        scratch_shapes=[pltpu.VMEM((b, nt, LANES), F32)] * 2, name="s5_fwd",
        compiler_params=_cparams(("arbitrary",)))(proj3, lam_re, lam_im, bre, bim, cre, cim, dvec)


def _s5_bwd(proj3, ucol, dy3, xr3, xi3, lam_re, lam_im, bre, bim, cre, cim, dvec, sw):
    b, s, _ = proj3.shape
    nt = lam_re.shape[0]
    ncb = sw // LANES
    tpc = nt // ncb
    tc = S5_TC
    nchunk = s // tc

    def body(u_ref, dy_ref, xr_ref, xi_ref, pr_ref, pi_ref, lr_ref, li_ref, bre_ref, bim_ref, cre_ref, cim_ref, d_ref,
             du_ref, dbre, dbim, dcre, dcim, dlr, dli, dd, gr, gi, sr, si):
        step_id = pl.program_id(0)

        @pl.when(step_id == 0)
        def _():
            for r in (sr, si, dbre, dbim, dcre, dcim, dlr, dli, dd):
                r[...] = jnp.zeros_like(r)

        for bi in range(b):
            for cb in range(ncb):
                dycb = dy_ref[bi, :, cb * LANES:(cb + 1) * LANES]
                for t in range(cb * tpc, (cb + 1) * tpc):
                    gr[bi, pl.ds(t, tc, stride=nt), :] = _mm_nt(dycb, cre_ref[t])
                    gi[bi, pl.ds(t, tc, stride=nt), :] = -_mm_nt(dycb, cim_ref[t])
        lr, li = lr_ref[...], li_ref[...]

        def step(jj, carry):
            off = pl.multiple_of((tc - 1 - jj) * nt, nt)
            new = []
            for bi in range(b):
                nr_, ni_ = carry[2 * bi], carry[2 * bi + 1]
                vr = gr[bi, pl.ds(off, nt), :] + lr * nr_ + li * ni_
                vi = gi[bi, pl.ds(off, nt), :] + lr * ni_ - li * nr_
                gr[bi, pl.ds(off, nt), :] = vr
                gi[bi, pl.ds(off, nt), :] = vi
                new += [vr, vi]
            return tuple(new)

        init = tuple(v for bi in range(b) for v in (sr[bi], si[bi]))
        fin = lax.fori_loop(0, tc, step, init, unroll=4)
        for bi in range(b):
            sr[bi] = fin[2 * bi]
            si[bi] = fin[2 * bi + 1]

        has_prev = (step_id != nchunk - 1).astype(F32)
        rest = (tc - 1) * nt
        alr = jnp.zeros((nt, LANES), F32)
        ali = jnp.zeros((nt, LANES), F32)
        for bi in range(b):
            s0r, s0i = gr[bi, pl.ds(0, nt), :], gi[bi, pl.ds(0, nt), :]
            x0r, x0i = pr_ref[bi] * has_prev, pi_ref[bi] * has_prev
            alr += s0r * x0r + s0i * x0i
            ali += s0i * x0r - s0r * x0i
            s1r, s1i = gr[bi, pl.ds(nt, rest), :], gi[bi, pl.ds(nt, rest), :]
            x1r, x1i = xr_ref[bi, pl.ds(0, rest), :], xi_ref[bi, pl.ds(0, rest), :]
            alr += jnp.sum((s1r * x1r + s1i * x1i).reshape(tc - 1, nt, LANES), axis=0)
            ali += jnp.sum((s1i * x1r - s1r * x1i).reshape(tc - 1, nt, LANES), axis=0)
        dlr[...] += alr
        dli[...] += ali

        for bi in range(b):
            for cb in range(ncb):
                cols = slice(cb * LANES, (cb + 1) * LANES)
                ucb = u_ref[bi, :, cols]
                dycb = dy_ref[bi, :, cols]
                acc = d_ref[:, cols] * dycb
                for t in range(cb * tpc, (cb + 1) * tpc):
                    rows = pl.ds(t, tc, stride=nt)
                    vr, vi = gr[bi, rows, :], gi[bi, rows, :]
                    acc += _mm_nt(vr, bre_ref[t]) + _mm_nt(vi, bim_ref[t])
                    dbre[t] += _mm_tn(ucb, vr)
                    dbim[t] += _mm_tn(ucb, vi)
                    dcre[t] += _mm_tn(xr_ref[bi, rows, :], dycb)
                    dcim[t] -= _mm_tn(xi_ref[bi, rows, :], dycb)
                du_ref[bi, :, cols] = acc
                dd[:, cols] += _colsum(dycb * ucb)

    def whole(a):
        return pl.BlockSpec(a.shape, lambda c, n=len(a.shape): (0,) * n)

    def rev(c):
        return nchunk - 1 - c

    xblk = pl.BlockSpec((b, tc * nt, LANES), lambda c: (0, rev(c), 0))
    prev = pl.BlockSpec((b, nt, LANES), lambda c: (0, jnp.maximum(rev(c) * tc - 1, 0), 0))
    sd = jax.ShapeDtypeStruct
    blk = sd(bre.shape, F32)
    acc_shapes = [blk, blk, sd(cre.shape, F32), sd(cre.shape, F32), sd(lam_re.shape, F32), sd(lam_re.shape, F32), sd((1, sw), F32)]
    return pl.pallas_call(
        body, grid=(nchunk,),
        in_specs=[pl.BlockSpec((b, tc, sw), lambda c: (0, rev(c), ucol)), pl.BlockSpec((b, tc, sw), lambda c: (0, rev(c), 0)),
                  xblk, xblk, prev, prev] + [whole(a) for a in (lam_re, lam_im, bre, bim, cre, cim, dvec)],
        out_specs=[pl.BlockSpec((b, tc, sw), lambda c: (0, rev(c), 0))] + [whole(a) for a in acc_shapes],
        out_shape=[sd((b, s, sw), F32)] + acc_shapes,
        scratch_shapes=[pltpu.VMEM((b, tc * nt, LANES), F32)] * 2 + [pltpu.VMEM((b, nt, LANES), F32)] * 2, name="s5_bwd",
        compiler_params=_cparams(("arbitrary",)))(proj3, dy3, xr3, xi3, xr3, xi3, lam_re, lam_im, bre, bim, cre, cim, dvec)


def _s5_blocks(bb, cc_, ncb):
    g, n, p = bb.shape
    eye = jnp.eye(g, dtype=bb.dtype)
    bd = jnp.einsum("gnp,gh->gnhp", bb, eye).reshape(g * n, g * p)
    cd = jnp.einsum("gnp,gh->hpgn", cc_, eye).reshape(g * p, g * n)
    nt = g * p // LANES
    tpc = nt // ncb
    bblk = jnp.stack([bd[(t // tpc) * LANES:(t // tpc + 1) * LANES, t * LANES:(t + 1) * LANES] for t in range(nt)])
    cblk = jnp.stack([cd[t * LANES:(t + 1) * LANES, (t // tpc) * LANES:(t // tpc + 1) * LANES] for t in range(nt)])
    return bblk, cblk


def _s5_unblock(dbblk, dcblk, g, n, p, ncb):
    nt = g * p // LANES
    tpc = nt // ncb
    gpt = LANES // p
    db, dc = [], []
    for gi in range(g):
        t, gl = gi // gpt, gi % gpt
        r0 = (gi * n) % LANES
        db.append(dbblk[t, r0:r0 + n, gl * p:(gl + 1) * p])
        dc.append(dcblk[t, gl * p:(gl + 1) * p, r0:r0 + n].T)
    return jnp.stack(db), jnp.stack(dc)


def _adamw_math(w, g, m, v):
    m = ADAM_B1 * m + (1.0 - ADAM_B1) * g
    v = ADAM_B2 * v + (1.0 - ADAM_B2) * (g * g)
    m_hat = m / (1.0 - ADAM_B1 ** ADAM_STEP)
    v_hat = v / (1.0 - ADAM_B2 ** ADAM_STEP)
    delta = -ADAM_LR * (m_hat / (jnp.sqrt(v_hat) + ADAM_EPS) + ADAM_WD * w)
    return delta, m, v


def _adamw_call(name, parts, w, m, v, tr):
    nparts, nl, r, c = parts.shape
    assert r % tr == 0

    def body(p_ref, w_ref, m_ref, v_ref, g_out, d_out, m_out, v_out):
        g = p_ref[0].astype(F32)
        for k in range(1, nparts):
            g = g + p_ref[k].astype(F32)
        delta, mn, vn = _adamw_math(w_ref[...], g, m_ref[...], v_ref[...])
        g_out[...] = g
        d_out[...] = delta
        m_out[...] = mn
        v_out[...] = vn

    blk = pl.BlockSpec((None, tr, c), lambda l, i: (l, i, 0))
    return pl.pallas_call(
        body, grid=(nl, r // tr), in_specs=[pl.BlockSpec((nparts, None, tr, c), lambda l, i: (0, l, i, 0)), blk, blk, blk],
        out_specs=[blk] * 4, out_shape=[jax.ShapeDtypeStruct((nl, r, c), F32)] * 4, name=name,
        compiler_params=_cparams(("arbitrary", "arbitrary")))(parts, w, m, v)


def _pack_small(pieces):
    flat = jnp.concatenate([p.reshape(-1) for p in pieces])
    n = flat.shape[0]
    unit = SUBLANES * LANES
    padded = -(-n // unit) * unit
    return jnp.pad(flat, (0, padded - n)).reshape(padded // LANES, LANES)


def _unpack_small(packed, shapes):
    flat = packed.reshape(-1)
    out, off = [], 0
    for shp in shapes:
        sz = int(np.prod(shp))
        out.append(flat[off:off + sz].reshape(shp))
        off += sz
    return out


def _step(x, positions, weights, moments_m, moments_v, loss_target, distributed):
    f32 = F32
    bsz, seq, dm = x.shape
    depth = weights["w_in"].shape[0]
    aw = weights["attn_gain"].shape[1]
    sw = weights["ssm_gain"].shape[1]
    dff = weights["b_ff1"].shape[1]
    ng, npst = weights["ssm_a_re"].shape[1:]
    gdim = weights["ssm_d"].shape[2]
    in_w = 3 * aw + sw
    t_rows = bsz * seq
    alpha = (2.0 * depth) ** 0.25
    ncb = sw // LANES
    nt = ng * npst // LANES
    ndev = NDEV if distributed else 1
    tm = 256

    shards = [weights[n].astype(MXU_DTYPE) for n in BIG_NAMES]
    if distributed:
        gathered = _exchange("weights_allgather", shards, ["gather"] * len(shards))
    else:
        gathered = [s_[None] for s_ in shards]
    g_in, g_glu, g_out, g_ff1, g_ff2 = gathered

    half = HEAD_DIM // 2
    inv_freq = ROPE_THETA ** (-jnp.arange(half, dtype=f32) * 2.0 / HEAD_DIM)
    reps = LANES // half
    inv_row = jnp.tile(inv_freq, reps)[None, :]
    sign_row = jnp.tile(jnp.concatenate([-jnp.ones((half,), f32), jnp.ones((half,), f32)]), LANES // HEAD_DIM)[None, :]
    ctab, stab = _rope_tables(positions[..., None], inv_row, sign_row)

    def row(v):
        return v.reshape(1, -1)

    h = x.reshape(t_rows, dm)
    saved = []
    for l in range(depth):
        w_in_l = g_in[:, l]
        w_glu_l = g_glu[:, l].reshape(sw, sw)
        w_out_l = g_out[:, l].reshape(dm, dm)
        w_ff1_l = g_ff1[:, l]
        w_ff2_l = g_ff2[:, l].reshape(dff, dm)

        def in_proj(rv, cr):
            return [jnp.concatenate([_mm(rv[0], cr[0][j]) for j in range(ndev)], axis=-1)], []

        (proj,), _ = _rows_call("in_proj", in_proj, [h], [w_in_l], [(in_w, f32)], [], tm)
        proj3 = proj.reshape(bsz, seq, in_w)
        attn3, lse3 = _attn_fwd(proj3, ctab, stab, aw)

        a_re, a_im = weights["ssm_a_re"][l][:, None, :], weights["ssm_a_im"][l][:, None, :]
        log_dt = weights["ssm_log_dt"][l][:, None, None]
        bt_re = weights["ssm_b_re"][l].transpose(0, 2, 1)
        bt_im = weights["ssm_b_im"][l].transpose(0, 2, 1)
        lb_re, lb_im, bb_re, bb_im = _s5_params(a_re, a_im, log_dt, bt_re, bt_im)
        lam_re, lam_im = lb_re.reshape(nt, LANES), lb_im.reshape(nt, LANES)
        bre, cre = _s5_blocks(bb_re, weights["ssm_c_re"][l], ncb)
        bim, cim = _s5_blocks(bb_im, weights["ssm_c_im"][l], ncb)
        s5c = [a_.astype(MXU_DTYPE) for a_ in (bre, bim, cre, cim)]
        dvec = row(weights["ssm_d"][l])
        ypre3, xr3, xi3 = _s5_fwd(proj3, 3 * aw // sw, lam_re, lam_im, *s5c, dvec, sw)
        attn, ypre = attn3.reshape(t_rows, aw), ypre3.reshape(t_rows, sw)

        b_glu, ga, gs = row(weights["b_glu"][l]), row(weights["attn_gain"][l]), row(weights["ssm_gain"][l])

        def mix(rv, cr):
            at, yp = rv
            g = _gelu(yp)
            ssm = g * jax.nn.sigmoid(_mm(g, cr[0][...]) + cr[1][...])
            return [jnp.concatenate([_rms_norm(at, cr[2][...]), _rms_norm(ssm, cr[3][...])], axis=-1)], []

        (mixed,), _ = _rows_call("mix", mix, [attn, ypre], [w_glu_l, b_glu, ga, gs], [(dm, MXU_DTYPE)], [], tm)

        b_out, g1, b1 = row(weights["b_out"][l]), row(weights["ln1_g"][l]), row(weights["ln1_b"][l])

        def out_proj(rv, cr):
            pre = alpha * rv[1] + _mm(rv[0], cr[0][...]) + cr[1][...]
            return [pre, _layer_norm(pre, cr[2][...], cr[3][...])], []

        (pre1, h1), _ = _rows_call("out_proj", out_proj, [mixed, h], [w_out_l, b_out, g1, b1], [(dm, f32), (dm, f32)], [], tm)

        b_ff1 = row(weights["b_ff1"][l])

        def ff1(rv, cr):
            return [jnp.concatenate([_mm(rv[0], cr[0][j]) for j in range(ndev)], axis=-1) + cr[1][...]], []

        (act,), _ = _rows_call("ff1", ff1, [h1], [w_ff1_l, b_ff1], [(dff, f32)], [], tm)

        b_ff2, g2, b2 = row(weights["b_ff2"][l]), row(weights["ln2_g"][l]), row(weights["ln2_b"][l])

        def ff2(rv, cr):
            r = jnp.square(jnp.maximum(rv[0], 0.0))
            pre = alpha * rv[1] + _mm(r, cr[0][...]) + cr[1][...]
            return [pre, _layer_norm(pre, cr[2][...], cr[3][...])], []

        (pre2, h2), _ = _rows_call("ff2", ff2, [act, h1], [w_ff2_l, b_ff2, g2, b2], [(dm, f32), (dm, f32)], [], tm)

        saved.append(dict(h=h, proj3=proj3, attn=attn, lse3=lse3, attn3=attn3, ypre=ypre, xr3=xr3, xi3=xi3, mixed=mixed, pre1=pre1,
                          h1=h1, act=act, pre2=pre2, w_in=w_in_l, w_glu=w_glu_l, w_out=w_out_l, w_ff1=w_ff1_l, w_ff2=w_ff2_l,
                          lam=(lam_re, lam_im), s5c=s5c, dvec=dvec, b_glu=b_glu, ga=ga, gs=gs, g1=g1, g2=g2,
                          s5in=(a_re, a_im, log_dt, bt_re, bt_im)))
        h = h2

    g2_last = saved[-1]["g2"]

    def loss_fn(rv, cr):
        y, tgt, pre = rv
        err = y - tgt
        part = 0.5 * jnp.sum(jnp.mean(err * err, axis=-1, keepdims=True), axis=0, keepdims=True)
        dpre, dg, db = _layer_norm_bwd(err * (1.0 / dm), pre, cr[0][...])
        return [dpre], [jnp.broadcast_to(part, (1, LANES)), dg, db, _colsum(dpre)]

    (dpre2,), (loss_acc, dg2, db2, dbff2) = _rows_call(
        "loss", loss_fn, [h, loss_target.reshape(t_rows, dm), saved[-1]["pre2"]], [g2_last], [(dm, f32)],
        [(1, LANES), (1, dm), (1, dm), (1, dm)], tm)
    loss = loss_acc[0, 0]
    if distributed:
        loss = lax.psum(loss, MESH_AXES)

    big_parts = {n: [None] * depth for n in BIG_NAMES}
    small_parts = [None] * depth
    grad_x = None
    for l in reversed(range(depth)):
        sv = saved[l]

        def ff2_bwd(rv, cr):
            da = _mm_nt(rv[0], cr[0][...]) * (2.0 * jnp.maximum(rv[1], 0.0))
            return [da], [_colsum(da)]

        (dact,), (dbff1,) = _rows_call("ff2_bwd", ff2_bwd, [dpre2, sv["act"]], [sv["w_ff2"]], [(dff, MXU_DTYPE)], [(1, dff)], tm)
        big_parts["w_ff2"][l] = _wgrad_call("wgrad_ff2", sv["act"], dpre2, "x", ndev, ndev // 2 if ndev > 1 else 1,
                                            (dff // ndev, dm), tm, prologue=lambda a: jnp.square(jnp.maximum(a, 0.0)))
        big_parts["w_ff1"][l] = _wgrad_call("wgrad_ff1", sv["h1"], dact, "y", ndev, ndev // 2 if ndev > 1 else 1,
                                            (dm, dff // ndev), tm)

        def ff1_bwd(rv, cr):
            dacc = alpha * rv[1]
            wpb = dff // ndev
            for j in range(ndev):
                dacc = dacc + _mm_nt(rv[0][:, j * wpb:(j + 1) * wpb], cr[0][j])
            dpre, dg, db = _layer_norm_bwd(dacc, rv[2], cr[1][...])
            return [dpre], [dg, db, _colsum(dpre)]

        (dpre1,), (dg1, db1, dbout) = _rows_call("ff1_bwd", ff1_bwd, [dact, dpre2, sv["pre1"]], [sv["w_ff1"], sv["g1"]],
                                                 [(dm, f32)], [(1, dm)] * 3, tm)
        big_parts["w_out"][l] = _wgrad_call("wgrad_out", sv["mixed"], dpre1, "x", ndev, ndev, (dm // ndev, dm), tm)

        def mix_bwd(rv, cr):
            dp, at, yp = rv
            w_out_r, w_glu_r, bg, ga_, gs_ = cr
            dmixed = _mm_nt(dp, w_out_r[...])
            g = _gelu(yp)
            sig = jax.nn.sigmoid(_mm(g, w_glu_r[...]) + bg[...])
            ssm = g * sig
            dat, dga = _rms_norm_bwd(dmixed[:, :aw], at, ga_[...])
            dssm, dgs = _rms_norm_bwd(dmixed[:, aw:], ssm, gs_[...])
            dz = dssm * g * sig * (1.0 - sig)
            dg = dssm * sig + _mm_nt(dz, w_glu_r[...])
            return [dat, dg * _gelu_grad(yp), dz], [dga, dgs, _colsum(dz)]

        (dattn, dypre, dz), (dga, dgs, dbglu) = _rows_call(
            "mix_bwd", mix_bwd, [dpre1, sv["attn"], sv["ypre"]], [sv["w_out"], sv["w_glu"], sv["b_glu"], sv["ga"], sv["gs"]],
            [(aw, f32), (sw, f32), (sw, MXU_DTYPE)], [(1, aw), (1, sw), (1, sw)], tm)
        big_parts["w_glu"][l] = _wgrad_call("wgrad_glu", sv["ypre"], dz, "x", 1, 1, (sw, sw), tm, prologue=_gelu).reshape(
            ndev, sw // ndev, sw)

        du3, dbre, dbim, dcre, dcim, dlr, dli, dd = _s5_bwd(
            sv["proj3"], 3 * aw // sw, dypre.reshape(bsz, seq, sw), sv["xr3"], sv["xi3"], *sv["lam"], *sv["s5c"], sv["dvec"], sw)
        dbb_re, dc_re = _s5_unblock(dbre, dcre, ng, gdim, npst, ncb)
        dbb_im, dc_im = _s5_unblock(dbim, dcim, ng, gdim, npst, ncb)
        da_re, da_im, dldt, dbt_re, dbt_im = _s5_params_bwd(
            *sv["s5in"], (dlr.reshape(ng, 1, npst), dli.reshape(ng, 1, npst), dbb_re, dbb_im))

        dq3, dk3, dv3 = _attn_bwd(sv["proj3"], ctab, stab, dattn.reshape(bsz, seq, aw), sv["attn3"], sv["lse3"], aw)
        dproj = jnp.concatenate([dq3, dk3, dv3, du3], axis=-1).reshape(t_rows, in_w)
        big_parts["w_in"][l] = _wgrad_call("wgrad_in", sv["h"], dproj, "y", ndev, ndev, (dm, in_w // ndev), tm)

        small_parts[l] = dict(
            attn_gain=dga, ssm_gain=dgs, ssm_a_re=da_re[:, 0], ssm_a_im=da_im[:, 0], ssm_log_dt=dldt[:, 0, 0], ssm_b_re=dbt_re.transpose(0, 2, 1),
            ssm_b_im=dbt_im.transpose(0, 2, 1), ssm_c_re=dc_re, ssm_c_im=dc_im, ssm_d=dd, b_glu=dbglu, b_out=dbout, ln1_g=dg1,
            ln1_b=db1, b_ff1=dbff1, b_ff2=dbff2, ln2_g=dg2, ln2_b=db2)

        wpb = in_w // ndev
        if l > 0:
            prev = saved[l - 1]

            def in_bwd(rv, cr):
                dacc = alpha * rv[1]
                for j in range(ndev):
                    dacc = dacc + _mm_nt(rv[0][:, j * wpb:(j + 1) * wpb], cr[0][j])
                dpre, dg, db = _layer_norm_bwd(dacc, rv[2], cr[1][...])
                return [dpre], [dg, db, _colsum(dpre)]

            (dpre2,), (dg2, db2, dbff2) = _rows_call("in_bwd", in_bwd, [dproj, dpre1, prev["pre2"]], [sv["w_in"], prev["g2"]],
                                                     [(dm, f32)], [(1, dm)] * 3, tm)
        else:
            def in_bwd0(rv, cr):
                dacc = alpha * rv[1]
                for j in range(ndev):
                    dacc = dacc + _mm_nt(rv[0][:, j * wpb:(j + 1) * wpb], cr[0][j])
                return [dacc], []

            (grad_x,), _ = _rows_call("in_bwd0", in_bwd0, [dproj, dpre1], [sv["w_in"]], [(dm, f32)], [], tm)

    small_shapes = [weights[n].shape[1:] for n in SMALL_NAMES]
    small_grad = jnp.stack([_pack_small([small_parts[l][n] for n in SMALL_NAMES]) for l in range(depth)])
    big_stacked = [jnp.stack(big_parts[n], axis=1) for n in BIG_NAMES]
    if distributed:
        recv = _exchange("grads_exchange", big_stacked + [small_grad], ["scatter"] * len(BIG_NAMES) + ["gather"])
    else:
        recv = big_stacked + [small_grad[None]]

    outs = {}
    for n, parts in zip(BIG_NAMES, recv[:-1], strict=True):
        w = weights[n]
        r, c = w.shape[1:]
        outs[n] = _adamw_call("adamw_" + n, parts, w, moments_m[n], moments_v[n], min(r, 256))
    nsmall = small_grad.shape[1]
    packed = [jnp.stack([_pack_small([d[n][l] for n in SMALL_NAMES]) for l in range(depth)]).reshape(1, depth * nsmall, LANES)
              for d in (weights, moments_m, moments_v)]
    sparts = recv[-1].reshape(ndev, 1, depth * nsmall, LANES)
    sg, sd_, sm, sv_ = _adamw_call("adamw_small", sparts, *packed, nsmall)
    for k, arr in enumerate((sg, sd_, sm, sv_)):
        per_layer = arr.reshape(depth, nsmall, LANES)
        pieces = [_unpack_small(per_layer[l], small_shapes) for l in range(depth)]
        for i, n in enumerate(SMALL_NAMES):
            outs.setdefault(n, [None] * 4)
            outs[n][k] = jnp.stack([pieces[l][i] for l in range(depth)])

    result = [loss, grad_x.reshape(bsz, seq, dm)]
    for k in range(4):
        result += [outs[n][k] for n in WEIGHT_ORDER]
    return tuple(result)


def kernel(x, positions, w_in, attn_gain, ssm_gain, ssm_a_re, ssm_a_im, ssm_log_dt, ssm_b_re, ssm_b_im, ssm_c_re, ssm_c_im, ssm_d, w_glu, b_glu, w_out, b_out, ln1_g, ln1_b, w_ff1, b_ff1, w_ff2, b_ff2, ln2_g, ln2_b, loss_target, m_w_in, m_attn_gain, m_ssm_gain, m_ssm_a_re, m_ssm_a_im, m_ssm_log_dt, m_ssm_b_re, m_ssm_b_im, m_ssm_c_re, m_ssm_c_im, m_ssm_d, m_w_glu, m_b_glu, m_w_out, m_b_out, m_ln1_g, m_ln1_b, m_w_ff1, m_b_ff1, m_w_ff2, m_b_ff2, m_ln2_g, m_ln2_b, v_w_in, v_attn_gain, v_ssm_gain, v_ssm_a_re, v_ssm_a_im, v_ssm_log_dt, v_ssm_b_re, v_ssm_b_im, v_ssm_c_re, v_ssm_c_im, v_ssm_d, v_w_glu, v_b_glu, v_w_out, v_b_out, v_ln1_g, v_ln1_b, v_w_ff1, v_b_ff1, v_w_ff2, v_b_ff2, v_ln2_g, v_ln2_b):
    loc = locals()
    weights = {n: loc[n] for n in WEIGHT_ORDER}
    moments_m = {n: loc["m_" + n] for n in WEIGHT_ORDER}
    moments_v = {n: loc["v_" + n] for n in WEIGHT_ORDER}
    return _step(x, positions, weights, moments_m, moments_v, loss_target, distributed=True)
```

```python
import functools
import math

import jax
import jax.numpy as jnp
import numpy as np
from jax import lax
from jax.experimental import pallas as pl
from jax.experimental.pallas import tpu as pltpu

F32 = jnp.float32
MXU_DTYPE = jnp.bfloat16

HEAD_DIM = 64
DILATION_PAIRS = ((128, 1), (512, 4), (2048, 16))
ROPE_THETA = 10000.0
SSM_GROUP_DIM = 16
SSM_STATE = 64
LN_EPS = 1e-5
RMS_EPS = 1e-6
NEG_INF = -1e30
ADAM_LR, ADAM_B1, ADAM_B2, ADAM_EPS, ADAM_WD, ADAM_STEP = 0.001, 0.9, 0.999, 1e-08, 0.01, 10

LANES = 128
SUBLANES = 8
QBLK = 128
VMEM_LIMIT = 56 * 2**20
MESH_AXES = ("x", "y", "c")
NDEV = 8

SMALL_NAMES = ("attn_gain", "ssm_gain", "ssm_a_re", "ssm_a_im", "ssm_log_dt", "ssm_b_re", "ssm_b_im", "ssm_c_re",
               "ssm_c_im", "ssm_d", "b_glu", "b_out", "ln1_g", "ln1_b", "b_ff1", "b_ff2", "ln2_g", "ln2_b")
BIG_NAMES = ("w_in", "w_glu", "w_out", "w_ff1", "w_ff2")
WEIGHT_ORDER = ("w_in", "attn_gain", "ssm_gain", "ssm_a_re", "ssm_a_im", "ssm_log_dt", "ssm_b_re", "ssm_b_im", "ssm_c_re",
                "ssm_c_im", "ssm_d", "w_glu", "b_glu", "w_out", "b_out", "ln1_g", "ln1_b", "w_ff1", "b_ff1", "w_ff2",
                "b_ff2", "ln2_g", "ln2_b")


def _cparams(sem=None):
    return pltpu.CompilerParams(dimension_semantics=sem, vmem_limit_bytes=VMEM_LIMIT)


def _mm(a, b):
    return jnp.dot(a.astype(MXU_DTYPE), b.astype(MXU_DTYPE), preferred_element_type=F32)


def _mm_nt(a, b):
    return lax.dot_general(a.astype(MXU_DTYPE), b.astype(MXU_DTYPE), (((1,), (1,)), ((), ())), preferred_element_type=F32)


def _mm_tn(a, b):
    return lax.dot_general(a.astype(MXU_DTYPE), b.astype(MXU_DTYPE), (((0,), (0,)), ((), ())), preferred_element_type=F32)


def _colsum(x):
    return jnp.sum(x, axis=0, keepdims=True)


def _layer_norm(x, g, b):
    mu = jnp.mean(x, axis=-1, keepdims=True)
    xc = x - mu
    var = jnp.mean(xc * xc, axis=-1, keepdims=True)
    return xc * lax.rsqrt(var + LN_EPS) * g + b


def _layer_norm_bwd(dy, pre, g):
    mu = jnp.mean(pre, axis=-1, keepdims=True)
    xc = pre - mu
    var = jnp.mean(xc * xc, axis=-1, keepdims=True)
    r = lax.rsqrt(var + LN_EPS)
    xhat = xc * r
    dyg = dy * g
    dpre = r * (dyg - jnp.mean(dyg, axis=-1, keepdims=True) - xhat * jnp.mean(dyg * xhat, axis=-1, keepdims=True))
    return dpre, _colsum(dy * xhat), _colsum(dy)


def _rms_norm(x, g):
    return x * lax.rsqrt(jnp.mean(x * x, axis=-1, keepdims=True) + RMS_EPS) * g


def _rms_norm_bwd(dy, x, g):
    r = lax.rsqrt(jnp.mean(x * x, axis=-1, keepdims=True) + RMS_EPS)
    dyg = dy * g
    dx = dyg * r - x * (r * r * r) * jnp.mean(dyg * x, axis=-1, keepdims=True)
    return dx, _colsum(dy * x * r)


_GELU_C = math.sqrt(2.0 / math.pi)


def _gelu(x):
    return 0.5 * x * (1.0 + jnp.tanh(_GELU_C * (x + 0.044715 * (x * x * x))))


def _gelu_grad(x):
    t = jnp.tanh(_GELU_C * (x + 0.044715 * (x * x * x)))
    return 0.5 * (1.0 + t) + 0.5 * x * (1.0 - t * t) * (_GELU_C * (1.0 + 3.0 * 0.044715 * x * x))


def _rows_call(name, fn, rows, consts, out_rows, out_accs, tm):
    rows = [r if isinstance(r, tuple) else (r, r.shape[1], 0) for r in rows]
    m = rows[0][0].shape[0]
    assert m % tm == 0
    nr, nc, no, na = len(rows), len(consts), len(out_rows), len(out_accs)

    def body(*refs):
        rr, cr = refs[:nr], refs[nr:nr + nc]
        orr, ar = refs[nr + nc:nr + nc + no], refs[nr + nc + no:]
        outs, accs = fn([r[...] for r in rr], cr)
        for o, v in zip(orr, outs, strict=True):
            o[...] = v.astype(o.dtype)
        if na:
            first = pl.program_id(0) == 0

            @pl.when(first)
            def _():
                for a, v in zip(ar, accs, strict=True):
                    a[...] = v

            @pl.when(jnp.logical_not(first))
            def _():
                for a, v in zip(ar, accs, strict=True):
                    a[...] += v

    def whole(shape):
        return pl.BlockSpec(shape, lambda i, n=len(shape): (0,) * n)

    in_specs = [pl.BlockSpec((tm, w), lambda i, cb=cb: (i, cb)) for _, w, cb in rows] + [whole(c.shape) for c in consts]
    out_specs = [pl.BlockSpec((tm, w), lambda i: (i, 0)) for w, _ in out_rows] + [whole(s) for s in out_accs]
    out_shape = [jax.ShapeDtypeStruct((m, w), dt) for w, dt in out_rows] + [jax.ShapeDtypeStruct(s, F32) for s in out_accs]
    res = pl.pallas_call(body, grid=(m // tm,), in_specs=in_specs, out_specs=out_specs, out_shape=out_shape, name=name,
                         compiler_params=_cparams(("arbitrary",)))(*[r[0] for r in rows], *consts)
    return res[:no], res[no:]


def _wgrad_call(name, x, dy, split, nblk, jb, blk_shape, tm, prologue=None):
    m = x.shape[0]
    kk, nn = blk_shape
    assert m % tm == 0 and nblk % jb == 0
    xw = kk * jb if split == "x" else x.shape[1]
    yw = nn * jb if split == "y" else dy.shape[1]
    nrow = m // tm

    def body(x_ref, dy_ref, o_ref, acc_ref):
        i = pl.program_id(1)

        @pl.when(i == 0)
        def _():
            acc_ref[...] = jnp.zeros_like(acc_ref)

        xv = x_ref[...]
        if prologue is not None:
            xv = prologue(xv)
        xv = xv.astype(MXU_DTYPE)
        dv = dy_ref[...].astype(MXU_DTYPE)
        for j in range(jb):
            xa = xv[:, j * kk:(j + 1) * kk] if split == "x" else xv
            da = dv[:, j * nn:(j + 1) * nn] if split == "y" else dv
            acc_ref[j] += _mm_tn(xa, da)

        @pl.when(i == nrow - 1)
        def _():
            o_ref[...] = acc_ref[...].astype(o_ref.dtype)

    in_specs = [pl.BlockSpec((tm, xw), (lambda j, i: (i, j)) if split == "x" else (lambda j, i: (i, 0))),
                pl.BlockSpec((tm, yw), (lambda j, i: (i, j)) if split == "y" else (lambda j, i: (i, 0)))]
    return pl.pallas_call(
        body, grid=(nblk // jb, nrow), in_specs=in_specs,
        out_specs=pl.BlockSpec((jb, kk, nn), lambda j, i: (j, 0, 0)),
        out_shape=jax.ShapeDtypeStruct((nblk, kk, nn), MXU_DTYPE),
        scratch_shapes=[pltpu.VMEM((jb, kk, nn), F32)], name=name,
        compiler_params=_cparams(("arbitrary", "arbitrary")))(x, dy)


_HBM_SPEC = pl.BlockSpec(memory_space=pltpu.HBM)
_SEM_SPEC = pl.BlockSpec(memory_space=pltpu.SEMAPHORE)
_EFFECT = pltpu.SideEffectType.DATAFLOW_SIDE_EFFECTING


def _my_index():
    return 4 * lax.axis_index("x") + 2 * lax.axis_index("y") + lax.axis_index("c")


def _peer_copies(ins, lands, send_sems, recv_sems, modes):
    x, y, c = lax.axis_index("x"), lax.axis_index("y"), lax.axis_index("c")
    me = 4 * x + 2 * y + c
    pairs = []
    for k in range(NDEV - 1):
        fx, fy, fc = ((k + 1) >> 2) & 1, ((k + 1) >> 1) & 1, (k + 1) & 1
        px, py, pc = (x + fx) % 2, (y + fy) % 2, (c + fc) % 2
        idx = 4 * px + 2 * py + pc
        for a, md in enumerate(modes):
            src = ins[a] if md == "gather" else ins[a].at[idx]
            sem = a * (NDEV - 1) + k
            common = dict(src_ref=src, send_sem=send_sems.at[sem], recv_sem=recv_sems.at[sem], device_id=(px, py, pc),
                          device_id_type=pl.DeviceIdType.MESH)
            pairs.append((pltpu.make_async_remote_copy(dst_ref=lands[a].at[me], **common),
                          pltpu.make_async_remote_copy(dst_ref=lands[a].at[idx], **common)))
    return pairs


def _exchange_start(name, arrays, modes):
    n = len(arrays)
    me = _my_index()
    lands = []
    for a, md in zip(arrays, modes, strict=True):
        piece = a if md == "gather" else lax.dynamic_index_in_dim(a, me, 0, keepdims=False)
        lands.append(lax.dynamic_update_index_in_dim(lax.empty((NDEV,) + piece.shape, a.dtype), piece, me, 0))

    def body(*refs):
        ins, lnd = refs[:n], refs[n:2 * n]
        send_sems, recv_sems = refs[2 * n], refs[2 * n + 1]
        token = refs[-1]
        for out_copy, _ in _peer_copies(ins, lnd, send_sems, recv_sems, modes):
            out_copy.start()
        token[...] = jnp.zeros_like(token)

    sems = pltpu.SemaphoreType.DMA((n * (NDEV - 1),))
    thru = [pltpu.HBM(a.shape, a.dtype) for a in list(arrays) + lands]
    res = pl.pallas_call(
        body, name=name, out_shape=(sems, sems, *thru, jax.ShapeDtypeStruct((SUBLANES, LANES), F32)),
        in_specs=[_HBM_SPEC] * (2 * n),
        out_specs=(_SEM_SPEC, _SEM_SPEC, *([_HBM_SPEC] * (2 * n)), pl.BlockSpec(memory_space=pltpu.VMEM)),
        input_output_aliases={i: 2 + i for i in range(2 * n)},
        compiler_params=pltpu.CompilerParams(has_side_effects=_EFFECT),
    )(*[pltpu.with_memory_space_constraint(a, pltpu.HBM) for a in list(arrays) + lands])
    return (res[0], res[1], res[2:2 + n], res[2 + n:2 + 2 * n]), res[-1]


def _exchange_wait(name, handle, modes, after):
    send_sems, recv_sems, ins_thru, lands_thru = handle
    n = len(ins_thru)

    def body(*refs):
        ins, lnd = refs[:n], refs[n:2 * n]
        for out_copy, arrival in _peer_copies(ins, lnd, refs[2 * n], refs[2 * n + 1], modes):
            out_copy.wait_send()
            arrival.wait_recv()

    thru = [pltpu.HBM(a.shape, a.dtype) for a in list(ins_thru) + list(lands_thru)]
    res = pl.pallas_call(
        body, name=name, out_shape=tuple(thru),
        in_specs=[_HBM_SPEC] * (2 * n) + [_SEM_SPEC, _SEM_SPEC, pl.BlockSpec(memory_space=pl.ANY)],
        out_specs=tuple([_HBM_SPEC] * (2 * n)), input_output_aliases={i: i for i in range(2 * n)},
        compiler_params=pltpu.CompilerParams(has_side_effects=_EFFECT),
    )(*ins_thru, *lands_thru, send_sems, recv_sems, after)
    return res[n:]


def _rope_tables(positions3, inv_freq_row, sign_row):
    b, s, _ = positions3.shape

    def body(pos_ref, f_ref, sg_ref, c_ref, s_ref):
        ang = pos_ref[...].astype(F32) * f_ref[...]
        c_ref[...] = jnp.cos(ang)
        s_ref[...] = jnp.sin(ang) * sg_ref[...]

    row = pl.BlockSpec((1, LANES), lambda i: (0, 0))
    blk = pl.BlockSpec((None, s, LANES), lambda i: (i, 0, 0))
    return pl.pallas_call(
        body, grid=(b,), in_specs=[pl.BlockSpec((None, s, 1), lambda i: (i, 0, 0)), row, row], out_specs=[blk, blk],
        out_shape=[jax.ShapeDtypeStruct((b, s, LANES), F32)] * 2, name="rope_tables",
        compiler_params=_cparams(("arbitrary",)))(positions3, inv_freq_row, sign_row)


def _swap_halves(t):
    lane = lax.broadcasted_iota(jnp.int32, t.shape, 1)
    half = HEAD_DIM // 2
    return jnp.where((lane % HEAD_DIM) < half, pltpu.roll(t, LANES - half, 1), pltpu.roll(t, half, 1))


def _band_masks():
    qi = lax.broadcasted_iota(jnp.int32, (QBLK, 2 * QBLK), 0)
    kj = lax.broadcasted_iota(jnp.int32, (QBLK, 2 * QBLK), 1)
    band = (kj >= qi) & (kj <= qi + QBLK)
    q1 = lax.broadcasted_iota(jnp.int32, (QBLK, QBLK), 0)
    k1 = lax.broadcasted_iota(jnp.int32, (QBLK, QBLK), 1)
    return band, k1 <= q1


def _permute_in(dst, src, d, s):
    n = s // d
    for r in range(d):
        dst[pl.ds(r * n, n), :] = src[pl.ds(r, n, stride=d), :]


def _attn_fwd(proj3, ctab, stab, aw):
    b, s, _ = proj3.shape
    npair = aw // LANES
    scale = HEAD_DIM ** -0.5
    nbr = len(DILATION_PAIRS)

    def body(q_ref, k_ref, v_ref, c_ref, s_ref, o_ref, l_ref, qs, ks, vs, qp, kp, vp, op, lp, ob, lb):
        band, causal = _band_masks()
        cc, ss = c_ref[...], s_ref[...]
        q2 = q_ref[...]
        k2 = k_ref[...]
        q2 = (q2 * cc + _swap_halves(q2) * ss) * scale
        k2 = k2 * cc + _swap_halves(k2) * ss
        v2 = v_ref[...]
        for hh in range(LANES // HEAD_DIM):
            lo = hh * HEAD_DIM
            qs[...] = q2[:, lo:lo + HEAD_DIM]
            ks[...] = k2[:, lo:lo + HEAD_DIM]
            vs[...] = v2[:, lo:lo + HEAD_DIM]
            for br, (window, d) in enumerate(DILATION_PAIRS):
                if d == 1:
                    qr, kr, vr = qs, ks, vs
                else:
                    _permute_in(qp, qs, d, s)
                    _permute_in(kp, ks, d, s)
                    _permute_in(vp, vs, d, s)
                    qr, kr, vr = qp, kp, vp
                nb = (s // d) // QBLK
                for i in range(s // QBLK):
                    r0 = i * QBLK
                    first = (i % nb) == 0
                    k0, nk = (r0, QBLK) if first else (r0 - QBLK, 2 * QBLK)
                    qb = qr[pl.ds(r0, QBLK), :]
                    kb = kr[pl.ds(k0, nk), :]
                    vb = vr[pl.ds(k0, nk), :]
                    sc = jnp.where(causal if first else band, _mm_nt(qb, kb), NEG_INF)
                    mx = jnp.max(sc, axis=-1, keepdims=True)
                    p = jnp.exp(sc - mx)
                    den = jnp.sum(p, axis=-1, keepdims=True)
                    op[pl.ds(r0, QBLK), :] = _mm(p, vb) / den
                    lp[pl.ds(r0, QBLK), :] = mx + jnp.log(den)
                if d == 1:
                    ob[br] = op[...]
                    lb[br] = lp[...]
                else:
                    n = s // d
                    for r in range(d):
                        ob[br, pl.ds(r, n, stride=d), :] = op[pl.ds(r * n, n), :]
                        lb[br, pl.ds(r, n, stride=d), :] = lp[pl.ds(r * n, n), :]
            ls = [lb[br] for br in range(nbr)]
            mx = functools.reduce(jnp.maximum, ls)
            ws = [jnp.exp(l - mx) for l in ls]
            tot = functools.reduce(lambda a, b_: a + b_, ws)
            out = functools.reduce(lambda a, b_: a + b_, [(w / tot) * ob[br] for br, w in enumerate(ws)])
            o_ref[:, lo:lo + HEAD_DIM] = out
            l_ref[:, lo:lo + HEAD_DIM] = jnp.broadcast_to(mx + jnp.log(tot), (s, HEAD_DIM))

    def col(off):
        return pl.BlockSpec((None, s, LANES), lambda bi, hp, off=off: (bi, 0, off + hp))

    tab = pl.BlockSpec((None, s, LANES), lambda bi, hp: (bi, 0, 0))
    hd = pltpu.VMEM((s, HEAD_DIM), F32)
    one = pltpu.VMEM((s, 1), F32)
    return pl.pallas_call(
        body, grid=(b, npair), in_specs=[col(0), col(npair), col(2 * npair), tab, tab], out_specs=[col(0), col(0)],
        out_shape=[jax.ShapeDtypeStruct((b, s, aw), F32)] * 2,
        scratch_shapes=[hd] * 7 + [one, pltpu.VMEM((nbr, s, HEAD_DIM), F32), pltpu.VMEM((nbr, s, 1), F32)],
        name="attn_fwd", compiler_params=_cparams(("arbitrary", "arbitrary")))(proj3, proj3, proj3, ctab, stab)


def _attn_bwd(proj3, ctab, stab, dout3, out3, lse3, aw):
    b, s, _ = proj3.shape
    npair = aw // LANES
    scale = HEAD_DIM ** -0.5

    def body(q_ref, k_ref, v_ref, c_ref, s_ref, do_ref, o_ref, l_ref, dq_ref, dk_ref, dv_ref,
             qs, ks, vs, dos, qp, kp, vp, dop, dqp, dkp, dvp, dqn, dkn, dvn, ls, dl, lp, dlp, dq2, dk2):
        band, causal = _band_masks()
        cc, ss = c_ref[...], s_ref[...]
        q2 = q_ref[...]
        k2 = k_ref[...]
        q2 = (q2 * cc + _swap_halves(q2) * ss) * scale
        k2 = k2 * cc + _swap_halves(k2) * ss
        v2 = v_ref[...]
        do2 = do_ref[...]
        dd2 = do2 * o_ref[...]
        l2 = l_ref[...]
        for hh in range(LANES // HEAD_DIM):
            lo = hh * HEAD_DIM
            qs[...] = q2[:, lo:lo + HEAD_DIM]
            ks[...] = k2[:, lo:lo + HEAD_DIM]
            vs[...] = v2[:, lo:lo + HEAD_DIM]
            dos[...] = do2[:, lo:lo + HEAD_DIM]
            ls[...] = l2[:, lo:lo + 1]
            dl[...] = jnp.sum(dd2[:, lo:lo + HEAD_DIM], axis=-1, keepdims=True)
            dqn[...] = jnp.zeros_like(dqn)
            dkn[...] = jnp.zeros_like(dkn)
            dvn[...] = jnp.zeros_like(dvn)
            for window, d in DILATION_PAIRS:
                if d == 1:
                    qr, kr, vr, dor, lr, dlr = qs, ks, vs, dos, ls, dl
                else:
                    for dst, src in ((qp, qs), (kp, ks), (vp, vs), (dop, dos), (lp, ls), (dlp, dl)):
                        _permute_in(dst, src, d, s)
                    qr, kr, vr, dor, lr, dlr = qp, kp, vp, dop, lp, dlp
                dkp[...] = jnp.zeros_like(dkp)
                dvp[...] = jnp.zeros_like(dvp)
                nb = (s // d) // QBLK
                for i in range(s // QBLK):
                    r0 = i * QBLK
                    first = (i % nb) == 0
                    k0, nk = (r0, QBLK) if first else (r0 - QBLK, 2 * QBLK)
                    qb = qr[pl.ds(r0, QBLK), :]
                    dob = dor[pl.ds(r0, QBLK), :]
                    kb = kr[pl.ds(k0, nk), :]
                    vb = vr[pl.ds(k0, nk), :]
                    sc = jnp.where(causal if first else band, _mm_nt(qb, kb), NEG_INF)
                    p = jnp.exp(sc - lr[pl.ds(r0, QBLK), :])
                    ds_ = p * (_mm_nt(dob, vb) - dlr[pl.ds(r0, QBLK), :])
                    dqp[pl.ds(r0, QBLK), :] = _mm(ds_, kb)
                    dkp[pl.ds(k0, nk), :] += _mm_tn(ds_, qb)
                    dvp[pl.ds(k0, nk), :] += _mm_tn(p, dob)
                n = s // d
                for r in range(d):
                    if d == 1:
                        src, dst = pl.ds(0, s), pl.ds(0, s)
                    else:
                        src, dst = pl.ds(r * n, n), pl.ds(r, n, stride=d)
                    dqn[dst, :] += dqp[src, :]
                    dkn[dst, :] += dkp[src, :]
                    dvn[dst, :] += dvp[src, :]
            dq2[:, lo:lo + HEAD_DIM] = dqn[...] * scale
            dk2[:, lo:lo + HEAD_DIM] = dkn[...]
            dv_ref[:, lo:lo + HEAD_DIM] = dvn[...]
        g = dq2[...]
        dq_ref[...] = g * cc + _swap_halves(g * ss)
        g = dk2[...]
        dk_ref[...] = g * cc + _swap_halves(g * ss)

    def col(off):
        return pl.BlockSpec((None, s, LANES), lambda bi, hp, off=off: (bi, 0, off + hp))

    tab = pl.BlockSpec((None, s, LANES), lambda bi, hp: (bi, 0, 0))
    hd = pltpu.VMEM((s, HEAD_DIM), F32)
    one = pltpu.VMEM((s, 1), F32)
    pair = pltpu.VMEM((s, LANES), F32)
    return pl.pallas_call(
        body, grid=(b, npair), in_specs=[col(0), col(npair), col(2 * npair), tab, tab, col(0), col(0), col(0)],
        out_specs=[col(0)] * 3, out_shape=[jax.ShapeDtypeStruct((b, s, aw), F32)] * 3,
        scratch_shapes=[hd] * 14 + [one] * 4 + [pair] * 2,
        name="attn_bwd", compiler_params=_cparams(("arbitrary", "arbitrary")))(
            proj3, proj3, proj3, ctab, stab, dout3, out3, lse3)


def _s5_discretise(a_re, a_im, log_dt, bt_re, bt_im):
    dt = jnp.exp(log_dt)
    mag = jnp.exp(a_re * dt)
    ang = a_im * dt
    lb_re = mag * jnp.cos(ang)
    lb_im = mag * jnp.sin(ang)
    den = a_re * a_re + a_im * a_im
    nr = lb_re - 1.0
    ni = lb_im
    cr = (nr * a_re + ni * a_im) / den
    ci = (ni * a_re - nr * a_im) / den
    return lb_re, lb_im, cr * bt_re - ci * bt_im, cr * bt_im + ci * bt_re


def _s5_params(a_re, a_im, log_dt, bt_re, bt_im):
    def body(ar, ai, ld, br, bi, o1, o2, o3, o4):
        r = _s5_discretise(ar[...], ai[...], ld[...], br[...], bi[...])
        for o, v in zip((o1, o2, o3, o4), r, strict=True):
            o[...] = v

    sd = jax.ShapeDtypeStruct
    return pl.pallas_call(body, out_shape=[sd(a_re.shape, F32)] * 2 + [sd(bt_re.shape, F32)] * 2, name="s5_params")(
        a_re, a_im, log_dt, bt_re, bt_im)


def _s5_params_bwd(a_re, a_im, log_dt, bt_re, bt_im, cts):
    def body(ar, ai, ld, br, bi, c1, c2, c3, c4, o1, o2, o3, o4, o5):
        _, vjp = jax.vjp(_s5_discretise, ar[...], ai[...], ld[...], br[...], bi[...])
        r = vjp((c1[...], c2[...], c3[...], c4[...]))
        for o, v in zip((o1, o2, o3, o4, o5), r, strict=True):
            o[...] = v

    sd = jax.ShapeDtypeStruct
    return pl.pallas_call(
        body, out_shape=[sd(a_re.shape, F32)] * 2 + [sd(log_dt.shape, F32)] + [sd(bt_re.shape, F32)] * 2, name="s5_params_bwd")(
            a_re, a_im, log_dt, bt_re, bt_im, *cts)


S5_TC = 128


def _s5_fwd(proj3, ucol, lam_re, lam_im, bre, bim, cre, cim, dvec, sw):
    b, s, _ = proj3.shape
    nt = lam_re.shape[0]
    ncb = sw // LANES
    tpc = nt // ncb
    tc = S5_TC

    def body(u_ref, lr_ref, li_ref, bre_ref, bim_ref, cre_ref, cim_ref, d_ref, y_ref, xr_ref, xi_ref, sr, si):
        @pl.when(pl.program_id(0) == 0)
        def _():
            sr[...] = jnp.zeros_like(sr)
            si[...] = jnp.zeros_like(si)

        u_all = u_ref[...].reshape(b * tc, sw)
        for cb in range(ncb):
            ucb = u_all[:, cb * LANES:(cb + 1) * LANES].astype(MXU_DTYPE)
            for t in range(cb * tpc, (cb + 1) * tpc):
                bur, bui = _mm(ucb, bre_ref[t]), _mm(ucb, bim_ref[t])
                for bi in range(b):
                    xr_ref[bi, pl.ds(t, tc, stride=nt), :] = bur[bi * tc:(bi + 1) * tc]
                    xi_ref[bi, pl.ds(t, tc, stride=nt), :] = bui[bi * tc:(bi + 1) * tc]
        lr, li = lr_ref[...], li_ref[...]

        def step(j, carry):
            off = pl.multiple_of(j * nt, nt)
            new = []
            for bi in range(b):
                pr, pi = carry[2 * bi], carry[2 * bi + 1]
                nr = lr * pr - li * pi + xr_ref[bi, pl.ds(off, nt), :]
                ni = lr * pi + li * pr + xi_ref[bi, pl.ds(off, nt), :]
                xr_ref[bi, pl.ds(off, nt), :] = nr
                xi_ref[bi, pl.ds(off, nt), :] = ni
                new += [nr, ni]
            return tuple(new)

        init = tuple(v for bi in range(b) for v in (sr[bi], si[bi]))
        fin = lax.fori_loop(0, tc, step, init, unroll=4)
        for bi in range(b):
            sr[bi] = fin[2 * bi]
            si[bi] = fin[2 * bi + 1]
        for cb in range(ncb):
            cols = slice(cb * LANES, (cb + 1) * LANES)
            acc = d_ref[:, cols] * u_all[:, cols]
            for t in range(cb * tpc, (cb + 1) * tpc):
                rows = pl.ds(t, tc, stride=nt)
                xr_all = jnp.concatenate([xr_ref[bi, rows, :].astype(MXU_DTYPE) for bi in range(b)], axis=0)
                xi_all = jnp.concatenate([xi_ref[bi, rows, :].astype(MXU_DTYPE) for bi in range(b)], axis=0)
                acc = acc + (_mm(xr_all, cre_ref[t]) - _mm(xi_all, cim_ref[t]))
            for bi in range(b):
                y_ref[bi, :, cols] = acc[bi * tc:(bi + 1) * tc]

    def whole(a):
        return pl.BlockSpec(a.shape, lambda c, n=a.ndim: (0,) * n)

    xblk = pl.BlockSpec((b, tc * nt, LANES), lambda c: (0, c, 0))
    return pl.pallas_call(
        body, grid=(s // tc,),
        in_specs=[pl.BlockSpec((b, tc, sw), lambda c: (0, c, ucol))] + [whole(a) for a in (lam_re, lam_im, bre, bim, cre, cim, dvec)],
        out_specs=[pl.BlockSpec((b, tc, sw), lambda c: (0, c, 0)), xblk, xblk],
        out_shape=[jax.ShapeDtypeStruct((b, s, sw), F32)] + [jax.ShapeDtypeStruct((b, s * nt, LANES), F32)] * 2,
        scratch_shapes=[pltpu.VMEM((b, nt, LANES), F32)] * 2, name="s5_fwd",
        compiler_params=_cparams(("arbitrary",)))(proj3, lam_re, lam_im, bre, bim, cre, cim, dvec)


def _s5_bwd(proj3, ucol, dy3, xr3, xi3, lam_re, lam_im, bre, bim, cre, cim, dvec, sw):
    b, s, _ = proj3.shape
    nt = lam_re.shape[0]
    ncb = sw // LANES
    tpc = nt // ncb
    tc = S5_TC
    nchunk = s // tc

    def body(u_ref, dy_ref, xr_ref, xi_ref, pr_ref, pi_ref, lr_ref, li_ref, bre_ref, bim_ref, cre_ref, cim_ref, d_ref,
             du_ref, dbre, dbim, dcre, dcim, dlr, dli, dd, gr, gi, sr, si):
        step_id = pl.program_id(0)

        @pl.when(step_id == 0)
        def _():
            for r in (sr, si, dbre, dbim, dcre, dcim, dlr, dli, dd):
                r[...] = jnp.zeros_like(r)

        u_all = u_ref[...].reshape(b * tc, sw)
        dy_all = dy_ref[...].reshape(b * tc, sw)
        for cb in range(ncb):
            dycb = dy_all[:, cb * LANES:(cb + 1) * LANES].astype(MXU_DTYPE)
            for t in range(cb * tpc, (cb + 1) * tpc):
                dxr, dxi = _mm_nt(dycb, cre_ref[t]), _mm_nt(dycb, cim_ref[t])
                for bi in range(b):
                    gr[bi, pl.ds(t, tc, stride=nt), :] = dxr[bi * tc:(bi + 1) * tc]
                    gi[bi, pl.ds(t, tc, stride=nt), :] = -dxi[bi * tc:(bi + 1) * tc]
        lr, li = lr_ref[...], li_ref[...]

        def step(jj, carry):
            off = pl.multiple_of((tc - 1 - jj) * nt, nt)
            new = []
            for bi in range(b):
                nr_, ni_ = carry[2 * bi], carry[2 * bi + 1]
                vr = gr[bi, pl.ds(off, nt), :] + lr * nr_ + li * ni_
                vi = gi[bi, pl.ds(off, nt), :] + lr * ni_ - li * nr_
                gr[bi, pl.ds(off, nt), :] = vr
                gi[bi, pl.ds(off, nt), :] = vi
                new += [vr, vi]
            return tuple(new)

        init = tuple(v for bi in range(b) for v in (sr[bi], si[bi]))
        fin = lax.fori_loop(0, tc, step, init, unroll=4)
        for bi in range(b):
            sr[bi] = fin[2 * bi]
            si[bi] = fin[2 * bi + 1]

        has_prev = (step_id != nchunk - 1).astype(F32)
        rest = (tc - 1) * nt
        alr = jnp.zeros((nt, LANES), F32)
        ali = jnp.zeros((nt, LANES), F32)
        for bi in range(b):
            s0r, s0i = gr[bi, pl.ds(0, nt), :], gi[bi, pl.ds(0, nt), :]
            x0r, x0i = pr_ref[bi] * has_prev, pi_ref[bi] * has_prev
            alr += s0r * x0r + s0i * x0i
            ali += s0i * x0r - s0r * x0i
            s1r, s1i = gr[bi, pl.ds(nt, rest), :], gi[bi, pl.ds(nt, rest), :]
            x1r, x1i = xr_ref[bi, pl.ds(0, rest), :], xi_ref[bi, pl.ds(0, rest), :]
            alr += jnp.sum((s1r * x1r + s1i * x1i).reshape(tc - 1, nt, LANES), axis=0)
            ali += jnp.sum((s1i * x1r - s1r * x1i).reshape(tc - 1, nt, LANES), axis=0)
        dlr[...] += alr
        dli[...] += ali

        for cb in range(ncb):
            cols = slice(cb * LANES, (cb + 1) * LANES)
            ucb32, dycb32 = u_all[:, cols], dy_all[:, cols]
            ucb, dycb = ucb32.astype(MXU_DTYPE), dycb32.astype(MXU_DTYPE)
            acc = d_ref[:, cols] * dycb32
            for t in range(cb * tpc, (cb + 1) * tpc):
                rows = pl.ds(t, tc, stride=nt)

                def stacked(ref):
                    return jnp.concatenate([ref[bi, rows, :].astype(MXU_DTYPE) for bi in range(b)], axis=0)

                vr, vi = stacked(gr), stacked(gi)
                acc = acc + (_mm_nt(vr, bre_ref[t]) + _mm_nt(vi, bim_ref[t]))
                dbre[t] += _mm_tn(ucb, vr)
                dbim[t] += _mm_tn(ucb, vi)
                dcre[t] += _mm_tn(stacked(xr_ref), dycb)
                dcim[t] -= _mm_tn(stacked(xi_ref), dycb)
            for bi in range(b):
                du_ref[bi, :, cols] = acc[bi * tc:(bi + 1) * tc]
            dd[:, cols] += _colsum(dycb32 * ucb32)

    def whole(a):
        return pl.BlockSpec(a.shape, lambda c, n=len(a.shape): (0,) * n)

    def rev(c):
        return nchunk - 1 - c

    xblk = pl.BlockSpec((b, tc * nt, LANES), lambda c: (0, rev(c), 0))
    prev = pl.BlockSpec((b, nt, LANES), lambda c: (0, jnp.maximum(rev(c) * tc - 1, 0), 0))
    sd = jax.ShapeDtypeStruct
    blk = sd(bre.shape, F32)
    acc_shapes = [blk, blk, sd(cre.shape, F32), sd(cre.shape, F32), sd(lam_re.shape, F32), sd(lam_re.shape, F32), sd((1, sw), F32)]
    return pl.pallas_call(
        body, grid=(nchunk,),
        in_specs=[pl.BlockSpec((b, tc, sw), lambda c: (0, rev(c), ucol)), pl.BlockSpec((b, tc, sw), lambda c: (0, rev(c), 0)),
                  xblk, xblk, prev, prev] + [whole(a) for a in (lam_re, lam_im, bre, bim, cre, cim, dvec)],
        out_specs=[pl.BlockSpec((b, tc, sw), lambda c: (0, rev(c), 0))] + [whole(a) for a in acc_shapes],
        out_shape=[sd((b, s, sw), F32)] + acc_shapes,
        scratch_shapes=[pltpu.VMEM((b, tc * nt, LANES), F32)] * 2 + [pltpu.VMEM((b, nt, LANES), F32)] * 2, name="s5_bwd",
        compiler_params=_cparams(("arbitrary",)))(proj3, dy3, xr3, xi3, xr3, xi3, lam_re, lam_im, bre, bim, cre, cim, dvec)


def _s5_blocks(bb, cc_, ncb):
    g, n, p = bb.shape
    eye = jnp.eye(g, dtype=bb.dtype)
    bd = jnp.einsum("gnp,gh->gnhp", bb, eye).reshape(g * n, g * p)
    cd = jnp.einsum("gnp,gh->hpgn", cc_, eye).reshape(g * p, g * n)
    nt = g * p // LANES
    tpc = nt // ncb
    bblk = jnp.stack([bd[(t // tpc) * LANES:(t // tpc + 1) * LANES, t * LANES:(t + 1) * LANES] for t in range(nt)])
    cblk = jnp.stack([cd[t * LANES:(t + 1) * LANES, (t // tpc) * LANES:(t // tpc + 1) * LANES] for t in range(nt)])
    return bblk, cblk


def _s5_unblock(dbblk, dcblk, g, n, p, ncb):
    nt = g * p // LANES
    tpc = nt // ncb
    gpt = LANES // p
    db, dc = [], []
    for gi in range(g):
        t, gl = gi // gpt, gi % gpt
        r0 = (gi * n) % LANES
        db.append(dbblk[t, r0:r0 + n, gl * p:(gl + 1) * p])
        dc.append(dcblk[t, gl * p:(gl + 1) * p, r0:r0 + n].T)
    return jnp.stack(db), jnp.stack(dc)


def _adamw_math(w, g, m, v):
    m = ADAM_B1 * m + (1.0 - ADAM_B1) * g
    v = ADAM_B2 * v + (1.0 - ADAM_B2) * (g * g)
    m_hat = m / (1.0 - ADAM_B1 ** ADAM_STEP)
    v_hat = v / (1.0 - ADAM_B2 ** ADAM_STEP)
    delta = -ADAM_LR * (m_hat / (jnp.sqrt(v_hat) + ADAM_EPS) + ADAM_WD * w)
    return delta, m, v


def _adamw_layer(name, parts, w, m, v, layer, prev, tr):
    nparts, r, c = parts.shape
    assert r % tr == 0
    if prev is None:
        prev = [lax.empty(w.shape, F32) for _ in range(4)]

    def body(p_ref, w_ref, m_ref, v_ref, *rest):
        g_out, d_out, m_out, v_out = rest[4:]
        g = p_ref[0].astype(F32)
        for k in range(1, nparts):
            g = g + p_ref[k].astype(F32)
        delta, mn, vn = _adamw_math(w_ref[...], g, m_ref[...], v_ref[...])
        g_out[...] = g
        d_out[...] = delta
        m_out[...] = mn
        v_out[...] = vn

    blk = pl.BlockSpec((None, tr, c), lambda i: (layer, i, 0))
    kept = pl.BlockSpec(memory_space=pl.ANY)
    return pl.pallas_call(
        body, grid=(r // tr,), in_specs=[pl.BlockSpec((nparts, tr, c), lambda i: (0, i, 0)), blk, blk, blk] + [kept] * 4,
        out_specs=[blk] * 4, out_shape=[jax.ShapeDtypeStruct(w.shape, F32)] * 4, name=name,
        input_output_aliases={4 + k: k for k in range(4)},
        compiler_params=_cparams(("arbitrary",)))(parts, w, m, v, *prev)


def _pack_small(pieces):
    flat = jnp.concatenate([p.reshape(-1) for p in pieces])
    n = flat.shape[0]
    unit = SUBLANES * LANES
    padded = -(-n // unit) * unit
    return jnp.pad(flat, (0, padded - n)).reshape(padded // LANES, LANES)


def _unpack_small(packed, shapes):
    flat = packed.reshape(-1)
    out, off = [], 0
    for shp in shapes:
        sz = int(np.prod(shp))
        out.append(flat[off:off + sz].reshape(shp))
        off += sz
    return out


def _step(x, positions, weights, moments_m, moments_v, loss_target, distributed):
    f32 = F32
    bsz, seq, dm = x.shape
    depth = weights["w_in"].shape[0]
    aw = weights["attn_gain"].shape[1]
    sw = weights["ssm_gain"].shape[1]
    dff = weights["b_ff1"].shape[1]
    ng, npst = weights["ssm_a_re"].shape[1:]
    gdim = weights["ssm_d"].shape[2]
    in_w = 3 * aw + sw
    t_rows = bsz * seq
    alpha = (2.0 * depth) ** 0.25
    ncb = sw // LANES
    nt = ng * npst // LANES
    ndev = NDEV if distributed else 1
    tm = 256

    gather_modes = ["gather"] * len(BIG_NAMES)
    if distributed:
        started = [_exchange_start(f"weights_start_l{l}", [weights[n][l].astype(MXU_DTYPE) for n in BIG_NAMES], gather_modes)
                   for l in range(depth)]
        later_tokens = functools.reduce(lambda a, b_: a + b_, [tok for _, tok in started[1:]])

    half = HEAD_DIM // 2
    inv_freq = ROPE_THETA ** (-jnp.arange(half, dtype=f32) * 2.0 / HEAD_DIM)
    reps = LANES // half
    inv_row = jnp.tile(inv_freq, reps)[None, :]
    sign_row = jnp.tile(jnp.concatenate([-jnp.ones((half,), f32), jnp.ones((half,), f32)]), LANES // HEAD_DIM)[None, :]
    ctab, stab = _rope_tables(positions[..., None], inv_row, sign_row)

    def row(v):
        return v.reshape(1, -1)

    h = x.reshape(t_rows, dm)
    saved = []
    for l in range(depth):
        if distributed:
            g_in, g_glu, g_out, g_ff1, g_ff2 = _exchange_wait(
                f"weights_wait_l{l}", started[l][0], gather_modes, later_tokens if l == 0 else h)
        else:
            g_in, g_glu, g_out, g_ff1, g_ff2 = [weights[n][l].astype(MXU_DTYPE)[None] for n in BIG_NAMES]
        w_in_l = g_in
        w_glu_l = g_glu.reshape(sw, sw)
        w_out_l = g_out.reshape(dm, dm)
        w_ff1_l = g_ff1
        w_ff2_l = g_ff2.reshape(dff, dm)

        def in_proj(rv, cr):
            return [jnp.concatenate([_mm(rv[0], cr[0][j]) for j in range(ndev)], axis=-1)], []

        (proj,), _ = _rows_call("in_proj", in_proj, [h], [w_in_l], [(in_w, f32)], [], tm)
        proj3 = proj.reshape(bsz, seq, in_w)
        attn3, lse3 = _attn_fwd(proj3, ctab, stab, aw)

        a_re, a_im = weights["ssm_a_re"][l][:, None, :], weights["ssm_a_im"][l][:, None, :]
        log_dt = weights["ssm_log_dt"][l][:, None, None]
        bt_re = weights["ssm_b_re"][l].transpose(0, 2, 1)
        bt_im = weights["ssm_b_im"][l].transpose(0, 2, 1)
        lb_re, lb_im, bb_re, bb_im = _s5_params(a_re, a_im, log_dt, bt_re, bt_im)
        lam_re, lam_im = lb_re.reshape(nt, LANES), lb_im.reshape(nt, LANES)
        bre, cre = _s5_blocks(bb_re, weights["ssm_c_re"][l], ncb)
        bim, cim = _s5_blocks(bb_im, weights["ssm_c_im"][l], ncb)
        s5c = [a_.astype(MXU_DTYPE) for a_ in (bre, bim, cre, cim)]
        dvec = row(weights["ssm_d"][l])
        ypre3, xr3, xi3 = _s5_fwd(proj3, 3 * aw // sw, lam_re, lam_im, *s5c, dvec, sw)
        attn, ypre = attn3.reshape(t_rows, aw), ypre3.reshape(t_rows, sw)

        b_glu, ga, gs = row(weights["b_glu"][l]), row(weights["attn_gain"][l]), row(weights["ssm_gain"][l])

        def mix(rv, cr):
            at, yp = rv
            g = _gelu(yp)
            ssm = g * jax.nn.sigmoid(_mm(g, cr[0][...]) + cr[1][...])
            return [jnp.concatenate([_rms_norm(at, cr[2][...]), _rms_norm(ssm, cr[3][...])], axis=-1)], []

        (mixed,), _ = _rows_call("mix", mix, [attn, ypre], [w_glu_l, b_glu, ga, gs], [(dm, MXU_DTYPE)], [], tm)

        b_out, g1, b1 = row(weights["b_out"][l]), row(weights["ln1_g"][l]), row(weights["ln1_b"][l])

        def out_proj(rv, cr):
            pre = alpha * rv[1] + _mm(rv[0], cr[0][...]) + cr[1][...]
            return [pre, _layer_norm(pre, cr[2][...], cr[3][...])], []

        (pre1, h1), _ = _rows_call("out_proj", out_proj, [mixed, h], [w_out_l, b_out, g1, b1], [(dm, f32), (dm, f32)], [], tm)

        b_ff1 = row(weights["b_ff1"][l])

        def ff1(rv, cr):
            return [jnp.concatenate([_mm(rv[0], cr[0][j]) for j in range(ndev)], axis=-1) + cr[1][...]], []

        (act,), _ = _rows_call("ff1", ff1, [h1], [w_ff1_l, b_ff1], [(dff, f32)], [], tm)

        b_ff2, g2, b2 = row(weights["b_ff2"][l]), row(weights["ln2_g"][l]), row(weights["ln2_b"][l])

        def ff2(rv, cr):
            r = jnp.square(jnp.maximum(rv[0], 0.0))
            pre = alpha * rv[1] + _mm(r, cr[0][...]) + cr[1][...]
            return [pre, _layer_norm(pre, cr[2][...], cr[3][...])], []

        (pre2, h2), _ = _rows_call("ff2", ff2, [act, h1], [w_ff2_l, b_ff2, g2, b2], [(dm, f32), (dm, f32)], [], tm)

        saved.append(dict(h=h, proj3=proj3, attn=attn, lse3=lse3, attn3=attn3, ypre=ypre, xr3=xr3, xi3=xi3, mixed=mixed, pre1=pre1,
                          h1=h1, act=act, pre2=pre2, w_in=w_in_l, w_glu=w_glu_l, w_out=w_out_l, w_ff1=w_ff1_l, w_ff2=w_ff2_l,
                          lam=(lam_re, lam_im), s5c=s5c, dvec=dvec, b_glu=b_glu, ga=ga, gs=gs, g1=g1, g2=g2,
                          s5in=(a_re, a_im, log_dt, bt_re, bt_im)))
        h = h2

    g2_last = saved[-1]["g2"]

    def loss_fn(rv, cr):
        y, tgt, pre = rv
        err = y - tgt
        part = 0.5 * jnp.sum(jnp.mean(err * err, axis=-1, keepdims=True), axis=0, keepdims=True)
        dpre, dg, db = _layer_norm_bwd(err * (1.0 / dm), pre, cr[0][...])
        return [dpre], [jnp.broadcast_to(part, (1, LANES)), dg, db, _colsum(dpre)]

    (dpre2,), (loss_acc, dg2, db2, dbff2) = _rows_call(
        "loss", loss_fn, [h, loss_target.reshape(t_rows, dm), saved[-1]["pre2"]], [g2_last], [(dm, f32)],
        [(1, LANES), (1, dm), (1, dm), (1, dm)], tm)
    loss = loss_acc[0, 0]
    if distributed:
        loss = lax.psum(loss, MESH_AXES)

    big_parts = {n: [None] * depth for n in BIG_NAMES}
    small_parts = [None] * depth
    grad_handles = [None] * depth
    grad_modes = ["scatter"] * len(BIG_NAMES) + ["gather"]
    grad_x = None
    for l in reversed(range(depth)):
        sv = saved[l]

        def ff2_bwd(rv, cr):
            da = _mm_nt(rv[0], cr[0][...]) * (2.0 * jnp.maximum(rv[1], 0.0))
            return [da], [_colsum(da)]

        (dact,), (dbff1,) = _rows_call("ff2_bwd", ff2_bwd, [dpre2, sv["act"]], [sv["w_ff2"]], [(dff, MXU_DTYPE)], [(1, dff)], tm)
        big_parts["w_ff2"][l] = _wgrad_call("wgrad_ff2", sv["act"], dpre2, "x", ndev, ndev // 2 if ndev > 1 else 1,
                                            (dff // ndev, dm), tm, prologue=lambda a: jnp.square(jnp.maximum(a, 0.0)))
        big_parts["w_ff1"][l] = _wgrad_call("wgrad_ff1", sv["h1"], dact, "y", ndev, ndev // 2 if ndev > 1 else 1,
                                            (dm, dff // ndev), tm)

        def ff1_bwd(rv, cr):
            dacc = alpha * rv[1]
            wpb = dff // ndev
            for j in range(ndev):
                dacc = dacc + _mm_nt(rv[0][:, j * wpb:(j + 1) * wpb], cr[0][j])
            dpre, dg, db = _layer_norm_bwd(dacc, rv[2], cr[1][...])
            return [dpre], [dg, db, _colsum(dpre)]

        (dpre1,), (dg1, db1, dbout) = _rows_call("ff1_bwd", ff1_bwd, [dact, dpre2, sv["pre1"]], [sv["w_ff1"], sv["g1"]],
                                                 [(dm, f32)], [(1, dm)] * 3, tm)
        big_parts["w_out"][l] = _wgrad_call("wgrad_out", sv["mixed"], dpre1, "x", ndev, ndev, (dm // ndev, dm), tm)

        def mix_bwd(rv, cr):
            dp, at, yp = rv
            w_out_r, w_glu_r, bg, ga_, gs_ = cr
            dmixed = _mm_nt(dp, w_out_r[...])
            g = _gelu(yp)
            sig = jax.nn.sigmoid(_mm(g, w_glu_r[...]) + bg[...])
            ssm = g * sig
            dat, dga = _rms_norm_bwd(dmixed[:, :aw], at, ga_[...])
            dssm, dgs = _rms_norm_bwd(dmixed[:, aw:], ssm, gs_[...])
            dz = dssm * g * sig * (1.0 - sig)
            dg = dssm * sig + _mm_nt(dz, w_glu_r[...])
            return [dat, dg * _gelu_grad(yp), dz], [dga, dgs, _colsum(dz)]

        (dattn, dypre, dz), (dga, dgs, dbglu) = _rows_call(
            "mix_bwd", mix_bwd, [dpre1, sv["attn"], sv["ypre"]], [sv["w_out"], sv["w_glu"], sv["b_glu"], sv["ga"], sv["gs"]],
            [(aw, f32), (sw, f32), (sw, MXU_DTYPE)], [(1, aw), (1, sw), (1, sw)], tm)
        big_parts["w_glu"][l] = _wgrad_call("wgrad_glu", sv["ypre"], dz, "x", 1, 1, (sw, sw), tm, prologue=_gelu).reshape(
            ndev, sw // ndev, sw)

        du3, dbre, dbim, dcre, dcim, dlr, dli, dd = _s5_bwd(
            sv["proj3"], 3 * aw // sw, dypre.reshape(bsz, seq, sw), sv["xr3"], sv["xi3"], *sv["lam"], *sv["s5c"], sv["dvec"], sw)
        dbb_re, dc_re = _s5_unblock(dbre, dcre, ng, gdim, npst, ncb)
        dbb_im, dc_im = _s5_unblock(dbim, dcim, ng, gdim, npst, ncb)
        da_re, da_im, dldt, dbt_re, dbt_im = _s5_params_bwd(
            *sv["s5in"], (dlr.reshape(ng, 1, npst), dli.reshape(ng, 1, npst), dbb_re, dbb_im))

        dq3, dk3, dv3 = _attn_bwd(sv["proj3"], ctab, stab, dattn.reshape(bsz, seq, aw), sv["attn3"], sv["lse3"], aw)
        dproj = jnp.concatenate([dq3, dk3, dv3, du3], axis=-1).reshape(t_rows, in_w)
        big_parts["w_in"][l] = _wgrad_call("wgrad_in", sv["h"], dproj, "y", ndev, ndev, (dm, in_w // ndev), tm)

        small_parts[l] = dict(
            attn_gain=dga, ssm_gain=dgs, ssm_a_re=da_re[:, 0], ssm_a_im=da_im[:, 0], ssm_log_dt=dldt[:, 0, 0], ssm_b_re=dbt_re.transpose(0, 2, 1),
            ssm_b_im=dbt_im.transpose(0, 2, 1), ssm_c_re=dc_re, ssm_c_im=dc_im, ssm_d=dd, b_glu=dbglu, b_out=dbout, ln1_g=dg1,
            ln1_b=db1, b_ff1=dbff1, b_ff2=dbff2, ln2_g=dg2, ln2_b=db2)

        layer_grads = [big_parts[n][l] for n in BIG_NAMES] + [_pack_small([small_parts[l][n] for n in SMALL_NAMES])]
        if distributed:
            grad_handles[l], token = _exchange_start(f"grads_start_l{l}", layer_grads, grad_modes)
        else:
            grad_handles[l], token = layer_grads, jnp.zeros((SUBLANES, LANES), f32)

        wpb = in_w // ndev
        if l > 0:
            prev = saved[l - 1]

            def in_bwd(rv, cr):
                dacc = alpha * rv[1]
                for j in range(ndev):
                    dacc = dacc + _mm_nt(rv[0][:, j * wpb:(j + 1) * wpb], cr[0][j])
                dpre, dg, db = _layer_norm_bwd(dacc, rv[2], cr[1][...])
                return [dpre], [dg, db, _colsum(dpre)]

            (dpre2,), (dg2, db2, dbff2) = _rows_call("in_bwd", in_bwd, [dproj, dpre1, prev["pre2"]],
                                                     [sv["w_in"], prev["g2"], token], [(dm, f32)], [(1, dm)] * 3, tm)
        else:
            def in_bwd0(rv, cr):
                dacc = alpha * rv[1]
                for j in range(ndev):
                    dacc = dacc + _mm_nt(rv[0][:, j * wpb:(j + 1) * wpb], cr[0][j])
                return [dacc], []

            (grad_x,), _ = _rows_call("in_bwd0", in_bwd0, [dproj, dpre1], [sv["w_in"], token], [(dm, f32)], [], tm)

    small_shapes = [weights[n].shape[1:] for n in SMALL_NAMES]
    outs = {n: None for n in BIG_NAMES}
    small_out = [None] * depth
    after = grad_x
    for l in reversed(range(depth)):
        if distributed:
            recv = _exchange_wait(f"grads_wait_l{l}", grad_handles[l], grad_modes, after)
        else:
            recv = [g_[None] if i == len(BIG_NAMES) else g_ for i, g_ in enumerate(grad_handles[l])]
        for n, parts in zip(BIG_NAMES, recv[:-1], strict=True):
            outs[n] = _adamw_layer("adamw_" + n, parts, weights[n], moments_m[n], moments_v[n], l, outs[n],
                                   min(parts.shape[1], 256))
        packed = [_pack_small([d[n][l] for n in SMALL_NAMES])[None] for d in (weights, moments_m, moments_v)]
        small_out[l] = _adamw_layer("adamw_small", recv[-1], *packed, 0, None, recv[-1].shape[1])
        after = small_out[l][0]
    for k in range(4):
        pieces = [_unpack_small(small_out[l][k][0], small_shapes) for l in range(depth)]
        for i, n in enumerate(SMALL_NAMES):
            outs.setdefault(n, [None] * 4)
            outs[n][k] = jnp.stack([pieces[l][i] for l in range(depth)])

    result = [loss, grad_x.reshape(bsz, seq, dm)]
    for k in range(4):
        result += [outs[n][k] for n in WEIGHT_ORDER]
    return tuple(result)


def kernel(x, positions, w_in, attn_gain, ssm_gain, ssm_a_re, ssm_a_im, ssm_log_dt, ssm_b_re, ssm_b_im, ssm_c_re, ssm_c_im, ssm_d, w_glu, b_glu, w_out, b_out, ln1_g, ln1_b, w_ff1, b_ff1, w_ff2, b_ff2, ln2_g, ln2_b, loss_target, m_w_in, m_attn_gain, m_ssm_gain, m_ssm_a_re, m_ssm_a_im, m_ssm_log_dt, m_ssm_b_re, m_ssm_b_im, m_ssm_c_re, m_ssm_c_im, m_ssm_d, m_w_glu, m_b_glu, m_w_out, m_b_out, m_ln1_g, m_ln1_b, m_w_ff1, m_b_ff1, m_w_ff2, m_b_ff2, m_ln2_g, m_ln2_b, v_w_in, v_attn_gain, v_ssm_gain, v_ssm_a_re, v_ssm_a_im, v_ssm_log_dt, v_ssm_b_re, v_ssm_b_im, v_ssm_c_re, v_ssm_c_im, v_ssm_d, v_w_glu, v_b_glu, v_w_out, v_b_out, v_ln1_g, v_ln1_b, v_w_ff1, v_b_ff1, v_w_ff2, v_b_ff2, v_ln2_g, v_ln2_b):
    loc = locals()
    weights = {n: loc[n] for n in WEIGHT_ORDER}
    moments_m = {n: loc["m_" + n] for n in WEIGHT_ORDER}
    moments_v = {n: loc["v_" + n] for n in WEIGHT_ORDER}
    return _step(x, positions, weights, moments_m, moments_v, loss_target, distributed=True)
```

```python
import functools
import math

import jax
import jax.numpy as jnp
import numpy as np
from jax import lax
from jax.experimental import pallas as pl
from jax.experimental.pallas import tpu as pltpu

F32 = jnp.float32
MXU_DTYPE = jnp.bfloat16

HEAD_DIM = 64
DILATION_PAIRS = ((128, 1), (512, 4), (2048, 16))
ROPE_THETA = 10000.0
SSM_GROUP_DIM = 16
SSM_STATE = 64
LN_EPS = 1e-5
RMS_EPS = 1e-6
NEG_INF = -1e30
ADAM_LR, ADAM_B1, ADAM_B2, ADAM_EPS, ADAM_WD, ADAM_STEP = 0.001, 0.9, 0.999, 1e-08, 0.01, 10

LANES = 128
SUBLANES = 8
QBLK = 128
VMEM_LIMIT = 56 * 2**20
MESH_AXES = ("x", "y", "c")
NDEV = 8

SMALL_NAMES = ("attn_gain", "ssm_gain", "ssm_a_re", "ssm_a_im", "ssm_log_dt", "ssm_b_re", "ssm_b_im", "ssm_c_re",
               "ssm_c_im", "ssm_d", "b_glu", "b_out", "ln1_g", "ln1_b", "b_ff1", "b_ff2", "ln2_g", "ln2_b")
BIG_NAMES = ("w_in", "w_glu", "w_out", "w_ff1", "w_ff2")
WEIGHT_ORDER = ("w_in", "attn_gain", "ssm_gain", "ssm_a_re", "ssm_a_im", "ssm_log_dt", "ssm_b_re", "ssm_b_im", "ssm_c_re",
                "ssm_c_im", "ssm_d", "w_glu", "b_glu", "w_out", "b_out", "ln1_g", "ln1_b", "w_ff1", "b_ff1", "w_ff2",
                "b_ff2", "ln2_g", "ln2_b")


def _cparams(sem=None):
    return pltpu.CompilerParams(dimension_semantics=sem, vmem_limit_bytes=VMEM_LIMIT)


def _mm(a, b):
    return jnp.dot(a.astype(MXU_DTYPE), b.astype(MXU_DTYPE), preferred_element_type=F32)


def _mm_nt(a, b):
    return lax.dot_general(a.astype(MXU_DTYPE), b.astype(MXU_DTYPE), (((1,), (1,)), ((), ())), preferred_element_type=F32)


def _mm_tn(a, b):
    return lax.dot_general(a.astype(MXU_DTYPE), b.astype(MXU_DTYPE), (((0,), (0,)), ((), ())), preferred_element_type=F32)


def _colsum(x):
    return jnp.sum(x, axis=0, keepdims=True)


def _layer_norm(x, g, b):
    mu = jnp.mean(x, axis=-1, keepdims=True)
    xc = x - mu
    var = jnp.mean(xc * xc, axis=-1, keepdims=True)
    return xc * lax.rsqrt(var + LN_EPS) * g + b


def _layer_norm_bwd(dy, pre, g):
    mu = jnp.mean(pre, axis=-1, keepdims=True)
    xc = pre - mu
    var = jnp.mean(xc * xc, axis=-1, keepdims=True)
    r = lax.rsqrt(var + LN_EPS)
    xhat = xc * r
    dyg = dy * g
    dpre = r * (dyg - jnp.mean(dyg, axis=-1, keepdims=True) - xhat * jnp.mean(dyg * xhat, axis=-1, keepdims=True))
    return dpre, _colsum(dy * xhat), _colsum(dy)


def _rms_norm(x, g):
    return x * lax.rsqrt(jnp.mean(x * x, axis=-1, keepdims=True) + RMS_EPS) * g


def _rms_norm_bwd(dy, x, g):
    r = lax.rsqrt(jnp.mean(x * x, axis=-1, keepdims=True) + RMS_EPS)
    dyg = dy * g
    dx = dyg * r - x * (r * r * r) * jnp.mean(dyg * x, axis=-1, keepdims=True)
    return dx, _colsum(dy * x * r)


_GELU_C = math.sqrt(2.0 / math.pi)


def _gelu(x):
    return 0.5 * x * (1.0 + jnp.tanh(_GELU_C * (x + 0.044715 * (x * x * x))))


def _gelu_grad(x):
    t = jnp.tanh(_GELU_C * (x + 0.044715 * (x * x * x)))
    return 0.5 * (1.0 + t) + 0.5 * x * (1.0 - t * t) * (_GELU_C * (1.0 + 3.0 * 0.044715 * x * x))


def _rows_call(name, fn, rows, consts, out_rows, out_accs, tm):
    rows = [r if isinstance(r, tuple) else (r, r.shape[1], 0) for r in rows]
    m = rows[0][0].shape[0]
    assert m % tm == 0
    nr, nc, no, na = len(rows), len(consts), len(out_rows), len(out_accs)

    def body(*refs):
        rr, cr = refs[:nr], refs[nr:nr + nc]
        orr, ar = refs[nr + nc:nr + nc + no], refs[nr + nc + no:]
        outs, accs = fn([r[...] for r in rr], cr)
        for o, v in zip(orr, outs, strict=True):
            o[...] = v.astype(o.dtype)
        if na:
            first = pl.program_id(0) == 0

            @pl.when(first)
            def _():
                for a, v in zip(ar, accs, strict=True):
                    a[...] = v

            @pl.when(jnp.logical_not(first))
            def _():
                for a, v in zip(ar, accs, strict=True):
                    a[...] += v

    def whole(shape):
        return pl.BlockSpec(shape, lambda i, n=len(shape): (0,) * n)

    in_specs = [pl.BlockSpec((tm, w), lambda i, cb=cb: (i, cb)) for _, w, cb in rows] + [whole(c.shape) for c in consts]
    out_specs = [pl.BlockSpec((tm, w), lambda i: (i, 0)) for w, _ in out_rows] + [whole(s) for s in out_accs]
    out_shape = [jax.ShapeDtypeStruct((m, w), dt) for w, dt in out_rows] + [jax.ShapeDtypeStruct(s, F32) for s in out_accs]
    res = pl.pallas_call(body, grid=(m // tm,), in_specs=in_specs, out_specs=out_specs, out_shape=out_shape, name=name,
                         compiler_params=_cparams(("arbitrary",)))(*[r[0] for r in rows], *consts)
    return res[:no], res[no:]


def _wgrad_call(name, x, dy, split, nblk, jb, blk_shape, tm, prologue=None):
    dys = list(dy) if isinstance(dy, (list, tuple)) else [dy]
    m = x.shape[0]
    kk, nn = blk_shape
    assert m % tm == 0 and nblk % jb == 0 and (len(dys) == 1 or (split == "y" and jb == nblk))
    xw = kk * jb if split == "x" else x.shape[1]
    yws = [d.shape[1] for d in dys] if (split == "x" or len(dys) > 1) else [nn * jb]
    nrow = m // tm

    def body(x_ref, *rest):
        dy_refs, o_ref, acc_ref = rest[:len(dys)], rest[-2], rest[-1]
        i = pl.program_id(1)

        @pl.when(i == 0)
        def _():
            acc_ref[...] = jnp.zeros_like(acc_ref)

        xv = x_ref[...]
        if prologue is not None:
            xv = prologue(xv)
        xv = xv.astype(MXU_DTYPE)
        dv = [r[...].astype(MXU_DTYPE) for r in dy_refs]
        dv = dv[0] if len(dv) == 1 else jnp.concatenate(dv, axis=-1)
        for j in range(jb):
            xa = xv[:, j * kk:(j + 1) * kk] if split == "x" else xv
            da = dv[:, j * nn:(j + 1) * nn] if split == "y" else dv
            acc_ref[j] += _mm_tn(xa, da)

        @pl.when(i == nrow - 1)
        def _():
            o_ref[...] = acc_ref[...].astype(o_ref.dtype)

    in_specs = [pl.BlockSpec((tm, xw), (lambda j, i: (i, j)) if split == "x" else (lambda j, i: (i, 0)))]
    in_specs += [pl.BlockSpec((tm, yw), (lambda j, i: (i, j)) if (split == "y" and len(dys) == 1) else (lambda j, i: (i, 0)))
                 for yw in yws]
    return pl.pallas_call(
        body, grid=(nblk // jb, nrow), in_specs=in_specs,
        out_specs=pl.BlockSpec((jb, kk, nn), lambda j, i: (j, 0, 0)),
        out_shape=jax.ShapeDtypeStruct((nblk, kk, nn), MXU_DTYPE),
        scratch_shapes=[pltpu.VMEM((jb, kk, nn), F32)], name=name,
        compiler_params=_cparams(("arbitrary", "arbitrary")))(x, *dys)


_HBM_SPEC = pl.BlockSpec(memory_space=pltpu.HBM)
_SEM_SPEC = pl.BlockSpec(memory_space=pltpu.SEMAPHORE)
_EFFECT = pltpu.SideEffectType.DATAFLOW_SIDE_EFFECTING


def _my_index():
    return 4 * lax.axis_index("x") + 2 * lax.axis_index("y") + lax.axis_index("c")


def _peer_copies(ins, lands, send_sems, recv_sems, modes):
    x, y, c = lax.axis_index("x"), lax.axis_index("y"), lax.axis_index("c")
    me = 4 * x + 2 * y + c
    pairs = []
    for k in range(NDEV - 1):
        fx, fy, fc = ((k + 1) >> 2) & 1, ((k + 1) >> 1) & 1, (k + 1) & 1
        px, py, pc = (x + fx) % 2, (y + fy) % 2, (c + fc) % 2
        idx = 4 * px + 2 * py + pc
        for a, md in enumerate(modes):
            src = ins[a] if md == "gather" else ins[a].at[idx]
            sem = a * (NDEV - 1) + k
            common = dict(src_ref=src, send_sem=send_sems.at[sem], recv_sem=recv_sems.at[sem], device_id=(px, py, pc),
                          device_id_type=pl.DeviceIdType.MESH)
            pairs.append((pltpu.make_async_remote_copy(dst_ref=lands[a].at[me], **common),
                          pltpu.make_async_remote_copy(dst_ref=lands[a].at[idx], **common)))
    return pairs


def _exchange_start(name, arrays, modes, after=None):
    n = len(arrays)
    extra = [] if after is None else [after]
    me = _my_index()
    lands = []
    for a, md in zip(arrays, modes, strict=True):
        piece = a if md == "gather" else lax.dynamic_index_in_dim(a, me, 0, keepdims=False)
        lands.append(lax.dynamic_update_index_in_dim(lax.empty((NDEV,) + piece.shape, a.dtype), piece, me, 0))

    def body(*refs):
        ins, lnd = refs[:n], refs[n:2 * n]
        send_sems, recv_sems = refs[2 * n + len(extra)], refs[2 * n + len(extra) + 1]
        token = refs[-1]
        for out_copy, _ in _peer_copies(ins, lnd, send_sems, recv_sems, modes):
            out_copy.start()
        token[...] = jnp.zeros_like(token)

    sems = pltpu.SemaphoreType.DMA((n * (NDEV - 1),))
    thru = [pltpu.HBM(a.shape, a.dtype) for a in list(arrays) + lands]
    res = pl.pallas_call(
        body, name=name, out_shape=(sems, sems, *thru, jax.ShapeDtypeStruct((SUBLANES, LANES), F32)),
        in_specs=[_HBM_SPEC] * (2 * n) + [pl.BlockSpec(memory_space=pl.ANY)] * len(extra),
        out_specs=(_SEM_SPEC, _SEM_SPEC, *([_HBM_SPEC] * (2 * n)), pl.BlockSpec(memory_space=pltpu.VMEM)),
        input_output_aliases={i: 2 + i for i in range(2 * n)},
        compiler_params=pltpu.CompilerParams(has_side_effects=_EFFECT),
    )(*[pltpu.with_memory_space_constraint(a, pltpu.HBM) for a in list(arrays) + lands], *extra)
    return (res[0], res[1], res[2:2 + n], res[2 + n:2 + 2 * n]), res[-1]


def _exchange_wait(name, handle, modes, after):
    send_sems, recv_sems, ins_thru, lands_thru = handle
    n = len(ins_thru)

    def body(*refs):
        ins, lnd = refs[:n], refs[n:2 * n]
        for out_copy, arrival in _peer_copies(ins, lnd, refs[2 * n], refs[2 * n + 1], modes):
            out_copy.wait_send()
            arrival.wait_recv()

    thru = [pltpu.HBM(a.shape, a.dtype) for a in list(ins_thru) + list(lands_thru)]
    res = pl.pallas_call(
        body, name=name, out_shape=tuple(thru),
        in_specs=[_HBM_SPEC] * (2 * n) + [_SEM_SPEC, _SEM_SPEC, pl.BlockSpec(memory_space=pl.ANY)],
        out_specs=tuple([_HBM_SPEC] * (2 * n)), input_output_aliases={i: i for i in range(2 * n)},
        compiler_params=pltpu.CompilerParams(has_side_effects=_EFFECT),
    )(*ins_thru, *lands_thru, send_sems, recv_sems, after)
    return res[n:]


def _rope_tables(positions3, inv_freq_row, sign_row):
    b, s, _ = positions3.shape

    def body(pos_ref, f_ref, sg_ref, c_ref, s_ref):
        ang = pos_ref[...].astype(F32) * f_ref[...]
        c_ref[...] = jnp.cos(ang)
        s_ref[...] = jnp.sin(ang) * sg_ref[...]

    row = pl.BlockSpec((1, LANES), lambda i: (0, 0))
    blk = pl.BlockSpec((None, s, LANES), lambda i: (i, 0, 0))
    return pl.pallas_call(
        body, grid=(b,), in_specs=[pl.BlockSpec((None, s, 1), lambda i: (i, 0, 0)), row, row], out_specs=[blk, blk],
        out_shape=[jax.ShapeDtypeStruct((b, s, LANES), F32)] * 2, name="rope_tables",
        compiler_params=_cparams(("arbitrary",)))(positions3, inv_freq_row, sign_row)


def _swap_halves(t):
    lane = lax.broadcasted_iota(jnp.int32, t.shape, 1)
    half = HEAD_DIM // 2
    return jnp.where((lane % HEAD_DIM) < half, pltpu.roll(t, LANES - half, 1), pltpu.roll(t, half, 1))


def _band_masks():
    qi = lax.broadcasted_iota(jnp.int32, (QBLK, 2 * QBLK), 0)
    kj = lax.broadcasted_iota(jnp.int32, (QBLK, 2 * QBLK), 1)
    band = (kj >= qi) & (kj <= qi + QBLK)
    q1 = lax.broadcasted_iota(jnp.int32, (QBLK, QBLK), 0)
    k1 = lax.broadcasted_iota(jnp.int32, (QBLK, QBLK), 1)
    return band, k1 <= q1


def _permute_in(dst, src, d, s):
    n = s // d
    for r in range(d):
        dst[pl.ds(r * n, n), :] = src[pl.ds(r, n, stride=d), :]


def _attn_fwd(proj3, ctab, stab, aw):
    b, s, _ = proj3.shape
    npair = aw // LANES
    scale = HEAD_DIM ** -0.5
    nbr = len(DILATION_PAIRS)

    def body(q_ref, k_ref, v_ref, c_ref, s_ref, o_ref, l_ref, qs, ks, vs, qp, kp, vp, op, lp, ob, lb):
        band, causal = _band_masks()
        cc, ss = c_ref[...], s_ref[...]
        q2 = q_ref[...]
        k2 = k_ref[...]
        q2 = (q2 * cc + _swap_halves(q2) * ss) * scale
        k2 = k2 * cc + _swap_halves(k2) * ss
        v2 = v_ref[...]
        for hh in range(LANES // HEAD_DIM):
            lo = hh * HEAD_DIM
            qs[...] = q2[:, lo:lo + HEAD_DIM]
            ks[...] = k2[:, lo:lo + HEAD_DIM]
            vs[...] = v2[:, lo:lo + HEAD_DIM]
            for br, (window, d) in enumerate(DILATION_PAIRS):
                if d == 1:
                    qr, kr, vr = qs, ks, vs
                else:
                    _permute_in(qp, qs, d, s)
                    _permute_in(kp, ks, d, s)
                    _permute_in(vp, vs, d, s)
                    qr, kr, vr = qp, kp, vp
                nb = (s // d) // QBLK
                for i in range(s // QBLK):
                    r0 = i * QBLK
                    first = (i % nb) == 0
                    k0, nk = (r0, QBLK) if first else (r0 - QBLK, 2 * QBLK)
                    qb = qr[pl.ds(r0, QBLK), :]
                    kb = kr[pl.ds(k0, nk), :]
                    vb = vr[pl.ds(k0, nk), :]
                    sc = jnp.where(causal if first else band, _mm_nt(qb, kb), NEG_INF)
                    mx = jnp.max(sc, axis=-1, keepdims=True)
                    p = jnp.exp(sc - mx)
                    den = jnp.sum(p, axis=-1, keepdims=True)
                    op[pl.ds(r0, QBLK), :] = _mm(p, vb) / den
                    lp[pl.ds(r0, QBLK), :] = mx + jnp.log(den)
                if d == 1:
                    ob[br] = op[...]
                    lb[br] = lp[...]
                else:
                    n = s // d
                    for r in range(d):
                        ob[br, pl.ds(r, n, stride=d), :] = op[pl.ds(r * n, n), :]
                        lb[br, pl.ds(r, n, stride=d), :] = lp[pl.ds(r * n, n), :]
            ls = [lb[br] for br in range(nbr)]
            mx = functools.reduce(jnp.maximum, ls)
            ws = [jnp.exp(l - mx) for l in ls]
            tot = functools.reduce(lambda a, b_: a + b_, ws)
            out = functools.reduce(lambda a, b_: a + b_, [(w / tot) * ob[br] for br, w in enumerate(ws)])
            o_ref[:, lo:lo + HEAD_DIM] = out
            l_ref[:, lo:lo + HEAD_DIM] = jnp.broadcast_to(mx + jnp.log(tot), (s, HEAD_DIM))

    def col(off):
        return pl.BlockSpec((None, s, LANES), lambda bi, hp, off=off: (bi, 0, off + hp))

    tab = pl.BlockSpec((None, s, LANES), lambda bi, hp: (bi, 0, 0))
    hd = pltpu.VMEM((s, HEAD_DIM), F32)
    one = pltpu.VMEM((s, 1), F32)
    return pl.pallas_call(
        body, grid=(b, npair), in_specs=[col(0), col(npair), col(2 * npair), tab, tab], out_specs=[col(0), col(0)],
        out_shape=[jax.ShapeDtypeStruct((b, s, aw), F32)] * 2,
        scratch_shapes=[hd] * 7 + [one, pltpu.VMEM((nbr, s, HEAD_DIM), F32), pltpu.VMEM((nbr, s, 1), F32)],
        name="attn_fwd", compiler_params=_cparams(("arbitrary", "arbitrary")))(proj3, proj3, proj3, ctab, stab)


def _attn_bwd(proj3, ctab, stab, dout3, out3, lse3, aw):
    b, s, _ = proj3.shape
    npair = aw // LANES
    scale = HEAD_DIM ** -0.5

    def body(q_ref, k_ref, v_ref, c_ref, s_ref, do_ref, o_ref, l_ref, dq_ref, dk_ref, dv_ref,
             qs, ks, vs, dos, qp, kp, vp, dop, dqp, dkp, dvp, dqn, dkn, dvn, ls, dl, lp, dlp, dq2, dk2):
        band, causal = _band_masks()
        cc, ss = c_ref[...], s_ref[...]
        q2 = q_ref[...]
        k2 = k_ref[...]
        q2 = (q2 * cc + _swap_halves(q2) * ss) * scale
        k2 = k2 * cc + _swap_halves(k2) * ss
        v2 = v_ref[...]
        do2 = do_ref[...]
        dd2 = do2 * o_ref[...]
        l2 = l_ref[...]
        for hh in range(LANES // HEAD_DIM):
            lo = hh * HEAD_DIM
            qs[...] = q2[:, lo:lo + HEAD_DIM]
            ks[...] = k2[:, lo:lo + HEAD_DIM]
            vs[...] = v2[:, lo:lo + HEAD_DIM]
            dos[...] = do2[:, lo:lo + HEAD_DIM]
            ls[...] = l2[:, lo:lo + 1]
            dl[...] = jnp.sum(dd2[:, lo:lo + HEAD_DIM], axis=-1, keepdims=True)
            dqn[...] = jnp.zeros_like(dqn)
            dkn[...] = jnp.zeros_like(dkn)
            dvn[...] = jnp.zeros_like(dvn)
            for window, d in DILATION_PAIRS:
                if d == 1:
                    qr, kr, vr, dor, lr, dlr = qs, ks, vs, dos, ls, dl
                else:
                    for dst, src in ((qp, qs), (kp, ks), (vp, vs), (dop, dos), (lp, ls), (dlp, dl)):
                        _permute_in(dst, src, d, s)
                    qr, kr, vr, dor, lr, dlr = qp, kp, vp, dop, lp, dlp
                dkp[...] = jnp.zeros_like(dkp)
                dvp[...] = jnp.zeros_like(dvp)
                nb = (s // d) // QBLK
                for i in range(s // QBLK):
                    r0 = i * QBLK
                    first = (i % nb) == 0
                    k0, nk = (r0, QBLK) if first else (r0 - QBLK, 2 * QBLK)
                    qb = qr[pl.ds(r0, QBLK), :]
                    dob = dor[pl.ds(r0, QBLK), :]
                    kb = kr[pl.ds(k0, nk), :]
                    vb = vr[pl.ds(k0, nk), :]
                    sc = jnp.where(causal if first else band, _mm_nt(qb, kb), NEG_INF)
                    p = jnp.exp(sc - lr[pl.ds(r0, QBLK), :])
                    ds_ = p * (_mm_nt(dob, vb) - dlr[pl.ds(r0, QBLK), :])
                    dqp[pl.ds(r0, QBLK), :] = _mm(ds_, kb)
                    dkp[pl.ds(k0, nk), :] += _mm_tn(ds_, qb)
                    dvp[pl.ds(k0, nk), :] += _mm_tn(p, dob)
                n = s // d
                for r in range(d):
                    if d == 1:
                        src, dst = pl.ds(0, s), pl.ds(0, s)
                    else:
                        src, dst = pl.ds(r * n, n), pl.ds(r, n, stride=d)
                    dqn[dst, :] += dqp[src, :]
                    dkn[dst, :] += dkp[src, :]
                    dvn[dst, :] += dvp[src, :]
            dq2[:, lo:lo + HEAD_DIM] = dqn[...] * scale
            dk2[:, lo:lo + HEAD_DIM] = dkn[...]
            dv_ref[:, lo:lo + HEAD_DIM] = dvn[...]
        g = dq2[...]
        dq_ref[...] = g * cc + _swap_halves(g * ss)
        g = dk2[...]
        dk_ref[...] = g * cc + _swap_halves(g * ss)

    def col(off):
        return pl.BlockSpec((None, s, LANES), lambda bi, hp, off=off: (bi, 0, off + hp))

    tab = pl.BlockSpec((None, s, LANES), lambda bi, hp: (bi, 0, 0))
    hd = pltpu.VMEM((s, HEAD_DIM), F32)
    one = pltpu.VMEM((s, 1), F32)
    pair = pltpu.VMEM((s, LANES), F32)
    return pl.pallas_call(
        body, grid=(b, npair), in_specs=[col(0), col(npair), col(2 * npair), tab, tab, col(0), col(0), col(0)],
        out_specs=[col(0)] * 3, out_shape=[jax.ShapeDtypeStruct((b, s, aw), F32)] * 3,
        scratch_shapes=[hd] * 14 + [one] * 4 + [pair] * 2,
        name="attn_bwd", compiler_params=_cparams(("arbitrary", "arbitrary")))(
            proj3, proj3, proj3, ctab, stab, dout3, out3, lse3)


def _s5_discretise(a_re, a_im, log_dt, bt_re, bt_im):
    dt = jnp.exp(log_dt)
    mag = jnp.exp(a_re * dt)
    ang = a_im * dt
    lb_re = mag * jnp.cos(ang)
    lb_im = mag * jnp.sin(ang)
    den = a_re * a_re + a_im * a_im
    nr = lb_re - 1.0
    ni = lb_im
    cr = (nr * a_re + ni * a_im) / den
    ci = (ni * a_re - nr * a_im) / den
    return lb_re, lb_im, cr * bt_re - ci * bt_im, cr * bt_im + ci * bt_re


def _s5_params(a_re, a_im, log_dt, bt_re, bt_im):
    def body(ar, ai, ld, br, bi, o1, o2, o3, o4):
        r = _s5_discretise(ar[...], ai[...], ld[...], br[...], bi[...])
        for o, v in zip((o1, o2, o3, o4), r, strict=True):
            o[...] = v

    sd = jax.ShapeDtypeStruct
    return pl.pallas_call(body, out_shape=[sd(a_re.shape, F32)] * 2 + [sd(bt_re.shape, F32)] * 2, name="s5_params")(
        a_re, a_im, log_dt, bt_re, bt_im)


def _s5_params_bwd(a_re, a_im, log_dt, bt_re, bt_im, cts):
    def body(ar, ai, ld, br, bi, c1, c2, c3, c4, o1, o2, o3, o4, o5):
        _, vjp = jax.vjp(_s5_discretise, ar[...], ai[...], ld[...], br[...], bi[...])
        r = vjp((c1[...], c2[...], c3[...], c4[...]))
        for o, v in zip((o1, o2, o3, o4, o5), r, strict=True):
            o[...] = v

    sd = jax.ShapeDtypeStruct
    return pl.pallas_call(
        body, out_shape=[sd(a_re.shape, F32)] * 2 + [sd(log_dt.shape, F32)] + [sd(bt_re.shape, F32)] * 2, name="s5_params_bwd")(
            a_re, a_im, log_dt, bt_re, bt_im, *cts)


S5_TC = 128


def _s5_fwd(proj3, ucol, lam_re, lam_im, bre, bim, cre, cim, dvec, sw):
    b, s, _ = proj3.shape
    nt = lam_re.shape[0]
    ncb = sw // LANES
    tpc = nt // ncb
    tc = S5_TC

    def body(u_ref, lr_ref, li_ref, bre_ref, bim_ref, cre_ref, cim_ref, d_ref, y_ref, xr_ref, xi_ref, sr, si):
        @pl.when(pl.program_id(0) == 0)
        def _():
            sr[...] = jnp.zeros_like(sr)
            si[...] = jnp.zeros_like(si)

        u_all = u_ref[...].reshape(b * tc, sw)
        for cb in range(ncb):
            ucb = u_all[:, cb * LANES:(cb + 1) * LANES].astype(MXU_DTYPE)
            for t in range(cb * tpc, (cb + 1) * tpc):
                bur, bui = _mm(ucb, bre_ref[t]), _mm(ucb, bim_ref[t])
                for bi in range(b):
                    xr_ref[bi, pl.ds(t, tc, stride=nt), :] = bur[bi * tc:(bi + 1) * tc]
                    xi_ref[bi, pl.ds(t, tc, stride=nt), :] = bui[bi * tc:(bi + 1) * tc]
        lr, li = lr_ref[...], li_ref[...]

        def step(j, carry):
            off = pl.multiple_of(j * nt, nt)
            new = []
            for bi in range(b):
                pr, pi = carry[2 * bi], carry[2 * bi + 1]
                nr = lr * pr - li * pi + xr_ref[bi, pl.ds(off, nt), :]
                ni = lr * pi + li * pr + xi_ref[bi, pl.ds(off, nt), :]
                xr_ref[bi, pl.ds(off, nt), :] = nr
                xi_ref[bi, pl.ds(off, nt), :] = ni
                new += [nr, ni]
            return tuple(new)

        init = tuple(v for bi in range(b) for v in (sr[bi], si[bi]))
        fin = lax.fori_loop(0, tc, step, init, unroll=4)
        for bi in range(b):
            sr[bi] = fin[2 * bi]
            si[bi] = fin[2 * bi + 1]
        for cb in range(ncb):
            cols = slice(cb * LANES, (cb + 1) * LANES)
            acc = d_ref[:, cols] * u_all[:, cols]
            for t in range(cb * tpc, (cb + 1) * tpc):
                rows = pl.ds(t, tc, stride=nt)
                xr_all = jnp.concatenate([xr_ref[bi, rows, :].astype(MXU_DTYPE) for bi in range(b)], axis=0)
                xi_all = jnp.concatenate([xi_ref[bi, rows, :].astype(MXU_DTYPE) for bi in range(b)], axis=0)
                acc = acc + (_mm(xr_all, cre_ref[t]) - _mm(xi_all, cim_ref[t]))
            for bi in range(b):
                y_ref[bi, :, cols] = acc[bi * tc:(bi + 1) * tc]

    def whole(a):
        return pl.BlockSpec(a.shape, lambda c, n=a.ndim: (0,) * n)

    xblk = pl.BlockSpec((b, tc * nt, LANES), lambda c: (0, c, 0))
    return pl.pallas_call(
        body, grid=(s // tc,),
        in_specs=[pl.BlockSpec((b, tc, sw), lambda c: (0, c, ucol))] + [whole(a) for a in (lam_re, lam_im, bre, bim, cre, cim, dvec)],
        out_specs=[pl.BlockSpec((b, tc, sw), lambda c: (0, c, 0)), xblk, xblk],
        out_shape=[jax.ShapeDtypeStruct((b, s, sw), F32)] + [jax.ShapeDtypeStruct((b, s * nt, LANES), F32)] * 2,
        scratch_shapes=[pltpu.VMEM((b, nt, LANES), F32)] * 2, name="s5_fwd",
        compiler_params=_cparams(("arbitrary",)))(proj3, lam_re, lam_im, bre, bim, cre, cim, dvec)


def _s5_bwd(proj3, ucol, dy3, xr3, xi3, lam_re, lam_im, bre, bim, cre, cim, dvec, sw):
    b, s, _ = proj3.shape
    nt = lam_re.shape[0]
    ncb = sw // LANES
    tpc = nt // ncb
    tc = S5_TC
    nchunk = s // tc

    def body(u_ref, dy_ref, xr_ref, xi_ref, pr_ref, pi_ref, lr_ref, li_ref, bre_ref, bim_ref, cre_ref, cim_ref, d_ref,
             du_ref, dbre, dbim, dcre, dcim, dlr, dli, dd, gr, gi, sr, si):
        step_id = pl.program_id(0)

        @pl.when(step_id == 0)
        def _():
            for r in (sr, si, dbre, dbim, dcre, dcim, dlr, dli, dd):
                r[...] = jnp.zeros_like(r)

        u_all = u_ref[...].reshape(b * tc, sw)
        dy_all = dy_ref[...].reshape(b * tc, sw)
        for cb in range(ncb):
            dycb = dy_all[:, cb * LANES:(cb + 1) * LANES].astype(MXU_DTYPE)
            for t in range(cb * tpc, (cb + 1) * tpc):
                dxr, dxi = _mm_nt(dycb, cre_ref[t]), _mm_nt(dycb, cim_ref[t])
                for bi in range(b):
                    gr[bi, pl.ds(t, tc, stride=nt), :] = dxr[bi * tc:(bi + 1) * tc]
                    gi[bi, pl.ds(t, tc, stride=nt), :] = -dxi[bi * tc:(bi + 1) * tc]
        lr, li = lr_ref[...], li_ref[...]

        def step(jj, carry):
            off = pl.multiple_of((tc - 1 - jj) * nt, nt)
            new = []
            for bi in range(b):
                nr_, ni_ = carry[2 * bi], carry[2 * bi + 1]
                vr = gr[bi, pl.ds(off, nt), :] + lr * nr_ + li * ni_
                vi = gi[bi, pl.ds(off, nt), :] + lr * ni_ - li * nr_
                gr[bi, pl.ds(off, nt), :] = vr
                gi[bi, pl.ds(off, nt), :] = vi
                new += [vr, vi]
            return tuple(new)

        init = tuple(v for bi in range(b) for v in (sr[bi], si[bi]))
        fin = lax.fori_loop(0, tc, step, init, unroll=4)
        for bi in range(b):
            sr[bi] = fin[2 * bi]
            si[bi] = fin[2 * bi + 1]

        has_prev = (step_id != nchunk - 1).astype(F32)
        rest = (tc - 1) * nt
        alr = jnp.zeros((nt, LANES), F32)
        ali = jnp.zeros((nt, LANES), F32)
        for bi in range(b):
            s0r, s0i = gr[bi, pl.ds(0, nt), :], gi[bi, pl.ds(0, nt), :]
            x0r, x0i = pr_ref[bi] * has_prev, pi_ref[bi] * has_prev
            alr += s0r * x0r + s0i * x0i
            ali += s0i * x0r - s0r * x0i
            s1r, s1i = gr[bi, pl.ds(nt, rest), :], gi[bi, pl.ds(nt, rest), :]
            x1r, x1i = xr_ref[bi, pl.ds(0, rest), :], xi_ref[bi, pl.ds(0, rest), :]
            alr += jnp.sum((s1r * x1r + s1i * x1i).reshape(tc - 1, nt, LANES), axis=0)
            ali += jnp.sum((s1i * x1r - s1r * x1i).reshape(tc - 1, nt, LANES), axis=0)
        dlr[...] += alr
        dli[...] += ali

        for cb in range(ncb):
            cols = slice(cb * LANES, (cb + 1) * LANES)
            ucb32, dycb32 = u_all[:, cols], dy_all[:, cols]
            ucb, dycb = ucb32.astype(MXU_DTYPE), dycb32.astype(MXU_DTYPE)
            acc = d_ref[:, cols] * dycb32
            for t in range(cb * tpc, (cb + 1) * tpc):
                rows = pl.ds(t, tc, stride=nt)

                def stacked(ref):
                    return jnp.concatenate([ref[bi, rows, :].astype(MXU_DTYPE) for bi in range(b)], axis=0)

                vr, vi = stacked(gr), stacked(gi)
                acc = acc + (_mm_nt(vr, bre_ref[t]) + _mm_nt(vi, bim_ref[t]))
                dbre[t] += _mm_tn(ucb, vr)
                dbim[t] += _mm_tn(ucb, vi)
                dcre[t] += _mm_tn(stacked(xr_ref), dycb)
                dcim[t] -= _mm_tn(stacked(xi_ref), dycb)
            for bi in range(b):
                du_ref[bi, :, cols] = acc[bi * tc:(bi + 1) * tc]
            dd[:, cols] += _colsum(dycb32 * ucb32)

    def whole(a):
        return pl.BlockSpec(a.shape, lambda c, n=len(a.shape): (0,) * n)

    def rev(c):
        return nchunk - 1 - c

    xblk = pl.BlockSpec((b, tc * nt, LANES), lambda c: (0, rev(c), 0))
    prev = pl.BlockSpec((b, nt, LANES), lambda c: (0, jnp.maximum(rev(c) * tc - 1, 0), 0))
    sd = jax.ShapeDtypeStruct
    blk = sd(bre.shape, F32)
    acc_shapes = [blk, blk, sd(cre.shape, F32), sd(cre.shape, F32), sd(lam_re.shape, F32), sd(lam_re.shape, F32), sd((1, sw), F32)]
    return pl.pallas_call(
        body, grid=(nchunk,),
        in_specs=[pl.BlockSpec((b, tc, sw), lambda c: (0, rev(c), ucol)), pl.BlockSpec((b, tc, sw), lambda c: (0, rev(c), 0)),
                  xblk, xblk, prev, prev] + [whole(a) for a in (lam_re, lam_im, bre, bim, cre, cim, dvec)],
        out_specs=[pl.BlockSpec((b, tc, sw), lambda c: (0, rev(c), 0))] + [whole(a) for a in acc_shapes],
        out_shape=[sd((b, s, sw), F32)] + acc_shapes,
        scratch_shapes=[pltpu.VMEM((b, tc * nt, LANES), F32)] * 2 + [pltpu.VMEM((b, nt, LANES), F32)] * 2, name="s5_bwd",
        compiler_params=_cparams(("arbitrary",)))(proj3, dy3, xr3, xi3, xr3, xi3, lam_re, lam_im, bre, bim, cre, cim, dvec)


def _s5_blocks(bb, cc_, ncb):
    g, n, p = bb.shape
    mask = jnp.asarray(_s5_tile_mask(g, p, ncb))
    nt, gpt, gpc = mask.shape
    bbt, cct = bb.reshape(nt, gpt, n, p), cc_.reshape(nt, gpt, n, p)
    spread = mask[:, :, :, None, None]
    bblk = (bbt[:, :, None] * spread).transpose(0, 2, 3, 1, 4).reshape(nt, gpc * n, gpt * p)
    cblk = (cct[:, :, None] * spread).transpose(0, 1, 4, 2, 3).reshape(nt, gpt * p, gpc * n)
    return bblk, cblk


def _s5_tile_mask(g, p, ncb):
    nt, gpt, gpc = g * p // LANES, LANES // p, g // ncb
    mask = np.zeros((nt, gpt, gpc), np.float32)
    for t in range(nt):
        for gl in range(gpt):
            mask[t, gl, (t * gpt + gl) % gpc] = 1.0
    return mask


def _s5_unblock(dbblk, dcblk, g, n, p, ncb):
    mask = jnp.asarray(_s5_tile_mask(g, p, ncb))
    nt, gpt, gpc = mask.shape
    db = jnp.sum(dbblk.reshape(nt, gpc, n, gpt, p) * mask.transpose(0, 2, 1)[:, :, None, :, None], axis=1)
    dc = jnp.sum(dcblk.reshape(nt, gpt, p, gpc, n) * mask[:, :, None, :, None], axis=3)
    return db.transpose(0, 2, 1, 3).reshape(g, n, p), dc.transpose(0, 1, 3, 2).reshape(g, n, p)


def _adamw_math(w, g, m, v):
    m = ADAM_B1 * m + (1.0 - ADAM_B1) * g
    v = ADAM_B2 * v + (1.0 - ADAM_B2) * (g * g)
    m_hat = m / (1.0 - ADAM_B1 ** ADAM_STEP)
    v_hat = v / (1.0 - ADAM_B2 ** ADAM_STEP)
    delta = -ADAM_LR * (m_hat / (jnp.sqrt(v_hat) + ADAM_EPS) + ADAM_WD * w)
    return delta, m, v


def _adamw_layer(name, parts, w, m, v, layer, prev, tr):
    nparts, r, c = parts.shape
    assert r % tr == 0
    if prev is None:
        prev = [lax.empty(w.shape, F32) for _ in range(4)]

    def body(p_ref, w_ref, m_ref, v_ref, *rest):
        g_out, d_out, m_out, v_out = rest[4:]
        g = p_ref[0].astype(F32)
        for k in range(1, nparts):
            g = g + p_ref[k].astype(F32)
        delta, mn, vn = _adamw_math(w_ref[...], g, m_ref[...], v_ref[...])
        g_out[...] = g
        d_out[...] = delta
        m_out[...] = mn
        v_out[...] = vn

    blk = pl.BlockSpec((None, tr, c), lambda i: (layer, i, 0))
    kept = pl.BlockSpec(memory_space=pl.ANY)
    return pl.pallas_call(
        body, grid=(r // tr,), in_specs=[pl.BlockSpec((nparts, tr, c), lambda i: (0, i, 0)), blk, blk, blk] + [kept] * 4,
        out_specs=[blk] * 4, out_shape=[jax.ShapeDtypeStruct(w.shape, F32)] * 4, name=name,
        input_output_aliases={4 + k: k for k in range(4)},
        compiler_params=_cparams(("arbitrary",)))(parts, w, m, v, *prev)


def _pack_small(pieces):
    flat = jnp.concatenate([p.reshape(-1) for p in pieces])
    n = flat.shape[0]
    unit = SUBLANES * LANES
    padded = -(-n // unit) * unit
    return jnp.pad(flat, (0, padded - n)).reshape(padded // LANES, LANES)


def _pack_small_layers(pieces):
    nl = pieces[0].shape[0]
    flat = jnp.concatenate([p.reshape(nl, -1) for p in pieces], axis=1)
    n = flat.shape[1]
    unit = SUBLANES * LANES
    padded = -(-n // unit) * unit
    return jnp.pad(flat, ((0, 0), (0, padded - n))).reshape(nl, padded // LANES, LANES)


def _step(x, positions, weights, moments_m, moments_v, loss_target, distributed):
    f32 = F32
    bsz, seq, dm = x.shape
    depth = weights["w_in"].shape[0]
    aw = weights["attn_gain"].shape[1]
    sw = weights["ssm_gain"].shape[1]
    dff = weights["b_ff1"].shape[1]
    ng, npst = weights["ssm_a_re"].shape[1:]
    gdim = weights["ssm_d"].shape[2]
    in_w = 3 * aw + sw
    t_rows = bsz * seq
    alpha = (2.0 * depth) ** 0.25
    ncb = sw // LANES
    nt = ng * npst // LANES
    ndev = NDEV if distributed else 1
    tm = 256

    gather_modes = ["gather"] * len(BIG_NAMES)
    no_token = jnp.zeros((SUBLANES, LANES), f32)

    def shards(l, names):
        return [weights[n][l].astype(MXU_DTYPE) for n in names]

    if distributed:
        first, _ = _exchange_start("weights_start_l0_in", shards(0, BIG_NAMES[:1]), gather_modes[:1])
        (g_in0,) = _exchange_wait("weights_wait_l0_in", first, gather_modes[:1], positions)
        rest0, tok_rest0 = _exchange_start("weights_start_l0_rest", shards(0, BIG_NAMES[1:]), gather_modes[1:], after=g_in0)

    half = HEAD_DIM // 2
    inv_freq = ROPE_THETA ** (-jnp.arange(half, dtype=f32) * 2.0 / HEAD_DIM)
    reps = LANES // half
    inv_row = jnp.tile(inv_freq, reps)[None, :]
    sign_row = jnp.tile(jnp.concatenate([-jnp.ones((half,), f32), jnp.ones((half,), f32)]), LANES // HEAD_DIM)[None, :]
    ctab, stab = _rope_tables(positions[..., None], inv_row, sign_row)

    def row(v):
        return v.reshape(1, -1)

    h = x.reshape(t_rows, dm)
    saved = []
    for l in range(depth):
        tok_in = tok_mix = no_token
        if not distributed:
            g_in, g_glu, g_out, g_ff1, g_ff2 = [weights[n][l].astype(MXU_DTYPE)[None] for n in BIG_NAMES]
        elif l == 0:
            g_in, tok_in = g_in0, tok_rest0
        else:
            g_in, g_glu, g_out, g_ff1, g_ff2 = _exchange_wait(f"weights_wait_l{l}", next_gather, gather_modes, h)
            if l + 1 < depth:
                next_gather, tok_in = _exchange_start(f"weights_start_l{l + 1}", shards(l + 1, BIG_NAMES), gather_modes, after=g_in)
        w_in_l = g_in

        def in_proj(rv, cr):
            return [jnp.concatenate([_mm(rv[0], cr[0][j]) for j in range(ndev)], axis=-1)], []

        (proj,), _ = _rows_call("in_proj", in_proj, [h], [w_in_l, tok_in], [(in_w, f32)], [], tm)
        proj3 = proj.reshape(bsz, seq, in_w)
        attn3, lse3 = _attn_fwd(proj3, ctab, stab, aw)
        if distributed and l == 0:
            g_glu, g_out, g_ff1, g_ff2 = _exchange_wait("weights_wait_l0_rest", rest0, gather_modes[1:], attn3)
            if depth > 1:
                next_gather, tok_mix = _exchange_start("weights_start_l1", shards(1, BIG_NAMES), gather_modes, after=g_glu)
        w_glu_l = g_glu.reshape(sw, sw)
        w_out_l = g_out.reshape(dm, dm)
        w_ff1_l = g_ff1
        w_ff2_l = g_ff2.reshape(dff, dm)

        a_re, a_im = weights["ssm_a_re"][l][:, None, :], weights["ssm_a_im"][l][:, None, :]
        log_dt = weights["ssm_log_dt"][l][:, None, None]
        bt_re = weights["ssm_b_re"][l].transpose(0, 2, 1)
        bt_im = weights["ssm_b_im"][l].transpose(0, 2, 1)
        lb_re, lb_im, bb_re, bb_im = _s5_params(a_re, a_im, log_dt, bt_re, bt_im)
        lam_re, lam_im = lb_re.reshape(nt, LANES), lb_im.reshape(nt, LANES)
        bre, cre = _s5_blocks(bb_re, weights["ssm_c_re"][l], ncb)
        bim, cim = _s5_blocks(bb_im, weights["ssm_c_im"][l], ncb)
        s5c = [a_.astype(MXU_DTYPE) for a_ in (bre, bim, cre, cim)]
        dvec = row(weights["ssm_d"][l])
        ypre3, xr3, xi3 = _s5_fwd(proj3, 3 * aw // sw, lam_re, lam_im, *s5c, dvec, sw)
        attn, ypre = attn3.reshape(t_rows, aw), ypre3.reshape(t_rows, sw)

        b_glu, ga, gs = row(weights["b_glu"][l]), row(weights["attn_gain"][l]), row(weights["ssm_gain"][l])

        def mix(rv, cr):
            at, yp = rv
            g = _gelu(yp)
            ssm = g * jax.nn.sigmoid(_mm(g, cr[0][...]) + cr[1][...])
            return [jnp.concatenate([_rms_norm(at, cr[2][...]), _rms_norm(ssm, cr[3][...])], axis=-1)], []

        (mixed,), _ = _rows_call("mix", mix, [attn, ypre], [w_glu_l, b_glu, ga, gs, tok_mix], [(dm, MXU_DTYPE)], [], tm)

        b_out, g1, b1 = row(weights["b_out"][l]), row(weights["ln1_g"][l]), row(weights["ln1_b"][l])

        def out_proj(rv, cr):
            pre = alpha * rv[1] + _mm(rv[0], cr[0][...]) + cr[1][...]
            return [pre, _layer_norm(pre, cr[2][...], cr[3][...])], []

        (pre1, h1), _ = _rows_call("out_proj", out_proj, [mixed, h], [w_out_l, b_out, g1, b1], [(dm, f32), (dm, f32)], [], tm)

        b_ff1 = row(weights["b_ff1"][l])

        def ff1(rv, cr):
            return [jnp.concatenate([_mm(rv[0], cr[0][j]) for j in range(ndev)], axis=-1) + cr[1][...]], []

        (act,), _ = _rows_call("ff1", ff1, [h1], [w_ff1_l, b_ff1], [(dff, f32)], [], tm)

        b_ff2, g2, b2 = row(weights["b_ff2"][l]), row(weights["ln2_g"][l]), row(weights["ln2_b"][l])

        def ff2(rv, cr):
            r = jnp.square(jnp.maximum(rv[0], 0.0))
            pre = alpha * rv[1] + _mm(r, cr[0][...]) + cr[1][...]
            return [pre, _layer_norm(pre, cr[2][...], cr[3][...])], []

        (pre2, h2), _ = _rows_call("ff2", ff2, [act, h1], [w_ff2_l, b_ff2, g2, b2], [(dm, f32), (dm, f32)], [], tm)

        saved.append(dict(h=h, proj3=proj3, attn=attn, lse3=lse3, attn3=attn3, ypre=ypre, xr3=xr3, xi3=xi3, mixed=mixed, pre1=pre1,
                          h1=h1, act=act, pre2=pre2, w_in=w_in_l, w_glu=w_glu_l, w_out=w_out_l, w_ff1=w_ff1_l, w_ff2=w_ff2_l,
                          lam=(lam_re, lam_im), s5c=s5c, dvec=dvec, b_glu=b_glu, ga=ga, gs=gs, g1=g1, g2=g2,
                          s5in=(a_re, a_im, log_dt, bt_re, bt_im)))
        h = h2

    g2_last = saved[-1]["g2"]

    def loss_fn(rv, cr):
        y, tgt, pre = rv
        err = y - tgt
        part = 0.5 * jnp.sum(jnp.mean(err * err, axis=-1, keepdims=True), axis=0, keepdims=True)
        dpre, dg, db = _layer_norm_bwd(err * (1.0 / dm), pre, cr[0][...])
        return [dpre], [jnp.broadcast_to(part, (1, LANES)), dg, db, _colsum(dpre)]

    (dpre2,), (loss_acc, dg2, db2, dbff2) = _rows_call(
        "loss", loss_fn, [h, loss_target.reshape(t_rows, dm), saved[-1]["pre2"]], [g2_last], [(dm, f32)],
        [(1, LANES), (1, dm), (1, dm), (1, dm)], tm)
    loss = loss_acc[0, 0]
    if distributed:
        loss = lax.psum(loss, MESH_AXES)

    big_parts = {n: [None] * depth for n in BIG_NAMES}
    small_parts = [None] * depth
    grad_handles = [None] * depth
    grad_modes = ["scatter"] * len(BIG_NAMES) + ["gather"]
    grad_x = None
    for l in reversed(range(depth)):
        sv = saved[l]

        def ff2_bwd(rv, cr):
            da = _mm_nt(rv[0], cr[0][...]) * (2.0 * jnp.maximum(rv[1], 0.0))
            return [da], [_colsum(da)]

        (dact,), (dbff1,) = _rows_call("ff2_bwd", ff2_bwd, [dpre2, sv["act"]], [sv["w_ff2"]], [(dff, MXU_DTYPE)], [(1, dff)], tm)
        big_parts["w_ff2"][l] = _wgrad_call("wgrad_ff2", sv["act"], dpre2, "x", ndev, ndev // 2 if ndev > 1 else 1,
                                            (dff // ndev, dm), tm, prologue=lambda a: jnp.square(jnp.maximum(a, 0.0)))
        big_parts["w_ff1"][l] = _wgrad_call("wgrad_ff1", sv["h1"], dact, "y", ndev, ndev // 2 if ndev > 1 else 1,
                                            (dm, dff // ndev), tm)

        def ff1_bwd(rv, cr):
            dacc = alpha * rv[1]
            wpb = dff // ndev
            for j in range(ndev):
                dacc = dacc + _mm_nt(rv[0][:, j * wpb:(j + 1) * wpb], cr[0][j])
            dpre, dg, db = _layer_norm_bwd(dacc, rv[2], cr[1][...])
            return [dpre], [dg, db, _colsum(dpre)]

        (dpre1,), (dg1, db1, dbout) = _rows_call("ff1_bwd", ff1_bwd, [dact, dpre2, sv["pre1"]], [sv["w_ff1"], sv["g1"]],
                                                 [(dm, f32)], [(1, dm)] * 3, tm)
        big_parts["w_out"][l] = _wgrad_call("wgrad_out", sv["mixed"], dpre1, "x", ndev, ndev, (dm // ndev, dm), tm)

        def mix_bwd(rv, cr):
            dp, at, yp = rv
            w_out_r, w_glu_r, bg, ga_, gs_ = cr
            dmixed = _mm_nt(dp, w_out_r[...])
            g = _gelu(yp)
            sig = jax.nn.sigmoid(_mm(g, w_glu_r[...]) + bg[...])
            ssm = g * sig
            dat, dga = _rms_norm_bwd(dmixed[:, :aw], at, ga_[...])
            dssm, dgs = _rms_norm_bwd(dmixed[:, aw:], ssm, gs_[...])
            dz = dssm * g * sig * (1.0 - sig)
            dg = dssm * sig + _mm_nt(dz, w_glu_r[...])
            return [dat, dg * _gelu_grad(yp), dz], [dga, dgs, _colsum(dz)]

        (dattn, dypre, dz), (dga, dgs, dbglu) = _rows_call(
            "mix_bwd", mix_bwd, [dpre1, sv["attn"], sv["ypre"]], [sv["w_out"], sv["w_glu"], sv["b_glu"], sv["ga"], sv["gs"]],
            [(aw, f32), (sw, f32), (sw, MXU_DTYPE)], [(1, aw), (1, sw), (1, sw)], tm)
        big_parts["w_glu"][l] = _wgrad_call("wgrad_glu", sv["ypre"], dz, "x", 1, 1, (sw, sw), tm, prologue=_gelu).reshape(
            ndev, sw // ndev, sw)

        du3, dbre, dbim, dcre, dcim, dlr, dli, dd = _s5_bwd(
            sv["proj3"], 3 * aw // sw, dypre.reshape(bsz, seq, sw), sv["xr3"], sv["xi3"], *sv["lam"], *sv["s5c"], sv["dvec"], sw)
        dbb_re, dc_re = _s5_unblock(dbre, dcre, ng, gdim, npst, ncb)
        dbb_im, dc_im = _s5_unblock(dbim, dcim, ng, gdim, npst, ncb)
        da_re, da_im, dldt, dbt_re, dbt_im = _s5_params_bwd(
            *sv["s5in"], (dlr.reshape(ng, 1, npst), dli.reshape(ng, 1, npst), dbb_re, dbb_im))

        dq3, dk3, dv3 = _attn_bwd(sv["proj3"], ctab, stab, dattn.reshape(bsz, seq, aw), sv["attn3"], sv["lse3"], aw)
        dproj = [dq3.reshape(t_rows, aw), dk3.reshape(t_rows, aw), dv3.reshape(t_rows, aw), du3.reshape(t_rows, sw)]
        big_parts["w_in"][l] = _wgrad_call("wgrad_in", sv["h"], dproj, "y", ndev, ndev, (dm, in_w // ndev), tm)

        small_parts[l] = dict(
            attn_gain=dga, ssm_gain=dgs, ssm_a_re=da_re[:, 0], ssm_a_im=da_im[:, 0], ssm_log_dt=dldt[:, 0, 0], ssm_b_re=dbt_re.transpose(0, 2, 1),
            ssm_b_im=dbt_im.transpose(0, 2, 1), ssm_c_re=dc_re, ssm_c_im=dc_im, ssm_d=dd, b_glu=dbglu, b_out=dbout, ln1_g=dg1,
            ln1_b=db1, b_ff1=dbff1, b_ff2=dbff2, ln2_g=dg2, ln2_b=db2)

        layer_grads = [big_parts[n][l] for n in BIG_NAMES] + [_pack_small([small_parts[l][n] for n in SMALL_NAMES])]
        if distributed:
            grad_handles[l], token = _exchange_start(f"grads_start_l{l}", layer_grads, grad_modes)
        else:
            grad_handles[l], token = layer_grads, jnp.zeros((SUBLANES, LANES), f32)

        wpb = in_w // ndev
        if l > 0:
            prev = saved[l - 1]

            def in_bwd(rv, cr):
                dacc = alpha * rv[4]
                dp = jnp.concatenate([v_.astype(MXU_DTYPE) for v_ in rv[:4]], axis=-1)
                for j in range(ndev):
                    dacc = dacc + _mm_nt(dp[:, j * wpb:(j + 1) * wpb], cr[0][j])
                dpre, dg, db = _layer_norm_bwd(dacc, rv[5], cr[1][...])
                return [dpre], [dg, db, _colsum(dpre)]

            (dpre2,), (dg2, db2, dbff2) = _rows_call("in_bwd", in_bwd, dproj + [dpre1, prev["pre2"]],
                                                     [sv["w_in"], prev["g2"], token], [(dm, f32)], [(1, dm)] * 3, tm)
        else:
            def in_bwd0(rv, cr):
                dacc = alpha * rv[4]
                dp = jnp.concatenate([v_.astype(MXU_DTYPE) for v_ in rv[:4]], axis=-1)
                for j in range(ndev):
                    dacc = dacc + _mm_nt(dp[:, j * wpb:(j + 1) * wpb], cr[0][j])
                return [dacc], []

            (grad_x,), _ = _rows_call("in_bwd0", in_bwd0, dproj + [dpre1], [sv["w_in"], token], [(dm, f32)], [], tm)

    small_shapes = [weights[n].shape[1:] for n in SMALL_NAMES]
    outs = {n: None for n in BIG_NAMES}
    packed = [_pack_small_layers([d[n] for n in SMALL_NAMES]) for d in (weights, moments_m, moments_v)]
    small_out = None
    after = grad_x
    for l in reversed(range(depth)):
        if distributed:
            recv = _exchange_wait(f"grads_wait_l{l}", grad_handles[l], grad_modes, after)
        else:
            recv = [g_[None] if i == len(BIG_NAMES) else g_ for i, g_ in enumerate(grad_handles[l])]
        for n, parts in zip(BIG_NAMES, recv[:-1], strict=True):
            outs[n] = _adamw_layer("adamw_" + n, parts, weights[n], moments_m[n], moments_v[n], l, outs[n],
                                   min(parts.shape[1], 256))
        small_out = _adamw_layer("adamw_small", recv[-1], *packed, l, small_out, recv[-1].shape[1])
        after = small_out[0]
    for k in range(4):
        flat = small_out[k].reshape(depth, -1)
        off = 0
        for n, shp in zip(SMALL_NAMES, small_shapes, strict=True):
            sz = int(np.prod(shp))
            outs.setdefault(n, [None] * 4)
            outs[n][k] = flat[:, off:off + sz].reshape((depth,) + tuple(shp))
            off += sz

    result = [loss, grad_x.reshape(bsz, seq, dm)]
    for k in range(4):
        result += [outs[n][k] for n in WEIGHT_ORDER]
    return tuple(result)


def kernel(x, positions, w_in, attn_gain, ssm_gain, ssm_a_re, ssm_a_im, ssm_log_dt, ssm_b_re, ssm_b_im, ssm_c_re, ssm_c_im, ssm_d, w_glu, b_glu, w_out, b_out, ln1_g, ln1_b, w_ff1, b_ff1, w_ff2, b_ff2, ln2_g, ln2_b, loss_target, m_w_in, m_attn_gain, m_ssm_gain, m_ssm_a_re, m_ssm_a_im, m_ssm_log_dt, m_ssm_b_re, m_ssm_b_im, m_ssm_c_re, m_ssm_c_im, m_ssm_d, m_w_glu, m_b_glu, m_w_out, m_b_out, m_ln1_g, m_ln1_b, m_w_ff1, m_b_ff1, m_w_ff2, m_b_ff2, m_ln2_g, m_ln2_b, v_w_in, v_attn_gain, v_ssm_gain, v_ssm_a_re, v_ssm_a_im, v_ssm_log_dt, v_ssm_b_re, v_ssm_b_im, v_ssm_c_re, v_ssm_c_im, v_ssm_d, v_w_glu, v_b_glu, v_w_out, v_b_out, v_ln1_g, v_ln1_b, v_w_ff1, v_b_ff1, v_w_ff2, v_b_ff2, v_ln2_g, v_ln2_b):
    loc = locals()
    weights = {n: loc[n] for n in WEIGHT_ORDER}
    moments_m = {n: loc["m_" + n] for n in WEIGHT_ORDER}
    moments_v = {n: loc["v_" + n] for n in WEIGHT_ORDER}
    return _step(x, positions, weights, moments_m, moments_v, loss_target, distributed=True)
```

```python
import functools
import math

import jax
import jax.numpy as jnp
import numpy as np
from jax import lax
from jax.experimental import pallas as pl
from jax.experimental.pallas import tpu as pltpu

F32 = jnp.float32
MXU_DTYPE = jnp.bfloat16

HEAD_DIM = 64
DILATION_PAIRS = ((128, 1), (512, 4), (2048, 16))
ROPE_THETA = 10000.0
SSM_GROUP_DIM = 16
SSM_STATE = 64
LN_EPS = 1e-5
RMS_EPS = 1e-6
NEG_INF = -1e30
ADAM_LR, ADAM_B1, ADAM_B2, ADAM_EPS, ADAM_WD, ADAM_STEP = 0.001, 0.9, 0.999, 1e-08, 0.01, 10

LANES = 128
SUBLANES = 8
QBLK = 128
VMEM_LIMIT = 56 * 2**20
MESH_AXES = ("x", "y", "c")
NDEV = 8

SMALL_NAMES = ("attn_gain", "ssm_gain", "ssm_a_re", "ssm_a_im", "ssm_log_dt", "ssm_b_re", "ssm_b_im", "ssm_c_re",
               "ssm_c_im", "ssm_d", "b_glu", "b_out", "ln1_g", "ln1_b", "b_ff1", "b_ff2", "ln2_g", "ln2_b")
BIG_NAMES = ("w_in", "w_glu", "w_out", "w_ff1", "w_ff2")
WEIGHT_ORDER = ("w_in", "attn_gain", "ssm_gain", "ssm_a_re", "ssm_a_im", "ssm_log_dt", "ssm_b_re", "ssm_b_im", "ssm_c_re",
                "ssm_c_im", "ssm_d", "w_glu", "b_glu", "w_out", "b_out", "ln1_g", "ln1_b", "w_ff1", "b_ff1", "w_ff2",
                "b_ff2", "ln2_g", "ln2_b")


def _cparams(sem=None):
    return pltpu.CompilerParams(dimension_semantics=sem, vmem_limit_bytes=VMEM_LIMIT)


def _mm(a, b):
    return jnp.dot(a.astype(MXU_DTYPE), b.astype(MXU_DTYPE), preferred_element_type=F32)


def _mm_nt(a, b):
    return lax.dot_general(a.astype(MXU_DTYPE), b.astype(MXU_DTYPE), (((1,), (1,)), ((), ())), preferred_element_type=F32)


def _mm_tn(a, b):
    return lax.dot_general(a.astype(MXU_DTYPE), b.astype(MXU_DTYPE), (((0,), (0,)), ((), ())), preferred_element_type=F32)


def _colsum(x):
    return jnp.sum(x, axis=0, keepdims=True)


def _layer_norm(x, g, b):
    mu = jnp.mean(x, axis=-1, keepdims=True)
    xc = x - mu
    var = jnp.mean(xc * xc, axis=-1, keepdims=True)
    return xc * lax.rsqrt(var + LN_EPS) * g + b


def _layer_norm_bwd(dy, pre, g):
    mu = jnp.mean(pre, axis=-1, keepdims=True)
    xc = pre - mu
    var = jnp.mean(xc * xc, axis=-1, keepdims=True)
    r = lax.rsqrt(var + LN_EPS)
    xhat = xc * r
    dyg = dy * g
    dpre = r * (dyg - jnp.mean(dyg, axis=-1, keepdims=True) - xhat * jnp.mean(dyg * xhat, axis=-1, keepdims=True))
    return dpre, _colsum(dy * xhat), _colsum(dy)


def _rms_norm(x, g):
    return x * lax.rsqrt(jnp.mean(x * x, axis=-1, keepdims=True) + RMS_EPS) * g


def _rms_norm_bwd(dy, x, g):
    r = lax.rsqrt(jnp.mean(x * x, axis=-1, keepdims=True) + RMS_EPS)
    dyg = dy * g
    dx = dyg * r - x * (r * r * r) * jnp.mean(dyg * x, axis=-1, keepdims=True)
    return dx, _colsum(dy * x * r)


_GELU_C = math.sqrt(2.0 / math.pi)


def _gelu(x):
    return 0.5 * x * (1.0 + jnp.tanh(_GELU_C * (x + 0.044715 * (x * x * x))))


def _gelu_grad(x):
    t = jnp.tanh(_GELU_C * (x + 0.044715 * (x * x * x)))
    return 0.5 * (1.0 + t) + 0.5 * x * (1.0 - t * t) * (_GELU_C * (1.0 + 3.0 * 0.044715 * x * x))


def _rows_call(name, fn, rows, consts, out_rows, out_accs, tm):
    rows = [r if isinstance(r, tuple) else (r, r.shape[1], 0) for r in rows]
    m = rows[0][0].shape[0]
    assert m % tm == 0
    nr, nc, no, na = len(rows), len(consts), len(out_rows), len(out_accs)

    def body(*refs):
        rr, cr = refs[:nr], refs[nr:nr + nc]
        orr, ar = refs[nr + nc:nr + nc + no], refs[nr + nc + no:]
        outs, accs = fn([r[...] for r in rr], cr)
        for o, v in zip(orr, outs, strict=True):
            o[...] = v.astype(o.dtype)
        if na:
            first = pl.program_id(0) == 0

            @pl.when(first)
            def _():
                for a, v in zip(ar, accs, strict=True):
                    a[...] = v

            @pl.when(jnp.logical_not(first))
            def _():
                for a, v in zip(ar, accs, strict=True):
                    a[...] += v

    def whole(shape):
        return pl.BlockSpec(shape, lambda i, n=len(shape): (0,) * n)

    in_specs = [pl.BlockSpec((tm, w), lambda i, cb=cb: (i, cb)) for _, w, cb in rows] + [whole(c.shape) for c in consts]
    out_specs = [pl.BlockSpec((tm, w), lambda i: (i, 0)) for w, _ in out_rows] + [whole(s) for s in out_accs]
    out_shape = [jax.ShapeDtypeStruct((m, w), dt) for w, dt in out_rows] + [jax.ShapeDtypeStruct(s, F32) for s in out_accs]
    res = pl.pallas_call(body, grid=(m // tm,), in_specs=in_specs, out_specs=out_specs, out_shape=out_shape, name=name,
                         compiler_params=_cparams(("arbitrary",)))(*[r[0] for r in rows], *consts)
    return res[:no], res[no:]


def _wgrad_call(name, x, dy, split, nblk, jb, blk_shape, tm, prologue=None):
    dys = list(dy) if isinstance(dy, (list, tuple)) else [dy]
    m = x.shape[0]
    kk, nn = blk_shape
    assert m % tm == 0 and nblk % jb == 0 and (len(dys) == 1 or (split == "y" and jb == nblk))
    xw = kk * jb if split == "x" else x.shape[1]
    yws = [d.shape[1] for d in dys] if (split == "x" or len(dys) > 1) else [nn * jb]
    nrow = m // tm

    def body(x_ref, *rest):
        dy_refs, o_ref, acc_ref = rest[:len(dys)], rest[-2], rest[-1]
        i = pl.program_id(1)

        @pl.when(i == 0)
        def _():
            acc_ref[...] = jnp.zeros_like(acc_ref)

        xv = x_ref[...]
        if prologue is not None:
            xv = prologue(xv)
        xv = xv.astype(MXU_DTYPE)
        dv = [r[...].astype(MXU_DTYPE) for r in dy_refs]
        dv = dv[0] if len(dv) == 1 else jnp.concatenate(dv, axis=-1)
        for j in range(jb):
            xa = xv[:, j * kk:(j + 1) * kk] if split == "x" else xv
            da = dv[:, j * nn:(j + 1) * nn] if split == "y" else dv
            acc_ref[j] += _mm_tn(xa, da)

        @pl.when(i == nrow - 1)
        def _():
            o_ref[...] = acc_ref[...].astype(o_ref.dtype)

    in_specs = [pl.BlockSpec((tm, xw), (lambda j, i: (i, j)) if split == "x" else (lambda j, i: (i, 0)))]
    in_specs += [pl.BlockSpec((tm, yw), (lambda j, i: (i, j)) if (split == "y" and len(dys) == 1) else (lambda j, i: (i, 0)))
                 for yw in yws]
    return pl.pallas_call(
        body, grid=(nblk // jb, nrow), in_specs=in_specs,
        out_specs=pl.BlockSpec((jb, kk, nn), lambda j, i: (j, 0, 0)),
        out_shape=jax.ShapeDtypeStruct((nblk, kk, nn), MXU_DTYPE),
        scratch_shapes=[pltpu.VMEM((jb, kk, nn), F32)], name=name,
        compiler_params=_cparams(("arbitrary", "arbitrary")))(x, *dys)


_HBM_SPEC = pl.BlockSpec(memory_space=pltpu.HBM)
_SEM_SPEC = pl.BlockSpec(memory_space=pltpu.SEMAPHORE)
_EFFECT = pltpu.SideEffectType.DATAFLOW_SIDE_EFFECTING


def _my_index():
    return 4 * lax.axis_index("x") + 2 * lax.axis_index("y") + lax.axis_index("c")


def _peer_copies(ins, lands, send_sems, recv_sems, modes):
    x, y, c = lax.axis_index("x"), lax.axis_index("y"), lax.axis_index("c")
    me = 4 * x + 2 * y + c
    pairs = []
    for k in range(NDEV - 1):
        fx, fy, fc = ((k + 1) >> 2) & 1, ((k + 1) >> 1) & 1, (k + 1) & 1
        px, py, pc = (x + fx) % 2, (y + fy) % 2, (c + fc) % 2
        idx = 4 * px + 2 * py + pc
        for a, md in enumerate(modes):
            src = ins[a] if md == "gather" else ins[a].at[idx]
            sem = a * (NDEV - 1) + k
            common = dict(src_ref=src, send_sem=send_sems.at[sem], recv_sem=recv_sems.at[sem], device_id=(px, py, pc),
                          device_id_type=pl.DeviceIdType.MESH)
            pairs.append((pltpu.make_async_remote_copy(dst_ref=lands[a].at[me], **common),
                          pltpu.make_async_remote_copy(dst_ref=lands[a].at[idx], **common)))
    return pairs


def _exchange_start(name, arrays, modes, after=None):
    n = len(arrays)
    extra = [] if after is None else [after]
    me = _my_index()
    lands = []
    for a, md in zip(arrays, modes, strict=True):
        piece = a if md == "gather" else lax.dynamic_index_in_dim(a, me, 0, keepdims=False)
        lands.append(lax.dynamic_update_index_in_dim(lax.empty((NDEV,) + piece.shape, a.dtype), piece, me, 0))

    def body(*refs):
        ins, lnd = refs[:n], refs[n:2 * n]
        send_sems, recv_sems = refs[2 * n + len(extra)], refs[2 * n + len(extra) + 1]
        token = refs[-1]
        for out_copy, _ in _peer_copies(ins, lnd, send_sems, recv_sems, modes):
            out_copy.start()
        token[...] = jnp.zeros_like(token)

    sems = pltpu.SemaphoreType.DMA((n * (NDEV - 1),))
    thru = [pltpu.HBM(a.shape, a.dtype) for a in list(arrays) + lands]
    res = pl.pallas_call(
        body, name=name, out_shape=(sems, sems, *thru, jax.ShapeDtypeStruct((SUBLANES, LANES), F32)),
        in_specs=[_HBM_SPEC] * (2 * n) + [pl.BlockSpec(memory_space=pl.ANY)] * len(extra),
        out_specs=(_SEM_SPEC, _SEM_SPEC, *([_HBM_SPEC] * (2 * n)), pl.BlockSpec(memory_space=pltpu.VMEM)),
        input_output_aliases={i: 2 + i for i in range(2 * n)},
        compiler_params=pltpu.CompilerParams(has_side_effects=_EFFECT),
    )(*[pltpu.with_memory_space_constraint(a, pltpu.HBM) for a in list(arrays) + lands], *extra)
    return (res[0], res[1], res[2:2 + n], res[2 + n:2 + 2 * n]), res[-1]


def _exchange_wait(name, handle, modes, after):
    send_sems, recv_sems, ins_thru, lands_thru = handle
    n = len(ins_thru)

    def body(*refs):
        ins, lnd = refs[:n], refs[n:2 * n]
        for out_copy, arrival in _peer_copies(ins, lnd, refs[2 * n], refs[2 * n + 1], modes):
            out_copy.wait_send()
            arrival.wait_recv()

    thru = [pltpu.HBM(a.shape, a.dtype) for a in list(ins_thru) + list(lands_thru)]
    res = pl.pallas_call(
        body, name=name, out_shape=tuple(thru),
        in_specs=[_HBM_SPEC] * (2 * n) + [_SEM_SPEC, _SEM_SPEC, pl.BlockSpec(memory_space=pl.ANY)],
        out_specs=tuple([_HBM_SPEC] * (2 * n)), input_output_aliases={i: i for i in range(2 * n)},
        compiler_params=pltpu.CompilerParams(has_side_effects=_EFFECT),
    )(*ins_thru, *lands_thru, send_sems, recv_sems, after)
    return res[n:]


def _rope_tables(positions3, inv_freq_row, sign_row):
    b, s, _ = positions3.shape

    def body(pos_ref, f_ref, sg_ref, c_ref, s_ref):
        ang = pos_ref[...].astype(F32) * f_ref[...]
        c_ref[...] = jnp.cos(ang)
        s_ref[...] = jnp.sin(ang) * sg_ref[...]

    row = pl.BlockSpec((1, LANES), lambda i: (0, 0))
    blk = pl.BlockSpec((None, s, LANES), lambda i: (i, 0, 0))
    return pl.pallas_call(
        body, grid=(b,), in_specs=[pl.BlockSpec((None, s, 1), lambda i: (i, 0, 0)), row, row], out_specs=[blk, blk],
        out_shape=[jax.ShapeDtypeStruct((b, s, LANES), F32)] * 2, name="rope_tables",
        compiler_params=_cparams(("arbitrary",)))(positions3, inv_freq_row, sign_row)


def _swap_halves(t):
    lane = lax.broadcasted_iota(jnp.int32, t.shape, 1)
    half = HEAD_DIM // 2
    return jnp.where((lane % HEAD_DIM) < half, pltpu.roll(t, LANES - half, 1), pltpu.roll(t, half, 1))


def _segment_rows(r, d, s):
    n = s // d
    return (pl.ds(r, n, stride=d) if d > 1 else pl.ds(0, s)), pl.ds(r * n, n)


def _head_lanes(shape):
    lane = lax.broadcasted_iota(jnp.int32, shape, len(shape) - 1)
    return lane < HEAD_DIM


def _bmm_nt(a, b):
    return lax.dot_general(a.astype(MXU_DTYPE), b.astype(MXU_DTYPE), (((2,), (2,)), ((0,), (0,))), preferred_element_type=F32)


def _bmm(a, b):
    return lax.dot_general(a.astype(MXU_DTYPE), b.astype(MXU_DTYPE), (((2,), (1,)), ((0,), (0,))), preferred_element_type=F32)


def _bmm_tn(a, b):
    return lax.dot_general(a.astype(MXU_DTYPE), b.astype(MXU_DTYPE), (((1,), (1,)), ((0,), (0,))), preferred_element_type=F32)


def _stack_heads(t3):
    head_a = _head_lanes(t3.shape)
    zero = jnp.zeros_like(t3)
    return jnp.concatenate([jnp.where(head_a, t3, zero), jnp.where(head_a, zero, t3)], axis=1)


def _with_previous(t3):
    return jnp.concatenate([jnp.concatenate([t3[:1], t3[:-1]], axis=0), t3], axis=1)


def _head_columns(t3):
    return jnp.concatenate([t3[:, :, lo:lo + 1] for lo in range(0, LANES, HEAD_DIM)], axis=1)


def _to_own_block(t, nb):
    if nb == 1:
        return t
    from_next = jnp.concatenate([t[1:, :QBLK], jnp.zeros_like(t[:1, :QBLK])], axis=0)
    return t[:, QBLK:] + from_next


def _branch_operands(qh, kh, vh, nblk, nb):
    q3 = _stack_heads(qh[...].reshape(nblk, QBLK, LANES))
    k3, v3 = kh[...].reshape(nblk, QBLK, LANES), vh[...].reshape(nblk, QBLK, LANES)
    nh = LANES // HEAD_DIM
    if nb == 1:
        qi = lax.broadcasted_iota(jnp.int32, (1, nh * QBLK, QBLK), 1) % QBLK
        kj = lax.broadcasted_iota(jnp.int32, (1, nh * QBLK, QBLK), 2)
        return q3, k3, v3, kj <= qi
    shape = (nblk, nh * QBLK, 2 * QBLK)
    qi = lax.broadcasted_iota(jnp.int32, shape, 1) % QBLK
    kj = lax.broadcasted_iota(jnp.int32, shape, 2)
    opens = (lax.broadcasted_iota(jnp.int32, shape, 0) % nb) == 0
    mask = (kj >= qi) & (kj <= qi + QBLK) & ((kj >= QBLK) | jnp.logical_not(opens))
    return q3, _with_previous(k3), _with_previous(v3), mask


def _attn_fwd(proj3, ctab, stab, aw):
    b, s, _ = proj3.shape
    npair = aw // LANES
    scale = HEAD_DIM ** -0.5
    nbr = len(DILATION_PAIRS)

    def body(q_ref, k_ref, v_ref, c_ref, s_ref, o_ref, l_ref, qf, kf, vf, qh, kh, vh, op, lp, ob, lb):
        cc, ss = c_ref[...], s_ref[...]
        q2, k2 = q_ref[...], k_ref[...]
        qf[...] = (q2 * cc + _swap_halves(q2) * ss) * scale
        kf[...] = k2 * cc + _swap_halves(k2) * ss
        vf[...] = v_ref[...]
        nblk = s // QBLK
        head_a = _head_lanes((nblk, QBLK, LANES))
        for br, (window, d) in enumerate(DILATION_PAIRS):
            for r in range(d):
                nat, perm = _segment_rows(r, d, s)
                for dst, src in ((qh, qf), (kh, kf), (vh, vf)):
                    dst[perm, :] = src[nat, :].astype(MXU_DTYPE)
            o_dst, l_dst = (ob.at[br], lb.at[br]) if d == 1 else (op, lp)
            nb = (s // d) // QBLK
            q3, kk, vv, mask = _branch_operands(qh, kh, vh, nblk, nb)
            sc = jnp.where(mask, _bmm_nt(q3, kk), NEG_INF)
            mx = jnp.max(sc, axis=-1, keepdims=True)
            p = jnp.exp(sc - mx)
            den = jnp.sum(p, axis=-1, keepdims=True)
            o2 = _bmm(p, vv) / den
            l2 = mx + jnp.log(den)
            o_dst[...] = jnp.where(head_a, o2[:, :QBLK], o2[:, QBLK:]).reshape(s, LANES)
            l_dst[...] = jnp.where(head_a, l2[:, :QBLK], l2[:, QBLK:]).reshape(s, LANES)
            if d > 1:
                for r in range(d):
                    nat, perm = _segment_rows(r, d, s)
                    ob[br, nat, :] = op[perm, :]
                    lb[br, nat, :] = lp[perm, :]
        ls = [lb[br] for br in range(nbr)]
        mx = functools.reduce(jnp.maximum, ls)
        ws = [jnp.exp(l - mx) for l in ls]
        tot = functools.reduce(lambda a, b_: a + b_, ws)
        o_ref[...] = functools.reduce(lambda a, b_: a + b_, [(w / tot) * ob[br] for br, w in enumerate(ws)])
        l_ref[...] = mx + jnp.log(tot)

    def col(off):
        return pl.BlockSpec((None, s, LANES), lambda bi, hp, off=off: (bi, 0, off + hp))

    tab = pl.BlockSpec((None, s, LANES), lambda bi, hp: (bi, 0, 0))
    f32s = pltpu.VMEM((s, LANES), F32)
    mxs = pltpu.VMEM((s, LANES), MXU_DTYPE)
    br_s = pltpu.VMEM((nbr, s, LANES), F32)
    return pl.pallas_call(
        body, grid=(b, npair), in_specs=[col(0), col(npair), col(2 * npair), tab, tab], out_specs=[col(0), col(0)],
        out_shape=[jax.ShapeDtypeStruct((b, s, aw), F32)] * 2,
        scratch_shapes=[f32s] * 3 + [mxs] * 3 + [f32s] * 2 + [br_s] * 2,
        name="attn_fwd", compiler_params=_cparams(("arbitrary", "arbitrary")))(proj3, proj3, proj3, ctab, stab)


def _attn_bwd(proj3, ctab, stab, dout3, out3, lse3, aw):
    b, s, _ = proj3.shape
    npair = aw // LANES
    scale = HEAD_DIM ** -0.5
    nheads = LANES // HEAD_DIM

    def body(q_ref, k_ref, v_ref, c_ref, s_ref, do_ref, o_ref, l_ref, dq_ref, dk_ref, dv_ref,
             qf, kf, vf, dof, dlf, qh, kh, vh, doh, lpm, dpm, dqp, dkp, dvp, dqn, dkn, dvn):
        cc, ss = c_ref[...], s_ref[...]
        q2, k2 = q_ref[...], k_ref[...]
        qf[...] = (q2 * cc + _swap_halves(q2) * ss) * scale
        kf[...] = k2 * cc + _swap_halves(k2) * ss
        vf[...] = v_ref[...]
        do2 = do_ref[...]
        dof[...] = do2
        dd = do2 * o_ref[...]
        in_a = _head_lanes((s, LANES))
        sum_a = jnp.sum(jnp.where(in_a, dd, 0.0), axis=-1, keepdims=True)
        sum_b = jnp.sum(jnp.where(in_a, 0.0, dd), axis=-1, keepdims=True)
        dlf[...] = jnp.where(in_a, sum_a, sum_b)
        for r_ in (dqn, dkn, dvn):
            r_[...] = jnp.zeros_like(r_)
        nblk = s // QBLK
        head_a = _head_lanes((nblk, QBLK, LANES))
        for window, d in DILATION_PAIRS:
            for r in range(d):
                nat, perm = _segment_rows(r, d, s)
                for dst, src in ((qh, qf), (kh, kf), (vh, vf), (doh, dof)):
                    dst[perm, :] = src[nat, :].astype(MXU_DTYPE)
                if d > 1:
                    lpm[perm, :] = l_ref[nat, :]
                    dpm[perm, :] = dlf[nat, :]
            l_src, d_src = (l_ref, dlf) if d == 1 else (lpm, dpm)
            nb = (s // d) // QBLK
            q3, kk, vv, mask = _branch_operands(qh, kh, vh, nblk, nb)
            do3 = _stack_heads(doh[...].reshape(nblk, QBLK, LANES))
            lcol, dcol = _head_columns(l_src[...].reshape(nblk, QBLK, LANES)), _head_columns(d_src[...].reshape(nblk, QBLK, LANES))
            p = jnp.exp(jnp.where(mask, _bmm_nt(q3, kk), NEG_INF) - lcol)
            ds_ = p * (_bmm_nt(do3, vv) - dcol)
            dq2 = _bmm(ds_, kk)
            dq_new = jnp.where(head_a, dq2[:, :QBLK], dq2[:, QBLK:]).reshape(s, LANES)
            dk_new = _to_own_block(_bmm_tn(ds_, q3), nb).reshape(s, LANES)
            dv_new = _to_own_block(_bmm_tn(p, do3), nb).reshape(s, LANES)
            if d == 1:
                dqn[...] += dq_new
                dkn[...] += dk_new
                dvn[...] += dv_new
            else:
                dqp[...] = dq_new
                dkp[...] = dk_new
                dvp[...] = dv_new
                for r in range(d):
                    nat, perm = _segment_rows(r, d, s)
                    dqn[nat, :] += dqp[perm, :]
                    dkn[nat, :] += dkp[perm, :]
                    dvn[nat, :] += dvp[perm, :]
        g = dqn[...] * scale
        dq_ref[...] = g * cc + _swap_halves(g * ss)
        g = dkn[...]
        dk_ref[...] = g * cc + _swap_halves(g * ss)
        dv_ref[...] = dvn[...]

    def col(off):
        return pl.BlockSpec((None, s, LANES), lambda bi, hp, off=off: (bi, 0, off + hp))

    tab = pl.BlockSpec((None, s, LANES), lambda bi, hp: (bi, 0, 0))
    f32s = pltpu.VMEM((s, LANES), F32)
    mxs = pltpu.VMEM((s, LANES), MXU_DTYPE)
    return pl.pallas_call(
        body, grid=(b, npair), in_specs=[col(0), col(npair), col(2 * npair), tab, tab, col(0), col(0), col(0)],
        out_specs=[col(0)] * 3, out_shape=[jax.ShapeDtypeStruct((b, s, aw), F32)] * 3,
        scratch_shapes=[f32s] * 5 + [mxs] * 4 + [f32s] * 8,
        name="attn_bwd", compiler_params=_cparams(("arbitrary", "arbitrary")))(
            proj3, proj3, proj3, ctab, stab, dout3, out3, lse3)


def _s5_discretise(a_re, a_im, log_dt, bt_re, bt_im):
    dt = jnp.exp(log_dt)
    mag = jnp.exp(a_re * dt)
    ang = a_im * dt
    lb_re = mag * jnp.cos(ang)
    lb_im = mag * jnp.sin(ang)
    den = a_re * a_re + a_im * a_im
    nr = lb_re - 1.0
    ni = lb_im
    cr = (nr * a_re + ni * a_im) / den
    ci = (ni * a_re - nr * a_im) / den
    return lb_re, lb_im, cr * bt_re - ci * bt_im, cr * bt_im + ci * bt_re


def _s5_params(a_re, a_im, log_dt, bt_re, bt_im):
    def body(ar, ai, ld, br, bi, o1, o2, o3, o4):
        r = _s5_discretise(ar[...], ai[...], ld[...], br[...], bi[...])
        for o, v in zip((o1, o2, o3, o4), r, strict=True):
            o[...] = v

    sd = jax.ShapeDtypeStruct
    return pl.pallas_call(body, out_shape=[sd(a_re.shape, F32)] * 2 + [sd(bt_re.shape, F32)] * 2, name="s5_params")(
        a_re, a_im, log_dt, bt_re, bt_im)


def _s5_params_bwd(a_re, a_im, log_dt, bt_re, bt_im, cts):
    def body(ar, ai, ld, br, bi, c1, c2, c3, c4, o1, o2, o3, o4, o5):
        _, vjp = jax.vjp(_s5_discretise, ar[...], ai[...], ld[...], br[...], bi[...])
        r = vjp((c1[...], c2[...], c3[...], c4[...]))
        for o, v in zip((o1, o2, o3, o4, o5), r, strict=True):
            o[...] = v

    sd = jax.ShapeDtypeStruct
    return pl.pallas_call(
        body, out_shape=[sd(a_re.shape, F32)] * 2 + [sd(log_dt.shape, F32)] + [sd(bt_re.shape, F32)] * 2, name="s5_params_bwd")(
            a_re, a_im, log_dt, bt_re, bt_im, *cts)


S5_TC = 128


def _s5_fwd(proj3, ucol, lam_re, lam_im, bre, bim, cre, cim, dvec, sw):
    b, s, _ = proj3.shape
    nt = lam_re.shape[0]
    ncb = sw // LANES
    tpc = nt // ncb
    tc = S5_TC

    def body(u_ref, lr_ref, li_ref, bre_ref, bim_ref, cre_ref, cim_ref, d_ref, y_ref, xr_ref, xi_ref, sr, si):
        @pl.when(pl.program_id(0) == 0)
        def _():
            sr[...] = jnp.zeros_like(sr)
            si[...] = jnp.zeros_like(si)

        u_all = u_ref[...].reshape(b * tc, sw)
        for cb in range(ncb):
            ucb = u_all[:, cb * LANES:(cb + 1) * LANES].astype(MXU_DTYPE)
            for t in range(cb * tpc, (cb + 1) * tpc):
                bur, bui = _mm(ucb, bre_ref[t]), _mm(ucb, bim_ref[t])
                for bi in range(b):
                    xr_ref[bi, pl.ds(t, tc, stride=nt), :] = bur[bi * tc:(bi + 1) * tc]
                    xi_ref[bi, pl.ds(t, tc, stride=nt), :] = bui[bi * tc:(bi + 1) * tc]
        lr, li = lr_ref[...], li_ref[...]

        def step(j, carry):
            off = pl.multiple_of(j * nt, nt)
            new = []
            for bi in range(b):
                pr, pi = carry[2 * bi], carry[2 * bi + 1]
                nr = lr * pr - li * pi + xr_ref[bi, pl.ds(off, nt), :]
                ni = lr * pi + li * pr + xi_ref[bi, pl.ds(off, nt), :]
                xr_ref[bi, pl.ds(off, nt), :] = nr
                xi_ref[bi, pl.ds(off, nt), :] = ni
                new += [nr, ni]
            return tuple(new)

        init = tuple(v for bi in range(b) for v in (sr[bi], si[bi]))
        fin = lax.fori_loop(0, tc, step, init, unroll=4)
        for bi in range(b):
            sr[bi] = fin[2 * bi]
            si[bi] = fin[2 * bi + 1]
        for cb in range(ncb):
            cols = slice(cb * LANES, (cb + 1) * LANES)
            acc = d_ref[:, cols] * u_all[:, cols]
            for t in range(cb * tpc, (cb + 1) * tpc):
                rows = pl.ds(t, tc, stride=nt)
                xr_all = jnp.concatenate([xr_ref[bi, rows, :].astype(MXU_DTYPE) for bi in range(b)], axis=0)
                xi_all = jnp.concatenate([xi_ref[bi, rows, :].astype(MXU_DTYPE) for bi in range(b)], axis=0)
                acc = acc + (_mm(xr_all, cre_ref[t]) - _mm(xi_all, cim_ref[t]))
            for bi in range(b):
                y_ref[bi, :, cols] = acc[bi * tc:(bi + 1) * tc]

    def whole(a):
        return pl.BlockSpec(a.shape, lambda c, n=a.ndim: (0,) * n)

    xblk = pl.BlockSpec((b, tc * nt, LANES), lambda c: (0, c, 0))
    return pl.pallas_call(
        body, grid=(s // tc,),
        in_specs=[pl.BlockSpec((b, tc, sw), lambda c: (0, c, ucol))] + [whole(a) for a in (lam_re, lam_im, bre, bim, cre, cim, dvec)],
        out_specs=[pl.BlockSpec((b, tc, sw), lambda c: (0, c, 0)), xblk, xblk],
        out_shape=[jax.ShapeDtypeStruct((b, s, sw), F32)] + [jax.ShapeDtypeStruct((b, s * nt, LANES), F32)] * 2,
        scratch_shapes=[pltpu.VMEM((b, nt, LANES), F32)] * 2, name="s5_fwd",
        compiler_params=_cparams(("arbitrary",)))(proj3, lam_re, lam_im, bre, bim, cre, cim, dvec)


def _s5_bwd(proj3, ucol, dy3, xr3, xi3, lam_re, lam_im, bre, bim, cre, cim, dvec, sw):
    b, s, _ = proj3.shape
    nt = lam_re.shape[0]
    ncb = sw // LANES
    tpc = nt // ncb
    tc = S5_TC
    nchunk = s // tc

    def body(u_ref, dy_ref, xr_ref, xi_ref, pr_ref, pi_ref, lr_ref, li_ref, bre_ref, bim_ref, cre_ref, cim_ref, d_ref,
             du_ref, dbre, dbim, dcre, dcim, dlr, dli, dd, gr, gi, sr, si):
        step_id = pl.program_id(0)

        @pl.when(step_id == 0)
        def _():
            for r in (sr, si, dbre, dbim, dcre, dcim, dlr, dli, dd):
                r[...] = jnp.zeros_like(r)

        u_all = u_ref[...].reshape(b * tc, sw)
        dy_all = dy_ref[...].reshape(b * tc, sw)
        for cb in range(ncb):
            dycb = dy_all[:, cb * LANES:(cb + 1) * LANES].astype(MXU_DTYPE)
            for t in range(cb * tpc, (cb + 1) * tpc):
                dxr, dxi = _mm_nt(dycb, cre_ref[t]), _mm_nt(dycb, cim_ref[t])
                for bi in range(b):
                    gr[bi, pl.ds(t, tc, stride=nt), :] = dxr[bi * tc:(bi + 1) * tc]
                    gi[bi, pl.ds(t, tc, stride=nt), :] = -dxi[bi * tc:(bi + 1) * tc]
        lr, li = lr_ref[...], li_ref[...]

        def step(jj, carry):
            off = pl.multiple_of((tc - 1 - jj) * nt, nt)
            new = []
            for bi in range(b):
                nr_, ni_ = carry[2 * bi], carry[2 * bi + 1]
                vr = gr[bi, pl.ds(off, nt), :] + lr * nr_ + li * ni_
                vi = gi[bi, pl.ds(off, nt), :] + lr * ni_ - li * nr_
                gr[bi, pl.ds(off, nt), :] = vr
                gi[bi, pl.ds(off, nt), :] = vi
                new += [vr, vi]
            return tuple(new)

        init = tuple(v for bi in range(b) for v in (sr[bi], si[bi]))
        fin = lax.fori_loop(0, tc, step, init, unroll=4)
        for bi in range(b):
            sr[bi] = fin[2 * bi]
            si[bi] = fin[2 * bi + 1]

        has_prev = (step_id != nchunk - 1).astype(F32)
        rest = (tc - 1) * nt
        alr = jnp.zeros((nt, LANES), F32)
        ali = jnp.zeros((nt, LANES), F32)
        for bi in range(b):
            s0r, s0i = gr[bi, pl.ds(0, nt), :], gi[bi, pl.ds(0, nt), :]
            x0r, x0i = pr_ref[bi] * has_prev, pi_ref[bi] * has_prev
            alr += s0r * x0r + s0i * x0i
            ali += s0i * x0r - s0r * x0i
            s1r, s1i = gr[bi, pl.ds(nt, rest), :], gi[bi, pl.ds(nt, rest), :]
            x1r, x1i = xr_ref[bi, pl.ds(0, rest), :], xi_ref[bi, pl.ds(0, rest), :]
            alr += jnp.sum((s1r * x1r + s1i * x1i).reshape(tc - 1, nt, LANES), axis=0)
            ali += jnp.sum((s1i * x1r - s1r * x1i).reshape(tc - 1, nt, LANES), axis=0)
        dlr[...] += alr
        dli[...] += ali

        for cb in range(ncb):
            cols = slice(cb * LANES, (cb + 1) * LANES)
            ucb32, dycb32 = u_all[:, cols], dy_all[:, cols]
            ucb, dycb = ucb32.astype(MXU_DTYPE), dycb32.astype(MXU_DTYPE)
            acc = d_ref[:, cols] * dycb32
            for t in range(cb * tpc, (cb + 1) * tpc):
                rows = pl.ds(t, tc, stride=nt)

                def stacked(ref):
                    return jnp.concatenate([ref[bi, rows, :].astype(MXU_DTYPE) for bi in range(b)], axis=0)

                vr, vi = stacked(gr), stacked(gi)
                acc = acc + (_mm_nt(vr, bre_ref[t]) + _mm_nt(vi, bim_ref[t]))
                dbre[t] += _mm_tn(ucb, vr)
                dbim[t] += _mm_tn(ucb, vi)
                dcre[t] += _mm_tn(stacked(xr_ref), dycb)
                dcim[t] -= _mm_tn(stacked(xi_ref), dycb)
            for bi in range(b):
                du_ref[bi, :, cols] = acc[bi * tc:(bi + 1) * tc]
            dd[:, cols] += _colsum(dycb32 * ucb32)

    def whole(a):
        return pl.BlockSpec(a.shape, lambda c, n=len(a.shape): (0,) * n)

    def rev(c):
        return nchunk - 1 - c

    xblk = pl.BlockSpec((b, tc * nt, LANES), lambda c: (0, rev(c), 0))
    prev = pl.BlockSpec((b, nt, LANES), lambda c: (0, jnp.maximum(rev(c) * tc - 1, 0), 0))
    sd = jax.ShapeDtypeStruct
    blk = sd(bre.shape, F32)
    acc_shapes = [blk, blk, sd(cre.shape, F32), sd(cre.shape, F32), sd(lam_re.shape, F32), sd(lam_re.shape, F32), sd((1, sw), F32)]
    return pl.pallas_call(
        body, grid=(nchunk,),
        in_specs=[pl.BlockSpec((b, tc, sw), lambda c: (0, rev(c), ucol)), pl.BlockSpec((b, tc, sw), lambda c: (0, rev(c), 0)),
                  xblk, xblk, prev, prev] + [whole(a) for a in (lam_re, lam_im, bre, bim, cre, cim, dvec)],
        out_specs=[pl.BlockSpec((b, tc, sw), lambda c: (0, rev(c), 0))] + [whole(a) for a in acc_shapes],
        out_shape=[sd((b, s, sw), F32)] + acc_shapes,
        scratch_shapes=[pltpu.VMEM((b, tc * nt, LANES), F32)] * 2 + [pltpu.VMEM((b, nt, LANES), F32)] * 2, name="s5_bwd",
        compiler_params=_cparams(("arbitrary",)))(proj3, dy3, xr3, xi3, xr3, xi3, lam_re, lam_im, bre, bim, cre, cim, dvec)


def _s5_blocks(bb, cc_, ncb):
    g, n, p = bb.shape
    mask = jnp.asarray(_s5_tile_mask(g, p, ncb))
    nt, gpt, gpc = mask.shape
    bbt, cct = bb.reshape(nt, gpt, n, p), cc_.reshape(nt, gpt, n, p)
    spread = mask[:, :, :, None, None]
    bblk = (bbt[:, :, None] * spread).transpose(0, 2, 3, 1, 4).reshape(nt, gpc * n, gpt * p)
    cblk = (cct[:, :, None] * spread).transpose(0, 1, 4, 2, 3).reshape(nt, gpt * p, gpc * n)
    return bblk, cblk


def _s5_tile_mask(g, p, ncb):
    nt, gpt, gpc = g * p // LANES, LANES // p, g // ncb
    mask = np.zeros((nt, gpt, gpc), np.float32)
    for t in range(nt):
        for gl in range(gpt):
            mask[t, gl, (t * gpt + gl) % gpc] = 1.0
    return mask


def _s5_unblock(dbblk, dcblk, g, n, p, ncb):
    mask = jnp.asarray(_s5_tile_mask(g, p, ncb))
    nt, gpt, gpc = mask.shape
    db = jnp.sum(dbblk.reshape(nt, gpc, n, gpt, p) * mask.transpose(0, 2, 1)[:, :, None, :, None], axis=1)
    dc = jnp.sum(dcblk.reshape(nt, gpt, p, gpc, n) * mask[:, :, None, :, None], axis=3)
    return db.transpose(0, 2, 1, 3).reshape(g, n, p), dc.transpose(0, 1, 3, 2).reshape(g, n, p)


def _adamw_math(w, g, m, v):
    m = ADAM_B1 * m + (1.0 - ADAM_B1) * g
    v = ADAM_B2 * v + (1.0 - ADAM_B2) * (g * g)
    m_hat = m / (1.0 - ADAM_B1 ** ADAM_STEP)
    v_hat = v / (1.0 - ADAM_B2 ** ADAM_STEP)
    delta = -ADAM_LR * (m_hat / (jnp.sqrt(v_hat) + ADAM_EPS) + ADAM_WD * w)
    return delta, m, v


def _adamw_layer(name, parts, w, m, v, layer, prev, tr):
    nparts, r, c = parts.shape
    assert r % tr == 0
    if prev is None:
        prev = [lax.empty(w.shape, F32) for _ in range(4)]

    def body(p_ref, w_ref, m_ref, v_ref, *rest):
        g_out, d_out, m_out, v_out = rest[4:]
        g = p_ref[0].astype(F32)
        for k in range(1, nparts):
            g = g + p_ref[k].astype(F32)
        delta, mn, vn = _adamw_math(w_ref[...], g, m_ref[...], v_ref[...])
        g_out[...] = g
        d_out[...] = delta
        m_out[...] = mn
        v_out[...] = vn

    blk = pl.BlockSpec((None, tr, c), lambda i: (layer, i, 0))
    kept = pl.BlockSpec(memory_space=pl.ANY)
    return pl.pallas_call(
        body, grid=(r // tr,), in_specs=[pl.BlockSpec((nparts, tr, c), lambda i: (0, i, 0)), blk, blk, blk] + [kept] * 4,
        out_specs=[blk] * 4, out_shape=[jax.ShapeDtypeStruct(w.shape, F32)] * 4, name=name,
        input_output_aliases={4 + k: k for k in range(4)},
        compiler_params=_cparams(("arbitrary",)))(parts, w, m, v, *prev)


def _pack_small(pieces):
    flat = jnp.concatenate([p.reshape(-1) for p in pieces])
    n = flat.shape[0]
    unit = SUBLANES * LANES
    padded = -(-n // unit) * unit
    return jnp.pad(flat, (0, padded - n)).reshape(padded // LANES, LANES)


def _pack_small_layers(pieces):
    nl = pieces[0].shape[0]
    flat = jnp.concatenate([p.reshape(nl, -1) for p in pieces], axis=1)
    n = flat.shape[1]
    unit = SUBLANES * LANES
    padded = -(-n // unit) * unit
    return jnp.pad(flat, ((0, 0), (0, padded - n))).reshape(nl, padded // LANES, LANES)


def _step(x, positions, weights, moments_m, moments_v, loss_target, distributed):
    f32 = F32
    bsz, seq, dm = x.shape
    depth = weights["w_in"].shape[0]
    aw = weights["attn_gain"].shape[1]
    sw = weights["ssm_gain"].shape[1]
    dff = weights["b_ff1"].shape[1]
    ng, npst = weights["ssm_a_re"].shape[1:]
    gdim = weights["ssm_d"].shape[2]
    in_w = 3 * aw + sw
    t_rows = bsz * seq
    alpha = (2.0 * depth) ** 0.25
    ncb = sw // LANES
    nt = ng * npst // LANES
    ndev = NDEV if distributed else 1
    tm = 256

    gather_modes = ["gather"] * len(BIG_NAMES)
    no_token = jnp.zeros((SUBLANES, LANES), f32)

    def shards(l, names):
        return [weights[n][l].astype(MXU_DTYPE) for n in names]

    if distributed:
        first, _ = _exchange_start("weights_start_l0_in", shards(0, BIG_NAMES[:1]), gather_modes[:1])
        (g_in0,) = _exchange_wait("weights_wait_l0_in", first, gather_modes[:1], positions)
        rest0, tok_rest0 = _exchange_start("weights_start_l0_rest", shards(0, BIG_NAMES[1:]), gather_modes[1:], after=g_in0)

    half = HEAD_DIM // 2
    inv_freq = ROPE_THETA ** (-jnp.arange(half, dtype=f32) * 2.0 / HEAD_DIM)
    reps = LANES // half
    inv_row = jnp.tile(inv_freq, reps)[None, :]
    sign_row = jnp.tile(jnp.concatenate([-jnp.ones((half,), f32), jnp.ones((half,), f32)]), LANES // HEAD_DIM)[None, :]
    ctab, stab = _rope_tables(positions[..., None], inv_row, sign_row)

    def row(v):
        return v.reshape(1, -1)

    h = x.reshape(t_rows, dm)
    saved = []
    for l in range(depth):
        tok_in = tok_mix = no_token
        if not distributed:
            g_in, g_glu, g_out, g_ff1, g_ff2 = [weights[n][l].astype(MXU_DTYPE)[None] for n in BIG_NAMES]
        elif l == 0:
            g_in, tok_in = g_in0, tok_rest0
        else:
            g_in, g_glu, g_out, g_ff1, g_ff2 = _exchange_wait(f"weights_wait_l{l}", next_gather, gather_modes, h)
            if l + 1 < depth:
                next_gather, tok_in = _exchange_start(f"weights_start_l{l + 1}", shards(l + 1, BIG_NAMES), gather_modes, after=g_in)
        w_in_l = g_in

        def in_proj(rv, cr):
            return [jnp.concatenate([_mm(rv[0], cr[0][j]) for j in range(ndev)], axis=-1)], []

        (proj,), _ = _rows_call("in_proj", in_proj, [h], [w_in_l, tok_in], [(in_w, f32)], [], tm)
        proj3 = proj.reshape(bsz, seq, in_w)
        attn3, lse3 = _attn_fwd(proj3, ctab, stab, aw)
        if distributed and l == 0:
            g_glu, g_out, g_ff1, g_ff2 = _exchange_wait("weights_wait_l0_rest", rest0, gather_modes[1:], attn3)
            if depth > 1:
                next_gather, tok_mix = _exchange_start("weights_start_l1", shards(1, BIG_NAMES), gather_modes, after=g_glu)
        w_glu_l = g_glu.reshape(sw, sw)
        w_out_l = g_out.reshape(dm, dm)
        w_ff1_l = g_ff1
        w_ff2_l = g_ff2.reshape(dff, dm)

        a_re, a_im = weights["ssm_a_re"][l][:, None, :], weights["ssm_a_im"][l][:, None, :]
        log_dt = weights["ssm_log_dt"][l][:, None, None]
        bt_re = weights["ssm_b_re"][l].transpose(0, 2, 1)
        bt_im = weights["ssm_b_im"][l].transpose(0, 2, 1)
        lb_re, lb_im, bb_re, bb_im = _s5_params(a_re, a_im, log_dt, bt_re, bt_im)
        lam_re, lam_im = lb_re.reshape(nt, LANES), lb_im.reshape(nt, LANES)
        bre, cre = _s5_blocks(bb_re, weights["ssm_c_re"][l], ncb)
        bim, cim = _s5_blocks(bb_im, weights["ssm_c_im"][l], ncb)
        s5c = [a_.astype(MXU_DTYPE) for a_ in (bre, bim, cre, cim)]
        dvec = row(weights["ssm_d"][l])
        ypre3, xr3, xi3 = _s5_fwd(proj3, 3 * aw // sw, lam_re, lam_im, *s5c, dvec, sw)
        attn, ypre = attn3.reshape(t_rows, aw), ypre3.reshape(t_rows, sw)

        b_glu, ga, gs = row(weights["b_glu"][l]), row(weights["attn_gain"][l]), row(weights["ssm_gain"][l])

        def mix(rv, cr):
            at, yp = rv
            g = _gelu(yp)
            ssm = g * jax.nn.sigmoid(_mm(g, cr[0][...]) + cr[1][...])
            return [jnp.concatenate([_rms_norm(at, cr[2][...]), _rms_norm(ssm, cr[3][...])], axis=-1)], []

        (mixed,), _ = _rows_call("mix", mix, [attn, ypre], [w_glu_l, b_glu, ga, gs, tok_mix], [(dm, MXU_DTYPE)], [], tm)

        b_out, g1, b1 = row(weights["b_out"][l]), row(weights["ln1_g"][l]), row(weights["ln1_b"][l])

        def out_proj(rv, cr):
            pre = alpha * rv[1] + _mm(rv[0], cr[0][...]) + cr[1][...]
            return [pre, _layer_norm(pre, cr[2][...], cr[3][...])], []

        (pre1, h1), _ = _rows_call("out_proj", out_proj, [mixed, h], [w_out_l, b_out, g1, b1], [(dm, f32), (dm, f32)], [], tm)

        b_ff1 = row(weights["b_ff1"][l])

        def ff1(rv, cr):
            return [jnp.concatenate([_mm(rv[0], cr[0][j]) for j in range(ndev)], axis=-1) + cr[1][...]], []

        (act,), _ = _rows_call("ff1", ff1, [h1], [w_ff1_l, b_ff1], [(dff, f32)], [], tm)

        b_ff2, g2, b2 = row(weights["b_ff2"][l]), row(weights["ln2_g"][l]), row(weights["ln2_b"][l])

        def ff2(rv, cr):
            r = jnp.square(jnp.maximum(rv[0], 0.0))
            pre = alpha * rv[1] + _mm(r, cr[0][...]) + cr[1][...]
            return [pre, _layer_norm(pre, cr[2][...], cr[3][...])], []

        (pre2, h2), _ = _rows_call("ff2", ff2, [act, h1], [w_ff2_l, b_ff2, g2, b2], [(dm, f32), (dm, f32)], [], tm)

        saved.append(dict(h=h, proj3=proj3, attn=attn, lse3=lse3, attn3=attn3, ypre=ypre, xr3=xr3, xi3=xi3, mixed=mixed, pre1=pre1,
                          h1=h1, act=act, pre2=pre2, w_in=w_in_l, w_glu=w_glu_l, w_out=w_out_l, w_ff1=w_ff1_l, w_ff2=w_ff2_l,
                          lam=(lam_re, lam_im), s5c=s5c, dvec=dvec, b_glu=b_glu, ga=ga, gs=gs, g1=g1, g2=g2,
                          s5in=(a_re, a_im, log_dt, bt_re, bt_im)))
        h = h2

    g2_last = saved[-1]["g2"]

    def loss_fn(rv, cr):
        y, tgt, pre = rv
        err = y - tgt
        part = 0.5 * jnp.sum(jnp.mean(err * err, axis=-1, keepdims=True), axis=0, keepdims=True)
        dpre, dg, db = _layer_norm_bwd(err * (1.0 / dm), pre, cr[0][...])
        return [dpre], [jnp.broadcast_to(part, (1, LANES)), dg, db, _colsum(dpre)]

    (dpre2,), (loss_acc, dg2, db2, dbff2) = _rows_call(
        "loss", loss_fn, [h, loss_target.reshape(t_rows, dm), saved[-1]["pre2"]], [g2_last], [(dm, f32)],
        [(1, LANES), (1, dm), (1, dm), (1, dm)], tm)
    loss = loss_acc[0, 0]
    if distributed:
        loss = lax.psum(loss, MESH_AXES)

    big_parts = {n: [None] * depth for n in BIG_NAMES}
    small_parts = [None] * depth
    grad_handles = [None] * depth
    grad_modes = ["scatter"] * len(BIG_NAMES) + ["gather"]
    grad_x = None
    for l in reversed(range(depth)):
        sv = saved[l]

        def ff2_bwd(rv, cr):
            da = _mm_nt(rv[0], cr[0][...]) * (2.0 * jnp.maximum(rv[1], 0.0))
            return [da], [_colsum(da)]

        (dact,), (dbff1,) = _rows_call("ff2_bwd", ff2_bwd, [dpre2, sv["act"]], [sv["w_ff2"]], [(dff, MXU_DTYPE)], [(1, dff)], tm)
        big_parts["w_ff2"][l] = _wgrad_call("wgrad_ff2", sv["act"], dpre2, "x", ndev, ndev // 2 if ndev > 1 else 1,
                                            (dff // ndev, dm), tm, prologue=lambda a: jnp.square(jnp.maximum(a, 0.0)))
        big_parts["w_ff1"][l] = _wgrad_call("wgrad_ff1", sv["h1"], dact, "y", ndev, ndev // 2 if ndev > 1 else 1,
                                            (dm, dff // ndev), tm)

        def ff1_bwd(rv, cr):
            dacc = alpha * rv[1]
            wpb = dff // ndev
            for j in range(ndev):
                dacc = dacc + _mm_nt(rv[0][:, j * wpb:(j + 1) * wpb], cr[0][j])
            dpre, dg, db = _layer_norm_bwd(dacc, rv[2], cr[1][...])
            return [dpre], [dg, db, _colsum(dpre)]

        (dpre1,), (dg1, db1, dbout) = _rows_call("ff1_bwd", ff1_bwd, [dact, dpre2, sv["pre1"]], [sv["w_ff1"], sv["g1"]],
                                                 [(dm, f32)], [(1, dm)] * 3, tm)
        big_parts["w_out"][l] = _wgrad_call("wgrad_out", sv["mixed"], dpre1, "x", ndev, ndev, (dm // ndev, dm), tm)

        def mix_bwd(rv, cr):
            dp, at, yp = rv
            w_out_r, w_glu_r, bg, ga_, gs_ = cr
            dmixed = _mm_nt(dp, w_out_r[...])
            g = _gelu(yp)
            sig = jax.nn.sigmoid(_mm(g, w_glu_r[...]) + bg[...])
            ssm = g * sig
            dat, dga = _rms_norm_bwd(dmixed[:, :aw], at, ga_[...])
            dssm, dgs = _rms_norm_bwd(dmixed[:, aw:], ssm, gs_[...])
            dz = dssm * g * sig * (1.0 - sig)
            dg = dssm * sig + _mm_nt(dz, w_glu_r[...])
            return [dat, dg * _gelu_grad(yp), dz], [dga, dgs, _colsum(dz)]

        (dattn, dypre, dz), (dga, dgs, dbglu) = _rows_call(
            "mix_bwd", mix_bwd, [dpre1, sv["attn"], sv["ypre"]], [sv["w_out"], sv["w_glu"], sv["b_glu"], sv["ga"], sv["gs"]],
            [(aw, f32), (sw, f32), (sw, MXU_DTYPE)], [(1, aw), (1, sw), (1, sw)], tm)
        big_parts["w_glu"][l] = _wgrad_call("wgrad_glu", sv["ypre"], dz, "x", 1, 1, (sw, sw), tm, prologue=_gelu).reshape(
            ndev, sw // ndev, sw)

        du3, dbre, dbim, dcre, dcim, dlr, dli, dd = _s5_bwd(
            sv["proj3"], 3 * aw // sw, dypre.reshape(bsz, seq, sw), sv["xr3"], sv["xi3"], *sv["lam"], *sv["s5c"], sv["dvec"], sw)
        dbb_re, dc_re = _s5_unblock(dbre, dcre, ng, gdim, npst, ncb)
        dbb_im, dc_im = _s5_unblock(dbim, dcim, ng, gdim, npst, ncb)
        da_re, da_im, dldt, dbt_re, dbt_im = _s5_params_bwd(
            *sv["s5in"], (dlr.reshape(ng, 1, npst), dli.reshape(ng, 1, npst), dbb_re, dbb_im))

        dq3, dk3, dv3 = _attn_bwd(sv["proj3"], ctab, stab, dattn.reshape(bsz, seq, aw), sv["attn3"], sv["lse3"], aw)
        dproj = [dq3.reshape(t_rows, aw), dk3.reshape(t_rows, aw), dv3.reshape(t_rows, aw), du3.reshape(t_rows, sw)]
        big_parts["w_in"][l] = _wgrad_call("wgrad_in", sv["h"], dproj, "y", ndev, ndev, (dm, in_w // ndev), tm)

        small_parts[l] = dict(
            attn_gain=dga, ssm_gain=dgs, ssm_a_re=da_re[:, 0], ssm_a_im=da_im[:, 0], ssm_log_dt=dldt[:, 0, 0], ssm_b_re=dbt_re.transpose(0, 2, 1),
            ssm_b_im=dbt_im.transpose(0, 2, 1), ssm_c_re=dc_re, ssm_c_im=dc_im, ssm_d=dd, b_glu=dbglu, b_out=dbout, ln1_g=dg1,
            ln1_b=db1, b_ff1=dbff1, b_ff2=dbff2, ln2_g=dg2, ln2_b=db2)

        layer_grads = [big_parts[n][l] for n in BIG_NAMES] + [_pack_small([small_parts[l][n] for n in SMALL_NAMES])]
        if distributed:
            grad_handles[l], token = _exchange_start(f"grads_start_l{l}", layer_grads, grad_modes)
        else:
            grad_handles[l], token = layer_grads, jnp.zeros((SUBLANES, LANES), f32)

        wpb = in_w // ndev
        if l > 0:
            prev = saved[l - 1]

            def in_bwd(rv, cr):
                dacc = alpha * rv[4]
                dp = jnp.concatenate([v_.astype(MXU_DTYPE) for v_ in rv[:4]], axis=-1)
                for j in range(ndev):
                    dacc = dacc + _mm_nt(dp[:, j * wpb:(j + 1) * wpb], cr[0][j])
                dpre, dg, db = _layer_norm_bwd(dacc, rv[5], cr[1][...])
                return [dpre], [dg, db, _colsum(dpre)]

            (dpre2,), (dg2, db2, dbff2) = _rows_call("in_bwd", in_bwd, dproj + [dpre1, prev["pre2"]],
                                                     [sv["w_in"], prev["g2"], token], [(dm, f32)], [(1, dm)] * 3, tm)
        else:
            def in_bwd0(rv, cr):
                dacc = alpha * rv[4]
                dp = jnp.concatenate([v_.astype(MXU_DTYPE) for v_ in rv[:4]], axis=-1)
                for j in range(ndev):
                    dacc = dacc + _mm_nt(dp[:, j * wpb:(j + 1) * wpb], cr[0][j])
                return [dacc], []

            (grad_x,), _ = _rows_call("in_bwd0", in_bwd0, dproj + [dpre1], [sv["w_in"], token], [(dm, f32)], [], tm)

    small_shapes = [weights[n].shape[1:] for n in SMALL_NAMES]
    outs = {n: None for n in BIG_NAMES}
    packed = [_pack_small_layers([d[n] for n in SMALL_NAMES]) for d in (weights, moments_m, moments_v)]
    small_out = None
    after = grad_x
    for l in reversed(range(depth)):
        if distributed:
            recv = _exchange_wait(f"grads_wait_l{l}", grad_handles[l], grad_modes, after)
        else:
            recv = [g_[None] if i == len(BIG_NAMES) else g_ for i, g_ in enumerate(grad_handles[l])]
        for n, parts in zip(BIG_NAMES, recv[:-1], strict=True):
            outs[n] = _adamw_layer("adamw_" + n, parts, weights[n], moments_m[n], moments_v[n], l, outs[n],
                                   min(parts.shape[1], 256))
        small_out = _adamw_layer("adamw_small", recv[-1], *packed, l, small_out, recv[-1].shape[1])
        after = small_out[0]
    for k in range(4):
        flat = small_out[k].reshape(depth, -1)
        off = 0
        for n, shp in zip(SMALL_NAMES, small_shapes, strict=True):
            sz = int(np.prod(shp))
            outs.setdefault(n, [None] * 4)
            outs[n][k] = flat[:, off:off + sz].reshape((depth,) + tuple(shp))
            off += sz

    result = [loss, grad_x.reshape(bsz, seq, dm)]
    for k in range(4):
        result += [outs[n][k] for n in WEIGHT_ORDER]
    return tuple(result)


def kernel(x, positions, w_in, attn_gain, ssm_gain, ssm_a_re, ssm_a_im, ssm_log_dt, ssm_b_re, ssm_b_im, ssm_c_re, ssm_c_im, ssm_d, w_glu, b_glu, w_out, b_out, ln1_g, ln1_b, w_ff1, b_ff1, w_ff2, b_ff2, ln2_g, ln2_b, loss_target, m_w_in, m_attn_gain, m_ssm_gain, m_ssm_a_re, m_ssm_a_im, m_ssm_log_dt, m_ssm_b_re, m_ssm_b_im, m_ssm_c_re, m_ssm_c_im, m_ssm_d, m_w_glu, m_b_glu, m_w_out, m_b_out, m_ln1_g, m_ln1_b, m_w_ff1, m_b_ff1, m_w_ff2, m_b_ff2, m_ln2_g, m_ln2_b, v_w_in, v_attn_gain, v_ssm_gain, v_ssm_a_re, v_ssm_a_im, v_ssm_log_dt, v_ssm_b_re, v_ssm_b_im, v_ssm_c_re, v_ssm_c_im, v_ssm_d, v_w_glu, v_b_glu, v_w_out, v_b_out, v_ln1_g, v_ln1_b, v_w_ff1, v_b_ff1, v_w_ff2, v_b_ff2, v_ln2_g, v_ln2_b):
    loc = locals()
    weights = {n: loc[n] for n in WEIGHT_ORDER}
    moments_m = {n: loc["m_" + n] for n in WEIGHT_ORDER}
    moments_v = {n: loc["v_" + n] for n in WEIGHT_ORDER}
    return _step(x, positions, weights, moments_m, moments_v, loss_target, distributed=True)
```

```python
import functools
import math

import jax
import jax.numpy as jnp
import numpy as np
from jax import lax
from jax.experimental import pallas as pl
from jax.experimental.pallas import tpu as pltpu

F32 = jnp.float32
MXU_DTYPE = jnp.bfloat16

HEAD_DIM = 64
DILATION_PAIRS = ((128, 1), (512, 4), (2048, 16))
ROPE_THETA = 10000.0
SSM_GROUP_DIM = 16
SSM_STATE = 64
LN_EPS = 1e-5
RMS_EPS = 1e-6
NEG_INF = -1e30
ADAM_LR, ADAM_B1, ADAM_B2, ADAM_EPS, ADAM_WD, ADAM_STEP = 0.001, 0.9, 0.999, 1e-08, 0.01, 10

LANES = 128
SUBLANES = 8
QBLK = 128
VMEM_LIMIT = 56 * 2**20
MESH_AXES = ("x", "y", "c")
NDEV = 8

SMALL_NAMES = ("attn_gain", "ssm_gain", "ssm_a_re", "ssm_a_im", "ssm_log_dt", "ssm_b_re", "ssm_b_im", "ssm_c_re",
               "ssm_c_im", "ssm_d", "b_glu", "b_out", "ln1_g", "ln1_b", "b_ff1", "b_ff2", "ln2_g", "ln2_b")
BIG_NAMES = ("w_in", "w_glu", "w_out", "w_ff1", "w_ff2")
WEIGHT_ORDER = ("w_in", "attn_gain", "ssm_gain", "ssm_a_re", "ssm_a_im", "ssm_log_dt", "ssm_b_re", "ssm_b_im", "ssm_c_re",
                "ssm_c_im", "ssm_d", "w_glu", "b_glu", "w_out", "b_out", "ln1_g", "ln1_b", "w_ff1", "b_ff1", "w_ff2",
                "b_ff2", "ln2_g", "ln2_b")


def _cparams(sem=None):
    return pltpu.CompilerParams(dimension_semantics=sem, vmem_limit_bytes=VMEM_LIMIT)


def _mm(a, b):
    return jnp.dot(a.astype(MXU_DTYPE), b.astype(MXU_DTYPE), preferred_element_type=F32)


def _mm_nt(a, b):
    return lax.dot_general(a.astype(MXU_DTYPE), b.astype(MXU_DTYPE), (((1,), (1,)), ((), ())), preferred_element_type=F32)


def _mm_tn(a, b):
    return lax.dot_general(a.astype(MXU_DTYPE), b.astype(MXU_DTYPE), (((0,), (0,)), ((), ())), preferred_element_type=F32)


def _colsum(x):
    return jnp.sum(x, axis=0, keepdims=True)


def _layer_norm(x, g, b):
    mu = jnp.mean(x, axis=-1, keepdims=True)
    xc = x - mu
    var = jnp.mean(xc * xc, axis=-1, keepdims=True)
    return xc * lax.rsqrt(var + LN_EPS) * g + b


def _layer_norm_bwd(dy, pre, g):
    mu = jnp.mean(pre, axis=-1, keepdims=True)
    xc = pre - mu
    var = jnp.mean(xc * xc, axis=-1, keepdims=True)
    r = lax.rsqrt(var + LN_EPS)
    xhat = xc * r
    dyg = dy * g
    dpre = r * (dyg - jnp.mean(dyg, axis=-1, keepdims=True) - xhat * jnp.mean(dyg * xhat, axis=-1, keepdims=True))
    return dpre, _colsum(dy * xhat), _colsum(dy)


def _rms_norm(x, g):
    return x * lax.rsqrt(jnp.mean(x * x, axis=-1, keepdims=True) + RMS_EPS) * g


def _rms_norm_bwd(dy, x, g):
    r = lax.rsqrt(jnp.mean(x * x, axis=-1, keepdims=True) + RMS_EPS)
    dyg = dy * g
    dx = dyg * r - x * (r * r * r) * jnp.mean(dyg * x, axis=-1, keepdims=True)
    return dx, _colsum(dy * x * r)


_GELU_C = math.sqrt(2.0 / math.pi)


def _gelu(x):
    return 0.5 * x * (1.0 + jnp.tanh(_GELU_C * (x + 0.044715 * (x * x * x))))


def _gelu_grad(x):
    t = jnp.tanh(_GELU_C * (x + 0.044715 * (x * x * x)))
    return 0.5 * (1.0 + t) + 0.5 * x * (1.0 - t * t) * (_GELU_C * (1.0 + 3.0 * 0.044715 * x * x))


def _rows_call(name, fn, rows, consts, out_rows, out_accs, tm):
    rows = [r if isinstance(r, tuple) else (r, r.shape[1], 0) for r in rows]
    m = rows[0][0].shape[0]
    assert m % tm == 0
    nr, nc, no, na = len(rows), len(consts), len(out_rows), len(out_accs)

    def body(*refs):
        rr, cr = refs[:nr], refs[nr:nr + nc]
        orr, ar = refs[nr + nc:nr + nc + no], refs[nr + nc + no:]
        outs, accs = fn([r[...] for r in rr], cr)
        for o, v in zip(orr, outs, strict=True):
            o[...] = v.astype(o.dtype)
        if na:
            first = pl.program_id(0) == 0

            @pl.when(first)
            def _():
                for a, v in zip(ar, accs, strict=True):
                    a[...] = v

            @pl.when(jnp.logical_not(first))
            def _():
                for a, v in zip(ar, accs, strict=True):
                    a[...] += v

    def whole(shape):
        return pl.BlockSpec(shape, lambda i, n=len(shape): (0,) * n)

    in_specs = [pl.BlockSpec((tm, w), lambda i, cb=cb: (i, cb)) for _, w, cb in rows] + [whole(c.shape) for c in consts]
    out_specs = [pl.BlockSpec((tm, w), lambda i: (i, 0)) for w, _ in out_rows] + [whole(s) for s in out_accs]
    out_shape = [jax.ShapeDtypeStruct((m, w), dt) for w, dt in out_rows] + [jax.ShapeDtypeStruct(s, F32) for s in out_accs]
    res = pl.pallas_call(body, grid=(m // tm,), in_specs=in_specs, out_specs=out_specs, out_shape=out_shape, name=name,
                         compiler_params=_cparams(("arbitrary",)))(*[r[0] for r in rows], *consts)
    return res[:no], res[no:]


def _wgrad_call(name, x, dy, split, nblk, jb, blk_shape, tm, prologue=None):
    dys = list(dy) if isinstance(dy, (list, tuple)) else [dy]
    m = x.shape[0]
    kk, nn = blk_shape
    assert m % tm == 0 and nblk % jb == 0 and (len(dys) == 1 or (split == "y" and jb == nblk))
    xw = kk * jb if split == "x" else x.shape[1]
    yws = [d.shape[1] for d in dys] if (split == "x" or len(dys) > 1) else [nn * jb]
    nrow = m // tm

    def body(x_ref, *rest):
        dy_refs, o_ref, acc_ref = rest[:len(dys)], rest[-2], rest[-1]
        i = pl.program_id(1)

        @pl.when(i == 0)
        def _():
            acc_ref[...] = jnp.zeros_like(acc_ref)

        xv = x_ref[...]
        if prologue is not None:
            xv = prologue(xv)
        xv = xv.astype(MXU_DTYPE)
        dv = [r[...].astype(MXU_DTYPE) for r in dy_refs]
        dv = dv[0] if len(dv) == 1 else jnp.concatenate(dv, axis=-1)
        for j in range(jb):
            xa = xv[:, j * kk:(j + 1) * kk] if split == "x" else xv
            da = dv[:, j * nn:(j + 1) * nn] if split == "y" else dv
            acc_ref[j] += _mm_tn(xa, da)

        @pl.when(i == nrow - 1)
        def _():
            o_ref[...] = acc_ref[...].astype(o_ref.dtype)

    in_specs = [pl.BlockSpec((tm, xw), (lambda j, i: (i, j)) if split == "x" else (lambda j, i: (i, 0)))]
    in_specs += [pl.BlockSpec((tm, yw), (lambda j, i: (i, j)) if (split == "y" and len(dys) == 1) else (lambda j, i: (i, 0)))
                 for yw in yws]
    return pl.pallas_call(
        body, grid=(nblk // jb, nrow), in_specs=in_specs,
        out_specs=pl.BlockSpec((jb, kk, nn), lambda j, i: (j, 0, 0)),
        out_shape=jax.ShapeDtypeStruct((nblk, kk, nn), MXU_DTYPE),
        scratch_shapes=[pltpu.VMEM((jb, kk, nn), F32)], name=name,
        compiler_params=_cparams(("arbitrary", "arbitrary")))(x, *dys)


_HBM_SPEC = pl.BlockSpec(memory_space=pltpu.HBM)
_SEM_SPEC = pl.BlockSpec(memory_space=pltpu.SEMAPHORE)
_EFFECT = pltpu.SideEffectType.DATAFLOW_SIDE_EFFECTING


def _my_index():
    return 4 * lax.axis_index("x") + 2 * lax.axis_index("y") + lax.axis_index("c")


def _peer_copies(ins, lands, send_sems, recv_sems, modes):
    x, y, c = lax.axis_index("x"), lax.axis_index("y"), lax.axis_index("c")
    me = 4 * x + 2 * y + c
    pairs = []
    for k in range(NDEV - 1):
        fx, fy, fc = ((k + 1) >> 2) & 1, ((k + 1) >> 1) & 1, (k + 1) & 1
        px, py, pc = (x + fx) % 2, (y + fy) % 2, (c + fc) % 2
        idx = 4 * px + 2 * py + pc
        for a, md in enumerate(modes):
            src = ins[a] if md == "gather" else ins[a].at[idx]
            sem = a * (NDEV - 1) + k
            common = dict(src_ref=src, send_sem=send_sems.at[sem], recv_sem=recv_sems.at[sem], device_id=(px, py, pc),
                          device_id_type=pl.DeviceIdType.MESH)
            pairs.append((pltpu.make_async_remote_copy(dst_ref=lands[a].at[me], **common),
                          pltpu.make_async_remote_copy(dst_ref=lands[a].at[idx], **common)))
    return pairs


def _exchange_start(name, arrays, modes, after=None):
    n = len(arrays)
    extra = [] if after is None else [after]
    me = _my_index()
    lands = []
    for a, md in zip(arrays, modes, strict=True):
        piece = a if md == "gather" else lax.dynamic_index_in_dim(a, me, 0, keepdims=False)
        lands.append(lax.dynamic_update_index_in_dim(lax.empty((NDEV,) + piece.shape, a.dtype), piece, me, 0))

    def body(*refs):
        ins, lnd = refs[:n], refs[n:2 * n]
        send_sems, recv_sems = refs[2 * n + len(extra)], refs[2 * n + len(extra) + 1]
        token = refs[-1]
        for out_copy, _ in _peer_copies(ins, lnd, send_sems, recv_sems, modes):
            out_copy.start()
        token[...] = jnp.zeros_like(token)

    sems = pltpu.SemaphoreType.DMA((n * (NDEV - 1),))
    thru = [pltpu.HBM(a.shape, a.dtype) for a in list(arrays) + lands]
    res = pl.pallas_call(
        body, name=name, out_shape=(sems, sems, *thru, jax.ShapeDtypeStruct((SUBLANES, LANES), F32)),
        in_specs=[_HBM_SPEC] * (2 * n) + [pl.BlockSpec(memory_space=pl.ANY)] * len(extra),
        out_specs=(_SEM_SPEC, _SEM_SPEC, *([_HBM_SPEC] * (2 * n)), pl.BlockSpec(memory_space=pltpu.VMEM)),
        input_output_aliases={i: 2 + i for i in range(2 * n)},
        compiler_params=pltpu.CompilerParams(has_side_effects=_EFFECT),
    )(*[pltpu.with_memory_space_constraint(a, pltpu.HBM) for a in list(arrays) + lands], *extra)
    return (res[0], res[1], res[2:2 + n], res[2 + n:2 + 2 * n]), res[-1]


def _exchange_wait(name, handle, modes, after):
    send_sems, recv_sems, ins_thru, lands_thru = handle
    n = len(ins_thru)

    def body(*refs):
        ins, lnd = refs[:n], refs[n:2 * n]
        for out_copy, arrival in _peer_copies(ins, lnd, refs[2 * n], refs[2 * n + 1], modes):
            out_copy.wait_send()
            arrival.wait_recv()

    thru = [pltpu.HBM(a.shape, a.dtype) for a in list(ins_thru) + list(lands_thru)]
    res = pl.pallas_call(
        body, name=name, out_shape=tuple(thru),
        in_specs=[_HBM_SPEC] * (2 * n) + [_SEM_SPEC, _SEM_SPEC, pl.BlockSpec(memory_space=pl.ANY)],
        out_specs=tuple([_HBM_SPEC] * (2 * n)), input_output_aliases={i: i for i in range(2 * n)},
        compiler_params=pltpu.CompilerParams(has_side_effects=_EFFECT),
    )(*ins_thru, *lands_thru, send_sems, recv_sems, after)
    return res[n:]


def _rope_tables(positions3, inv_freq_row, sign_row):
    b, s, _ = positions3.shape

    def body(pos_ref, f_ref, sg_ref, c_ref, s_ref):
        ang = pos_ref[...].astype(F32) * f_ref[...]
        c_ref[...] = jnp.cos(ang)
        s_ref[...] = jnp.sin(ang) * sg_ref[...]

    row = pl.BlockSpec((1, LANES), lambda i: (0, 0))
    blk = pl.BlockSpec((None, s, LANES), lambda i: (i, 0, 0))
    return pl.pallas_call(
        body, grid=(b,), in_specs=[pl.BlockSpec((None, s, 1), lambda i: (i, 0, 0)), row, row], out_specs=[blk, blk],
        out_shape=[jax.ShapeDtypeStruct((b, s, LANES), F32)] * 2, name="rope_tables",
        compiler_params=_cparams(("arbitrary",)))(positions3, inv_freq_row, sign_row)


def _swap_halves(t):
    lane = lax.broadcasted_iota(jnp.int32, t.shape, 1)
    half = HEAD_DIM // 2
    return jnp.where((lane % HEAD_DIM) < half, pltpu.roll(t, LANES - half, 1), pltpu.roll(t, half, 1))


def _segment_rows(r, d, s):
    n = s // d
    return (pl.ds(r, n, stride=d) if d > 1 else pl.ds(0, s)), pl.ds(r * n, n)


def _head_lanes(shape):
    lane = lax.broadcasted_iota(jnp.int32, shape, len(shape) - 1)
    return lane < HEAD_DIM


def _bmm_nt(a, b):
    return lax.dot_general(a.astype(MXU_DTYPE), b.astype(MXU_DTYPE), (((2,), (2,)), ((0,), (0,))), preferred_element_type=F32)


def _bmm(a, b):
    return lax.dot_general(a.astype(MXU_DTYPE), b.astype(MXU_DTYPE), (((2,), (1,)), ((0,), (0,))), preferred_element_type=F32)


def _bmm_tn(a, b):
    return lax.dot_general(a.astype(MXU_DTYPE), b.astype(MXU_DTYPE), (((1,), (1,)), ((0,), (0,))), preferred_element_type=F32)


def _stack_heads(t3):
    head_a = _head_lanes(t3.shape)
    zero = jnp.zeros_like(t3)
    return jnp.concatenate([jnp.where(head_a, t3, zero), jnp.where(head_a, zero, t3)], axis=1)


def _with_previous(t3):
    return jnp.concatenate([jnp.concatenate([t3[:1], t3[:-1]], axis=0), t3], axis=1)


def _head_columns(t3):
    return jnp.concatenate([t3[:, :, lo:lo + 1] for lo in range(0, LANES, HEAD_DIM)], axis=1)


def _to_own_block(t, nb):
    if nb == 1:
        return t
    from_next = jnp.concatenate([t[1:, :QBLK], jnp.zeros_like(t[:1, :QBLK])], axis=0)
    return t[:, QBLK:] + from_next


def _branch_operands(qh, kh, vh, nblk, nb):
    q3 = _stack_heads(qh[...].reshape(nblk, QBLK, LANES))
    k3, v3 = kh[...].reshape(nblk, QBLK, LANES), vh[...].reshape(nblk, QBLK, LANES)
    nh = LANES // HEAD_DIM
    if nb == 1:
        qi = lax.broadcasted_iota(jnp.int32, (1, nh * QBLK, QBLK), 1) % QBLK
        kj = lax.broadcasted_iota(jnp.int32, (1, nh * QBLK, QBLK), 2)
        return q3, k3, v3, kj <= qi
    shape = (nblk, nh * QBLK, 2 * QBLK)
    qi = lax.broadcasted_iota(jnp.int32, shape, 1) % QBLK
    kj = lax.broadcasted_iota(jnp.int32, shape, 2)
    opens = (lax.broadcasted_iota(jnp.int32, shape, 0) % nb) == 0
    mask = (kj >= qi) & (kj <= qi + QBLK) & ((kj >= QBLK) | jnp.logical_not(opens))
    return q3, _with_previous(k3), _with_previous(v3), mask


def _attn_fwd(proj3, ctab, stab, aw):
    b, s, _ = proj3.shape
    npair = aw // LANES
    scale = HEAD_DIM ** -0.5
    nbr = len(DILATION_PAIRS)

    def body(q_ref, k_ref, v_ref, c_ref, s_ref, o_ref, l_ref, qf, kf, vf, qh, kh, vh, op, lp, ob, lb):
        cc, ss = c_ref[...], s_ref[...]
        q2, k2 = q_ref[...], k_ref[...]
        qf[...] = (q2 * cc + _swap_halves(q2) * ss) * scale
        kf[...] = k2 * cc + _swap_halves(k2) * ss
        vf[...] = v_ref[...]
        nblk = s // QBLK
        head_a = _head_lanes((nblk, QBLK, LANES))
        for br, (window, d) in enumerate(DILATION_PAIRS):
            for r in range(d):
                nat, perm = _segment_rows(r, d, s)
                for dst, src in ((qh, qf), (kh, kf), (vh, vf)):
                    dst[perm, :] = src[nat, :].astype(MXU_DTYPE)
            o_dst, l_dst = (ob.at[br], lb.at[br]) if d == 1 else (op, lp)
            nb = (s // d) // QBLK
            q3, kk, vv, mask = _branch_operands(qh, kh, vh, nblk, nb)
            sc = jnp.where(mask, _bmm_nt(q3, kk), NEG_INF)
            mx = jnp.max(sc, axis=-1, keepdims=True)
            p = jnp.exp(sc - mx)
            den = jnp.sum(p, axis=-1, keepdims=True)
            o2 = _bmm(p, vv) / den
            l2 = mx + jnp.log(den)
            o_dst[...] = jnp.where(head_a, o2[:, :QBLK], o2[:, QBLK:]).reshape(s, LANES)
            l_dst[...] = jnp.where(head_a, l2[:, :QBLK], l2[:, QBLK:]).reshape(s, LANES)
            if d > 1:
                for r in range(d):
                    nat, perm = _segment_rows(r, d, s)
                    ob[br, nat, :] = op[perm, :]
                    lb[br, nat, :] = lp[perm, :]
        ls = [lb[br] for br in range(nbr)]
        mx = functools.reduce(jnp.maximum, ls)
        ws = [jnp.exp(l - mx) for l in ls]
        tot = functools.reduce(lambda a, b_: a + b_, ws)
        o_ref[...] = functools.reduce(lambda a, b_: a + b_, [(w / tot) * ob[br] for br, w in enumerate(ws)])
        l_ref[...] = mx + jnp.log(tot)

    def col(off):
        return pl.BlockSpec((None, s, LANES), lambda bi, hp, off=off: (bi, 0, off + hp))

    tab = pl.BlockSpec((None, s, LANES), lambda bi, hp: (bi, 0, 0))
    f32s = pltpu.VMEM((s, LANES), F32)
    mxs = pltpu.VMEM((s, LANES), MXU_DTYPE)
    br_s = pltpu.VMEM((nbr, s, LANES), F32)
    return pl.pallas_call(
        body, grid=(b, npair), in_specs=[col(0), col(npair), col(2 * npair), tab, tab], out_specs=[col(0), col(0)],
        out_shape=[jax.ShapeDtypeStruct((b, s, aw), F32)] * 2,
        scratch_shapes=[f32s] * 3 + [mxs] * 3 + [f32s] * 2 + [br_s] * 2,
        name="attn_fwd", compiler_params=_cparams(("arbitrary", "arbitrary")))(proj3, proj3, proj3, ctab, stab)


def _attn_bwd(proj3, ctab, stab, dout3, out3, lse3, aw):
    b, s, _ = proj3.shape
    npair = aw // LANES
    scale = HEAD_DIM ** -0.5
    nheads = LANES // HEAD_DIM

    def body(q_ref, k_ref, v_ref, c_ref, s_ref, do_ref, o_ref, l_ref, dq_ref, dk_ref, dv_ref,
             qf, kf, vf, dof, dlf, qh, kh, vh, doh, lpm, dpm, dqp, dkp, dvp, dqn, dkn, dvn):
        cc, ss = c_ref[...], s_ref[...]
        q2, k2 = q_ref[...], k_ref[...]
        qf[...] = (q2 * cc + _swap_halves(q2) * ss) * scale
        kf[...] = k2 * cc + _swap_halves(k2) * ss
        vf[...] = v_ref[...]
        do2 = do_ref[...]
        dof[...] = do2
        dd = do2 * o_ref[...]
        in_a = _head_lanes((s, LANES))
        sum_a = jnp.sum(jnp.where(in_a, dd, 0.0), axis=-1, keepdims=True)
        sum_b = jnp.sum(jnp.where(in_a, 0.0, dd), axis=-1, keepdims=True)
        dlf[...] = jnp.where(in_a, sum_a, sum_b)
        for r_ in (dqn, dkn, dvn):
            r_[...] = jnp.zeros_like(r_)
        nblk = s // QBLK
        head_a = _head_lanes((nblk, QBLK, LANES))
        for window, d in DILATION_PAIRS:
            for r in range(d):
                nat, perm = _segment_rows(r, d, s)
                for dst, src in ((qh, qf), (kh, kf), (vh, vf), (doh, dof)):
                    dst[perm, :] = src[nat, :].astype(MXU_DTYPE)
                if d > 1:
                    lpm[perm, :] = l_ref[nat, :]
                    dpm[perm, :] = dlf[nat, :]
            l_src, d_src = (l_ref, dlf) if d == 1 else (lpm, dpm)
            nb = (s // d) // QBLK
            q3, kk, vv, mask = _branch_operands(qh, kh, vh, nblk, nb)
            do3 = _stack_heads(doh[...].reshape(nblk, QBLK, LANES))
            lcol, dcol = _head_columns(l_src[...].reshape(nblk, QBLK, LANES)), _head_columns(d_src[...].reshape(nblk, QBLK, LANES))
            p = jnp.exp(jnp.where(mask, _bmm_nt(q3, kk), NEG_INF) - lcol)
            ds_ = p * (_bmm_nt(do3, vv) - dcol)
            dq2 = _bmm(ds_, kk)
            dq_new = jnp.where(head_a, dq2[:, :QBLK], dq2[:, QBLK:]).reshape(s, LANES)
            dk_new = _to_own_block(_bmm_tn(ds_, q3), nb).reshape(s, LANES)
            dv_new = _to_own_block(_bmm_tn(p, do3), nb).reshape(s, LANES)
            if d == 1:
                dqn[...] += dq_new
                dkn[...] += dk_new
                dvn[...] += dv_new
            else:
                dqp[...] = dq_new
                dkp[...] = dk_new
                dvp[...] = dv_new
                for r in range(d):
                    nat, perm = _segment_rows(r, d, s)
                    dqn[nat, :] += dqp[perm, :]
                    dkn[nat, :] += dkp[perm, :]
                    dvn[nat, :] += dvp[perm, :]
        g = dqn[...] * scale
        dq_ref[...] = g * cc + _swap_halves(g * ss)
        g = dkn[...]
        dk_ref[...] = g * cc + _swap_halves(g * ss)
        dv_ref[...] = dvn[...]

    def col(off):
        return pl.BlockSpec((None, s, LANES), lambda bi, hp, off=off: (bi, 0, off + hp))

    tab = pl.BlockSpec((None, s, LANES), lambda bi, hp: (bi, 0, 0))
    f32s = pltpu.VMEM((s, LANES), F32)
    mxs = pltpu.VMEM((s, LANES), MXU_DTYPE)
    return pl.pallas_call(
        body, grid=(b, npair), in_specs=[col(0), col(npair), col(2 * npair), tab, tab, col(0), col(0), col(0)],
        out_specs=[col(0)] * 3, out_shape=[jax.ShapeDtypeStruct((b, s, aw), F32)] * 3,
        scratch_shapes=[f32s] * 5 + [mxs] * 4 + [f32s] * 8,
        name="attn_bwd", compiler_params=_cparams(("arbitrary", "arbitrary")))(
            proj3, proj3, proj3, ctab, stab, dout3, out3, lse3)


def _s5_discretise(a_re, a_im, log_dt, bt_re, bt_im):
    dt = jnp.exp(log_dt)
    mag = jnp.exp(a_re * dt)
    ang = a_im * dt
    lb_re = mag * jnp.cos(ang)
    lb_im = mag * jnp.sin(ang)
    den = a_re * a_re + a_im * a_im
    nr = lb_re - 1.0
    ni = lb_im
    cr = (nr * a_re + ni * a_im) / den
    ci = (ni * a_re - nr * a_im) / den
    return lb_re, lb_im, cr * bt_re - ci * bt_im, cr * bt_im + ci * bt_re


def _s5_params(a_re, a_im, log_dt, bt_re, bt_im):
    def body(ar, ai, ld, br, bi, o1, o2, o3, o4):
        r = _s5_discretise(ar[...], ai[...], ld[...], br[...], bi[...])
        for o, v in zip((o1, o2, o3, o4), r, strict=True):
            o[...] = v

    sd = jax.ShapeDtypeStruct
    return pl.pallas_call(body, out_shape=[sd(a_re.shape, F32)] * 2 + [sd(bt_re.shape, F32)] * 2, name="s5_params")(
        a_re, a_im, log_dt, bt_re, bt_im)


def _s5_params_bwd(a_re, a_im, log_dt, bt_re, bt_im, cts):
    def body(ar, ai, ld, br, bi, c1, c2, c3, c4, o1, o2, o3, o4, o5):
        _, vjp = jax.vjp(_s5_discretise, ar[...], ai[...], ld[...], br[...], bi[...])
        r = vjp((c1[...], c2[...], c3[...], c4[...]))
        for o, v in zip((o1, o2, o3, o4, o5), r, strict=True):
            o[...] = v

    sd = jax.ShapeDtypeStruct
    return pl.pallas_call(
        body, out_shape=[sd(a_re.shape, F32)] * 2 + [sd(log_dt.shape, F32)] + [sd(bt_re.shape, F32)] * 2, name="s5_params_bwd")(
            a_re, a_im, log_dt, bt_re, bt_im, *cts)


S5_TC = 128


def _time_major(tiles):
    nt, tc, lanes = tiles.shape
    return jnp.swapaxes(tiles, 0, 1).reshape(tc * nt, lanes)


def _tile_major(rows, nt):
    return jnp.swapaxes(rows.astype(MXU_DTYPE).reshape(rows.shape[0] // nt, nt, rows.shape[1]), 0, 1)


def _s5_fwd(proj3, ucol, lam_re, lam_im, bre, bim, cre, cim, dvec, sw):
    b, s, _ = proj3.shape
    nt = lam_re.shape[0]
    ncb = sw // LANES
    tpc = nt // ncb
    tc = S5_TC

    def body(u_ref, lr_ref, li_ref, bre_ref, bim_ref, cre_ref, cim_ref, d_ref, y_ref, xr_ref, xi_ref, sr, si):
        @pl.when(pl.program_id(0) == 0)
        def _():
            sr[...] = jnp.zeros_like(sr)
            si[...] = jnp.zeros_like(si)

        u_all = u_ref[...].reshape(b * tc, sw)
        bur, bui = [], []
        for cb in range(ncb):
            ucb = u_all[:, cb * LANES:(cb + 1) * LANES].astype(MXU_DTYPE)
            for t in range(cb * tpc, (cb + 1) * tpc):
                bur.append(_mm(ucb, bre_ref[t]))
                bui.append(_mm(ucb, bim_ref[t]))
        for bi in range(b):
            rows = slice(bi * tc, (bi + 1) * tc)
            xr_ref[bi] = _time_major(jnp.stack([v_[rows] for v_ in bur]))
            xi_ref[bi] = _time_major(jnp.stack([v_[rows] for v_ in bui]))
        lr, li = lr_ref[...], li_ref[...]

        def step(j, carry):
            off = pl.multiple_of(j * nt, nt)
            new = []
            for bi in range(b):
                pr, pi = carry[2 * bi], carry[2 * bi + 1]
                nr = lr * pr - li * pi + xr_ref[bi, pl.ds(off, nt), :]
                ni = lr * pi + li * pr + xi_ref[bi, pl.ds(off, nt), :]
                xr_ref[bi, pl.ds(off, nt), :] = nr
                xi_ref[bi, pl.ds(off, nt), :] = ni
                new += [nr, ni]
            return tuple(new)

        init = tuple(v for bi in range(b) for v in (sr[bi], si[bi]))
        fin = lax.fori_loop(0, tc, step, init, unroll=4)
        for bi in range(b):
            sr[bi] = fin[2 * bi]
            si[bi] = fin[2 * bi + 1]
        xr_t = [_tile_major(xr_ref[bi], nt) for bi in range(b)]
        xi_t = [_tile_major(xi_ref[bi], nt) for bi in range(b)]
        for cb in range(ncb):
            cols = slice(cb * LANES, (cb + 1) * LANES)
            acc = d_ref[:, cols] * u_all[:, cols]
            for t in range(cb * tpc, (cb + 1) * tpc):
                xr_all = jnp.concatenate([xr_t[bi][t] for bi in range(b)], axis=0)
                xi_all = jnp.concatenate([xi_t[bi][t] for bi in range(b)], axis=0)
                acc = acc + (_mm(xr_all, cre_ref[t]) - _mm(xi_all, cim_ref[t]))
            for bi in range(b):
                y_ref[bi, :, cols] = acc[bi * tc:(bi + 1) * tc]

    def whole(a):
        return pl.BlockSpec(a.shape, lambda c, n=a.ndim: (0,) * n)

    xblk = pl.BlockSpec((b, tc * nt, LANES), lambda c: (0, c, 0))
    return pl.pallas_call(
        body, grid=(s // tc,),
        in_specs=[pl.BlockSpec((b, tc, sw), lambda c: (0, c, ucol))] + [whole(a) for a in (lam_re, lam_im, bre, bim, cre, cim, dvec)],
        out_specs=[pl.BlockSpec((b, tc, sw), lambda c: (0, c, 0)), xblk, xblk],
        out_shape=[jax.ShapeDtypeStruct((b, s, sw), F32)] + [jax.ShapeDtypeStruct((b, s * nt, LANES), F32)] * 2,
        scratch_shapes=[pltpu.VMEM((b, nt, LANES), F32)] * 2, name="s5_fwd",
        compiler_params=_cparams(("arbitrary",)))(proj3, lam_re, lam_im, bre, bim, cre, cim, dvec)


def _s5_bwd(proj3, ucol, dy3, xr3, xi3, lam_re, lam_im, bre, bim, cre, cim, dvec, sw):
    b, s, _ = proj3.shape
    nt = lam_re.shape[0]
    ncb = sw // LANES
    tpc = nt // ncb
    tc = S5_TC
    nchunk = s // tc

    def body(u_ref, dy_ref, xr_ref, xi_ref, pr_ref, pi_ref, lr_ref, li_ref, bre_ref, bim_ref, cre_ref, cim_ref, d_ref,
             du_ref, dbre, dbim, dcre, dcim, dlr, dli, dd, gr, gi, sr, si):
        step_id = pl.program_id(0)

        @pl.when(step_id == 0)
        def _():
            for r in (sr, si, dbre, dbim, dcre, dcim, dlr, dli, dd):
                r[...] = jnp.zeros_like(r)

        u_all = u_ref[...].reshape(b * tc, sw)
        dy_all = dy_ref[...].reshape(b * tc, sw)
        dxr, dxi = [], []
        for cb in range(ncb):
            dycb = dy_all[:, cb * LANES:(cb + 1) * LANES].astype(MXU_DTYPE)
            for t in range(cb * tpc, (cb + 1) * tpc):
                dxr.append(_mm_nt(dycb, cre_ref[t]))
                dxi.append(-_mm_nt(dycb, cim_ref[t]))
        for bi in range(b):
            rows = slice(bi * tc, (bi + 1) * tc)
            gr[bi] = _time_major(jnp.stack([v_[rows] for v_ in dxr]))
            gi[bi] = _time_major(jnp.stack([v_[rows] for v_ in dxi]))
        lr, li = lr_ref[...], li_ref[...]

        def step(jj, carry):
            off = pl.multiple_of((tc - 1 - jj) * nt, nt)
            new = []
            for bi in range(b):
                nr_, ni_ = carry[2 * bi], carry[2 * bi + 1]
                vr = gr[bi, pl.ds(off, nt), :] + lr * nr_ + li * ni_
                vi = gi[bi, pl.ds(off, nt), :] + lr * ni_ - li * nr_
                gr[bi, pl.ds(off, nt), :] = vr
                gi[bi, pl.ds(off, nt), :] = vi
                new += [vr, vi]
            return tuple(new)

        init = tuple(v for bi in range(b) for v in (sr[bi], si[bi]))
        fin = lax.fori_loop(0, tc, step, init, unroll=4)
        for bi in range(b):
            sr[bi] = fin[2 * bi]
            si[bi] = fin[2 * bi + 1]

        has_prev = (step_id != nchunk - 1).astype(F32)
        rest = (tc - 1) * nt
        alr = jnp.zeros((nt, LANES), F32)
        ali = jnp.zeros((nt, LANES), F32)
        for bi in range(b):
            s0r, s0i = gr[bi, pl.ds(0, nt), :], gi[bi, pl.ds(0, nt), :]
            x0r, x0i = pr_ref[bi] * has_prev, pi_ref[bi] * has_prev
            alr += s0r * x0r + s0i * x0i
            ali += s0i * x0r - s0r * x0i
            s1r, s1i = gr[bi, pl.ds(nt, rest), :], gi[bi, pl.ds(nt, rest), :]
            x1r, x1i = xr_ref[bi, pl.ds(0, rest), :], xi_ref[bi, pl.ds(0, rest), :]
            alr += jnp.sum((s1r * x1r + s1i * x1i).reshape(tc - 1, nt, LANES), axis=0)
            ali += jnp.sum((s1i * x1r - s1r * x1i).reshape(tc - 1, nt, LANES), axis=0)
        dlr[...] += alr
        dli[...] += ali

        tiles = [[_tile_major(ref[bi], nt) for bi in range(b)] for ref in (gr, gi, xr_ref, xi_ref)]

        def stacked(k, t):
            return jnp.concatenate([tiles[k][bi][t] for bi in range(b)], axis=0)

        for cb in range(ncb):
            cols = slice(cb * LANES, (cb + 1) * LANES)
            ucb32, dycb32 = u_all[:, cols], dy_all[:, cols]
            ucb, dycb = ucb32.astype(MXU_DTYPE), dycb32.astype(MXU_DTYPE)
            acc = d_ref[:, cols] * dycb32
            for t in range(cb * tpc, (cb + 1) * tpc):
                vr, vi = stacked(0, t), stacked(1, t)
                acc = acc + (_mm_nt(vr, bre_ref[t]) + _mm_nt(vi, bim_ref[t]))
                dbre[t] += _mm_tn(ucb, vr)
                dbim[t] += _mm_tn(ucb, vi)
                dcre[t] += _mm_tn(stacked(2, t), dycb)
                dcim[t] -= _mm_tn(stacked(3, t), dycb)
            for bi in range(b):
                du_ref[bi, :, cols] = acc[bi * tc:(bi + 1) * tc]
            dd[:, cols] += _colsum(dycb32 * ucb32)

    def whole(a):
        return pl.BlockSpec(a.shape, lambda c, n=len(a.shape): (0,) * n)

    def rev(c):
        return nchunk - 1 - c

    xblk = pl.BlockSpec((b, tc * nt, LANES), lambda c: (0, rev(c), 0))
    prev = pl.BlockSpec((b, nt, LANES), lambda c: (0, jnp.maximum(rev(c) * tc - 1, 0), 0))
    sd = jax.ShapeDtypeStruct
    blk = sd(bre.shape, F32)
    acc_shapes = [blk, blk, sd(cre.shape, F32), sd(cre.shape, F32), sd(lam_re.shape, F32), sd(lam_re.shape, F32), sd((1, sw), F32)]
    return pl.pallas_call(
        body, grid=(nchunk,),
        in_specs=[pl.BlockSpec((b, tc, sw), lambda c: (0, rev(c), ucol)), pl.BlockSpec((b, tc, sw), lambda c: (0, rev(c), 0)),
                  xblk, xblk, prev, prev] + [whole(a) for a in (lam_re, lam_im, bre, bim, cre, cim, dvec)],
        out_specs=[pl.BlockSpec((b, tc, sw), lambda c: (0, rev(c), 0))] + [whole(a) for a in acc_shapes],
        out_shape=[sd((b, s, sw), F32)] + acc_shapes,
        scratch_shapes=[pltpu.VMEM((b, tc * nt, LANES), F32)] * 2 + [pltpu.VMEM((b, nt, LANES), F32)] * 2, name="s5_bwd",
        compiler_params=_cparams(("arbitrary",)))(proj3, dy3, xr3, xi3, xr3, xi3, lam_re, lam_im, bre, bim, cre, cim, dvec)


def _s5_blocks(bb, cc_, ncb):
    g, n, p = bb.shape
    mask = jnp.asarray(_s5_tile_mask(g, p, ncb))
    nt, gpt, gpc = mask.shape
    bbt, cct = bb.reshape(nt, gpt, n, p), cc_.reshape(nt, gpt, n, p)
    spread = mask[:, :, :, None, None]
    bblk = (bbt[:, :, None] * spread).transpose(0, 2, 3, 1, 4).reshape(nt, gpc * n, gpt * p)
    cblk = (cct[:, :, None] * spread).transpose(0, 1, 4, 2, 3).reshape(nt, gpt * p, gpc * n)
    return bblk, cblk


def _s5_tile_mask(g, p, ncb):
    nt, gpt, gpc = g * p // LANES, LANES // p, g // ncb
    mask = np.zeros((nt, gpt, gpc), np.float32)
    for t in range(nt):
        for gl in range(gpt):
            mask[t, gl, (t * gpt + gl) % gpc] = 1.0
    return mask


def _s5_unblock(dbblk, dcblk, g, n, p, ncb):
    mask = jnp.asarray(_s5_tile_mask(g, p, ncb))
    nt, gpt, gpc = mask.shape
    db = jnp.sum(dbblk.reshape(nt, gpc, n, gpt, p) * mask.transpose(0, 2, 1)[:, :, None, :, None], axis=1)
    dc = jnp.sum(dcblk.reshape(nt, gpt, p, gpc, n) * mask[:, :, None, :, None], axis=3)
    return db.transpose(0, 2, 1, 3).reshape(g, n, p), dc.transpose(0, 1, 3, 2).reshape(g, n, p)


def _adamw_math(w, g, m, v):
    m = ADAM_B1 * m + (1.0 - ADAM_B1) * g
    v = ADAM_B2 * v + (1.0 - ADAM_B2) * (g * g)
    m_hat = m / (1.0 - ADAM_B1 ** ADAM_STEP)
    v_hat = v / (1.0 - ADAM_B2 ** ADAM_STEP)
    delta = -ADAM_LR * (m_hat / (jnp.sqrt(v_hat) + ADAM_EPS) + ADAM_WD * w)
    return delta, m, v


def _adamw_layer(name, parts, w, m, v, layer, prev, tr):
    nparts, r, c = parts.shape
    assert r % tr == 0
    if prev is None:
        prev = [lax.empty(w.shape, F32) for _ in range(4)]

    def body(p_ref, w_ref, m_ref, v_ref, *rest):
        g_out, d_out, m_out, v_out = rest[4:]
        g = p_ref[0].astype(F32)
        for k in range(1, nparts):
            g = g + p_ref[k].astype(F32)
        delta, mn, vn = _adamw_math(w_ref[...], g, m_ref[...], v_ref[...])
        g_out[...] = g
        d_out[...] = delta
        m_out[...] = mn
        v_out[...] = vn

    blk = pl.BlockSpec((None, tr, c), lambda i: (layer, i, 0))
    kept = pl.BlockSpec(memory_space=pl.ANY)
    return pl.pallas_call(
        body, grid=(r // tr,), in_specs=[pl.BlockSpec((nparts, tr, c), lambda i: (0, i, 0)), blk, blk, blk] + [kept] * 4,
        out_specs=[blk] * 4, out_shape=[jax.ShapeDtypeStruct(w.shape, F32)] * 4, name=name,
        input_output_aliases={4 + k: k for k in range(4)},
        compiler_params=_cparams(("arbitrary",)))(parts, w, m, v, *prev)


def _pack_small(pieces):
    flat = jnp.concatenate([p.reshape(-1) for p in pieces])
    n = flat.shape[0]
    unit = SUBLANES * LANES
    padded = -(-n // unit) * unit
    return jnp.pad(flat, (0, padded - n)).reshape(padded // LANES, LANES)


def _pack_small_layers(pieces):
    nl = pieces[0].shape[0]
    flat = jnp.concatenate([p.reshape(nl, -1) for p in pieces], axis=1)
    n = flat.shape[1]
    unit = SUBLANES * LANES
    padded = -(-n // unit) * unit
    return jnp.pad(flat, ((0, 0), (0, padded - n))).reshape(nl, padded // LANES, LANES)


def _step(x, positions, weights, moments_m, moments_v, loss_target, distributed):
    f32 = F32
    bsz, seq, dm = x.shape
    depth = weights["w_in"].shape[0]
    aw = weights["attn_gain"].shape[1]
    sw = weights["ssm_gain"].shape[1]
    dff = weights["b_ff1"].shape[1]
    ng, npst = weights["ssm_a_re"].shape[1:]
    gdim = weights["ssm_d"].shape[2]
    in_w = 3 * aw + sw
    t_rows = bsz * seq
    alpha = (2.0 * depth) ** 0.25
    ncb = sw // LANES
    nt = ng * npst // LANES
    ndev = NDEV if distributed else 1
    tm, tl = 256, 512

    gather_modes = ["gather"] * len(BIG_NAMES)
    no_token = jnp.zeros((SUBLANES, LANES), f32)

    def shards(l, names):
        return [weights[n][l].astype(MXU_DTYPE) for n in names]

    if distributed:
        first, _ = _exchange_start("weights_start_l0_in", shards(0, BIG_NAMES[:1]), gather_modes[:1])
        (g_in0,) = _exchange_wait("weights_wait_l0_in", first, gather_modes[:1], positions)
        rest0, tok_rest0 = _exchange_start("weights_start_l0_rest", shards(0, BIG_NAMES[1:]), gather_modes[1:], after=g_in0)

    half = HEAD_DIM // 2
    inv_freq = ROPE_THETA ** (-jnp.arange(half, dtype=f32) * 2.0 / HEAD_DIM)
    reps = LANES // half
    inv_row = jnp.tile(inv_freq, reps)[None, :]
    sign_row = jnp.tile(jnp.concatenate([-jnp.ones((half,), f32), jnp.ones((half,), f32)]), LANES // HEAD_DIM)[None, :]
    ctab, stab = _rope_tables(positions[..., None], inv_row, sign_row)

    def row(v):
        return v.reshape(1, -1)

    h = x.reshape(t_rows, dm)
    saved = []
    for l in range(depth):
        tok_in = tok_mix = no_token
        if not distributed:
            g_in, g_glu, g_out, g_ff1, g_ff2 = [weights[n][l].astype(MXU_DTYPE)[None] for n in BIG_NAMES]
        elif l == 0:
            g_in, tok_in = g_in0, tok_rest0
        else:
            g_in, g_glu, g_out, g_ff1, g_ff2 = _exchange_wait(f"weights_wait_l{l}", next_gather, gather_modes, h)
            if l + 1 < depth:
                next_gather, tok_in = _exchange_start(f"weights_start_l{l + 1}", shards(l + 1, BIG_NAMES), gather_modes, after=g_in)
        w_in_l = g_in

        def in_proj(rv, cr):
            return [jnp.concatenate([_mm(rv[0], cr[0][j]) for j in range(ndev)], axis=-1)], []

        (proj,), _ = _rows_call("in_proj", in_proj, [h], [w_in_l, tok_in], [(in_w, f32)], [], tl)
        proj3 = proj.reshape(bsz, seq, in_w)
        attn3, lse3 = _attn_fwd(proj3, ctab, stab, aw)
        if distributed and l == 0:
            g_glu, g_out, g_ff1, g_ff2 = _exchange_wait("weights_wait_l0_rest", rest0, gather_modes[1:], attn3)
            if depth > 1:
                next_gather, tok_mix = _exchange_start("weights_start_l1", shards(1, BIG_NAMES), gather_modes, after=g_glu)
        w_glu_l = g_glu.reshape(sw, sw)
        w_out_l = g_out.reshape(dm, dm)
        w_ff1_l = g_ff1
        w_ff2_l = g_ff2.reshape(dff, dm)

        a_re, a_im = weights["ssm_a_re"][l][:, None, :], weights["ssm_a_im"][l][:, None, :]
        log_dt = weights["ssm_log_dt"][l][:, None, None]
        bt_re = weights["ssm_b_re"][l].transpose(0, 2, 1)
        bt_im = weights["ssm_b_im"][l].transpose(0, 2, 1)
        lb_re, lb_im, bb_re, bb_im = _s5_params(a_re, a_im, log_dt, bt_re, bt_im)
        lam_re, lam_im = lb_re.reshape(nt, LANES), lb_im.reshape(nt, LANES)
        bre, cre = _s5_blocks(bb_re, weights["ssm_c_re"][l], ncb)
        bim, cim = _s5_blocks(bb_im, weights["ssm_c_im"][l], ncb)
        s5c = [a_.astype(MXU_DTYPE) for a_ in (bre, bim, cre, cim)]
        dvec = row(weights["ssm_d"][l])
        ypre3, xr3, xi3 = _s5_fwd(proj3, 3 * aw // sw, lam_re, lam_im, *s5c, dvec, sw)
        attn, ypre = attn3.reshape(t_rows, aw), ypre3.reshape(t_rows, sw)

        b_glu, ga, gs = row(weights["b_glu"][l]), row(weights["attn_gain"][l]), row(weights["ssm_gain"][l])

        def mix(rv, cr):
            at, yp = rv
            g = _gelu(yp)
            ssm = g * jax.nn.sigmoid(_mm(g, cr[0][...]) + cr[1][...])
            return [jnp.concatenate([_rms_norm(at, cr[2][...]), _rms_norm(ssm, cr[3][...])], axis=-1)], []

        (mixed,), _ = _rows_call("mix", mix, [attn, ypre], [w_glu_l, b_glu, ga, gs, tok_mix], [(dm, MXU_DTYPE)], [], tl)

        b_out, g1, b1 = row(weights["b_out"][l]), row(weights["ln1_g"][l]), row(weights["ln1_b"][l])

        def out_proj(rv, cr):
            pre = alpha * rv[1] + _mm(rv[0], cr[0][...]) + cr[1][...]
            return [pre, _layer_norm(pre, cr[2][...], cr[3][...])], []

        (pre1, h1), _ = _rows_call("out_proj", out_proj, [mixed, h], [w_out_l, b_out, g1, b1], [(dm, f32), (dm, f32)], [], tl)

        b_ff1 = row(weights["b_ff1"][l])

        def ff1(rv, cr):
            return [jnp.concatenate([_mm(rv[0], cr[0][j]) for j in range(ndev)], axis=-1) + cr[1][...]], []

        (act,), _ = _rows_call("ff1", ff1, [h1], [w_ff1_l, b_ff1], [(dff, f32)], [], tm)

        b_ff2, g2, b2 = row(weights["b_ff2"][l]), row(weights["ln2_g"][l]), row(weights["ln2_b"][l])

        def ff2(rv, cr):
            r = jnp.square(jnp.maximum(rv[0], 0.0))
            pre = alpha * rv[1] + _mm(r, cr[0][...]) + cr[1][...]
            return [pre, _layer_norm(pre, cr[2][...], cr[3][...])], []

        (pre2, h2), _ = _rows_call("ff2", ff2, [act, h1], [w_ff2_l, b_ff2, g2, b2], [(dm, f32), (dm, f32)], [], tm)

        saved.append(dict(h=h, proj3=proj3, attn=attn, lse3=lse3, attn3=attn3, ypre=ypre, xr3=xr3, xi3=xi3, mixed=mixed, pre1=pre1,
                          h1=h1, act=act, pre2=pre2, w_in=w_in_l, w_glu=w_glu_l, w_out=w_out_l, w_ff1=w_ff1_l, w_ff2=w_ff2_l,
                          lam=(lam_re, lam_im), s5c=s5c, dvec=dvec, b_glu=b_glu, ga=ga, gs=gs, g1=g1, g2=g2,
                          s5in=(a_re, a_im, log_dt, bt_re, bt_im)))
        h = h2

    g2_last = saved[-1]["g2"]

    def loss_fn(rv, cr):
        y, tgt, pre = rv
        err = y - tgt
        part = 0.5 * jnp.sum(jnp.mean(err * err, axis=-1, keepdims=True), axis=0, keepdims=True)
        dpre, dg, db = _layer_norm_bwd(err * (1.0 / dm), pre, cr[0][...])
        return [dpre], [jnp.broadcast_to(part, (1, LANES)), dg, db, _colsum(dpre)]

    (dpre2,), (loss_acc, dg2, db2, dbff2) = _rows_call(
        "loss", loss_fn, [h, loss_target.reshape(t_rows, dm), saved[-1]["pre2"]], [g2_last], [(dm, f32)],
        [(1, LANES), (1, dm), (1, dm), (1, dm)], tl)
    loss = loss_acc[0, 0]
    if distributed:
        loss = lax.psum(loss, MESH_AXES)

    big_parts = {n: [None] * depth for n in BIG_NAMES}
    small_parts = [None] * depth
    grad_handles = [None] * depth
    grad_modes = ["scatter"] * len(BIG_NAMES) + ["gather"]
    grad_x = None
    for l in reversed(range(depth)):
        sv = saved[l]

        def ff2_bwd(rv, cr):
            da = _mm_nt(rv[0], cr[0][...]) * (2.0 * jnp.maximum(rv[1], 0.0))
            return [da], [_colsum(da)]

        (dact,), (dbff1,) = _rows_call("ff2_bwd", ff2_bwd, [dpre2, sv["act"]], [sv["w_ff2"]], [(dff, MXU_DTYPE)], [(1, dff)], tm)
        big_parts["w_ff2"][l] = _wgrad_call("wgrad_ff2", sv["act"], dpre2, "x", ndev, ndev // 2 if ndev > 1 else 1,
                                            (dff // ndev, dm), tm, prologue=lambda a: jnp.square(jnp.maximum(a, 0.0)))
        big_parts["w_ff1"][l] = _wgrad_call("wgrad_ff1", sv["h1"], dact, "y", ndev, ndev // 2 if ndev > 1 else 1,
                                            (dm, dff // ndev), tm)

        def ff1_bwd(rv, cr):
            dacc = alpha * rv[1]
            wpb = dff // ndev
            for j in range(ndev):
                dacc = dacc + _mm_nt(rv[0][:, j * wpb:(j + 1) * wpb], cr[0][j])
            dpre, dg, db = _layer_norm_bwd(dacc, rv[2], cr[1][...])
            return [dpre], [dg, db, _colsum(dpre)]

        (dpre1,), (dg1, db1, dbout) = _rows_call("ff1_bwd", ff1_bwd, [dact, dpre2, sv["pre1"]], [sv["w_ff1"], sv["g1"]],
                                                 [(dm, f32)], [(1, dm)] * 3, tm)
        big_parts["w_out"][l] = _wgrad_call("wgrad_out", sv["mixed"], dpre1, "x", ndev, ndev, (dm // ndev, dm), tl)

        def mix_bwd(rv, cr):
            dp, at, yp = rv
            w_out_r, w_glu_r, bg, ga_, gs_ = cr
            dmixed = _mm_nt(dp, w_out_r[...])
            g = _gelu(yp)
            sig = jax.nn.sigmoid(_mm(g, w_glu_r[...]) + bg[...])
            ssm = g * sig
            dat, dga = _rms_norm_bwd(dmixed[:, :aw], at, ga_[...])
            dssm, dgs = _rms_norm_bwd(dmixed[:, aw:], ssm, gs_[...])
            dz = dssm * g * sig * (1.0 - sig)
            dg = dssm * sig + _mm_nt(dz, w_glu_r[...])
            return [dat, dg * _gelu_grad(yp), dz], [dga, dgs, _colsum(dz)]

        (dattn, dypre, dz), (dga, dgs, dbglu) = _rows_call(
            "mix_bwd", mix_bwd, [dpre1, sv["attn"], sv["ypre"]], [sv["w_out"], sv["w_glu"], sv["b_glu"], sv["ga"], sv["gs"]],
            [(aw, f32), (sw, f32), (sw, MXU_DTYPE)], [(1, aw), (1, sw), (1, sw)], tl)
        big_parts["w_glu"][l] = _wgrad_call("wgrad_glu", sv["ypre"], dz, "x", 1, 1, (sw, sw), tl, prologue=_gelu).reshape(
            ndev, sw // ndev, sw)

        du3, dbre, dbim, dcre, dcim, dlr, dli, dd = _s5_bwd(
            sv["proj3"], 3 * aw // sw, dypre.reshape(bsz, seq, sw), sv["xr3"], sv["xi3"], *sv["lam"], *sv["s5c"], sv["dvec"], sw)
        dbb_re, dc_re = _s5_unblock(dbre, dcre, ng, gdim, npst, ncb)
        dbb_im, dc_im = _s5_unblock(dbim, dcim, ng, gdim, npst, ncb)
        da_re, da_im, dldt, dbt_re, dbt_im = _s5_params_bwd(
            *sv["s5in"], (dlr.reshape(ng, 1, npst), dli.reshape(ng, 1, npst), dbb_re, dbb_im))

        dq3, dk3, dv3 = _attn_bwd(sv["proj3"], ctab, stab, dattn.reshape(bsz, seq, aw), sv["attn3"], sv["lse3"], aw)
        dproj = [dq3.reshape(t_rows, aw), dk3.reshape(t_rows, aw), dv3.reshape(t_rows, aw), du3.reshape(t_rows, sw)]
        big_parts["w_in"][l] = _wgrad_call("wgrad_in", sv["h"], dproj, "y", ndev, ndev, (dm, in_w // ndev), tl)

        small_parts[l] = dict(
            attn_gain=dga, ssm_gain=dgs, ssm_a_re=da_re[:, 0], ssm_a_im=da_im[:, 0], ssm_log_dt=dldt[:, 0, 0], ssm_b_re=dbt_re.transpose(0, 2, 1),
            ssm_b_im=dbt_im.transpose(0, 2, 1), ssm_c_re=dc_re, ssm_c_im=dc_im, ssm_d=dd, b_glu=dbglu, b_out=dbout, ln1_g=dg1,
            ln1_b=db1, b_ff1=dbff1, b_ff2=dbff2, ln2_g=dg2, ln2_b=db2)

        layer_grads = [big_parts[n][l] for n in BIG_NAMES] + [_pack_small([small_parts[l][n] for n in SMALL_NAMES])]
        if distributed:
            grad_handles[l], token = _exchange_start(f"grads_start_l{l}", layer_grads, grad_modes)
        else:
            grad_handles[l], token = layer_grads, jnp.zeros((SUBLANES, LANES), f32)

        wpb = in_w // ndev
        if l > 0:
            prev = saved[l - 1]

            def in_bwd(rv, cr):
                dacc = alpha * rv[4]
                dp = jnp.concatenate([v_.astype(MXU_DTYPE) for v_ in rv[:4]], axis=-1)
                for j in range(ndev):
                    dacc = dacc + _mm_nt(dp[:, j * wpb:(j + 1) * wpb], cr[0][j])
                dpre, dg, db = _layer_norm_bwd(dacc, rv[5], cr[1][...])
                return [dpre], [dg, db, _colsum(dpre)]

            (dpre2,), (dg2, db2, dbff2) = _rows_call("in_bwd", in_bwd, dproj + [dpre1, prev["pre2"]],
                                                     [sv["w_in"], prev["g2"], token], [(dm, f32)], [(1, dm)] * 3, tl)
        else:
            def in_bwd0(rv, cr):
                dacc = alpha * rv[4]
                dp = jnp.concatenate([v_.astype(MXU_DTYPE) for v_ in rv[:4]], axis=-1)
                for j in range(ndev):
                    dacc = dacc + _mm_nt(dp[:, j * wpb:(j + 1) * wpb], cr[0][j])
                return [dacc], []

            (grad_x,), _ = _rows_call("in_bwd0", in_bwd0, dproj + [dpre1], [sv["w_in"], token], [(dm, f32)], [], tl)

    small_shapes = [weights[n].shape[1:] for n in SMALL_NAMES]
    outs = {n: None for n in BIG_NAMES}
    packed = [_pack_small_layers([d[n] for n in SMALL_NAMES]) for d in (weights, moments_m, moments_v)]
    small_out = None
    after = grad_x
    for l in reversed(range(depth)):
        if distributed:
            recv = _exchange_wait(f"grads_wait_l{l}", grad_handles[l], grad_modes, after)
        else:
            recv = [g_[None] if i == len(BIG_NAMES) else g_ for i, g_ in enumerate(grad_handles[l])]
        for n, parts in zip(BIG_NAMES, recv[:-1], strict=True):
            outs[n] = _adamw_layer("adamw_" + n, parts, weights[n], moments_m[n], moments_v[n], l, outs[n],
                                   min(parts.shape[1], 256))
        small_out = _adamw_layer("adamw_small", recv[-1], *packed, l, small_out, recv[-1].shape[1])
        after = small_out[0]
    for k in range(4):
        flat = small_out[k].reshape(depth, -1)
        off = 0
        for n, shp in zip(SMALL_NAMES, small_shapes, strict=True):
            sz = int(np.prod(shp))
            outs.setdefault(n, [None] * 4)
            outs[n][k] = flat[:, off:off + sz].reshape((depth,) + tuple(shp))
            off += sz

    result = [loss, grad_x.reshape(bsz, seq, dm)]
    for k in range(4):
        result += [outs[n][k] for n in WEIGHT_ORDER]
    return tuple(result)


def kernel(x, positions, w_in, attn_gain, ssm_gain, ssm_a_re, ssm_a_im, ssm_log_dt, ssm_b_re, ssm_b_im, ssm_c_re, ssm_c_im, ssm_d, w_glu, b_glu, w_out, b_out, ln1_g, ln1_b, w_ff1, b_ff1, w_ff2, b_ff2, ln2_g, ln2_b, loss_target, m_w_in, m_attn_gain, m_ssm_gain, m_ssm_a_re, m_ssm_a_im, m_ssm_log_dt, m_ssm_b_re, m_ssm_b_im, m_ssm_c_re, m_ssm_c_im, m_ssm_d, m_w_glu, m_b_glu, m_w_out, m_b_out, m_ln1_g, m_ln1_b, m_w_ff1, m_b_ff1, m_w_ff2, m_b_ff2, m_ln2_g, m_ln2_b, v_w_in, v_attn_gain, v_ssm_gain, v_ssm_a_re, v_ssm_a_im, v_ssm_log_dt, v_ssm_b_re, v_ssm_b_im, v_ssm_c_re, v_ssm_c_im, v_ssm_d, v_w_glu, v_b_glu, v_w_out, v_b_out, v_ln1_g, v_ln1_b, v_w_ff1, v_b_ff1, v_w_ff2, v_b_ff2, v_ln2_g, v_ln2_b):
    loc = locals()
    weights = {n: loc[n] for n in WEIGHT_ORDER}
    moments_m = {n: loc["m_" + n] for n in WEIGHT_ORDER}
    moments_v = {n: loc["v_" + n] for n in WEIGHT_ORDER}
    return _step(x, positions, weights, moments_m, moments_v, loss_target, distributed=True)
```

```python
import functools
import math

import jax
import jax.numpy as jnp
import numpy as np
from jax import lax
from jax.experimental import pallas as pl
from jax.experimental.pallas import tpu as pltpu

F32 = jnp.float32
MXU_DTYPE = jnp.bfloat16

HEAD_DIM = 64
DILATION_PAIRS = ((128, 1), (512, 4), (2048, 16))
ROPE_THETA = 10000.0
SSM_GROUP_DIM = 16
SSM_STATE = 64
LN_EPS = 1e-5
RMS_EPS = 1e-6
NEG_INF = -1e30
ADAM_LR, ADAM_B1, ADAM_B2, ADAM_EPS, ADAM_WD, ADAM_STEP = 0.001, 0.9, 0.999, 1e-08, 0.01, 10

LANES = 128
SUBLANES = 8
QBLK = 128
VMEM_LIMIT = 56 * 2**20
MESH_AXES = ("x", "y", "c")
NDEV = 8

SMALL_NAMES = ("attn_gain", "ssm_gain", "ssm_a_re", "ssm_a_im", "ssm_log_dt", "ssm_b_re", "ssm_b_im", "ssm_c_re",
               "ssm_c_im", "ssm_d", "b_glu", "b_out", "ln1_g", "ln1_b", "b_ff1", "b_ff2", "ln2_g", "ln2_b")
BIG_NAMES = ("w_in", "w_glu", "w_out", "w_ff1", "w_ff2")
WEIGHT_ORDER = ("w_in", "attn_gain", "ssm_gain", "ssm_a_re", "ssm_a_im", "ssm_log_dt", "ssm_b_re", "ssm_b_im", "ssm_c_re",
                "ssm_c_im", "ssm_d", "w_glu", "b_glu", "w_out", "b_out", "ln1_g", "ln1_b", "w_ff1", "b_ff1", "w_ff2",
                "b_ff2", "ln2_g", "ln2_b")


def _cparams(sem=None):
    return pltpu.CompilerParams(dimension_semantics=sem, vmem_limit_bytes=VMEM_LIMIT)


def _mm(a, b):
    return jnp.dot(a.astype(MXU_DTYPE), b.astype(MXU_DTYPE), preferred_element_type=F32)


def _mm_nt(a, b):
    return lax.dot_general(a.astype(MXU_DTYPE), b.astype(MXU_DTYPE), (((1,), (1,)), ((), ())), preferred_element_type=F32)


def _mm_tn(a, b):
    return lax.dot_general(a.astype(MXU_DTYPE), b.astype(MXU_DTYPE), (((0,), (0,)), ((), ())), preferred_element_type=F32)


def _colsum(x):
    return jnp.sum(x, axis=0, keepdims=True)


def _layer_norm(x, g, b):
    mu = jnp.mean(x, axis=-1, keepdims=True)
    xc = x - mu
    var = jnp.mean(xc * xc, axis=-1, keepdims=True)
    return xc * lax.rsqrt(var + LN_EPS) * g + b


def _layer_norm_bwd(dy, pre, g):
    mu = jnp.mean(pre, axis=-1, keepdims=True)
    xc = pre - mu
    var = jnp.mean(xc * xc, axis=-1, keepdims=True)
    r = lax.rsqrt(var + LN_EPS)
    xhat = xc * r
    dyg = dy * g
    dpre = r * (dyg - jnp.mean(dyg, axis=-1, keepdims=True) - xhat * jnp.mean(dyg * xhat, axis=-1, keepdims=True))
    return dpre, _colsum(dy * xhat), _colsum(dy)


def _rms_norm(x, g):
    return x * lax.rsqrt(jnp.mean(x * x, axis=-1, keepdims=True) + RMS_EPS) * g


def _rms_norm_bwd(dy, x, g):
    r = lax.rsqrt(jnp.mean(x * x, axis=-1, keepdims=True) + RMS_EPS)
    dyg = dy * g
    dx = dyg * r - x * (r * r * r) * jnp.mean(dyg * x, axis=-1, keepdims=True)
    return dx, _colsum(dy * x * r)


_GELU_C = math.sqrt(2.0 / math.pi)


def _gelu(x):
    return 0.5 * x * (1.0 + jnp.tanh(_GELU_C * (x + 0.044715 * (x * x * x))))


def _gelu_grad(x):
    t = jnp.tanh(_GELU_C * (x + 0.044715 * (x * x * x)))
    return 0.5 * (1.0 + t) + 0.5 * x * (1.0 - t * t) * (_GELU_C * (1.0 + 3.0 * 0.044715 * x * x))


def _rows_call(name, fn, rows, consts, out_rows, out_accs, tm):
    rows = [r if isinstance(r, tuple) else (r, r.shape[1], 0) for r in rows]
    m = rows[0][0].shape[0]
    assert m % tm == 0
    nr, nc, no, na = len(rows), len(consts), len(out_rows), len(out_accs)

    def body(*refs):
        rr, cr = refs[:nr], refs[nr:nr + nc]
        orr, ar = refs[nr + nc:nr + nc + no], refs[nr + nc + no:]
        outs, accs = fn([r[...] for r in rr], cr)
        for o, v in zip(orr, outs, strict=True):
            o[...] = v.astype(o.dtype)
        if na:
            first = pl.program_id(0) == 0

            @pl.when(first)
            def _():
                for a, v in zip(ar, accs, strict=True):
                    a[...] = v

            @pl.when(jnp.logical_not(first))
            def _():
                for a, v in zip(ar, accs, strict=True):
                    a[...] += v

    def whole(shape):
        return pl.BlockSpec(shape, lambda i, n=len(shape): (0,) * n)

    in_specs = [pl.BlockSpec((tm, w), lambda i, cb=cb: (i, cb)) for _, w, cb in rows] + [whole(c.shape) for c in consts]
    out_specs = [pl.BlockSpec((tm, w), lambda i: (i, 0)) for w, _ in out_rows] + [whole(s) for s in out_accs]
    out_shape = [jax.ShapeDtypeStruct((m, w), dt) for w, dt in out_rows] + [jax.ShapeDtypeStruct(s, F32) for s in out_accs]
    res = pl.pallas_call(body, grid=(m // tm,), in_specs=in_specs, out_specs=out_specs, out_shape=out_shape, name=name,
                         compiler_params=_cparams(("arbitrary",)))(*[r[0] for r in rows], *consts)
    return res[:no], res[no:]


def _wgrad_call(name, x, dy, split, nblk, jb, blk_shape, tm, prologue=None):
    dys = list(dy) if isinstance(dy, (list, tuple)) else [dy]
    m = x.shape[0]
    kk, nn = blk_shape
    assert m % tm == 0 and nblk % jb == 0 and (len(dys) == 1 or (split == "y" and jb == nblk))
    xw = kk * jb if split == "x" else x.shape[1]
    yws = [d.shape[1] for d in dys] if (split == "x" or len(dys) > 1) else [nn * jb]
    nrow = m // tm

    def body(x_ref, *rest):
        dy_refs, o_ref, acc_ref = rest[:len(dys)], rest[-2], rest[-1]
        i = pl.program_id(1)

        @pl.when(i == 0)
        def _():
            acc_ref[...] = jnp.zeros_like(acc_ref)

        xv = x_ref[...]
        if prologue is not None:
            xv = prologue(xv)
        xv = xv.astype(MXU_DTYPE)
        dv = [r[...].astype(MXU_DTYPE) for r in dy_refs]
        dv = dv[0] if len(dv) == 1 else jnp.concatenate(dv, axis=-1)
        for j in range(jb):
            xa = xv[:, j * kk:(j + 1) * kk] if split == "x" else xv
            da = dv[:, j * nn:(j + 1) * nn] if split == "y" else dv
            acc_ref[j] += _mm_tn(xa, da)

        @pl.when(i == nrow - 1)
        def _():
            o_ref[...] = acc_ref[...].astype(o_ref.dtype)

    in_specs = [pl.BlockSpec((tm, xw), (lambda j, i: (i, j)) if split == "x" else (lambda j, i: (i, 0)))]
    in_specs += [pl.BlockSpec((tm, yw), (lambda j, i: (i, j)) if (split == "y" and len(dys) == 1) else (lambda j, i: (i, 0)))
                 for yw in yws]
    return pl.pallas_call(
        body, grid=(nblk // jb, nrow), in_specs=in_specs,
        out_specs=pl.BlockSpec((jb, kk, nn), lambda j, i: (j, 0, 0)),
        out_shape=jax.ShapeDtypeStruct((nblk, kk, nn), MXU_DTYPE),
        scratch_shapes=[pltpu.VMEM((jb, kk, nn), F32)], name=name,
        compiler_params=_cparams(("arbitrary", "arbitrary")))(x, *dys)


_HBM_SPEC = pl.BlockSpec(memory_space=pltpu.HBM)
_SEM_SPEC = pl.BlockSpec(memory_space=pltpu.SEMAPHORE)
_EFFECT = pltpu.SideEffectType.DATAFLOW_SIDE_EFFECTING


def _my_index():
    return 4 * lax.axis_index("x") + 2 * lax.axis_index("y") + lax.axis_index("c")


def _peer_copies(ins, lands, send_sems, recv_sems, modes):
    x, y, c = lax.axis_index("x"), lax.axis_index("y"), lax.axis_index("c")
    me = 4 * x + 2 * y + c
    pairs = []
    for k in range(NDEV - 1):
        fx, fy, fc = ((k + 1) >> 2) & 1, ((k + 1) >> 1) & 1, (k + 1) & 1
        px, py, pc = (x + fx) % 2, (y + fy) % 2, (c + fc) % 2
        idx = 4 * px + 2 * py + pc
        for a, md in enumerate(modes):
            src = ins[a] if md == "gather" else ins[a].at[idx]
            sem = a * (NDEV - 1) + k
            common = dict(src_ref=src, send_sem=send_sems.at[sem], recv_sem=recv_sems.at[sem], device_id=(px, py, pc),
                          device_id_type=pl.DeviceIdType.MESH)
            pairs.append((pltpu.make_async_remote_copy(dst_ref=lands[a].at[me], **common),
                          pltpu.make_async_remote_copy(dst_ref=lands[a].at[idx], **common)))
    return pairs


def _exchange_start(name, arrays, modes, after=None):
    n = len(arrays)
    extra = [] if after is None else [after]
    me = _my_index()
    lands = []
    for a, md in zip(arrays, modes, strict=True):
        piece = a if md == "gather" else lax.dynamic_index_in_dim(a, me, 0, keepdims=False)
        lands.append(lax.dynamic_update_index_in_dim(lax.empty((NDEV,) + piece.shape, a.dtype), piece, me, 0))

    def body(*refs):
        ins, lnd = refs[:n], refs[n:2 * n]
        send_sems, recv_sems = refs[2 * n + len(extra)], refs[2 * n + len(extra) + 1]
        token = refs[-1]
        for out_copy, _ in _peer_copies(ins, lnd, send_sems, recv_sems, modes):
            out_copy.start()
        token[...] = jnp.zeros_like(token)

    sems = pltpu.SemaphoreType.DMA((n * (NDEV - 1),))
    thru = [pltpu.HBM(a.shape, a.dtype) for a in list(arrays) + lands]
    res = pl.pallas_call(
        body, name=name, out_shape=(sems, sems, *thru, jax.ShapeDtypeStruct((SUBLANES, LANES), F32)),
        in_specs=[_HBM_SPEC] * (2 * n) + [pl.BlockSpec(memory_space=pl.ANY)] * len(extra),
        out_specs=(_SEM_SPEC, _SEM_SPEC, *([_HBM_SPEC] * (2 * n)), pl.BlockSpec(memory_space=pltpu.VMEM)),
        input_output_aliases={i: 2 + i for i in range(2 * n)},
        compiler_params=pltpu.CompilerParams(has_side_effects=_EFFECT),
    )(*[pltpu.with_memory_space_constraint(a, pltpu.HBM) for a in list(arrays) + lands], *extra)
    return (res[0], res[1], res[2:2 + n], res[2 + n:2 + 2 * n]), res[-1]


def _exchange_wait(name, handle, modes, after):
    send_sems, recv_sems, ins_thru, lands_thru = handle
    n = len(ins_thru)

    def body(*refs):
        ins, lnd = refs[:n], refs[n:2 * n]
        for out_copy, arrival in _peer_copies(ins, lnd, refs[2 * n], refs[2 * n + 1], modes):
            out_copy.wait_send()
            arrival.wait_recv()

    thru = [pltpu.HBM(a.shape, a.dtype) for a in list(ins_thru) + list(lands_thru)]
    res = pl.pallas_call(
        body, name=name, out_shape=tuple(thru),
        in_specs=[_HBM_SPEC] * (2 * n) + [_SEM_SPEC, _SEM_SPEC, pl.BlockSpec(memory_space=pl.ANY)],
        out_specs=tuple([_HBM_SPEC] * (2 * n)), input_output_aliases={i: i for i in range(2 * n)},
        compiler_params=pltpu.CompilerParams(has_side_effects=_EFFECT),
    )(*ins_thru, *lands_thru, send_sems, recv_sems, after)
    return res[n:]


def _rope_tables(positions3, inv_freq_row, sign_row):
    b, s, _ = positions3.shape

    def body(pos_ref, f_ref, sg_ref, c_ref, s_ref):
        ang = pos_ref[...].astype(F32) * f_ref[...]
        c_ref[...] = jnp.cos(ang)
        s_ref[...] = jnp.sin(ang) * sg_ref[...]

    row = pl.BlockSpec((1, LANES), lambda i: (0, 0))
    blk = pl.BlockSpec((None, s, LANES), lambda i: (i, 0, 0))
    return pl.pallas_call(
        body, grid=(b,), in_specs=[pl.BlockSpec((None, s, 1), lambda i: (i, 0, 0)), row, row], out_specs=[blk, blk],
        out_shape=[jax.ShapeDtypeStruct((b, s, LANES), F32)] * 2, name="rope_tables",
        compiler_params=_cparams(("arbitrary",)))(positions3, inv_freq_row, sign_row)


def _swap_halves(t):
    lane = lax.broadcasted_iota(jnp.int32, t.shape, 1)
    half = HEAD_DIM // 2
    return jnp.where((lane % HEAD_DIM) < half, pltpu.roll(t, LANES - half, 1), pltpu.roll(t, half, 1))


def _segment_rows(r, d, s):
    n = s // d
    return (pl.ds(r, n, stride=d) if d > 1 else pl.ds(0, s)), pl.ds(r * n, n)


def _head_lanes(shape):
    lane = lax.broadcasted_iota(jnp.int32, shape, len(shape) - 1)
    return lane < HEAD_DIM


def _bmm_nt(a, b):
    return lax.dot_general(a.astype(MXU_DTYPE), b.astype(MXU_DTYPE), (((2,), (2,)), ((0,), (0,))), preferred_element_type=F32)


def _bmm(a, b):
    return lax.dot_general(a.astype(MXU_DTYPE), b.astype(MXU_DTYPE), (((2,), (1,)), ((0,), (0,))), preferred_element_type=F32)


def _bmm_tn(a, b):
    return lax.dot_general(a.astype(MXU_DTYPE), b.astype(MXU_DTYPE), (((1,), (1,)), ((0,), (0,))), preferred_element_type=F32)


def _stack_heads(t3):
    head_a = _head_lanes(t3.shape)
    zero = jnp.zeros_like(t3)
    return jnp.concatenate([jnp.where(head_a, t3, zero), jnp.where(head_a, zero, t3)], axis=1)


QUNIT = QBLK // 2


def _with_previous(t3, nprev):
    shifted = [jnp.concatenate([t3[:k], t3[:-k]], axis=0) for k in range(nprev, 0, -1)]
    return jnp.concatenate(shifted + [t3], axis=1)


def _head_columns(t3):
    return jnp.concatenate([t3[:, :, lo:lo + 1] for lo in range(0, LANES, HEAD_DIM)], axis=1)


def _to_own_unit(t, nprev):
    unit = t.shape[1] // (nprev + 1)
    out = t[:, nprev * unit:]
    for k in range(1, nprev + 1):
        part = t[:, (nprev - k) * unit:(nprev - k + 1) * unit]
        out = out + jnp.concatenate([part[k:], jnp.zeros_like(part[:k])], axis=0)
    return out


def _branch_operands(qh, kh, vh, s, nb):
    nh = LANES // HEAD_DIM
    unit, nprev = (QBLK, 0) if nb == 1 else (QUNIT, QBLK // QUNIT)
    g = s // unit
    q3 = _stack_heads(qh[...].reshape(g, unit, LANES))
    k3, v3 = kh[...].reshape(g, unit, LANES), vh[...].reshape(g, unit, LANES)
    if nprev == 0:
        qi = lax.broadcasted_iota(jnp.int32, (1, nh * unit, unit), 1) % unit
        kj = lax.broadcasted_iota(jnp.int32, (1, nh * unit, unit), 2)
        return q3, k3, v3, kj <= qi, nprev
    shape = (g, nh * unit, (nprev + 1) * unit)
    qi = lax.broadcasted_iota(jnp.int32, shape, 1) % unit
    kj = lax.broadcasted_iota(jnp.int32, shape, 2)
    j = lax.broadcasted_iota(jnp.int32, shape, 0)
    per_block = QBLK // unit
    opens = ((j // per_block) % nb) == 0
    mask = (kj >= qi) & (kj <= qi + QBLK) & ((kj >= QBLK - unit * (j % per_block)) | jnp.logical_not(opens))
    return q3, _with_previous(k3, nprev), _with_previous(v3, nprev), mask, nprev


def _attn_fwd(proj3, ctab, stab, aw):
    b, s, _ = proj3.shape
    npair = aw // LANES
    scale = HEAD_DIM ** -0.5
    nbr = len(DILATION_PAIRS)

    def body(q_ref, k_ref, v_ref, c_ref, s_ref, o_ref, l_ref, qf, kf, vf, qh, kh, vh, op, lp, ob, lb):
        cc, ss = c_ref[...], s_ref[...]
        q2, k2 = q_ref[...], k_ref[...]
        qf[...] = (q2 * cc + _swap_halves(q2) * ss) * scale
        kf[...] = k2 * cc + _swap_halves(k2) * ss
        vf[...] = v_ref[...]
        for br, (window, d) in enumerate(DILATION_PAIRS):
            for r in range(d):
                nat, perm = _segment_rows(r, d, s)
                for dst, src in ((qh, qf), (kh, kf), (vh, vf)):
                    dst[perm, :] = src[nat, :].astype(MXU_DTYPE)
            o_dst, l_dst = (ob.at[br], lb.at[br]) if d == 1 else (op, lp)
            q3, kk, vv, mask, _ = _branch_operands(qh, kh, vh, s, (s // d) // QBLK)
            unit = q3.shape[1] // (LANES // HEAD_DIM)
            head_a = _head_lanes((q3.shape[0], unit, LANES))
            sc = jnp.where(mask, _bmm_nt(q3, kk), NEG_INF)
            mx = jnp.max(sc, axis=-1, keepdims=True)
            p = jnp.exp(sc - mx)
            den = jnp.sum(p, axis=-1, keepdims=True)
            o2 = _bmm(p, vv) / den
            l2 = mx + jnp.log(den)
            o_dst[...] = jnp.where(head_a, o2[:, :unit], o2[:, unit:]).reshape(s, LANES)
            l_dst[...] = jnp.where(head_a, l2[:, :unit], l2[:, unit:]).reshape(s, LANES)
            if d > 1:
                for r in range(d):
                    nat, perm = _segment_rows(r, d, s)
                    ob[br, nat, :] = op[perm, :]
                    lb[br, nat, :] = lp[perm, :]
        ls = [lb[br] for br in range(nbr)]
        mx = functools.reduce(jnp.maximum, ls)
        ws = [jnp.exp(l - mx) for l in ls]
        tot = functools.reduce(lambda a, b_: a + b_, ws)
        o_ref[...] = functools.reduce(lambda a, b_: a + b_, [(w / tot) * ob[br] for br, w in enumerate(ws)])
        l_ref[...] = mx + jnp.log(tot)

    def col(off):
        return pl.BlockSpec((None, s, LANES), lambda bi, hp, off=off: (bi, 0, off + hp))

    tab = pl.BlockSpec((None, s, LANES), lambda bi, hp: (bi, 0, 0))
    f32s = pltpu.VMEM((s, LANES), F32)
    mxs = pltpu.VMEM((s, LANES), MXU_DTYPE)
    br_s = pltpu.VMEM((nbr, s, LANES), F32)
    return pl.pallas_call(
        body, grid=(b, npair), in_specs=[col(0), col(npair), col(2 * npair), tab, tab], out_specs=[col(0), col(0)],
        out_shape=[jax.ShapeDtypeStruct((b, s, aw), F32)] * 2,
        scratch_shapes=[f32s] * 3 + [mxs] * 3 + [f32s] * 2 + [br_s] * 2,
        name="attn_fwd", compiler_params=_cparams(("arbitrary", "arbitrary")))(proj3, proj3, proj3, ctab, stab)


def _attn_bwd(proj3, ctab, stab, dout3, out3, lse3, aw):
    b, s, _ = proj3.shape
    npair = aw // LANES
    scale = HEAD_DIM ** -0.5
    nheads = LANES // HEAD_DIM

    def body(q_ref, k_ref, v_ref, c_ref, s_ref, do_ref, o_ref, l_ref, dq_ref, dk_ref, dv_ref,
             qf, kf, vf, dof, dlf, qh, kh, vh, doh, lpm, dpm, dqp, dkp, dvp, dqn, dkn, dvn):
        cc, ss = c_ref[...], s_ref[...]
        q2, k2 = q_ref[...], k_ref[...]
        qf[...] = (q2 * cc + _swap_halves(q2) * ss) * scale
        kf[...] = k2 * cc + _swap_halves(k2) * ss
        vf[...] = v_ref[...]
        do2 = do_ref[...]
        dof[...] = do2
        dd = do2 * o_ref[...]
        in_a = _head_lanes((s, LANES))
        sum_a = jnp.sum(jnp.where(in_a, dd, 0.0), axis=-1, keepdims=True)
        sum_b = jnp.sum(jnp.where(in_a, 0.0, dd), axis=-1, keepdims=True)
        dlf[...] = jnp.where(in_a, sum_a, sum_b)
        for r_ in (dqn, dkn, dvn):
            r_[...] = jnp.zeros_like(r_)
        for window, d in DILATION_PAIRS:
            for r in range(d):
                nat, perm = _segment_rows(r, d, s)
                for dst, src in ((qh, qf), (kh, kf), (vh, vf), (doh, dof)):
                    dst[perm, :] = src[nat, :].astype(MXU_DTYPE)
                if d > 1:
                    lpm[perm, :] = l_ref[nat, :]
                    dpm[perm, :] = dlf[nat, :]
            l_src, d_src = (l_ref, dlf) if d == 1 else (lpm, dpm)
            q3, kk, vv, mask, nprev = _branch_operands(qh, kh, vh, s, (s // d) // QBLK)
            g_, unit = q3.shape[0], q3.shape[1] // nheads
            head_a = _head_lanes((g_, unit, LANES))
            do3 = _stack_heads(doh[...].reshape(g_, unit, LANES))
            lcol, dcol = _head_columns(l_src[...].reshape(g_, unit, LANES)), _head_columns(d_src[...].reshape(g_, unit, LANES))
            p = jnp.exp(jnp.where(mask, _bmm_nt(q3, kk), NEG_INF) - lcol)
            ds_ = p * (_bmm_nt(do3, vv) - dcol)
            dq2 = _bmm(ds_, kk)
            dq_new = jnp.where(head_a, dq2[:, :unit], dq2[:, unit:]).reshape(s, LANES)
            dk_new = _to_own_unit(_bmm_tn(ds_, q3), nprev).reshape(s, LANES)
            dv_new = _to_own_unit(_bmm_tn(p, do3), nprev).reshape(s, LANES)
            if d == 1:
                dqn[...] += dq_new
                dkn[...] += dk_new
                dvn[...] += dv_new
            else:
                dqp[...] = dq_new
                dkp[...] = dk_new
                dvp[...] = dv_new
                for r in range(d):
                    nat, perm = _segment_rows(r, d, s)
                    dqn[nat, :] += dqp[perm, :]
                    dkn[nat, :] += dkp[perm, :]
                    dvn[nat, :] += dvp[perm, :]
        g = dqn[...] * scale
        dq_ref[...] = g * cc + _swap_halves(g * ss)
        g = dkn[...]
        dk_ref[...] = g * cc + _swap_halves(g * ss)
        dv_ref[...] = dvn[...]

    def col(off):
        return pl.BlockSpec((None, s, LANES), lambda bi, hp, off=off: (bi, 0, off + hp))

    tab = pl.BlockSpec((None, s, LANES), lambda bi, hp: (bi, 0, 0))
    f32s = pltpu.VMEM((s, LANES), F32)
    mxs = pltpu.VMEM((s, LANES), MXU_DTYPE)
    return pl.pallas_call(
        body, grid=(b, npair), in_specs=[col(0), col(npair), col(2 * npair), tab, tab, col(0), col(0), col(0)],
        out_specs=[col(0)] * 3, out_shape=[jax.ShapeDtypeStruct((b, s, aw), F32)] * 3,
        scratch_shapes=[f32s] * 5 + [mxs] * 4 + [f32s] * 8,
        name="attn_bwd", compiler_params=_cparams(("arbitrary", "arbitrary")))(
            proj3, proj3, proj3, ctab, stab, dout3, out3, lse3)


def _s5_discretise(a_re, a_im, log_dt, bt_re, bt_im):
    dt = jnp.exp(log_dt)
    mag = jnp.exp(a_re * dt)
    ang = a_im * dt
    lb_re = mag * jnp.cos(ang)
    lb_im = mag * jnp.sin(ang)
    den = a_re * a_re + a_im * a_im
    nr = lb_re - 1.0
    ni = lb_im
    cr = (nr * a_re + ni * a_im) / den
    ci = (ni * a_re - nr * a_im) / den
    return lb_re, lb_im, cr * bt_re - ci * bt_im, cr * bt_im + ci * bt_re


def _s5_params(a_re, a_im, log_dt, bt_re, bt_im):
    def body(ar, ai, ld, br, bi, o1, o2, o3, o4):
        r = _s5_discretise(ar[...], ai[...], ld[...], br[...], bi[...])
        for o, v in zip((o1, o2, o3, o4), r, strict=True):
            o[...] = v

    sd = jax.ShapeDtypeStruct
    return pl.pallas_call(body, out_shape=[sd(a_re.shape, F32)] * 2 + [sd(bt_re.shape, F32)] * 2, name="s5_params")(
        a_re, a_im, log_dt, bt_re, bt_im)


def _s5_params_bwd(a_re, a_im, log_dt, bt_re, bt_im, cts):
    def body(ar, ai, ld, br, bi, c1, c2, c3, c4, o1, o2, o3, o4, o5):
        _, vjp = jax.vjp(_s5_discretise, ar[...], ai[...], ld[...], br[...], bi[...])
        r = vjp((c1[...], c2[...], c3[...], c4[...]))
        for o, v in zip((o1, o2, o3, o4, o5), r, strict=True):
            o[...] = v

    sd = jax.ShapeDtypeStruct
    return pl.pallas_call(
        body, out_shape=[sd(a_re.shape, F32)] * 2 + [sd(log_dt.shape, F32)] + [sd(bt_re.shape, F32)] * 2, name="s5_params_bwd")(
            a_re, a_im, log_dt, bt_re, bt_im, *cts)


S5_TC = 128


def _time_major(tiles):
    nt, tc, lanes = tiles.shape
    return jnp.swapaxes(tiles, 0, 1).reshape(tc * nt, lanes)


def _tile_major(rows, nt):
    return jnp.swapaxes(rows.astype(MXU_DTYPE).reshape(rows.shape[0] // nt, nt, rows.shape[1]), 0, 1)


def _s5_fwd(proj3, ucol, lam_re, lam_im, bre, bim, cre, cim, dvec, sw):
    b, s, _ = proj3.shape
    nt = lam_re.shape[0]
    ncb = sw // LANES
    tpc = nt // ncb
    tc = S5_TC

    def body(u_ref, lr_ref, li_ref, bre_ref, bim_ref, cre_ref, cim_ref, d_ref, y_ref, xr_ref, xi_ref, sr, si):
        @pl.when(pl.program_id(0) == 0)
        def _():
            sr[...] = jnp.zeros_like(sr)
            si[...] = jnp.zeros_like(si)

        u_all = u_ref[...].reshape(b * tc, sw)
        bur, bui = [], []
        for cb in range(ncb):
            ucb = u_all[:, cb * LANES:(cb + 1) * LANES].astype(MXU_DTYPE)
            for t in range(cb * tpc, (cb + 1) * tpc):
                bur.append(_mm(ucb, bre_ref[t]))
                bui.append(_mm(ucb, bim_ref[t]))
        for bi in range(b):
            rows = slice(bi * tc, (bi + 1) * tc)
            xr_ref[bi] = _time_major(jnp.stack([v_[rows] for v_ in bur]))
            xi_ref[bi] = _time_major(jnp.stack([v_[rows] for v_ in bui]))
        lr, li = lr_ref[...], li_ref[...]

        def step(j, carry):
            off = pl.multiple_of(j * nt, nt)
            new = []
            for bi in range(b):
                pr, pi = carry[2 * bi], carry[2 * bi + 1]
                nr = lr * pr - li * pi + xr_ref[bi, pl.ds(off, nt), :]
                ni = lr * pi + li * pr + xi_ref[bi, pl.ds(off, nt), :]
                xr_ref[bi, pl.ds(off, nt), :] = nr
                xi_ref[bi, pl.ds(off, nt), :] = ni
                new += [nr, ni]
            return tuple(new)

        init = tuple(v for bi in range(b) for v in (sr[bi], si[bi]))
        fin = lax.fori_loop(0, tc, step, init, unroll=4)
        for bi in range(b):
            sr[bi] = fin[2 * bi]
            si[bi] = fin[2 * bi + 1]
        xr_t = [_tile_major(xr_ref[bi], nt) for bi in range(b)]
        xi_t = [_tile_major(xi_ref[bi], nt) for bi in range(b)]
        for cb in range(ncb):
            cols = slice(cb * LANES, (cb + 1) * LANES)
            acc = d_ref[:, cols] * u_all[:, cols]
            for t in range(cb * tpc, (cb + 1) * tpc):
                xr_all = jnp.concatenate([xr_t[bi][t] for bi in range(b)], axis=0)
                xi_all = jnp.concatenate([xi_t[bi][t] for bi in range(b)], axis=0)
                acc = acc + (_mm(xr_all, cre_ref[t]) - _mm(xi_all, cim_ref[t]))
            for bi in range(b):
                y_ref[bi, :, cols] = acc[bi * tc:(bi + 1) * tc]

    def whole(a):
        return pl.BlockSpec(a.shape, lambda c, n=a.ndim: (0,) * n)

    xblk = pl.BlockSpec((b, tc * nt, LANES), lambda c: (0, c, 0))
    return pl.pallas_call(
        body, grid=(s // tc,),
        in_specs=[pl.BlockSpec((b, tc, sw), lambda c: (0, c, ucol))] + [whole(a) for a in (lam_re, lam_im, bre, bim, cre, cim, dvec)],
        out_specs=[pl.BlockSpec((b, tc, sw), lambda c: (0, c, 0)), xblk, xblk],
        out_shape=[jax.ShapeDtypeStruct((b, s, sw), F32)] + [jax.ShapeDtypeStruct((b, s * nt, LANES), F32)] * 2,
        scratch_shapes=[pltpu.VMEM((b, nt, LANES), F32)] * 2, name="s5_fwd",
        compiler_params=_cparams(("arbitrary",)))(proj3, lam_re, lam_im, bre, bim, cre, cim, dvec)


def _s5_bwd(proj3, ucol, dy3, xr3, xi3, lam_re, lam_im, bre, bim, cre, cim, dvec, sw):
    b, s, _ = proj3.shape
    nt = lam_re.shape[0]
    ncb = sw // LANES
    tpc = nt // ncb
    tc = S5_TC
    nchunk = s // tc

    def body(u_ref, dy_ref, xr_ref, xi_ref, pr_ref, pi_ref, lr_ref, li_ref, bre_ref, bim_ref, cre_ref, cim_ref, d_ref,
             du_ref, dbre, dbim, dcre, dcim, dlr, dli, dd, gr, gi, sr, si):
        step_id = pl.program_id(0)

        @pl.when(step_id == 0)
        def _():
            for r in (sr, si, dbre, dbim, dcre, dcim, dlr, dli, dd):
                r[...] = jnp.zeros_like(r)

        u_all = u_ref[...].reshape(b * tc, sw)
        dy_all = dy_ref[...].reshape(b * tc, sw)
        dxr, dxi = [], []
        for cb in range(ncb):
            dycb = dy_all[:, cb * LANES:(cb + 1) * LANES].astype(MXU_DTYPE)
            for t in range(cb * tpc, (cb + 1) * tpc):
                dxr.append(_mm_nt(dycb, cre_ref[t]))
                dxi.append(-_mm_nt(dycb, cim_ref[t]))
        for bi in range(b):
            rows = slice(bi * tc, (bi + 1) * tc)
            gr[bi] = _time_major(jnp.stack([v_[rows] for v_ in dxr]))
            gi[bi] = _time_major(jnp.stack([v_[rows] for v_ in dxi]))
        lr, li = lr_ref[...], li_ref[...]

        def step(jj, carry):
            off = pl.multiple_of((tc - 1 - jj) * nt, nt)
            new = []
            for bi in range(b):
                nr_, ni_ = carry[2 * bi], carry[2 * bi + 1]
                vr = gr[bi, pl.ds(off, nt), :] + lr * nr_ + li * ni_
                vi = gi[bi, pl.ds(off, nt), :] + lr * ni_ - li * nr_
                gr[bi, pl.ds(off, nt), :] = vr
                gi[bi, pl.ds(off, nt), :] = vi
                new += [vr, vi]
            return tuple(new)

        init = tuple(v for bi in range(b) for v in (sr[bi], si[bi]))
        fin = lax.fori_loop(0, tc, step, init, unroll=4)
        for bi in range(b):
            sr[bi] = fin[2 * bi]
            si[bi] = fin[2 * bi + 1]

        has_prev = (step_id != nchunk - 1).astype(F32)
        rest = (tc - 1) * nt
        alr = jnp.zeros((nt, LANES), F32)
        ali = jnp.zeros((nt, LANES), F32)
        for bi in range(b):
            s0r, s0i = gr[bi, pl.ds(0, nt), :], gi[bi, pl.ds(0, nt), :]
            x0r, x0i = pr_ref[bi] * has_prev, pi_ref[bi] * has_prev
            alr += s0r * x0r + s0i * x0i
            ali += s0i * x0r - s0r * x0i
            s1r, s1i = gr[bi, pl.ds(nt, rest), :], gi[bi, pl.ds(nt, rest), :]
            x1r, x1i = xr_ref[bi, pl.ds(0, rest), :], xi_ref[bi, pl.ds(0, rest), :]
            alr += jnp.sum((s1r * x1r + s1i * x1i).reshape(tc - 1, nt, LANES), axis=0)
            ali += jnp.sum((s1i * x1r - s1r * x1i).reshape(tc - 1, nt, LANES), axis=0)
        dlr[...] += alr
        dli[...] += ali

        tiles = [[_tile_major(ref[bi], nt) for bi in range(b)] for ref in (gr, gi, xr_ref, xi_ref)]

        def stacked(k, t):
            return jnp.concatenate([tiles[k][bi][t] for bi in range(b)], axis=0)

        for cb in range(ncb):
            cols = slice(cb * LANES, (cb + 1) * LANES)
            ucb32, dycb32 = u_all[:, cols], dy_all[:, cols]
            ucb, dycb = ucb32.astype(MXU_DTYPE), dycb32.astype(MXU_DTYPE)
            acc = d_ref[:, cols] * dycb32
            for t in range(cb * tpc, (cb + 1) * tpc):
                vr, vi = stacked(0, t), stacked(1, t)
                acc = acc + (_mm_nt(vr, bre_ref[t]) + _mm_nt(vi, bim_ref[t]))
                dbre[t] += _mm_tn(ucb, vr)
                dbim[t] += _mm_tn(ucb, vi)
                dcre[t] += _mm_tn(stacked(2, t), dycb)
                dcim[t] -= _mm_tn(stacked(3, t), dycb)
            for bi in range(b):
                du_ref[bi, :, cols] = acc[bi * tc:(bi + 1) * tc]
            dd[:, cols] += _colsum(dycb32 * ucb32)

    def whole(a):
        return pl.BlockSpec(a.shape, lambda c, n=len(a.shape): (0,) * n)

    def rev(c):
        return nchunk - 1 - c

    xblk = pl.BlockSpec((b, tc * nt, LANES), lambda c: (0, rev(c), 0))
    prev = pl.BlockSpec((b, nt, LANES), lambda c: (0, jnp.maximum(rev(c) * tc - 1, 0), 0))
    sd = jax.ShapeDtypeStruct
    blk = sd(bre.shape, F32)
    acc_shapes = [blk, blk, sd(cre.shape, F32), sd(cre.shape, F32), sd(lam_re.shape, F32), sd(lam_re.shape, F32), sd((1, sw), F32)]
    return pl.pallas_call(
        body, grid=(nchunk,),
        in_specs=[pl.BlockSpec((b, tc, sw), lambda c: (0, rev(c), ucol)), pl.BlockSpec((b, tc, sw), lambda c: (0, rev(c), 0)),
                  xblk, xblk, prev, prev] + [whole(a) for a in (lam_re, lam_im, bre, bim, cre, cim, dvec)],
        out_specs=[pl.BlockSpec((b, tc, sw), lambda c: (0, rev(c), 0))] + [whole(a) for a in acc_shapes],
        out_shape=[sd((b, s, sw), F32)] + acc_shapes,
        scratch_shapes=[pltpu.VMEM((b, tc * nt, LANES), F32)] * 2 + [pltpu.VMEM((b, nt, LANES), F32)] * 2, name="s5_bwd",
        compiler_params=_cparams(("arbitrary",)))(proj3, dy3, xr3, xi3, xr3, xi3, lam_re, lam_im, bre, bim, cre, cim, dvec)


def _s5_blocks(bb, cc_, ncb):
    g, n, p = bb.shape
    mask = jnp.asarray(_s5_tile_mask(g, p, ncb))
    nt, gpt, gpc = mask.shape
    bbt, cct = bb.reshape(nt, gpt, n, p), cc_.reshape(nt, gpt, n, p)
    spread = mask[:, :, :, None, None]
    bblk = (bbt[:, :, None] * spread).transpose(0, 2, 3, 1, 4).reshape(nt, gpc * n, gpt * p)
    cblk = (cct[:, :, None] * spread).transpose(0, 1, 4, 2, 3).reshape(nt, gpt * p, gpc * n)
    return bblk, cblk


def _s5_tile_mask(g, p, ncb):
    nt, gpt, gpc = g * p // LANES, LANES // p, g // ncb
    mask = np.zeros((nt, gpt, gpc), np.float32)
    for t in range(nt):
        for gl in range(gpt):
            mask[t, gl, (t * gpt + gl) % gpc] = 1.0
    return mask


def _s5_unblock(dbblk, dcblk, g, n, p, ncb):
    mask = jnp.asarray(_s5_tile_mask(g, p, ncb))
    nt, gpt, gpc = mask.shape
    db = jnp.sum(dbblk.reshape(nt, gpc, n, gpt, p) * mask.transpose(0, 2, 1)[:, :, None, :, None], axis=1)
    dc = jnp.sum(dcblk.reshape(nt, gpt, p, gpc, n) * mask[:, :, None, :, None], axis=3)
    return db.transpose(0, 2, 1, 3).reshape(g, n, p), dc.transpose(0, 1, 3, 2).reshape(g, n, p)


def _adamw_math(w, g, m, v):
    m = ADAM_B1 * m + (1.0 - ADAM_B1) * g
    v = ADAM_B2 * v + (1.0 - ADAM_B2) * (g * g)
    m_hat = m / (1.0 - ADAM_B1 ** ADAM_STEP)
    v_hat = v / (1.0 - ADAM_B2 ** ADAM_STEP)
    delta = -ADAM_LR * (m_hat / (jnp.sqrt(v_hat) + ADAM_EPS) + ADAM_WD * w)
    return delta, m, v


def _adamw_layer(name, parts, w, m, v, layer, prev, tr):
    nparts, r, c = parts.shape
    assert r % tr == 0
    if prev is None:
        prev = [lax.empty(w.shape, F32) for _ in range(4)]

    def body(p_ref, w_ref, m_ref, v_ref, *rest):
        g_out, d_out, m_out, v_out = rest[4:]
        g = p_ref[0].astype(F32)
        for k in range(1, nparts):
            g = g + p_ref[k].astype(F32)
        delta, mn, vn = _adamw_math(w_ref[...], g, m_ref[...], v_ref[...])
        g_out[...] = g
        d_out[...] = delta
        m_out[...] = mn
        v_out[...] = vn

    blk = pl.BlockSpec((None, tr, c), lambda i: (layer, i, 0))
    kept = pl.BlockSpec(memory_space=pl.ANY)
    return pl.pallas_call(
        body, grid=(r // tr,), in_specs=[pl.BlockSpec((nparts, tr, c), lambda i: (0, i, 0)), blk, blk, blk] + [kept] * 4,
        out_specs=[blk] * 4, out_shape=[jax.ShapeDtypeStruct(w.shape, F32)] * 4, name=name,
        input_output_aliases={4 + k: k for k in range(4)},
        compiler_params=_cparams(("arbitrary",)))(parts, w, m, v, *prev)


def _pack_small(pieces):
    flat = jnp.concatenate([p.reshape(-1) for p in pieces])
    n = flat.shape[0]
    unit = SUBLANES * LANES
    padded = -(-n // unit) * unit
    return jnp.pad(flat, (0, padded - n)).reshape(padded // LANES, LANES)


def _pack_small_layers(pieces):
    nl = pieces[0].shape[0]
    flat = jnp.concatenate([p.reshape(nl, -1) for p in pieces], axis=1)
    n = flat.shape[1]
    unit = SUBLANES * LANES
    padded = -(-n // unit) * unit
    return jnp.pad(flat, ((0, 0), (0, padded - n))).reshape(nl, padded // LANES, LANES)


def _step(x, positions, weights, moments_m, moments_v, loss_target, distributed):
    f32 = F32
    bsz, seq, dm = x.shape
    depth = weights["w_in"].shape[0]
    aw = weights["attn_gain"].shape[1]
    sw = weights["ssm_gain"].shape[1]
    dff = weights["b_ff1"].shape[1]
    ng, npst = weights["ssm_a_re"].shape[1:]
    gdim = weights["ssm_d"].shape[2]
    in_w = 3 * aw + sw
    t_rows = bsz * seq
    alpha = (2.0 * depth) ** 0.25
    ncb = sw // LANES
    nt = ng * npst // LANES
    ndev = NDEV if distributed else 1
    tm = tl = 512

    gather_modes = ["gather"] * len(BIG_NAMES)
    no_token = jnp.zeros((SUBLANES, LANES), f32)

    def shards(l, names):
        return [weights[n][l].astype(MXU_DTYPE) for n in names]

    if distributed:
        first, _ = _exchange_start("weights_start_l0_in", shards(0, BIG_NAMES[:1]), gather_modes[:1])
        (g_in0,) = _exchange_wait("weights_wait_l0_in", first, gather_modes[:1], positions)
        rest0, tok_rest0 = _exchange_start("weights_start_l0_rest", shards(0, BIG_NAMES[1:]), gather_modes[1:], after=g_in0)

    half = HEAD_DIM // 2
    inv_freq = ROPE_THETA ** (-jnp.arange(half, dtype=f32) * 2.0 / HEAD_DIM)
    reps = LANES // half
    inv_row = jnp.tile(inv_freq, reps)[None, :]
    sign_row = jnp.tile(jnp.concatenate([-jnp.ones((half,), f32), jnp.ones((half,), f32)]), LANES // HEAD_DIM)[None, :]
    ctab, stab = _rope_tables(positions[..., None], inv_row, sign_row)

    def row(v):
        return v.reshape(1, -1)

    h = x.reshape(t_rows, dm)
    saved = []
    for l in range(depth):
        tok_in = tok_mix = no_token
        if not distributed:
            g_in, g_glu, g_out, g_ff1, g_ff2 = [weights[n][l].astype(MXU_DTYPE)[None] for n in BIG_NAMES]
        elif l == 0:
            g_in, tok_in = g_in0, tok_rest0
        else:
            g_in, g_glu, g_out, g_ff1, g_ff2 = _exchange_wait(f"weights_wait_l{l}", next_gather, gather_modes, h)
            if l + 1 < depth:
                next_gather, tok_in = _exchange_start(f"weights_start_l{l + 1}", shards(l + 1, BIG_NAMES), gather_modes, after=g_in)
        w_in_l = g_in

        def in_proj(rv, cr):
            return [jnp.concatenate([_mm(rv[0], cr[0][j]) for j in range(ndev)], axis=-1)], []

        (proj,), _ = _rows_call("in_proj", in_proj, [h], [w_in_l, tok_in], [(in_w, f32)], [], tl)
        proj3 = proj.reshape(bsz, seq, in_w)
        attn3, lse3 = _attn_fwd(proj3, ctab, stab, aw)
        if distributed and l == 0:
            g_glu, g_out, g_ff1, g_ff2 = _exchange_wait("weights_wait_l0_rest", rest0, gather_modes[1:], attn3)
            if depth > 1:
                next_gather, tok_mix = _exchange_start("weights_start_l1", shards(1, BIG_NAMES), gather_modes, after=g_glu)
        w_glu_l = g_glu.reshape(sw, sw)
        w_out_l = g_out.reshape(dm, dm)
        w_ff1_l = g_ff1
        w_ff2_l = g_ff2.reshape(dff, dm)

        a_re, a_im = weights["ssm_a_re"][l][:, None, :], weights["ssm_a_im"][l][:, None, :]
        log_dt = weights["ssm_log_dt"][l][:, None, None]
        bt_re = weights["ssm_b_re"][l].transpose(0, 2, 1)
        bt_im = weights["ssm_b_im"][l].transpose(0, 2, 1)
        lb_re, lb_im, bb_re, bb_im = _s5_params(a_re, a_im, log_dt, bt_re, bt_im)
        lam_re, lam_im = lb_re.reshape(nt, LANES), lb_im.reshape(nt, LANES)
        bre, cre = _s5_blocks(bb_re, weights["ssm_c_re"][l], ncb)
        bim, cim = _s5_blocks(bb_im, weights["ssm_c_im"][l], ncb)
        s5c = [a_.astype(MXU_DTYPE) for a_ in (bre, bim, cre, cim)]
        dvec = row(weights["ssm_d"][l])
        ypre3, xr3, xi3 = _s5_fwd(proj3, 3 * aw // sw, lam_re, lam_im, *s5c, dvec, sw)
        attn, ypre = attn3.reshape(t_rows, aw), ypre3.reshape(t_rows, sw)

        b_glu, ga, gs = row(weights["b_glu"][l]), row(weights["attn_gain"][l]), row(weights["ssm_gain"][l])

        def mix(rv, cr):
            at, yp = rv
            g = _gelu(yp)
            ssm = g * jax.nn.sigmoid(_mm(g, cr[0][...]) + cr[1][...])
            return [jnp.concatenate([_rms_norm(at, cr[2][...]), _rms_norm(ssm, cr[3][...])], axis=-1)], []

        (mixed,), _ = _rows_call("mix", mix, [attn, ypre], [w_glu_l, b_glu, ga, gs, tok_mix], [(dm, MXU_DTYPE)], [], tl)

        b_out, g1, b1 = row(weights["b_out"][l]), row(weights["ln1_g"][l]), row(weights["ln1_b"][l])

        def out_proj(rv, cr):
            pre = alpha * rv[1] + _mm(rv[0], cr[0][...]) + cr[1][...]
            return [pre, _layer_norm(pre, cr[2][...], cr[3][...])], []

        (pre1, h1), _ = _rows_call("out_proj", out_proj, [mixed, h], [w_out_l, b_out, g1, b1], [(dm, f32), (dm, f32)], [], tl)

        b_ff1 = row(weights["b_ff1"][l])

        def ff1(rv, cr):
            pre_act = jnp.concatenate([_mm(rv[0], cr[0][j]) for j in range(ndev)], axis=-1) + cr[1][...]
            return [jnp.square(jnp.maximum(pre_act, 0.0))], []

        (act,), _ = _rows_call("ff1", ff1, [h1], [w_ff1_l, b_ff1], [(dff, MXU_DTYPE)], [], tm)

        b_ff2, g2, b2 = row(weights["b_ff2"][l]), row(weights["ln2_g"][l]), row(weights["ln2_b"][l])

        def ff2(rv, cr):
            pre = alpha * rv[1] + _mm(rv[0], cr[0][...]) + cr[1][...]
            return [pre, _layer_norm(pre, cr[2][...], cr[3][...])], []

        (pre2, h2), _ = _rows_call("ff2", ff2, [act, h1], [w_ff2_l, b_ff2, g2, b2], [(dm, f32), (dm, f32)], [], tm)

        saved.append(dict(h=h, proj3=proj3, attn=attn, lse3=lse3, attn3=attn3, ypre=ypre, xr3=xr3, xi3=xi3, mixed=mixed, pre1=pre1,
                          h1=h1, act=act, pre2=pre2, w_in=w_in_l, w_glu=w_glu_l, w_out=w_out_l, w_ff1=w_ff1_l, w_ff2=w_ff2_l,
                          lam=(lam_re, lam_im), s5c=s5c, dvec=dvec, b_glu=b_glu, ga=ga, gs=gs, g1=g1, g2=g2,
                          s5in=(a_re, a_im, log_dt, bt_re, bt_im)))
        h = h2

    g2_last = saved[-1]["g2"]

    def loss_fn(rv, cr):
        y, tgt, pre = rv
        err = y - tgt
        part = 0.5 * jnp.sum(jnp.mean(err * err, axis=-1, keepdims=True), axis=0, keepdims=True)
        dpre, dg, db = _layer_norm_bwd(err * (1.0 / dm), pre, cr[0][...])
        return [dpre], [jnp.broadcast_to(part, (1, LANES)), dg, db, _colsum(dpre)]

    (dpre2,), (loss_acc, dg2, db2, dbff2) = _rows_call(
        "loss", loss_fn, [h, loss_target.reshape(t_rows, dm), saved[-1]["pre2"]], [g2_last], [(dm, f32)],
        [(1, LANES), (1, dm), (1, dm), (1, dm)], tl)
    loss = loss_acc[0, 0]
    if distributed:
        loss = lax.psum(loss, MESH_AXES)

    big_parts = {n: [None] * depth for n in BIG_NAMES}
    small_parts = [None] * depth
    grad_handles = [None] * depth
    grad_modes = ["scatter"] * len(BIG_NAMES) + ["gather"]
    grad_x = None
    for l in reversed(range(depth)):
        sv = saved[l]

        def ff2_bwd(rv, cr):
            da = _mm_nt(rv[0], cr[0][...]) * (2.0 * jnp.sqrt(rv[1].astype(F32)))
            return [da], [_colsum(da)]

        (dact,), (dbff1,) = _rows_call("ff2_bwd", ff2_bwd, [dpre2, sv["act"]], [sv["w_ff2"]], [(dff, MXU_DTYPE)], [(1, dff)], tm)
        big_parts["w_ff2"][l] = _wgrad_call("wgrad_ff2", sv["act"], dpre2, "x", ndev, ndev // 2 if ndev > 1 else 1,
                                            (dff // ndev, dm), tm)
        big_parts["w_ff1"][l] = _wgrad_call("wgrad_ff1", sv["h1"], dact, "y", ndev, ndev // 2 if ndev > 1 else 1,
                                            (dm, dff // ndev), tm)

        def ff1_bwd(rv, cr):
            dacc = alpha * rv[1]
            wpb = dff // ndev
            for j in range(ndev):
                dacc = dacc + _mm_nt(rv[0][:, j * wpb:(j + 1) * wpb], cr[0][j])
            dpre, dg, db = _layer_norm_bwd(dacc, rv[2], cr[1][...])
            return [dpre], [dg, db, _colsum(dpre)]

        (dpre1,), (dg1, db1, dbout) = _rows_call("ff1_bwd", ff1_bwd, [dact, dpre2, sv["pre1"]], [sv["w_ff1"], sv["g1"]],
                                                 [(dm, f32)], [(1, dm)] * 3, tm)
        big_parts["w_out"][l] = _wgrad_call("wgrad_out", sv["mixed"], dpre1, "x", ndev, ndev, (dm // ndev, dm), tl)

        def mix_bwd(rv, cr):
            dp, at, yp = rv
            w_out_r, w_glu_r, bg, ga_, gs_ = cr
            dmixed = _mm_nt(dp, w_out_r[...])
            g = _gelu(yp)
            sig = jax.nn.sigmoid(_mm(g, w_glu_r[...]) + bg[...])
            ssm = g * sig
            dat, dga = _rms_norm_bwd(dmixed[:, :aw], at, ga_[...])
            dssm, dgs = _rms_norm_bwd(dmixed[:, aw:], ssm, gs_[...])
            dz = dssm * g * sig * (1.0 - sig)
            dg = dssm * sig + _mm_nt(dz, w_glu_r[...])
            return [dat, dg * _gelu_grad(yp), dz], [dga, dgs, _colsum(dz)]

        (dattn, dypre, dz), (dga, dgs, dbglu) = _rows_call(
            "mix_bwd", mix_bwd, [dpre1, sv["attn"], sv["ypre"]], [sv["w_out"], sv["w_glu"], sv["b_glu"], sv["ga"], sv["gs"]],
            [(aw, f32), (sw, f32), (sw, MXU_DTYPE)], [(1, aw), (1, sw), (1, sw)], tl)
        big_parts["w_glu"][l] = _wgrad_call("wgrad_glu", sv["ypre"], dz, "x", 1, 1, (sw, sw), tl, prologue=_gelu).reshape(
            ndev, sw // ndev, sw)

        du3, dbre, dbim, dcre, dcim, dlr, dli, dd = _s5_bwd(
            sv["proj3"], 3 * aw // sw, dypre.reshape(bsz, seq, sw), sv["xr3"], sv["xi3"], *sv["lam"], *sv["s5c"], sv["dvec"], sw)
        dbb_re, dc_re = _s5_unblock(dbre, dcre, ng, gdim, npst, ncb)
        dbb_im, dc_im = _s5_unblock(dbim, dcim, ng, gdim, npst, ncb)
        da_re, da_im, dldt, dbt_re, dbt_im = _s5_params_bwd(
            *sv["s5in"], (dlr.reshape(ng, 1, npst), dli.reshape(ng, 1, npst), dbb_re, dbb_im))

        dq3, dk3, dv3 = _attn_bwd(sv["proj3"], ctab, stab, dattn.reshape(bsz, seq, aw), sv["attn3"], sv["lse3"], aw)
        dproj = [dq3.reshape(t_rows, aw), dk3.reshape(t_rows, aw), dv3.reshape(t_rows, aw), du3.reshape(t_rows, sw)]
        big_parts["w_in"][l] = _wgrad_call("wgrad_in", sv["h"], dproj, "y", ndev, ndev, (dm, in_w // ndev), tl)

        small_parts[l] = dict(
            attn_gain=dga, ssm_gain=dgs, ssm_a_re=da_re[:, 0], ssm_a_im=da_im[:, 0], ssm_log_dt=dldt[:, 0, 0], ssm_b_re=dbt_re.transpose(0, 2, 1),
            ssm_b_im=dbt_im.transpose(0, 2, 1), ssm_c_re=dc_re, ssm_c_im=dc_im, ssm_d=dd, b_glu=dbglu, b_out=dbout, ln1_g=dg1,
            ln1_b=db1, b_ff1=dbff1, b_ff2=dbff2, ln2_g=dg2, ln2_b=db2)

        layer_grads = [big_parts[n][l] for n in BIG_NAMES] + [_pack_small([small_parts[l][n] for n in SMALL_NAMES])]
        if distributed:
            grad_handles[l], token = _exchange_start(f"grads_start_l{l}", layer_grads, grad_modes)
        else:
            grad_handles[l], token = layer_grads, jnp.zeros((SUBLANES, LANES), f32)

        wpb = in_w // ndev
        if l > 0:
            prev = saved[l - 1]

            def in_bwd(rv, cr):
                dacc = alpha * rv[4]
                dp = jnp.concatenate([v_.astype(MXU_DTYPE) for v_ in rv[:4]], axis=-1)
                for j in range(ndev):
                    dacc = dacc + _mm_nt(dp[:, j * wpb:(j + 1) * wpb], cr[0][j])
                dpre, dg, db = _layer_norm_bwd(dacc, rv[5], cr[1][...])
                return [dpre], [dg, db, _colsum(dpre)]

            (dpre2,), (dg2, db2, dbff2) = _rows_call("in_bwd", in_bwd, dproj + [dpre1, prev["pre2"]],
                                                     [sv["w_in"], prev["g2"], token], [(dm, f32)], [(1, dm)] * 3, tl)
        else:
            def in_bwd0(rv, cr):
                dacc = alpha * rv[4]
                dp = jnp.concatenate([v_.astype(MXU_DTYPE) for v_ in rv[:4]], axis=-1)
                for j in range(ndev):
                    dacc = dacc + _mm_nt(dp[:, j * wpb:(j + 1) * wpb], cr[0][j])
                return [dacc], []

            (grad_x,), _ = _rows_call("in_bwd0", in_bwd0, dproj + [dpre1], [sv["w_in"], token], [(dm, f32)], [], tl)

    small_shapes = [weights[n].shape[1:] for n in SMALL_NAMES]
    outs = {n: None for n in BIG_NAMES}
    packed = [_pack_small_layers([d[n] for n in SMALL_NAMES]) for d in (weights, moments_m, moments_v)]
    small_out = None
    after = grad_x
    for l in reversed(range(depth)):
        if distributed:
            recv = _exchange_wait(f"grads_wait_l{l}", grad_handles[l], grad_modes, after)
        else:
            recv = [g_[None] if i == len(BIG_NAMES) else g_ for i, g_ in enumerate(grad_handles[l])]
        for n, parts in zip(BIG_NAMES, recv[:-1], strict=True):
            outs[n] = _adamw_layer("adamw_" + n, parts, weights[n], moments_m[n], moments_v[n], l, outs[n],
                                   min(parts.shape[1], 256))
        small_out = _adamw_layer("adamw_small", recv[-1], *packed, l, small_out, recv[-1].shape[1])
        after = small_out[0]
    for k in range(4):
        flat = small_out[k].reshape(depth, -1)
        off = 0
        for n, shp in zip(SMALL_NAMES, small_shapes, strict=True):
            sz = int(np.prod(shp))
            outs.setdefault(n, [None] * 4)
            outs[n][k] = flat[:, off:off + sz].reshape((depth,) + tuple(shp))
            off += sz

    result = [loss, grad_x.reshape(bsz, seq, dm)]
    for k in range(4):
        result += [outs[n][k] for n in WEIGHT_ORDER]
    return tuple(result)


def kernel(x, positions, w_in, attn_gain, ssm_gain, ssm_a_re, ssm_a_im, ssm_log_dt, ssm_b_re, ssm_b_im, ssm_c_re, ssm_c_im, ssm_d, w_glu, b_glu, w_out, b_out, ln1_g, ln1_b, w_ff1, b_ff1, w_ff2, b_ff2, ln2_g, ln2_b, loss_target, m_w_in, m_attn_gain, m_ssm_gain, m_ssm_a_re, m_ssm_a_im, m_ssm_log_dt, m_ssm_b_re, m_ssm_b_im, m_ssm_c_re, m_ssm_c_im, m_ssm_d, m_w_glu, m_b_glu, m_w_out, m_b_out, m_ln1_g, m_ln1_b, m_w_ff1, m_b_ff1, m_w_ff2, m_b_ff2, m_ln2_g, m_ln2_b, v_w_in, v_attn_gain, v_ssm_gain, v_ssm_a_re, v_ssm_a_im, v_ssm_log_dt, v_ssm_b_re, v_ssm_b_im, v_ssm_c_re, v_ssm_c_im, v_ssm_d, v_w_glu, v_b_glu, v_w_out, v_b_out, v_ln1_g, v_ln1_b, v_w_ff1, v_b_ff1, v_w_ff2, v_b_ff2, v_ln2_g, v_ln2_b):
    loc = locals()
    weights = {n: loc[n] for n in WEIGHT_ORDER}
    moments_m = {n: loc["m_" + n] for n in WEIGHT_ORDER}
    moments_v = {n: loc["v_" + n] for n in WEIGHT_ORDER}
    return _step(x, positions, weights, moments_m, moments_v, loss_target, distributed=True)
```

```python
import functools
import math

import jax
import jax.numpy as jnp
import numpy as np
from jax import lax
from jax.experimental import pallas as pl
from jax.experimental.pallas import tpu as pltpu

F32 = jnp.float32
MXU_DTYPE = jnp.bfloat16

HEAD_DIM = 64
DILATION_PAIRS = ((128, 1), (512, 4), (2048, 16))
ROPE_THETA = 10000.0
SSM_GROUP_DIM = 16
SSM_STATE = 64
LN_EPS = 1e-5
RMS_EPS = 1e-6
NEG_INF = -1e30
ADAM_LR, ADAM_B1, ADAM_B2, ADAM_EPS, ADAM_WD, ADAM_STEP = 0.001, 0.9, 0.999, 1e-08, 0.01, 10

LANES = 128
SUBLANES = 8
QBLK = 128
VMEM_LIMIT = 56 * 2**20
MESH_AXES = ("x", "y", "c")
NDEV = 8

SMALL_NAMES = ("attn_gain", "ssm_gain", "ssm_a_re", "ssm_a_im", "ssm_log_dt", "ssm_b_re", "ssm_b_im", "ssm_c_re",
               "ssm_c_im", "ssm_d", "b_glu", "b_out", "ln1_g", "ln1_b", "b_ff1", "b_ff2", "ln2_g", "ln2_b")
BIG_NAMES = ("w_in", "w_glu", "w_out", "w_ff1", "w_ff2")
WEIGHT_ORDER = ("w_in", "attn_gain", "ssm_gain", "ssm_a_re", "ssm_a_im", "ssm_log_dt", "ssm_b_re", "ssm_b_im", "ssm_c_re",
                "ssm_c_im", "ssm_d", "w_glu", "b_glu", "w_out", "b_out", "ln1_g", "ln1_b", "w_ff1", "b_ff1", "w_ff2",
                "b_ff2", "ln2_g", "ln2_b")


def _cparams(sem=None):
    return pltpu.CompilerParams(dimension_semantics=sem, vmem_limit_bytes=VMEM_LIMIT)


def _mm(a, b):
    return jnp.dot(a.astype(MXU_DTYPE), b.astype(MXU_DTYPE), preferred_element_type=F32)


def _mm_nt(a, b):
    return lax.dot_general(a.astype(MXU_DTYPE), b.astype(MXU_DTYPE), (((1,), (1,)), ((), ())), preferred_element_type=F32)


def _mm_tn(a, b):
    return lax.dot_general(a.astype(MXU_DTYPE), b.astype(MXU_DTYPE), (((0,), (0,)), ((), ())), preferred_element_type=F32)


def _colsum(x):
    return jnp.sum(x, axis=0, keepdims=True)


def _layer_norm(x, g, b):
    mu = jnp.mean(x, axis=-1, keepdims=True)
    xc = x - mu
    var = jnp.mean(xc * xc, axis=-1, keepdims=True)
    return xc * lax.rsqrt(var + LN_EPS) * g + b


def _layer_norm_bwd(dy, pre, g):
    mu = jnp.mean(pre, axis=-1, keepdims=True)
    xc = pre - mu
    var = jnp.mean(xc * xc, axis=-1, keepdims=True)
    r = lax.rsqrt(var + LN_EPS)
    xhat = xc * r
    dyg = dy * g
    dpre = r * (dyg - jnp.mean(dyg, axis=-1, keepdims=True) - xhat * jnp.mean(dyg * xhat, axis=-1, keepdims=True))
    return dpre, _colsum(dy * xhat), _colsum(dy)


def _rms_norm(x, g):
    return x * lax.rsqrt(jnp.mean(x * x, axis=-1, keepdims=True) + RMS_EPS) * g


def _rms_norm_bwd(dy, x, g):
    r = lax.rsqrt(jnp.mean(x * x, axis=-1, keepdims=True) + RMS_EPS)
    dyg = dy * g
    dx = dyg * r - x * (r * r * r) * jnp.mean(dyg * x, axis=-1, keepdims=True)
    return dx, _colsum(dy * x * r)


_GELU_C = math.sqrt(2.0 / math.pi)


def _gelu(x):
    return 0.5 * x * (1.0 + jnp.tanh(_GELU_C * (x + 0.044715 * (x * x * x))))


def _gelu_grad(x):
    t = jnp.tanh(_GELU_C * (x + 0.044715 * (x * x * x)))
    return 0.5 * (1.0 + t) + 0.5 * x * (1.0 - t * t) * (_GELU_C * (1.0 + 3.0 * 0.044715 * x * x))


def _rows_call(name, fn, rows, consts, out_rows, out_accs, tm):
    rows = [r if isinstance(r, tuple) else (r, r.shape[1], 0) for r in rows]
    m = rows[0][0].shape[0]
    assert m % tm == 0
    nr, nc, no, na = len(rows), len(consts), len(out_rows), len(out_accs)

    def body(*refs):
        rr, cr = refs[:nr], refs[nr:nr + nc]
        orr, ar = refs[nr + nc:nr + nc + no], refs[nr + nc + no:]
        outs, accs = fn([r[...] for r in rr], cr)
        for o, v in zip(orr, outs, strict=True):
            o[...] = v.astype(o.dtype)
        if na:
            first = pl.program_id(0) == 0

            @pl.when(first)
            def _():
                for a, v in zip(ar, accs, strict=True):
                    a[...] = v

            @pl.when(jnp.logical_not(first))
            def _():
                for a, v in zip(ar, accs, strict=True):
                    a[...] += v

    def whole(shape):
        return pl.BlockSpec(shape, lambda i, n=len(shape): (0,) * n)

    in_specs = [pl.BlockSpec((tm, w), lambda i, cb=cb: (i, cb)) for _, w, cb in rows] + [whole(c.shape) for c in consts]
    out_specs = [pl.BlockSpec((tm, w), lambda i: (i, 0)) for w, _ in out_rows] + [whole(s) for s in out_accs]
    out_shape = [jax.ShapeDtypeStruct((m, w), dt) for w, dt in out_rows] + [jax.ShapeDtypeStruct(s, F32) for s in out_accs]
    res = pl.pallas_call(body, grid=(m // tm,), in_specs=in_specs, out_specs=out_specs, out_shape=out_shape, name=name,
                         compiler_params=_cparams(("arbitrary",)))(*[r[0] for r in rows], *consts)
    return res[:no], res[no:]


def _wgrad_call(name, x, dy, split, nblk, jb, blk_shape, tm, prologue=None):
    dys = list(dy) if isinstance(dy, (list, tuple)) else [dy]
    m = x.shape[0]
    kk, nn = blk_shape
    assert m % tm == 0 and nblk % jb == 0 and (len(dys) == 1 or (split == "y" and jb == nblk))
    xw = kk * jb if split == "x" else x.shape[1]
    yws = [d.shape[1] for d in dys] if (split == "x" or len(dys) > 1) else [nn * jb]
    nrow = m // tm

    def body(x_ref, *rest):
        dy_refs, o_ref, acc_ref = rest[:len(dys)], rest[-2], rest[-1]
        i = pl.program_id(1)

        @pl.when(i == 0)
        def _():
            acc_ref[...] = jnp.zeros_like(acc_ref)

        xv = x_ref[...]
        if prologue is not None:
            xv = prologue(xv)
        xv = xv.astype(MXU_DTYPE)
        dv = [r[...].astype(MXU_DTYPE) for r in dy_refs]
        dv = dv[0] if len(dv) == 1 else jnp.concatenate(dv, axis=-1)
        for j in range(jb):
            xa = xv[:, j * kk:(j + 1) * kk] if split == "x" else xv
            da = dv[:, j * nn:(j + 1) * nn] if split == "y" else dv
            acc_ref[j] += _mm_tn(xa, da)

        @pl.when(i == nrow - 1)
        def _():
            o_ref[...] = acc_ref[...].astype(o_ref.dtype)

    in_specs = [pl.BlockSpec((tm, xw), (lambda j, i: (i, j)) if split == "x" else (lambda j, i: (i, 0)))]
    in_specs += [pl.BlockSpec((tm, yw), (lambda j, i: (i, j)) if (split == "y" and len(dys) == 1) else (lambda j, i: (i, 0)))
                 for yw in yws]
    return pl.pallas_call(
        body, grid=(nblk // jb, nrow), in_specs=in_specs,
        out_specs=pl.BlockSpec((jb, kk, nn), lambda j, i: (j, 0, 0)),
        out_shape=jax.ShapeDtypeStruct((nblk, kk, nn), MXU_DTYPE),
        scratch_shapes=[pltpu.VMEM((jb, kk, nn), F32)], name=name,
        compiler_params=_cparams(("arbitrary", "arbitrary")))(x, *dys)


_HBM_SPEC = pl.BlockSpec(memory_space=pltpu.HBM)
_SEM_SPEC = pl.BlockSpec(memory_space=pltpu.SEMAPHORE)
_EFFECT = pltpu.SideEffectType.DATAFLOW_SIDE_EFFECTING


def _my_index():
    return 4 * lax.axis_index("x") + 2 * lax.axis_index("y") + lax.axis_index("c")


def _peer_copies(ins, lands, send_sems, recv_sems, modes):
    x, y, c = lax.axis_index("x"), lax.axis_index("y"), lax.axis_index("c")
    me = 4 * x + 2 * y + c
    pairs = []
    for k in range(NDEV - 1):
        fx, fy, fc = ((k + 1) >> 2) & 1, ((k + 1) >> 1) & 1, (k + 1) & 1
        px, py, pc = (x + fx) % 2, (y + fy) % 2, (c + fc) % 2
        idx = 4 * px + 2 * py + pc
        for a, md in enumerate(modes):
            src = ins[a] if md == "gather" else ins[a].at[idx]
            sem = a * (NDEV - 1) + k
            common = dict(src_ref=src, send_sem=send_sems.at[sem], recv_sem=recv_sems.at[sem], device_id=(px, py, pc),
                          device_id_type=pl.DeviceIdType.MESH)
            pairs.append((pltpu.make_async_remote_copy(dst_ref=lands[a].at[me], **common),
                          pltpu.make_async_remote_copy(dst_ref=lands[a].at[idx], **common)))
    return pairs


def _exchange_start(name, arrays, modes, after=None):
    n = len(arrays)
    extra = [] if after is None else [after]
    lands = [lax.empty((NDEV,) + (a.shape if md == "gather" else a.shape[1:]), a.dtype)
             for a, md in zip(arrays, modes, strict=True)]

    def body(*refs):
        ins, lnd = refs[:n], refs[n:2 * n]
        send_sems, recv_sems = refs[2 * n + len(extra)], refs[2 * n + len(extra) + 1]
        token = refs[-1]
        for out_copy, _ in _peer_copies(ins, lnd, send_sems, recv_sems, modes):
            out_copy.start()
        token[...] = jnp.zeros_like(token)

    sems = pltpu.SemaphoreType.DMA((n * (NDEV - 1),))
    thru = [pltpu.HBM(a.shape, a.dtype) for a in list(arrays) + lands]
    res = pl.pallas_call(
        body, name=name, out_shape=(sems, sems, *thru, jax.ShapeDtypeStruct((SUBLANES, LANES), F32)),
        in_specs=[_HBM_SPEC] * (2 * n) + [pl.BlockSpec(memory_space=pl.ANY)] * len(extra),
        out_specs=(_SEM_SPEC, _SEM_SPEC, *([_HBM_SPEC] * (2 * n)), pl.BlockSpec(memory_space=pltpu.VMEM)),
        input_output_aliases={i: 2 + i for i in range(2 * n)},
        compiler_params=pltpu.CompilerParams(has_side_effects=_EFFECT),
    )(*[pltpu.with_memory_space_constraint(a, pltpu.HBM) for a in list(arrays) + lands], *extra)
    return (res[0], res[1], res[2:2 + n], res[2 + n:2 + 2 * n]), res[-1]


def _exchange_wait(name, handle, modes, after):
    send_sems, recv_sems, ins_thru, lands_thru = handle
    n = len(ins_thru)

    def body(*refs):
        ins, lnd = refs[:n], refs[n:2 * n]
        lnd_out, local_sems = refs[3 * n + 3:4 * n + 3], refs[-1]
        me = _my_index()
        own = [pltpu.make_async_copy(ins[a] if md == "gather" else ins[a].at[me], lnd_out[a].at[me], local_sems.at[a])
               for a, md in enumerate(modes)]
        for cp in own:
            cp.start()
        for out_copy, arrival in _peer_copies(ins, lnd, refs[2 * n], refs[2 * n + 1], modes):
            out_copy.wait_send()
            arrival.wait_recv()
        for cp in own:
            cp.wait()

    thru = [pltpu.HBM(a.shape, a.dtype) for a in list(ins_thru) + list(lands_thru)]
    res = pl.pallas_call(
        body, name=name, out_shape=tuple(thru),
        in_specs=[_HBM_SPEC] * (2 * n) + [_SEM_SPEC, _SEM_SPEC, pl.BlockSpec(memory_space=pl.ANY)],
        out_specs=tuple([_HBM_SPEC] * (2 * n)), input_output_aliases={i: i for i in range(2 * n)},
        scratch_shapes=[pltpu.SemaphoreType.DMA((n,))],
        compiler_params=pltpu.CompilerParams(has_side_effects=_EFFECT),
    )(*ins_thru, *lands_thru, send_sems, recv_sems, after)
    return res[n:]


def _rope_tables(positions3, inv_freq_row, sign_row):
    b, s, _ = positions3.shape

    def body(pos_ref, f_ref, sg_ref, c_ref, s_ref):
        ang = pos_ref[...].astype(F32) * f_ref[...]
        c_ref[...] = jnp.cos(ang)
        s_ref[...] = jnp.sin(ang) * sg_ref[...]

    row = pl.BlockSpec((1, LANES), lambda i: (0, 0))
    blk = pl.BlockSpec((None, s, LANES), lambda i: (i, 0, 0))
    return pl.pallas_call(
        body, grid=(b,), in_specs=[pl.BlockSpec((None, s, 1), lambda i: (i, 0, 0)), row, row], out_specs=[blk, blk],
        out_shape=[jax.ShapeDtypeStruct((b, s, LANES), F32)] * 2, name="rope_tables",
        compiler_params=_cparams(("arbitrary",)))(positions3, inv_freq_row, sign_row)


def _swap_halves(t):
    lane = lax.broadcasted_iota(jnp.int32, t.shape, 1)
    half = HEAD_DIM // 2
    return jnp.where((lane % HEAD_DIM) < half, pltpu.roll(t, LANES - half, 1), pltpu.roll(t, half, 1))


def _segment_rows(r, d, s):
    n = s // d
    return (pl.ds(r, n, stride=d) if d > 1 else pl.ds(0, s)), pl.ds(r * n, n)


def _head_lanes(shape):
    lane = lax.broadcasted_iota(jnp.int32, shape, len(shape) - 1)
    return lane < HEAD_DIM


def _bmm_nt(a, b):
    return lax.dot_general(a.astype(MXU_DTYPE), b.astype(MXU_DTYPE), (((2,), (2,)), ((0,), (0,))), preferred_element_type=F32)


def _bmm(a, b):
    return lax.dot_general(a.astype(MXU_DTYPE), b.astype(MXU_DTYPE), (((2,), (1,)), ((0,), (0,))), preferred_element_type=F32)


def _bmm_tn(a, b):
    return lax.dot_general(a.astype(MXU_DTYPE), b.astype(MXU_DTYPE), (((1,), (1,)), ((0,), (0,))), preferred_element_type=F32)


def _stack_heads(t3):
    head_a = _head_lanes(t3.shape)
    zero = jnp.zeros_like(t3)
    return jnp.concatenate([jnp.where(head_a, t3, zero), jnp.where(head_a, zero, t3)], axis=1)


QUNIT = QBLK // 2


def _with_previous(t3, nprev):
    shifted = [jnp.concatenate([t3[:k], t3[:-k]], axis=0) for k in range(nprev, 0, -1)]
    return jnp.concatenate(shifted + [t3], axis=1)


def _head_columns(t3):
    return jnp.concatenate([t3[:, :, lo:lo + 1] for lo in range(0, LANES, HEAD_DIM)], axis=1)


def _to_own_unit(t, nprev):
    unit = t.shape[1] // (nprev + 1)
    out = t[:, nprev * unit:]
    for k in range(1, nprev + 1):
        part = t[:, (nprev - k) * unit:(nprev - k + 1) * unit]
        out = out + jnp.concatenate([part[k:], jnp.zeros_like(part[:k])], axis=0)
    return out


def _branch_operands(qh, kh, vh, s, nb):
    nh = LANES // HEAD_DIM
    unit, nprev = (QBLK, 0) if nb == 1 else (QUNIT, QBLK // QUNIT)
    g = s // unit
    q3 = _stack_heads(qh[...].reshape(g, unit, LANES))
    k3, v3 = kh[...].reshape(g, unit, LANES), vh[...].reshape(g, unit, LANES)
    if nprev == 0:
        qi = lax.broadcasted_iota(jnp.int32, (1, nh * unit, unit), 1) % unit
        kj = lax.broadcasted_iota(jnp.int32, (1, nh * unit, unit), 2)
        return q3, k3, v3, kj <= qi, nprev
    shape = (g, nh * unit, (nprev + 1) * unit)
    qi = lax.broadcasted_iota(jnp.int32, shape, 1) % unit
    kj = lax.broadcasted_iota(jnp.int32, shape, 2)
    j = lax.broadcasted_iota(jnp.int32, shape, 0)
    per_block = QBLK // unit
    opens = ((j // per_block) % nb) == 0
    mask = (kj >= qi) & (kj <= qi + QBLK) & ((kj >= QBLK - unit * (j % per_block)) | jnp.logical_not(opens))
    return q3, _with_previous(k3, nprev), _with_previous(v3, nprev), mask, nprev


def _attn_fwd(proj3, ctab, stab, aw):
    b, s, _ = proj3.shape
    npair = aw // LANES
    scale = HEAD_DIM ** -0.5
    nbr = len(DILATION_PAIRS)

    def body(q_ref, k_ref, v_ref, c_ref, s_ref, o_ref, l_ref, qf, kf, vf, qh, kh, vh, op, lp, ob, lb):
        cc, ss = c_ref[...], s_ref[...]
        q2, k2 = q_ref[...], k_ref[...]
        qf[...] = (q2 * cc + _swap_halves(q2) * ss) * scale
        kf[...] = k2 * cc + _swap_halves(k2) * ss
        vf[...] = v_ref[...]
        for br, (window, d) in enumerate(DILATION_PAIRS):
            for r in range(d):
                nat, perm = _segment_rows(r, d, s)
                for dst, src in ((qh, qf), (kh, kf), (vh, vf)):
                    dst[perm, :] = src[nat, :].astype(MXU_DTYPE)
            o_dst, l_dst = (ob.at[br], lb.at[br]) if d == 1 else (op, lp)
            q3, kk, vv, mask, _ = _branch_operands(qh, kh, vh, s, (s // d) // QBLK)
            unit = q3.shape[1] // (LANES // HEAD_DIM)
            head_a = _head_lanes((q3.shape[0], unit, LANES))
            sc = jnp.where(mask, _bmm_nt(q3, kk), NEG_INF)
            mx = jnp.max(sc, axis=-1, keepdims=True)
            p = jnp.exp(sc - mx)
            den = jnp.sum(p, axis=-1, keepdims=True)
            o2 = _bmm(p, vv) / den
            l2 = mx + jnp.log(den)
            o_dst[...] = jnp.where(head_a, o2[:, :unit], o2[:, unit:]).reshape(s, LANES)
            l_dst[...] = jnp.where(head_a, l2[:, :unit], l2[:, unit:]).reshape(s, LANES)
            if d > 1:
                for r in range(d):
                    nat, perm = _segment_rows(r, d, s)
                    ob[br, nat, :] = op[perm, :]
                    lb[br, nat, :] = lp[perm, :]
        ls = [lb[br] for br in range(nbr)]
        mx = functools.reduce(jnp.maximum, ls)
        ws = [jnp.exp(l - mx) for l in ls]
        tot = functools.reduce(lambda a, b_: a + b_, ws)
        o_ref[...] = functools.reduce(lambda a, b_: a + b_, [(w / tot) * ob[br] for br, w in enumerate(ws)])
        l_ref[...] = mx + jnp.log(tot)

    def col(off):
        return pl.BlockSpec((None, s, LANES), lambda bi, hp, off=off: (bi, 0, off + hp))

    tab = pl.BlockSpec((None, s, LANES), lambda bi, hp: (bi, 0, 0))
    f32s = pltpu.VMEM((s, LANES), F32)
    mxs = pltpu.VMEM((s, LANES), MXU_DTYPE)
    br_s = pltpu.VMEM((nbr, s, LANES), F32)
    return pl.pallas_call(
        body, grid=(b, npair), in_specs=[col(0), col(npair), col(2 * npair), tab, tab], out_specs=[col(0), col(0)],
        out_shape=[jax.ShapeDtypeStruct((b, s, aw), F32)] * 2,
        scratch_shapes=[f32s] * 3 + [mxs] * 3 + [f32s] * 2 + [br_s] * 2,
        name="attn_fwd", compiler_params=_cparams(("arbitrary", "arbitrary")))(proj3, proj3, proj3, ctab, stab)


def _attn_bwd(proj3, ctab, stab, dout3, out3, lse3, aw):
    b, s, _ = proj3.shape
    npair = aw // LANES
    scale = HEAD_DIM ** -0.5
    nheads = LANES // HEAD_DIM

    def body(q_ref, k_ref, v_ref, c_ref, s_ref, do_ref, o_ref, l_ref, dq_ref, dk_ref, dv_ref,
             qf, kf, vf, dof, dlf, qh, kh, vh, doh, lpm, dpm, dqp, dkp, dvp, dqn, dkn, dvn):
        cc, ss = c_ref[...], s_ref[...]
        q2, k2 = q_ref[...], k_ref[...]
        qf[...] = (q2 * cc + _swap_halves(q2) * ss) * scale
        kf[...] = k2 * cc + _swap_halves(k2) * ss
        vf[...] = v_ref[...]
        do2 = do_ref[...]
        dof[...] = do2
        dd = do2 * o_ref[...]
        in_a = _head_lanes((s, LANES))
        sum_a = jnp.sum(jnp.where(in_a, dd, 0.0), axis=-1, keepdims=True)
        sum_b = jnp.sum(jnp.where(in_a, 0.0, dd), axis=-1, keepdims=True)
        dlf[...] = jnp.where(in_a, sum_a, sum_b)
        for r_ in (dqn, dkn, dvn):
            r_[...] = jnp.zeros_like(r_)
        for window, d in DILATION_PAIRS:
            for r in range(d):
                nat, perm = _segment_rows(r, d, s)
                for dst, src in ((qh, qf), (kh, kf), (vh, vf), (doh, dof)):
                    dst[perm, :] = src[nat, :].astype(MXU_DTYPE)
                if d > 1:
                    lpm[perm, :] = l_ref[nat, :]
                    dpm[perm, :] = dlf[nat, :]
            l_src, d_src = (l_ref, dlf) if d == 1 else (lpm, dpm)
            q3, kk, vv, mask, nprev = _branch_operands(qh, kh, vh, s, (s // d) // QBLK)
            g_, unit = q3.shape[0], q3.shape[1] // nheads
            head_a = _head_lanes((g_, unit, LANES))
            do3 = _stack_heads(doh[...].reshape(g_, unit, LANES))
            lcol, dcol = _head_columns(l_src[...].reshape(g_, unit, LANES)), _head_columns(d_src[...].reshape(g_, unit, LANES))
            p = jnp.exp(jnp.where(mask, _bmm_nt(q3, kk), NEG_INF) - lcol)
            ds_ = p * (_bmm_nt(do3, vv) - dcol)
            dq2 = _bmm(ds_, kk)
            dq_new = jnp.where(head_a, dq2[:, :unit], dq2[:, unit:]).reshape(s, LANES)
            dk_new = _to_own_unit(_bmm_tn(ds_, q3), nprev).reshape(s, LANES)
            dv_new = _to_own_unit(_bmm_tn(p, do3), nprev).reshape(s, LANES)
            if d == 1:
                dqn[...] += dq_new
                dkn[...] += dk_new
                dvn[...] += dv_new
            else:
                dqp[...] = dq_new
                dkp[...] = dk_new
                dvp[...] = dv_new
                for r in range(d):
                    nat, perm = _segment_rows(r, d, s)
                    dqn[nat, :] += dqp[perm, :]
                    dkn[nat, :] += dkp[perm, :]
                    dvn[nat, :] += dvp[perm, :]
        g = dqn[...] * scale
        dq_ref[...] = g * cc + _swap_halves(g * ss)
        g = dkn[...]
        dk_ref[...] = g * cc + _swap_halves(g * ss)
        dv_ref[...] = dvn[...]

    def col(off):
        return pl.BlockSpec((None, s, LANES), lambda bi, hp, off=off: (bi, 0, off + hp))

    tab = pl.BlockSpec((None, s, LANES), lambda bi, hp: (bi, 0, 0))
    f32s = pltpu.VMEM((s, LANES), F32)
    mxs = pltpu.VMEM((s, LANES), MXU_DTYPE)
    return pl.pallas_call(
        body, grid=(b, npair), in_specs=[col(0), col(npair), col(2 * npair), tab, tab, col(0), col(0), col(0)],
        out_specs=[col(0)] * 3, out_shape=[jax.ShapeDtypeStruct((b, s, aw), F32)] * 3,
        scratch_shapes=[f32s] * 5 + [mxs] * 4 + [f32s] * 8,
        name="attn_bwd", compiler_params=_cparams(("arbitrary", "arbitrary")))(
            proj3, proj3, proj3, ctab, stab, dout3, out3, lse3)


def _s5_discretise(a_re, a_im, log_dt, bt_re, bt_im):
    dt = jnp.exp(log_dt)
    mag = jnp.exp(a_re * dt)
    ang = a_im * dt
    lb_re = mag * jnp.cos(ang)
    lb_im = mag * jnp.sin(ang)
    den = a_re * a_re + a_im * a_im
    nr = lb_re - 1.0
    ni = lb_im
    cr = (nr * a_re + ni * a_im) / den
    ci = (ni * a_re - nr * a_im) / den
    return lb_re, lb_im, cr * bt_re - ci * bt_im, cr * bt_im + ci * bt_re


def _s5_params(a_re, a_im, log_dt, bt_re, bt_im):
    def body(ar, ai, ld, br, bi, o1, o2, o3, o4):
        r = _s5_discretise(ar[...], ai[...], ld[...], br[...], bi[...])
        for o, v in zip((o1, o2, o3, o4), r, strict=True):
            o[...] = v

    sd = jax.ShapeDtypeStruct
    return pl.pallas_call(body, out_shape=[sd(a_re.shape, F32)] * 2 + [sd(bt_re.shape, F32)] * 2, name="s5_params")(
        a_re, a_im, log_dt, bt_re, bt_im)


def _s5_params_bwd(a_re, a_im, log_dt, bt_re, bt_im, cts):
    def body(ar, ai, ld, br, bi, c1, c2, c3, c4, o1, o2, o3, o4, o5):
        _, vjp = jax.vjp(_s5_discretise, ar[...], ai[...], ld[...], br[...], bi[...])
        r = vjp((c1[...], c2[...], c3[...], c4[...]))
        for o, v in zip((o1, o2, o3, o4, o5), r, strict=True):
            o[...] = v

    sd = jax.ShapeDtypeStruct
    return pl.pallas_call(
        body, out_shape=[sd(a_re.shape, F32)] * 2 + [sd(log_dt.shape, F32)] + [sd(bt_re.shape, F32)] * 2, name="s5_params_bwd")(
            a_re, a_im, log_dt, bt_re, bt_im, *cts)


S5_TC = 128


def _time_major(tiles):
    nt, tc, lanes = tiles.shape
    return jnp.swapaxes(tiles, 0, 1).reshape(tc * nt, lanes)


def _tile_major(rows, nt):
    return jnp.swapaxes(rows.astype(MXU_DTYPE).reshape(rows.shape[0] // nt, nt, rows.shape[1]), 0, 1)


def _s5_fwd(proj3, ucol, lam_re, lam_im, bre, bim, cre, cim, dvec, sw):
    b, s, _ = proj3.shape
    nt = lam_re.shape[0]
    ncb = sw // LANES
    tpc = nt // ncb
    tc = S5_TC

    def body(u_ref, lr_ref, li_ref, bre_ref, bim_ref, cre_ref, cim_ref, d_ref, y_ref, xr_out, xi_out, xr_ref, xi_ref, sr, si):
        @pl.when(pl.program_id(0) == 0)
        def _():
            sr[...] = jnp.zeros_like(sr)
            si[...] = jnp.zeros_like(si)

        u_all = u_ref[...].reshape(b * tc, sw)
        bur, bui = [], []
        for cb in range(ncb):
            ucb = u_all[:, cb * LANES:(cb + 1) * LANES].astype(MXU_DTYPE)
            for t in range(cb * tpc, (cb + 1) * tpc):
                bur.append(_mm(ucb, bre_ref[t]))
                bui.append(_mm(ucb, bim_ref[t]))
        for bi in range(b):
            rows = slice(bi * tc, (bi + 1) * tc)
            xr_ref[bi] = _time_major(jnp.stack([v_[rows] for v_ in bur]))
            xi_ref[bi] = _time_major(jnp.stack([v_[rows] for v_ in bui]))
        lr, li = lr_ref[...], li_ref[...]

        def step(j, carry):
            off = pl.multiple_of(j * nt, nt)
            new = []
            for bi in range(b):
                pr, pi = carry[2 * bi], carry[2 * bi + 1]
                nr = lr * pr - li * pi + xr_ref[bi, pl.ds(off, nt), :]
                ni = lr * pi + li * pr + xi_ref[bi, pl.ds(off, nt), :]
                xr_ref[bi, pl.ds(off, nt), :] = nr
                xi_ref[bi, pl.ds(off, nt), :] = ni
                new += [nr, ni]
            return tuple(new)

        init = tuple(v for bi in range(b) for v in (sr[bi], si[bi]))
        fin = lax.fori_loop(0, tc, step, init, unroll=4)
        for bi in range(b):
            sr[bi] = fin[2 * bi]
            si[bi] = fin[2 * bi + 1]
        xr_t, xi_t = [], []
        for bi in range(b):
            for src, dst, tiles in ((xr_ref, xr_out, xr_t), (xi_ref, xi_out, xi_t)):
                v16 = src[bi].astype(MXU_DTYPE)
                dst[bi] = v16
                tiles.append(_tile_major(v16, nt))
        for cb in range(ncb):
            cols = slice(cb * LANES, (cb + 1) * LANES)
            acc = d_ref[:, cols] * u_all[:, cols]
            for t in range(cb * tpc, (cb + 1) * tpc):
                xr_all = jnp.concatenate([xr_t[bi][t] for bi in range(b)], axis=0)
                xi_all = jnp.concatenate([xi_t[bi][t] for bi in range(b)], axis=0)
                acc = acc + (_mm(xr_all, cre_ref[t]) - _mm(xi_all, cim_ref[t]))
            for bi in range(b):
                y_ref[bi, :, cols] = acc[bi * tc:(bi + 1) * tc]

    def whole(a):
        return pl.BlockSpec(a.shape, lambda c, n=a.ndim: (0,) * n)

    xblk = pl.BlockSpec((b, tc * nt, LANES), lambda c: (0, c, 0))
    return pl.pallas_call(
        body, grid=(s // tc,),
        in_specs=[pl.BlockSpec((b, tc, sw), lambda c: (0, c, ucol))] + [whole(a) for a in (lam_re, lam_im, bre, bim, cre, cim, dvec)],
        out_specs=[pl.BlockSpec((b, tc, sw), lambda c: (0, c, 0)), xblk, xblk],
        out_shape=[jax.ShapeDtypeStruct((b, s, sw), F32)] + [jax.ShapeDtypeStruct((b, s * nt, LANES), MXU_DTYPE)] * 2,
        scratch_shapes=[pltpu.VMEM((b, tc * nt, LANES), F32)] * 2 + [pltpu.VMEM((b, nt, LANES), F32)] * 2, name="s5_fwd",
        compiler_params=_cparams(("arbitrary",)))(proj3, lam_re, lam_im, bre, bim, cre, cim, dvec)


def _s5_bwd(proj3, ucol, dy3, xr3, xi3, lam_re, lam_im, bre, bim, cre, cim, dvec, sw):
    b, s, _ = proj3.shape
    nt = lam_re.shape[0]
    ncb = sw // LANES
    tpc = nt // ncb
    tc = S5_TC
    nchunk = s // tc

    def body(u_ref, dy_ref, xr_ref, xi_ref, pr_ref, pi_ref, lr_ref, li_ref, bre_ref, bim_ref, cre_ref, cim_ref, d_ref,
             du_ref, dbre, dbim, dcre, dcim, dlr, dli, dd, gr, gi, sr, si):
        step_id = pl.program_id(0)

        @pl.when(step_id == 0)
        def _():
            for r in (sr, si, dbre, dbim, dcre, dcim, dlr, dli, dd):
                r[...] = jnp.zeros_like(r)

        u_all = u_ref[...].reshape(b * tc, sw)
        dy_all = dy_ref[...].reshape(b * tc, sw)
        dxr, dxi = [], []
        for cb in range(ncb):
            dycb = dy_all[:, cb * LANES:(cb + 1) * LANES].astype(MXU_DTYPE)
            for t in range(cb * tpc, (cb + 1) * tpc):
                dxr.append(_mm_nt(dycb, cre_ref[t]))
                dxi.append(-_mm_nt(dycb, cim_ref[t]))
        for bi in range(b):
            rows = slice(bi * tc, (bi + 1) * tc)
            gr[bi] = _time_major(jnp.stack([v_[rows] for v_ in dxr]))
            gi[bi] = _time_major(jnp.stack([v_[rows] for v_ in dxi]))
        lr, li = lr_ref[...], li_ref[...]

        def step(jj, carry):
            off = pl.multiple_of((tc - 1 - jj) * nt, nt)
            new = []
            for bi in range(b):
                nr_, ni_ = carry[2 * bi], carry[2 * bi + 1]
                vr = gr[bi, pl.ds(off, nt), :] + lr * nr_ + li * ni_
                vi = gi[bi, pl.ds(off, nt), :] + lr * ni_ - li * nr_
                gr[bi, pl.ds(off, nt), :] = vr
                gi[bi, pl.ds(off, nt), :] = vi
                new += [vr, vi]
            return tuple(new)

        init = tuple(v for bi in range(b) for v in (sr[bi], si[bi]))
        fin = lax.fori_loop(0, tc, step, init, unroll=4)
        for bi in range(b):
            sr[bi] = fin[2 * bi]
            si[bi] = fin[2 * bi + 1]

        has_prev = (step_id != nchunk - 1).astype(F32)
        rest = (tc - 1) * nt
        alr = jnp.zeros((nt, LANES), F32)
        ali = jnp.zeros((nt, LANES), F32)
        for bi in range(b):
            s0r, s0i = gr[bi, pl.ds(0, nt), :], gi[bi, pl.ds(0, nt), :]
            x0r, x0i = pr_ref[bi].astype(F32) * has_prev, pi_ref[bi].astype(F32) * has_prev
            alr += s0r * x0r + s0i * x0i
            ali += s0i * x0r - s0r * x0i
            s1r, s1i = gr[bi, pl.ds(nt, rest), :], gi[bi, pl.ds(nt, rest), :]
            x1r, x1i = xr_ref[bi, pl.ds(0, rest), :].astype(F32), xi_ref[bi, pl.ds(0, rest), :].astype(F32)
            alr += jnp.sum((s1r * x1r + s1i * x1i).reshape(tc - 1, nt, LANES), axis=0)
            ali += jnp.sum((s1i * x1r - s1r * x1i).reshape(tc - 1, nt, LANES), axis=0)
        dlr[...] += alr
        dli[...] += ali

        tiles = [[_tile_major(ref[bi], nt) for bi in range(b)] for ref in (gr, gi, xr_ref, xi_ref)]

        def stacked(k, t):
            return jnp.concatenate([tiles[k][bi][t] for bi in range(b)], axis=0)

        for cb in range(ncb):
            cols = slice(cb * LANES, (cb + 1) * LANES)
            ucb32, dycb32 = u_all[:, cols], dy_all[:, cols]
            ucb, dycb = ucb32.astype(MXU_DTYPE), dycb32.astype(MXU_DTYPE)
            acc = d_ref[:, cols] * dycb32
            for t in range(cb * tpc, (cb + 1) * tpc):
                vr, vi = stacked(0, t), stacked(1, t)
                acc = acc + (_mm_nt(vr, bre_ref[t]) + _mm_nt(vi, bim_ref[t]))
                dbre[t] += _mm_tn(ucb, vr)
                dbim[t] += _mm_tn(ucb, vi)
                dcre[t] += _mm_tn(stacked(2, t), dycb)
                dcim[t] -= _mm_tn(stacked(3, t), dycb)
            for bi in range(b):
                du_ref[bi, :, cols] = acc[bi * tc:(bi + 1) * tc]
            dd[:, cols] += _colsum(dycb32 * ucb32)

    def whole(a):
        return pl.BlockSpec(a.shape, lambda c, n=len(a.shape): (0,) * n)

    def rev(c):
        return nchunk - 1 - c

    xblk = pl.BlockSpec((b, tc * nt, LANES), lambda c: (0, rev(c), 0))
    prev = pl.BlockSpec((b, nt, LANES), lambda c: (0, jnp.maximum(rev(c) * tc - 1, 0), 0))
    sd = jax.ShapeDtypeStruct
    blk = sd(bre.shape, F32)
    acc_shapes = [blk, blk, sd(cre.shape, F32), sd(cre.shape, F32), sd(lam_re.shape, F32), sd(lam_re.shape, F32), sd((1, sw), F32)]
    return pl.pallas_call(
        body, grid=(nchunk,),
        in_specs=[pl.BlockSpec((b, tc, sw), lambda c: (0, rev(c), ucol)), pl.BlockSpec((b, tc, sw), lambda c: (0, rev(c), 0)),
                  xblk, xblk, prev, prev] + [whole(a) for a in (lam_re, lam_im, bre, bim, cre, cim, dvec)],
        out_specs=[pl.BlockSpec((b, tc, sw), lambda c: (0, rev(c), 0))] + [whole(a) for a in acc_shapes],
        out_shape=[sd((b, s, sw), F32)] + acc_shapes,
        scratch_shapes=[pltpu.VMEM((b, tc * nt, LANES), F32)] * 2 + [pltpu.VMEM((b, nt, LANES), F32)] * 2, name="s5_bwd",
        compiler_params=_cparams(("arbitrary",)))(proj3, dy3, xr3, xi3, xr3, xi3, lam_re, lam_im, bre, bim, cre, cim, dvec)


def _s5_blocks(bb, cc_, ncb):
    g, n, p = bb.shape
    mask = jnp.asarray(_s5_tile_mask(g, p, ncb))
    nt, gpt, gpc = mask.shape
    bbt, cct = bb.reshape(nt, gpt, n, p), cc_.reshape(nt, gpt, n, p)
    spread = mask[:, :, :, None, None]
    bblk = (bbt[:, :, None] * spread).transpose(0, 2, 3, 1, 4).reshape(nt, gpc * n, gpt * p)
    cblk = (cct[:, :, None] * spread).transpose(0, 1, 4, 2, 3).reshape(nt, gpt * p, gpc * n)
    return bblk, cblk


def _s5_tile_mask(g, p, ncb):
    nt, gpt, gpc = g * p // LANES, LANES // p, g // ncb
    mask = np.zeros((nt, gpt, gpc), np.float32)
    for t in range(nt):
        for gl in range(gpt):
            mask[t, gl, (t * gpt + gl) % gpc] = 1.0
    return mask


def _s5_unblock(dbblk, dcblk, g, n, p, ncb):
    mask = jnp.asarray(_s5_tile_mask(g, p, ncb))
    nt, gpt, gpc = mask.shape
    db = jnp.sum(dbblk.reshape(nt, gpc, n, gpt, p) * mask.transpose(0, 2, 1)[:, :, None, :, None], axis=1)
    dc = jnp.sum(dcblk.reshape(nt, gpt, p, gpc, n) * mask[:, :, None, :, None], axis=3)
    return db.transpose(0, 2, 1, 3).reshape(g, n, p), dc.transpose(0, 1, 3, 2).reshape(g, n, p)


def _adamw_math(w, g, m, v):
    m = ADAM_B1 * m + (1.0 - ADAM_B1) * g
    v = ADAM_B2 * v + (1.0 - ADAM_B2) * (g * g)
    m_hat = m / (1.0 - ADAM_B1 ** ADAM_STEP)
    v_hat = v / (1.0 - ADAM_B2 ** ADAM_STEP)
    delta = -ADAM_LR * (m_hat / (jnp.sqrt(v_hat) + ADAM_EPS) + ADAM_WD * w)
    return delta, m, v


def _adamw_layer(name, parts, w, m, v, layer, prev, tr):
    nparts, r, c = parts.shape
    assert r % tr == 0
    if prev is None:
        prev = [lax.empty(w.shape, F32) for _ in range(4)]

    def body(p_ref, w_ref, m_ref, v_ref, *rest):
        g_out, d_out, m_out, v_out = rest[4:]
        g = p_ref[0].astype(F32)
        for k in range(1, nparts):
            g = g + p_ref[k].astype(F32)
        delta, mn, vn = _adamw_math(w_ref[...], g, m_ref[...], v_ref[...])
        g_out[...] = g
        d_out[...] = delta
        m_out[...] = mn
        v_out[...] = vn

    blk = pl.BlockSpec((None, tr, c), lambda i: (layer, i, 0))
    kept = pl.BlockSpec(memory_space=pl.ANY)
    return pl.pallas_call(
        body, grid=(r // tr,), in_specs=[pl.BlockSpec((nparts, tr, c), lambda i: (0, i, 0)), blk, blk, blk] + [kept] * 4,
        out_specs=[blk] * 4, out_shape=[jax.ShapeDtypeStruct(w.shape, F32)] * 4, name=name,
        input_output_aliases={4 + k: k for k in range(4)},
        compiler_params=_cparams(("arbitrary",)))(parts, w, m, v, *prev)


def _pack_small(pieces):
    flat = jnp.concatenate([p.reshape(-1) for p in pieces])
    n = flat.shape[0]
    unit = SUBLANES * LANES
    padded = -(-n // unit) * unit
    return jnp.pad(flat, (0, padded - n)).reshape(padded // LANES, LANES)


def _pack_small_layers(pieces):
    nl = pieces[0].shape[0]
    flat = jnp.concatenate([p.reshape(nl, -1) for p in pieces], axis=1)
    n = flat.shape[1]
    unit = SUBLANES * LANES
    padded = -(-n // unit) * unit
    return jnp.pad(flat, ((0, 0), (0, padded - n))).reshape(nl, padded // LANES, LANES)


def _step(x, positions, weights, moments_m, moments_v, loss_target, distributed):
    f32 = F32
    bsz, seq, dm = x.shape
    depth = weights["w_in"].shape[0]
    aw = weights["attn_gain"].shape[1]
    sw = weights["ssm_gain"].shape[1]
    dff = weights["b_ff1"].shape[1]
    ng, npst = weights["ssm_a_re"].shape[1:]
    gdim = weights["ssm_d"].shape[2]
    in_w = 3 * aw + sw
    t_rows = bsz * seq
    alpha = (2.0 * depth) ** 0.25
    ncb = sw // LANES
    nt = ng * npst // LANES
    ndev = NDEV if distributed else 1
    tm = tl = 512

    gather_modes = ["gather"] * len(BIG_NAMES)
    no_token = jnp.zeros((SUBLANES, LANES), f32)

    def shards(l, names):
        return [weights[n][l].astype(MXU_DTYPE) for n in names]

    if distributed:
        first, _ = _exchange_start("weights_start_l0_in", shards(0, BIG_NAMES[:1]), gather_modes[:1])
        (g_in0,) = _exchange_wait("weights_wait_l0_in", first, gather_modes[:1], positions)
        rest0, tok_rest0 = _exchange_start("weights_start_l0_rest", shards(0, BIG_NAMES[1:]), gather_modes[1:], after=g_in0)

    half = HEAD_DIM // 2
    inv_freq = ROPE_THETA ** (-jnp.arange(half, dtype=f32) * 2.0 / HEAD_DIM)
    reps = LANES // half
    inv_row = jnp.tile(inv_freq, reps)[None, :]
    sign_row = jnp.tile(jnp.concatenate([-jnp.ones((half,), f32), jnp.ones((half,), f32)]), LANES // HEAD_DIM)[None, :]
    ctab, stab = _rope_tables(positions[..., None], inv_row, sign_row)

    def row(v):
        return v.reshape(1, -1)

    h = x.reshape(t_rows, dm)
    saved = []
    for l in range(depth):
        tok_in = tok_mix = no_token
        if not distributed:
            g_in, g_glu, g_out, g_ff1, g_ff2 = [weights[n][l].astype(MXU_DTYPE)[None] for n in BIG_NAMES]
        elif l == 0:
            g_in, tok_in = g_in0, tok_rest0
        else:
            g_in, g_glu, g_out, g_ff1, g_ff2 = _exchange_wait(f"weights_wait_l{l}", next_gather, gather_modes, h)
            if l + 1 < depth:
                next_gather, tok_in = _exchange_start(f"weights_start_l{l + 1}", shards(l + 1, BIG_NAMES), gather_modes, after=g_in)
        w_in_l = g_in

        def in_proj(rv, cr):
            return [jnp.concatenate([_mm(rv[0], cr[0][j]) for j in range(ndev)], axis=-1)], []

        (proj,), _ = _rows_call("in_proj", in_proj, [h], [w_in_l, tok_in], [(in_w, f32)], [], tl)
        proj3 = proj.reshape(bsz, seq, in_w)
        attn3, lse3 = _attn_fwd(proj3, ctab, stab, aw)
        if distributed and l == 0:
            g_glu, g_out, g_ff1, g_ff2 = _exchange_wait("weights_wait_l0_rest", rest0, gather_modes[1:], attn3)
            if depth > 1:
                next_gather, tok_mix = _exchange_start("weights_start_l1", shards(1, BIG_NAMES), gather_modes, after=g_glu)
        w_glu_l = g_glu.reshape(sw, sw)
        w_out_l = g_out.reshape(dm, dm)
        w_ff1_l = g_ff1
        w_ff2_l = g_ff2.reshape(dff, dm)

        a_re, a_im = weights["ssm_a_re"][l][:, None, :], weights["ssm_a_im"][l][:, None, :]
        log_dt = weights["ssm_log_dt"][l][:, None, None]
        bt_re = weights["ssm_b_re"][l].transpose(0, 2, 1)
        bt_im = weights["ssm_b_im"][l].transpose(0, 2, 1)
        lb_re, lb_im, bb_re, bb_im = _s5_params(a_re, a_im, log_dt, bt_re, bt_im)
        lam_re, lam_im = lb_re.reshape(nt, LANES), lb_im.reshape(nt, LANES)
        bre, cre = _s5_blocks(bb_re, weights["ssm_c_re"][l], ncb)
        bim, cim = _s5_blocks(bb_im, weights["ssm_c_im"][l], ncb)
        s5c = [a_.astype(MXU_DTYPE) for a_ in (bre, bim, cre, cim)]
        dvec = row(weights["ssm_d"][l])
        ypre3, xr3, xi3 = _s5_fwd(proj3, 3 * aw // sw, lam_re, lam_im, *s5c, dvec, sw)
        attn, ypre = attn3.reshape(t_rows, aw), ypre3.reshape(t_rows, sw)

        b_glu, ga, gs = row(weights["b_glu"][l]), row(weights["attn_gain"][l]), row(weights["ssm_gain"][l])

        def mix(rv, cr):
            at, yp = rv
            g = _gelu(yp)
            ssm = g * jax.nn.sigmoid(_mm(g, cr[0][...]) + cr[1][...])
            return [jnp.concatenate([_rms_norm(at, cr[2][...]), _rms_norm(ssm, cr[3][...])], axis=-1)], []

        (mixed,), _ = _rows_call("mix", mix, [attn, ypre], [w_glu_l, b_glu, ga, gs, tok_mix], [(dm, MXU_DTYPE)], [], tl)

        b_out, g1, b1 = row(weights["b_out"][l]), row(weights["ln1_g"][l]), row(weights["ln1_b"][l])

        def out_proj(rv, cr):
            pre = alpha * rv[1] + _mm(rv[0], cr[0][...]) + cr[1][...]
            return [pre, _layer_norm(pre, cr[2][...], cr[3][...])], []

        (pre1, h1), _ = _rows_call("out_proj", out_proj, [mixed, h], [w_out_l, b_out, g1, b1], [(dm, f32), (dm, f32)], [], tl)

        b_ff1 = row(weights["b_ff1"][l])

        def ff1(rv, cr):
            pre_act = jnp.concatenate([_mm(rv[0], cr[0][j]) for j in range(ndev)], axis=-1) + cr[1][...]
            return [jnp.square(jnp.maximum(pre_act, 0.0))], []

        (act,), _ = _rows_call("ff1", ff1, [h1], [w_ff1_l, b_ff1], [(dff, MXU_DTYPE)], [], tm)

        b_ff2, g2, b2 = row(weights["b_ff2"][l]), row(weights["ln2_g"][l]), row(weights["ln2_b"][l])

        def ff2(rv, cr):
            pre = alpha * rv[1] + _mm(rv[0], cr[0][...]) + cr[1][...]
            return [pre, _layer_norm(pre, cr[2][...], cr[3][...])], []

        (pre2, h2), _ = _rows_call("ff2", ff2, [act, h1], [w_ff2_l, b_ff2, g2, b2], [(dm, f32), (dm, f32)], [], tm)

        saved.append(dict(h=h, proj3=proj3, attn=attn, lse3=lse3, attn3=attn3, ypre=ypre, xr3=xr3, xi3=xi3, mixed=mixed, pre1=pre1,
                          h1=h1, act=act, pre2=pre2, w_in=w_in_l, w_glu=w_glu_l, w_out=w_out_l, w_ff1=w_ff1_l, w_ff2=w_ff2_l,
                          lam=(lam_re, lam_im), s5c=s5c, dvec=dvec, b_glu=b_glu, ga=ga, gs=gs, g1=g1, g2=g2,
                          s5in=(a_re, a_im, log_dt, bt_re, bt_im)))
        h = h2

    g2_last = saved[-1]["g2"]

    def loss_fn(rv, cr):
        y, tgt, pre = rv
        err = y - tgt
        part = 0.5 * jnp.sum(jnp.mean(err * err, axis=-1, keepdims=True), axis=0, keepdims=True)
        dpre, dg, db = _layer_norm_bwd(err * (1.0 / dm), pre, cr[0][...])
        return [dpre], [jnp.broadcast_to(part, (1, LANES)), dg, db, _colsum(dpre)]

    (dpre2,), (loss_acc, dg2, db2, dbff2) = _rows_call(
        "loss", loss_fn, [h, loss_target.reshape(t_rows, dm), saved[-1]["pre2"]], [g2_last], [(dm, f32)],
        [(1, LANES), (1, dm), (1, dm), (1, dm)], tl)
    loss = loss_acc[0, 0]
    if distributed:
        loss = lax.psum(loss, MESH_AXES)

    big_parts = {n: [None] * depth for n in BIG_NAMES}
    small_parts = [None] * depth
    grad_handles = [None] * depth
    grad_modes = ["scatter"] * len(BIG_NAMES) + ["gather"]
    grad_x = None
    for l in reversed(range(depth)):
        sv = saved[l]

        def ff2_bwd(rv, cr):
            r = rv[1].astype(F32)
            relu2 = jnp.where(r > 0.0, (2.0 * r) * lax.rsqrt(r), 0.0)
            da = _mm_nt(rv[0], cr[0][...]) * relu2
            return [da], [_colsum(da)]

        (dact,), (dbff1,) = _rows_call("ff2_bwd", ff2_bwd, [dpre2, sv["act"]], [sv["w_ff2"]], [(dff, MXU_DTYPE)], [(1, dff)], tm)
        big_parts["w_ff2"][l] = _wgrad_call("wgrad_ff2", sv["act"], dpre2, "x", ndev, ndev // 2 if ndev > 1 else 1,
                                            (dff // ndev, dm), tm)
        big_parts["w_ff1"][l] = _wgrad_call("wgrad_ff1", sv["h1"], dact, "y", ndev, ndev // 2 if ndev > 1 else 1,
                                            (dm, dff // ndev), tm)

        def ff1_bwd(rv, cr):
            dacc = alpha * rv[1]
            wpb = dff // ndev
            for j in range(ndev):
                dacc = dacc + _mm_nt(rv[0][:, j * wpb:(j + 1) * wpb], cr[0][j])
            dpre, dg, db = _layer_norm_bwd(dacc, rv[2], cr[1][...])
            return [dpre], [dg, db, _colsum(dpre)]

        (dpre1,), (dg1, db1, dbout) = _rows_call("ff1_bwd", ff1_bwd, [dact, dpre2, sv["pre1"]], [sv["w_ff1"], sv["g1"]],
                                                 [(dm, f32)], [(1, dm)] * 3, tm)
        big_parts["w_out"][l] = _wgrad_call("wgrad_out", sv["mixed"], dpre1, "x", ndev, ndev, (dm // ndev, dm), tl)

        def mix_bwd(rv, cr):
            dp, at, yp = rv
            w_out_r, w_glu_r, bg, ga_, gs_ = cr
            dmixed = _mm_nt(dp, w_out_r[...])
            g = _gelu(yp)
            sig = jax.nn.sigmoid(_mm(g, w_glu_r[...]) + bg[...])
            ssm = g * sig
            dat, dga = _rms_norm_bwd(dmixed[:, :aw], at, ga_[...])
            dssm, dgs = _rms_norm_bwd(dmixed[:, aw:], ssm, gs_[...])
            dz = dssm * g * sig * (1.0 - sig)
            dg = dssm * sig + _mm_nt(dz, w_glu_r[...])
            return [dat, dg * _gelu_grad(yp), dz], [dga, dgs, _colsum(dz)]

        (dattn, dypre, dz), (dga, dgs, dbglu) = _rows_call(
            "mix_bwd", mix_bwd, [dpre1, sv["attn"], sv["ypre"]], [sv["w_out"], sv["w_glu"], sv["b_glu"], sv["ga"], sv["gs"]],
            [(aw, f32), (sw, f32), (sw, MXU_DTYPE)], [(1, aw), (1, sw), (1, sw)], tl)
        big_parts["w_glu"][l] = _wgrad_call("wgrad_glu", sv["ypre"], dz, "x", 1, 1, (sw, sw), tl, prologue=_gelu).reshape(
            ndev, sw // ndev, sw)

        du3, dbre, dbim, dcre, dcim, dlr, dli, dd = _s5_bwd(
            sv["proj3"], 3 * aw // sw, dypre.reshape(bsz, seq, sw), sv["xr3"], sv["xi3"], *sv["lam"], *sv["s5c"], sv["dvec"], sw)
        dbb_re, dc_re = _s5_unblock(dbre, dcre, ng, gdim, npst, ncb)
        dbb_im, dc_im = _s5_unblock(dbim, dcim, ng, gdim, npst, ncb)
        da_re, da_im, dldt, dbt_re, dbt_im = _s5_params_bwd(
            *sv["s5in"], (dlr.reshape(ng, 1, npst), dli.reshape(ng, 1, npst), dbb_re, dbb_im))

        dq3, dk3, dv3 = _attn_bwd(sv["proj3"], ctab, stab, dattn.reshape(bsz, seq, aw), sv["attn3"], sv["lse3"], aw)
        dproj = [dq3.reshape(t_rows, aw), dk3.reshape(t_rows, aw), dv3.reshape(t_rows, aw), du3.reshape(t_rows, sw)]
        big_parts["w_in"][l] = _wgrad_call("wgrad_in", sv["h"], dproj, "y", ndev, ndev, (dm, in_w // ndev), tl)

        small_parts[l] = dict(
            attn_gain=dga, ssm_gain=dgs, ssm_a_re=da_re[:, 0], ssm_a_im=da_im[:, 0], ssm_log_dt=dldt[:, 0, 0], ssm_b_re=dbt_re.transpose(0, 2, 1),
            ssm_b_im=dbt_im.transpose(0, 2, 1), ssm_c_re=dc_re, ssm_c_im=dc_im, ssm_d=dd, b_glu=dbglu, b_out=dbout, ln1_g=dg1,
            ln1_b=db1, b_ff1=dbff1, b_ff2=dbff2, ln2_g=dg2, ln2_b=db2)

        layer_grads = [big_parts[n][l] for n in BIG_NAMES] + [_pack_small([small_parts[l][n] for n in SMALL_NAMES])]
        if distributed:
            grad_handles[l], token = _exchange_start(f"grads_start_l{l}", layer_grads, grad_modes)
        else:
            grad_handles[l], token = layer_grads, jnp.zeros((SUBLANES, LANES), f32)

        wpb = in_w // ndev
        if l > 0:
            prev = saved[l - 1]

            def in_bwd(rv, cr):
                dacc = alpha * rv[4]
                dp = jnp.concatenate([v_.astype(MXU_DTYPE) for v_ in rv[:4]], axis=-1)
                for j in range(ndev):
                    dacc = dacc + _mm_nt(dp[:, j * wpb:(j + 1) * wpb], cr[0][j])
                dpre, dg, db = _layer_norm_bwd(dacc, rv[5], cr[1][...])
                return [dpre], [dg, db, _colsum(dpre)]

            (dpre2,), (dg2, db2, dbff2) = _rows_call("in_bwd", in_bwd, dproj + [dpre1, prev["pre2"]],
                                                     [sv["w_in"], prev["g2"], token], [(dm, f32)], [(1, dm)] * 3, tl)
        else:
            def in_bwd0(rv, cr):
                dacc = alpha * rv[4]
                dp = jnp.concatenate([v_.astype(MXU_DTYPE) for v_ in rv[:4]], axis=-1)
                for j in range(ndev):
                    dacc = dacc + _mm_nt(dp[:, j * wpb:(j + 1) * wpb], cr[0][j])
                return [dacc], []

            (grad_x,), _ = _rows_call("in_bwd0", in_bwd0, dproj + [dpre1], [sv["w_in"], token], [(dm, f32)], [], tl)

    small_shapes = [weights[n].shape[1:] for n in SMALL_NAMES]
    outs = {n: None for n in BIG_NAMES}
    packed = [_pack_small_layers([d[n] for n in SMALL_NAMES]) for d in (weights, moments_m, moments_v)]
    small_out = None
    after = grad_x
    for l in reversed(range(depth)):
        if distributed:
            recv = _exchange_wait(f"grads_wait_l{l}", grad_handles[l], grad_modes, after)
        else:
            recv = [g_[None] if i == len(BIG_NAMES) else g_ for i, g_ in enumerate(grad_handles[l])]
        for n, parts in zip(BIG_NAMES, recv[:-1], strict=True):
            outs[n] = _adamw_layer("adamw_" + n, parts, weights[n], moments_m[n], moments_v[n], l, outs[n],
                                   min(parts.shape[1], 256))
        small_out = _adamw_layer("adamw_small", recv[-1], *packed, l, small_out, recv[-1].shape[1])
        after = small_out[0]
    for k in range(4):
        flat = small_out[k].reshape(depth, -1)
        off = 0
        for n, shp in zip(SMALL_NAMES, small_shapes, strict=True):
            sz = int(np.prod(shp))
            outs.setdefault(n, [None] * 4)
            outs[n][k] = flat[:, off:off + sz].reshape((depth,) + tuple(shp))
            off += sz

    result = [loss, grad_x.reshape(bsz, seq, dm)]
    for k in range(4):
        result += [outs[n][k] for n in WEIGHT_ORDER]
    return tuple(result)


def kernel(x, positions, w_in, attn_gain, ssm_gain, ssm_a_re, ssm_a_im, ssm_log_dt, ssm_b_re, ssm_b_im, ssm_c_re, ssm_c_im, ssm_d, w_glu, b_glu, w_out, b_out, ln1_g, ln1_b, w_ff1, b_ff1, w_ff2, b_ff2, ln2_g, ln2_b, loss_target, m_w_in, m_attn_gain, m_ssm_gain, m_ssm_a_re, m_ssm_a_im, m_ssm_log_dt, m_ssm_b_re, m_ssm_b_im, m_ssm_c_re, m_ssm_c_im, m_ssm_d, m_w_glu, m_b_glu, m_w_out, m_b_out, m_ln1_g, m_ln1_b, m_w_ff1, m_b_ff1, m_w_ff2, m_b_ff2, m_ln2_g, m_ln2_b, v_w_in, v_attn_gain, v_ssm_gain, v_ssm_a_re, v_ssm_a_im, v_ssm_log_dt, v_ssm_b_re, v_ssm_b_im, v_ssm_c_re, v_ssm_c_im, v_ssm_d, v_w_glu, v_b_glu, v_w_out, v_b_out, v_ln1_g, v_ln1_b, v_w_ff1, v_b_ff1, v_w_ff2, v_b_ff2, v_ln2_g, v_ln2_b):
    loc = locals()
    weights = {n: loc[n] for n in WEIGHT_ORDER}
    moments_m = {n: loc["m_" + n] for n in WEIGHT_ORDER}
    moments_v = {n: loc["v_" + n] for n in WEIGHT_ORDER}
    return _step(x, positions, weights, moments_m, moments_v, loss_target, distributed=True)
```

```python
import functools
import math

import jax
import jax.numpy as jnp
import numpy as np
from jax import lax
from jax.experimental import pallas as pl
from jax.experimental.pallas import tpu as pltpu

F32 = jnp.float32
MXU_DTYPE = jnp.bfloat16

HEAD_DIM = 64
DILATION_PAIRS = ((128, 1), (512, 4), (2048, 16))
ROPE_THETA = 10000.0
SSM_GROUP_DIM = 16
SSM_STATE = 64
LN_EPS = 1e-5
RMS_EPS = 1e-6
NEG_INF = -1e30
ADAM_LR, ADAM_B1, ADAM_B2, ADAM_EPS, ADAM_WD, ADAM_STEP = 0.001, 0.9, 0.999, 1e-08, 0.01, 10

LANES = 128
SUBLANES = 8
QBLK = 128
VMEM_LIMIT = 56 * 2**20
MESH_AXES = ("x", "y", "c")
NDEV = 8

SMALL_NAMES = ("attn_gain", "ssm_gain", "ssm_a_re", "ssm_a_im", "ssm_log_dt", "ssm_b_re", "ssm_b_im", "ssm_c_re",
               "ssm_c_im", "ssm_d", "b_glu", "b_out", "ln1_g", "ln1_b", "b_ff1", "b_ff2", "ln2_g", "ln2_b")
BIG_NAMES = ("w_in", "w_glu", "w_out", "w_ff1", "w_ff2")
WEIGHT_ORDER = ("w_in", "attn_gain", "ssm_gain", "ssm_a_re", "ssm_a_im", "ssm_log_dt", "ssm_b_re", "ssm_b_im", "ssm_c_re",
                "ssm_c_im", "ssm_d", "w_glu", "b_glu", "w_out", "b_out", "ln1_g", "ln1_b", "w_ff1", "b_ff1", "w_ff2",
                "b_ff2", "ln2_g", "ln2_b")


def _cparams(sem=None):
    return pltpu.CompilerParams(dimension_semantics=sem, vmem_limit_bytes=VMEM_LIMIT)


def _mm(a, b):
    return jnp.dot(a.astype(MXU_DTYPE), b.astype(MXU_DTYPE), preferred_element_type=F32)


def _mm_nt(a, b):
    return lax.dot_general(a.astype(MXU_DTYPE), b.astype(MXU_DTYPE), (((1,), (1,)), ((), ())), preferred_element_type=F32)


def _mm_tn(a, b):
    return lax.dot_general(a.astype(MXU_DTYPE), b.astype(MXU_DTYPE), (((0,), (0,)), ((), ())), preferred_element_type=F32)


def _colsum(x):
    return jnp.sum(x, axis=0, keepdims=True)


def _layer_norm(x, g, b):
    mu = jnp.mean(x, axis=-1, keepdims=True)
    xc = x - mu
    var = jnp.mean(xc * xc, axis=-1, keepdims=True)
    return xc * lax.rsqrt(var + LN_EPS) * g + b


def _layer_norm_bwd(dy, pre, g):
    mu = jnp.mean(pre, axis=-1, keepdims=True)
    xc = pre - mu
    var = jnp.mean(xc * xc, axis=-1, keepdims=True)
    r = lax.rsqrt(var + LN_EPS)
    xhat = xc * r
    dyg = dy * g
    dpre = r * (dyg - jnp.mean(dyg, axis=-1, keepdims=True) - xhat * jnp.mean(dyg * xhat, axis=-1, keepdims=True))
    return dpre, _colsum(dy * xhat), _colsum(dy)


def _rms_norm(x, g):
    return x * lax.rsqrt(jnp.mean(x * x, axis=-1, keepdims=True) + RMS_EPS) * g


def _rms_norm_bwd(dy, x, g):
    r = lax.rsqrt(jnp.mean(x * x, axis=-1, keepdims=True) + RMS_EPS)
    dyg = dy * g
    dx = dyg * r - x * (r * r * r) * jnp.mean(dyg * x, axis=-1, keepdims=True)
    return dx, _colsum(dy * x * r)


_GELU_C = math.sqrt(2.0 / math.pi)


def _gelu(x):
    return 0.5 * x * (1.0 + jnp.tanh(_GELU_C * (x + 0.044715 * (x * x * x))))


def _gelu_grad(x):
    t = jnp.tanh(_GELU_C * (x + 0.044715 * (x * x * x)))
    return 0.5 * (1.0 + t) + 0.5 * x * (1.0 - t * t) * (_GELU_C * (1.0 + 3.0 * 0.044715 * x * x))


def _rows_call(name, fn, rows, consts, out_rows, out_accs, tm):
    rows = [r if isinstance(r, tuple) else (r, r.shape[1], 0) for r in rows]
    m = rows[0][0].shape[0]
    assert m % tm == 0
    nr, nc, no, na = len(rows), len(consts), len(out_rows), len(out_accs)

    def body(*refs):
        rr, cr = refs[:nr], refs[nr:nr + nc]
        orr, ar = refs[nr + nc:nr + nc + no], refs[nr + nc + no:]
        outs, accs = fn([r[...] for r in rr], cr)
        for o, v in zip(orr, outs, strict=True):
            o[...] = v.astype(o.dtype)
        if na:
            first = pl.program_id(0) == 0

            @pl.when(first)
            def _():
                for a, v in zip(ar, accs, strict=True):
                    a[...] = v

            @pl.when(jnp.logical_not(first))
            def _():
                for a, v in zip(ar, accs, strict=True):
                    a[...] += v

    def whole(shape):
        return pl.BlockSpec(shape, lambda i, n=len(shape): (0,) * n)

    in_specs = [pl.BlockSpec((tm, w), lambda i, cb=cb: (i, cb)) for _, w, cb in rows] + [whole(c.shape) for c in consts]
    out_specs = [pl.BlockSpec((tm, w), lambda i: (i, 0)) for w, _ in out_rows] + [whole(s) for s in out_accs]
    out_shape = [jax.ShapeDtypeStruct((m, w), dt) for w, dt in out_rows] + [jax.ShapeDtypeStruct(s, F32) for s in out_accs]
    res = pl.pallas_call(body, grid=(m // tm,), in_specs=in_specs, out_specs=out_specs, out_shape=out_shape, name=name,
                         compiler_params=_cparams(("arbitrary",)))(*[r[0] for r in rows], *consts)
    return res[:no], res[no:]


def _wgrad_call(name, x, dy, split, nblk, jb, blk_shape, tm, prologue=None):
    dys = list(dy) if isinstance(dy, (list, tuple)) else [dy]
    m = x.shape[0]
    kk, nn = blk_shape
    assert m % tm == 0 and nblk % jb == 0 and (len(dys) == 1 or (split == "y" and jb == nblk))
    xw = kk * jb if split == "x" else x.shape[1]
    yws = [d.shape[1] for d in dys] if (split == "x" or len(dys) > 1) else [nn * jb]
    nrow = m // tm

    def body(x_ref, *rest):
        dy_refs, o_ref, acc_ref = rest[:len(dys)], rest[-2], rest[-1]
        i = pl.program_id(1)

        @pl.when(i == 0)
        def _():
            acc_ref[...] = jnp.zeros_like(acc_ref)

        xv = x_ref[...]
        if prologue is not None:
            xv = prologue(xv)
        xv = xv.astype(MXU_DTYPE)
        dv = [r[...].astype(MXU_DTYPE) for r in dy_refs]
        dv = dv[0] if len(dv) == 1 else jnp.concatenate(dv, axis=-1)
        for j in range(jb):
            xa = xv[:, j * kk:(j + 1) * kk] if split == "x" else xv
            da = dv[:, j * nn:(j + 1) * nn] if split == "y" else dv
            acc_ref[j] += _mm_tn(xa, da)

        @pl.when(i == nrow - 1)
        def _():
            o_ref[...] = acc_ref[...].astype(o_ref.dtype)

    in_specs = [pl.BlockSpec((tm, xw), (lambda j, i: (i, j)) if split == "x" else (lambda j, i: (i, 0)))]
    in_specs += [pl.BlockSpec((tm, yw), (lambda j, i: (i, j)) if (split == "y" and len(dys) == 1) else (lambda j, i: (i, 0)))
                 for yw in yws]
    return pl.pallas_call(
        body, grid=(nblk // jb, nrow), in_specs=in_specs,
        out_specs=pl.BlockSpec((jb, kk, nn), lambda j, i: (j, 0, 0)),
        out_shape=jax.ShapeDtypeStruct((nblk, kk, nn), MXU_DTYPE),
        scratch_shapes=[pltpu.VMEM((jb, kk, nn), F32)], name=name,
        compiler_params=_cparams(("arbitrary", "arbitrary")))(x, *dys)


_HBM_SPEC = pl.BlockSpec(memory_space=pltpu.HBM)
_SEM_SPEC = pl.BlockSpec(memory_space=pltpu.SEMAPHORE)
_EFFECT = pltpu.SideEffectType.DATAFLOW_SIDE_EFFECTING


def _my_index():
    return 4 * lax.axis_index("x") + 2 * lax.axis_index("y") + lax.axis_index("c")


def _peer_copies(ins, lands, send_sems, recv_sems, modes):
    x, y, c = lax.axis_index("x"), lax.axis_index("y"), lax.axis_index("c")
    me = 4 * x + 2 * y + c
    pairs = []
    for k in range(NDEV - 1):
        fx, fy, fc = ((k + 1) >> 2) & 1, ((k + 1) >> 1) & 1, (k + 1) & 1
        px, py, pc = (x + fx) % 2, (y + fy) % 2, (c + fc) % 2
        idx = 4 * px + 2 * py + pc
        for a, md in enumerate(modes):
            src = ins[a] if md == "gather" else ins[a].at[idx]
            sem = a * (NDEV - 1) + k
            common = dict(src_ref=src, send_sem=send_sems.at[sem], recv_sem=recv_sems.at[sem], device_id=(px, py, pc),
                          device_id_type=pl.DeviceIdType.MESH)
            pairs.append((pltpu.make_async_remote_copy(dst_ref=lands[a].at[me], **common),
                          pltpu.make_async_remote_copy(dst_ref=lands[a].at[idx], **common)))
    return pairs


def _exchange_start(name, arrays, modes, after=None):
    n = len(arrays)
    extra = [] if after is None else [after]
    me = _my_index()
    lands = []
    for a, md in zip(arrays, modes, strict=True):
        piece = a if md == "gather" else lax.dynamic_index_in_dim(a, me, 0, keepdims=False)
        lands.append(lax.dynamic_update_index_in_dim(lax.empty((NDEV,) + piece.shape, a.dtype), piece, me, 0))

    def body(*refs):
        ins, lnd = refs[:n], refs[n:2 * n]
        send_sems, recv_sems = refs[2 * n + len(extra)], refs[2 * n + len(extra) + 1]
        token = refs[-1]
        for out_copy, _ in _peer_copies(ins, lnd, send_sems, recv_sems, modes):
            out_copy.start()
        token[...] = jnp.zeros_like(token)

    sems = pltpu.SemaphoreType.DMA((n * (NDEV - 1),))
    thru = [pltpu.HBM(a.shape, a.dtype) for a in list(arrays) + lands]
    res = pl.pallas_call(
        body, name=name, out_shape=(sems, sems, *thru, jax.ShapeDtypeStruct((SUBLANES, LANES), F32)),
        in_specs=[_HBM_SPEC] * (2 * n) + [pl.BlockSpec(memory_space=pl.ANY)] * len(extra),
        out_specs=(_SEM_SPEC, _SEM_SPEC, *([_HBM_SPEC] * (2 * n)), pl.BlockSpec(memory_space=pltpu.VMEM)),
        input_output_aliases={i: 2 + i for i in range(2 * n)},
        compiler_params=pltpu.CompilerParams(has_side_effects=_EFFECT),
    )(*[pltpu.with_memory_space_constraint(a, pltpu.HBM) for a in list(arrays) + lands], *extra)
    return (res[0], res[1], res[2:2 + n], res[2 + n:2 + 2 * n]), res[-1]


def _exchange_wait(name, handle, modes, after):
    send_sems, recv_sems, ins_thru, lands_thru = handle
    n = len(ins_thru)

    def body(*refs):
        ins, lnd = refs[:n], refs[n:2 * n]
        for out_copy, arrival in _peer_copies(ins, lnd, refs[2 * n], refs[2 * n + 1], modes):
            out_copy.wait_send()
            arrival.wait_recv()

    thru = [pltpu.HBM(a.shape, a.dtype) for a in list(ins_thru) + list(lands_thru)]
    res = pl.pallas_call(
        body, name=name, out_shape=tuple(thru),
        in_specs=[_HBM_SPEC] * (2 * n) + [_SEM_SPEC, _SEM_SPEC, pl.BlockSpec(memory_space=pl.ANY)],
        out_specs=tuple([_HBM_SPEC] * (2 * n)), input_output_aliases={i: i for i in range(2 * n)},
        compiler_params=pltpu.CompilerParams(has_side_effects=_EFFECT),
    )(*ins_thru, *lands_thru, send_sems, recv_sems, after)
    return res[n:]


def _rope_tables(positions3, inv_freq_row, sign_row):
    b, s, _ = positions3.shape

    def body(pos_ref, f_ref, sg_ref, c_ref, s_ref):
        ang = pos_ref[...].astype(F32) * f_ref[...]
        c_ref[...] = jnp.cos(ang)
        s_ref[...] = jnp.sin(ang) * sg_ref[...]

    row = pl.BlockSpec((1, LANES), lambda i: (0, 0))
    blk = pl.BlockSpec((None, s, LANES), lambda i: (i, 0, 0))
    return pl.pallas_call(
        body, grid=(b,), in_specs=[pl.BlockSpec((None, s, 1), lambda i: (i, 0, 0)), row, row], out_specs=[blk, blk],
        out_shape=[jax.ShapeDtypeStruct((b, s, LANES), F32)] * 2, name="rope_tables",
        compiler_params=_cparams(("arbitrary",)))(positions3, inv_freq_row, sign_row)


def _swap_halves(t):
    lane = lax.broadcasted_iota(jnp.int32, t.shape, 1)
    half = HEAD_DIM // 2
    return jnp.where((lane % HEAD_DIM) < half, pltpu.roll(t, LANES - half, 1), pltpu.roll(t, half, 1))


def _segment_rows(r, d, s):
    n = s // d
    return (pl.ds(r, n, stride=d) if d > 1 else pl.ds(0, s)), pl.ds(r * n, n)


def _head_lanes(shape):
    lane = lax.broadcasted_iota(jnp.int32, shape, len(shape) - 1)
    return lane < HEAD_DIM


def _bmm_nt(a, b):
    return lax.dot_general(a.astype(MXU_DTYPE), b.astype(MXU_DTYPE), (((2,), (2,)), ((0,), (0,))), preferred_element_type=F32)


def _bmm(a, b):
    return lax.dot_general(a.astype(MXU_DTYPE), b.astype(MXU_DTYPE), (((2,), (1,)), ((0,), (0,))), preferred_element_type=F32)


def _bmm_tn(a, b):
    return lax.dot_general(a.astype(MXU_DTYPE), b.astype(MXU_DTYPE), (((1,), (1,)), ((0,), (0,))), preferred_element_type=F32)


def _stack_heads(t3):
    head_a = _head_lanes(t3.shape)
    zero = jnp.zeros_like(t3)
    return jnp.concatenate([jnp.where(head_a, t3, zero), jnp.where(head_a, zero, t3)], axis=1)


QUNIT = QBLK // 2


def _with_previous(t3, nprev):
    shifted = [jnp.concatenate([t3[:k], t3[:-k]], axis=0) for k in range(nprev, 0, -1)]
    return jnp.concatenate(shifted + [t3], axis=1)


def _head_columns(t3):
    return jnp.concatenate([t3[:, :, lo:lo + 1] for lo in range(0, LANES, HEAD_DIM)], axis=1)


def _to_own_unit(t, nprev):
    unit = t.shape[1] // (nprev + 1)
    out = t[:, nprev * unit:]
    for k in range(1, nprev + 1):
        part = t[:, (nprev - k) * unit:(nprev - k + 1) * unit]
        out = out + jnp.concatenate([part[k:], jnp.zeros_like(part[:k])], axis=0)
    return out


def _branch_operands(qh, kh, vh, s, nb):
    nh = LANES // HEAD_DIM
    unit, nprev = (QBLK, 0) if nb == 1 else (QUNIT, QBLK // QUNIT)
    g = s // unit
    q3 = _stack_heads(qh[...].reshape(g, unit, LANES))
    k3, v3 = kh[...].reshape(g, unit, LANES), vh[...].reshape(g, unit, LANES)
    if nprev == 0:
        qi = lax.broadcasted_iota(jnp.int32, (1, nh * unit, unit), 1) % unit
        kj = lax.broadcasted_iota(jnp.int32, (1, nh * unit, unit), 2)
        return q3, k3, v3, kj <= qi, nprev
    shape = (g, nh * unit, (nprev + 1) * unit)
    qi = lax.broadcasted_iota(jnp.int32, shape, 1) % unit
    kj = lax.broadcasted_iota(jnp.int32, shape, 2)
    j = lax.broadcasted_iota(jnp.int32, shape, 0)
    per_block = QBLK // unit
    opens = ((j // per_block) % nb) == 0
    mask = (kj >= qi) & (kj <= qi + QBLK) & ((kj >= QBLK - unit * (j % per_block)) | jnp.logical_not(opens))
    return q3, _with_previous(k3, nprev), _with_previous(v3, nprev), mask, nprev


def _attn_fwd(proj3, ctab, stab, aw):
    b, s, _ = proj3.shape
    npair = aw // LANES
    scale = HEAD_DIM ** -0.5
    nbr = len(DILATION_PAIRS)

    def body(q_ref, k_ref, v_ref, c_ref, s_ref, o_ref, l_ref, qf, kf, vf, qh, kh, vh, op, lp, ob, lb):
        cc, ss = c_ref[...], s_ref[...]
        q2, k2 = q_ref[...], k_ref[...]
        qf[...] = (q2 * cc + _swap_halves(q2) * ss) * scale
        kf[...] = k2 * cc + _swap_halves(k2) * ss
        vf[...] = v_ref[...]
        for br, (window, d) in enumerate(DILATION_PAIRS):
            for r in range(d):
                nat, perm = _segment_rows(r, d, s)
                for dst, src in ((qh, qf), (kh, kf), (vh, vf)):
                    dst[perm, :] = src[nat, :].astype(MXU_DTYPE)
            o_dst, l_dst = (ob.at[br], lb.at[br]) if d == 1 else (op, lp)
            q3, kk, vv, mask, _ = _branch_operands(qh, kh, vh, s, (s // d) // QBLK)
            unit = q3.shape[1] // (LANES // HEAD_DIM)
            head_a = _head_lanes((q3.shape[0], unit, LANES))
            sc = jnp.where(mask, _bmm_nt(q3, kk), NEG_INF)
            mx = jnp.max(sc, axis=-1, keepdims=True)
            p = jnp.exp(sc - mx)
            den = jnp.sum(p, axis=-1, keepdims=True)
            o2 = _bmm(p, vv) / den
            l2 = mx + jnp.log(den)
            o_dst[...] = jnp.where(head_a, o2[:, :unit], o2[:, unit:]).reshape(s, LANES)
            l_dst[...] = jnp.where(head_a, l2[:, :unit], l2[:, unit:]).reshape(s, LANES)
            if d > 1:
                for r in range(d):
                    nat, perm = _segment_rows(r, d, s)
                    ob[br, nat, :] = op[perm, :]
                    lb[br, nat, :] = lp[perm, :]
        ls = [lb[br] for br in range(nbr)]
        mx = functools.reduce(jnp.maximum, ls)
        ws = [jnp.exp(l - mx) for l in ls]
        tot = functools.reduce(lambda a, b_: a + b_, ws)
        o_ref[...] = functools.reduce(lambda a, b_: a + b_, [(w / tot) * ob[br] for br, w in enumerate(ws)])
        l_ref[...] = mx + jnp.log(tot)

    def col(off):
        return pl.BlockSpec((None, s, LANES), lambda bi, hp, off=off: (bi, 0, off + hp))

    tab = pl.BlockSpec((None, s, LANES), lambda bi, hp: (bi, 0, 0))
    f32s = pltpu.VMEM((s, LANES), F32)
    mxs = pltpu.VMEM((s, LANES), MXU_DTYPE)
    br_s = pltpu.VMEM((nbr, s, LANES), F32)
    return pl.pallas_call(
        body, grid=(b, npair), in_specs=[col(0), col(npair), col(2 * npair), tab, tab], out_specs=[col(0), col(0)],
        out_shape=[jax.ShapeDtypeStruct((b, s, aw), F32)] * 2,
        scratch_shapes=[f32s] * 3 + [mxs] * 3 + [f32s] * 2 + [br_s] * 2,
        name="attn_fwd", compiler_params=_cparams(("arbitrary", "arbitrary")))(proj3, proj3, proj3, ctab, stab)


def _attn_bwd(proj3, ctab, stab, dout3, out3, lse3, aw):
    b, s, _ = proj3.shape
    npair = aw // LANES
    scale = HEAD_DIM ** -0.5
    nheads = LANES // HEAD_DIM

    def body(q_ref, k_ref, v_ref, c_ref, s_ref, do_ref, o_ref, l_ref, dq_ref, dk_ref, dv_ref,
             qf, kf, vf, dof, dlf, qh, kh, vh, doh, lpm, dpm, dqp, dkp, dvp, dqn, dkn, dvn):
        cc, ss = c_ref[...], s_ref[...]
        q2, k2 = q_ref[...], k_ref[...]
        qf[...] = (q2 * cc + _swap_halves(q2) * ss) * scale
        kf[...] = k2 * cc + _swap_halves(k2) * ss
        vf[...] = v_ref[...]
        do2 = do_ref[...]
        dof[...] = do2
        dd = do2 * o_ref[...]
        in_a = _head_lanes((s, LANES))
        sum_a = jnp.sum(jnp.where(in_a, dd, 0.0), axis=-1, keepdims=True)
        sum_b = jnp.sum(jnp.where(in_a, 0.0, dd), axis=-1, keepdims=True)
        dlf[...] = jnp.where(in_a, sum_a, sum_b)
        for r_ in (dqn, dkn, dvn):
            r_[...] = jnp.zeros_like(r_)
        for window, d in DILATION_PAIRS:
            for r in range(d):
                nat, perm = _segment_rows(r, d, s)
                for dst, src in ((qh, qf), (kh, kf), (vh, vf), (doh, dof)):
                    dst[perm, :] = src[nat, :].astype(MXU_DTYPE)
                if d > 1:
                    lpm[perm, :] = l_ref[nat, :]
                    dpm[perm, :] = dlf[nat, :]
            l_src, d_src = (l_ref, dlf) if d == 1 else (lpm, dpm)
            q3, kk, vv, mask, nprev = _branch_operands(qh, kh, vh, s, (s // d) // QBLK)
            g_, unit = q3.shape[0], q3.shape[1] // nheads
            head_a = _head_lanes((g_, unit, LANES))
            do3 = _stack_heads(doh[...].reshape(g_, unit, LANES))
            lcol, dcol = _head_columns(l_src[...].reshape(g_, unit, LANES)), _head_columns(d_src[...].reshape(g_, unit, LANES))
            p = jnp.exp(jnp.where(mask, _bmm_nt(q3, kk), NEG_INF) - lcol)
            ds_ = p * (_bmm_nt(do3, vv) - dcol)
            dq2 = _bmm(ds_, kk)
            dq_new = jnp.where(head_a, dq2[:, :unit], dq2[:, unit:]).reshape(s, LANES)
            dk_new = _to_own_unit(_bmm_tn(ds_, q3), nprev).reshape(s, LANES)
            dv_new = _to_own_unit(_bmm_tn(p, do3), nprev).reshape(s, LANES)
            if d == 1:
                dqn[...] += dq_new
                dkn[...] += dk_new
                dvn[...] += dv_new
            else:
                dqp[...] = dq_new
                dkp[...] = dk_new
                dvp[...] = dv_new
                for r in range(d):
                    nat, perm = _segment_rows(r, d, s)
                    dqn[nat, :] += dqp[perm, :]
                    dkn[nat, :] += dkp[perm, :]
                    dvn[nat, :] += dvp[perm, :]
        g = dqn[...] * scale
        dq_ref[...] = g * cc + _swap_halves(g * ss)
        g = dkn[...]
        dk_ref[...] = g * cc + _swap_halves(g * ss)
        dv_ref[...] = dvn[...]

    def col(off):
        return pl.BlockSpec((None, s, LANES), lambda bi, hp, off=off: (bi, 0, off + hp))

    tab = pl.BlockSpec((None, s, LANES), lambda bi, hp: (bi, 0, 0))
    f32s = pltpu.VMEM((s, LANES), F32)
    mxs = pltpu.VMEM((s, LANES), MXU_DTYPE)
    return pl.pallas_call(
        body, grid=(b, npair), in_specs=[col(0), col(npair), col(2 * npair), tab, tab, col(0), col(0), col(0)],
        out_specs=[col(0)] * 3, out_shape=[jax.ShapeDtypeStruct((b, s, aw), F32)] * 3,
        scratch_shapes=[f32s] * 5 + [mxs] * 4 + [f32s] * 8,
        name="attn_bwd", compiler_params=_cparams(("arbitrary", "arbitrary")))(
            proj3, proj3, proj3, ctab, stab, dout3, out3, lse3)


def _s5_discretise(a_re, a_im, log_dt, bt_re, bt_im):
    dt = jnp.exp(log_dt)
    mag = jnp.exp(a_re * dt)
    ang = a_im * dt
    lb_re = mag * jnp.cos(ang)
    lb_im = mag * jnp.sin(ang)
    den = a_re * a_re + a_im * a_im
    nr = lb_re - 1.0
    ni = lb_im
    cr = (nr * a_re + ni * a_im) / den
    ci = (ni * a_re - nr * a_im) / den
    return lb_re, lb_im, cr * bt_re - ci * bt_im, cr * bt_im + ci * bt_re


def _s5_params(a_re, a_im, log_dt, bt_re, bt_im):
    def body(ar, ai, ld, br, bi, o1, o2, o3, o4):
        r = _s5_discretise(ar[...], ai[...], ld[...], br[...], bi[...])
        for o, v in zip((o1, o2, o3, o4), r, strict=True):
            o[...] = v

    sd = jax.ShapeDtypeStruct
    return pl.pallas_call(body, out_shape=[sd(a_re.shape, F32)] * 2 + [sd(bt_re.shape, F32)] * 2, name="s5_params")(
        a_re, a_im, log_dt, bt_re, bt_im)


def _s5_params_bwd(a_re, a_im, log_dt, bt_re, bt_im, cts):
    def body(ar, ai, ld, br, bi, c1, c2, c3, c4, o1, o2, o3, o4, o5):
        _, vjp = jax.vjp(_s5_discretise, ar[...], ai[...], ld[...], br[...], bi[...])
        r = vjp((c1[...], c2[...], c3[...], c4[...]))
        for o, v in zip((o1, o2, o3, o4, o5), r, strict=True):
            o[...] = v

    sd = jax.ShapeDtypeStruct
    return pl.pallas_call(
        body, out_shape=[sd(a_re.shape, F32)] * 2 + [sd(log_dt.shape, F32)] + [sd(bt_re.shape, F32)] * 2, name="s5_params_bwd")(
            a_re, a_im, log_dt, bt_re, bt_im, *cts)


S5_TC = 128


def _time_major(tiles):
    nt, tc, lanes = tiles.shape
    return jnp.swapaxes(tiles, 0, 1).reshape(tc * nt, lanes)


def _tile_major(rows, nt):
    return jnp.swapaxes(rows.astype(MXU_DTYPE).reshape(rows.shape[0] // nt, nt, rows.shape[1]), 0, 1)


def _s5_fwd(proj3, ucol, lam_re, lam_im, bre, bim, cre, cim, dvec, sw):
    b, s, _ = proj3.shape
    nt = lam_re.shape[0]
    ncb = sw // LANES
    tpc = nt // ncb
    tc = S5_TC

    def body(u_ref, lr_ref, li_ref, bre_ref, bim_ref, cre_ref, cim_ref, d_ref, y_ref, xr_ref, xi_ref, sr, si):
        @pl.when(pl.program_id(0) == 0)
        def _():
            sr[...] = jnp.zeros_like(sr)
            si[...] = jnp.zeros_like(si)

        u_all = u_ref[...].reshape(b * tc, sw)
        bur, bui = [], []
        for cb in range(ncb):
            ucb = u_all[:, cb * LANES:(cb + 1) * LANES].astype(MXU_DTYPE)
            for t in range(cb * tpc, (cb + 1) * tpc):
                bur.append(_mm(ucb, bre_ref[t]))
                bui.append(_mm(ucb, bim_ref[t]))
        for bi in range(b):
            rows = slice(bi * tc, (bi + 1) * tc)
            xr_ref[bi] = _time_major(jnp.stack([v_[rows] for v_ in bur]))
            xi_ref[bi] = _time_major(jnp.stack([v_[rows] for v_ in bui]))
        lr, li = lr_ref[...], li_ref[...]

        def step(j, carry):
            off = pl.multiple_of(j * nt, nt)
            new = []
            for bi in range(b):
                pr, pi = carry[2 * bi], carry[2 * bi + 1]
                nr = lr * pr - li * pi + xr_ref[bi, pl.ds(off, nt), :]
                ni = lr * pi + li * pr + xi_ref[bi, pl.ds(off, nt), :]
                xr_ref[bi, pl.ds(off, nt), :] = nr
                xi_ref[bi, pl.ds(off, nt), :] = ni
                new += [nr, ni]
            return tuple(new)

        init = tuple(v for bi in range(b) for v in (sr[bi], si[bi]))
        fin = lax.fori_loop(0, tc, step, init, unroll=4)
        for bi in range(b):
            sr[bi] = fin[2 * bi]
            si[bi] = fin[2 * bi + 1]
        xr_t = [_tile_major(xr_ref[bi], nt) for bi in range(b)]
        xi_t = [_tile_major(xi_ref[bi], nt) for bi in range(b)]
        for cb in range(ncb):
            cols = slice(cb * LANES, (cb + 1) * LANES)
            acc = d_ref[:, cols] * u_all[:, cols]
            for t in range(cb * tpc, (cb + 1) * tpc):
                xr_all = jnp.concatenate([xr_t[bi][t] for bi in range(b)], axis=0)
                xi_all = jnp.concatenate([xi_t[bi][t] for bi in range(b)], axis=0)
                acc = acc + (_mm(xr_all, cre_ref[t]) - _mm(xi_all, cim_ref[t]))
            for bi in range(b):
                y_ref[bi, :, cols] = acc[bi * tc:(bi + 1) * tc]

    def whole(a):
        return pl.BlockSpec(a.shape, lambda c, n=a.ndim: (0,) * n)

    xblk = pl.BlockSpec((b, tc * nt, LANES), lambda c: (0, c, 0))
    return pl.pallas_call(
        body, grid=(s // tc,),
        in_specs=[pl.BlockSpec((b, tc, sw), lambda c: (0, c, ucol))] + [whole(a) for a in (lam_re, lam_im, bre, bim, cre, cim, dvec)],
        out_specs=[pl.BlockSpec((b, tc, sw), lambda c: (0, c, 0)), xblk, xblk],
        out_shape=[jax.ShapeDtypeStruct((b, s, sw), F32)] + [jax.ShapeDtypeStruct((b, s * nt, LANES), F32)] * 2,
        scratch_shapes=[pltpu.VMEM((b, nt, LANES), F32)] * 2, name="s5_fwd",
        compiler_params=_cparams(("arbitrary",)))(proj3, lam_re, lam_im, bre, bim, cre, cim, dvec)


def _s5_bwd(proj3, ucol, dy3, xr3, xi3, lam_re, lam_im, bre, bim, cre, cim, dvec, sw):
    b, s, _ = proj3.shape
    nt = lam_re.shape[0]
    ncb = sw // LANES
    tpc = nt // ncb
    tc = S5_TC
    nchunk = s // tc

    def body(u_ref, dy_ref, xr_ref, xi_ref, pr_ref, pi_ref, lr_ref, li_ref, bre_ref, bim_ref, cre_ref, cim_ref, d_ref,
             du_ref, dbre, dbim, dcre, dcim, dlr, dli, dd, gr, gi, sr, si):
        step_id = pl.program_id(0)

        @pl.when(step_id == 0)
        def _():
            for r in (sr, si, dbre, dbim, dcre, dcim, dlr, dli, dd):
                r[...] = jnp.zeros_like(r)

        u_all = u_ref[...].reshape(b * tc, sw)
        dy_all = dy_ref[...].reshape(b * tc, sw)
        dxr, dxi = [], []
        for cb in range(ncb):
            dycb = dy_all[:, cb * LANES:(cb + 1) * LANES].astype(MXU_DTYPE)
            for t in range(cb * tpc, (cb + 1) * tpc):
                dxr.append(_mm_nt(dycb, cre_ref[t]))
                dxi.append(-_mm_nt(dycb, cim_ref[t]))
        for bi in range(b):
            rows = slice(bi * tc, (bi + 1) * tc)
            gr[bi] = _time_major(jnp.stack([v_[rows] for v_ in dxr]))
            gi[bi] = _time_major(jnp.stack([v_[rows] for v_ in dxi]))
        lr, li = lr_ref[...], li_ref[...]

        def step(jj, carry):
            off = pl.multiple_of((tc - 1 - jj) * nt, nt)
            new = []
            for bi in range(b):
                nr_, ni_ = carry[2 * bi], carry[2 * bi + 1]
                vr = gr[bi, pl.ds(off, nt), :] + lr * nr_ + li * ni_
                vi = gi[bi, pl.ds(off, nt), :] + lr * ni_ - li * nr_
                gr[bi, pl.ds(off, nt), :] = vr
                gi[bi, pl.ds(off, nt), :] = vi
                new += [vr, vi]
            return tuple(new)

        init = tuple(v for bi in range(b) for v in (sr[bi], si[bi]))
        fin = lax.fori_loop(0, tc, step, init, unroll=4)
        for bi in range(b):
            sr[bi] = fin[2 * bi]
            si[bi] = fin[2 * bi + 1]

        has_prev = (step_id != nchunk - 1).astype(F32)
        rest = (tc - 1) * nt
        alr = jnp.zeros((nt, LANES), F32)
        ali = jnp.zeros((nt, LANES), F32)
        for bi in range(b):
            s0r, s0i = gr[bi, pl.ds(0, nt), :], gi[bi, pl.ds(0, nt), :]
            x0r, x0i = pr_ref[bi] * has_prev, pi_ref[bi] * has_prev
            alr += s0r * x0r + s0i * x0i
            ali += s0i * x0r - s0r * x0i
            s1r, s1i = gr[bi, pl.ds(nt, rest), :], gi[bi, pl.ds(nt, rest), :]
            x1r, x1i = xr_ref[bi, pl.ds(0, rest), :], xi_ref[bi, pl.ds(0, rest), :]
            alr += jnp.sum((s1r * x1r + s1i * x1i).reshape(tc - 1, nt, LANES), axis=0)
            ali += jnp.sum((s1i * x1r - s1r * x1i).reshape(tc - 1, nt, LANES), axis=0)
        dlr[...] += alr
        dli[...] += ali

        tiles = [[_tile_major(ref[bi], nt) for bi in range(b)] for ref in (gr, gi, xr_ref, xi_ref)]

        def stacked(k, t):
            return jnp.concatenate([tiles[k][bi][t] for bi in range(b)], axis=0)

        for cb in range(ncb):
            cols = slice(cb * LANES, (cb + 1) * LANES)
            ucb32, dycb32 = u_all[:, cols], dy_all[:, cols]
            ucb, dycb = ucb32.astype(MXU_DTYPE), dycb32.astype(MXU_DTYPE)
            acc = d_ref[:, cols] * dycb32
            for t in range(cb * tpc, (cb + 1) * tpc):
                vr, vi = stacked(0, t), stacked(1, t)
                acc = acc + (_mm_nt(vr, bre_ref[t]) + _mm_nt(vi, bim_ref[t]))
                dbre[t] += _mm_tn(ucb, vr)
                dbim[t] += _mm_tn(ucb, vi)
                dcre[t] += _mm_tn(stacked(2, t), dycb)
                dcim[t] -= _mm_tn(stacked(3, t), dycb)
            for bi in range(b):
                du_ref[bi, :, cols] = acc[bi * tc:(bi + 1) * tc]
            dd[:, cols] += _colsum(dycb32 * ucb32)

    def whole(a):
        return pl.BlockSpec(a.shape, lambda c, n=len(a.shape): (0,) * n)

    def rev(c):
        return nchunk - 1 - c

    xblk = pl.BlockSpec((b, tc * nt, LANES), lambda c: (0, rev(c), 0))
    prev = pl.BlockSpec((b, nt, LANES), lambda c: (0, jnp.maximum(rev(c) * tc - 1, 0), 0))
    sd = jax.ShapeDtypeStruct
    blk = sd(bre.shape, F32)
    acc_shapes = [blk, blk, sd(cre.shape, F32), sd(cre.shape, F32), sd(lam_re.shape, F32), sd(lam_re.shape, F32), sd((1, sw), F32)]
    return pl.pallas_call(
        body, grid=(nchunk,),
        in_specs=[pl.BlockSpec((b, tc, sw), lambda c: (0, rev(c), ucol)), pl.BlockSpec((b, tc, sw), lambda c: (0, rev(c), 0)),
                  xblk, xblk, prev, prev] + [whole(a) for a in (lam_re, lam_im, bre, bim, cre, cim, dvec)],
        out_specs=[pl.BlockSpec((b, tc, sw), lambda c: (0, rev(c), 0))] + [whole(a) for a in acc_shapes],
        out_shape=[sd((b, s, sw), F32)] + acc_shapes,
        scratch_shapes=[pltpu.VMEM((b, tc * nt, LANES), F32)] * 2 + [pltpu.VMEM((b, nt, LANES), F32)] * 2, name="s5_bwd",
        compiler_params=_cparams(("arbitrary",)))(proj3, dy3, xr3, xi3, xr3, xi3, lam_re, lam_im, bre, bim, cre, cim, dvec)


def _s5_blocks(bb, cc_, ncb):
    g, n, p = bb.shape
    mask = jnp.asarray(_s5_tile_mask(g, p, ncb))
    nt, gpt, gpc = mask.shape
    bbt, cct = bb.reshape(nt, gpt, n, p), cc_.reshape(nt, gpt, n, p)
    spread = mask[:, :, :, None, None]
    bblk = (bbt[:, :, None] * spread).transpose(0, 2, 3, 1, 4).reshape(nt, gpc * n, gpt * p)
    cblk = (cct[:, :, None] * spread).transpose(0, 1, 4, 2, 3).reshape(nt, gpt * p, gpc * n)
    return bblk, cblk


def _s5_tile_mask(g, p, ncb):
    nt, gpt, gpc = g * p // LANES, LANES // p, g // ncb
    mask = np.zeros((nt, gpt, gpc), np.float32)
    for t in range(nt):
        for gl in range(gpt):
            mask[t, gl, (t * gpt + gl) % gpc] = 1.0
    return mask


def _s5_unblock(dbblk, dcblk, g, n, p, ncb):
    mask = jnp.asarray(_s5_tile_mask(g, p, ncb))
    nt, gpt, gpc = mask.shape
    db = jnp.sum(dbblk.reshape(nt, gpc, n, gpt, p) * mask.transpose(0, 2, 1)[:, :, None, :, None], axis=1)
    dc = jnp.sum(dcblk.reshape(nt, gpt, p, gpc, n) * mask[:, :, None, :, None], axis=3)
    return db.transpose(0, 2, 1, 3).reshape(g, n, p), dc.transpose(0, 1, 3, 2).reshape(g, n, p)


def _adamw_math(w, g, m, v):
    m = ADAM_B1 * m + (1.0 - ADAM_B1) * g
    v = ADAM_B2 * v + (1.0 - ADAM_B2) * (g * g)
    m_hat = m / (1.0 - ADAM_B1 ** ADAM_STEP)
    v_hat = v / (1.0 - ADAM_B2 ** ADAM_STEP)
    delta = -ADAM_LR * (m_hat / (jnp.sqrt(v_hat) + ADAM_EPS) + ADAM_WD * w)
    return delta, m, v


def _adamw_layer(name, parts, w, m, v, layer, prev, tr):
    nparts, r, c = parts.shape
    assert r % tr == 0
    if prev is None:
        prev = [lax.empty(w.shape, F32) for _ in range(4)]

    def body(p_ref, w_ref, m_ref, v_ref, *rest):
        g_out, d_out, m_out, v_out = rest[4:]
        g = p_ref[0].astype(F32)
        for k in range(1, nparts):
            g = g + p_ref[k].astype(F32)
        delta, mn, vn = _adamw_math(w_ref[...], g, m_ref[...], v_ref[...])
        g_out[...] = g
        d_out[...] = delta
        m_out[...] = mn
        v_out[...] = vn

    blk = pl.BlockSpec((None, tr, c), lambda i: (layer, i, 0))
    kept = pl.BlockSpec(memory_space=pl.ANY)
    return pl.pallas_call(
        body, grid=(r // tr,), in_specs=[pl.BlockSpec((nparts, tr, c), lambda i: (0, i, 0)), blk, blk, blk] + [kept] * 4,
        out_specs=[blk] * 4, out_shape=[jax.ShapeDtypeStruct(w.shape, F32)] * 4, name=name,
        input_output_aliases={4 + k: k for k in range(4)},
        compiler_params=_cparams(("arbitrary",)))(parts, w, m, v, *prev)


def _pack_small(pieces):
    flat = jnp.concatenate([p.reshape(-1) for p in pieces])
    n = flat.shape[0]
    unit = SUBLANES * LANES
    padded = -(-n // unit) * unit
    return jnp.pad(flat, (0, padded - n)).reshape(padded // LANES, LANES)


def _pack_small_layers(pieces):
    nl = pieces[0].shape[0]
    flat = jnp.concatenate([p.reshape(nl, -1) for p in pieces], axis=1)
    n = flat.shape[1]
    unit = SUBLANES * LANES
    padded = -(-n // unit) * unit
    return jnp.pad(flat, ((0, 0), (0, padded - n))).reshape(nl, padded // LANES, LANES)


def _step(x, positions, weights, moments_m, moments_v, loss_target, distributed):
    f32 = F32
    bsz, seq, dm = x.shape
    depth = weights["w_in"].shape[0]
    aw = weights["attn_gain"].shape[1]
    sw = weights["ssm_gain"].shape[1]
    dff = weights["b_ff1"].shape[1]
    ng, npst = weights["ssm_a_re"].shape[1:]
    gdim = weights["ssm_d"].shape[2]
    in_w = 3 * aw + sw
    t_rows = bsz * seq
    alpha = (2.0 * depth) ** 0.25
    ncb = sw // LANES
    nt = ng * npst // LANES
    ndev = NDEV if distributed else 1
    tm = tl = 512

    gather_modes = ["gather"] * len(BIG_NAMES)
    no_token = jnp.zeros((SUBLANES, LANES), f32)

    def shards(l, names):
        return [weights[n][l].astype(MXU_DTYPE) for n in names]

    if distributed:
        first, _ = _exchange_start("weights_start_l0_in", shards(0, BIG_NAMES[:1]), gather_modes[:1])
        (g_in0,) = _exchange_wait("weights_wait_l0_in", first, gather_modes[:1], positions)
        rest0, tok_rest0 = _exchange_start("weights_start_l0_rest", shards(0, BIG_NAMES[1:]), gather_modes[1:], after=g_in0)

    half = HEAD_DIM // 2
    inv_freq = ROPE_THETA ** (-jnp.arange(half, dtype=f32) * 2.0 / HEAD_DIM)
    reps = LANES // half
    inv_row = jnp.tile(inv_freq, reps)[None, :]
    sign_row = jnp.tile(jnp.concatenate([-jnp.ones((half,), f32), jnp.ones((half,), f32)]), LANES // HEAD_DIM)[None, :]
    ctab, stab = _rope_tables(positions[..., None], inv_row, sign_row)

    def row(v):
        return v.reshape(1, -1)

    h = x.reshape(t_rows, dm)
    saved = []
    for l in range(depth):
        tok_in = tok_mix = no_token
        if not distributed:
            g_in, g_glu, g_out, g_ff1, g_ff2 = [weights[n][l].astype(MXU_DTYPE)[None] for n in BIG_NAMES]
        elif l == 0:
            g_in, tok_in = g_in0, tok_rest0
        else:
            g_in, g_glu, g_out, g_ff1, g_ff2 = _exchange_wait(f"weights_wait_l{l}", next_gather, gather_modes, h)
            if l + 1 < depth:
                next_gather, tok_in = _exchange_start(f"weights_start_l{l + 1}", shards(l + 1, BIG_NAMES), gather_modes, after=g_in)
        w_in_l = g_in

        def in_proj(rv, cr):
            return [jnp.concatenate([_mm(rv[0], cr[0][j]) for j in range(ndev)], axis=-1)], []

        (proj,), _ = _rows_call("in_proj", in_proj, [h], [w_in_l, tok_in], [(in_w, f32)], [], tl)
        proj3 = proj.reshape(bsz, seq, in_w)
        attn3, lse3 = _attn_fwd(proj3, ctab, stab, aw)
        if distributed and l == 0:
            g_glu, g_out, g_ff1, g_ff2 = _exchange_wait("weights_wait_l0_rest", rest0, gather_modes[1:], attn3)
            if depth > 1:
                next_gather, tok_mix = _exchange_start("weights_start_l1", shards(1, BIG_NAMES), gather_modes, after=g_glu)
        w_glu_l = g_glu.reshape(sw, sw)
        w_out_l = g_out.reshape(dm, dm)
        w_ff1_l = g_ff1
        w_ff2_l = g_ff2.reshape(dff, dm)

        a_re, a_im = weights["ssm_a_re"][l][:, None, :], weights["ssm_a_im"][l][:, None, :]
        log_dt = weights["ssm_log_dt"][l][:, None, None]
        bt_re = weights["ssm_b_re"][l].transpose(0, 2, 1)
        bt_im = weights["ssm_b_im"][l].transpose(0, 2, 1)
        lb_re, lb_im, bb_re, bb_im = _s5_params(a_re, a_im, log_dt, bt_re, bt_im)
        lam_re, lam_im = lb_re.reshape(nt, LANES), lb_im.reshape(nt, LANES)
        bre, cre = _s5_blocks(bb_re, weights["ssm_c_re"][l], ncb)
        bim, cim = _s5_blocks(bb_im, weights["ssm_c_im"][l], ncb)
        s5c = [a_.astype(MXU_DTYPE) for a_ in (bre, bim, cre, cim)]
        dvec = row(weights["ssm_d"][l])
        ypre3, xr3, xi3 = _s5_fwd(proj3, 3 * aw // sw, lam_re, lam_im, *s5c, dvec, sw)
        attn, ypre = attn3.reshape(t_rows, aw), ypre3.reshape(t_rows, sw)

        b_glu, ga, gs = row(weights["b_glu"][l]), row(weights["attn_gain"][l]), row(weights["ssm_gain"][l])

        def mix(rv, cr):
            at, yp = rv
            g = _gelu(yp)
            ssm = g * jax.nn.sigmoid(_mm(g, cr[0][...]) + cr[1][...])
            return [jnp.concatenate([_rms_norm(at, cr[2][...]), _rms_norm(ssm, cr[3][...])], axis=-1)], []

        (mixed,), _ = _rows_call("mix", mix, [attn, ypre], [w_glu_l, b_glu, ga, gs, tok_mix], [(dm, MXU_DTYPE)], [], tl)

        b_out, g1, b1 = row(weights["b_out"][l]), row(weights["ln1_g"][l]), row(weights["ln1_b"][l])

        def out_proj(rv, cr):
            pre = alpha * rv[1] + _mm(rv[0], cr[0][...]) + cr[1][...]
            return [pre, _layer_norm(pre, cr[2][...], cr[3][...])], []

        (pre1, h1), _ = _rows_call("out_proj", out_proj, [mixed, h], [w_out_l, b_out, g1, b1], [(dm, f32), (dm, f32)], [], tl)

        b_ff1 = row(weights["b_ff1"][l])

        def ff1(rv, cr):
            pre_act = jnp.concatenate([_mm(rv[0], cr[0][j]) for j in range(ndev)], axis=-1) + cr[1][...]
            return [jnp.square(jnp.maximum(pre_act, 0.0))], []

        (act,), _ = _rows_call("ff1", ff1, [h1], [w_ff1_l, b_ff1], [(dff, MXU_DTYPE)], [], tm)

        b_ff2, g2, b2 = row(weights["b_ff2"][l]), row(weights["ln2_g"][l]), row(weights["ln2_b"][l])

        def ff2(rv, cr):
            pre = alpha * rv[1] + _mm(rv[0], cr[0][...]) + cr[1][...]
            return [pre, _layer_norm(pre, cr[2][...], cr[3][...])], []

        (pre2, h2), _ = _rows_call("ff2", ff2, [act, h1], [w_ff2_l, b_ff2, g2, b2], [(dm, f32), (dm, f32)], [], tm)

        saved.append(dict(h=h, proj3=proj3, attn=attn, lse3=lse3, attn3=attn3, ypre=ypre, xr3=xr3, xi3=xi3, mixed=mixed, pre1=pre1,
                          h1=h1, act=act, pre2=pre2, w_in=w_in_l, w_glu=w_glu_l, w_out=w_out_l, w_ff1=w_ff1_l, w_ff2=w_ff2_l,
                          lam=(lam_re, lam_im), s5c=s5c, dvec=dvec, b_glu=b_glu, ga=ga, gs=gs, g1=g1, g2=g2,
                          s5in=(a_re, a_im, log_dt, bt_re, bt_im)))
        h = h2

    g2_last = saved[-1]["g2"]

    def loss_fn(rv, cr):
        y, tgt, pre = rv
        err = y - tgt
        part = 0.5 * jnp.sum(jnp.mean(err * err, axis=-1, keepdims=True), axis=0, keepdims=True)
        dpre, dg, db = _layer_norm_bwd(err * (1.0 / dm), pre, cr[0][...])
        return [dpre], [jnp.broadcast_to(part, (1, LANES)), dg, db, _colsum(dpre)]

    (dpre2,), (loss_acc, dg2, db2, dbff2) = _rows_call(
        "loss", loss_fn, [h, loss_target.reshape(t_rows, dm), saved[-1]["pre2"]], [g2_last], [(dm, f32)],
        [(1, LANES), (1, dm), (1, dm), (1, dm)], tl)
    loss = loss_acc[0, 0]
    if distributed:
        loss = lax.psum(loss, MESH_AXES)

    big_parts = {n: [None] * depth for n in BIG_NAMES}
    small_parts = [None] * depth
    grad_handles = [None] * depth
    grad_modes = ["scatter"] * len(BIG_NAMES) + ["gather"]
    grad_x = None
    for l in reversed(range(depth)):
        sv = saved[l]

        def ff2_bwd(rv, cr):
            r = rv[1].astype(F32)
            relu2 = jnp.where(r > 0.0, (2.0 * r) * lax.rsqrt(r), 0.0)
            da = _mm_nt(rv[0], cr[0][...]) * relu2
            return [da], [_colsum(da)]

        (dact,), (dbff1,) = _rows_call("ff2_bwd", ff2_bwd, [dpre2, sv["act"]], [sv["w_ff2"]], [(dff, MXU_DTYPE)], [(1, dff)], tm)
        big_parts["w_ff2"][l] = _wgrad_call("wgrad_ff2", sv["act"], dpre2, "x", ndev, ndev // 2 if ndev > 1 else 1,
                                            (dff // ndev, dm), tm)
        big_parts["w_ff1"][l] = _wgrad_call("wgrad_ff1", sv["h1"], dact, "y", ndev, ndev // 2 if ndev > 1 else 1,
                                            (dm, dff // ndev), tm)

        def ff1_bwd(rv, cr):
            dacc = alpha * rv[1]
            wpb = dff // ndev
            for j in range(ndev):
                dacc = dacc + _mm_nt(rv[0][:, j * wpb:(j + 1) * wpb], cr[0][j])
            dpre, dg, db = _layer_norm_bwd(dacc, rv[2], cr[1][...])
            return [dpre], [dg, db, _colsum(dpre)]

        (dpre1,), (dg1, db1, dbout) = _rows_call("ff1_bwd", ff1_bwd, [dact, dpre2, sv["pre1"]], [sv["w_ff1"], sv["g1"]],
                                                 [(dm, f32)], [(1, dm)] * 3, tm)
        big_parts["w_out"][l] = _wgrad_call("wgrad_out", sv["mixed"], dpre1, "x", ndev, ndev, (dm // ndev, dm), tl)

        def mix_bwd(rv, cr):
            dp, at, yp = rv
            w_out_r, w_glu_r, bg, ga_, gs_ = cr
            dmixed = _mm_nt(dp, w_out_r[...])
            g = _gelu(yp)
            sig = jax.nn.sigmoid(_mm(g, w_glu_r[...]) + bg[...])
            ssm = g * sig
            dat, dga = _rms_norm_bwd(dmixed[:, :aw], at, ga_[...])
            dssm, dgs = _rms_norm_bwd(dmixed[:, aw:], ssm, gs_[...])
            dz = dssm * g * sig * (1.0 - sig)
            dg = dssm * sig + _mm_nt(dz, w_glu_r[...])
            return [dat, dg * _gelu_grad(yp), dz], [dga, dgs, _colsum(dz)]

        (dattn, dypre, dz), (dga, dgs, dbglu) = _rows_call(
            "mix_bwd", mix_bwd, [dpre1, sv["attn"], sv["ypre"]], [sv["w_out"], sv["w_glu"], sv["b_glu"], sv["ga"], sv["gs"]],
            [(aw, f32), (sw, f32), (sw, MXU_DTYPE)], [(1, aw), (1, sw), (1, sw)], tl)
        big_parts["w_glu"][l] = _wgrad_call("wgrad_glu", sv["ypre"], dz, "x", 1, 1, (sw, sw), tl, prologue=_gelu).reshape(
            ndev, sw // ndev, sw)

        du3, dbre, dbim, dcre, dcim, dlr, dli, dd = _s5_bwd(
            sv["proj3"], 3 * aw // sw, dypre.reshape(bsz, seq, sw), sv["xr3"], sv["xi3"], *sv["lam"], *sv["s5c"], sv["dvec"], sw)
        dbb_re, dc_re = _s5_unblock(dbre, dcre, ng, gdim, npst, ncb)
        dbb_im, dc_im = _s5_unblock(dbim, dcim, ng, gdim, npst, ncb)
        da_re, da_im, dldt, dbt_re, dbt_im = _s5_params_bwd(
            *sv["s5in"], (dlr.reshape(ng, 1, npst), dli.reshape(ng, 1, npst), dbb_re, dbb_im))

        dq3, dk3, dv3 = _attn_bwd(sv["proj3"], ctab, stab, dattn.reshape(bsz, seq, aw), sv["attn3"], sv["lse3"], aw)
        dproj = [dq3.reshape(t_rows, aw), dk3.reshape(t_rows, aw), dv3.reshape(t_rows, aw), du3.reshape(t_rows, sw)]
        big_parts["w_in"][l] = _wgrad_call("wgrad_in", sv["h"], dproj, "y", ndev, ndev, (dm, in_w // ndev), tl)

        small_parts[l] = dict(
            attn_gain=dga, ssm_gain=dgs, ssm_a_re=da_re[:, 0], ssm_a_im=da_im[:, 0], ssm_log_dt=dldt[:, 0, 0], ssm_b_re=dbt_re.transpose(0, 2, 1),
            ssm_b_im=dbt_im.transpose(0, 2, 1), ssm_c_re=dc_re, ssm_c_im=dc_im, ssm_d=dd, b_glu=dbglu, b_out=dbout, ln1_g=dg1,
            ln1_b=db1, b_ff1=dbff1, b_ff2=dbff2, ln2_g=dg2, ln2_b=db2)

        layer_grads = [big_parts[n][l] for n in BIG_NAMES] + [_pack_small([small_parts[l][n] for n in SMALL_NAMES])]
        if distributed:
            grad_handles[l], token = _exchange_start(f"grads_start_l{l}", layer_grads, grad_modes)
        else:
            grad_handles[l], token = layer_grads, jnp.zeros((SUBLANES, LANES), f32)

        wpb = in_w // ndev
        if l > 0:
            prev = saved[l - 1]

            def in_bwd(rv, cr):
                dacc = alpha * rv[4]
                dp = jnp.concatenate([v_.astype(MXU_DTYPE) for v_ in rv[:4]], axis=-1)
                for j in range(ndev):
                    dacc = dacc + _mm_nt(dp[:, j * wpb:(j + 1) * wpb], cr[0][j])
                dpre, dg, db = _layer_norm_bwd(dacc, rv[5], cr[1][...])
                return [dpre], [dg, db, _colsum(dpre)]

            (dpre2,), (dg2, db2, dbff2) = _rows_call("in_bwd", in_bwd, dproj + [dpre1, prev["pre2"]],
                                                     [sv["w_in"], prev["g2"], token], [(dm, f32)], [(1, dm)] * 3, tl)
        else:
            def in_bwd0(rv, cr):
                dacc = alpha * rv[4]
                dp = jnp.concatenate([v_.astype(MXU_DTYPE) for v_ in rv[:4]], axis=-1)
                for j in range(ndev):
                    dacc = dacc + _mm_nt(dp[:, j * wpb:(j + 1) * wpb], cr[0][j])
                return [dacc], []

            (grad_x,), _ = _rows_call("in_bwd0", in_bwd0, dproj + [dpre1], [sv["w_in"], token], [(dm, f32)], [], tl)

    small_shapes = [weights[n].shape[1:] for n in SMALL_NAMES]
    outs = {n: None for n in BIG_NAMES}
    packed = [_pack_small_layers([d[n] for n in SMALL_NAMES]) for d in (weights, moments_m, moments_v)]
    small_out = None
    after = grad_x
    for l in reversed(range(depth)):
        if distributed:
            recv = _exchange_wait(f"grads_wait_l{l}", grad_handles[l], grad_modes, after)
        else:
            recv = [g_[None] if i == len(BIG_NAMES) else g_ for i, g_ in enumerate(grad_handles[l])]
        for n, parts in zip(BIG_NAMES, recv[:-1], strict=True):
            outs[n] = _adamw_layer("adamw_" + n, parts, weights[n], moments_m[n], moments_v[n], l, outs[n],
                                   min(parts.shape[1], 256))
        small_out = _adamw_layer("adamw_small", recv[-1], *packed, l, small_out, recv[-1].shape[1])
        after = small_out[0]
    for k in range(4):
        flat = small_out[k].reshape(depth, -1)
        off = 0
        for n, shp in zip(SMALL_NAMES, small_shapes, strict=True):
            sz = int(np.prod(shp))
            outs.setdefault(n, [None] * 4)
            outs[n][k] = flat[:, off:off + sz].reshape((depth,) + tuple(shp))
            off += sz

    result = [loss, grad_x.reshape(bsz, seq, dm)]
    for k in range(4):
        result += [outs[n][k] for n in WEIGHT_ORDER]
    return tuple(result)


def kernel(x, positions, w_in, attn_gain, ssm_gain, ssm_a_re, ssm_a_im, ssm_log_dt, ssm_b_re, ssm_b_im, ssm_c_re, ssm_c_im, ssm_d, w_glu, b_glu, w_out, b_out, ln1_g, ln1_b, w_ff1, b_ff1, w_ff2, b_ff2, ln2_g, ln2_b, loss_target, m_w_in, m_attn_gain, m_ssm_gain, m_ssm_a_re, m_ssm_a_im, m_ssm_log_dt, m_ssm_b_re, m_ssm_b_im, m_ssm_c_re, m_ssm_c_im, m_ssm_d, m_w_glu, m_b_glu, m_w_out, m_b_out, m_ln1_g, m_ln1_b, m_w_ff1, m_b_ff1, m_w_ff2, m_b_ff2, m_ln2_g, m_ln2_b, v_w_in, v_attn_gain, v_ssm_gain, v_ssm_a_re, v_ssm_a_im, v_ssm_log_dt, v_ssm_b_re, v_ssm_b_im, v_ssm_c_re, v_ssm_c_im, v_ssm_d, v_w_glu, v_b_glu, v_w_out, v_b_out, v_ln1_g, v_ln1_b, v_w_ff1, v_b_ff1, v_w_ff2, v_b_ff2, v_ln2_g, v_ln2_b):
    loc = locals()
    weights = {n: loc[n] for n in WEIGHT_ORDER}
    moments_m = {n: loc["m_" + n] for n in WEIGHT_ORDER}
    moments_v = {n: loc["v_" + n] for n in WEIGHT_ORDER}
    return _step(x, positions, weights, moments_m, moments_v, loss_target, distributed=True)
```

```python
import functools
import math

import jax
import jax.numpy as jnp
import numpy as np
from jax import lax
from jax.experimental import pallas as pl
from jax.experimental.pallas import tpu as pltpu

F32 = jnp.float32
MXU_DTYPE = jnp.bfloat16

HEAD_DIM = 64
DILATION_PAIRS = ((128, 1), (512, 4), (2048, 16))
ROPE_THETA = 10000.0
SSM_GROUP_DIM = 16
SSM_STATE = 64
LN_EPS = 1e-5
RMS_EPS = 1e-6
NEG_INF = -1e30
ADAM_LR, ADAM_B1, ADAM_B2, ADAM_EPS, ADAM_WD, ADAM_STEP = 0.001, 0.9, 0.999, 1e-08, 0.01, 10

LANES = 128
SUBLANES = 8
QBLK = 128
VMEM_LIMIT = 56 * 2**20
MESH_AXES = ("x", "y", "c")
NDEV = 8

SMALL_NAMES = ("attn_gain", "ssm_gain", "ssm_a_re", "ssm_a_im", "ssm_log_dt", "ssm_b_re", "ssm_b_im", "ssm_c_re",
               "ssm_c_im", "ssm_d", "b_glu", "b_out", "ln1_g", "ln1_b", "b_ff1", "b_ff2", "ln2_g", "ln2_b")
BIG_NAMES = ("w_in", "w_glu", "w_out", "w_ff1", "w_ff2")
WEIGHT_ORDER = ("w_in", "attn_gain", "ssm_gain", "ssm_a_re", "ssm_a_im", "ssm_log_dt", "ssm_b_re", "ssm_b_im", "ssm_c_re",
                "ssm_c_im", "ssm_d", "w_glu", "b_glu", "w_out", "b_out", "ln1_g", "ln1_b", "w_ff1", "b_ff1", "w_ff2",
                "b_ff2", "ln2_g", "ln2_b")


def _cparams(sem=None):
    return pltpu.CompilerParams(dimension_semantics=sem, vmem_limit_bytes=VMEM_LIMIT)


def _mm(a, b):
    return jnp.dot(a.astype(MXU_DTYPE), b.astype(MXU_DTYPE), preferred_element_type=F32)


def _mm_nt(a, b):
    return lax.dot_general(a.astype(MXU_DTYPE), b.astype(MXU_DTYPE), (((1,), (1,)), ((), ())), preferred_element_type=F32)


def _mm_tn(a, b):
    return lax.dot_general(a.astype(MXU_DTYPE), b.astype(MXU_DTYPE), (((0,), (0,)), ((), ())), preferred_element_type=F32)


def _colsum(x):
    return jnp.sum(x, axis=0, keepdims=True)


def _layer_norm(x, g, b):
    mu = jnp.mean(x, axis=-1, keepdims=True)
    xc = x - mu
    var = jnp.mean(xc * xc, axis=-1, keepdims=True)
    return xc * lax.rsqrt(var + LN_EPS) * g + b


def _layer_norm_bwd(dy, pre, g):
    mu = jnp.mean(pre, axis=-1, keepdims=True)
    xc = pre - mu
    var = jnp.mean(xc * xc, axis=-1, keepdims=True)
    r = lax.rsqrt(var + LN_EPS)
    xhat = xc * r
    dyg = dy * g
    dpre = r * (dyg - jnp.mean(dyg, axis=-1, keepdims=True) - xhat * jnp.mean(dyg * xhat, axis=-1, keepdims=True))
    return dpre, _colsum(dy * xhat), _colsum(dy)


def _rms_norm(x, g):
    return x * lax.rsqrt(jnp.mean(x * x, axis=-1, keepdims=True) + RMS_EPS) * g


def _rms_norm_bwd(dy, x, g):
    r = lax.rsqrt(jnp.mean(x * x, axis=-1, keepdims=True) + RMS_EPS)
    dyg = dy * g
    dx = dyg * r - x * (r * r * r) * jnp.mean(dyg * x, axis=-1, keepdims=True)
    return dx, _colsum(dy * x * r)


_GELU_C = math.sqrt(2.0 / math.pi)


def _gelu(x):
    return 0.5 * x * (1.0 + jnp.tanh(_GELU_C * (x + 0.044715 * (x * x * x))))


def _gelu_grad(x):
    t = jnp.tanh(_GELU_C * (x + 0.044715 * (x * x * x)))
    return 0.5 * (1.0 + t) + 0.5 * x * (1.0 - t * t) * (_GELU_C * (1.0 + 3.0 * 0.044715 * x * x))


def _rows_call(name, fn, rows, consts, out_rows, out_accs, tm):
    rows = [r if isinstance(r, tuple) else (r, r.shape[1], 0) for r in rows]
    m = rows[0][0].shape[0]
    assert m % tm == 0
    nr, nc, no, na = len(rows), len(consts), len(out_rows), len(out_accs)

    def body(*refs):
        rr, cr = refs[:nr], refs[nr:nr + nc]
        orr, ar = refs[nr + nc:nr + nc + no], refs[nr + nc + no:]
        outs, accs = fn([r[...] for r in rr], cr)
        for o, v in zip(orr, outs, strict=True):
            o[...] = v.astype(o.dtype)
        if na:
            first = pl.program_id(0) == 0

            @pl.when(first)
            def _():
                for a, v in zip(ar, accs, strict=True):
                    a[...] = v

            @pl.when(jnp.logical_not(first))
            def _():
                for a, v in zip(ar, accs, strict=True):
                    a[...] += v

    def whole(shape):
        return pl.BlockSpec(shape, lambda i, n=len(shape): (0,) * n)

    in_specs = [pl.BlockSpec((tm, w), lambda i, cb=cb: (i, cb)) for _, w, cb in rows] + [whole(c.shape) for c in consts]
    out_specs = [pl.BlockSpec((tm, w), lambda i: (i, 0)) for w, _ in out_rows] + [whole(s) for s in out_accs]
    out_shape = [jax.ShapeDtypeStruct((m, w), dt) for w, dt in out_rows] + [jax.ShapeDtypeStruct(s, F32) for s in out_accs]
    res = pl.pallas_call(body, grid=(m // tm,), in_specs=in_specs, out_specs=out_specs, out_shape=out_shape, name=name,
                         compiler_params=_cparams(("arbitrary",)))(*[r[0] for r in rows], *consts)
    return res[:no], res[no:]


def _wgrad_call(name, x, dy, split, nblk, jb, blk_shape, tm, prologue=None):
    dys = list(dy) if isinstance(dy, (list, tuple)) else [dy]
    m = x.shape[0]
    kk, nn = blk_shape
    assert m % tm == 0 and nblk % jb == 0 and (len(dys) == 1 or (split == "y" and jb == nblk))
    xw = kk * jb if split == "x" else x.shape[1]
    yws = [d.shape[1] for d in dys] if (split == "x" or len(dys) > 1) else [nn * jb]
    nrow = m // tm

    def body(x_ref, *rest):
        dy_refs, o_ref, acc_ref = rest[:len(dys)], rest[-2], rest[-1]
        i = pl.program_id(1)

        @pl.when(i == 0)
        def _():
            acc_ref[...] = jnp.zeros_like(acc_ref)

        xv = x_ref[...]
        if prologue is not None:
            xv = prologue(xv)
        xv = xv.astype(MXU_DTYPE)
        dv = [r[...].astype(MXU_DTYPE) for r in dy_refs]
        dv = dv[0] if len(dv) == 1 else jnp.concatenate(dv, axis=-1)
        for j in range(jb):
            xa = xv[:, j * kk:(j + 1) * kk] if split == "x" else xv
            da = dv[:, j * nn:(j + 1) * nn] if split == "y" else dv
            acc_ref[j] += _mm_tn(xa, da)

        @pl.when(i == nrow - 1)
        def _():
            o_ref[...] = acc_ref[...].astype(o_ref.dtype)

    in_specs = [pl.BlockSpec((tm, xw), (lambda j, i: (i, j)) if split == "x" else (lambda j, i: (i, 0)))]
    in_specs += [pl.BlockSpec((tm, yw), (lambda j, i: (i, j)) if (split == "y" and len(dys) == 1) else (lambda j, i: (i, 0)))
                 for yw in yws]
    return pl.pallas_call(
        body, grid=(nblk // jb, nrow), in_specs=in_specs,
        out_specs=pl.BlockSpec((jb, kk, nn), lambda j, i: (j, 0, 0)),
        out_shape=jax.ShapeDtypeStruct((nblk, kk, nn), MXU_DTYPE),
        scratch_shapes=[pltpu.VMEM((jb, kk, nn), F32)], name=name,
        compiler_params=_cparams(("arbitrary", "arbitrary")))(x, *dys)


_HBM_SPEC = pl.BlockSpec(memory_space=pltpu.HBM)
_SEM_SPEC = pl.BlockSpec(memory_space=pltpu.SEMAPHORE)
_EFFECT = pltpu.SideEffectType.DATAFLOW_SIDE_EFFECTING


def _my_index():
    return 4 * lax.axis_index("x") + 2 * lax.axis_index("y") + lax.axis_index("c")


def _peer_copies(ins, lands, send_sems, recv_sems, modes):
    x, y, c = lax.axis_index("x"), lax.axis_index("y"), lax.axis_index("c")
    me = 4 * x + 2 * y + c
    pairs = []
    for k in range(NDEV - 1):
        fx, fy, fc = ((k + 1) >> 2) & 1, ((k + 1) >> 1) & 1, (k + 1) & 1
        px, py, pc = (x + fx) % 2, (y + fy) % 2, (c + fc) % 2
        idx = 4 * px + 2 * py + pc
        for a, md in enumerate(modes):
            src = ins[a] if md == "gather" else ins[a].at[idx]
            sem = a * (NDEV - 1) + k
            common = dict(src_ref=src, send_sem=send_sems.at[sem], recv_sem=recv_sems.at[sem], device_id=(px, py, pc),
                          device_id_type=pl.DeviceIdType.MESH)
            pairs.append((pltpu.make_async_remote_copy(dst_ref=lands[a].at[me], **common),
                          pltpu.make_async_remote_copy(dst_ref=lands[a].at[idx], **common)))
    return pairs


def _exchange_start(name, arrays, modes, after=None):
    n = len(arrays)
    extra = [] if after is None else [after]
    me = _my_index()
    lands = []
    for a, md in zip(arrays, modes, strict=True):
        piece = a if md == "gather" else lax.dynamic_index_in_dim(a, me, 0, keepdims=False)
        lands.append(lax.dynamic_update_index_in_dim(lax.empty((NDEV,) + piece.shape, a.dtype), piece, me, 0))

    def body(*refs):
        ins, lnd = refs[:n], refs[n:2 * n]
        send_sems, recv_sems = refs[2 * n + len(extra)], refs[2 * n + len(extra) + 1]
        token = refs[-1]
        for out_copy, _ in _peer_copies(ins, lnd, send_sems, recv_sems, modes):
            out_copy.start()
        token[...] = jnp.zeros_like(token)

    sems = pltpu.SemaphoreType.DMA((n * (NDEV - 1),))
    thru = [pltpu.HBM(a.shape, a.dtype) for a in list(arrays) + lands]
    res = pl.pallas_call(
        body, name=name, out_shape=(sems, sems, *thru, jax.ShapeDtypeStruct((SUBLANES, LANES), F32)),
        in_specs=[_HBM_SPEC] * (2 * n) + [pl.BlockSpec(memory_space=pl.ANY)] * len(extra),
        out_specs=(_SEM_SPEC, _SEM_SPEC, *([_HBM_SPEC] * (2 * n)), pl.BlockSpec(memory_space=pltpu.VMEM)),
        input_output_aliases={i: 2 + i for i in range(2 * n)},
        compiler_params=pltpu.CompilerParams(has_side_effects=_EFFECT),
    )(*[pltpu.with_memory_space_constraint(a, pltpu.HBM) for a in list(arrays) + lands], *extra)
    return (res[0], res[1], res[2:2 + n], res[2 + n:2 + 2 * n]), res[-1]


def _exchange_wait(name, handle, modes, after):
    send_sems, recv_sems, ins_thru, lands_thru = handle
    n = len(ins_thru)

    def body(*refs):
        ins, lnd = refs[:n], refs[n:2 * n]
        for out_copy, arrival in _peer_copies(ins, lnd, refs[2 * n], refs[2 * n + 1], modes):
            out_copy.wait_send()
            arrival.wait_recv()

    thru = [pltpu.HBM(a.shape, a.dtype) for a in list(ins_thru) + list(lands_thru)]
    res = pl.pallas_call(
        body, name=name, out_shape=tuple(thru),
        in_specs=[_HBM_SPEC] * (2 * n) + [_SEM_SPEC, _SEM_SPEC, pl.BlockSpec(memory_space=pl.ANY)],
        out_specs=tuple([_HBM_SPEC] * (2 * n)), input_output_aliases={i: i for i in range(2 * n)},
        compiler_params=pltpu.CompilerParams(has_side_effects=_EFFECT),
    )(*ins_thru, *lands_thru, send_sems, recv_sems, after)
    return res[n:]


def _rope_tables(positions3, inv_freq_row, sign_row):
    b, s, _ = positions3.shape

    def body(pos_ref, f_ref, sg_ref, c_ref, s_ref):
        ang = pos_ref[...].astype(F32) * f_ref[...]
        c_ref[...] = jnp.cos(ang)
        s_ref[...] = jnp.sin(ang) * sg_ref[...]

    row = pl.BlockSpec((1, LANES), lambda i: (0, 0))
    blk = pl.BlockSpec((None, s, LANES), lambda i: (i, 0, 0))
    return pl.pallas_call(
        body, grid=(b,), in_specs=[pl.BlockSpec((None, s, 1), lambda i: (i, 0, 0)), row, row], out_specs=[blk, blk],
        out_shape=[jax.ShapeDtypeStruct((b, s, LANES), F32)] * 2, name="rope_tables",
        compiler_params=_cparams(("arbitrary",)))(positions3, inv_freq_row, sign_row)


def _swap_halves(t):
    lane = lax.broadcasted_iota(jnp.int32, t.shape, 1)
    half = HEAD_DIM // 2
    return jnp.where((lane % HEAD_DIM) < half, pltpu.roll(t, LANES - half, 1), pltpu.roll(t, half, 1))


def _segment_rows(r, d, s):
    n = s // d
    return (pl.ds(r, n, stride=d) if d > 1 else pl.ds(0, s)), pl.ds(r * n, n)


def _head_lanes(shape):
    lane = lax.broadcasted_iota(jnp.int32, shape, len(shape) - 1)
    return lane < HEAD_DIM


def _bmm_nt(a, b):
    return lax.dot_general(a.astype(MXU_DTYPE), b.astype(MXU_DTYPE), (((2,), (2,)), ((0,), (0,))), preferred_element_type=F32)


def _bmm(a, b):
    return lax.dot_general(a.astype(MXU_DTYPE), b.astype(MXU_DTYPE), (((2,), (1,)), ((0,), (0,))), preferred_element_type=F32)


def _bmm_tn(a, b):
    return lax.dot_general(a.astype(MXU_DTYPE), b.astype(MXU_DTYPE), (((1,), (1,)), ((0,), (0,))), preferred_element_type=F32)


def _stack_heads(t3):
    head_a = _head_lanes(t3.shape)
    zero = jnp.zeros_like(t3)
    return jnp.concatenate([jnp.where(head_a, t3, zero), jnp.where(head_a, zero, t3)], axis=1)


QUNIT = QBLK // 2


def _with_previous(t3, nprev):
    shifted = [jnp.concatenate([t3[:k], t3[:-k]], axis=0) for k in range(nprev, 0, -1)]
    return jnp.concatenate(shifted + [t3], axis=1)


def _head_columns(t3):
    return jnp.concatenate([t3[:, :, lo:lo + 1] for lo in range(0, LANES, HEAD_DIM)], axis=1)


def _to_own_unit(t, nprev):
    unit = t.shape[1] // (nprev + 1)
    out = t[:, nprev * unit:]
    for k in range(1, nprev + 1):
        part = t[:, (nprev - k) * unit:(nprev - k + 1) * unit]
        out = out + jnp.concatenate([part[k:], jnp.zeros_like(part[:k])], axis=0)
    return out


def _branch_operands(qh, kh, vh, s, nb):
    nh = LANES // HEAD_DIM
    unit, nprev = (QBLK, 0) if nb == 1 else (QUNIT, QBLK // QUNIT)
    g = s // unit
    q3 = _stack_heads(qh[...].reshape(g, unit, LANES))
    k3, v3 = kh[...].reshape(g, unit, LANES), vh[...].reshape(g, unit, LANES)
    if nprev == 0:
        qi = lax.broadcasted_iota(jnp.int32, (1, nh * unit, unit), 1) % unit
        kj = lax.broadcasted_iota(jnp.int32, (1, nh * unit, unit), 2)
        return q3, k3, v3, kj <= qi, nprev
    shape = (g, nh * unit, (nprev + 1) * unit)
    qi = lax.broadcasted_iota(jnp.int32, shape, 1) % unit
    kj = lax.broadcasted_iota(jnp.int32, shape, 2)
    j = lax.broadcasted_iota(jnp.int32, shape, 0)
    per_block = QBLK // unit
    opens = ((j // per_block) % nb) == 0
    mask = (kj >= qi) & (kj <= qi + QBLK) & ((kj >= QBLK - unit * (j % per_block)) | jnp.logical_not(opens))
    return q3, _with_previous(k3, nprev), _with_previous(v3, nprev), mask, nprev


def _attn_fwd(proj3, ctab, stab, aw):
    b, s, _ = proj3.shape
    npair = aw // LANES
    scale = HEAD_DIM ** -0.5
    nbr = len(DILATION_PAIRS)

    def body(q_ref, k_ref, v_ref, c_ref, s_ref, o_ref, l_ref, qf, kf, vf, qh, kh, vh, op, lp, ob, lb):
        cc, ss = c_ref[...], s_ref[...]
        q2, k2 = q_ref[...], k_ref[...]
        qf[...] = (q2 * cc + _swap_halves(q2) * ss) * scale
        kf[...] = k2 * cc + _swap_halves(k2) * ss
        vf[...] = v_ref[...]
        for br, (window, d) in enumerate(DILATION_PAIRS):
            for r in range(d):
                nat, perm = _segment_rows(r, d, s)
                for dst, src in ((qh, qf), (kh, kf), (vh, vf)):
                    dst[perm, :] = src[nat, :].astype(MXU_DTYPE)
            o_dst, l_dst = (ob.at[br], lb.at[br]) if d == 1 else (op, lp)
            q3, kk, vv, mask, _ = _branch_operands(qh, kh, vh, s, (s // d) // QBLK)
            unit = q3.shape[1] // (LANES // HEAD_DIM)
            head_a = _head_lanes((q3.shape[0], unit, LANES))
            sc = jnp.where(mask, _bmm_nt(q3, kk), NEG_INF)
            mx = jnp.max(sc, axis=-1, keepdims=True)
            p = jnp.exp(sc - mx)
            den = jnp.sum(p, axis=-1, keepdims=True)
            o2 = _bmm(p, vv) / den
            l2 = mx + jnp.log(den)
            o_dst[...] = jnp.where(head_a, o2[:, :unit], o2[:, unit:]).reshape(s, LANES)
            l_dst[...] = jnp.where(head_a, l2[:, :unit], l2[:, unit:]).reshape(s, LANES)
            if d > 1:
                for r in range(d):
                    nat, perm = _segment_rows(r, d, s)
                    ob[br, nat, :] = op[perm, :]
                    lb[br, nat, :] = lp[perm, :]
        ls = [lb[br] for br in range(nbr)]
        mx = functools.reduce(jnp.maximum, ls)
        ws = [jnp.exp(l - mx) for l in ls]
        tot = functools.reduce(lambda a, b_: a + b_, ws)
        o_ref[...] = functools.reduce(lambda a, b_: a + b_, [(w / tot) * ob[br] for br, w in enumerate(ws)])
        l_ref[...] = mx + jnp.log(tot)

    def col(off):
        return pl.BlockSpec((None, s, LANES), lambda bi, hp, off=off: (bi, 0, off + hp))

    tab = pl.BlockSpec((None, s, LANES), lambda bi, hp: (bi, 0, 0))
    f32s = pltpu.VMEM((s, LANES), F32)
    mxs = pltpu.VMEM((s, LANES), MXU_DTYPE)
    br_s = pltpu.VMEM((nbr, s, LANES), F32)
    return pl.pallas_call(
        body, grid=(b, npair), in_specs=[col(0), col(npair), col(2 * npair), tab, tab], out_specs=[col(0), col(0)],
        out_shape=[jax.ShapeDtypeStruct((b, s, aw), F32)] * 2,
        scratch_shapes=[f32s] * 3 + [mxs] * 3 + [f32s] * 2 + [br_s] * 2,
        name="attn_fwd", compiler_params=_cparams(("arbitrary", "arbitrary")))(proj3, proj3, proj3, ctab, stab)


def _attn_bwd(proj3, ctab, stab, dout3, out3, lse3, aw):
    b, s, _ = proj3.shape
    npair = aw // LANES
    scale = HEAD_DIM ** -0.5
    nheads = LANES // HEAD_DIM

    def body(q_ref, k_ref, v_ref, c_ref, s_ref, do_ref, o_ref, l_ref, dq_ref, dk_ref, dv_ref,
             qf, kf, vf, dof, dlf, qh, kh, vh, doh, lpm, dpm, dqp, dkp, dvp, dqn, dkn, dvn):
        cc, ss = c_ref[...], s_ref[...]
        q2, k2 = q_ref[...], k_ref[...]
        qf[...] = (q2 * cc + _swap_halves(q2) * ss) * scale
        kf[...] = k2 * cc + _swap_halves(k2) * ss
        vf[...] = v_ref[...]
        do2 = do_ref[...]
        dof[...] = do2
        dd = do2 * o_ref[...]
        in_a = _head_lanes((s, LANES))
        sum_a = jnp.sum(jnp.where(in_a, dd, 0.0), axis=-1, keepdims=True)
        sum_b = jnp.sum(jnp.where(in_a, 0.0, dd), axis=-1, keepdims=True)
        dlf[...] = jnp.where(in_a, sum_a, sum_b)
        for r_ in (dqn, dkn, dvn):
            r_[...] = jnp.zeros_like(r_)
        for window, d in DILATION_PAIRS:
            for r in range(d):
                nat, perm = _segment_rows(r, d, s)
                for dst, src in ((qh, qf), (kh, kf), (vh, vf), (doh, dof)):
                    dst[perm, :] = src[nat, :].astype(MXU_DTYPE)
                if d > 1:
                    lpm[perm, :] = l_ref[nat, :]
                    dpm[perm, :] = dlf[nat, :]
            l_src, d_src = (l_ref, dlf) if d == 1 else (lpm, dpm)
            q3, kk, vv, mask, nprev = _branch_operands(qh, kh, vh, s, (s // d) // QBLK)
            g_, unit = q3.shape[0], q3.shape[1] // nheads
            head_a = _head_lanes((g_, unit, LANES))
            do3 = _stack_heads(doh[...].reshape(g_, unit, LANES))
            lcol, dcol = _head_columns(l_src[...].reshape(g_, unit, LANES)), _head_columns(d_src[...].reshape(g_, unit, LANES))
            p = jnp.exp(jnp.where(mask, _bmm_nt(q3, kk), NEG_INF) - lcol)
            ds_ = p * (_bmm_nt(do3, vv) - dcol)
            dq2 = _bmm(ds_, kk)
            dq_new = jnp.where(head_a, dq2[:, :unit], dq2[:, unit:]).reshape(s, LANES)
            dk_new = _to_own_unit(_bmm_tn(ds_, q3), nprev).reshape(s, LANES)
            dv_new = _to_own_unit(_bmm_tn(p, do3), nprev).reshape(s, LANES)
            if d == 1:
                dqn[...] += dq_new
                dkn[...] += dk_new
                dvn[...] += dv_new
            else:
                dqp[...] = dq_new
                dkp[...] = dk_new
                dvp[...] = dv_new
                for r in range(d):
                    nat, perm = _segment_rows(r, d, s)
                    dqn[nat, :] += dqp[perm, :]
                    dkn[nat, :] += dkp[perm, :]
                    dvn[nat, :] += dvp[perm, :]
        g = dqn[...] * scale
        dq_ref[...] = g * cc + _swap_halves(g * ss)
        g = dkn[...]
        dk_ref[...] = g * cc + _swap_halves(g * ss)
        dv_ref[...] = dvn[...]

    def col(off):
        return pl.BlockSpec((None, s, LANES), lambda bi, hp, off=off: (bi, 0, off + hp))

    tab = pl.BlockSpec((None, s, LANES), lambda bi, hp: (bi, 0, 0))
    f32s = pltpu.VMEM((s, LANES), F32)
    mxs = pltpu.VMEM((s, LANES), MXU_DTYPE)
    return pl.pallas_call(
        body, grid=(b, npair), in_specs=[col(0), col(npair), col(2 * npair), tab, tab, col(0), col(0), col(0)],
        out_specs=[col(0)] * 3, out_shape=[jax.ShapeDtypeStruct((b, s, aw), F32)] * 3,
        scratch_shapes=[f32s] * 5 + [mxs] * 4 + [f32s] * 8,
        name="attn_bwd", compiler_params=_cparams(("arbitrary", "arbitrary")))(
            proj3, proj3, proj3, ctab, stab, dout3, out3, lse3)


def _s5_discretise(a_re, a_im, log_dt, bt_re, bt_im):
    dt = jnp.exp(log_dt)
    mag = jnp.exp(a_re * dt)
    ang = a_im * dt
    lb_re = mag * jnp.cos(ang)
    lb_im = mag * jnp.sin(ang)
    den = a_re * a_re + a_im * a_im
    nr = lb_re - 1.0
    ni = lb_im
    cr = (nr * a_re + ni * a_im) / den
    ci = (ni * a_re - nr * a_im) / den
    return lb_re, lb_im, cr * bt_re - ci * bt_im, cr * bt_im + ci * bt_re


def _s5_params(a_re, a_im, log_dt, bt_re, bt_im):
    def body(ar, ai, ld, br, bi, o1, o2, o3, o4):
        r = _s5_discretise(ar[...], ai[...], ld[...], br[...], bi[...])
        for o, v in zip((o1, o2, o3, o4), r, strict=True):
            o[...] = v

    sd = jax.ShapeDtypeStruct
    return pl.pallas_call(body, out_shape=[sd(a_re.shape, F32)] * 2 + [sd(bt_re.shape, F32)] * 2, name="s5_params")(
        a_re, a_im, log_dt, bt_re, bt_im)


def _s5_params_bwd(a_re, a_im, log_dt, bt_re, bt_im, cts):
    def body(ar, ai, ld, br, bi, c1, c2, c3, c4, o1, o2, o3, o4, o5):
        _, vjp = jax.vjp(_s5_discretise, ar[...], ai[...], ld[...], br[...], bi[...])
        r = vjp((c1[...], c2[...], c3[...], c4[...]))
        for o, v in zip((o1, o2, o3, o4, o5), r, strict=True):
            o[...] = v

    sd = jax.ShapeDtypeStruct
    return pl.pallas_call(
        body, out_shape=[sd(a_re.shape, F32)] * 2 + [sd(log_dt.shape, F32)] + [sd(bt_re.shape, F32)] * 2, name="s5_params_bwd")(
            a_re, a_im, log_dt, bt_re, bt_im, *cts)


S5_TC = 128


def _time_major(tiles):
    nt, tc, lanes = tiles.shape
    return jnp.swapaxes(tiles, 0, 1).reshape(tc * nt, lanes)


def _tile_major(rows, nt):
    return jnp.swapaxes(rows.astype(MXU_DTYPE).reshape(rows.shape[0] // nt, nt, rows.shape[1]), 0, 1)


def _s5_fwd(proj3, ucol, lam_re, lam_im, bre, bim, cre, cim, dvec, sw):
    b, s, _ = proj3.shape
    nt = lam_re.shape[0]
    ncb = sw // LANES
    tpc = nt // ncb
    tc = S5_TC

    def body(u_ref, lr_ref, li_ref, bre_ref, bim_ref, cre_ref, cim_ref, d_ref, y_ref, xr_ref, xi_ref, sr, si):
        @pl.when(pl.program_id(0) == 0)
        def _():
            sr[...] = jnp.zeros_like(sr)
            si[...] = jnp.zeros_like(si)

        u_all = u_ref[...].reshape(b * tc, sw)
        bur, bui = [], []
        for cb in range(ncb):
            ucb = u_all[:, cb * LANES:(cb + 1) * LANES].astype(MXU_DTYPE)
            for t in range(cb * tpc, (cb + 1) * tpc):
                bur.append(_mm(ucb, bre_ref[t]))
                bui.append(_mm(ucb, bim_ref[t]))
        for bi in range(b):
            rows = slice(bi * tc, (bi + 1) * tc)
            xr_ref[bi] = _time_major(jnp.stack([v_[rows] for v_ in bur]))
            xi_ref[bi] = _time_major(jnp.stack([v_[rows] for v_ in bui]))
        lr, li = lr_ref[...], li_ref[...]

        def step(j, carry):
            off = pl.multiple_of(j * nt, nt)
            new = []
            for bi in range(b):
                pr, pi = carry[2 * bi], carry[2 * bi + 1]
                nr = lr * pr - li * pi + xr_ref[bi, pl.ds(off, nt), :]
                ni = lr * pi + li * pr + xi_ref[bi, pl.ds(off, nt), :]
                xr_ref[bi, pl.ds(off, nt), :] = nr
                xi_ref[bi, pl.ds(off, nt), :] = ni
                new += [nr, ni]
            return tuple(new)

        init = tuple(v for bi in range(b) for v in (sr[bi], si[bi]))
        fin = lax.fori_loop(0, tc, step, init, unroll=4)
        for bi in range(b):
            sr[bi] = fin[2 * bi]
            si[bi] = fin[2 * bi + 1]
        xr_t = [_tile_major(xr_ref[bi], nt) for bi in range(b)]
        xi_t = [_tile_major(xi_ref[bi], nt) for bi in range(b)]
        for cb in range(ncb):
            cols = slice(cb * LANES, (cb + 1) * LANES)
            acc = d_ref[:, cols] * u_all[:, cols]
            for t in range(cb * tpc, (cb + 1) * tpc):
                xr_all = jnp.concatenate([xr_t[bi][t] for bi in range(b)], axis=0)
                xi_all = jnp.concatenate([xi_t[bi][t] for bi in range(b)], axis=0)
                acc = acc + (_mm(xr_all, cre_ref[t]) - _mm(xi_all, cim_ref[t]))
            for bi in range(b):
                y_ref[bi, :, cols] = acc[bi * tc:(bi + 1) * tc]

    def whole(a):
        return pl.BlockSpec(a.shape, lambda c, n=a.ndim: (0,) * n)

    xblk = pl.BlockSpec((b, tc * nt, LANES), lambda c: (0, c, 0))
    return pl.pallas_call(
        body, grid=(s // tc,),
        in_specs=[pl.BlockSpec((b, tc, sw), lambda c: (0, c, ucol))] + [whole(a) for a in (lam_re, lam_im, bre, bim, cre, cim, dvec)],
        out_specs=[pl.BlockSpec((b, tc, sw), lambda c: (0, c, 0)), xblk, xblk],
        out_shape=[jax.ShapeDtypeStruct((b, s, sw), F32)] + [jax.ShapeDtypeStruct((b, s * nt, LANES), F32)] * 2,
        scratch_shapes=[pltpu.VMEM((b, nt, LANES), F32)] * 2, name="s5_fwd",
        compiler_params=_cparams(("arbitrary",)))(proj3, lam_re, lam_im, bre, bim, cre, cim, dvec)


def _s5_bwd(proj3, ucol, dy3, xr3, xi3, lam_re, lam_im, bre, bim, cre, cim, dvec, sw):
    b, s, _ = proj3.shape
    nt = lam_re.shape[0]
    ncb = sw // LANES
    tpc = nt // ncb
    tc = S5_TC
    nchunk = s // tc

    def body(u_ref, dy_ref, xr_ref, xi_ref, pr_ref, pi_ref, lr_ref, li_ref, bre_ref, bim_ref, cre_ref, cim_ref, d_ref,
             du_ref, dbre, dbim, dcre, dcim, dlr, dli, dd, gr, gi, sr, si):
        step_id = pl.program_id(0)

        @pl.when(step_id == 0)
        def _():
            for r in (sr, si, dbre, dbim, dcre, dcim, dlr, dli, dd):
                r[...] = jnp.zeros_like(r)

        u_all = u_ref[...].reshape(b * tc, sw)
        dy_all = dy_ref[...].reshape(b * tc, sw)
        dxr, dxi = [], []
        for cb in range(ncb):
            dycb = dy_all[:, cb * LANES:(cb + 1) * LANES].astype(MXU_DTYPE)
            for t in range(cb * tpc, (cb + 1) * tpc):
                dxr.append(_mm_nt(dycb, cre_ref[t]))
                dxi.append(-_mm_nt(dycb, cim_ref[t]))
        for bi in range(b):
            rows = slice(bi * tc, (bi + 1) * tc)
            gr[bi] = _time_major(jnp.stack([v_[rows] for v_ in dxr]))
            gi[bi] = _time_major(jnp.stack([v_[rows] for v_ in dxi]))
        lr, li = lr_ref[...], li_ref[...]

        def step(jj, carry):
            off = pl.multiple_of((tc - 1 - jj) * nt, nt)
            new = []
            for bi in range(b):
                nr_, ni_ = carry[2 * bi], carry[2 * bi + 1]
                vr = gr[bi, pl.ds(off, nt), :] + lr * nr_ + li * ni_
                vi = gi[bi, pl.ds(off, nt), :] + lr * ni_ - li * nr_
                gr[bi, pl.ds(off, nt), :] = vr
                gi[bi, pl.ds(off, nt), :] = vi
                new += [vr, vi]
            return tuple(new)

        init = tuple(v for bi in range(b) for v in (sr[bi], si[bi]))
        fin = lax.fori_loop(0, tc, step, init, unroll=4)
        for bi in range(b):
            sr[bi] = fin[2 * bi]
            si[bi] = fin[2 * bi + 1]

        has_prev = (step_id != nchunk - 1).astype(F32)
        rest = (tc - 1) * nt
        alr = jnp.zeros((nt, LANES), F32)
        ali = jnp.zeros((nt, LANES), F32)
        for bi in range(b):
            s0r, s0i = gr[bi, pl.ds(0, nt), :], gi[bi, pl.ds(0, nt), :]
            x0r, x0i = pr_ref[bi] * has_prev, pi_ref[bi] * has_prev
            alr += s0r * x0r + s0i * x0i
            ali += s0i * x0r - s0r * x0i
            s1r, s1i = gr[bi, pl.ds(nt, rest), :], gi[bi, pl.ds(nt, rest), :]
            x1r, x1i = xr_ref[bi, pl.ds(0, rest), :], xi_ref[bi, pl.ds(0, rest), :]
            alr += jnp.sum((s1r * x1r + s1i * x1i).reshape(tc - 1, nt, LANES), axis=0)
            ali += jnp.sum((s1i * x1r - s1r * x1i).reshape(tc - 1, nt, LANES), axis=0)
        dlr[...] += alr
        dli[...] += ali

        tiles = [[_tile_major(ref[bi], nt) for bi in range(b)] for ref in (gr, gi, xr_ref, xi_ref)]

        def stacked(k, t):
            return jnp.concatenate([tiles[k][bi][t] for bi in range(b)], axis=0)

        for cb in range(ncb):
            cols = slice(cb * LANES, (cb + 1) * LANES)
            ucb32, dycb32 = u_all[:, cols], dy_all[:, cols]
            ucb, dycb = ucb32.astype(MXU_DTYPE), dycb32.astype(MXU_DTYPE)
            acc = d_ref[:, cols] * dycb32
            for t in range(cb * tpc, (cb + 1) * tpc):
                vr, vi = stacked(0, t), stacked(1, t)
                acc = acc + (_mm_nt(vr, bre_ref[t]) + _mm_nt(vi, bim_ref[t]))
                dbre[t] += _mm_tn(ucb, vr)
                dbim[t] += _mm_tn(ucb, vi)
                dcre[t] += _mm_tn(stacked(2, t), dycb)
                dcim[t] -= _mm_tn(stacked(3, t), dycb)
            for bi in range(b):
                du_ref[bi, :, cols] = acc[bi * tc:(bi + 1) * tc]
            dd[:, cols] += _colsum(dycb32 * ucb32)

    def whole(a):
        return pl.BlockSpec(a.shape, lambda c, n=len(a.shape): (0,) * n)

    def rev(c):
        return nchunk - 1 - c

    xblk = pl.BlockSpec((b, tc * nt, LANES), lambda c: (0, rev(c), 0))
    prev = pl.BlockSpec((b, nt, LANES), lambda c: (0, jnp.maximum(rev(c) * tc - 1, 0), 0))
    sd = jax.ShapeDtypeStruct
    blk = sd(bre.shape, F32)
    acc_shapes = [blk, blk, sd(cre.shape, F32), sd(cre.shape, F32), sd(lam_re.shape, F32), sd(lam_re.shape, F32), sd((1, sw), F32)]
    return pl.pallas_call(
        body, grid=(nchunk,),
        in_specs=[pl.BlockSpec((b, tc, sw), lambda c: (0, rev(c), ucol)), pl.BlockSpec((b, tc, sw), lambda c: (0, rev(c), 0)),
                  xblk, xblk, prev, prev] + [whole(a) for a in (lam_re, lam_im, bre, bim, cre, cim, dvec)],
        out_specs=[pl.BlockSpec((b, tc, sw), lambda c: (0, rev(c), 0))] + [whole(a) for a in acc_shapes],
        out_shape=[sd((b, s, sw), F32)] + acc_shapes,
        scratch_shapes=[pltpu.VMEM((b, tc * nt, LANES), F32)] * 2 + [pltpu.VMEM((b, nt, LANES), F32)] * 2, name="s5_bwd",
        compiler_params=_cparams(("arbitrary",)))(proj3, dy3, xr3, xi3, xr3, xi3, lam_re, lam_im, bre, bim, cre, cim, dvec)


def _s5_blocks(bb, cc_, ncb):
    g, n, p = bb.shape
    mask = jnp.asarray(_s5_tile_mask(g, p, ncb))
    nt, gpt, gpc = mask.shape
    bbt, cct = bb.reshape(nt, gpt, n, p), cc_.reshape(nt, gpt, n, p)
    spread = mask[:, :, :, None, None]
    bblk = (bbt[:, :, None] * spread).transpose(0, 2, 3, 1, 4).reshape(nt, gpc * n, gpt * p)
    cblk = (cct[:, :, None] * spread).transpose(0, 1, 4, 2, 3).reshape(nt, gpt * p, gpc * n)
    return bblk, cblk


def _s5_tile_mask(g, p, ncb):
    nt, gpt, gpc = g * p // LANES, LANES // p, g // ncb
    mask = np.zeros((nt, gpt, gpc), np.float32)
    for t in range(nt):
        for gl in range(gpt):
            mask[t, gl, (t * gpt + gl) % gpc] = 1.0
    return mask


def _s5_unblock(dbblk, dcblk, g, n, p, ncb):
    mask = jnp.asarray(_s5_tile_mask(g, p, ncb))
    nt, gpt, gpc = mask.shape
    db = jnp.sum(dbblk.reshape(nt, gpc, n, gpt, p) * mask.transpose(0, 2, 1)[:, :, None, :, None], axis=1)
    dc = jnp.sum(dcblk.reshape(nt, gpt, p, gpc, n) * mask[:, :, None, :, None], axis=3)
    return db.transpose(0, 2, 1, 3).reshape(g, n, p), dc.transpose(0, 1, 3, 2).reshape(g, n, p)


def _adamw_math(w, g, m, v):
    m = ADAM_B1 * m + (1.0 - ADAM_B1) * g
    v = ADAM_B2 * v + (1.0 - ADAM_B2) * (g * g)
    m_hat = m / (1.0 - ADAM_B1 ** ADAM_STEP)
    v_hat = v / (1.0 - ADAM_B2 ** ADAM_STEP)
    delta = -ADAM_LR * (m_hat / (jnp.sqrt(v_hat) + ADAM_EPS) + ADAM_WD * w)
    return delta, m, v


def _adamw_layer(name, parts, w, m, v, layer, prev, tr):
    nparts, r, c = parts.shape
    assert r % tr == 0
    if prev is None:
        prev = [lax.empty(w.shape, F32) for _ in range(4)]

    def body(p_ref, w_ref, m_ref, v_ref, *rest):
        g_out, d_out, m_out, v_out = rest[4:]
        g = p_ref[0].astype(F32)
        for k in range(1, nparts):
            g = g + p_ref[k].astype(F32)
        delta, mn, vn = _adamw_math(w_ref[...], g, m_ref[...], v_ref[...])
        g_out[...] = g
        d_out[...] = delta
        m_out[...] = mn
        v_out[...] = vn

    blk = pl.BlockSpec((None, tr, c), lambda i: (layer, i, 0))
    kept = pl.BlockSpec(memory_space=pl.ANY)
    return pl.pallas_call(
        body, grid=(r // tr,), in_specs=[pl.BlockSpec((nparts, tr, c), lambda i: (0, i, 0)), blk, blk, blk] + [kept] * 4,
        out_specs=[blk] * 4, out_shape=[jax.ShapeDtypeStruct(w.shape, F32)] * 4, name=name,
        input_output_aliases={4 + k: k for k in range(4)},
        compiler_params=_cparams(("arbitrary",)))(parts, w, m, v, *prev)


def _piece_rows(size):
    unit = SUBLANES * LANES
    return -(-size // unit) * SUBLANES


def _pack_small(pieces, lead=0):
    rows = []
    for p in pieces:
        head = p.shape[:lead]
        flat = p.reshape(head + (-1,))
        nrow = _piece_rows(flat.shape[-1])
        flat = jnp.pad(flat, [(0, 0)] * lead + [(0, nrow * LANES - flat.shape[-1])])
        rows.append(flat.reshape(head + (nrow, LANES)))
    return jnp.concatenate(rows, axis=lead)


def _step(x, positions, weights, moments_m, moments_v, loss_target, distributed):
    f32 = F32
    bsz, seq, dm = x.shape
    depth = weights["w_in"].shape[0]
    aw = weights["attn_gain"].shape[1]
    sw = weights["ssm_gain"].shape[1]
    dff = weights["b_ff1"].shape[1]
    ng, npst = weights["ssm_a_re"].shape[1:]
    gdim = weights["ssm_d"].shape[2]
    in_w = 3 * aw + sw
    t_rows = bsz * seq
    alpha = (2.0 * depth) ** 0.25
    ncb = sw // LANES
    nt = ng * npst // LANES
    ndev = NDEV if distributed else 1
    tm = tl = 512

    gather_modes = ["gather"] * len(BIG_NAMES)
    no_token = jnp.zeros((SUBLANES, LANES), f32)

    def shards(l, names):
        return [weights[n][l].astype(MXU_DTYPE) for n in names]

    if distributed:
        first, _ = _exchange_start("weights_start_l0_in", shards(0, BIG_NAMES[:1]), gather_modes[:1])
        (g_in0,) = _exchange_wait("weights_wait_l0_in", first, gather_modes[:1], positions)
        rest0, tok_rest0 = _exchange_start("weights_start_l0_rest", shards(0, BIG_NAMES[1:]), gather_modes[1:], after=g_in0)

    half = HEAD_DIM // 2
    inv_freq = ROPE_THETA ** (-jnp.arange(half, dtype=f32) * 2.0 / HEAD_DIM)
    reps = LANES // half
    inv_row = jnp.tile(inv_freq, reps)[None, :]
    sign_row = jnp.tile(jnp.concatenate([-jnp.ones((half,), f32), jnp.ones((half,), f32)]), LANES // HEAD_DIM)[None, :]
    ctab, stab = _rope_tables(positions[..., None], inv_row, sign_row)

    def row(v):
        return v.reshape(1, -1)

    h = x.reshape(t_rows, dm)
    saved = []
    for l in range(depth):
        tok_in = tok_mix = no_token
        if not distributed:
            g_in, g_glu, g_out, g_ff1, g_ff2 = [weights[n][l].astype(MXU_DTYPE)[None] for n in BIG_NAMES]
        elif l == 0:
            g_in, tok_in = g_in0, tok_rest0
        else:
            g_in, g_glu, g_out, g_ff1, g_ff2 = _exchange_wait(f"weights_wait_l{l}", next_gather, gather_modes, h)
            if l + 1 < depth:
                next_gather, tok_in = _exchange_start(f"weights_start_l{l + 1}", shards(l + 1, BIG_NAMES), gather_modes, after=g_in)
        w_in_l = g_in

        def in_proj(rv, cr):
            return [jnp.concatenate([_mm(rv[0], cr[0][j]) for j in range(ndev)], axis=-1)], []

        (proj,), _ = _rows_call("in_proj", in_proj, [h], [w_in_l, tok_in], [(in_w, f32)], [], tl)
        proj3 = proj.reshape(bsz, seq, in_w)
        attn3, lse3 = _attn_fwd(proj3, ctab, stab, aw)
        if distributed and l == 0:
            g_glu, g_out, g_ff1, g_ff2 = _exchange_wait("weights_wait_l0_rest", rest0, gather_modes[1:], attn3)
            if depth > 1:
                next_gather, tok_mix = _exchange_start("weights_start_l1", shards(1, BIG_NAMES), gather_modes, after=g_glu)
        w_glu_l = g_glu.reshape(sw, sw)
        w_out_l = g_out.reshape(dm, dm)
        w_ff1_l = g_ff1
        w_ff2_l = g_ff2.reshape(dff, dm)

        a_re, a_im = weights["ssm_a_re"][l][:, None, :], weights["ssm_a_im"][l][:, None, :]
        log_dt = weights["ssm_log_dt"][l][:, None, None]
        bt_re = weights["ssm_b_re"][l].transpose(0, 2, 1)
        bt_im = weights["ssm_b_im"][l].transpose(0, 2, 1)
        lb_re, lb_im, bb_re, bb_im = _s5_params(a_re, a_im, log_dt, bt_re, bt_im)
        lam_re, lam_im = lb_re.reshape(nt, LANES), lb_im.reshape(nt, LANES)
        bre, cre = _s5_blocks(bb_re, weights["ssm_c_re"][l], ncb)
        bim, cim = _s5_blocks(bb_im, weights["ssm_c_im"][l], ncb)
        s5c = [a_.astype(MXU_DTYPE) for a_ in (bre, bim, cre, cim)]
        dvec = row(weights["ssm_d"][l])
        ypre3, xr3, xi3 = _s5_fwd(proj3, 3 * aw // sw, lam_re, lam_im, *s5c, dvec, sw)
        attn, ypre = attn3.reshape(t_rows, aw), ypre3.reshape(t_rows, sw)

        b_glu, ga, gs = row(weights["b_glu"][l]), row(weights["attn_gain"][l]), row(weights["ssm_gain"][l])

        def mix(rv, cr):
            at, yp = rv
            g = _gelu(yp)
            ssm = g * jax.nn.sigmoid(_mm(g, cr[0][...]) + cr[1][...])
            return [jnp.concatenate([_rms_norm(at, cr[2][...]), _rms_norm(ssm, cr[3][...])], axis=-1)], []

        (mixed,), _ = _rows_call("mix", mix, [attn, ypre], [w_glu_l, b_glu, ga, gs, tok_mix], [(dm, MXU_DTYPE)], [], tl)

        b_out, g1, b1 = row(weights["b_out"][l]), row(weights["ln1_g"][l]), row(weights["ln1_b"][l])

        def out_proj(rv, cr):
            pre = alpha * rv[1] + _mm(rv[0], cr[0][...]) + cr[1][...]
            return [pre, _layer_norm(pre, cr[2][...], cr[3][...])], []

        (pre1, h1), _ = _rows_call("out_proj", out_proj, [mixed, h], [w_out_l, b_out, g1, b1], [(dm, f32), (dm, f32)], [], tl)

        b_ff1 = row(weights["b_ff1"][l])

        def ff1(rv, cr):
            pre_act = jnp.concatenate([_mm(rv[0], cr[0][j]) for j in range(ndev)], axis=-1) + cr[1][...]
            return [jnp.square(jnp.maximum(pre_act, 0.0))], []

        (act,), _ = _rows_call("ff1", ff1, [h1], [w_ff1_l, b_ff1], [(dff, MXU_DTYPE)], [], tm)

        b_ff2, g2, b2 = row(weights["b_ff2"][l]), row(weights["ln2_g"][l]), row(weights["ln2_b"][l])

        def ff2(rv, cr):
            pre = alpha * rv[1] + _mm(rv[0], cr[0][...]) + cr[1][...]
            return [pre, _layer_norm(pre, cr[2][...], cr[3][...])], []

        (pre2, h2), _ = _rows_call("ff2", ff2, [act, h1], [w_ff2_l, b_ff2, g2, b2], [(dm, f32), (dm, f32)], [], tm)

        saved.append(dict(h=h, proj3=proj3, attn=attn, lse3=lse3, attn3=attn3, ypre=ypre, xr3=xr3, xi3=xi3, mixed=mixed, pre1=pre1,
                          h1=h1, act=act, pre2=pre2, w_in=w_in_l, w_glu=w_glu_l, w_out=w_out_l, w_ff1=w_ff1_l, w_ff2=w_ff2_l,
                          lam=(lam_re, lam_im), s5c=s5c, dvec=dvec, b_glu=b_glu, ga=ga, gs=gs, g1=g1, g2=g2,
                          s5in=(a_re, a_im, log_dt, bt_re, bt_im)))
        h = h2

    g2_last = saved[-1]["g2"]

    def loss_fn(rv, cr):
        y, tgt, pre = rv
        err = y - tgt
        part = 0.5 * jnp.sum(jnp.mean(err * err, axis=-1, keepdims=True), axis=0, keepdims=True)
        dpre, dg, db = _layer_norm_bwd(err * (1.0 / dm), pre, cr[0][...])
        return [dpre], [jnp.broadcast_to(part, (1, LANES)), dg, db, _colsum(dpre)]

    (dpre2,), (loss_acc, dg2, db2, dbff2) = _rows_call(
        "loss", loss_fn, [h, loss_target.reshape(t_rows, dm), saved[-1]["pre2"]], [g2_last], [(dm, f32)],
        [(1, LANES), (1, dm), (1, dm), (1, dm)], tl)
    loss = loss_acc[0, 0]
    if distributed:
        loss = lax.psum(loss, MESH_AXES)

    big_parts = {n: [None] * depth for n in BIG_NAMES}
    small_parts = [None] * depth
    grad_handles = [None] * depth
    grad_modes = ["scatter"] * len(BIG_NAMES) + ["gather"]
    grad_x = None
    for l in reversed(range(depth)):
        sv = saved[l]

        def ff2_bwd(rv, cr):
            r = rv[1].astype(F32)
            relu2 = jnp.where(r > 0.0, (2.0 * r) * lax.rsqrt(r), 0.0)
            da = _mm_nt(rv[0], cr[0][...]) * relu2
            return [da], [_colsum(da)]

        (dact,), (dbff1,) = _rows_call("ff2_bwd", ff2_bwd, [dpre2, sv["act"]], [sv["w_ff2"]], [(dff, MXU_DTYPE)], [(1, dff)], tm)
        big_parts["w_ff2"][l] = _wgrad_call("wgrad_ff2", sv["act"], dpre2, "x", ndev, ndev // 2 if ndev > 1 else 1,
                                            (dff // ndev, dm), tm)
        big_parts["w_ff1"][l] = _wgrad_call("wgrad_ff1", sv["h1"], dact, "y", ndev, ndev // 2 if ndev > 1 else 1,
                                            (dm, dff // ndev), tm)

        def ff1_bwd(rv, cr):
            dacc = alpha * rv[1]
            wpb = dff // ndev
            for j in range(ndev):
                dacc = dacc + _mm_nt(rv[0][:, j * wpb:(j + 1) * wpb], cr[0][j])
            dpre, dg, db = _layer_norm_bwd(dacc, rv[2], cr[1][...])
            return [dpre], [dg, db, _colsum(dpre)]

        (dpre1,), (dg1, db1, dbout) = _rows_call("ff1_bwd", ff1_bwd, [dact, dpre2, sv["pre1"]], [sv["w_ff1"], sv["g1"]],
                                                 [(dm, f32)], [(1, dm)] * 3, tm)
        big_parts["w_out"][l] = _wgrad_call("wgrad_out", sv["mixed"], dpre1, "x", ndev, ndev, (dm // ndev, dm), tl)

        def mix_bwd(rv, cr):
            dp, at, yp = rv
            w_out_r, w_glu_r, bg, ga_, gs_ = cr
            dmixed = _mm_nt(dp, w_out_r[...])
            g = _gelu(yp)
            sig = jax.nn.sigmoid(_mm(g, w_glu_r[...]) + bg[...])
            ssm = g * sig
            dat, dga = _rms_norm_bwd(dmixed[:, :aw], at, ga_[...])
            dssm, dgs = _rms_norm_bwd(dmixed[:, aw:], ssm, gs_[...])
            dz = dssm * g * sig * (1.0 - sig)
            dg = dssm * sig + _mm_nt(dz, w_glu_r[...])
            return [dat, dg * _gelu_grad(yp), dz], [dga, dgs, _colsum(dz)]

        (dattn, dypre, dz), (dga, dgs, dbglu) = _rows_call(
            "mix_bwd", mix_bwd, [dpre1, sv["attn"], sv["ypre"]], [sv["w_out"], sv["w_glu"], sv["b_glu"], sv["ga"], sv["gs"]],
            [(aw, f32), (sw, f32), (sw, MXU_DTYPE)], [(1, aw), (1, sw), (1, sw)], tl)
        big_parts["w_glu"][l] = _wgrad_call("wgrad_glu", sv["ypre"], dz, "x", 1, 1, (sw, sw), tl, prologue=_gelu).reshape(
            ndev, sw // ndev, sw)

        du3, dbre, dbim, dcre, dcim, dlr, dli, dd = _s5_bwd(
            sv["proj3"], 3 * aw // sw, dypre.reshape(bsz, seq, sw), sv["xr3"], sv["xi3"], *sv["lam"], *sv["s5c"], sv["dvec"], sw)
        dbb_re, dc_re = _s5_unblock(dbre, dcre, ng, gdim, npst, ncb)
        dbb_im, dc_im = _s5_unblock(dbim, dcim, ng, gdim, npst, ncb)
        da_re, da_im, dldt, dbt_re, dbt_im = _s5_params_bwd(
            *sv["s5in"], (dlr.reshape(ng, 1, npst), dli.reshape(ng, 1, npst), dbb_re, dbb_im))

        dq3, dk3, dv3 = _attn_bwd(sv["proj3"], ctab, stab, dattn.reshape(bsz, seq, aw), sv["attn3"], sv["lse3"], aw)
        dproj = [dq3.reshape(t_rows, aw), dk3.reshape(t_rows, aw), dv3.reshape(t_rows, aw), du3.reshape(t_rows, sw)]
        big_parts["w_in"][l] = _wgrad_call("wgrad_in", sv["h"], dproj, "y", ndev, ndev, (dm, in_w // ndev), tl)

        small_parts[l] = dict(
            attn_gain=dga, ssm_gain=dgs, ssm_a_re=da_re[:, 0], ssm_a_im=da_im[:, 0], ssm_log_dt=dldt[:, 0, 0], ssm_b_re=dbt_re.transpose(0, 2, 1),
            ssm_b_im=dbt_im.transpose(0, 2, 1), ssm_c_re=dc_re, ssm_c_im=dc_im, ssm_d=dd, b_glu=dbglu, b_out=dbout, ln1_g=dg1,
            ln1_b=db1, b_ff1=dbff1, b_ff2=dbff2, ln2_g=dg2, ln2_b=db2)

        layer_grads = [big_parts[n][l] for n in BIG_NAMES] + [_pack_small([small_parts[l][n] for n in SMALL_NAMES])]
        if distributed:
            grad_handles[l], token = _exchange_start(f"grads_start_l{l}", layer_grads, grad_modes)
        else:
            grad_handles[l], token = layer_grads, jnp.zeros((SUBLANES, LANES), f32)

        wpb = in_w // ndev
        if l > 0:
            prev = saved[l - 1]

            def in_bwd(rv, cr):
                dacc = alpha * rv[4]
                dp = jnp.concatenate([v_.astype(MXU_DTYPE) for v_ in rv[:4]], axis=-1)
                for j in range(ndev):
                    dacc = dacc + _mm_nt(dp[:, j * wpb:(j + 1) * wpb], cr[0][j])
                dpre, dg, db = _layer_norm_bwd(dacc, rv[5], cr[1][...])
                return [dpre], [dg, db, _colsum(dpre)]

            (dpre2,), (dg2, db2, dbff2) = _rows_call("in_bwd", in_bwd, dproj + [dpre1, prev["pre2"]],
                                                     [sv["w_in"], prev["g2"], token], [(dm, f32)], [(1, dm)] * 3, tl)
        else:
            def in_bwd0(rv, cr):
                dacc = alpha * rv[4]
                dp = jnp.concatenate([v_.astype(MXU_DTYPE) for v_ in rv[:4]], axis=-1)
                for j in range(ndev):
                    dacc = dacc + _mm_nt(dp[:, j * wpb:(j + 1) * wpb], cr[0][j])
                return [dacc], []

            (grad_x,), _ = _rows_call("in_bwd0", in_bwd0, dproj + [dpre1], [sv["w_in"], token], [(dm, f32)], [], tl)

    small_shapes = [weights[n].shape[1:] for n in SMALL_NAMES]
    outs = {n: None for n in BIG_NAMES}
    packed = [_pack_small([d[n] for n in SMALL_NAMES], lead=1) for d in (weights, moments_m, moments_v)]
    small_out = None
    after = grad_x
    for l in reversed(range(depth)):
        if distributed:
            recv = _exchange_wait(f"grads_wait_l{l}", grad_handles[l], grad_modes, after)
        else:
            recv = [g_[None] if i == len(BIG_NAMES) else g_ for i, g_ in enumerate(grad_handles[l])]
        for n, parts in zip(BIG_NAMES, recv[:-1], strict=True):
            outs[n] = _adamw_layer("adamw_" + n, parts, weights[n], moments_m[n], moments_v[n], l, outs[n],
                                   min(parts.shape[1], 256))
        small_out = _adamw_layer("adamw_small", recv[-1], *packed, l, small_out, recv[-1].shape[1])
        after = small_out[0]
    for k in range(4):
        row = 0
        for n, shp in zip(SMALL_NAMES, small_shapes, strict=True):
            sz = int(np.prod(shp))
            nrow = _piece_rows(sz)
            piece = small_out[k][:, row:row + nrow].reshape(depth, nrow * LANES)
            outs.setdefault(n, [None] * 4)
            outs[n][k] = piece[:, :sz].reshape((depth,) + tuple(shp))
            row += nrow

    result = [loss, grad_x.reshape(bsz, seq, dm)]
    for k in range(4):
        result += [outs[n][k] for n in WEIGHT_ORDER]
    return tuple(result)


def kernel(x, positions, w_in, attn_gain, ssm_gain, ssm_a_re, ssm_a_im, ssm_log_dt, ssm_b_re, ssm_b_im, ssm_c_re, ssm_c_im, ssm_d, w_glu, b_glu, w_out, b_out, ln1_g, ln1_b, w_ff1, b_ff1, w_ff2, b_ff2, ln2_g, ln2_b, loss_target, m_w_in, m_attn_gain, m_ssm_gain, m_ssm_a_re, m_ssm_a_im, m_ssm_log_dt, m_ssm_b_re, m_ssm_b_im, m_ssm_c_re, m_ssm_c_im, m_ssm_d, m_w_glu, m_b_glu, m_w_out, m_b_out, m_ln1_g, m_ln1_b, m_w_ff1, m_b_ff1, m_w_ff2, m_b_ff2, m_ln2_g, m_ln2_b, v_w_in, v_attn_gain, v_ssm_gain, v_ssm_a_re, v_ssm_a_im, v_ssm_log_dt, v_ssm_b_re, v_ssm_b_im, v_ssm_c_re, v_ssm_c_im, v_ssm_d, v_w_glu, v_b_glu, v_w_out, v_b_out, v_ln1_g, v_ln1_b, v_w_ff1, v_b_ff1, v_w_ff2, v_b_ff2, v_ln2_g, v_ln2_b):
    loc = locals()
    weights = {n: loc[n] for n in WEIGHT_ORDER}
    moments_m = {n: loc["m_" + n] for n in WEIGHT_ORDER}
    moments_v = {n: loc["v_" + n] for n in WEIGHT_ORDER}
    return _step(x, positions, weights, moments_m, moments_v, loss_target, distributed=True)
```

```python
import functools
import math

import jax
import jax.numpy as jnp
import numpy as np
from jax import lax
from jax.experimental import pallas as pl
from jax.experimental.pallas import tpu as pltpu

F32 = jnp.float32
MXU_DTYPE = jnp.bfloat16

HEAD_DIM = 64
DILATION_PAIRS = ((128, 1), (512, 4), (2048, 16))
ROPE_THETA = 10000.0
SSM_GROUP_DIM = 16
SSM_STATE = 64
LN_EPS = 1e-5
RMS_EPS = 1e-6
NEG_INF = -1e30
ADAM_LR, ADAM_B1, ADAM_B2, ADAM_EPS, ADAM_WD, ADAM_STEP = 0.001, 0.9, 0.999, 1e-08, 0.01, 10

LANES = 128
SUBLANES = 8
QBLK = 128
VMEM_LIMIT = 56 * 2**20
MESH_AXES = ("x", "y", "c")
NDEV = 8

SMALL_NAMES = ("attn_gain", "ssm_gain", "ssm_a_re", "ssm_a_im", "ssm_log_dt", "ssm_b_re", "ssm_b_im", "ssm_c_re",
               "ssm_c_im", "ssm_d", "b_glu", "b_out", "ln1_g", "ln1_b", "b_ff1", "b_ff2", "ln2_g", "ln2_b")
BIG_NAMES = ("w_in", "w_glu", "w_out", "w_ff1", "w_ff2")
WEIGHT_ORDER = ("w_in", "attn_gain", "ssm_gain", "ssm_a_re", "ssm_a_im", "ssm_log_dt", "ssm_b_re", "ssm_b_im", "ssm_c_re",
                "ssm_c_im", "ssm_d", "w_glu", "b_glu", "w_out", "b_out", "ln1_g", "ln1_b", "w_ff1", "b_ff1", "w_ff2",
                "b_ff2", "ln2_g", "ln2_b")


def _cparams(sem=None):
    return pltpu.CompilerParams(dimension_semantics=sem, vmem_limit_bytes=VMEM_LIMIT)


def _mm(a, b):
    return jnp.dot(a.astype(MXU_DTYPE), b.astype(MXU_DTYPE), preferred_element_type=F32)


def _mm_nt(a, b):
    return lax.dot_general(a.astype(MXU_DTYPE), b.astype(MXU_DTYPE), (((1,), (1,)), ((), ())), preferred_element_type=F32)


def _mm_tn(a, b):
    return lax.dot_general(a.astype(MXU_DTYPE), b.astype(MXU_DTYPE), (((0,), (0,)), ((), ())), preferred_element_type=F32)


def _colsum(x):
    return jnp.sum(x, axis=0, keepdims=True)


def _layer_norm(x, g, b):
    mu = jnp.mean(x, axis=-1, keepdims=True)
    xc = x - mu
    var = jnp.mean(xc * xc, axis=-1, keepdims=True)
    return xc * lax.rsqrt(var + LN_EPS) * g + b


def _layer_norm_bwd(dy, pre, g):
    mu = jnp.mean(pre, axis=-1, keepdims=True)
    xc = pre - mu
    var = jnp.mean(xc * xc, axis=-1, keepdims=True)
    r = lax.rsqrt(var + LN_EPS)
    xhat = xc * r
    dyg = dy * g
    dpre = r * (dyg - jnp.mean(dyg, axis=-1, keepdims=True) - xhat * jnp.mean(dyg * xhat, axis=-1, keepdims=True))
    return dpre, _colsum(dy * xhat), _colsum(dy)


def _rms_norm(x, g):
    return x * lax.rsqrt(jnp.mean(x * x, axis=-1, keepdims=True) + RMS_EPS) * g


def _rms_norm_bwd(dy, x, g):
    r = lax.rsqrt(jnp.mean(x * x, axis=-1, keepdims=True) + RMS_EPS)
    dyg = dy * g
    dx = dyg * r - x * (r * r * r) * jnp.mean(dyg * x, axis=-1, keepdims=True)
    return dx, _colsum(dy * x * r)


_GELU_C = math.sqrt(2.0 / math.pi)


def _gelu(x):
    return 0.5 * x * (1.0 + jnp.tanh(_GELU_C * (x + 0.044715 * (x * x * x))))


def _gelu_grad(x):
    t = jnp.tanh(_GELU_C * (x + 0.044715 * (x * x * x)))
    return 0.5 * (1.0 + t) + 0.5 * x * (1.0 - t * t) * (_GELU_C * (1.0 + 3.0 * 0.044715 * x * x))


def _rows_call(name, fn, rows, consts, out_rows, out_accs, tm):
    rows = [r if isinstance(r, tuple) else (r, r.shape[1], 0) for r in rows]
    m = rows[0][0].shape[0]
    assert m % tm == 0
    nr, nc, no, na = len(rows), len(consts), len(out_rows), len(out_accs)

    def body(*refs):
        rr, cr = refs[:nr], refs[nr:nr + nc]
        orr, ar = refs[nr + nc:nr + nc + no], refs[nr + nc + no:]
        outs, accs = fn([r[...] for r in rr], cr)
        for o, v in zip(orr, outs, strict=True):
            o[...] = v.astype(o.dtype)
        if na:
            first = pl.program_id(0) == 0

            @pl.when(first)
            def _():
                for a, v in zip(ar, accs, strict=True):
                    a[...] = v

            @pl.when(jnp.logical_not(first))
            def _():
                for a, v in zip(ar, accs, strict=True):
                    a[...] += v

    def whole(shape):
        return pl.BlockSpec(shape, lambda i, n=len(shape): (0,) * n)

    in_specs = [pl.BlockSpec((tm, w), lambda i, cb=cb: (i, cb)) for _, w, cb in rows] + [whole(c.shape) for c in consts]
    out_specs = [pl.BlockSpec((tm, w), lambda i: (i, 0)) for w, _ in out_rows] + [whole(s) for s in out_accs]
    out_shape = [jax.ShapeDtypeStruct((m, w), dt) for w, dt in out_rows] + [jax.ShapeDtypeStruct(s, F32) for s in out_accs]
    res = pl.pallas_call(body, grid=(m // tm,), in_specs=in_specs, out_specs=out_specs, out_shape=out_shape, name=name,
                         compiler_params=_cparams(("arbitrary",)))(*[r[0] for r in rows], *consts)
    return res[:no], res[no:]


def _wgrad_call(name, x, dy, split, nblk, jb, blk_shape, tm, prologue=None):
    dys = list(dy) if isinstance(dy, (list, tuple)) else [dy]
    m = x.shape[0]
    kk, nn = blk_shape
    assert m % tm == 0 and nblk % jb == 0 and (len(dys) == 1 or (split == "y" and jb == nblk))
    xw = kk * jb if split == "x" else x.shape[1]
    yws = [d.shape[1] for d in dys] if (split == "x" or len(dys) > 1) else [nn * jb]
    nrow = m // tm

    def body(x_ref, *rest):
        dy_refs, o_ref, acc_ref = rest[:len(dys)], rest[-2], rest[-1]
        i = pl.program_id(1)

        @pl.when(i == 0)
        def _():
            acc_ref[...] = jnp.zeros_like(acc_ref)

        xv = x_ref[...]
        if prologue is not None:
            xv = prologue(xv)
        xv = xv.astype(MXU_DTYPE)
        dv = [r[...].astype(MXU_DTYPE) for r in dy_refs]
        dv = dv[0] if len(dv) == 1 else jnp.concatenate(dv, axis=-1)
        for j in range(jb):
            xa = xv[:, j * kk:(j + 1) * kk] if split == "x" else xv
            da = dv[:, j * nn:(j + 1) * nn] if split == "y" else dv
            acc_ref[j] += _mm_tn(xa, da)

        @pl.when(i == nrow - 1)
        def _():
            o_ref[...] = acc_ref[...].astype(o_ref.dtype)

    in_specs = [pl.BlockSpec((tm, xw), (lambda j, i: (i, j)) if split == "x" else (lambda j, i: (i, 0)))]
    in_specs += [pl.BlockSpec((tm, yw), (lambda j, i: (i, j)) if (split == "y" and len(dys) == 1) else (lambda j, i: (i, 0)))
                 for yw in yws]
    return pl.pallas_call(
        body, grid=(nblk // jb, nrow), in_specs=in_specs,
        out_specs=pl.BlockSpec((jb, kk, nn), lambda j, i: (j, 0, 0)),
        out_shape=jax.ShapeDtypeStruct((nblk, kk, nn), MXU_DTYPE),
        scratch_shapes=[pltpu.VMEM((jb, kk, nn), F32)], name=name,
        compiler_params=_cparams(("arbitrary", "arbitrary")))(x, *dys)


_HBM_SPEC = pl.BlockSpec(memory_space=pltpu.HBM)
_SEM_SPEC = pl.BlockSpec(memory_space=pltpu.SEMAPHORE)
_EFFECT = pltpu.SideEffectType.DATAFLOW_SIDE_EFFECTING


def _my_index():
    return 4 * lax.axis_index("x") + 2 * lax.axis_index("y") + lax.axis_index("c")


def _peer_copies(ins, lands, send_sems, recv_sems, modes):
    x, y, c = lax.axis_index("x"), lax.axis_index("y"), lax.axis_index("c")
    me = 4 * x + 2 * y + c
    pairs = []
    for k in range(NDEV - 1):
        fx, fy, fc = ((k + 1) >> 2) & 1, ((k + 1) >> 1) & 1, (k + 1) & 1
        px, py, pc = (x + fx) % 2, (y + fy) % 2, (c + fc) % 2
        idx = 4 * px + 2 * py + pc
        for a, md in enumerate(modes):
            src = ins[a] if md == "gather" else ins[a].at[idx]
            sem = a * (NDEV - 1) + k
            common = dict(src_ref=src, send_sem=send_sems.at[sem], recv_sem=recv_sems.at[sem], device_id=(px, py, pc),
                          device_id_type=pl.DeviceIdType.MESH)
            pairs.append((pltpu.make_async_remote_copy(dst_ref=lands[a].at[me], **common),
                          pltpu.make_async_remote_copy(dst_ref=lands[a].at[idx], **common)))
    return pairs


def _exchange_start(name, arrays, modes, after=None):
    n = len(arrays)
    extra = [] if after is None else [after]
    me = _my_index()
    lands = []
    for a, md in zip(arrays, modes, strict=True):
        piece = a if md == "gather" else lax.dynamic_index_in_dim(a, me, 0, keepdims=False)
        lands.append(lax.dynamic_update_index_in_dim(lax.empty((NDEV,) + piece.shape, a.dtype), piece, me, 0))

    def body(*refs):
        ins, lnd = refs[:n], refs[n:2 * n]
        send_sems, recv_sems = refs[2 * n + len(extra)], refs[2 * n + len(extra) + 1]
        token = refs[-1]
        for out_copy, _ in _peer_copies(ins, lnd, send_sems, recv_sems, modes):
            out_copy.start()
        token[...] = jnp.zeros_like(token)

    sems = pltpu.SemaphoreType.DMA((n * (NDEV - 1),))
    thru = [pltpu.HBM(a.shape, a.dtype) for a in list(arrays) + lands]
    res = pl.pallas_call(
        body, name=name, out_shape=(sems, sems, *thru, jax.ShapeDtypeStruct((SUBLANES, LANES), F32)),
        in_specs=[_HBM_SPEC] * (2 * n) + [pl.BlockSpec(memory_space=pl.ANY)] * len(extra),
        out_specs=(_SEM_SPEC, _SEM_SPEC, *([_HBM_SPEC] * (2 * n)), pl.BlockSpec(memory_space=pltpu.VMEM)),
        input_output_aliases={i: 2 + i for i in range(2 * n)},
        compiler_params=pltpu.CompilerParams(has_side_effects=_EFFECT),
    )(*[pltpu.with_memory_space_constraint(a, pltpu.HBM) for a in list(arrays) + lands], *extra)
    return (res[0], res[1], res[2:2 + n], res[2 + n:2 + 2 * n]), res[-1]


def _exchange_wait(name, handle, modes, after):
    send_sems, recv_sems, ins_thru, lands_thru = handle
    n = len(ins_thru)

    def body(*refs):
        ins, lnd = refs[:n], refs[n:2 * n]
        for out_copy, arrival in _peer_copies(ins, lnd, refs[2 * n], refs[2 * n + 1], modes):
            out_copy.wait_send()
            arrival.wait_recv()

    thru = [pltpu.HBM(a.shape, a.dtype) for a in list(ins_thru) + list(lands_thru)]
    res = pl.pallas_call(
        body, name=name, out_shape=tuple(thru),
        in_specs=[_HBM_SPEC] * (2 * n) + [_SEM_SPEC, _SEM_SPEC, pl.BlockSpec(memory_space=pl.ANY)],
        out_specs=tuple([_HBM_SPEC] * (2 * n)), input_output_aliases={i: i for i in range(2 * n)},
        compiler_params=pltpu.CompilerParams(has_side_effects=_EFFECT),
    )(*ins_thru, *lands_thru, send_sems, recv_sems, after)
    return res[n:]


def _rope_tables(positions3, inv_freq_row, sign_row):
    b, s, _ = positions3.shape

    def body(pos_ref, f_ref, sg_ref, c_ref, s_ref):
        ang = pos_ref[...].astype(F32) * f_ref[...]
        c_ref[...] = jnp.cos(ang)
        s_ref[...] = jnp.sin(ang) * sg_ref[...]

    row = pl.BlockSpec((1, LANES), lambda i: (0, 0))
    blk = pl.BlockSpec((None, s, LANES), lambda i: (i, 0, 0))
    return pl.pallas_call(
        body, grid=(b,), in_specs=[pl.BlockSpec((None, s, 1), lambda i: (i, 0, 0)), row, row], out_specs=[blk, blk],
        out_shape=[jax.ShapeDtypeStruct((b, s, LANES), F32)] * 2, name="rope_tables",
        compiler_params=_cparams(("arbitrary",)))(positions3, inv_freq_row, sign_row)


def _swap_halves(t):
    lane = lax.broadcasted_iota(jnp.int32, t.shape, 1)
    half = HEAD_DIM // 2
    return jnp.where((lane % HEAD_DIM) < half, pltpu.roll(t, LANES - half, 1), pltpu.roll(t, half, 1))


def _segment_rows(r, d, s):
    n = s // d
    return (pl.ds(r, n, stride=d) if d > 1 else pl.ds(0, s)), pl.ds(r * n, n)


def _head_lanes(shape):
    lane = lax.broadcasted_iota(jnp.int32, shape, len(shape) - 1)
    return lane < HEAD_DIM


def _bmm_nt(a, b):
    return lax.dot_general(a.astype(MXU_DTYPE), b.astype(MXU_DTYPE), (((2,), (2,)), ((0,), (0,))), preferred_element_type=F32)


def _bmm(a, b):
    return lax.dot_general(a.astype(MXU_DTYPE), b.astype(MXU_DTYPE), (((2,), (1,)), ((0,), (0,))), preferred_element_type=F32)


def _bmm_tn(a, b):
    return lax.dot_general(a.astype(MXU_DTYPE), b.astype(MXU_DTYPE), (((1,), (1,)), ((0,), (0,))), preferred_element_type=F32)


def _stack_heads(t3):
    head_a = _head_lanes(t3.shape)
    zero = jnp.zeros_like(t3)
    return jnp.concatenate([jnp.where(head_a, t3, zero), jnp.where(head_a, zero, t3)], axis=1)


QUNIT = QBLK // 2


def _with_previous(t3, nprev):
    shifted = [jnp.concatenate([t3[:k], t3[:-k]], axis=0) for k in range(nprev, 0, -1)]
    return jnp.concatenate(shifted + [t3], axis=1)


def _head_columns(t3):
    return jnp.concatenate([t3[:, :, lo:lo + 1] for lo in range(0, LANES, HEAD_DIM)], axis=1)


def _to_own_unit(t, nprev):
    unit = t.shape[1] // (nprev + 1)
    out = t[:, nprev * unit:]
    for k in range(1, nprev + 1):
        part = t[:, (nprev - k) * unit:(nprev - k + 1) * unit]
        out = out + jnp.concatenate([part[k:], jnp.zeros_like(part[:k])], axis=0)
    return out


def _branch_operands(qh, kh, vh, s, nb):
    nh = LANES // HEAD_DIM
    unit, nprev = (QBLK, 0) if nb == 1 else (QUNIT, QBLK // QUNIT)
    g = s // unit
    q3 = _stack_heads(qh[...].reshape(g, unit, LANES))
    k3, v3 = kh[...].reshape(g, unit, LANES), vh[...].reshape(g, unit, LANES)
    if nprev == 0:
        qi = lax.broadcasted_iota(jnp.int32, (1, nh * unit, unit), 1) % unit
        kj = lax.broadcasted_iota(jnp.int32, (1, nh * unit, unit), 2)
        return q3, k3, v3, kj <= qi, nprev
    shape = (g, nh * unit, (nprev + 1) * unit)
    qi = lax.broadcasted_iota(jnp.int32, shape, 1) % unit
    kj = lax.broadcasted_iota(jnp.int32, shape, 2)
    j = lax.broadcasted_iota(jnp.int32, shape, 0)
    per_block = QBLK // unit
    opens = ((j // per_block) % nb) == 0
    mask = (kj >= qi) & (kj <= qi + QBLK) & ((kj >= QBLK - unit * (j % per_block)) | jnp.logical_not(opens))
    return q3, _with_previous(k3, nprev), _with_previous(v3, nprev), mask, nprev


def _attn_fwd(proj3, ctab, stab, aw):
    b, s, _ = proj3.shape
    npair = aw // LANES
    scale = HEAD_DIM ** -0.5
    nbr = len(DILATION_PAIRS)

    def body(q_ref, k_ref, v_ref, c_ref, s_ref, o_ref, l_ref, qf, kf, vf, qh, kh, vh, op, lp, ob, lb):
        cc, ss = c_ref[...], s_ref[...]
        q2, k2 = q_ref[...], k_ref[...]
        qf[...] = (q2 * cc + _swap_halves(q2) * ss) * scale
        kf[...] = k2 * cc + _swap_halves(k2) * ss
        vf[...] = v_ref[...]
        for br, (window, d) in enumerate(DILATION_PAIRS):
            for r in range(d):
                nat, perm = _segment_rows(r, d, s)
                for dst, src in ((qh, qf), (kh, kf), (vh, vf)):
                    dst[perm, :] = src[nat, :].astype(MXU_DTYPE)
            o_dst, l_dst = (ob.at[br], lb.at[br]) if d == 1 else (op, lp)
            q3, kk, vv, mask, _ = _branch_operands(qh, kh, vh, s, (s // d) // QBLK)
            unit = q3.shape[1] // (LANES // HEAD_DIM)
            head_a = _head_lanes((q3.shape[0], unit, LANES))
            sc = jnp.where(mask, _bmm_nt(q3, kk), NEG_INF)
            mx = jnp.max(sc, axis=-1, keepdims=True)
            p = jnp.exp(sc - mx)
            den = jnp.sum(p, axis=-1, keepdims=True)
            o2 = _bmm(p, vv) / den
            l2 = mx + jnp.log(den)
            o_dst[...] = jnp.where(head_a, o2[:, :unit], o2[:, unit:]).reshape(s, LANES)
            l_dst[...] = jnp.where(head_a, l2[:, :unit], l2[:, unit:]).reshape(s, LANES)
            if d > 1:
                for r in range(d):
                    nat, perm = _segment_rows(r, d, s)
                    ob[br, nat, :] = op[perm, :]
                    lb[br, nat, :] = lp[perm, :]
        ls = [lb[br] for br in range(nbr)]
        mx = functools.reduce(jnp.maximum, ls)
        ws = [jnp.exp(l - mx) for l in ls]
        tot = functools.reduce(lambda a, b_: a + b_, ws)
        o_ref[...] = functools.reduce(lambda a, b_: a + b_, [(w / tot) * ob[br] for br, w in enumerate(ws)])
        l_ref[...] = mx + jnp.log(tot)

    def col(off):
        return pl.BlockSpec((None, s, LANES), lambda bi, hp, off=off: (bi, 0, off + hp))

    tab = pl.BlockSpec((None, s, LANES), lambda bi, hp: (bi, 0, 0))
    f32s = pltpu.VMEM((s, LANES), F32)
    mxs = pltpu.VMEM((s, LANES), MXU_DTYPE)
    br_s = pltpu.VMEM((nbr, s, LANES), F32)
    return pl.pallas_call(
        body, grid=(b, npair), in_specs=[col(0), col(npair), col(2 * npair), tab, tab], out_specs=[col(0), col(0)],
        out_shape=[jax.ShapeDtypeStruct((b, s, aw), F32)] * 2,
        scratch_shapes=[f32s] * 3 + [mxs] * 3 + [f32s] * 2 + [br_s] * 2,
        name="attn_fwd", compiler_params=_cparams(("arbitrary", "arbitrary")))(proj3, proj3, proj3, ctab, stab)


def _attn_bwd(proj3, ctab, stab, dout3, out3, lse3, aw):
    b, s, _ = proj3.shape
    npair = aw // LANES
    scale = HEAD_DIM ** -0.5
    nheads = LANES // HEAD_DIM

    def body(q_ref, k_ref, v_ref, c_ref, s_ref, do_ref, o_ref, l_ref, dq_ref, dk_ref, dv_ref,
             qf, kf, vf, dof, dlf, qh, kh, vh, doh, lpm, dpm, dqp, dkp, dvp, dqn, dkn, dvn):
        cc, ss = c_ref[...], s_ref[...]
        q2, k2 = q_ref[...], k_ref[...]
        qf[...] = (q2 * cc + _swap_halves(q2) * ss) * scale
        kf[...] = k2 * cc + _swap_halves(k2) * ss
        vf[...] = v_ref[...]
        do2 = do_ref[...]
        dof[...] = do2
        dd = do2 * o_ref[...]
        in_a = _head_lanes((s, LANES))
        sum_a = jnp.sum(jnp.where(in_a, dd, 0.0), axis=-1, keepdims=True)
        sum_b = jnp.sum(jnp.where(in_a, 0.0, dd), axis=-1, keepdims=True)
        dlf[...] = jnp.where(in_a, sum_a, sum_b)
        for r_ in (dqn, dkn, dvn):
            r_[...] = jnp.zeros_like(r_)
        for window, d in DILATION_PAIRS:
            for r in range(d):
                nat, perm = _segment_rows(r, d, s)
                for dst, src in ((qh, qf), (kh, kf), (vh, vf), (doh, dof)):
                    dst[perm, :] = src[nat, :].astype(MXU_DTYPE)
                if d > 1:
                    lpm[perm, :] = l_ref[nat, :]
                    dpm[perm, :] = dlf[nat, :]
            l_src, d_src = (l_ref, dlf) if d == 1 else (lpm, dpm)
            q3, kk, vv, mask, nprev = _branch_operands(qh, kh, vh, s, (s // d) // QBLK)
            g_, unit = q3.shape[0], q3.shape[1] // nheads
            head_a = _head_lanes((g_, unit, LANES))
            do3 = _stack_heads(doh[...].reshape(g_, unit, LANES))
            lcol, dcol = _head_columns(l_src[...].reshape(g_, unit, LANES)), _head_columns(d_src[...].reshape(g_, unit, LANES))
            p = jnp.exp(jnp.where(mask, _bmm_nt(q3, kk), NEG_INF) - lcol)
            ds_ = p * (_bmm_nt(do3, vv) - dcol)
            dq2 = _bmm(ds_, kk)
            dq_new = jnp.where(head_a, dq2[:, :unit], dq2[:, unit:]).reshape(s, LANES)
            dk_new = _to_own_unit(_bmm_tn(ds_, q3), nprev).reshape(s, LANES)
            dv_new = _to_own_unit(_bmm_tn(p, do3), nprev).reshape(s, LANES)
            if d == 1:
                dqn[...] += dq_new
                dkn[...] += dk_new
                dvn[...] += dv_new
            else:
                dqp[...] = dq_new
                dkp[...] = dk_new
                dvp[...] = dv_new
                for r in range(d):
                    nat, perm = _segment_rows(r, d, s)
                    dqn[nat, :] += dqp[perm, :]
                    dkn[nat, :] += dkp[perm, :]
                    dvn[nat, :] += dvp[perm, :]
        g = dqn[...] * scale
        dq_ref[...] = g * cc + _swap_halves(g * ss)
        g = dkn[...]
        dk_ref[...] = g * cc + _swap_halves(g * ss)
        dv_ref[...] = dvn[...]

    def col(off):
        return pl.BlockSpec((None, s, LANES), lambda bi, hp, off=off: (bi, 0, off + hp))

    tab = pl.BlockSpec((None, s, LANES), lambda bi, hp: (bi, 0, 0))
    f32s = pltpu.VMEM((s, LANES), F32)
    mxs = pltpu.VMEM((s, LANES), MXU_DTYPE)
    return pl.pallas_call(
        body, grid=(b, npair), in_specs=[col(0), col(npair), col(2 * npair), tab, tab, col(0), col(0), col(0)],
        out_specs=[col(0)] * 3, out_shape=[jax.ShapeDtypeStruct((b, s, aw), F32)] * 3,
        scratch_shapes=[f32s] * 5 + [mxs] * 4 + [f32s] * 8,
        name="attn_bwd", compiler_params=_cparams(("arbitrary", "arbitrary")))(
            proj3, proj3, proj3, ctab, stab, dout3, out3, lse3)


def _s5_discretise(a_re, a_im, log_dt, bt_re, bt_im):
    dt = jnp.exp(log_dt)
    mag = jnp.exp(a_re * dt)
    ang = a_im * dt
    lb_re = mag * jnp.cos(ang)
    lb_im = mag * jnp.sin(ang)
    den = a_re * a_re + a_im * a_im
    nr = lb_re - 1.0
    ni = lb_im
    cr = (nr * a_re + ni * a_im) / den
    ci = (ni * a_re - nr * a_im) / den
    return lb_re, lb_im, cr * bt_re - ci * bt_im, cr * bt_im + ci * bt_re


def _s5_params(a_re, a_im, log_dt, bt_re, bt_im):
    def body(ar, ai, ld, br, bi, o1, o2, o3, o4):
        r = _s5_discretise(ar[...], ai[...], ld[...], br[...], bi[...])
        for o, v in zip((o1, o2, o3, o4), r, strict=True):
            o[...] = v

    sd = jax.ShapeDtypeStruct
    return pl.pallas_call(body, out_shape=[sd(a_re.shape, F32)] * 2 + [sd(bt_re.shape, F32)] * 2, name="s5_params")(
        a_re, a_im, log_dt, bt_re, bt_im)


def _s5_params_bwd(a_re, a_im, log_dt, bt_re, bt_im, cts):
    def body(ar, ai, ld, br, bi, c1, c2, c3, c4, o1, o2, o3, o4, o5):
        _, vjp = jax.vjp(_s5_discretise, ar[...], ai[...], ld[...], br[...], bi[...])
        r = vjp((c1[...], c2[...], c3[...], c4[...]))
        for o, v in zip((o1, o2, o3, o4, o5), r, strict=True):
            o[...] = v

    sd = jax.ShapeDtypeStruct
    return pl.pallas_call(
        body, out_shape=[sd(a_re.shape, F32)] * 2 + [sd(log_dt.shape, F32)] + [sd(bt_re.shape, F32)] * 2, name="s5_params_bwd")(
            a_re, a_im, log_dt, bt_re, bt_im, *cts)


S5_TC = 128


def _time_major(tiles):
    nt, tc, lanes = tiles.shape
    return jnp.swapaxes(tiles, 0, 1).reshape(tc * nt, lanes)


def _tile_major(rows, nt):
    return jnp.swapaxes(rows.astype(MXU_DTYPE).reshape(rows.shape[0] // nt, nt, rows.shape[1]), 0, 1)


def _s5_fwd(proj3, ucol, lam_re, lam_im, bre, bim, cre, cim, dvec, sw):
    b, s, _ = proj3.shape
    nt = lam_re.shape[0]
    ncb = sw // LANES
    tpc = nt // ncb
    tc = S5_TC

    def body(u_ref, lr_ref, li_ref, bre_ref, bim_ref, cre_ref, cim_ref, d_ref, y_ref, xr_ref, xi_ref, sr, si):
        @pl.when(pl.program_id(0) == 0)
        def _():
            sr[...] = jnp.zeros_like(sr)
            si[...] = jnp.zeros_like(si)

        u_all = u_ref[...].reshape(b * tc, sw)
        bur, bui = [], []
        for cb in range(ncb):
            ucb = u_all[:, cb * LANES:(cb + 1) * LANES].astype(MXU_DTYPE)
            for t in range(cb * tpc, (cb + 1) * tpc):
                bur.append(_mm(ucb, bre_ref[t]))
                bui.append(_mm(ucb, bim_ref[t]))
        for bi in range(b):
            rows = slice(bi * tc, (bi + 1) * tc)
            xr_ref[bi] = _time_major(jnp.stack([v_[rows] for v_ in bur]))
            xi_ref[bi] = _time_major(jnp.stack([v_[rows] for v_ in bui]))
        lr, li = lr_ref[...], li_ref[...]

        def step(j, carry):
            off = pl.multiple_of(j * nt, nt)
            new = []
            for bi in range(b):
                pr, pi = carry[2 * bi], carry[2 * bi + 1]
                nr = lr * pr - li * pi + xr_ref[bi, pl.ds(off, nt), :]
                ni = lr * pi + li * pr + xi_ref[bi, pl.ds(off, nt), :]
                xr_ref[bi, pl.ds(off, nt), :] = nr
                xi_ref[bi, pl.ds(off, nt), :] = ni
                new += [nr, ni]
            return tuple(new)

        init = tuple(v for bi in range(b) for v in (sr[bi], si[bi]))
        fin = lax.fori_loop(0, tc, step, init, unroll=4)
        for bi in range(b):
            sr[bi] = fin[2 * bi]
            si[bi] = fin[2 * bi + 1]
        xr_t = [_tile_major(xr_ref[bi], nt) for bi in range(b)]
        xi_t = [_tile_major(xi_ref[bi], nt) for bi in range(b)]
        for cb in range(ncb):
            cols = slice(cb * LANES, (cb + 1) * LANES)
            acc = d_ref[:, cols] * u_all[:, cols]
            for t in range(cb * tpc, (cb + 1) * tpc):
                xr_all = jnp.concatenate([xr_t[bi][t] for bi in range(b)], axis=0)
                xi_all = jnp.concatenate([xi_t[bi][t] for bi in range(b)], axis=0)
                acc = acc + (_mm(xr_all, cre_ref[t]) - _mm(xi_all, cim_ref[t]))
            for bi in range(b):
                y_ref[bi, :, cols] = acc[bi * tc:(bi + 1) * tc]

    def whole(a):
        return pl.BlockSpec(a.shape, lambda c, n=a.ndim: (0,) * n)

    xblk = pl.BlockSpec((b, tc * nt, LANES), lambda c: (0, c, 0))
    return pl.pallas_call(
        body, grid=(s // tc,),
        in_specs=[pl.BlockSpec((b, tc, sw), lambda c: (0, c, ucol))] + [whole(a) for a in (lam_re, lam_im, bre, bim, cre, cim, dvec)],
        out_specs=[pl.BlockSpec((b, tc, sw), lambda c: (0, c, 0)), xblk, xblk],
        out_shape=[jax.ShapeDtypeStruct((b, s, sw), F32)] + [jax.ShapeDtypeStruct((b, s * nt, LANES), F32)] * 2,
        scratch_shapes=[pltpu.VMEM((b, nt, LANES), F32)] * 2, name="s5_fwd",
        compiler_params=_cparams(("arbitrary",)))(proj3, lam_re, lam_im, bre, bim, cre, cim, dvec)


def _s5_bwd(proj3, ucol, dy3, xr3, xi3, lam_re, lam_im, bre, bim, cre, cim, dvec, sw):
    b, s, _ = proj3.shape
    nt = lam_re.shape[0]
    ncb = sw // LANES
    tpc = nt // ncb
    tc = S5_TC
    nchunk = s // tc

    def body(u_ref, dy_ref, xr_ref, xi_ref, pr_ref, pi_ref, lr_ref, li_ref, bre_ref, bim_ref, cre_ref, cim_ref, d_ref,
             du_ref, dbre, dbim, dcre, dcim, dlr, dli, dd, gr, gi, sr, si):
        step_id = pl.program_id(0)

        @pl.when(step_id == 0)
        def _():
            for r in (sr, si, dbre, dbim, dcre, dcim, dlr, dli, dd):
                r[...] = jnp.zeros_like(r)

        u_all = u_ref[...].reshape(b * tc, sw)
        dy_all = dy_ref[...].reshape(b * tc, sw)
        dxr, dxi = [], []
        for cb in range(ncb):
            dycb = dy_all[:, cb * LANES:(cb + 1) * LANES].astype(MXU_DTYPE)
            for t in range(cb * tpc, (cb + 1) * tpc):
                dxr.append(_mm_nt(dycb, cre_ref[t]))
                dxi.append(-_mm_nt(dycb, cim_ref[t]))
        for bi in range(b):
            rows = slice(bi * tc, (bi + 1) * tc)
            gr[bi] = _time_major(jnp.stack([v_[rows] for v_ in dxr]))
            gi[bi] = _time_major(jnp.stack([v_[rows] for v_ in dxi]))
        lr, li = lr_ref[...], li_ref[...]

        has_prev = (step_id != nchunk - 1).astype(F32)
        before = [(pr_ref[bi] * has_prev, pi_ref[bi] * has_prev) for bi in range(b)]

        def step(jj, carry):
            t = tc - 1 - jj
            off = pl.multiple_of(t * nt, nt)
            poff = pl.multiple_of(jnp.maximum(t - 1, 0) * nt, nt)
            new = []
            alr, ali = carry[-2], carry[-1]
            for bi in range(b):
                nr_, ni_ = carry[2 * bi], carry[2 * bi + 1]
                vr = gr[bi, pl.ds(off, nt), :] + lr * nr_ + li * ni_
                vi = gi[bi, pl.ds(off, nt), :] + lr * ni_ - li * nr_
                gr[bi, pl.ds(off, nt), :] = vr
                gi[bi, pl.ds(off, nt), :] = vi
                xpr = jnp.where(t > 0, xr_ref[bi, pl.ds(poff, nt), :], before[bi][0])
                xpi = jnp.where(t > 0, xi_ref[bi, pl.ds(poff, nt), :], before[bi][1])
                alr = alr + (vr * xpr + vi * xpi)
                ali = ali + (vi * xpr - vr * xpi)
                new += [vr, vi]
            return tuple(new) + (alr, ali)

        zero = jnp.zeros((nt, LANES), F32)
        init = tuple(v for bi in range(b) for v in (sr[bi], si[bi])) + (zero, zero)
        fin = lax.fori_loop(0, tc, step, init, unroll=4)
        for bi in range(b):
            sr[bi] = fin[2 * bi]
            si[bi] = fin[2 * bi + 1]
        dlr[...] += fin[-2]
        dli[...] += fin[-1]

        tiles = [[_tile_major(ref[bi], nt) for bi in range(b)] for ref in (gr, gi, xr_ref, xi_ref)]

        def stacked(k, t):
            return jnp.concatenate([tiles[k][bi][t] for bi in range(b)], axis=0)

        for cb in range(ncb):
            cols = slice(cb * LANES, (cb + 1) * LANES)
            ucb32, dycb32 = u_all[:, cols], dy_all[:, cols]
            ucb, dycb = ucb32.astype(MXU_DTYPE), dycb32.astype(MXU_DTYPE)
            acc = d_ref[:, cols] * dycb32
            for t in range(cb * tpc, (cb + 1) * tpc):
                vr, vi = stacked(0, t), stacked(1, t)
                acc = acc + (_mm_nt(vr, bre_ref[t]) + _mm_nt(vi, bim_ref[t]))
                dbre[t] += _mm_tn(ucb, vr)
                dbim[t] += _mm_tn(ucb, vi)
                dcre[t] += _mm_tn(stacked(2, t), dycb)
                dcim[t] -= _mm_tn(stacked(3, t), dycb)
            for bi in range(b):
                du_ref[bi, :, cols] = acc[bi * tc:(bi + 1) * tc]
            dd[:, cols] += _colsum(dycb32 * ucb32)

    def whole(a):
        return pl.BlockSpec(a.shape, lambda c, n=len(a.shape): (0,) * n)

    def rev(c):
        return nchunk - 1 - c

    xblk = pl.BlockSpec((b, tc * nt, LANES), lambda c: (0, rev(c), 0))
    prev = pl.BlockSpec((b, nt, LANES), lambda c: (0, jnp.maximum(rev(c) * tc - 1, 0), 0))
    sd = jax.ShapeDtypeStruct
    blk = sd(bre.shape, F32)
    acc_shapes = [blk, blk, sd(cre.shape, F32), sd(cre.shape, F32), sd(lam_re.shape, F32), sd(lam_re.shape, F32), sd((1, sw), F32)]
    return pl.pallas_call(
        body, grid=(nchunk,),
        in_specs=[pl.BlockSpec((b, tc, sw), lambda c: (0, rev(c), ucol)), pl.BlockSpec((b, tc, sw), lambda c: (0, rev(c), 0)),
                  xblk, xblk, prev, prev] + [whole(a) for a in (lam_re, lam_im, bre, bim, cre, cim, dvec)],
        out_specs=[pl.BlockSpec((b, tc, sw), lambda c: (0, rev(c), 0))] + [whole(a) for a in acc_shapes],
        out_shape=[sd((b, s, sw), F32)] + acc_shapes,
        scratch_shapes=[pltpu.VMEM((b, tc * nt, LANES), F32)] * 2 + [pltpu.VMEM((b, nt, LANES), F32)] * 2, name="s5_bwd",
        compiler_params=_cparams(("arbitrary",)))(proj3, dy3, xr3, xi3, xr3, xi3, lam_re, lam_im, bre, bim, cre, cim, dvec)


def _s5_blocks(bb, cc_, ncb):
    g, n, p = bb.shape
    mask = jnp.asarray(_s5_tile_mask(g, p, ncb))
    nt, gpt, gpc = mask.shape
    bbt, cct = bb.reshape(nt, gpt, n, p), cc_.reshape(nt, gpt, n, p)
    spread = mask[:, :, :, None, None]
    bblk = (bbt[:, :, None] * spread).transpose(0, 2, 3, 1, 4).reshape(nt, gpc * n, gpt * p)
    cblk = (cct[:, :, None] * spread).transpose(0, 1, 4, 2, 3).reshape(nt, gpt * p, gpc * n)
    return bblk, cblk


def _s5_tile_mask(g, p, ncb):
    nt, gpt, gpc = g * p // LANES, LANES // p, g // ncb
    mask = np.zeros((nt, gpt, gpc), np.float32)
    for t in range(nt):
        for gl in range(gpt):
            mask[t, gl, (t * gpt + gl) % gpc] = 1.0
    return mask


def _s5_unblock(dbblk, dcblk, g, n, p, ncb):
    mask = jnp.asarray(_s5_tile_mask(g, p, ncb))
    nt, gpt, gpc = mask.shape
    db = jnp.sum(dbblk.reshape(nt, gpc, n, gpt, p) * mask.transpose(0, 2, 1)[:, :, None, :, None], axis=1)
    dc = jnp.sum(dcblk.reshape(nt, gpt, p, gpc, n) * mask[:, :, None, :, None], axis=3)
    return db.transpose(0, 2, 1, 3).reshape(g, n, p), dc.transpose(0, 1, 3, 2).reshape(g, n, p)


def _adamw_math(w, g, m, v):
    m = ADAM_B1 * m + (1.0 - ADAM_B1) * g
    v = ADAM_B2 * v + (1.0 - ADAM_B2) * (g * g)
    m_hat = m / (1.0 - ADAM_B1 ** ADAM_STEP)
    v_hat = v / (1.0 - ADAM_B2 ** ADAM_STEP)
    delta = -ADAM_LR * (m_hat / (jnp.sqrt(v_hat) + ADAM_EPS) + ADAM_WD * w)
    return delta, m, v


def _adamw_layer(name, parts, w, m, v, layer, prev, tr):
    nparts, r, c = parts.shape
    assert r % tr == 0
    if prev is None:
        prev = [lax.empty(w.shape, F32) for _ in range(4)]

    def body(p_ref, w_ref, m_ref, v_ref, *rest):
        g_out, d_out, m_out, v_out = rest[4:]
        g = p_ref[0].astype(F32)
        for k in range(1, nparts):
            g = g + p_ref[k].astype(F32)
        delta, mn, vn = _adamw_math(w_ref[...], g, m_ref[...], v_ref[...])
        g_out[...] = g
        d_out[...] = delta
        m_out[...] = mn
        v_out[...] = vn

    blk = pl.BlockSpec((None, tr, c), lambda i: (layer, i, 0))
    kept = pl.BlockSpec(memory_space=pl.ANY)
    return pl.pallas_call(
        body, grid=(r // tr,), in_specs=[pl.BlockSpec((nparts, tr, c), lambda i: (0, i, 0)), blk, blk, blk] + [kept] * 4,
        out_specs=[blk] * 4, out_shape=[jax.ShapeDtypeStruct(w.shape, F32)] * 4, name=name,
        input_output_aliases={4 + k: k for k in range(4)},
        compiler_params=_cparams(("arbitrary",)))(parts, w, m, v, *prev)


def _piece_rows(size):
    unit = SUBLANES * LANES
    return -(-size // unit) * SUBLANES


def _pack_small(pieces, lead=0):
    rows = []
    for p in pieces:
        head = p.shape[:lead]
        flat = p.reshape(head + (-1,))
        nrow = _piece_rows(flat.shape[-1])
        flat = jnp.pad(flat, [(0, 0)] * lead + [(0, nrow * LANES - flat.shape[-1])])
        rows.append(flat.reshape(head + (nrow, LANES)))
    return jnp.concatenate(rows, axis=lead)


def _step(x, positions, weights, moments_m, moments_v, loss_target, distributed):
    f32 = F32
    bsz, seq, dm = x.shape
    depth = weights["w_in"].shape[0]
    aw = weights["attn_gain"].shape[1]
    sw = weights["ssm_gain"].shape[1]
    dff = weights["b_ff1"].shape[1]
    ng, npst = weights["ssm_a_re"].shape[1:]
    gdim = weights["ssm_d"].shape[2]
    in_w = 3 * aw + sw
    t_rows = bsz * seq
    alpha = (2.0 * depth) ** 0.25
    ncb = sw // LANES
    nt = ng * npst // LANES
    ndev = NDEV if distributed else 1
    tm = tl = 512

    gather_modes = ["gather"] * len(BIG_NAMES)
    no_token = jnp.zeros((SUBLANES, LANES), f32)

    def shards(l, names):
        return [weights[n][l].astype(MXU_DTYPE) for n in names]

    if distributed:
        first, _ = _exchange_start("weights_start_l0_in", shards(0, BIG_NAMES[:1]), gather_modes[:1])
        (g_in0,) = _exchange_wait("weights_wait_l0_in", first, gather_modes[:1], positions)
        rest0, tok_rest0 = _exchange_start("weights_start_l0_rest", shards(0, BIG_NAMES[1:]), gather_modes[1:], after=g_in0)

    half = HEAD_DIM // 2
    inv_freq = ROPE_THETA ** (-jnp.arange(half, dtype=f32) * 2.0 / HEAD_DIM)
    reps = LANES // half
    inv_row = jnp.tile(inv_freq, reps)[None, :]
    sign_row = jnp.tile(jnp.concatenate([-jnp.ones((half,), f32), jnp.ones((half,), f32)]), LANES // HEAD_DIM)[None, :]
    ctab, stab = _rope_tables(positions[..., None], inv_row, sign_row)

    def row(v):
        return v.reshape(1, -1)

    h = x.reshape(t_rows, dm)
    saved = []
    for l in range(depth):
        tok_in = tok_mix = no_token
        if not distributed:
            g_in, g_glu, g_out, g_ff1, g_ff2 = [weights[n][l].astype(MXU_DTYPE)[None] for n in BIG_NAMES]
        elif l == 0:
            g_in, tok_in = g_in0, tok_rest0
        else:
            g_in, g_glu, g_out, g_ff1, g_ff2 = _exchange_wait(f"weights_wait_l{l}", next_gather, gather_modes, h)
            if l + 1 < depth:
                next_gather, tok_in = _exchange_start(f"weights_start_l{l + 1}", shards(l + 1, BIG_NAMES), gather_modes, after=g_in)
        w_in_l = g_in

        def in_proj(rv, cr):
            return [jnp.concatenate([_mm(rv[0], cr[0][j]) for j in range(ndev)], axis=-1)], []

        (proj,), _ = _rows_call("in_proj", in_proj, [h], [w_in_l, tok_in], [(in_w, f32)], [], tl)
        proj3 = proj.reshape(bsz, seq, in_w)
        attn3, lse3 = _attn_fwd(proj3, ctab, stab, aw)
        if distributed and l == 0:
            g_glu, g_out, g_ff1, g_ff2 = _exchange_wait("weights_wait_l0_rest", rest0, gather_modes[1:], attn3)
            if depth > 1:
                next_gather, tok_mix = _exchange_start("weights_start_l1", shards(1, BIG_NAMES), gather_modes, after=g_glu)
        w_glu_l = g_glu.reshape(sw, sw)
        w_out_l = g_out.reshape(dm, dm)
        w_ff1_l = g_ff1
        w_ff2_l = g_ff2.reshape(dff, dm)

        a_re, a_im = weights["ssm_a_re"][l][:, None, :], weights["ssm_a_im"][l][:, None, :]
        log_dt = weights["ssm_log_dt"][l][:, None, None]
        bt_re = weights["ssm_b_re"][l].transpose(0, 2, 1)
        bt_im = weights["ssm_b_im"][l].transpose(0, 2, 1)
        lb_re, lb_im, bb_re, bb_im = _s5_params(a_re, a_im, log_dt, bt_re, bt_im)
        lam_re, lam_im = lb_re.reshape(nt, LANES), lb_im.reshape(nt, LANES)
        bre, cre = _s5_blocks(bb_re, weights["ssm_c_re"][l], ncb)
        bim, cim = _s5_blocks(bb_im, weights["ssm_c_im"][l], ncb)
        s5c = [a_.astype(MXU_DTYPE) for a_ in (bre, bim, cre, cim)]
        dvec = row(weights["ssm_d"][l])
        ypre3, xr3, xi3 = _s5_fwd(proj3, 3 * aw // sw, lam_re, lam_im, *s5c, dvec, sw)
        attn, ypre = attn3.reshape(t_rows, aw), ypre3.reshape(t_rows, sw)

        b_glu, ga, gs = row(weights["b_glu"][l]), row(weights["attn_gain"][l]), row(weights["ssm_gain"][l])

        def mix(rv, cr):
            at, yp = rv
            g = _gelu(yp)
            ssm = g * jax.nn.sigmoid(_mm(g, cr[0][...]) + cr[1][...])
            return [jnp.concatenate([_rms_norm(at, cr[2][...]), _rms_norm(ssm, cr[3][...])], axis=-1)], []

        (mixed,), _ = _rows_call("mix", mix, [attn, ypre], [w_glu_l, b_glu, ga, gs, tok_mix], [(dm, MXU_DTYPE)], [], tl)

        b_out, g1, b1 = row(weights["b_out"][l]), row(weights["ln1_g"][l]), row(weights["ln1_b"][l])

        def out_proj(rv, cr):
            pre = alpha * rv[1] + _mm(rv[0], cr[0][...]) + cr[1][...]
            return [pre, _layer_norm(pre, cr[2][...], cr[3][...])], []

        (pre1, h1), _ = _rows_call("out_proj", out_proj, [mixed, h], [w_out_l, b_out, g1, b1], [(dm, f32), (dm, f32)], [], tl)

        b_ff1 = row(weights["b_ff1"][l])

        def ff1(rv, cr):
            pre_act = jnp.concatenate([_mm(rv[0], cr[0][j]) for j in range(ndev)], axis=-1) + cr[1][...]
            return [jnp.square(jnp.maximum(pre_act, 0.0))], []

        (act,), _ = _rows_call("ff1", ff1, [h1], [w_ff1_l, b_ff1], [(dff, MXU_DTYPE)], [], tm)

        b_ff2, g2, b2 = row(weights["b_ff2"][l]), row(weights["ln2_g"][l]), row(weights["ln2_b"][l])

        def ff2(rv, cr):
            pre = alpha * rv[1] + _mm(rv[0], cr[0][...]) + cr[1][...]
            return [pre, _layer_norm(pre, cr[2][...], cr[3][...])], []

        (pre2, h2), _ = _rows_call("ff2", ff2, [act, h1], [w_ff2_l, b_ff2, g2, b2], [(dm, f32), (dm, f32)], [], tm)

        saved.append(dict(h=h, proj3=proj3, attn=attn, lse3=lse3, attn3=attn3, ypre=ypre, xr3=xr3, xi3=xi3, mixed=mixed, pre1=pre1,
                          h1=h1, act=act, pre2=pre2, w_in=w_in_l, w_glu=w_glu_l, w_out=w_out_l, w_ff1=w_ff1_l, w_ff2=w_ff2_l,
                          lam=(lam_re, lam_im), s5c=s5c, dvec=dvec, b_glu=b_glu, ga=ga, gs=gs, g1=g1, g2=g2,
                          s5in=(a_re, a_im, log_dt, bt_re, bt_im)))
        h = h2

    g2_last = saved[-1]["g2"]

    def loss_fn(rv, cr):
        y, tgt, pre = rv
        err = y - tgt
        part = 0.5 * jnp.sum(jnp.mean(err * err, axis=-1, keepdims=True), axis=0, keepdims=True)
        dpre, dg, db = _layer_norm_bwd(err * (1.0 / dm), pre, cr[0][...])
        return [dpre], [jnp.broadcast_to(part, (1, LANES)), dg, db, _colsum(dpre)]

    (dpre2,), (loss_acc, dg2, db2, dbff2) = _rows_call(
        "loss", loss_fn, [h, loss_target.reshape(t_rows, dm), saved[-1]["pre2"]], [g2_last], [(dm, f32)],
        [(1, LANES), (1, dm), (1, dm), (1, dm)], tl)
    loss = loss_acc[0, 0]
    if distributed:
        loss = lax.psum(loss, MESH_AXES)

    big_parts = {n: [None] * depth for n in BIG_NAMES}
    small_parts = [None] * depth
    grad_handles = [None] * depth
    grad_modes = ["scatter"] * len(BIG_NAMES) + ["gather"]
    grad_x = None
    for l in reversed(range(depth)):
        sv = saved[l]

        def ff2_bwd(rv, cr):
            r = rv[1].astype(F32)
            relu2 = jnp.where(r > 0.0, (2.0 * r) * lax.rsqrt(r), 0.0)
            da = _mm_nt(rv[0], cr[0][...]) * relu2
            return [da], [_colsum(da)]

        (dact,), (dbff1,) = _rows_call("ff2_bwd", ff2_bwd, [dpre2, sv["act"]], [sv["w_ff2"]], [(dff, MXU_DTYPE)], [(1, dff)], tm)
        big_parts["w_ff2"][l] = _wgrad_call("wgrad_ff2", sv["act"], dpre2, "x", ndev, ndev // 2 if ndev > 1 else 1,
                                            (dff // ndev, dm), tm)
        big_parts["w_ff1"][l] = _wgrad_call("wgrad_ff1", sv["h1"], dact, "y", ndev, ndev // 2 if ndev > 1 else 1,
                                            (dm, dff // ndev), tm)

        def ff1_bwd(rv, cr):
            dacc = alpha * rv[1]
            wpb = dff // ndev
            for j in range(ndev):
                dacc = dacc + _mm_nt(rv[0][:, j * wpb:(j + 1) * wpb], cr[0][j])
            dpre, dg, db = _layer_norm_bwd(dacc, rv[2], cr[1][...])
            return [dpre], [dg, db, _colsum(dpre)]

        (dpre1,), (dg1, db1, dbout) = _rows_call("ff1_bwd", ff1_bwd, [dact, dpre2, sv["pre1"]], [sv["w_ff1"], sv["g1"]],
                                                 [(dm, f32)], [(1, dm)] * 3, tm)
        big_parts["w_out"][l] = _wgrad_call("wgrad_out", sv["mixed"], dpre1, "x", ndev, ndev, (dm // ndev, dm), tl)

        def mix_bwd(rv, cr):
            dp, at, yp = rv
            w_out_r, w_glu_r, bg, ga_, gs_ = cr
            dmixed = _mm_nt(dp, w_out_r[...])
            g = _gelu(yp)
            sig = jax.nn.sigmoid(_mm(g, w_glu_r[...]) + bg[...])
            ssm = g * sig
            dat, dga = _rms_norm_bwd(dmixed[:, :aw], at, ga_[...])
            dssm, dgs = _rms_norm_bwd(dmixed[:, aw:], ssm, gs_[...])
            dz = dssm * g * sig * (1.0 - sig)
            dg = dssm * sig + _mm_nt(dz, w_glu_r[...])
            return [dat, dg * _gelu_grad(yp), dz], [dga, dgs, _colsum(dz)]

        (dattn, dypre, dz), (dga, dgs, dbglu) = _rows_call(
            "mix_bwd", mix_bwd, [dpre1, sv["attn"], sv["ypre"]], [sv["w_out"], sv["w_glu"], sv["b_glu"], sv["ga"], sv["gs"]],
            [(aw, f32), (sw, f32), (sw, MXU_DTYPE)], [(1, aw), (1, sw), (1, sw)], tl)
        big_parts["w_glu"][l] = _wgrad_call("wgrad_glu", sv["ypre"], dz, "x", 1, 1, (sw, sw), tl, prologue=_gelu).reshape(
            ndev, sw // ndev, sw)

        du3, dbre, dbim, dcre, dcim, dlr, dli, dd = _s5_bwd(
            sv["proj3"], 3 * aw // sw, dypre.reshape(bsz, seq, sw), sv["xr3"], sv["xi3"], *sv["lam"], *sv["s5c"], sv["dvec"], sw)
        dbb_re, dc_re = _s5_unblock(dbre, dcre, ng, gdim, npst, ncb)
        dbb_im, dc_im = _s5_unblock(dbim, dcim, ng, gdim, npst, ncb)
        da_re, da_im, dldt, dbt_re, dbt_im = _s5_params_bwd(
            *sv["s5in"], (dlr.reshape(ng, 1, npst), dli.reshape(ng, 1, npst), dbb_re, dbb_im))

        dq3, dk3, dv3 = _attn_bwd(sv["proj3"], ctab, stab, dattn.reshape(bsz, seq, aw), sv["attn3"], sv["lse3"], aw)
        dproj = [dq3.reshape(t_rows, aw), dk3.reshape(t_rows, aw), dv3.reshape(t_rows, aw), du3.reshape(t_rows, sw)]
        big_parts["w_in"][l] = _wgrad_call("wgrad_in", sv["h"], dproj, "y", ndev, ndev, (dm, in_w // ndev), tl)

        small_parts[l] = dict(
            attn_gain=dga, ssm_gain=dgs, ssm_a_re=da_re[:, 0], ssm_a_im=da_im[:, 0], ssm_log_dt=dldt[:, 0, 0], ssm_b_re=dbt_re.transpose(0, 2, 1),
            ssm_b_im=dbt_im.transpose(0, 2, 1), ssm_c_re=dc_re, ssm_c_im=dc_im, ssm_d=dd, b_glu=dbglu, b_out=dbout, ln1_g=dg1,
            ln1_b=db1, b_ff1=dbff1, b_ff2=dbff2, ln2_g=dg2, ln2_b=db2)

        layer_grads = [big_parts[n][l] for n in BIG_NAMES] + [_pack_small([small_parts[l][n] for n in SMALL_NAMES])]
        if distributed:
            grad_handles[l], token = _exchange_start(f"grads_start_l{l}", layer_grads, grad_modes)
        else:
            grad_handles[l], token = layer_grads, jnp.zeros((SUBLANES, LANES), f32)

        wpb = in_w // ndev
        if l > 0:
            prev = saved[l - 1]

            def in_bwd(rv, cr):
                dacc = alpha * rv[4]
                dp = jnp.concatenate([v_.astype(MXU_DTYPE) for v_ in rv[:4]], axis=-1)
                for j in range(ndev):
                    dacc = dacc + _mm_nt(dp[:, j * wpb:(j + 1) * wpb], cr[0][j])
                dpre, dg, db = _layer_norm_bwd(dacc, rv[5], cr[1][...])
                return [dpre], [dg, db, _colsum(dpre)]

            (dpre2,), (dg2, db2, dbff2) = _rows_call("in_bwd", in_bwd, dproj + [dpre1, prev["pre2"]],
                                                     [sv["w_in"], prev["g2"], token], [(dm, f32)], [(1, dm)] * 3, tl)
        else:
            def in_bwd0(rv, cr):
                dacc = alpha * rv[4]
                dp = jnp.concatenate([v_.astype(MXU_DTYPE) for v_ in rv[:4]], axis=-1)
                for j in range(ndev):
                    dacc = dacc + _mm_nt(dp[:, j * wpb:(j + 1) * wpb], cr[0][j])
                return [dacc], []

            (grad_x,), _ = _rows_call("in_bwd0", in_bwd0, dproj + [dpre1], [sv["w_in"], token], [(dm, f32)], [], tl)

    small_shapes = [weights[n].shape[1:] for n in SMALL_NAMES]
    outs = {n: None for n in BIG_NAMES}
    packed = [_pack_small([d[n] for n in SMALL_NAMES], lead=1) for d in (weights, moments_m, moments_v)]
    small_out = None
    after = grad_x
    for l in reversed(range(depth)):
        if distributed:
            recv = _exchange_wait(f"grads_wait_l{l}", grad_handles[l], grad_modes, after)
        else:
            recv = [g_[None] if i == len(BIG_NAMES) else g_ for i, g_ in enumerate(grad_handles[l])]
        for n, parts in zip(BIG_NAMES, recv[:-1], strict=True):
            outs[n] = _adamw_layer("adamw_" + n, parts, weights[n], moments_m[n], moments_v[n], l, outs[n],
                                   min(parts.shape[1], 256))
        small_out = _adamw_layer("adamw_small", recv[-1], *packed, l, small_out, recv[-1].shape[1])
        after = small_out[0]
    for k in range(4):
        row = 0
        for n, shp in zip(SMALL_NAMES, small_shapes, strict=True):
            sz = int(np.prod(shp))
            nrow = _piece_rows(sz)
            piece = small_out[k][:, row:row + nrow].reshape(depth, nrow * LANES)
            outs.setdefault(n, [None] * 4)
            outs[n][k] = piece[:, :sz].reshape((depth,) + tuple(shp))
            row += nrow

    result = [loss, grad_x.reshape(bsz, seq, dm)]
    for k in range(4):
        result += [outs[n][k] for n in WEIGHT_ORDER]
    return tuple(result)


def kernel(x, positions, w_in, attn_gain, ssm_gain, ssm_a_re, ssm_a_im, ssm_log_dt, ssm_b_re, ssm_b_im, ssm_c_re, ssm_c_im, ssm_d, w_glu, b_glu, w_out, b_out, ln1_g, ln1_b, w_ff1, b_ff1, w_ff2, b_ff2, ln2_g, ln2_b, loss_target, m_w_in, m_attn_gain, m_ssm_gain, m_ssm_a_re, m_ssm_a_im, m_ssm_log_dt, m_ssm_b_re, m_ssm_b_im, m_ssm_c_re, m_ssm_c_im, m_ssm_d, m_w_glu, m_b_glu, m_w_out, m_b_out, m_ln1_g, m_ln1_b, m_w_ff1, m_b_ff1, m_w_ff2, m_b_ff2, m_ln2_g, m_ln2_b, v_w_in, v_attn_gain, v_ssm_gain, v_ssm_a_re, v_ssm_a_im, v_ssm_log_dt, v_ssm_b_re, v_ssm_b_im, v_ssm_c_re, v_ssm_c_im, v_ssm_d, v_w_glu, v_b_glu, v_w_out, v_b_out, v_ln1_g, v_ln1_b, v_w_ff1, v_b_ff1, v_w_ff2, v_b_ff2, v_ln2_g, v_ln2_b):
    loc = locals()
    weights = {n: loc[n] for n in WEIGHT_ORDER}
    moments_m = {n: loc["m_" + n] for n in WEIGHT_ORDER}
    moments_v = {n: loc["v_" + n] for n in WEIGHT_ORDER}
    return _step(x, positions, weights, moments_m, moments_v, loss_target, distributed=True)
```

```python
import functools
import math

import jax
import jax.numpy as jnp
import numpy as np
from jax import lax
from jax.experimental import pallas as pl
from jax.experimental.pallas import tpu as pltpu

F32 = jnp.float32
MXU_DTYPE = jnp.bfloat16

HEAD_DIM = 64
DILATION_PAIRS = ((128, 1), (512, 4), (2048, 16))
ROPE_THETA = 10000.0
LN_EPS = 1e-5
RMS_EPS = 1e-6
NEG_INF = -1e30
ADAM_LR, ADAM_B1, ADAM_B2, ADAM_EPS, ADAM_WD, ADAM_STEP = 0.001, 0.9, 0.999, 1e-08, 0.01, 10

LANES = 128
SUBLANES = 8
QBLK = 128
VMEM_LIMIT = 56 * 2**20
MESH_AXES = ("x", "y", "c")
NDEV = 8

SMALL_NAMES = ("attn_gain", "ssm_gain", "ssm_a_re", "ssm_a_im", "ssm_log_dt", "ssm_b_re", "ssm_b_im", "ssm_c_re",
               "ssm_c_im", "ssm_d", "b_glu", "b_out", "ln1_g", "ln1_b", "b_ff1", "b_ff2", "ln2_g", "ln2_b")
BIG_NAMES = ("w_in", "w_glu", "w_out", "w_ff1", "w_ff2")
WEIGHT_ORDER = ("w_in", "attn_gain", "ssm_gain", "ssm_a_re", "ssm_a_im", "ssm_log_dt", "ssm_b_re", "ssm_b_im", "ssm_c_re",
                "ssm_c_im", "ssm_d", "w_glu", "b_glu", "w_out", "b_out", "ln1_g", "ln1_b", "w_ff1", "b_ff1", "w_ff2",
                "b_ff2", "ln2_g", "ln2_b")


def _cparams(sem=None):
    return pltpu.CompilerParams(dimension_semantics=sem, vmem_limit_bytes=VMEM_LIMIT)


def _mm(a, b):
    return jnp.dot(a.astype(MXU_DTYPE), b.astype(MXU_DTYPE), preferred_element_type=F32)


def _mm_nt(a, b):
    return lax.dot_general(a.astype(MXU_DTYPE), b.astype(MXU_DTYPE), (((1,), (1,)), ((), ())), preferred_element_type=F32)


def _mm_tn(a, b):
    return lax.dot_general(a.astype(MXU_DTYPE), b.astype(MXU_DTYPE), (((0,), (0,)), ((), ())), preferred_element_type=F32)


def _colsum(x):
    return jnp.sum(x, axis=0, keepdims=True)


def _layer_norm(x, g, b):
    mu = jnp.mean(x, axis=-1, keepdims=True)
    xc = x - mu
    var = jnp.mean(xc * xc, axis=-1, keepdims=True)
    return xc * lax.rsqrt(var + LN_EPS) * g + b


def _layer_norm_bwd(dy, pre, g):
    mu = jnp.mean(pre, axis=-1, keepdims=True)
    xc = pre - mu
    var = jnp.mean(xc * xc, axis=-1, keepdims=True)
    r = lax.rsqrt(var + LN_EPS)
    xhat = xc * r
    dyg = dy * g
    dpre = r * (dyg - jnp.mean(dyg, axis=-1, keepdims=True) - xhat * jnp.mean(dyg * xhat, axis=-1, keepdims=True))
    return dpre, _colsum(dy * xhat), _colsum(dy)


def _rms_norm(x, g):
    return x * lax.rsqrt(jnp.mean(x * x, axis=-1, keepdims=True) + RMS_EPS) * g


def _rms_norm_bwd(dy, x, g):
    r = lax.rsqrt(jnp.mean(x * x, axis=-1, keepdims=True) + RMS_EPS)
    dyg = dy * g
    dx = dyg * r - x * (r * r * r) * jnp.mean(dyg * x, axis=-1, keepdims=True)
    return dx, _colsum(dy * x * r)


_GELU_C = math.sqrt(2.0 / math.pi)


def _gelu(x):
    return 0.5 * x * (1.0 + jnp.tanh(_GELU_C * (x + 0.044715 * (x * x * x))))


def _gelu_grad(x):
    t = jnp.tanh(_GELU_C * (x + 0.044715 * (x * x * x)))
    return 0.5 * (1.0 + t) + 0.5 * x * (1.0 - t * t) * (_GELU_C * (1.0 + 3.0 * 0.044715 * x * x))


def _rows_call(name, fn, rows, consts, out_rows, out_accs, tm):
    rows = [r if isinstance(r, tuple) else (r, r.shape[1], 0) for r in rows]
    m = rows[0][0].shape[0]
    assert m % tm == 0
    nr, nc, no, na = len(rows), len(consts), len(out_rows), len(out_accs)

    def body(*refs):
        rr, cr = refs[:nr], refs[nr:nr + nc]
        orr, ar = refs[nr + nc:nr + nc + no], refs[nr + nc + no:]
        outs, accs = fn([r[...] for r in rr], cr)
        for o, v in zip(orr, outs, strict=True):
            o[...] = v.astype(o.dtype)
        if na:
            first = pl.program_id(0) == 0

            @pl.when(first)
            def _():
                for a, v in zip(ar, accs, strict=True):
                    a[...] = v

            @pl.when(jnp.logical_not(first))
            def _():
                for a, v in zip(ar, accs, strict=True):
                    a[...] += v

    def whole(shape):
        return pl.BlockSpec(shape, lambda i, n=len(shape): (0,) * n)

    in_specs = [pl.BlockSpec((tm, w), lambda i, cb=cb: (i, cb)) for _, w, cb in rows] + [whole(c.shape) for c in consts]
    out_specs = [pl.BlockSpec((tm, w), lambda i: (i, 0)) for w, _ in out_rows] + [whole(s) for s in out_accs]
    out_shape = [jax.ShapeDtypeStruct((m, w), dt) for w, dt in out_rows] + [jax.ShapeDtypeStruct(s, F32) for s in out_accs]
    res = pl.pallas_call(body, grid=(m // tm,), in_specs=in_specs, out_specs=out_specs, out_shape=out_shape, name=name,
                         compiler_params=_cparams(("arbitrary",)))(*[r[0] for r in rows], *consts)
    return res[:no], res[no:]


def _wgrad_call(name, x, dy, split, nblk, jb, blk_shape, tm, prologue=None):
    dys = list(dy) if isinstance(dy, (list, tuple)) else [dy]
    m = x.shape[0]
    kk, nn = blk_shape
    assert m % tm == 0 and nblk % jb == 0 and (len(dys) == 1 or (split == "y" and jb == nblk))
    xw = kk * jb if split == "x" else x.shape[1]
    yws = [d.shape[1] for d in dys] if (split == "x" or len(dys) > 1) else [nn * jb]
    nrow = m // tm

    def body(x_ref, *rest):
        dy_refs, o_ref, acc_ref = rest[:len(dys)], rest[-2], rest[-1]
        i = pl.program_id(1)

        @pl.when(i == 0)
        def _():
            acc_ref[...] = jnp.zeros_like(acc_ref)

        xv = x_ref[...]
        if prologue is not None:
            xv = prologue(xv)
        xv = xv.astype(MXU_DTYPE)
        dv = [r[...].astype(MXU_DTYPE) for r in dy_refs]
        dv = dv[0] if len(dv) == 1 else jnp.concatenate(dv, axis=-1)
        for j in range(jb):
            xa = xv[:, j * kk:(j + 1) * kk] if split == "x" else xv
            da = dv[:, j * nn:(j + 1) * nn] if split == "y" else dv
            acc_ref[j] += _mm_tn(xa, da)

        @pl.when(i == nrow - 1)
        def _():
            o_ref[...] = acc_ref[...].astype(o_ref.dtype)

    in_specs = [pl.BlockSpec((tm, xw), (lambda j, i: (i, j)) if split == "x" else (lambda j, i: (i, 0)))]
    in_specs += [pl.BlockSpec((tm, yw), (lambda j, i: (i, j)) if (split == "y" and len(dys) == 1) else (lambda j, i: (i, 0)))
                 for yw in yws]
    return pl.pallas_call(
        body, grid=(nblk // jb, nrow), in_specs=in_specs,
        out_specs=pl.BlockSpec((jb, kk, nn), lambda j, i: (j, 0, 0)),
        out_shape=jax.ShapeDtypeStruct((nblk, kk, nn), MXU_DTYPE),
        scratch_shapes=[pltpu.VMEM((jb, kk, nn), F32)], name=name,
        compiler_params=_cparams(("arbitrary", "arbitrary")))(x, *dys)


_HBM_SPEC = pl.BlockSpec(memory_space=pltpu.HBM)
_SEM_SPEC = pl.BlockSpec(memory_space=pltpu.SEMAPHORE)
_EFFECT = pltpu.SideEffectType.DATAFLOW_SIDE_EFFECTING


def _my_index():
    return 4 * lax.axis_index("x") + 2 * lax.axis_index("y") + lax.axis_index("c")


def _peer_copies(ins, lands, send_sems, recv_sems, modes):
    x, y, c = lax.axis_index("x"), lax.axis_index("y"), lax.axis_index("c")
    me = 4 * x + 2 * y + c
    pairs = []
    for k in range(NDEV - 1):
        fx, fy, fc = ((k + 1) >> 2) & 1, ((k + 1) >> 1) & 1, (k + 1) & 1
        px, py, pc = (x + fx) % 2, (y + fy) % 2, (c + fc) % 2
        idx = 4 * px + 2 * py + pc
        for a, md in enumerate(modes):
            src = ins[a] if md == "gather" else ins[a].at[idx]
            sem = a * (NDEV - 1) + k
            common = dict(src_ref=src, send_sem=send_sems.at[sem], recv_sem=recv_sems.at[sem], device_id=(px, py, pc),
                          device_id_type=pl.DeviceIdType.MESH)
            pairs.append((pltpu.make_async_remote_copy(dst_ref=lands[a].at[me], **common),
                          pltpu.make_async_remote_copy(dst_ref=lands[a].at[idx], **common)))
    return pairs


def _exchange_start(name, arrays, modes, after=None):
    n = len(arrays)
    extra = [] if after is None else [after]
    me = _my_index()
    lands = []
    for a, md in zip(arrays, modes, strict=True):
        piece = a if md == "gather" else lax.dynamic_index_in_dim(a, me, 0, keepdims=False)
        lands.append(lax.dynamic_update_index_in_dim(lax.empty((NDEV,) + piece.shape, a.dtype), piece, me, 0))

    def body(*refs):
        ins, lnd = refs[:n], refs[n:2 * n]
        send_sems, recv_sems = refs[2 * n + len(extra)], refs[2 * n + len(extra) + 1]
        token = refs[-1]
        for out_copy, _ in _peer_copies(ins, lnd, send_sems, recv_sems, modes):
            out_copy.start()
        token[...] = jnp.zeros_like(token)

    sems = pltpu.SemaphoreType.DMA((n * (NDEV - 1),))
    thru = [pltpu.HBM(a.shape, a.dtype) for a in list(arrays) + lands]
    res = pl.pallas_call(
        body, name=name, out_shape=(sems, sems, *thru, jax.ShapeDtypeStruct((SUBLANES, LANES), F32)),
        in_specs=[_HBM_SPEC] * (2 * n) + [pl.BlockSpec(memory_space=pl.ANY)] * len(extra),
        out_specs=(_SEM_SPEC, _SEM_SPEC, *([_HBM_SPEC] * (2 * n)), pl.BlockSpec(memory_space=pltpu.VMEM)),
        input_output_aliases={i: 2 + i for i in range(2 * n)},
        compiler_params=pltpu.CompilerParams(has_side_effects=_EFFECT),
    )(*[pltpu.with_memory_space_constraint(a, pltpu.HBM) for a in list(arrays) + lands], *extra)
    return (res[0], res[1], res[2:2 + n], res[2 + n:2 + 2 * n]), res[-1]


def _exchange_wait(name, handle, modes, after):
    send_sems, recv_sems, ins_thru, lands_thru = handle
    n = len(ins_thru)

    def body(*refs):
        ins, lnd = refs[:n], refs[n:2 * n]
        for out_copy, arrival in _peer_copies(ins, lnd, refs[2 * n], refs[2 * n + 1], modes):
            out_copy.wait_send()
            arrival.wait_recv()

    thru = [pltpu.HBM(a.shape, a.dtype) for a in list(ins_thru) + list(lands_thru)]
    res = pl.pallas_call(
        body, name=name, out_shape=tuple(thru),
        in_specs=[_HBM_SPEC] * (2 * n) + [_SEM_SPEC, _SEM_SPEC, pl.BlockSpec(memory_space=pl.ANY)],
        out_specs=tuple([_HBM_SPEC] * (2 * n)), input_output_aliases={i: i for i in range(2 * n)},
        compiler_params=pltpu.CompilerParams(has_side_effects=_EFFECT),
    )(*ins_thru, *lands_thru, send_sems, recv_sems, after)
    return res[n:]


def _rope_tables(positions3, inv_freq_row, sign_row):
    b, s, _ = positions3.shape

    def body(pos_ref, f_ref, sg_ref, c_ref, s_ref):
        ang = pos_ref[...].astype(F32) * f_ref[...]
        c_ref[...] = jnp.cos(ang)
        s_ref[...] = jnp.sin(ang) * sg_ref[...]

    row = pl.BlockSpec((1, LANES), lambda i: (0, 0))
    blk = pl.BlockSpec((None, s, LANES), lambda i: (i, 0, 0))
    return pl.pallas_call(
        body, grid=(b,), in_specs=[pl.BlockSpec((None, s, 1), lambda i: (i, 0, 0)), row, row], out_specs=[blk, blk],
        out_shape=[jax.ShapeDtypeStruct((b, s, LANES), F32)] * 2, name="rope_tables",
        compiler_params=_cparams(("arbitrary",)))(positions3, inv_freq_row, sign_row)


def _swap_halves(t):
    lane = lax.broadcasted_iota(jnp.int32, t.shape, 1)
    half = HEAD_DIM // 2
    return jnp.where((lane % HEAD_DIM) < half, pltpu.roll(t, LANES - half, 1), pltpu.roll(t, half, 1))


def _segment_rows(r, d, s):
    n = s // d
    return (pl.ds(r, n, stride=d) if d > 1 else pl.ds(0, s)), pl.ds(r * n, n)


def _head_lanes(shape):
    lane = lax.broadcasted_iota(jnp.int32, shape, len(shape) - 1)
    return lane < HEAD_DIM


def _bmm_nt(a, b):
    return lax.dot_general(a.astype(MXU_DTYPE), b.astype(MXU_DTYPE), (((2,), (2,)), ((0,), (0,))), preferred_element_type=F32)


def _bmm(a, b):
    return lax.dot_general(a.astype(MXU_DTYPE), b.astype(MXU_DTYPE), (((2,), (1,)), ((0,), (0,))), preferred_element_type=F32)


def _bmm_tn(a, b):
    return lax.dot_general(a.astype(MXU_DTYPE), b.astype(MXU_DTYPE), (((1,), (1,)), ((0,), (0,))), preferred_element_type=F32)


def _stack_heads(t3):
    head_a = _head_lanes(t3.shape)
    zero = jnp.zeros_like(t3)
    return jnp.concatenate([jnp.where(head_a, t3, zero), jnp.where(head_a, zero, t3)], axis=1)


QUNIT = QBLK // 2


def _with_previous(t3, nprev):
    shifted = [jnp.concatenate([t3[:k], t3[:-k]], axis=0) for k in range(nprev, 0, -1)]
    return jnp.concatenate(shifted + [t3], axis=1)


def _head_columns(t3):
    return jnp.concatenate([t3[:, :, lo:lo + 1] for lo in range(0, LANES, HEAD_DIM)], axis=1)


def _to_own_unit(t, nprev):
    unit = t.shape[1] // (nprev + 1)
    out = t[:, nprev * unit:]
    for k in range(1, nprev + 1):
        part = t[:, (nprev - k) * unit:(nprev - k + 1) * unit]
        out = out + jnp.concatenate([part[k:], jnp.zeros_like(part[:k])], axis=0)
    return out


def _branch_operands(qh, kh, vh, s, nb):
    nh = LANES // HEAD_DIM
    unit, nprev = (QBLK, 0) if nb == 1 else (QUNIT, QBLK // QUNIT)
    g = s // unit
    q3 = _stack_heads(qh[...].reshape(g, unit, LANES))
    k3, v3 = kh[...].reshape(g, unit, LANES), vh[...].reshape(g, unit, LANES)
    if nprev == 0:
        qi = lax.broadcasted_iota(jnp.int32, (1, nh * unit, unit), 1) % unit
        kj = lax.broadcasted_iota(jnp.int32, (1, nh * unit, unit), 2)
        return q3, k3, v3, kj <= qi, nprev
    shape = (g, nh * unit, (nprev + 1) * unit)
    qi = lax.broadcasted_iota(jnp.int32, shape, 1) % unit
    kj = lax.broadcasted_iota(jnp.int32, shape, 2)
    j = lax.broadcasted_iota(jnp.int32, shape, 0)
    per_block = QBLK // unit
    opens = ((j // per_block) % nb) == 0
    mask = (kj >= qi) & (kj <= qi + QBLK) & ((kj >= QBLK - unit * (j % per_block)) | jnp.logical_not(opens))
    return q3, _with_previous(k3, nprev), _with_previous(v3, nprev), mask, nprev


def _attn_fwd(proj3, ctab, stab, aw):
    b, s, _ = proj3.shape
    npair = aw // LANES
    scale = HEAD_DIM ** -0.5
    nbr = len(DILATION_PAIRS)

    def body(q_ref, k_ref, v_ref, c_ref, s_ref, o_ref, l_ref, qf, kf, qh, kh, vh, op, lp, ob, lb):
        cc, ss = c_ref[...], s_ref[...]
        q2, k2 = q_ref[...], k_ref[...]
        qf[...] = (q2 * cc + _swap_halves(q2) * ss) * scale
        kf[...] = k2 * cc + _swap_halves(k2) * ss
        for br, (window, d) in enumerate(DILATION_PAIRS):
            for r in range(d):
                nat, perm = _segment_rows(r, d, s)
                for dst, src in ((qh, qf), (kh, kf), (vh, v_ref)):
                    dst[perm, :] = src[nat, :].astype(MXU_DTYPE)
            o_dst, l_dst = (ob.at[br], lb.at[br]) if d == 1 else (op, lp)
            q3, kk, vv, mask, _ = _branch_operands(qh, kh, vh, s, (s // d) // QBLK)
            unit = q3.shape[1] // (LANES // HEAD_DIM)
            head_a = _head_lanes((q3.shape[0], unit, LANES))
            sc = jnp.where(mask, _bmm_nt(q3, kk), NEG_INF)
            mx = jnp.max(sc, axis=-1, keepdims=True)
            p = jnp.exp(sc - mx)
            den = jnp.sum(p, axis=-1, keepdims=True)
            o2 = _bmm(p, vv) / den
            l2 = mx + jnp.log(den)
            o_dst[...] = jnp.where(head_a, o2[:, :unit], o2[:, unit:]).reshape(s, LANES)
            l_dst[...] = jnp.where(head_a, l2[:, :unit], l2[:, unit:]).reshape(s, LANES)
            if d > 1:
                for r in range(d):
                    nat, perm = _segment_rows(r, d, s)
                    ob[br, nat, :] = op[perm, :]
                    lb[br, nat, :] = lp[perm, :]
        ls = [lb[br] for br in range(nbr)]
        mx = functools.reduce(jnp.maximum, ls)
        ws = [jnp.exp(l - mx) for l in ls]
        tot = functools.reduce(lambda a, b_: a + b_, ws)
        o_ref[...] = functools.reduce(lambda a, b_: a + b_, [(w / tot) * ob[br] for br, w in enumerate(ws)])
        l_ref[...] = mx + jnp.log(tot)

    def col(off):
        return pl.BlockSpec((None, s, LANES), lambda bi, hp, off=off: (bi, 0, off + hp))

    tab = pl.BlockSpec((None, s, LANES), lambda bi, hp: (bi, 0, 0))
    f32s = pltpu.VMEM((s, LANES), F32)
    mxs = pltpu.VMEM((s, LANES), MXU_DTYPE)
    br_s = pltpu.VMEM((nbr, s, LANES), F32)
    return pl.pallas_call(
        body, grid=(b, npair), in_specs=[col(0), col(npair), col(2 * npair), tab, tab], out_specs=[col(0), col(0)],
        out_shape=[jax.ShapeDtypeStruct((b, s, aw), F32)] * 2,
        scratch_shapes=[f32s] * 2 + [mxs] * 3 + [f32s] * 2 + [br_s] * 2,
        name="attn_fwd", compiler_params=_cparams(("arbitrary", "arbitrary")))(proj3, proj3, proj3, ctab, stab)


def _attn_bwd(proj3, ctab, stab, dout3, out3, lse3, aw):
    b, s, _ = proj3.shape
    npair = aw // LANES
    scale = HEAD_DIM ** -0.5
    nheads = LANES // HEAD_DIM

    def body(q_ref, k_ref, v_ref, c_ref, s_ref, do_ref, o_ref, l_ref, dq_ref, dk_ref, dv_ref,
             qf, kf, dlf, qh, kh, vh, doh, lpm, dpm, dqp, dkp, dvp, dqn, dkn, dvn):
        cc, ss = c_ref[...], s_ref[...]
        q2, k2 = q_ref[...], k_ref[...]
        qf[...] = (q2 * cc + _swap_halves(q2) * ss) * scale
        kf[...] = k2 * cc + _swap_halves(k2) * ss
        dd = do_ref[...] * o_ref[...]
        in_a = _head_lanes((s, LANES))
        sum_a = jnp.sum(jnp.where(in_a, dd, 0.0), axis=-1, keepdims=True)
        sum_b = jnp.sum(jnp.where(in_a, 0.0, dd), axis=-1, keepdims=True)
        dlf[...] = jnp.where(in_a, sum_a, sum_b)
        for r_ in (dqn, dkn, dvn):
            r_[...] = jnp.zeros_like(r_)
        for window, d in DILATION_PAIRS:
            for r in range(d):
                nat, perm = _segment_rows(r, d, s)
                for dst, src in ((qh, qf), (kh, kf), (vh, v_ref), (doh, do_ref)):
                    dst[perm, :] = src[nat, :].astype(MXU_DTYPE)
                if d > 1:
                    lpm[perm, :] = l_ref[nat, :]
                    dpm[perm, :] = dlf[nat, :]
            l_src, d_src = (l_ref, dlf) if d == 1 else (lpm, dpm)
            q3, kk, vv, mask, nprev = _branch_operands(qh, kh, vh, s, (s // d) // QBLK)
            g_, unit = q3.shape[0], q3.shape[1] // nheads
            head_a = _head_lanes((g_, unit, LANES))
            do3 = _stack_heads(doh[...].reshape(g_, unit, LANES))
            lcol, dcol = _head_columns(l_src[...].reshape(g_, unit, LANES)), _head_columns(d_src[...].reshape(g_, unit, LANES))
            p = jnp.exp(jnp.where(mask, _bmm_nt(q3, kk), NEG_INF) - lcol)
            ds_ = p * (_bmm_nt(do3, vv) - dcol)
            dq2 = _bmm(ds_, kk)
            dq_new = jnp.where(head_a, dq2[:, :unit], dq2[:, unit:]).reshape(s, LANES)
            dk_new = _to_own_unit(_bmm_tn(ds_, q3), nprev).reshape(s, LANES)
            dv_new = _to_own_unit(_bmm_tn(p, do3), nprev).reshape(s, LANES)
            if d == 1:
                dqn[...] += dq_new
                dkn[...] += dk_new
                dvn[...] += dv_new
            else:
                dqp[...] = dq_new
                dkp[...] = dk_new
                dvp[...] = dv_new
                for r in range(d):
                    nat, perm = _segment_rows(r, d, s)
                    dqn[nat, :] += dqp[perm, :]
                    dkn[nat, :] += dkp[perm, :]
                    dvn[nat, :] += dvp[perm, :]
        g = dqn[...] * scale
        dq_ref[...] = g * cc + _swap_halves(g * ss)
        g = dkn[...]
        dk_ref[...] = g * cc + _swap_halves(g * ss)
        dv_ref[...] = dvn[...]

    def col(off):
        return pl.BlockSpec((None, s, LANES), lambda bi, hp, off=off: (bi, 0, off + hp))

    tab = pl.BlockSpec((None, s, LANES), lambda bi, hp: (bi, 0, 0))
    f32s = pltpu.VMEM((s, LANES), F32)
    mxs = pltpu.VMEM((s, LANES), MXU_DTYPE)
    return pl.pallas_call(
        body, grid=(b, npair), in_specs=[col(0), col(npair), col(2 * npair), tab, tab, col(0), col(0), col(0)],
        out_specs=[col(0)] * 3, out_shape=[jax.ShapeDtypeStruct((b, s, aw), F32)] * 3,
        scratch_shapes=[f32s] * 3 + [mxs] * 4 + [f32s] * 8,
        name="attn_bwd", compiler_params=_cparams(("arbitrary", "arbitrary")))(
            proj3, proj3, proj3, ctab, stab, dout3, out3, lse3)


def _s5_discretise(a_re, a_im, log_dt, bt_re, bt_im):
    dt = jnp.exp(log_dt)
    mag = jnp.exp(a_re * dt)
    ang = a_im * dt
    lb_re = mag * jnp.cos(ang)
    lb_im = mag * jnp.sin(ang)
    den = a_re * a_re + a_im * a_im
    nr = lb_re - 1.0
    ni = lb_im
    cr = (nr * a_re + ni * a_im) / den
    ci = (ni * a_re - nr * a_im) / den
    return lb_re, lb_im, cr * bt_re - ci * bt_im, cr * bt_im + ci * bt_re


def _s5_params(a_re, a_im, log_dt, bt_re, bt_im):
    def body(ar, ai, ld, br, bi, o1, o2, o3, o4):
        r = _s5_discretise(ar[...], ai[...], ld[...], br[...], bi[...])
        for o, v in zip((o1, o2, o3, o4), r, strict=True):
            o[...] = v

    sd = jax.ShapeDtypeStruct
    return pl.pallas_call(body, out_shape=[sd(a_re.shape, F32)] * 2 + [sd(bt_re.shape, F32)] * 2, name="s5_params")(
        a_re, a_im, log_dt, bt_re, bt_im)


def _s5_params_bwd(a_re, a_im, log_dt, bt_re, bt_im, cts):
    def body(ar, ai, ld, br, bi, c1, c2, c3, c4, o1, o2, o3, o4, o5):
        _, vjp = jax.vjp(_s5_discretise, ar[...], ai[...], ld[...], br[...], bi[...])
        r = vjp((c1[...], c2[...], c3[...], c4[...]))
        for o, v in zip((o1, o2, o3, o4, o5), r, strict=True):
            o[...] = v

    sd = jax.ShapeDtypeStruct
    return pl.pallas_call(
        body, out_shape=[sd(a_re.shape, F32)] * 2 + [sd(log_dt.shape, F32)] + [sd(bt_re.shape, F32)] * 2, name="s5_params_bwd")(
            a_re, a_im, log_dt, bt_re, bt_im, *cts)


S5_TC = 128


def _time_major(tiles):
    nt, tc, lanes = tiles.shape
    return jnp.swapaxes(tiles, 0, 1).reshape(tc * nt, lanes)


def _tile_major(rows, nt):
    return jnp.swapaxes(rows.astype(MXU_DTYPE).reshape(rows.shape[0] // nt, nt, rows.shape[1]), 0, 1)


def _s5_fwd(proj3, ucol, lam_re, lam_im, bre, bim, cre, cim, dvec, sw):
    b, s, _ = proj3.shape
    nt = lam_re.shape[0]
    ncb = sw // LANES
    tpc = nt // ncb
    tc = S5_TC

    def body(u_ref, lr_ref, li_ref, bre_ref, bim_ref, cre_ref, cim_ref, d_ref, y_ref, xr_ref, xi_ref, sr, si):
        @pl.when(pl.program_id(0) == 0)
        def _():
            sr[...] = jnp.zeros_like(sr)
            si[...] = jnp.zeros_like(si)

        u_all = u_ref[...].reshape(b * tc, sw)
        bur, bui = [], []
        for cb in range(ncb):
            ucb = u_all[:, cb * LANES:(cb + 1) * LANES].astype(MXU_DTYPE)
            for t in range(cb * tpc, (cb + 1) * tpc):
                bur.append(_mm(ucb, bre_ref[t]))
                bui.append(_mm(ucb, bim_ref[t]))
        for bi in range(b):
            rows = slice(bi * tc, (bi + 1) * tc)
            xr_ref[bi] = _time_major(jnp.stack([v_[rows] for v_ in bur]))
            xi_ref[bi] = _time_major(jnp.stack([v_[rows] for v_ in bui]))
        lr, li = lr_ref[...], li_ref[...]

        def step(j, carry):
            off = pl.multiple_of(j * nt, nt)
            new = []
            for bi in range(b):
                pr, pi = carry[2 * bi], carry[2 * bi + 1]
                nr = lr * pr - li * pi + xr_ref[bi, pl.ds(off, nt), :]
                ni = lr * pi + li * pr + xi_ref[bi, pl.ds(off, nt), :]
                xr_ref[bi, pl.ds(off, nt), :] = nr
                xi_ref[bi, pl.ds(off, nt), :] = ni
                new += [nr, ni]
            return tuple(new)

        init = tuple(v for bi in range(b) for v in (sr[bi], si[bi]))
        fin = lax.fori_loop(0, tc, step, init, unroll=4)
        for bi in range(b):
            sr[bi] = fin[2 * bi]
            si[bi] = fin[2 * bi + 1]
        xr_t = [_tile_major(xr_ref[bi], nt) for bi in range(b)]
        xi_t = [_tile_major(xi_ref[bi], nt) for bi in range(b)]
        for cb in range(ncb):
            cols = slice(cb * LANES, (cb + 1) * LANES)
            acc = d_ref[:, cols] * u_all[:, cols]
            for t in range(cb * tpc, (cb + 1) * tpc):
                xr_all = jnp.concatenate([xr_t[bi][t] for bi in range(b)], axis=0)
                xi_all = jnp.concatenate([xi_t[bi][t] for bi in range(b)], axis=0)
                acc = acc + (_mm(xr_all, cre_ref[t]) - _mm(xi_all, cim_ref[t]))
            for bi in range(b):
                y_ref[bi, :, cols] = acc[bi * tc:(bi + 1) * tc]

    def whole(a):
        return pl.BlockSpec(a.shape, lambda c, n=a.ndim: (0,) * n)

    xblk = pl.BlockSpec((b, tc * nt, LANES), lambda c: (0, c, 0))
    return pl.pallas_call(
        body, grid=(s // tc,),
        in_specs=[pl.BlockSpec((b, tc, sw), lambda c: (0, c, ucol))] + [whole(a) for a in (lam_re, lam_im, bre, bim, cre, cim, dvec)],
        out_specs=[pl.BlockSpec((b, tc, sw), lambda c: (0, c, 0)), xblk, xblk],
        out_shape=[jax.ShapeDtypeStruct((b, s, sw), F32)] + [jax.ShapeDtypeStruct((b, s * nt, LANES), F32)] * 2,
        scratch_shapes=[pltpu.VMEM((b, nt, LANES), F32)] * 2, name="s5_fwd",
        compiler_params=_cparams(("arbitrary",)))(proj3, lam_re, lam_im, bre, bim, cre, cim, dvec)


def _s5_bwd(proj3, ucol, dy3, xr3, xi3, lam_re, lam_im, bre, bim, cre, cim, dvec, sw):
    b, s, _ = proj3.shape
    nt = lam_re.shape[0]
    ncb = sw // LANES
    tpc = nt // ncb
    tc = S5_TC
    nchunk = s // tc

    def body(u_ref, dy_ref, xr_ref, xi_ref, pr_ref, pi_ref, lr_ref, li_ref, bre_ref, bim_ref, cre_ref, cim_ref, d_ref,
             du_ref, dbre, dbim, dcre, dcim, dlr, dli, dd, gr, gi, sr, si):
        step_id = pl.program_id(0)

        @pl.when(step_id == 0)
        def _():
            for r in (sr, si, dbre, dbim, dcre, dcim, dlr, dli, dd):
                r[...] = jnp.zeros_like(r)

        u_all = u_ref[...].reshape(b * tc, sw)
        dy_all = dy_ref[...].reshape(b * tc, sw)
        dxr, dxi = [], []
        for cb in range(ncb):
            dycb = dy_all[:, cb * LANES:(cb + 1) * LANES].astype(MXU_DTYPE)
            for t in range(cb * tpc, (cb + 1) * tpc):
                dxr.append(_mm_nt(dycb, cre_ref[t]))
                dxi.append(-_mm_nt(dycb, cim_ref[t]))
        for bi in range(b):
            rows = slice(bi * tc, (bi + 1) * tc)
            gr[bi] = _time_major(jnp.stack([v_[rows] for v_ in dxr]))
            gi[bi] = _time_major(jnp.stack([v_[rows] for v_ in dxi]))
        lr, li = lr_ref[...], li_ref[...]

        has_prev = (step_id != nchunk - 1).astype(F32)
        before = [(pr_ref[bi] * has_prev, pi_ref[bi] * has_prev) for bi in range(b)]

        def step(jj, carry):
            t = tc - 1 - jj
            off = pl.multiple_of(t * nt, nt)
            poff = pl.multiple_of(jnp.maximum(t - 1, 0) * nt, nt)
            new = []
            alr, ali = carry[-2], carry[-1]
            for bi in range(b):
                nr_, ni_ = carry[2 * bi], carry[2 * bi + 1]
                vr = gr[bi, pl.ds(off, nt), :] + lr * nr_ + li * ni_
                vi = gi[bi, pl.ds(off, nt), :] + lr * ni_ - li * nr_
                gr[bi, pl.ds(off, nt), :] = vr
                gi[bi, pl.ds(off, nt), :] = vi
                xpr = jnp.where(t > 0, xr_ref[bi, pl.ds(poff, nt), :], before[bi][0])
                xpi = jnp.where(t > 0, xi_ref[bi, pl.ds(poff, nt), :], before[bi][1])
                alr = alr + (vr * xpr + vi * xpi)
                ali = ali + (vi * xpr - vr * xpi)
                new += [vr, vi]
            return tuple(new) + (alr, ali)

        zero = jnp.zeros((nt, LANES), F32)
        init = tuple(v for bi in range(b) for v in (sr[bi], si[bi])) + (zero, zero)
        fin = lax.fori_loop(0, tc, step, init, unroll=4)
        for bi in range(b):
            sr[bi] = fin[2 * bi]
            si[bi] = fin[2 * bi + 1]
        dlr[...] += fin[-2]
        dli[...] += fin[-1]

        tiles = [[_tile_major(ref[bi], nt) for bi in range(b)] for ref in (gr, gi, xr_ref, xi_ref)]

        def stacked(k, t):
            return jnp.concatenate([tiles[k][bi][t] for bi in range(b)], axis=0)

        for cb in range(ncb):
            cols = slice(cb * LANES, (cb + 1) * LANES)
            ucb32, dycb32 = u_all[:, cols], dy_all[:, cols]
            ucb, dycb = ucb32.astype(MXU_DTYPE), dycb32.astype(MXU_DTYPE)
            acc = d_ref[:, cols] * dycb32
            for t in range(cb * tpc, (cb + 1) * tpc):
                vr, vi = stacked(0, t), stacked(1, t)
                acc = acc + (_mm_nt(vr, bre_ref[t]) + _mm_nt(vi, bim_ref[t]))
                dbre[t] += _mm_tn(ucb, vr)
                dbim[t] += _mm_tn(ucb, vi)
                dcre[t] += _mm_tn(stacked(2, t), dycb)
                dcim[t] -= _mm_tn(stacked(3, t), dycb)
            for bi in range(b):
                du_ref[bi, :, cols] = acc[bi * tc:(bi + 1) * tc]
            dd[:, cols] += _colsum(dycb32 * ucb32)

    def whole(a):
        return pl.BlockSpec(a.shape, lambda c, n=len(a.shape): (0,) * n)

    def rev(c):
        return nchunk - 1 - c

    xblk = pl.BlockSpec((b, tc * nt, LANES), lambda c: (0, rev(c), 0))
    prev = pl.BlockSpec((b, nt, LANES), lambda c: (0, jnp.maximum(rev(c) * tc - 1, 0), 0))
    sd = jax.ShapeDtypeStruct
    blk = sd(bre.shape, F32)
    acc_shapes = [blk, blk, sd(cre.shape, F32), sd(cre.shape, F32), sd(lam_re.shape, F32), sd(lam_re.shape, F32), sd((1, sw), F32)]
    return pl.pallas_call(
        body, grid=(nchunk,),
        in_specs=[pl.BlockSpec((b, tc, sw), lambda c: (0, rev(c), ucol)), pl.BlockSpec((b, tc, sw), lambda c: (0, rev(c), 0)),
                  xblk, xblk, prev, prev] + [whole(a) for a in (lam_re, lam_im, bre, bim, cre, cim, dvec)],
        out_specs=[pl.BlockSpec((b, tc, sw), lambda c: (0, rev(c), 0))] + [whole(a) for a in acc_shapes],
        out_shape=[sd((b, s, sw), F32)] + acc_shapes,
        scratch_shapes=[pltpu.VMEM((b, tc * nt, LANES), F32)] * 2 + [pltpu.VMEM((b, nt, LANES), F32)] * 2, name="s5_bwd",
        compiler_params=_cparams(("arbitrary",)))(proj3, dy3, xr3, xi3, xr3, xi3, lam_re, lam_im, bre, bim, cre, cim, dvec)


def _s5_blocks(bb, cc_, ncb):
    g, n, p = bb.shape
    mask = jnp.asarray(_s5_tile_mask(g, p, ncb))
    nt, gpt, gpc = mask.shape
    bbt, cct = bb.reshape(nt, gpt, n, p), cc_.reshape(nt, gpt, n, p)
    spread = mask[:, :, :, None, None]
    bblk = (bbt[:, :, None] * spread).transpose(0, 2, 3, 1, 4).reshape(nt, gpc * n, gpt * p)
    cblk = (cct[:, :, None] * spread).transpose(0, 1, 4, 2, 3).reshape(nt, gpt * p, gpc * n)
    return bblk, cblk


def _s5_tile_mask(g, p, ncb):
    nt, gpt, gpc = g * p // LANES, LANES // p, g // ncb
    mask = np.zeros((nt, gpt, gpc), np.float32)
    for t in range(nt):
        for gl in range(gpt):
            mask[t, gl, (t * gpt + gl) % gpc] = 1.0
    return mask


def _s5_unblock(dbblk, dcblk, g, n, p, ncb):
    mask = jnp.asarray(_s5_tile_mask(g, p, ncb))
    nt, gpt, gpc = mask.shape
    db = jnp.sum(dbblk.reshape(nt, gpc, n, gpt, p) * mask.transpose(0, 2, 1)[:, :, None, :, None], axis=1)
    dc = jnp.sum(dcblk.reshape(nt, gpt, p, gpc, n) * mask[:, :, None, :, None], axis=3)
    return db.transpose(0, 2, 1, 3).reshape(g, n, p), dc.transpose(0, 1, 3, 2).reshape(g, n, p)


def _adamw_math(w, g, m, v):
    m = ADAM_B1 * m + (1.0 - ADAM_B1) * g
    v = ADAM_B2 * v + (1.0 - ADAM_B2) * (g * g)
    m_hat = m / (1.0 - ADAM_B1 ** ADAM_STEP)
    v_hat = v / (1.0 - ADAM_B2 ** ADAM_STEP)
    delta = -ADAM_LR * (m_hat / (jnp.sqrt(v_hat) + ADAM_EPS) + ADAM_WD * w)
    return delta, m, v


def _adamw_layer(name, parts, w, m, v, layer, prev, tr):
    nparts, r, c = parts.shape
    assert r % tr == 0
    if prev is None:
        prev = [lax.empty(w.shape, F32) for _ in range(4)]

    def body(p_ref, w_ref, m_ref, v_ref, *rest):
        g_out, d_out, m_out, v_out = rest[4:]
        g = p_ref[0].astype(F32)
        for k in range(1, nparts):
            g = g + p_ref[k].astype(F32)
        delta, mn, vn = _adamw_math(w_ref[...], g, m_ref[...], v_ref[...])
        g_out[...] = g
        d_out[...] = delta
        m_out[...] = mn
        v_out[...] = vn

    blk = pl.BlockSpec((None, tr, c), lambda i: (layer, i, 0))
    kept = pl.BlockSpec(memory_space=pl.ANY)
    return pl.pallas_call(
        body, grid=(r // tr,), in_specs=[pl.BlockSpec((nparts, tr, c), lambda i: (0, i, 0)), blk, blk, blk] + [kept] * 4,
        out_specs=[blk] * 4, out_shape=[jax.ShapeDtypeStruct(w.shape, F32)] * 4, name=name,
        input_output_aliases={4 + k: k for k in range(4)},
        compiler_params=_cparams(("arbitrary",)))(parts, w, m, v, *prev)


def _piece_rows(size):
    unit = SUBLANES * LANES
    return -(-size // unit) * SUBLANES


def _pack_small(pieces, lead=0):
    rows = []
    for p in pieces:
        head = p.shape[:lead]
        flat = p.reshape(head + (-1,))
        nrow = _piece_rows(flat.shape[-1])
        flat = jnp.pad(flat, [(0, 0)] * lead + [(0, nrow * LANES - flat.shape[-1])])
        rows.append(flat.reshape(head + (nrow, LANES)))
    return jnp.concatenate(rows, axis=lead)


def _step(x, positions, weights, moments_m, moments_v, loss_target, distributed):
    f32 = F32
    bsz, seq, dm = x.shape
    depth = weights["w_in"].shape[0]
    aw = weights["attn_gain"].shape[1]
    sw = weights["ssm_gain"].shape[1]
    dff = weights["b_ff1"].shape[1]
    ng, npst = weights["ssm_a_re"].shape[1:]
    gdim = weights["ssm_d"].shape[2]
    in_w = 3 * aw + sw
    t_rows = bsz * seq
    alpha = (2.0 * depth) ** 0.25
    ncb = sw // LANES
    nt = ng * npst // LANES
    ndev = NDEV if distributed else 1
    tm = tl = 512

    gather_modes = ["gather"] * len(BIG_NAMES)
    no_token = jnp.zeros((SUBLANES, LANES), f32)

    def shards(l, names):
        return [weights[n][l].astype(MXU_DTYPE) for n in names]

    if distributed:
        first, _ = _exchange_start("weights_start_l0_in", shards(0, BIG_NAMES[:1]), gather_modes[:1])
        (g_in0,) = _exchange_wait("weights_wait_l0_in", first, gather_modes[:1], positions)
        rest0, tok_rest0 = _exchange_start("weights_start_l0_rest", shards(0, BIG_NAMES[1:]), gather_modes[1:], after=g_in0)

    half = HEAD_DIM // 2
    inv_freq = ROPE_THETA ** (-jnp.arange(half, dtype=f32) * 2.0 / HEAD_DIM)
    reps = LANES // half
    inv_row = jnp.tile(inv_freq, reps)[None, :]
    sign_row = jnp.tile(jnp.concatenate([-jnp.ones((half,), f32), jnp.ones((half,), f32)]), LANES // HEAD_DIM)[None, :]
    ctab, stab = _rope_tables(positions[..., None], inv_row, sign_row)

    def row(v):
        return v.reshape(1, -1)

    h = x.reshape(t_rows, dm)
    saved = []
    for l in range(depth):
        tok_in = tok_mix = no_token
        if not distributed:
            g_in, g_glu, g_out, g_ff1, g_ff2 = [weights[n][l].astype(MXU_DTYPE)[None] for n in BIG_NAMES]
        elif l == 0:
            g_in, tok_in = g_in0, tok_rest0
        else:
            g_in, g_glu, g_out, g_ff1, g_ff2 = _exchange_wait(f"weights_wait_l{l}", next_gather, gather_modes, h)
            if l + 1 < depth:
                next_gather, tok_in = _exchange_start(f"weights_start_l{l + 1}", shards(l + 1, BIG_NAMES), gather_modes, after=g_in)
        w_in_l = g_in

        def in_proj(rv, cr):
            return [jnp.concatenate([_mm(rv[0], cr[0][j]) for j in range(ndev)], axis=-1)], []

        (proj,), _ = _rows_call("in_proj", in_proj, [h], [w_in_l, tok_in], [(in_w, f32)], [], tl)
        proj3 = proj.reshape(bsz, seq, in_w)
        attn3, lse3 = _attn_fwd(proj3, ctab, stab, aw)
        if distributed and l == 0:
            g_glu, g_out, g_ff1, g_ff2 = _exchange_wait("weights_wait_l0_rest", rest0, gather_modes[1:], attn3)
            if depth > 1:
                next_gather, tok_mix = _exchange_start("weights_start_l1", shards(1, BIG_NAMES), gather_modes, after=g_glu)
        w_glu_l = g_glu.reshape(sw, sw)
        w_out_l = g_out.reshape(dm, dm)
        w_ff1_l = g_ff1
        w_ff2_l = g_ff2.reshape(dff, dm)

        a_re, a_im = weights["ssm_a_re"][l][:, None, :], weights["ssm_a_im"][l][:, None, :]
        log_dt = weights["ssm_log_dt"][l][:, None, None]
        bt_re = weights["ssm_b_re"][l].transpose(0, 2, 1)
        bt_im = weights["ssm_b_im"][l].transpose(0, 2, 1)
        lb_re, lb_im, bb_re, bb_im = _s5_params(a_re, a_im, log_dt, bt_re, bt_im)
        lam_re, lam_im = lb_re.reshape(nt, LANES), lb_im.reshape(nt, LANES)
        bre, cre = _s5_blocks(bb_re, weights["ssm_c_re"][l], ncb)
        bim, cim = _s5_blocks(bb_im, weights["ssm_c_im"][l], ncb)
        s5c = [a_.astype(MXU_DTYPE) for a_ in (bre, bim, cre, cim)]
        dvec = row(weights["ssm_d"][l])
        ypre3, xr3, xi3 = _s5_fwd(proj3, 3 * aw // sw, lam_re, lam_im, *s5c, dvec, sw)
        attn, ypre = attn3.reshape(t_rows, aw), ypre3.reshape(t_rows, sw)

        b_glu, ga, gs = row(weights["b_glu"][l]), row(weights["attn_gain"][l]), row(weights["ssm_gain"][l])

        def mix(rv, cr):
            at, yp = rv
            g = _gelu(yp)
            ssm = g * jax.nn.sigmoid(_mm(g, cr[0][...]) + cr[1][...])
            return [jnp.concatenate([_rms_norm(at, cr[2][...]), _rms_norm(ssm, cr[3][...])], axis=-1)], []

        (mixed,), _ = _rows_call("mix", mix, [attn, ypre], [w_glu_l, b_glu, ga, gs, tok_mix], [(dm, MXU_DTYPE)], [], tl)

        b_out, g1, b1 = row(weights["b_out"][l]), row(weights["ln1_g"][l]), row(weights["ln1_b"][l])

        def out_proj(rv, cr):
            pre = alpha * rv[1] + _mm(rv[0], cr[0][...]) + cr[1][...]
            return [pre, _layer_norm(pre, cr[2][...], cr[3][...])], []

        (pre1, h1), _ = _rows_call("out_proj", out_proj, [mixed, h], [w_out_l, b_out, g1, b1], [(dm, f32), (dm, f32)], [], tl)

        b_ff1 = row(weights["b_ff1"][l])

        def ff1(rv, cr):
            pre_act = jnp.concatenate([_mm(rv[0], cr[0][j]) for j in range(ndev)], axis=-1) + cr[1][...]
            return [jnp.square(jnp.maximum(pre_act, 0.0))], []

        (act,), _ = _rows_call("ff1", ff1, [h1], [w_ff1_l, b_ff1], [(dff, MXU_DTYPE)], [], tm)

        b_ff2, g2, b2 = row(weights["b_ff2"][l]), row(weights["ln2_g"][l]), row(weights["ln2_b"][l])

        def ff2(rv, cr):
            pre = alpha * rv[1] + _mm(rv[0], cr[0][...]) + cr[1][...]
            return [pre, _layer_norm(pre, cr[2][...], cr[3][...])], []

        (pre2, h2), _ = _rows_call("ff2", ff2, [act, h1], [w_ff2_l, b_ff2, g2, b2], [(dm, f32), (dm, f32)], [], tm)

        saved.append(dict(h=h, proj3=proj3, attn=attn, lse3=lse3, attn3=attn3, ypre=ypre, xr3=xr3, xi3=xi3, mixed=mixed, pre1=pre1,
                          h1=h1, act=act, pre2=pre2, w_in=w_in_l, w_glu=w_glu_l, w_out=w_out_l, w_ff1=w_ff1_l, w_ff2=w_ff2_l,
                          lam=(lam_re, lam_im), s5c=s5c, dvec=dvec, b_glu=b_glu, ga=ga, gs=gs, g1=g1, g2=g2,
                          s5in=(a_re, a_im, log_dt, bt_re, bt_im)))
        h = h2

    g2_last = saved[-1]["g2"]

    def loss_fn(rv, cr):
        y, tgt, pre = rv
        err = y - tgt
        part = 0.5 * jnp.sum(jnp.mean(err * err, axis=-1, keepdims=True), axis=0, keepdims=True)
        dpre, dg, db = _layer_norm_bwd(err * (1.0 / dm), pre, cr[0][...])
        return [dpre], [jnp.broadcast_to(part, (1, LANES)), dg, db, _colsum(dpre)]

    (dpre2,), (loss_acc, dg2, db2, dbff2) = _rows_call(
        "loss", loss_fn, [h, loss_target.reshape(t_rows, dm), saved[-1]["pre2"]], [g2_last], [(dm, f32)],
        [(1, LANES), (1, dm), (1, dm), (1, dm)], tl)
    loss = loss_acc[0, 0]
    if distributed:
        loss = lax.psum(loss, MESH_AXES)

    big_parts = {n: [None] * depth for n in BIG_NAMES}
    small_parts = [None] * depth
    grad_handles = [None] * depth
    grad_modes = ["scatter"] * len(BIG_NAMES) + ["gather"]
    grad_x = None
    for l in reversed(range(depth)):
        sv = saved[l]

        def ff2_bwd(rv, cr):
            r = rv[1].astype(F32)
            relu2 = jnp.where(r > 0.0, (2.0 * r) * lax.rsqrt(r), 0.0)
            da = _mm_nt(rv[0], cr[0][...]) * relu2
            return [da], [_colsum(da)]

        (dact,), (dbff1,) = _rows_call("ff2_bwd", ff2_bwd, [dpre2, sv["act"]], [sv["w_ff2"]], [(dff, MXU_DTYPE)], [(1, dff)], tm)
        big_parts["w_ff2"][l] = _wgrad_call("wgrad_ff2", sv["act"], dpre2, "x", ndev, ndev,
                                            (dff // ndev, dm), tm)
        big_parts["w_ff1"][l] = _wgrad_call("wgrad_ff1", sv["h1"], dact, "y", ndev, ndev,
                                            (dm, dff // ndev), tm)

        def ff1_bwd(rv, cr):
            dacc = alpha * rv[1]
            wpb = dff // ndev
            for j in range(ndev):
                dacc = dacc + _mm_nt(rv[0][:, j * wpb:(j + 1) * wpb], cr[0][j])
            dpre, dg, db = _layer_norm_bwd(dacc, rv[2], cr[1][...])
            return [dpre], [dg, db, _colsum(dpre)]

        (dpre1,), (dg1, db1, dbout) = _rows_call("ff1_bwd", ff1_bwd, [dact, dpre2, sv["pre1"]], [sv["w_ff1"], sv["g1"]],
                                                 [(dm, f32)], [(1, dm)] * 3, tm)
        big_parts["w_out"][l] = _wgrad_call("wgrad_out", sv["mixed"], dpre1, "x", ndev, ndev, (dm // ndev, dm), tl)

        def mix_bwd(rv, cr):
            dp, at, yp = rv
            w_out_r, w_glu_r, bg, ga_, gs_ = cr
            dmixed = _mm_nt(dp, w_out_r[...])
            g = _gelu(yp)
            sig = jax.nn.sigmoid(_mm(g, w_glu_r[...]) + bg[...])
            ssm = g * sig
            dat, dga = _rms_norm_bwd(dmixed[:, :aw], at, ga_[...])
            dssm, dgs = _rms_norm_bwd(dmixed[:, aw:], ssm, gs_[...])
            dz = dssm * g * sig * (1.0 - sig)
            dg = dssm * sig + _mm_nt(dz, w_glu_r[...])
            return [dat, dg * _gelu_grad(yp), dz], [dga, dgs, _colsum(dz)]

        (dattn, dypre, dz), (dga, dgs, dbglu) = _rows_call(
            "mix_bwd", mix_bwd, [dpre1, sv["attn"], sv["ypre"]], [sv["w_out"], sv["w_glu"], sv["b_glu"], sv["ga"], sv["gs"]],
            [(aw, f32), (sw, f32), (sw, MXU_DTYPE)], [(1, aw), (1, sw), (1, sw)], tl)
        big_parts["w_glu"][l] = _wgrad_call("wgrad_glu", sv["ypre"], dz, "x", 1, 1, (sw, sw), tl, prologue=_gelu).reshape(
            ndev, sw // ndev, sw)

        du3, dbre, dbim, dcre, dcim, dlr, dli, dd = _s5_bwd(
            sv["proj3"], 3 * aw // sw, dypre.reshape(bsz, seq, sw), sv["xr3"], sv["xi3"], *sv["lam"], *sv["s5c"], sv["dvec"], sw)
        dbb_re, dc_re = _s5_unblock(dbre, dcre, ng, gdim, npst, ncb)
        dbb_im, dc_im = _s5_unblock(dbim, dcim, ng, gdim, npst, ncb)
        da_re, da_im, dldt, dbt_re, dbt_im = _s5_params_bwd(
            *sv["s5in"], (dlr.reshape(ng, 1, npst), dli.reshape(ng, 1, npst), dbb_re, dbb_im))

        dq3, dk3, dv3 = _attn_bwd(sv["proj3"], ctab, stab, dattn.reshape(bsz, seq, aw), sv["attn3"], sv["lse3"], aw)
        dproj = [dq3.reshape(t_rows, aw), dk3.reshape(t_rows, aw), dv3.reshape(t_rows, aw), du3.reshape(t_rows, sw)]
        big_parts["w_in"][l] = _wgrad_call("wgrad_in", sv["h"], dproj, "y", ndev, ndev, (dm, in_w // ndev), tl)

        small_parts[l] = dict(
            attn_gain=dga, ssm_gain=dgs, ssm_a_re=da_re[:, 0], ssm_a_im=da_im[:, 0], ssm_log_dt=dldt[:, 0, 0], ssm_b_re=dbt_re.transpose(0, 2, 1),
            ssm_b_im=dbt_im.transpose(0, 2, 1), ssm_c_re=dc_re, ssm_c_im=dc_im, ssm_d=dd, b_glu=dbglu, b_out=dbout, ln1_g=dg1,
            ln1_b=db1, b_ff1=dbff1, b_ff2=dbff2, ln2_g=dg2, ln2_b=db2)

        layer_grads = [big_parts[n][l] for n in BIG_NAMES] + [_pack_small([small_parts[l][n] for n in SMALL_NAMES])]
        if distributed:
            grad_handles[l], token = _exchange_start(f"grads_start_l{l}", layer_grads, grad_modes)
        else:
            grad_handles[l], token = layer_grads, jnp.zeros((SUBLANES, LANES), f32)

        wpb = in_w // ndev
        if l > 0:
            prev = saved[l - 1]

            def in_bwd(rv, cr):
                dacc = alpha * rv[4]
                dp = jnp.concatenate([v_.astype(MXU_DTYPE) for v_ in rv[:4]], axis=-1)
                for j in range(ndev):
                    dacc = dacc + _mm_nt(dp[:, j * wpb:(j + 1) * wpb], cr[0][j])
                dpre, dg, db = _layer_norm_bwd(dacc, rv[5], cr[1][...])
                return [dpre], [dg, db, _colsum(dpre)]

            (dpre2,), (dg2, db2, dbff2) = _rows_call("in_bwd", in_bwd, dproj + [dpre1, prev["pre2"]],
                                                     [sv["w_in"], prev["g2"], token], [(dm, f32)], [(1, dm)] * 3, tl)
        else:
            def in_bwd0(rv, cr):
                dacc = alpha * rv[4]
                dp = jnp.concatenate([v_.astype(MXU_DTYPE) for v_ in rv[:4]], axis=-1)
                for j in range(ndev):
                    dacc = dacc + _mm_nt(dp[:, j * wpb:(j + 1) * wpb], cr[0][j])
                return [dacc], []

            (grad_x,), _ = _rows_call("in_bwd0", in_bwd0, dproj + [dpre1], [sv["w_in"], token], [(dm, f32)], [], tl)

    small_shapes = [weights[n].shape[1:] for n in SMALL_NAMES]
    outs = {n: None for n in BIG_NAMES}
    packed = [_pack_small([d[n] for n in SMALL_NAMES], lead=1) for d in (weights, moments_m, moments_v)]
    small_out = None
    after = grad_x
    for l in reversed(range(depth)):
        if distributed:
            recv = _exchange_wait(f"grads_wait_l{l}", grad_handles[l], grad_modes, after)
        else:
            recv = [g_[None] if i == len(BIG_NAMES) else g_ for i, g_ in enumerate(grad_handles[l])]
        for n, parts in zip(BIG_NAMES, recv[:-1], strict=True):
            outs[n] = _adamw_layer("adamw_" + n, parts, weights[n], moments_m[n], moments_v[n], l, outs[n],
                                   min(parts.shape[1], 256))
        small_out = _adamw_layer("adamw_small", recv[-1], *packed, l, small_out, recv[-1].shape[1])
        after = small_out[0]
    for k in range(4):
        row = 0
        for n, shp in zip(SMALL_NAMES, small_shapes, strict=True):
            sz = int(np.prod(shp))
            nrow = _piece_rows(sz)
            piece = small_out[k][:, row:row + nrow].reshape(depth, nrow * LANES)
            outs.setdefault(n, [None] * 4)
            outs[n][k] = piece[:, :sz].reshape((depth,) + tuple(shp))
            row += nrow

    result = [loss, grad_x.reshape(bsz, seq, dm)]
    for k in range(4):
        result += [outs[n][k] for n in WEIGHT_ORDER]
    return tuple(result)


def kernel(x, positions, w_in, attn_gain, ssm_gain, ssm_a_re, ssm_a_im, ssm_log_dt, ssm_b_re, ssm_b_im, ssm_c_re, ssm_c_im, ssm_d, w_glu, b_glu, w_out, b_out, ln1_g, ln1_b, w_ff1, b_ff1, w_ff2, b_ff2, ln2_g, ln2_b, loss_target, m_w_in, m_attn_gain, m_ssm_gain, m_ssm_a_re, m_ssm_a_im, m_ssm_log_dt, m_ssm_b_re, m_ssm_b_im, m_ssm_c_re, m_ssm_c_im, m_ssm_d, m_w_glu, m_b_glu, m_w_out, m_b_out, m_ln1_g, m_ln1_b, m_w_ff1, m_b_ff1, m_w_ff2, m_b_ff2, m_ln2_g, m_ln2_b, v_w_in, v_attn_gain, v_ssm_gain, v_ssm_a_re, v_ssm_a_im, v_ssm_log_dt, v_ssm_b_re, v_ssm_b_im, v_ssm_c_re, v_ssm_c_im, v_ssm_d, v_w_glu, v_b_glu, v_w_out, v_b_out, v_ln1_g, v_ln1_b, v_w_ff1, v_b_ff1, v_w_ff2, v_b_ff2, v_ln2_g, v_ln2_b):
    loc = locals()
    weights = {n: loc[n] for n in WEIGHT_ORDER}
    moments_m = {n: loc["m_" + n] for n in WEIGHT_ORDER}
    moments_v = {n: loc["v_" + n] for n in WEIGHT_ORDER}
    return _step(x, positions, weights, moments_m, moments_v, loss_target, distributed=True)
```

```python
import functools
import math

import jax
import jax.numpy as jnp
import numpy as np
from jax import lax
from jax.experimental import pallas as pl
from jax.experimental.pallas import tpu as pltpu

F32 = jnp.float32
MXU_DTYPE = jnp.bfloat16

HEAD_DIM = 64
DILATION_PAIRS = ((128, 1), (512, 4), (2048, 16))
ROPE_THETA = 10000.0
LN_EPS = 1e-5
RMS_EPS = 1e-6
NEG_INF = -1e30
ADAM_LR, ADAM_B1, ADAM_B2, ADAM_EPS, ADAM_WD, ADAM_STEP = 0.001, 0.9, 0.999, 1e-08, 0.01, 10

LANES = 128
SUBLANES = 8
QBLK = 128
VMEM_LIMIT = 56 * 2**20
MESH_AXES = ("x", "y", "c")
NDEV = 8

SMALL_NAMES = ("attn_gain", "ssm_gain", "ssm_a_re", "ssm_a_im", "ssm_log_dt", "ssm_b_re", "ssm_b_im", "ssm_c_re",
               "ssm_c_im", "ssm_d", "b_glu", "b_out", "ln1_g", "ln1_b", "b_ff1", "b_ff2", "ln2_g", "ln2_b")
BIG_NAMES = ("w_in", "w_glu", "w_out", "w_ff1", "w_ff2")
WEIGHT_ORDER = ("w_in", "attn_gain", "ssm_gain", "ssm_a_re", "ssm_a_im", "ssm_log_dt", "ssm_b_re", "ssm_b_im", "ssm_c_re",
                "ssm_c_im", "ssm_d", "w_glu", "b_glu", "w_out", "b_out", "ln1_g", "ln1_b", "w_ff1", "b_ff1", "w_ff2",
                "b_ff2", "ln2_g", "ln2_b")


def _cparams(sem=None):
    return pltpu.CompilerParams(dimension_semantics=sem, vmem_limit_bytes=VMEM_LIMIT)


def _mm(a, b):
    return jnp.dot(a.astype(MXU_DTYPE), b.astype(MXU_DTYPE), preferred_element_type=F32)


def _mm_nt(a, b):
    return lax.dot_general(a.astype(MXU_DTYPE), b.astype(MXU_DTYPE), (((1,), (1,)), ((), ())), preferred_element_type=F32)


def _mm_tn(a, b):
    return lax.dot_general(a.astype(MXU_DTYPE), b.astype(MXU_DTYPE), (((0,), (0,)), ((), ())), preferred_element_type=F32)


def _colsum(x):
    return jnp.sum(x, axis=0, keepdims=True)


def _layer_norm(x, g, b):
    mu = jnp.mean(x, axis=-1, keepdims=True)
    xc = x - mu
    var = jnp.mean(xc * xc, axis=-1, keepdims=True)
    return xc * lax.rsqrt(var + LN_EPS) * g + b


def _layer_norm_bwd(dy, pre, g):
    mu = jnp.mean(pre, axis=-1, keepdims=True)
    xc = pre - mu
    var = jnp.mean(xc * xc, axis=-1, keepdims=True)
    r = lax.rsqrt(var + LN_EPS)
    xhat = xc * r
    dyg = dy * g
    dpre = r * (dyg - jnp.mean(dyg, axis=-1, keepdims=True) - xhat * jnp.mean(dyg * xhat, axis=-1, keepdims=True))
    return dpre, _colsum(dy * xhat), _colsum(dy)


def _rms_norm(x, g):
    return x * lax.rsqrt(jnp.mean(x * x, axis=-1, keepdims=True) + RMS_EPS) * g


def _rms_norm_bwd(dy, x, g):
    r = lax.rsqrt(jnp.mean(x * x, axis=-1, keepdims=True) + RMS_EPS)
    dyg = dy * g
    dx = dyg * r - x * (r * r * r) * jnp.mean(dyg * x, axis=-1, keepdims=True)
    return dx, _colsum(dy * x * r)


_GELU_C = math.sqrt(2.0 / math.pi)


def _gelu(x):
    return 0.5 * x * (1.0 + jnp.tanh(_GELU_C * (x + 0.044715 * (x * x * x))))


def _gelu_grad(x):
    t = jnp.tanh(_GELU_C * (x + 0.044715 * (x * x * x)))
    return 0.5 * (1.0 + t) + 0.5 * x * (1.0 - t * t) * (_GELU_C * (1.0 + 3.0 * 0.044715 * x * x))


def _rows_call(name, fn, rows, consts, out_rows, out_accs, tm):
    rows = [r if isinstance(r, tuple) else (r, r.shape[1], 0) for r in rows]
    m = rows[0][0].shape[0]
    assert m % tm == 0
    nr, nc, no, na = len(rows), len(consts), len(out_rows), len(out_accs)

    def body(*refs):
        rr, cr = refs[:nr], refs[nr:nr + nc]
        orr, ar = refs[nr + nc:nr + nc + no], refs[nr + nc + no:]
        outs, accs = fn([r[...] for r in rr], cr)
        for o, v in zip(orr, outs, strict=True):
            o[...] = v.astype(o.dtype)
        if na:
            first = pl.program_id(0) == 0

            @pl.when(first)
            def _():
                for a, v in zip(ar, accs, strict=True):
                    a[...] = v

            @pl.when(jnp.logical_not(first))
            def _():
                for a, v in zip(ar, accs, strict=True):
                    a[...] += v

    def whole(shape):
        return pl.BlockSpec(shape, lambda i, n=len(shape): (0,) * n)

    in_specs = [pl.BlockSpec((tm, w), lambda i, cb=cb: (i, cb)) for _, w, cb in rows] + [whole(c.shape) for c in consts]
    out_specs = [pl.BlockSpec((tm, w), lambda i: (i, 0)) for w, _ in out_rows] + [whole(s) for s in out_accs]
    out_shape = [jax.ShapeDtypeStruct((m, w), dt) for w, dt in out_rows] + [jax.ShapeDtypeStruct(s, F32) for s in out_accs]
    res = pl.pallas_call(body, grid=(m // tm,), in_specs=in_specs, out_specs=out_specs, out_shape=out_shape, name=name,
                         compiler_params=_cparams(("arbitrary",)))(*[r[0] for r in rows], *consts)
    return res[:no], res[no:]


def _wgrad_call(name, x, dy, split, nblk, jb, blk_shape, tm, prologue=None):
    dys = list(dy) if isinstance(dy, (list, tuple)) else [dy]
    m = x.shape[0]
    kk, nn = blk_shape
    assert m % tm == 0 and nblk % jb == 0 and (len(dys) == 1 or (split == "y" and jb == nblk))
    xw = kk * jb if split == "x" else x.shape[1]
    yws = [d.shape[1] for d in dys] if (split == "x" or len(dys) > 1) else [nn * jb]
    nrow = m // tm

    def body(x_ref, *rest):
        dy_refs, o_ref, acc_ref = rest[:len(dys)], rest[-2], rest[-1]
        i = pl.program_id(1)

        @pl.when(i == 0)
        def _():
            acc_ref[...] = jnp.zeros_like(acc_ref)

        xv = x_ref[...]
        if prologue is not None:
            xv = prologue(xv)
        xv = xv.astype(MXU_DTYPE)
        dv = [r[...].astype(MXU_DTYPE) for r in dy_refs]
        dv = dv[0] if len(dv) == 1 else jnp.concatenate(dv, axis=-1)
        for j in range(jb):
            xa = xv[:, j * kk:(j + 1) * kk] if split == "x" else xv
            da = dv[:, j * nn:(j + 1) * nn] if split == "y" else dv
            acc_ref[j] += _mm_tn(xa, da)

        @pl.when(i == nrow - 1)
        def _():
            o_ref[...] = acc_ref[...].astype(o_ref.dtype)

    in_specs = [pl.BlockSpec((tm, xw), (lambda j, i: (i, j)) if split == "x" else (lambda j, i: (i, 0)))]
    in_specs += [pl.BlockSpec((tm, yw), (lambda j, i: (i, j)) if (split == "y" and len(dys) == 1) else (lambda j, i: (i, 0)))
                 for yw in yws]
    return pl.pallas_call(
        body, grid=(nblk // jb, nrow), in_specs=in_specs,
        out_specs=pl.BlockSpec((jb, kk, nn), lambda j, i: (j, 0, 0)),
        out_shape=jax.ShapeDtypeStruct((nblk, kk, nn), MXU_DTYPE),
        scratch_shapes=[pltpu.VMEM((jb, kk, nn), F32)], name=name,
        compiler_params=_cparams(("arbitrary", "arbitrary")))(x, *dys)


_HBM_SPEC = pl.BlockSpec(memory_space=pltpu.HBM)
_SEM_SPEC = pl.BlockSpec(memory_space=pltpu.SEMAPHORE)
_EFFECT = pltpu.SideEffectType.DATAFLOW_SIDE_EFFECTING


def _my_index():
    return 4 * lax.axis_index("x") + 2 * lax.axis_index("y") + lax.axis_index("c")


def _peer_copies(ins, lands, send_sems, recv_sems, modes):
    x, y, c = lax.axis_index("x"), lax.axis_index("y"), lax.axis_index("c")
    me = 4 * x + 2 * y + c
    pairs = []
    for k in range(NDEV - 1):
        fx, fy, fc = ((k + 1) >> 2) & 1, ((k + 1) >> 1) & 1, (k + 1) & 1
        px, py, pc = (x + fx) % 2, (y + fy) % 2, (c + fc) % 2
        idx = 4 * px + 2 * py + pc
        for a, md in enumerate(modes):
            src = ins[a] if md == "gather" else ins[a].at[idx]
            sem = a * (NDEV - 1) + k
            common = dict(src_ref=src, send_sem=send_sems.at[sem], recv_sem=recv_sems.at[sem], device_id=(px, py, pc),
                          device_id_type=pl.DeviceIdType.MESH)
            pairs.append((pltpu.make_async_remote_copy(dst_ref=lands[a].at[me], **common),
                          pltpu.make_async_remote_copy(dst_ref=lands[a].at[idx], **common)))
    return pairs


def _exchange_start(name, arrays, modes, after=None):
    n = len(arrays)
    extra = [] if after is None else [after]
    me = _my_index()
    lands = []
    for a, md in zip(arrays, modes, strict=True):
        piece = a if md == "gather" else lax.dynamic_index_in_dim(a, me, 0, keepdims=False)
        lands.append(lax.dynamic_update_index_in_dim(lax.empty((NDEV,) + piece.shape, a.dtype), piece, me, 0))

    def body(*refs):
        ins, lnd = refs[:n], refs[n:2 * n]
        send_sems, recv_sems = refs[2 * n + len(extra)], refs[2 * n + len(extra) + 1]
        token = refs[-1]
        for out_copy, _ in _peer_copies(ins, lnd, send_sems, recv_sems, modes):
            out_copy.start()
        token[...] = jnp.zeros_like(token)

    sems = pltpu.SemaphoreType.DMA((n * (NDEV - 1),))
    thru = [pltpu.HBM(a.shape, a.dtype) for a in list(arrays) + lands]
    res = pl.pallas_call(
        body, name=name, out_shape=(sems, sems, *thru, jax.ShapeDtypeStruct((SUBLANES, LANES), F32)),
        in_specs=[_HBM_SPEC] * (2 * n) + [pl.BlockSpec(memory_space=pl.ANY)] * len(extra),
        out_specs=(_SEM_SPEC, _SEM_SPEC, *([_HBM_SPEC] * (2 * n)), pl.BlockSpec(memory_space=pltpu.VMEM)),
        input_output_aliases={i: 2 + i for i in range(2 * n)},
        compiler_params=pltpu.CompilerParams(has_side_effects=_EFFECT),
    )(*[pltpu.with_memory_space_constraint(a, pltpu.HBM) for a in list(arrays) + lands], *extra)
    return (res[0], res[1], res[2:2 + n], res[2 + n:2 + 2 * n]), res[-1]


def _exchange_wait(name, handle, modes, after):
    send_sems, recv_sems, ins_thru, lands_thru = handle
    n = len(ins_thru)

    def body(*refs):
        ins, lnd = refs[:n], refs[n:2 * n]
        for out_copy, arrival in _peer_copies(ins, lnd, refs[2 * n], refs[2 * n + 1], modes):
            out_copy.wait_send()
            arrival.wait_recv()

    thru = [pltpu.HBM(a.shape, a.dtype) for a in list(ins_thru) + list(lands_thru)]
    res = pl.pallas_call(
        body, name=name, out_shape=tuple(thru),
        in_specs=[_HBM_SPEC] * (2 * n) + [_SEM_SPEC, _SEM_SPEC, pl.BlockSpec(memory_space=pl.ANY)],
        out_specs=tuple([_HBM_SPEC] * (2 * n)), input_output_aliases={i: i for i in range(2 * n)},
        compiler_params=pltpu.CompilerParams(has_side_effects=_EFFECT),
    )(*ins_thru, *lands_thru, send_sems, recv_sems, after)
    return res[n:]


def _rope_tables(positions3, inv_freq_row, sign_row):
    b, s, _ = positions3.shape

    def body(pos_ref, f_ref, sg_ref, c_ref, s_ref):
        ang = pos_ref[...].astype(F32) * f_ref[...]
        c_ref[...] = jnp.cos(ang)
        s_ref[...] = jnp.sin(ang) * sg_ref[...]

    row = pl.BlockSpec((1, LANES), lambda i: (0, 0))
    blk = pl.BlockSpec((None, s, LANES), lambda i: (i, 0, 0))
    return pl.pallas_call(
        body, grid=(b,), in_specs=[pl.BlockSpec((None, s, 1), lambda i: (i, 0, 0)), row, row], out_specs=[blk, blk],
        out_shape=[jax.ShapeDtypeStruct((b, s, LANES), F32)] * 2, name="rope_tables",
        compiler_params=_cparams(("arbitrary",)))(positions3, inv_freq_row, sign_row)


def _swap_halves(t):
    lane = lax.broadcasted_iota(jnp.int32, t.shape, 1)
    half = HEAD_DIM // 2
    return jnp.where((lane % HEAD_DIM) < half, pltpu.roll(t, LANES - half, 1), pltpu.roll(t, half, 1))


def _segment_rows(r, d, s):
    n = s // d
    return (pl.ds(r, n, stride=d) if d > 1 else pl.ds(0, s)), pl.ds(r * n, n)


def _head_lanes(shape):
    lane = lax.broadcasted_iota(jnp.int32, shape, len(shape) - 1)
    return lane < HEAD_DIM


def _bmm_nt(a, b):
    return lax.dot_general(a.astype(MXU_DTYPE), b.astype(MXU_DTYPE), (((2,), (2,)), ((0,), (0,))), preferred_element_type=F32)


def _bmm(a, b):
    return lax.dot_general(a.astype(MXU_DTYPE), b.astype(MXU_DTYPE), (((2,), (1,)), ((0,), (0,))), preferred_element_type=F32)


def _bmm_tn(a, b):
    return lax.dot_general(a.astype(MXU_DTYPE), b.astype(MXU_DTYPE), (((1,), (1,)), ((0,), (0,))), preferred_element_type=F32)


def _stack_heads(t3):
    head_a = _head_lanes(t3.shape)
    zero = jnp.zeros_like(t3)
    return jnp.concatenate([jnp.where(head_a, t3, zero), jnp.where(head_a, zero, t3)], axis=1)


QUNIT = QBLK // 2


def _with_previous(t3, nprev):
    shifted = [jnp.concatenate([t3[:k], t3[:-k]], axis=0) for k in range(nprev, 0, -1)]
    return jnp.concatenate(shifted + [t3], axis=1)


def _head_columns(t3):
    return jnp.concatenate([t3[:, :, lo:lo + 1] for lo in range(0, LANES, HEAD_DIM)], axis=1)


def _to_own_unit(t, nprev):
    unit = t.shape[1] // (nprev + 1)
    out = t[:, nprev * unit:]
    for k in range(1, nprev + 1):
        part = t[:, (nprev - k) * unit:(nprev - k + 1) * unit]
        out = out + jnp.concatenate([part[k:], jnp.zeros_like(part[:k])], axis=0)
    return out


def _branch_operands(qh, kh, vh, s, nb):
    nh = LANES // HEAD_DIM
    unit, nprev = (QBLK, 0) if nb == 1 else (QUNIT, QBLK // QUNIT)
    g = s // unit
    q3 = _stack_heads(qh[...].reshape(g, unit, LANES))
    k3, v3 = kh[...].reshape(g, unit, LANES), vh[...].reshape(g, unit, LANES)
    if nprev == 0:
        qi = lax.broadcasted_iota(jnp.int32, (1, nh * unit, unit), 1) % unit
        kj = lax.broadcasted_iota(jnp.int32, (1, nh * unit, unit), 2)
        return q3, k3, v3, kj <= qi, nprev
    shape = (g, nh * unit, (nprev + 1) * unit)
    qi = lax.broadcasted_iota(jnp.int32, shape, 1) % unit
    kj = lax.broadcasted_iota(jnp.int32, shape, 2)
    j = lax.broadcasted_iota(jnp.int32, shape, 0)
    per_block = QBLK // unit
    opens = ((j // per_block) % nb) == 0
    mask = (kj >= qi) & (kj <= qi + QBLK) & ((kj >= QBLK - unit * (j % per_block)) | jnp.logical_not(opens))
    return q3, _with_previous(k3, nprev), _with_previous(v3, nprev), mask, nprev


def _attn_fwd(proj3, ctab, stab, aw):
    b, s, _ = proj3.shape
    npair = aw // LANES
    scale = HEAD_DIM ** -0.5
    nbr = len(DILATION_PAIRS)

    def body(q_ref, k_ref, v_ref, c_ref, s_ref, o_ref, l_ref, qf, kf, qh, kh, vh, op, lp, ob, lb):
        cc, ss = c_ref[...], s_ref[...]
        q2, k2 = q_ref[...], k_ref[...]
        qf[...] = (q2 * cc + _swap_halves(q2) * ss) * scale
        kf[...] = k2 * cc + _swap_halves(k2) * ss
        for br, (window, d) in enumerate(DILATION_PAIRS):
            for r in range(d):
                nat, perm = _segment_rows(r, d, s)
                for dst, src in ((qh, qf), (kh, kf), (vh, v_ref)):
                    dst[perm, :] = src[nat, :].astype(MXU_DTYPE)
            o_dst, l_dst = (ob.at[br], lb.at[br]) if d == 1 else (op, lp)
            q3, kk, vv, mask, _ = _branch_operands(qh, kh, vh, s, (s // d) // QBLK)
            unit = q3.shape[1] // (LANES // HEAD_DIM)
            head_a = _head_lanes((q3.shape[0], unit, LANES))
            sc = jnp.where(mask, _bmm_nt(q3, kk), NEG_INF)
            mx = jnp.max(sc, axis=-1, keepdims=True)
            p = jnp.exp(sc - mx)
            den = jnp.sum(p, axis=-1, keepdims=True)
            o2 = _bmm(p, vv) / den
            l2 = mx + jnp.log(den)
            o_dst[...] = jnp.where(head_a, o2[:, :unit], o2[:, unit:]).reshape(s, LANES)
            l_dst[...] = jnp.where(head_a, l2[:, :unit], l2[:, unit:]).reshape(s, LANES)
            if d > 1:
                for r in range(d):
                    nat, perm = _segment_rows(r, d, s)
                    ob[br, nat, :] = op[perm, :]
                    lb[br, nat, :] = lp[perm, :]
        ls = [lb[br] for br in range(nbr)]
        mx = functools.reduce(jnp.maximum, ls)
        ws = [jnp.exp(l - mx) for l in ls]
        tot = functools.reduce(lambda a, b_: a + b_, ws)
        o_ref[...] = functools.reduce(lambda a, b_: a + b_, [(w / tot) * ob[br] for br, w in enumerate(ws)])
        l_ref[...] = mx + jnp.log(tot)

    def col(off):
        return pl.BlockSpec((None, s, LANES), lambda bi, hp, off=off: (bi, 0, off + hp))

    tab = pl.BlockSpec((None, s, LANES), lambda bi, hp: (bi, 0, 0))
    f32s = pltpu.VMEM((s, LANES), F32)
    mxs = pltpu.VMEM((s, LANES), MXU_DTYPE)
    br_s = pltpu.VMEM((nbr, s, LANES), F32)
    return pl.pallas_call(
        body, grid=(b, npair), in_specs=[col(0), col(npair), col(2 * npair), tab, tab], out_specs=[col(0), col(0)],
        out_shape=[jax.ShapeDtypeStruct((b, s, aw), F32)] * 2,
        scratch_shapes=[f32s] * 2 + [mxs] * 3 + [f32s] * 2 + [br_s] * 2,
        name="attn_fwd", compiler_params=_cparams(("arbitrary", "arbitrary")))(proj3, proj3, proj3, ctab, stab)


def _attn_bwd(proj3, ctab, stab, dout3, out3, lse3, aw):
    b, s, _ = proj3.shape
    npair = aw // LANES
    scale = HEAD_DIM ** -0.5
    nheads = LANES // HEAD_DIM

    def body(q_ref, k_ref, v_ref, c_ref, s_ref, do_ref, o_ref, l_ref, dq_ref, dk_ref, dv_ref,
             qf, kf, dlf, qh, kh, vh, doh, lpm, dpm, dqp, dkp, dvp, dqn, dkn, dvn):
        cc, ss = c_ref[...], s_ref[...]
        q2, k2 = q_ref[...], k_ref[...]
        qf[...] = (q2 * cc + _swap_halves(q2) * ss) * scale
        kf[...] = k2 * cc + _swap_halves(k2) * ss
        dd = do_ref[...] * o_ref[...]
        in_a = _head_lanes((s, LANES))
        sum_a = jnp.sum(jnp.where(in_a, dd, 0.0), axis=-1, keepdims=True)
        sum_b = jnp.sum(jnp.where(in_a, 0.0, dd), axis=-1, keepdims=True)
        dlf[...] = jnp.where(in_a, sum_a, sum_b)
        for r_ in (dqn, dkn, dvn):
            r_[...] = jnp.zeros_like(r_)
        for window, d in DILATION_PAIRS:
            for r in range(d):
                nat, perm = _segment_rows(r, d, s)
                for dst, src in ((qh, qf), (kh, kf), (vh, v_ref), (doh, do_ref)):
                    dst[perm, :] = src[nat, :].astype(MXU_DTYPE)
                if d > 1:
                    lpm[perm, :] = l_ref[nat, :]
                    dpm[perm, :] = dlf[nat, :]
            l_src, d_src = (l_ref, dlf) if d == 1 else (lpm, dpm)
            q3, kk, vv, mask, nprev = _branch_operands(qh, kh, vh, s, (s // d) // QBLK)
            g_, unit = q3.shape[0], q3.shape[1] // nheads
            head_a = _head_lanes((g_, unit, LANES))
            do3 = _stack_heads(doh[...].reshape(g_, unit, LANES))
            lcol, dcol = _head_columns(l_src[...].reshape(g_, unit, LANES)), _head_columns(d_src[...].reshape(g_, unit, LANES))
            p = jnp.exp(jnp.where(mask, _bmm_nt(q3, kk), NEG_INF) - lcol)
            ds_ = p * (_bmm_nt(do3, vv) - dcol)
            dq2 = _bmm(ds_, kk)
            dq_new = jnp.where(head_a, dq2[:, :unit], dq2[:, unit:]).reshape(s, LANES)
            dk_new = _to_own_unit(_bmm_tn(ds_, q3), nprev).reshape(s, LANES)
            dv_new = _to_own_unit(_bmm_tn(p, do3), nprev).reshape(s, LANES)
            if d == 1:
                dqn[...] += dq_new
                dkn[...] += dk_new
                dvn[...] += dv_new
            else:
                dqp[...] = dq_new
                dkp[...] = dk_new
                dvp[...] = dv_new
                for r in range(d):
                    nat, perm = _segment_rows(r, d, s)
                    dqn[nat, :] += dqp[perm, :]
                    dkn[nat, :] += dkp[perm, :]
                    dvn[nat, :] += dvp[perm, :]
        g = dqn[...] * scale
        dq_ref[...] = g * cc + _swap_halves(g * ss)
        g = dkn[...]
        dk_ref[...] = g * cc + _swap_halves(g * ss)
        dv_ref[...] = dvn[...]

    def col(off):
        return pl.BlockSpec((None, s, LANES), lambda bi, hp, off=off: (bi, 0, off + hp))

    tab = pl.BlockSpec((None, s, LANES), lambda bi, hp: (bi, 0, 0))
    f32s = pltpu.VMEM((s, LANES), F32)
    mxs = pltpu.VMEM((s, LANES), MXU_DTYPE)
    return pl.pallas_call(
        body, grid=(b, npair), in_specs=[col(0), col(npair), col(2 * npair), tab, tab, col(0), col(0), col(0)],
        out_specs=[col(0)] * 3, out_shape=[jax.ShapeDtypeStruct((b, s, aw), F32)] * 3,
        scratch_shapes=[f32s] * 3 + [mxs] * 4 + [f32s] * 8,
        name="attn_bwd", compiler_params=_cparams(("arbitrary", "arbitrary")))(
            proj3, proj3, proj3, ctab, stab, dout3, out3, lse3)


def _s5_discretise(a_re, a_im, log_dt, bt_re, bt_im):
    dt = jnp.exp(log_dt)
    mag = jnp.exp(a_re * dt)
    ang = a_im * dt
    lb_re = mag * jnp.cos(ang)
    lb_im = mag * jnp.sin(ang)
    den = a_re * a_re + a_im * a_im
    nr = lb_re - 1.0
    ni = lb_im
    cr = (nr * a_re + ni * a_im) / den
    ci = (ni * a_re - nr * a_im) / den
    return lb_re, lb_im, cr * bt_re - ci * bt_im, cr * bt_im + ci * bt_re


def _s5_params(a_re, a_im, log_dt, bt_re, bt_im):
    def body(ar, ai, ld, br, bi, o1, o2, o3, o4):
        r = _s5_discretise(ar[...], ai[...], ld[...], br[...], bi[...])
        for o, v in zip((o1, o2, o3, o4), r, strict=True):
            o[...] = v

    sd = jax.ShapeDtypeStruct
    return pl.pallas_call(body, out_shape=[sd(a_re.shape, F32)] * 2 + [sd(bt_re.shape, F32)] * 2, name="s5_params")(
        a_re, a_im, log_dt, bt_re, bt_im)


def _s5_params_bwd(a_re, a_im, log_dt, bt_re, bt_im, cts):
    def body(ar, ai, ld, br, bi, c1, c2, c3, c4, o1, o2, o3, o4, o5):
        _, vjp = jax.vjp(_s5_discretise, ar[...], ai[...], ld[...], br[...], bi[...])
        r = vjp((c1[...], c2[...], c3[...], c4[...]))
        for o, v in zip((o1, o2, o3, o4, o5), r, strict=True):
            o[...] = v

    sd = jax.ShapeDtypeStruct
    return pl.pallas_call(
        body, out_shape=[sd(a_re.shape, F32)] * 2 + [sd(log_dt.shape, F32)] + [sd(bt_re.shape, F32)] * 2, name="s5_params_bwd")(
            a_re, a_im, log_dt, bt_re, bt_im, *cts)


S5_TC = 128


def _time_major(tiles):
    nt, tc, lanes = tiles.shape
    return jnp.swapaxes(tiles, 0, 1).reshape(tc * nt, lanes)


def _tile_major(rows, nt):
    return jnp.swapaxes(rows.astype(MXU_DTYPE).reshape(rows.shape[0] // nt, nt, rows.shape[1]), 0, 1)


def _s5_fwd(proj3, ucol, lam_re, lam_im, bre, bim, cre, cim, dvec, sw):
    b, s, _ = proj3.shape
    nt = lam_re.shape[0]
    ncb = sw // LANES
    tpc = nt // ncb
    tc = S5_TC

    def body(u_ref, lr_ref, li_ref, bre_ref, bim_ref, cre_ref, cim_ref, d_ref, y_ref, xr_ref, xi_ref, sr, si):
        @pl.when(pl.program_id(0) == 0)
        def _():
            sr[...] = jnp.zeros_like(sr)
            si[...] = jnp.zeros_like(si)

        u_all = u_ref[...].reshape(b * tc, sw)
        bur, bui = [], []
        for cb in range(ncb):
            ucb = u_all[:, cb * LANES:(cb + 1) * LANES].astype(MXU_DTYPE)
            for t in range(cb * tpc, (cb + 1) * tpc):
                bur.append(_mm(ucb, bre_ref[t]))
                bui.append(_mm(ucb, bim_ref[t]))
        for bi in range(b):
            rows = slice(bi * tc, (bi + 1) * tc)
            xr_ref[bi] = _time_major(jnp.stack([v_[rows] for v_ in bur]))
            xi_ref[bi] = _time_major(jnp.stack([v_[rows] for v_ in bui]))
        lr, li = lr_ref[...], li_ref[...]

        def step(j, carry):
            off = pl.multiple_of(j * nt, nt)
            new = []
            for bi in range(b):
                pr, pi = carry[2 * bi], carry[2 * bi + 1]
                nr = lr * pr - li * pi + xr_ref[bi, pl.ds(off, nt), :]
                ni = lr * pi + li * pr + xi_ref[bi, pl.ds(off, nt), :]
                xr_ref[bi, pl.ds(off, nt), :] = nr
                xi_ref[bi, pl.ds(off, nt), :] = ni
                new += [nr, ni]
            return tuple(new)

        init = tuple(v for bi in range(b) for v in (sr[bi], si[bi]))
        fin = lax.fori_loop(0, tc, step, init, unroll=4)
        for bi in range(b):
            sr[bi] = fin[2 * bi]
            si[bi] = fin[2 * bi + 1]
        xr_t = [_tile_major(xr_ref[bi], nt) for bi in range(b)]
        xi_t = [_tile_major(xi_ref[bi], nt) for bi in range(b)]
        for cb in range(ncb):
            cols = slice(cb * LANES, (cb + 1) * LANES)
            acc = d_ref[:, cols] * u_all[:, cols]
            for t in range(cb * tpc, (cb + 1) * tpc):
                xr_all = jnp.concatenate([xr_t[bi][t] for bi in range(b)], axis=0)
                xi_all = jnp.concatenate([xi_t[bi][t] for bi in range(b)], axis=0)
                acc = acc + (_mm(xr_all, cre_ref[t]) - _mm(xi_all, cim_ref[t]))
            for bi in range(b):
                y_ref[bi, :, cols] = acc[bi * tc:(bi + 1) * tc]

    def whole(a):
        return pl.BlockSpec(a.shape, lambda c, n=a.ndim: (0,) * n)

    xblk = pl.BlockSpec((b, tc * nt, LANES), lambda c: (0, c, 0))
    return pl.pallas_call(
        body, grid=(s // tc,),
        in_specs=[pl.BlockSpec((b, tc, sw), lambda c: (0, c, ucol))] + [whole(a) for a in (lam_re, lam_im, bre, bim, cre, cim, dvec)],
        out_specs=[pl.BlockSpec((b, tc, sw), lambda c: (0, c, 0)), xblk, xblk],
        out_shape=[jax.ShapeDtypeStruct((b, s, sw), F32)] + [jax.ShapeDtypeStruct((b, s * nt, LANES), F32)] * 2,
        scratch_shapes=[pltpu.VMEM((b, nt, LANES), F32)] * 2, name="s5_fwd",
        compiler_params=_cparams(("arbitrary",)))(proj3, lam_re, lam_im, bre, bim, cre, cim, dvec)


def _s5_bwd(proj3, ucol, dy3, xr3, xi3, lam_re, lam_im, bre, bim, cre, cim, dvec, sw):
    b, s, _ = proj3.shape
    nt = lam_re.shape[0]
    ncb = sw // LANES
    tpc = nt // ncb
    tc = S5_TC
    nchunk = s // tc

    def body(u_ref, dy_ref, xr_ref, xi_ref, pr_ref, pi_ref, lr_ref, li_ref, bre_ref, bim_ref, cre_ref, cim_ref, d_ref,
             du_ref, dbre, dbim, dcre, dcim, dlr, dli, dd, gr, gi, sr, si):
        step_id = pl.program_id(0)

        @pl.when(step_id == 0)
        def _():
            for r in (sr, si, dbre, dbim, dcre, dcim, dlr, dli, dd):
                r[...] = jnp.zeros_like(r)

        u_all = u_ref[...].reshape(b * tc, sw)
        dy_all = dy_ref[...].reshape(b * tc, sw)
        dxr, dxi = [], []
        for cb in range(ncb):
            dycb = dy_all[:, cb * LANES:(cb + 1) * LANES].astype(MXU_DTYPE)
            for t in range(cb * tpc, (cb + 1) * tpc):
                dxr.append(_mm_nt(dycb, cre_ref[t]))
                dxi.append(-_mm_nt(dycb, cim_ref[t]))
        for bi in range(b):
            rows = slice(bi * tc, (bi + 1) * tc)
            gr[bi] = _time_major(jnp.stack([v_[rows] for v_ in dxr]))
            gi[bi] = _time_major(jnp.stack([v_[rows] for v_ in dxi]))
        lr, li = lr_ref[...], li_ref[...]

        has_prev = (step_id != nchunk - 1).astype(F32)
        before = [(pr_ref[bi] * has_prev, pi_ref[bi] * has_prev) for bi in range(b)]

        def step(jj, carry):
            t = tc - 1 - jj
            off = pl.multiple_of(t * nt, nt)
            poff = pl.multiple_of(jnp.maximum(t - 1, 0) * nt, nt)
            new = []
            alr, ali = carry[-2], carry[-1]
            for bi in range(b):
                nr_, ni_ = carry[2 * bi], carry[2 * bi + 1]
                vr = gr[bi, pl.ds(off, nt), :] + lr * nr_ + li * ni_
                vi = gi[bi, pl.ds(off, nt), :] + lr * ni_ - li * nr_
                gr[bi, pl.ds(off, nt), :] = vr
                gi[bi, pl.ds(off, nt), :] = vi
                xpr = jnp.where(t > 0, xr_ref[bi, pl.ds(poff, nt), :], before[bi][0])
                xpi = jnp.where(t > 0, xi_ref[bi, pl.ds(poff, nt), :], before[bi][1])
                alr = alr + (vr * xpr + vi * xpi)
                ali = ali + (vi * xpr - vr * xpi)
                new += [vr, vi]
            return tuple(new) + (alr, ali)

        zero = jnp.zeros((nt, LANES), F32)
        init = tuple(v for bi in range(b) for v in (sr[bi], si[bi])) + (zero, zero)
        fin = lax.fori_loop(0, tc, step, init, unroll=4)
        for bi in range(b):
            sr[bi] = fin[2 * bi]
            si[bi] = fin[2 * bi + 1]
        dlr[...] += fin[-2]
        dli[...] += fin[-1]

        tiles = [[_tile_major(ref[bi], nt) for bi in range(b)] for ref in (gr, gi, xr_ref, xi_ref)]

        def stacked(k, t):
            return jnp.concatenate([tiles[k][bi][t] for bi in range(b)], axis=0)

        for cb in range(ncb):
            cols = slice(cb * LANES, (cb + 1) * LANES)
            ucb32, dycb32 = u_all[:, cols], dy_all[:, cols]
            ucb, dycb = ucb32.astype(MXU_DTYPE), dycb32.astype(MXU_DTYPE)
            acc = d_ref[:, cols] * dycb32
            for t in range(cb * tpc, (cb + 1) * tpc):
                vr, vi = stacked(0, t), stacked(1, t)
                acc = acc + (_mm_nt(vr, bre_ref[t]) + _mm_nt(vi, bim_ref[t]))
                dbre[t] += _mm_tn(ucb, vr)
                dbim[t] += _mm_tn(ucb, vi)
                dcre[t] += _mm_tn(stacked(2, t), dycb)
                dcim[t] -= _mm_tn(stacked(3, t), dycb)
            for bi in range(b):
                du_ref[bi, :, cols] = acc[bi * tc:(bi + 1) * tc]
            dd[:, cols] += _colsum(dycb32 * ucb32)

    def whole(a):
        return pl.BlockSpec(a.shape, lambda c, n=len(a.shape): (0,) * n)

    def rev(c):
        return nchunk - 1 - c

    xblk = pl.BlockSpec((b, tc * nt, LANES), lambda c: (0, rev(c), 0))
    prev = pl.BlockSpec((b, nt, LANES), lambda c: (0, jnp.maximum(rev(c) * tc - 1, 0), 0))
    sd = jax.ShapeDtypeStruct
    blk = sd(bre.shape, F32)
    acc_shapes = [blk, blk, sd(cre.shape, F32), sd(cre.shape, F32), sd(lam_re.shape, F32), sd(lam_re.shape, F32), sd((1, sw), F32)]
    return pl.pallas_call(
        body, grid=(nchunk,),
        in_specs=[pl.BlockSpec((b, tc, sw), lambda c: (0, rev(c), ucol)), pl.BlockSpec((b, tc, sw), lambda c: (0, rev(c), 0)),
                  xblk, xblk, prev, prev] + [whole(a) for a in (lam_re, lam_im, bre, bim, cre, cim, dvec)],
        out_specs=[pl.BlockSpec((b, tc, sw), lambda c: (0, rev(c), 0))] + [whole(a) for a in acc_shapes],
        out_shape=[sd((b, s, sw), F32)] + acc_shapes,
        scratch_shapes=[pltpu.VMEM((b, tc * nt, LANES), F32)] * 2 + [pltpu.VMEM((b, nt, LANES), F32)] * 2, name="s5_bwd",
        compiler_params=_cparams(("arbitrary",)))(proj3, dy3, xr3, xi3, xr3, xi3, lam_re, lam_im, bre, bim, cre, cim, dvec)


def _s5_blocks(bb, cc_, ncb):
    g, n, p = bb.shape
    mask = jnp.asarray(_s5_tile_mask(g, p, ncb))
    nt, gpt, gpc = mask.shape
    bbt, cct = bb.reshape(nt, gpt, n, p), cc_.reshape(nt, gpt, n, p)
    spread = mask[:, :, :, None, None]
    bblk = (bbt[:, :, None] * spread).transpose(0, 2, 3, 1, 4).reshape(nt, gpc * n, gpt * p)
    cblk = (cct[:, :, None] * spread).transpose(0, 1, 4, 2, 3).reshape(nt, gpt * p, gpc * n)
    return bblk, cblk


def _s5_tile_mask(g, p, ncb):
    nt, gpt, gpc = g * p // LANES, LANES // p, g // ncb
    mask = np.zeros((nt, gpt, gpc), np.float32)
    for t in range(nt):
        for gl in range(gpt):
            mask[t, gl, (t * gpt + gl) % gpc] = 1.0
    return mask


def _s5_unblock(dbblk, dcblk, g, n, p, ncb):
    mask = jnp.asarray(_s5_tile_mask(g, p, ncb))
    nt, gpt, gpc = mask.shape
    db = jnp.sum(dbblk.reshape(nt, gpc, n, gpt, p) * mask.transpose(0, 2, 1)[:, :, None, :, None], axis=1)
    dc = jnp.sum(dcblk.reshape(nt, gpt, p, gpc, n) * mask[:, :, None, :, None], axis=3)
    return db.transpose(0, 2, 1, 3).reshape(g, n, p), dc.transpose(0, 1, 3, 2).reshape(g, n, p)


def _adamw_math(w, g, m, v):
    m = ADAM_B1 * m + (1.0 - ADAM_B1) * g
    v = ADAM_B2 * v + (1.0 - ADAM_B2) * (g * g)
    m_hat = m / (1.0 - ADAM_B1 ** ADAM_STEP)
    v_hat = v / (1.0 - ADAM_B2 ** ADAM_STEP)
    delta = -ADAM_LR * (m_hat / (jnp.sqrt(v_hat) + ADAM_EPS) + ADAM_WD * w)
    return delta, m, v


def _adamw_layer(name, parts, w, m, v, layer, prev, tr):
    nparts, r, c = parts.shape
    assert r % tr == 0
    if prev is None:
        prev = [lax.empty(w.shape, F32) for _ in range(4)]

    def body(p_ref, w_ref, m_ref, v_ref, *rest):
        g_out, d_out, m_out, v_out = rest[4:]
        g = p_ref[0].astype(F32)
        for k in range(1, nparts):
            g = g + p_ref[k].astype(F32)
        delta, mn, vn = _adamw_math(w_ref[...], g, m_ref[...], v_ref[...])
        g_out[...] = g
        d_out[...] = delta
        m_out[...] = mn
        v_out[...] = vn

    blk = pl.BlockSpec((None, tr, c), lambda i: (layer, i, 0))
    kept = pl.BlockSpec(memory_space=pl.ANY)
    return pl.pallas_call(
        body, grid=(r // tr,), in_specs=[pl.BlockSpec((nparts, tr, c), lambda i: (0, i, 0)), blk, blk, blk] + [kept] * 4,
        out_specs=[blk] * 4, out_shape=[jax.ShapeDtypeStruct(w.shape, F32)] * 4, name=name,
        input_output_aliases={4 + k: k for k in range(4)},
        compiler_params=_cparams(("arbitrary",)))(parts, w, m, v, *prev)


def _piece_rows(size):
    unit = SUBLANES * LANES
    return -(-size // unit) * SUBLANES


def _pack_small(pieces, lead=0):
    rows = []
    for p in pieces:
        head = p.shape[:lead]
        flat = p.reshape(head + (-1,))
        nrow = _piece_rows(flat.shape[-1])
        flat = jnp.pad(flat, [(0, 0)] * lead + [(0, nrow * LANES - flat.shape[-1])])
        rows.append(flat.reshape(head + (nrow, LANES)))
    return jnp.concatenate(rows, axis=lead)


def _step(x, positions, weights, moments_m, moments_v, loss_target, distributed):
    f32 = F32
    bsz, seq, dm = x.shape
    depth = weights["w_in"].shape[0]
    aw = weights["attn_gain"].shape[1]
    sw = weights["ssm_gain"].shape[1]
    dff = weights["b_ff1"].shape[1]
    ng, npst = weights["ssm_a_re"].shape[1:]
    gdim = weights["ssm_d"].shape[2]
    in_w = 3 * aw + sw
    t_rows = bsz * seq
    alpha = (2.0 * depth) ** 0.25
    ncb = sw // LANES
    nt = ng * npst // LANES
    ndev = NDEV if distributed else 1
    tm, tl = 512, 1024

    gather_modes = ["gather"] * len(BIG_NAMES)
    no_token = jnp.zeros((SUBLANES, LANES), f32)

    def shards(l, names):
        return [weights[n][l].astype(MXU_DTYPE) for n in names]

    if distributed:
        first, _ = _exchange_start("weights_start_l0_in", shards(0, BIG_NAMES[:1]), gather_modes[:1])
        (g_in0,) = _exchange_wait("weights_wait_l0_in", first, gather_modes[:1], positions)
        rest0, tok_rest0 = _exchange_start("weights_start_l0_rest", shards(0, BIG_NAMES[1:]), gather_modes[1:], after=g_in0)

    half = HEAD_DIM // 2
    inv_freq = ROPE_THETA ** (-jnp.arange(half, dtype=f32) * 2.0 / HEAD_DIM)
    reps = LANES // half
    inv_row = jnp.tile(inv_freq, reps)[None, :]
    sign_row = jnp.tile(jnp.concatenate([-jnp.ones((half,), f32), jnp.ones((half,), f32)]), LANES // HEAD_DIM)[None, :]
    ctab, stab = _rope_tables(positions[..., None], inv_row, sign_row)

    def row(v):
        return v.reshape(1, -1)

    h = x.reshape(t_rows, dm)
    saved = []
    for l in range(depth):
        tok_in = tok_mix = no_token
        if not distributed:
            g_in, g_glu, g_out, g_ff1, g_ff2 = [weights[n][l].astype(MXU_DTYPE)[None] for n in BIG_NAMES]
        elif l == 0:
            g_in, tok_in = g_in0, tok_rest0
        else:
            g_in, g_glu, g_out, g_ff1, g_ff2 = _exchange_wait(f"weights_wait_l{l}", next_gather, gather_modes, h)
            if l + 1 < depth:
                next_gather, tok_in = _exchange_start(f"weights_start_l{l + 1}", shards(l + 1, BIG_NAMES), gather_modes, after=g_in)
        w_in_l = g_in

        def in_proj(rv, cr):
            return [jnp.concatenate([_mm(rv[0], cr[0][j]) for j in range(ndev)], axis=-1)], []

        (proj,), _ = _rows_call("in_proj", in_proj, [h], [w_in_l, tok_in], [(in_w, f32)], [], tl)
        proj3 = proj.reshape(bsz, seq, in_w)
        attn3, lse3 = _attn_fwd(proj3, ctab, stab, aw)
        if distributed and l == 0:
            g_glu, g_out, g_ff1, g_ff2 = _exchange_wait("weights_wait_l0_rest", rest0, gather_modes[1:], attn3)
            if depth > 1:
                next_gather, tok_mix = _exchange_start("weights_start_l1", shards(1, BIG_NAMES), gather_modes, after=g_glu)
        w_glu_l = g_glu.reshape(sw, sw)
        w_out_l = g_out.reshape(dm, dm)
        w_ff1_l = g_ff1
        w_ff2_l = g_ff2.reshape(dff, dm)

        a_re, a_im = weights["ssm_a_re"][l][:, None, :], weights["ssm_a_im"][l][:, None, :]
        log_dt = weights["ssm_log_dt"][l][:, None, None]
        bt_re = weights["ssm_b_re"][l].transpose(0, 2, 1)
        bt_im = weights["ssm_b_im"][l].transpose(0, 2, 1)
        lb_re, lb_im, bb_re, bb_im = _s5_params(a_re, a_im, log_dt, bt_re, bt_im)
        lam_re, lam_im = lb_re.reshape(nt, LANES), lb_im.reshape(nt, LANES)
        bre, cre = _s5_blocks(bb_re, weights["ssm_c_re"][l], ncb)
        bim, cim = _s5_blocks(bb_im, weights["ssm_c_im"][l], ncb)
        s5c = [a_.astype(MXU_DTYPE) for a_ in (bre, bim, cre, cim)]
        dvec = row(weights["ssm_d"][l])
        ypre3, xr3, xi3 = _s5_fwd(proj3, 3 * aw // sw, lam_re, lam_im, *s5c, dvec, sw)
        attn, ypre = attn3.reshape(t_rows, aw), ypre3.reshape(t_rows, sw)

        b_glu, ga, gs = row(weights["b_glu"][l]), row(weights["attn_gain"][l]), row(weights["ssm_gain"][l])

        def mix(rv, cr):
            at, yp = rv
            g = _gelu(yp)
            ssm = g * jax.nn.sigmoid(_mm(g, cr[0][...]) + cr[1][...])
            return [jnp.concatenate([_rms_norm(at, cr[2][...]), _rms_norm(ssm, cr[3][...])], axis=-1)], []

        (mixed,), _ = _rows_call("mix", mix, [attn, ypre], [w_glu_l, b_glu, ga, gs, tok_mix], [(dm, MXU_DTYPE)], [], tl)

        b_out, g1, b1 = row(weights["b_out"][l]), row(weights["ln1_g"][l]), row(weights["ln1_b"][l])

        def out_proj(rv, cr):
            pre = alpha * rv[1] + _mm(rv[0], cr[0][...]) + cr[1][...]
            return [pre, _layer_norm(pre, cr[2][...], cr[3][...])], []

        (pre1, h1), _ = _rows_call("out_proj", out_proj, [mixed, h], [w_out_l, b_out, g1, b1], [(dm, f32), (dm, f32)], [], tl)

        b_ff1 = row(weights["b_ff1"][l])

        def ff1(rv, cr):
            pre_act = jnp.concatenate([_mm(rv[0], cr[0][j]) for j in range(ndev)], axis=-1) + cr[1][...]
            return [jnp.square(jnp.maximum(pre_act, 0.0))], []

        (act,), _ = _rows_call("ff1", ff1, [h1], [w_ff1_l, b_ff1], [(dff, MXU_DTYPE)], [], tm)

        b_ff2, g2, b2 = row(weights["b_ff2"][l]), row(weights["ln2_g"][l]), row(weights["ln2_b"][l])

        def ff2(rv, cr):
            pre = alpha * rv[1] + _mm(rv[0], cr[0][...]) + cr[1][...]
            return [pre, _layer_norm(pre, cr[2][...], cr[3][...])], []

        (pre2, h2), _ = _rows_call("ff2", ff2, [act, h1], [w_ff2_l, b_ff2, g2, b2], [(dm, f32), (dm, f32)], [], tm)

        saved.append(dict(h=h, proj3=proj3, attn=attn, lse3=lse3, attn3=attn3, ypre=ypre, xr3=xr3, xi3=xi3, mixed=mixed, pre1=pre1,
                          h1=h1, act=act, pre2=pre2, w_in=w_in_l, w_glu=w_glu_l, w_out=w_out_l, w_ff1=w_ff1_l, w_ff2=w_ff2_l,
                          lam=(lam_re, lam_im), s5c=s5c, dvec=dvec, b_glu=b_glu, ga=ga, gs=gs, g1=g1, g2=g2,
                          s5in=(a_re, a_im, log_dt, bt_re, bt_im)))
        h = h2

    g2_last = saved[-1]["g2"]

    def loss_fn(rv, cr):
        y, tgt, pre = rv
        err = y - tgt
        part = 0.5 * jnp.sum(jnp.mean(err * err, axis=-1, keepdims=True), axis=0, keepdims=True)
        dpre, dg, db = _layer_norm_bwd(err * (1.0 / dm), pre, cr[0][...])
        return [dpre], [jnp.broadcast_to(part, (1, LANES)), dg, db, _colsum(dpre)]

    (dpre2,), (loss_acc, dg2, db2, dbff2) = _rows_call(
        "loss", loss_fn, [h, loss_target.reshape(t_rows, dm), saved[-1]["pre2"]], [g2_last], [(dm, f32)],
        [(1, LANES), (1, dm), (1, dm), (1, dm)], tl)
    loss = loss_acc[0, 0]
    if distributed:
        loss = lax.psum(loss, MESH_AXES)

    big_parts = {n: [None] * depth for n in BIG_NAMES}
    small_parts = [None] * depth
    grad_handles = [None] * depth
    grad_modes = ["scatter"] * len(BIG_NAMES) + ["gather"]
    grad_x = None
    for l in reversed(range(depth)):
        sv = saved[l]

        def ff2_bwd(rv, cr):
            r = rv[1].astype(F32)
            relu2 = jnp.where(r > 0.0, (2.0 * r) * lax.rsqrt(r), 0.0)
            da = _mm_nt(rv[0], cr[0][...]) * relu2
            return [da], [_colsum(da)]

        (dact,), (dbff1,) = _rows_call("ff2_bwd", ff2_bwd, [dpre2, sv["act"]], [sv["w_ff2"]], [(dff, MXU_DTYPE)], [(1, dff)], tm)
        big_parts["w_ff2"][l] = _wgrad_call("wgrad_ff2", sv["act"], dpre2, "x", ndev, ndev,
                                            (dff // ndev, dm), tm)
        big_parts["w_ff1"][l] = _wgrad_call("wgrad_ff1", sv["h1"], dact, "y", ndev, ndev,
                                            (dm, dff // ndev), tm)

        def ff1_bwd(rv, cr):
            dacc = alpha * rv[1]
            wpb = dff // ndev
            for j in range(ndev):
                dacc = dacc + _mm_nt(rv[0][:, j * wpb:(j + 1) * wpb], cr[0][j])
            dpre, dg, db = _layer_norm_bwd(dacc, rv[2], cr[1][...])
            return [dpre], [dg, db, _colsum(dpre)]

        (dpre1,), (dg1, db1, dbout) = _rows_call("ff1_bwd", ff1_bwd, [dact, dpre2, sv["pre1"]], [sv["w_ff1"], sv["g1"]],
                                                 [(dm, f32)], [(1, dm)] * 3, tm)
        big_parts["w_out"][l] = _wgrad_call("wgrad_out", sv["mixed"], dpre1, "x", ndev, ndev, (dm // ndev, dm), tl)

        def mix_bwd(rv, cr):
            dp, at, yp = rv
            w_out_r, w_glu_r, bg, ga_, gs_ = cr
            dmixed = _mm_nt(dp, w_out_r[...])
            g = _gelu(yp)
            sig = jax.nn.sigmoid(_mm(g, w_glu_r[...]) + bg[...])
            ssm = g * sig
            dat, dga = _rms_norm_bwd(dmixed[:, :aw], at, ga_[...])
            dssm, dgs = _rms_norm_bwd(dmixed[:, aw:], ssm, gs_[...])
            dz = dssm * g * sig * (1.0 - sig)
            dg = dssm * sig + _mm_nt(dz, w_glu_r[...])
            return [dat, dg * _gelu_grad(yp), dz], [dga, dgs, _colsum(dz)]

        (dattn, dypre, dz), (dga, dgs, dbglu) = _rows_call(
            "mix_bwd", mix_bwd, [dpre1, sv["attn"], sv["ypre"]], [sv["w_out"], sv["w_glu"], sv["b_glu"], sv["ga"], sv["gs"]],
            [(aw, f32), (sw, f32), (sw, MXU_DTYPE)], [(1, aw), (1, sw), (1, sw)], tl)
        big_parts["w_glu"][l] = _wgrad_call("wgrad_glu", sv["ypre"], dz, "x", 1, 1, (sw, sw), tl, prologue=_gelu).reshape(
            ndev, sw // ndev, sw)

        du3, dbre, dbim, dcre, dcim, dlr, dli, dd = _s5_bwd(
            sv["proj3"], 3 * aw // sw, dypre.reshape(bsz, seq, sw), sv["xr3"], sv["xi3"], *sv["lam"], *sv["s5c"], sv["dvec"], sw)
        dbb_re, dc_re = _s5_unblock(dbre, dcre, ng, gdim, npst, ncb)
        dbb_im, dc_im = _s5_unblock(dbim, dcim, ng, gdim, npst, ncb)
        da_re, da_im, dldt, dbt_re, dbt_im = _s5_params_bwd(
            *sv["s5in"], (dlr.reshape(ng, 1, npst), dli.reshape(ng, 1, npst), dbb_re, dbb_im))

        dq3, dk3, dv3 = _attn_bwd(sv["proj3"], ctab, stab, dattn.reshape(bsz, seq, aw), sv["attn3"], sv["lse3"], aw)
        dproj = [dq3.reshape(t_rows, aw), dk3.reshape(t_rows, aw), dv3.reshape(t_rows, aw), du3.reshape(t_rows, sw)]
        big_parts["w_in"][l] = _wgrad_call("wgrad_in", sv["h"], dproj, "y", ndev, ndev, (dm, in_w // ndev), tl)

        small_parts[l] = dict(
            attn_gain=dga, ssm_gain=dgs, ssm_a_re=da_re[:, 0], ssm_a_im=da_im[:, 0], ssm_log_dt=dldt[:, 0, 0], ssm_b_re=dbt_re.transpose(0, 2, 1),
            ssm_b_im=dbt_im.transpose(0, 2, 1), ssm_c_re=dc_re, ssm_c_im=dc_im, ssm_d=dd, b_glu=dbglu, b_out=dbout, ln1_g=dg1,
            ln1_b=db1, b_ff1=dbff1, b_ff2=dbff2, ln2_g=dg2, ln2_b=db2)

        layer_grads = [big_parts[n][l] for n in BIG_NAMES] + [_pack_small([small_parts[l][n] for n in SMALL_NAMES])]
        if distributed:
            grad_handles[l], token = _exchange_start(f"grads_start_l{l}", layer_grads, grad_modes)
        else:
            grad_handles[l], token = layer_grads, jnp.zeros((SUBLANES, LANES), f32)

        wpb = in_w // ndev
        if l > 0:
            prev = saved[l - 1]

            def in_bwd(rv, cr):
                dacc = alpha * rv[4]
                dp = jnp.concatenate([v_.astype(MXU_DTYPE) for v_ in rv[:4]], axis=-1)
                for j in range(ndev):
                    dacc = dacc + _mm_nt(dp[:, j * wpb:(j + 1) * wpb], cr[0][j])
                dpre, dg, db = _layer_norm_bwd(dacc, rv[5], cr[1][...])
                return [dpre], [dg, db, _colsum(dpre)]

            (dpre2,), (dg2, db2, dbff2) = _rows_call("in_bwd", in_bwd, dproj + [dpre1, prev["pre2"]],
                                                     [sv["w_in"], prev["g2"], token], [(dm, f32)], [(1, dm)] * 3, tm)
        else:
            def in_bwd0(rv, cr):
                dacc = alpha * rv[4]
                dp = jnp.concatenate([v_.astype(MXU_DTYPE) for v_ in rv[:4]], axis=-1)
                for j in range(ndev):
                    dacc = dacc + _mm_nt(dp[:, j * wpb:(j + 1) * wpb], cr[0][j])
                return [dacc], []

            (grad_x,), _ = _rows_call("in_bwd0", in_bwd0, dproj + [dpre1], [sv["w_in"], token], [(dm, f32)], [], tm)

    small_shapes = [weights[n].shape[1:] for n in SMALL_NAMES]
    outs = {n: None for n in BIG_NAMES}
    packed = [_pack_small([d[n] for n in SMALL_NAMES], lead=1) for d in (weights, moments_m, moments_v)]
    small_out = None
    after = grad_x
    for l in reversed(range(depth)):
        if distributed:
            recv = _exchange_wait(f"grads_wait_l{l}", grad_handles[l], grad_modes, after)
        else:
            recv = [g_[None] if i == len(BIG_NAMES) else g_ for i, g_ in enumerate(grad_handles[l])]
        for n, parts in zip(BIG_NAMES, recv[:-1], strict=True):
            outs[n] = _adamw_layer("adamw_" + n, parts, weights[n], moments_m[n], moments_v[n], l, outs[n],
                                   min(parts.shape[1], 256))
        small_out = _adamw_layer("adamw_small", recv[-1], *packed, l, small_out, recv[-1].shape[1])
        after = small_out[0]
    for k in range(4):
        row = 0
        for n, shp in zip(SMALL_NAMES, small_shapes, strict=True):
            sz = int(np.prod(shp))
            nrow = _piece_rows(sz)
            piece = small_out[k][:, row:row + nrow].reshape(depth, nrow * LANES)
            outs.setdefault(n, [None] * 4)
            outs[n][k] = piece[:, :sz].reshape((depth,) + tuple(shp))
            row += nrow

    result = [loss, grad_x.reshape(bsz, seq, dm)]
    for k in range(4):
        result += [outs[n][k] for n in WEIGHT_ORDER]
    return tuple(result)


def kernel(x, positions, w_in, attn_gain, ssm_gain, ssm_a_re, ssm_a_im, ssm_log_dt, ssm_b_re, ssm_b_im, ssm_c_re, ssm_c_im, ssm_d, w_glu, b_glu, w_out, b_out, ln1_g, ln1_b, w_ff1, b_ff1, w_ff2, b_ff2, ln2_g, ln2_b, loss_target, m_w_in, m_attn_gain, m_ssm_gain, m_ssm_a_re, m_ssm_a_im, m_ssm_log_dt, m_ssm_b_re, m_ssm_b_im, m_ssm_c_re, m_ssm_c_im, m_ssm_d, m_w_glu, m_b_glu, m_w_out, m_b_out, m_ln1_g, m_ln1_b, m_w_ff1, m_b_ff1, m_w_ff2, m_b_ff2, m_ln2_g, m_ln2_b, v_w_in, v_attn_gain, v_ssm_gain, v_ssm_a_re, v_ssm_a_im, v_ssm_log_dt, v_ssm_b_re, v_ssm_b_im, v_ssm_c_re, v_ssm_c_im, v_ssm_d, v_w_glu, v_b_glu, v_w_out, v_b_out, v_ln1_g, v_ln1_b, v_w_ff1, v_b_ff1, v_w_ff2, v_b_ff2, v_ln2_g, v_ln2_b):
    loc = locals()
    weights = {n: loc[n] for n in WEIGHT_ORDER}
    moments_m = {n: loc["m_" + n] for n in WEIGHT_ORDER}
    moments_v = {n: loc["v_" + n] for n in WEIGHT_ORDER}
    return _step(x, positions, weights, moments_m, moments_v, loss_target, distributed=True)
```

```python
import functools
import math

import jax
import jax.numpy as jnp
import numpy as np
from jax import lax
from jax.experimental import pallas as pl
from jax.experimental.pallas import tpu as pltpu

F32 = jnp.float32
MXU_DTYPE = jnp.bfloat16

HEAD_DIM = 64
DILATION_PAIRS = ((128, 1), (512, 4), (2048, 16))
ROPE_THETA = 10000.0
LN_EPS = 1e-5
RMS_EPS = 1e-6
NEG_INF = -1e30
ADAM_LR, ADAM_B1, ADAM_B2, ADAM_EPS, ADAM_WD, ADAM_STEP = 0.001, 0.9, 0.999, 1e-08, 0.01, 10

LANES = 128
SUBLANES = 8
QBLK = 128
VMEM_LIMIT = 56 * 2**20
MESH_AXES = ("x", "y", "c")
NDEV = 8

SMALL_NAMES = ("attn_gain", "ssm_gain", "ssm_a_re", "ssm_a_im", "ssm_log_dt", "ssm_b_re", "ssm_b_im", "ssm_c_re",
               "ssm_c_im", "ssm_d", "b_glu", "b_out", "ln1_g", "ln1_b", "b_ff1", "b_ff2", "ln2_g", "ln2_b")
BIG_NAMES = ("w_in", "w_glu", "w_out", "w_ff1", "w_ff2")
WEIGHT_ORDER = ("w_in", "attn_gain", "ssm_gain", "ssm_a_re", "ssm_a_im", "ssm_log_dt", "ssm_b_re", "ssm_b_im", "ssm_c_re",
                "ssm_c_im", "ssm_d", "w_glu", "b_glu", "w_out", "b_out", "ln1_g", "ln1_b", "w_ff1", "b_ff1", "w_ff2",
                "b_ff2", "ln2_g", "ln2_b")


def _cparams(sem=None):
    return pltpu.CompilerParams(dimension_semantics=sem, vmem_limit_bytes=VMEM_LIMIT)


def _mm(a, b):
    return jnp.dot(a.astype(MXU_DTYPE), b.astype(MXU_DTYPE), preferred_element_type=F32)


def _mm_nt(a, b):
    return lax.dot_general(a.astype(MXU_DTYPE), b.astype(MXU_DTYPE), (((1,), (1,)), ((), ())), preferred_element_type=F32)


def _mm_tn(a, b):
    return lax.dot_general(a.astype(MXU_DTYPE), b.astype(MXU_DTYPE), (((0,), (0,)), ((), ())), preferred_element_type=F32)


def _colsum(x):
    return jnp.sum(x, axis=0, keepdims=True)


def _layer_norm(x, g, b):
    mu = jnp.mean(x, axis=-1, keepdims=True)
    xc = x - mu
    var = jnp.mean(xc * xc, axis=-1, keepdims=True)
    return xc * lax.rsqrt(var + LN_EPS) * g + b


def _layer_norm_bwd(dy, pre, g):
    mu = jnp.mean(pre, axis=-1, keepdims=True)
    xc = pre - mu
    var = jnp.mean(xc * xc, axis=-1, keepdims=True)
    r = lax.rsqrt(var + LN_EPS)
    xhat = xc * r
    dyg = dy * g
    dpre = r * (dyg - jnp.mean(dyg, axis=-1, keepdims=True) - xhat * jnp.mean(dyg * xhat, axis=-1, keepdims=True))
    return dpre, _colsum(dy * xhat), _colsum(dy)


def _rms_norm(x, g):
    return x * lax.rsqrt(jnp.mean(x * x, axis=-1, keepdims=True) + RMS_EPS) * g


def _rms_norm_bwd(dy, x, g):
    r = lax.rsqrt(jnp.mean(x * x, axis=-1, keepdims=True) + RMS_EPS)
    dyg = dy * g
    dx = dyg * r - x * (r * r * r) * jnp.mean(dyg * x, axis=-1, keepdims=True)
    return dx, _colsum(dy * x * r)


_GELU_C = math.sqrt(2.0 / math.pi)


def _gelu(x):
    return 0.5 * x * (1.0 + jnp.tanh(_GELU_C * (x + 0.044715 * (x * x * x))))


def _gelu_grad(x):
    t = jnp.tanh(_GELU_C * (x + 0.044715 * (x * x * x)))
    return 0.5 * (1.0 + t) + 0.5 * x * (1.0 - t * t) * (_GELU_C * (1.0 + 3.0 * 0.044715 * x * x))


def _rows_call(name, fn, rows, consts, out_rows, out_accs, tm):
    rows = [r if isinstance(r, tuple) else (r, r.shape[1], 0) for r in rows]
    m = rows[0][0].shape[0]
    assert m % tm == 0
    nr, nc, no, na = len(rows), len(consts), len(out_rows), len(out_accs)

    def body(*refs):
        rr, cr = refs[:nr], refs[nr:nr + nc]
        orr, ar = refs[nr + nc:nr + nc + no], refs[nr + nc + no:]
        outs, accs = fn([r[...] for r in rr], cr)
        for o, v in zip(orr, outs, strict=True):
            o[...] = v.astype(o.dtype)
        if na:
            first = pl.program_id(0) == 0

            @pl.when(first)
            def _():
                for a, v in zip(ar, accs, strict=True):
                    a[...] = v

            @pl.when(jnp.logical_not(first))
            def _():
                for a, v in zip(ar, accs, strict=True):
                    a[...] += v

    def whole(shape):
        return pl.BlockSpec(shape, lambda i, n=len(shape): (0,) * n)

    in_specs = [pl.BlockSpec((tm, w), lambda i, cb=cb: (i, cb)) for _, w, cb in rows] + [whole(c.shape) for c in consts]
    out_specs = [pl.BlockSpec((tm, w), lambda i: (i, 0)) for w, _ in out_rows] + [whole(s) for s in out_accs]
    out_shape = [jax.ShapeDtypeStruct((m, w), dt) for w, dt in out_rows] + [jax.ShapeDtypeStruct(s, F32) for s in out_accs]
    res = pl.pallas_call(body, grid=(m // tm,), in_specs=in_specs, out_specs=out_specs, out_shape=out_shape, name=name,
                         compiler_params=_cparams(("arbitrary",)))(*[r[0] for r in rows], *consts)
    return res[:no], res[no:]


def _wgrad_call(name, x, dy, split, nblk, jb, blk_shape, tm, prologue=None):
    dys = list(dy) if isinstance(dy, (list, tuple)) else [dy]
    m = x.shape[0]
    kk, nn = blk_shape
    assert m % tm == 0 and nblk % jb == 0 and (len(dys) == 1 or (split == "y" and jb == nblk))
    xw = kk * jb if split == "x" else x.shape[1]
    yws = [d.shape[1] for d in dys] if (split == "x" or len(dys) > 1) else [nn * jb]
    nrow = m // tm

    def body(x_ref, *rest):
        dy_refs, o_ref, acc_ref = rest[:len(dys)], rest[-2], rest[-1]
        i = pl.program_id(1)

        @pl.when(i == 0)
        def _():
            acc_ref[...] = jnp.zeros_like(acc_ref)

        xv = x_ref[...]
        if prologue is not None:
            xv = prologue(xv)
        xv = xv.astype(MXU_DTYPE)
        dv = [r[...].astype(MXU_DTYPE) for r in dy_refs]
        dv = dv[0] if len(dv) == 1 else jnp.concatenate(dv, axis=-1)
        for j in range(jb):
            xa = xv[:, j * kk:(j + 1) * kk] if split == "x" else xv
            da = dv[:, j * nn:(j + 1) * nn] if split == "y" else dv
            acc_ref[j] += _mm_tn(xa, da)

        @pl.when(i == nrow - 1)
        def _():
            o_ref[...] = acc_ref[...].astype(o_ref.dtype)

    in_specs = [pl.BlockSpec((tm, xw), (lambda j, i: (i, j)) if split == "x" else (lambda j, i: (i, 0)))]
    in_specs += [pl.BlockSpec((tm, yw), (lambda j, i: (i, j)) if (split == "y" and len(dys) == 1) else (lambda j, i: (i, 0)))
                 for yw in yws]
    return pl.pallas_call(
        body, grid=(nblk // jb, nrow), in_specs=in_specs,
        out_specs=pl.BlockSpec((jb, kk, nn), lambda j, i: (j, 0, 0)),
        out_shape=jax.ShapeDtypeStruct((nblk, kk, nn), MXU_DTYPE),
        scratch_shapes=[pltpu.VMEM((jb, kk, nn), F32)], name=name,
        compiler_params=_cparams(("arbitrary", "arbitrary")))(x, *dys)


_HBM_SPEC = pl.BlockSpec(memory_space=pltpu.HBM)
_SEM_SPEC = pl.BlockSpec(memory_space=pltpu.SEMAPHORE)
_EFFECT = pltpu.SideEffectType.DATAFLOW_SIDE_EFFECTING


def _my_index():
    return 4 * lax.axis_index("x") + 2 * lax.axis_index("y") + lax.axis_index("c")


def _peer_copies(ins, lands, send_sems, recv_sems, modes):
    x, y, c = lax.axis_index("x"), lax.axis_index("y"), lax.axis_index("c")
    me = 4 * x + 2 * y + c
    pairs = []
    for k in range(NDEV - 1):
        fx, fy, fc = ((k + 1) >> 2) & 1, ((k + 1) >> 1) & 1, (k + 1) & 1
        px, py, pc = (x + fx) % 2, (y + fy) % 2, (c + fc) % 2
        idx = 4 * px + 2 * py + pc
        for a, md in enumerate(modes):
            src = ins[a] if md == "gather" else ins[a].at[idx]
            sem = a * (NDEV - 1) + k
            common = dict(src_ref=src, send_sem=send_sems.at[sem], recv_sem=recv_sems.at[sem], device_id=(px, py, pc),
                          device_id_type=pl.DeviceIdType.MESH)
            pairs.append((pltpu.make_async_remote_copy(dst_ref=lands[a].at[me], **common),
                          pltpu.make_async_remote_copy(dst_ref=lands[a].at[idx], **common)))
    return pairs


def _exchange_start(name, arrays, modes, after=None):
    n = len(arrays)
    extra = [] if after is None else [after]
    me = _my_index()
    lands = []
    for a, md in zip(arrays, modes, strict=True):
        piece = a if md == "gather" else lax.dynamic_index_in_dim(a, me, 0, keepdims=False)
        lands.append(lax.dynamic_update_index_in_dim(lax.empty((NDEV,) + piece.shape, a.dtype), piece, me, 0))

    def body(*refs):
        ins, lnd = refs[:n], refs[n:2 * n]
        send_sems, recv_sems = refs[2 * n + len(extra)], refs[2 * n + len(extra) + 1]
        token = refs[-1]
        for out_copy, _ in _peer_copies(ins, lnd, send_sems, recv_sems, modes):
            out_copy.start()
        token[...] = jnp.zeros_like(token)

    sems = pltpu.SemaphoreType.DMA((n * (NDEV - 1),))
    thru = [pltpu.HBM(a.shape, a.dtype) for a in list(arrays) + lands]
    res = pl.pallas_call(
        body, name=name, out_shape=(sems, sems, *thru, jax.ShapeDtypeStruct((SUBLANES, LANES), F32)),
        in_specs=[_HBM_SPEC] * (2 * n) + [pl.BlockSpec(memory_space=pl.ANY)] * len(extra),
        out_specs=(_SEM_SPEC, _SEM_SPEC, *([_HBM_SPEC] * (2 * n)), pl.BlockSpec(memory_space=pltpu.VMEM)),
        input_output_aliases={i: 2 + i for i in range(2 * n)},
        compiler_params=pltpu.CompilerParams(has_side_effects=_EFFECT),
    )(*[pltpu.with_memory_space_constraint(a, pltpu.HBM) for a in list(arrays) + lands], *extra)
    return (res[0], res[1], res[2:2 + n], res[2 + n:2 + 2 * n]), res[-1]


def _exchange_wait(name, handle, modes, after):
    send_sems, recv_sems, ins_thru, lands_thru = handle
    n = len(ins_thru)

    def body(*refs):
        ins, lnd = refs[:n], refs[n:2 * n]
        for out_copy, arrival in _peer_copies(ins, lnd, refs[2 * n], refs[2 * n + 1], modes):
            out_copy.wait_send()
            arrival.wait_recv()

    thru = [pltpu.HBM(a.shape, a.dtype) for a in list(ins_thru) + list(lands_thru)]
    res = pl.pallas_call(
        body, name=name, out_shape=tuple(thru),
        in_specs=[_HBM_SPEC] * (2 * n) + [_SEM_SPEC, _SEM_SPEC, pl.BlockSpec(memory_space=pl.ANY)],
        out_specs=tuple([_HBM_SPEC] * (2 * n)), input_output_aliases={i: i for i in range(2 * n)},
        compiler_params=pltpu.CompilerParams(has_side_effects=_EFFECT),
    )(*ins_thru, *lands_thru, send_sems, recv_sems, after)
    return res[n:]


def _rope_tables(positions3, inv_freq_row, sign_row):
    b, s, _ = positions3.shape

    def body(pos_ref, f_ref, sg_ref, c_ref, s_ref):
        ang = pos_ref[...].astype(F32) * f_ref[...]
        c_ref[...] = jnp.cos(ang)
        s_ref[...] = jnp.sin(ang) * sg_ref[...]

    row = pl.BlockSpec((1, LANES), lambda i: (0, 0))
    blk = pl.BlockSpec((None, s, LANES), lambda i: (i, 0, 0))
    return pl.pallas_call(
        body, grid=(b,), in_specs=[pl.BlockSpec((None, s, 1), lambda i: (i, 0, 0)), row, row], out_specs=[blk, blk],
        out_shape=[jax.ShapeDtypeStruct((b, s, LANES), F32)] * 2, name="rope_tables",
        compiler_params=_cparams(("arbitrary",)))(positions3, inv_freq_row, sign_row)


def _swap_halves(t):
    lane = lax.broadcasted_iota(jnp.int32, t.shape, 1)
    half = HEAD_DIM // 2
    return jnp.where((lane % HEAD_DIM) < half, pltpu.roll(t, LANES - half, 1), pltpu.roll(t, half, 1))


def _segment_rows(r, d, s):
    n = s // d
    return (pl.ds(r, n, stride=d) if d > 1 else pl.ds(0, s)), pl.ds(r * n, n)


def _head_lanes(shape):
    lane = lax.broadcasted_iota(jnp.int32, shape, len(shape) - 1)
    return lane < HEAD_DIM


def _bmm_nt(a, b):
    return lax.dot_general(a.astype(MXU_DTYPE), b.astype(MXU_DTYPE), (((2,), (2,)), ((0,), (0,))), preferred_element_type=F32)


def _bmm(a, b):
    return lax.dot_general(a.astype(MXU_DTYPE), b.astype(MXU_DTYPE), (((2,), (1,)), ((0,), (0,))), preferred_element_type=F32)


def _bmm_tn(a, b):
    return lax.dot_general(a.astype(MXU_DTYPE), b.astype(MXU_DTYPE), (((1,), (1,)), ((0,), (0,))), preferred_element_type=F32)


def _stack_heads(t3):
    head_a = _head_lanes(t3.shape)
    zero = jnp.zeros_like(t3)
    return jnp.concatenate([jnp.where(head_a, t3, zero), jnp.where(head_a, zero, t3)], axis=1)


QUNIT = QBLK // 2


def _with_previous(t3, nprev):
    shifted = [jnp.concatenate([t3[:k], t3[:-k]], axis=0) for k in range(nprev, 0, -1)]
    return jnp.concatenate(shifted + [t3], axis=1)


def _head_columns(t3):
    return jnp.concatenate([t3[:, :, lo:lo + 1] for lo in range(0, LANES, HEAD_DIM)], axis=1)


def _to_own_unit(t, nprev):
    unit = t.shape[1] // (nprev + 1)
    out = t[:, nprev * unit:]
    for k in range(1, nprev + 1):
        part = t[:, (nprev - k) * unit:(nprev - k + 1) * unit]
        out = out + jnp.concatenate([part[k:], jnp.zeros_like(part[:k])], axis=0)
    return out


def _branch_operands(qh, kh, vh, s, nb):
    nh = LANES // HEAD_DIM
    unit, nprev = (QBLK, 0) if nb == 1 else (QUNIT, QBLK // QUNIT)
    g = s // unit
    q3 = _stack_heads(qh[...].reshape(g, unit, LANES))
    k3, v3 = kh[...].reshape(g, unit, LANES), vh[...].reshape(g, unit, LANES)
    if nprev == 0:
        qi = lax.broadcasted_iota(jnp.int32, (1, nh * unit, unit), 1) % unit
        kj = lax.broadcasted_iota(jnp.int32, (1, nh * unit, unit), 2)
        return q3, k3, v3, kj <= qi, nprev
    shape = (g, nh * unit, (nprev + 1) * unit)
    qi = lax.broadcasted_iota(jnp.int32, shape, 1) % unit
    kj = lax.broadcasted_iota(jnp.int32, shape, 2)
    j = lax.broadcasted_iota(jnp.int32, shape, 0)
    per_block = QBLK // unit
    opens = ((j // per_block) % nb) == 0
    mask = (kj >= qi) & (kj <= qi + QBLK) & ((kj >= QBLK - unit * (j % per_block)) | jnp.logical_not(opens))
    return q3, _with_previous(k3, nprev), _with_previous(v3, nprev), mask, nprev


def _attn_fwd(proj3, ctab, stab, aw):
    b, s, _ = proj3.shape
    npair = aw // LANES
    scale = HEAD_DIM ** -0.5
    nbr = len(DILATION_PAIRS)

    def body(q_ref, k_ref, v_ref, c_ref, s_ref, o_ref, l_ref, qf, kf, qh, kh, vh, op, lp, ob, lb):
        cc, ss = c_ref[...], s_ref[...]
        q2, k2 = q_ref[...], k_ref[...]
        qf[...] = (q2 * cc + _swap_halves(q2) * ss) * scale
        kf[...] = k2 * cc + _swap_halves(k2) * ss
        for br, (window, d) in enumerate(DILATION_PAIRS):
            for r in range(d):
                nat, perm = _segment_rows(r, d, s)
                for dst, src in ((qh, qf), (kh, kf), (vh, v_ref)):
                    dst[perm, :] = src[nat, :].astype(MXU_DTYPE)
            o_dst, l_dst = (ob.at[br], lb.at[br]) if d == 1 else (op, lp)
            q3, kk, vv, mask, _ = _branch_operands(qh, kh, vh, s, (s // d) // QBLK)
            unit = q3.shape[1] // (LANES // HEAD_DIM)
            head_a = _head_lanes((q3.shape[0], unit, LANES))
            sc = jnp.where(mask, _bmm_nt(q3, kk), NEG_INF)
            mx = jnp.max(sc, axis=-1, keepdims=True)
            p = jnp.exp(sc - mx)
            den = jnp.sum(p, axis=-1, keepdims=True)
            o2 = _bmm(p, vv) / den
            l2 = mx + jnp.log(den)
            o_dst[...] = jnp.where(head_a, o2[:, :unit], o2[:, unit:]).reshape(s, LANES)
            l_dst[...] = jnp.where(head_a, l2[:, :unit], l2[:, unit:]).reshape(s, LANES)
            if d > 1:
                for r in range(d):
                    nat, perm = _segment_rows(r, d, s)
                    ob[br, nat, :] = op[perm, :]
                    lb[br, nat, :] = lp[perm, :]
        ls = [lb[br] for br in range(nbr)]
        mx = functools.reduce(jnp.maximum, ls)
        ws = [jnp.exp(l - mx) for l in ls]
        tot = functools.reduce(lambda a, b_: a + b_, ws)
        o_ref[...] = functools.reduce(lambda a, b_: a + b_, [(w / tot) * ob[br] for br, w in enumerate(ws)])
        l_ref[...] = mx + jnp.log(tot)

    def col(off):
        return pl.BlockSpec((None, s, LANES), lambda bi, hp, off=off: (bi, 0, off + hp))

    tab = pl.BlockSpec((None, s, LANES), lambda bi, hp: (bi, 0, 0))
    f32s = pltpu.VMEM((s, LANES), F32)
    mxs = pltpu.VMEM((s, LANES), MXU_DTYPE)
    br_s = pltpu.VMEM((nbr, s, LANES), F32)
    return pl.pallas_call(
        body, grid=(b, npair), in_specs=[col(0), col(npair), col(2 * npair), tab, tab], out_specs=[col(0), col(0)],
        out_shape=[jax.ShapeDtypeStruct((b, s, aw), F32)] * 2,
        scratch_shapes=[f32s] * 2 + [mxs] * 3 + [f32s] * 2 + [br_s] * 2,
        name="attn_fwd", compiler_params=_cparams(("arbitrary", "arbitrary")))(proj3, proj3, proj3, ctab, stab)


def _attn_bwd(proj3, ctab, stab, dout3, out3, lse3, aw):
    b, s, _ = proj3.shape
    npair = aw // LANES
    scale = HEAD_DIM ** -0.5
    nheads = LANES // HEAD_DIM

    def body(q_ref, k_ref, v_ref, c_ref, s_ref, do_ref, o_ref, l_ref, dq_ref, dk_ref, dv_ref,
             qf, kf, dlf, qh, kh, vh, doh, lpm, dpm, dqp, dkp, dvp, dqn, dkn, dvn):
        cc, ss = c_ref[...], s_ref[...]
        q2, k2 = q_ref[...], k_ref[...]
        qf[...] = (q2 * cc + _swap_halves(q2) * ss) * scale
        kf[...] = k2 * cc + _swap_halves(k2) * ss
        dd = do_ref[...] * o_ref[...]
        in_a = _head_lanes((s, LANES))
        sum_a = jnp.sum(jnp.where(in_a, dd, 0.0), axis=-1, keepdims=True)
        sum_b = jnp.sum(jnp.where(in_a, 0.0, dd), axis=-1, keepdims=True)
        dlf[...] = jnp.where(in_a, sum_a, sum_b)
        for r_ in (dqn, dkn, dvn):
            r_[...] = jnp.zeros_like(r_)
        for window, d in DILATION_PAIRS:
            for r in range(d):
                nat, perm = _segment_rows(r, d, s)
                for dst, src in ((qh, qf), (kh, kf), (vh, v_ref), (doh, do_ref)):
                    dst[perm, :] = src[nat, :].astype(MXU_DTYPE)
                if d > 1:
                    lpm[perm, :] = l_ref[nat, :]
                    dpm[perm, :] = dlf[nat, :]
            l_src, d_src = (l_ref, dlf) if d == 1 else (lpm, dpm)
            q3, kk, vv, mask, nprev = _branch_operands(qh, kh, vh, s, (s // d) // QBLK)
            g_, unit = q3.shape[0], q3.shape[1] // nheads
            head_a = _head_lanes((g_, unit, LANES))
            do3 = _stack_heads(doh[...].reshape(g_, unit, LANES))
            lcol, dcol = _head_columns(l_src[...].reshape(g_, unit, LANES)), _head_columns(d_src[...].reshape(g_, unit, LANES))
            p = jnp.exp(jnp.where(mask, _bmm_nt(q3, kk), NEG_INF) - lcol)
            ds_ = p * (_bmm_nt(do3, vv) - dcol)
            dq2 = _bmm(ds_, kk)
            dq_new = jnp.where(head_a, dq2[:, :unit], dq2[:, unit:]).reshape(s, LANES)
            dk_new = _to_own_unit(_bmm_tn(ds_, q3), nprev).reshape(s, LANES)
            dv_new = _to_own_unit(_bmm_tn(p, do3), nprev).reshape(s, LANES)
            if d == 1:
                dqn[...] += dq_new
                dkn[...] += dk_new
                dvn[...] += dv_new
            else:
                dqp[...] = dq_new
                dkp[...] = dk_new
                dvp[...] = dv_new
                for r in range(d):
                    nat, perm = _segment_rows(r, d, s)
                    dqn[nat, :] += dqp[perm, :]
                    dkn[nat, :] += dkp[perm, :]
                    dvn[nat, :] += dvp[perm, :]
        g = dqn[...] * scale
        dq_ref[...] = (g * cc + _swap_halves(g * ss)).astype(dq_ref.dtype)
        g = dkn[...]
        dk_ref[...] = (g * cc + _swap_halves(g * ss)).astype(dk_ref.dtype)
        dv_ref[...] = dvn[...].astype(dv_ref.dtype)

    def col(off):
        return pl.BlockSpec((None, s, LANES), lambda bi, hp, off=off: (bi, 0, off + hp))

    tab = pl.BlockSpec((None, s, LANES), lambda bi, hp: (bi, 0, 0))
    f32s = pltpu.VMEM((s, LANES), F32)
    mxs = pltpu.VMEM((s, LANES), MXU_DTYPE)
    return pl.pallas_call(
        body, grid=(b, npair), in_specs=[col(0), col(npair), col(2 * npair), tab, tab, col(0), col(0), col(0)],
        out_specs=[col(0)] * 3, out_shape=[jax.ShapeDtypeStruct((b, s, aw), MXU_DTYPE)] * 3,
        scratch_shapes=[f32s] * 3 + [mxs] * 4 + [f32s] * 8,
        name="attn_bwd", compiler_params=_cparams(("arbitrary", "arbitrary")))(
            proj3, proj3, proj3, ctab, stab, dout3, out3, lse3)


def _s5_discretise(a_re, a_im, log_dt, bt_re, bt_im):
    dt = jnp.exp(log_dt)
    mag = jnp.exp(a_re * dt)
    ang = a_im * dt
    lb_re = mag * jnp.cos(ang)
    lb_im = mag * jnp.sin(ang)
    den = a_re * a_re + a_im * a_im
    nr = lb_re - 1.0
    ni = lb_im
    cr = (nr * a_re + ni * a_im) / den
    ci = (ni * a_re - nr * a_im) / den
    return lb_re, lb_im, cr * bt_re - ci * bt_im, cr * bt_im + ci * bt_re


def _s5_params(a_re, a_im, log_dt, bt_re, bt_im):
    def body(ar, ai, ld, br, bi, o1, o2, o3, o4):
        r = _s5_discretise(ar[...], ai[...], ld[...], br[...], bi[...])
        for o, v in zip((o1, o2, o3, o4), r, strict=True):
            o[...] = v

    sd = jax.ShapeDtypeStruct
    return pl.pallas_call(body, out_shape=[sd(a_re.shape, F32)] * 2 + [sd(bt_re.shape, F32)] * 2, name="s5_params")(
        a_re, a_im, log_dt, bt_re, bt_im)


def _s5_params_bwd(a_re, a_im, log_dt, bt_re, bt_im, cts):
    def body(ar, ai, ld, br, bi, c1, c2, c3, c4, o1, o2, o3, o4, o5):
        _, vjp = jax.vjp(_s5_discretise, ar[...], ai[...], ld[...], br[...], bi[...])
        r = vjp((c1[...], c2[...], c3[...], c4[...]))
        for o, v in zip((o1, o2, o3, o4, o5), r, strict=True):
            o[...] = v

    sd = jax.ShapeDtypeStruct
    return pl.pallas_call(
        body, out_shape=[sd(a_re.shape, F32)] * 2 + [sd(log_dt.shape, F32)] + [sd(bt_re.shape, F32)] * 2, name="s5_params_bwd")(
            a_re, a_im, log_dt, bt_re, bt_im, *cts)


S5_TC = 128


def _time_major(tiles):
    nt, tc, lanes = tiles.shape
    return jnp.swapaxes(tiles, 0, 1).reshape(tc * nt, lanes)


def _tile_major(rows, nt):
    return jnp.swapaxes(rows.astype(MXU_DTYPE).reshape(rows.shape[0] // nt, nt, rows.shape[1]), 0, 1)


def _s5_fwd(proj3, ucol, lam_re, lam_im, bre, bim, cre, cim, dvec, sw):
    b, s, _ = proj3.shape
    nt = lam_re.shape[0]
    ncb = sw // LANES
    tpc = nt // ncb
    tc = S5_TC

    def body(u_ref, lr_ref, li_ref, bre_ref, bim_ref, cre_ref, cim_ref, d_ref, y_ref, xr_ref, xi_ref, sr, si):
        @pl.when(pl.program_id(0) == 0)
        def _():
            sr[...] = jnp.zeros_like(sr)
            si[...] = jnp.zeros_like(si)

        u_all = u_ref[...].reshape(b * tc, sw)
        bur, bui = [], []
        for cb in range(ncb):
            ucb = u_all[:, cb * LANES:(cb + 1) * LANES].astype(MXU_DTYPE)
            for t in range(cb * tpc, (cb + 1) * tpc):
                bur.append(_mm(ucb, bre_ref[t]))
                bui.append(_mm(ucb, bim_ref[t]))
        for bi in range(b):
            rows = slice(bi * tc, (bi + 1) * tc)
            xr_ref[bi] = _time_major(jnp.stack([v_[rows] for v_ in bur]))
            xi_ref[bi] = _time_major(jnp.stack([v_[rows] for v_ in bui]))
        lr, li = lr_ref[...], li_ref[...]

        def step(j, carry):
            off = pl.multiple_of(j * nt, nt)
            new = []
            for bi in range(b):
                pr, pi = carry[2 * bi], carry[2 * bi + 1]
                nr = lr * pr - li * pi + xr_ref[bi, pl.ds(off, nt), :]
                ni = lr * pi + li * pr + xi_ref[bi, pl.ds(off, nt), :]
                xr_ref[bi, pl.ds(off, nt), :] = nr
                xi_ref[bi, pl.ds(off, nt), :] = ni
                new += [nr, ni]
            return tuple(new)

        init = tuple(v for bi in range(b) for v in (sr[bi], si[bi]))
        fin = lax.fori_loop(0, tc, step, init, unroll=4)
        for bi in range(b):
            sr[bi] = fin[2 * bi]
            si[bi] = fin[2 * bi + 1]
        xr_t = [_tile_major(xr_ref[bi], nt) for bi in range(b)]
        xi_t = [_tile_major(xi_ref[bi], nt) for bi in range(b)]
        for cb in range(ncb):
            cols = slice(cb * LANES, (cb + 1) * LANES)
            acc = d_ref[:, cols] * u_all[:, cols]
            for t in range(cb * tpc, (cb + 1) * tpc):
                xr_all = jnp.concatenate([xr_t[bi][t] for bi in range(b)], axis=0)
                xi_all = jnp.concatenate([xi_t[bi][t] for bi in range(b)], axis=0)
                acc = acc + (_mm(xr_all, cre_ref[t]) - _mm(xi_all, cim_ref[t]))
            for bi in range(b):
                y_ref[bi, :, cols] = acc[bi * tc:(bi + 1) * tc]

    def whole(a):
        return pl.BlockSpec(a.shape, lambda c, n=a.ndim: (0,) * n)

    xblk = pl.BlockSpec((b, tc * nt, LANES), lambda c: (0, c, 0))
    return pl.pallas_call(
        body, grid=(s // tc,),
        in_specs=[pl.BlockSpec((b, tc, sw), lambda c: (0, c, ucol))] + [whole(a) for a in (lam_re, lam_im, bre, bim, cre, cim, dvec)],
        out_specs=[pl.BlockSpec((b, tc, sw), lambda c: (0, c, 0)), xblk, xblk],
        out_shape=[jax.ShapeDtypeStruct((b, s, sw), F32)] + [jax.ShapeDtypeStruct((b, s * nt, LANES), F32)] * 2,
        scratch_shapes=[pltpu.VMEM((b, nt, LANES), F32)] * 2, name="s5_fwd",
        compiler_params=_cparams(("arbitrary",)))(proj3, lam_re, lam_im, bre, bim, cre, cim, dvec)


def _s5_bwd(proj3, ucol, dy3, xr3, xi3, lam_re, lam_im, bre, bim, cre, cim, dvec, sw):
    b, s, _ = proj3.shape
    nt = lam_re.shape[0]
    ncb = sw // LANES
    tpc = nt // ncb
    tc = S5_TC
    nchunk = s // tc

    def body(u_ref, dy_ref, xr_ref, xi_ref, pr_ref, pi_ref, lr_ref, li_ref, bre_ref, bim_ref, cre_ref, cim_ref, d_ref,
             du_ref, dbre, dbim, dcre, dcim, dlr, dli, dd, gr, gi, sr, si):
        step_id = pl.program_id(0)

        @pl.when(step_id == 0)
        def _():
            for r in (sr, si, dbre, dbim, dcre, dcim, dlr, dli, dd):
                r[...] = jnp.zeros_like(r)

        u_all = u_ref[...].reshape(b * tc, sw)
        dy_all = dy_ref[...].reshape(b * tc, sw)
        dxr, dxi = [], []
        for cb in range(ncb):
            dycb = dy_all[:, cb * LANES:(cb + 1) * LANES].astype(MXU_DTYPE)
            for t in range(cb * tpc, (cb + 1) * tpc):
                dxr.append(_mm_nt(dycb, cre_ref[t]))
                dxi.append(-_mm_nt(dycb, cim_ref[t]))
        for bi in range(b):
            rows = slice(bi * tc, (bi + 1) * tc)
            gr[bi] = _time_major(jnp.stack([v_[rows] for v_ in dxr]))
            gi[bi] = _time_major(jnp.stack([v_[rows] for v_ in dxi]))
        lr, li = lr_ref[...], li_ref[...]

        has_prev = (step_id != nchunk - 1).astype(F32)
        before = [(pr_ref[bi] * has_prev, pi_ref[bi] * has_prev) for bi in range(b)]

        def step(jj, carry):
            t = tc - 1 - jj
            off = pl.multiple_of(t * nt, nt)
            poff = pl.multiple_of(jnp.maximum(t - 1, 0) * nt, nt)
            new = []
            alr, ali = carry[-2], carry[-1]
            for bi in range(b):
                nr_, ni_ = carry[2 * bi], carry[2 * bi + 1]
                vr = gr[bi, pl.ds(off, nt), :] + lr * nr_ + li * ni_
                vi = gi[bi, pl.ds(off, nt), :] + lr * ni_ - li * nr_
                gr[bi, pl.ds(off, nt), :] = vr
                gi[bi, pl.ds(off, nt), :] = vi
                xpr = jnp.where(t > 0, xr_ref[bi, pl.ds(poff, nt), :], before[bi][0])
                xpi = jnp.where(t > 0, xi_ref[bi, pl.ds(poff, nt), :], before[bi][1])
                alr = alr + (vr * xpr + vi * xpi)
                ali = ali + (vi * xpr - vr * xpi)
                new += [vr, vi]
            return tuple(new) + (alr, ali)

        zero = jnp.zeros((nt, LANES), F32)
        init = tuple(v for bi in range(b) for v in (sr[bi], si[bi])) + (zero, zero)
        fin = lax.fori_loop(0, tc, step, init, unroll=4)
        for bi in range(b):
            sr[bi] = fin[2 * bi]
            si[bi] = fin[2 * bi + 1]
        dlr[...] += fin[-2]
        dli[...] += fin[-1]

        tiles = [[_tile_major(ref[bi], nt) for bi in range(b)] for ref in (gr, gi, xr_ref, xi_ref)]

        def stacked(k, t):
            return jnp.concatenate([tiles[k][bi][t] for bi in range(b)], axis=0)

        for cb in range(ncb):
            cols = slice(cb * LANES, (cb + 1) * LANES)
            ucb32, dycb32 = u_all[:, cols], dy_all[:, cols]
            ucb, dycb = ucb32.astype(MXU_DTYPE), dycb32.astype(MXU_DTYPE)
            acc = d_ref[:, cols] * dycb32
            for t in range(cb * tpc, (cb + 1) * tpc):
                vr, vi = stacked(0, t), stacked(1, t)
                acc = acc + (_mm_nt(vr, bre_ref[t]) + _mm_nt(vi, bim_ref[t]))
                dbre[t] += _mm_tn(ucb, vr)
                dbim[t] += _mm_tn(ucb, vi)
                dcre[t] += _mm_tn(stacked(2, t), dycb)
                dcim[t] -= _mm_tn(stacked(3, t), dycb)
            for bi in range(b):
                du_ref[bi, :, cols] = acc[bi * tc:(bi + 1) * tc].astype(du_ref.dtype)
            dd[:, cols] += _colsum(dycb32 * ucb32)

    def whole(a):
        return pl.BlockSpec(a.shape, lambda c, n=len(a.shape): (0,) * n)

    def rev(c):
        return nchunk - 1 - c

    xblk = pl.BlockSpec((b, tc * nt, LANES), lambda c: (0, rev(c), 0))
    prev = pl.BlockSpec((b, nt, LANES), lambda c: (0, jnp.maximum(rev(c) * tc - 1, 0), 0))
    sd = jax.ShapeDtypeStruct
    blk = sd(bre.shape, F32)
    acc_shapes = [blk, blk, sd(cre.shape, F32), sd(cre.shape, F32), sd(lam_re.shape, F32), sd(lam_re.shape, F32), sd((1, sw), F32)]
    return pl.pallas_call(
        body, grid=(nchunk,),
        in_specs=[pl.BlockSpec((b, tc, sw), lambda c: (0, rev(c), ucol)), pl.BlockSpec((b, tc, sw), lambda c: (0, rev(c), 0)),
                  xblk, xblk, prev, prev] + [whole(a) for a in (lam_re, lam_im, bre, bim, cre, cim, dvec)],
        out_specs=[pl.BlockSpec((b, tc, sw), lambda c: (0, rev(c), 0))] + [whole(a) for a in acc_shapes],
        out_shape=[sd((b, s, sw), MXU_DTYPE)] + acc_shapes,
        scratch_shapes=[pltpu.VMEM((b, tc * nt, LANES), F32)] * 2 + [pltpu.VMEM((b, nt, LANES), F32)] * 2, name="s5_bwd",
        compiler_params=_cparams(("arbitrary",)))(proj3, dy3, xr3, xi3, xr3, xi3, lam_re, lam_im, bre, bim, cre, cim, dvec)


def _s5_blocks(bb, cc_, ncb):
    g, n, p = bb.shape
    mask = jnp.asarray(_s5_tile_mask(g, p, ncb))
    nt, gpt, gpc = mask.shape
    bbt, cct = bb.reshape(nt, gpt, n, p), cc_.reshape(nt, gpt, n, p)
    spread = mask[:, :, :, None, None]
    bblk = (bbt[:, :, None] * spread).transpose(0, 2, 3, 1, 4).reshape(nt, gpc * n, gpt * p)
    cblk = (cct[:, :, None] * spread).transpose(0, 1, 4, 2, 3).reshape(nt, gpt * p, gpc * n)
    return bblk, cblk


def _s5_tile_mask(g, p, ncb):
    nt, gpt, gpc = g * p // LANES, LANES // p, g // ncb
    mask = np.zeros((nt, gpt, gpc), np.float32)
    for t in range(nt):
        for gl in range(gpt):
            mask[t, gl, (t * gpt + gl) % gpc] = 1.0
    return mask


def _s5_unblock(dbblk, dcblk, g, n, p, ncb):
    mask = jnp.asarray(_s5_tile_mask(g, p, ncb))
    nt, gpt, gpc = mask.shape
    db = jnp.sum(dbblk.reshape(nt, gpc, n, gpt, p) * mask.transpose(0, 2, 1)[:, :, None, :, None], axis=1)
    dc = jnp.sum(dcblk.reshape(nt, gpt, p, gpc, n) * mask[:, :, None, :, None], axis=3)
    return db.transpose(0, 2, 1, 3).reshape(g, n, p), dc.transpose(0, 1, 3, 2).reshape(g, n, p)


def _adamw_math(w, g, m, v):
    m = ADAM_B1 * m + (1.0 - ADAM_B1) * g
    v = ADAM_B2 * v + (1.0 - ADAM_B2) * (g * g)
    m_hat = m / (1.0 - ADAM_B1 ** ADAM_STEP)
    v_hat = v / (1.0 - ADAM_B2 ** ADAM_STEP)
    delta = -ADAM_LR * (m_hat / (jnp.sqrt(v_hat) + ADAM_EPS) + ADAM_WD * w)
    return delta, m, v


def _adamw_layer(name, parts, w, m, v, layer, prev, tr):
    nparts, r, c = parts.shape
    assert r % tr == 0
    if prev is None:
        prev = [lax.empty(w.shape, F32) for _ in range(4)]

    def body(p_ref, w_ref, m_ref, v_ref, *rest):
        g_out, d_out, m_out, v_out = rest[4:]
        g = p_ref[0].astype(F32)
        for k in range(1, nparts):
            g = g + p_ref[k].astype(F32)
        delta, mn, vn = _adamw_math(w_ref[...], g, m_ref[...], v_ref[...])
        g_out[...] = g
        d_out[...] = delta
        m_out[...] = mn
        v_out[...] = vn

    blk = pl.BlockSpec((None, tr, c), lambda i: (layer, i, 0))
    kept = pl.BlockSpec(memory_space=pl.ANY)
    return pl.pallas_call(
        body, grid=(r // tr,), in_specs=[pl.BlockSpec((nparts, tr, c), lambda i: (0, i, 0)), blk, blk, blk] + [kept] * 4,
        out_specs=[blk] * 4, out_shape=[jax.ShapeDtypeStruct(w.shape, F32)] * 4, name=name,
        input_output_aliases={4 + k: k for k in range(4)},
        compiler_params=_cparams(("arbitrary",)))(parts, w, m, v, *prev)


def _piece_rows(size):
    unit = SUBLANES * LANES
    return -(-size // unit) * SUBLANES


def _pack_small(pieces, lead=0):
    rows = []
    for p in pieces:
        head = p.shape[:lead]
        flat = p.reshape(head + (-1,))
        nrow = _piece_rows(flat.shape[-1])
        flat = jnp.pad(flat, [(0, 0)] * lead + [(0, nrow * LANES - flat.shape[-1])])
        rows.append(flat.reshape(head + (nrow, LANES)))
    return jnp.concatenate(rows, axis=lead)


def _step(x, positions, weights, moments_m, moments_v, loss_target, distributed):
    f32 = F32
    bsz, seq, dm = x.shape
    depth = weights["w_in"].shape[0]
    aw = weights["attn_gain"].shape[1]
    sw = weights["ssm_gain"].shape[1]
    dff = weights["b_ff1"].shape[1]
    ng, npst = weights["ssm_a_re"].shape[1:]
    gdim = weights["ssm_d"].shape[2]
    in_w = 3 * aw + sw
    t_rows = bsz * seq
    alpha = (2.0 * depth) ** 0.25
    ncb = sw // LANES
    nt = ng * npst // LANES
    ndev = NDEV if distributed else 1
    tm, tl = 512, 1024

    gather_modes = ["gather"] * len(BIG_NAMES)
    no_token = jnp.zeros((SUBLANES, LANES), f32)

    def shards(l, names):
        return [weights[n][l].astype(MXU_DTYPE) for n in names]

    if distributed:
        first, _ = _exchange_start("weights_start_l0_in", shards(0, BIG_NAMES[:1]), gather_modes[:1])
        (g_in0,) = _exchange_wait("weights_wait_l0_in", first, gather_modes[:1], positions)
        rest0, tok_rest0 = _exchange_start("weights_start_l0_rest", shards(0, BIG_NAMES[1:]), gather_modes[1:], after=g_in0)

    half = HEAD_DIM // 2
    inv_freq = ROPE_THETA ** (-jnp.arange(half, dtype=f32) * 2.0 / HEAD_DIM)
    reps = LANES // half
    inv_row = jnp.tile(inv_freq, reps)[None, :]
    sign_row = jnp.tile(jnp.concatenate([-jnp.ones((half,), f32), jnp.ones((half,), f32)]), LANES // HEAD_DIM)[None, :]
    ctab, stab = _rope_tables(positions[..., None], inv_row, sign_row)

    def row(v):
        return v.reshape(1, -1)

    h = x.reshape(t_rows, dm)
    saved = []
    for l in range(depth):
        tok_in = tok_mix = no_token
        if not distributed:
            g_in, g_glu, g_out, g_ff1, g_ff2 = [weights[n][l].astype(MXU_DTYPE)[None] for n in BIG_NAMES]
        elif l == 0:
            g_in, tok_in = g_in0, tok_rest0
        else:
            g_in, g_glu, g_out, g_ff1, g_ff2 = _exchange_wait(f"weights_wait_l{l}", next_gather, gather_modes, h)
            if l + 1 < depth:
                next_gather, tok_in = _exchange_start(f"weights_start_l{l + 1}", shards(l + 1, BIG_NAMES), gather_modes, after=g_in)
        w_in_l = g_in

        def in_proj(rv, cr):
            return [jnp.concatenate([_mm(rv[0], cr[0][j]) for j in range(ndev)], axis=-1)], []

        (proj,), _ = _rows_call("in_proj", in_proj, [h], [w_in_l, tok_in], [(in_w, f32)], [], tl)
        proj3 = proj.reshape(bsz, seq, in_w)
        attn3, lse3 = _attn_fwd(proj3, ctab, stab, aw)
        if distributed and l == 0:
            g_glu, g_out, g_ff1, g_ff2 = _exchange_wait("weights_wait_l0_rest", rest0, gather_modes[1:], attn3)
            if depth > 1:
                next_gather, tok_mix = _exchange_start("weights_start_l1", shards(1, BIG_NAMES), gather_modes, after=g_glu)
        w_glu_l = g_glu.reshape(sw, sw)
        w_out_l = g_out.reshape(dm, dm)
        w_ff1_l = g_ff1
        w_ff2_l = g_ff2.reshape(dff, dm)

        a_re, a_im = weights["ssm_a_re"][l][:, None, :], weights["ssm_a_im"][l][:, None, :]
        log_dt = weights["ssm_log_dt"][l][:, None, None]
        bt_re = weights["ssm_b_re"][l].transpose(0, 2, 1)
        bt_im = weights["ssm_b_im"][l].transpose(0, 2, 1)
        lb_re, lb_im, bb_re, bb_im = _s5_params(a_re, a_im, log_dt, bt_re, bt_im)
        lam_re, lam_im = lb_re.reshape(nt, LANES), lb_im.reshape(nt, LANES)
        bre, cre = _s5_blocks(bb_re, weights["ssm_c_re"][l], ncb)
        bim, cim = _s5_blocks(bb_im, weights["ssm_c_im"][l], ncb)
        s5c = [a_.astype(MXU_DTYPE) for a_ in (bre, bim, cre, cim)]
        dvec = row(weights["ssm_d"][l])
        ypre3, xr3, xi3 = _s5_fwd(proj3, 3 * aw // sw, lam_re, lam_im, *s5c, dvec, sw)
        attn, ypre = attn3.reshape(t_rows, aw), ypre3.reshape(t_rows, sw)

        b_glu, ga, gs = row(weights["b_glu"][l]), row(weights["attn_gain"][l]), row(weights["ssm_gain"][l])

        def mix(rv, cr):
            at, yp = rv
            g = _gelu(yp)
            ssm = g * jax.nn.sigmoid(_mm(g, cr[0][...]) + cr[1][...])
            return [jnp.concatenate([_rms_norm(at, cr[2][...]), _rms_norm(ssm, cr[3][...])], axis=-1)], []

        (mixed,), _ = _rows_call("mix", mix, [attn, ypre], [w_glu_l, b_glu, ga, gs, tok_mix], [(dm, MXU_DTYPE)], [], tl)

        b_out, g1, b1 = row(weights["b_out"][l]), row(weights["ln1_g"][l]), row(weights["ln1_b"][l])

        def out_proj(rv, cr):
            pre = alpha * rv[1] + _mm(rv[0], cr[0][...]) + cr[1][...]
            return [pre, _layer_norm(pre, cr[2][...], cr[3][...])], []

        (pre1, h1), _ = _rows_call("out_proj", out_proj, [mixed, h], [w_out_l, b_out, g1, b1], [(dm, f32), (dm, f32)], [], tl)

        b_ff1 = row(weights["b_ff1"][l])

        def ff1(rv, cr):
            pre_act = jnp.concatenate([_mm(rv[0], cr[0][j]) for j in range(ndev)], axis=-1) + cr[1][...]
            return [jnp.square(jnp.maximum(pre_act, 0.0))], []

        (act,), _ = _rows_call("ff1", ff1, [h1], [w_ff1_l, b_ff1], [(dff, MXU_DTYPE)], [], tm)

        b_ff2, g2, b2 = row(weights["b_ff2"][l]), row(weights["ln2_g"][l]), row(weights["ln2_b"][l])

        def ff2(rv, cr):
            pre = alpha * rv[1] + _mm(rv[0], cr[0][...]) + cr[1][...]
            return [pre, _layer_norm(pre, cr[2][...], cr[3][...])], []

        (pre2, h2), _ = _rows_call("ff2", ff2, [act, h1], [w_ff2_l, b_ff2, g2, b2], [(dm, f32), (dm, f32)], [], tm)

        saved.append(dict(h=h, proj3=proj3, attn=attn, lse3=lse3, attn3=attn3, ypre=ypre, xr3=xr3, xi3=xi3, mixed=mixed, pre1=pre1,
                          h1=h1, act=act, pre2=pre2, w_in=w_in_l, w_glu=w_glu_l, w_out=w_out_l, w_ff1=w_ff1_l, w_ff2=w_ff2_l,
                          lam=(lam_re, lam_im), s5c=s5c, dvec=dvec, b_glu=b_glu, ga=ga, gs=gs, g1=g1, g2=g2,
                          s5in=(a_re, a_im, log_dt, bt_re, bt_im)))
        h = h2

    g2_last = saved[-1]["g2"]

    def loss_fn(rv, cr):
        y, tgt, pre = rv
        err = y - tgt
        part = 0.5 * jnp.sum(jnp.mean(err * err, axis=-1, keepdims=True), axis=0, keepdims=True)
        dpre, dg, db = _layer_norm_bwd(err * (1.0 / dm), pre, cr[0][...])
        return [dpre], [jnp.broadcast_to(part, (1, LANES)), dg, db, _colsum(dpre)]

    (dpre2,), (loss_acc, dg2, db2, dbff2) = _rows_call(
        "loss", loss_fn, [h, loss_target.reshape(t_rows, dm), saved[-1]["pre2"]], [g2_last], [(dm, f32)],
        [(1, LANES), (1, dm), (1, dm), (1, dm)], tl)
    loss = loss_acc[0, 0]
    if distributed:
        loss = lax.psum(loss, MESH_AXES)

    big_parts = {n: [None] * depth for n in BIG_NAMES}
    small_parts = [None] * depth
    grad_handles = [None] * depth
    grad_modes = ["scatter"] * len(BIG_NAMES) + ["gather"]
    grad_x = None
    for l in reversed(range(depth)):
        sv = saved[l]

        def ff2_bwd(rv, cr):
            r = rv[1].astype(F32)
            relu2 = jnp.where(r > 0.0, (2.0 * r) * lax.rsqrt(r), 0.0)
            da = _mm_nt(rv[0], cr[0][...]) * relu2
            return [da], [_colsum(da)]

        (dact,), (dbff1,) = _rows_call("ff2_bwd", ff2_bwd, [dpre2, sv["act"]], [sv["w_ff2"]], [(dff, MXU_DTYPE)], [(1, dff)], tm)
        big_parts["w_ff2"][l] = _wgrad_call("wgrad_ff2", sv["act"], dpre2, "x", ndev, ndev,
                                            (dff // ndev, dm), tm)
        big_parts["w_ff1"][l] = _wgrad_call("wgrad_ff1", sv["h1"], dact, "y", ndev, ndev,
                                            (dm, dff // ndev), tm)

        def ff1_bwd(rv, cr):
            dacc = alpha * rv[1]
            wpb = dff // ndev
            for j in range(ndev):
                dacc = dacc + _mm_nt(rv[0][:, j * wpb:(j + 1) * wpb], cr[0][j])
            dpre, dg, db = _layer_norm_bwd(dacc, rv[2], cr[1][...])
            return [dpre], [dg, db, _colsum(dpre)]

        (dpre1,), (dg1, db1, dbout) = _rows_call("ff1_bwd", ff1_bwd, [dact, dpre2, sv["pre1"]], [sv["w_ff1"], sv["g1"]],
                                                 [(dm, f32)], [(1, dm)] * 3, tm)
        big_parts["w_out"][l] = _wgrad_call("wgrad_out", sv["mixed"], dpre1, "x", ndev, ndev, (dm // ndev, dm), tl)

        def mix_bwd(rv, cr):
            dp, at, yp = rv
            w_out_r, w_glu_r, bg, ga_, gs_ = cr
            dmixed = _mm_nt(dp, w_out_r[...])
            g = _gelu(yp)
            sig = jax.nn.sigmoid(_mm(g, w_glu_r[...]) + bg[...])
            ssm = g * sig
            dat, dga = _rms_norm_bwd(dmixed[:, :aw], at, ga_[...])
            dssm, dgs = _rms_norm_bwd(dmixed[:, aw:], ssm, gs_[...])
            dz = dssm * g * sig * (1.0 - sig)
            dg = dssm * sig + _mm_nt(dz, w_glu_r[...])
            return [dat, dg * _gelu_grad(yp), dz], [dga, dgs, _colsum(dz)]

        (dattn, dypre, dz), (dga, dgs, dbglu) = _rows_call(
            "mix_bwd", mix_bwd, [dpre1, sv["attn"], sv["ypre"]], [sv["w_out"], sv["w_glu"], sv["b_glu"], sv["ga"], sv["gs"]],
            [(aw, f32), (sw, f32), (sw, MXU_DTYPE)], [(1, aw), (1, sw), (1, sw)], tl)
        big_parts["w_glu"][l] = _wgrad_call("wgrad_glu", sv["ypre"], dz, "x", 1, 1, (sw, sw), tl, prologue=_gelu).reshape(
            ndev, sw // ndev, sw)

        du3, dbre, dbim, dcre, dcim, dlr, dli, dd = _s5_bwd(
            sv["proj3"], 3 * aw // sw, dypre.reshape(bsz, seq, sw), sv["xr3"], sv["xi3"], *sv["lam"], *sv["s5c"], sv["dvec"], sw)
        dbb_re, dc_re = _s5_unblock(dbre, dcre, ng, gdim, npst, ncb)
        dbb_im, dc_im = _s5_unblock(dbim, dcim, ng, gdim, npst, ncb)
        da_re, da_im, dldt, dbt_re, dbt_im = _s5_params_bwd(
            *sv["s5in"], (dlr.reshape(ng, 1, npst), dli.reshape(ng, 1, npst), dbb_re, dbb_im))

        dq3, dk3, dv3 = _attn_bwd(sv["proj3"], ctab, stab, dattn.reshape(bsz, seq, aw), sv["attn3"], sv["lse3"], aw)
        dproj = [dq3.reshape(t_rows, aw), dk3.reshape(t_rows, aw), dv3.reshape(t_rows, aw), du3.reshape(t_rows, sw)]
        big_parts["w_in"][l] = _wgrad_call("wgrad_in", sv["h"], dproj, "y", ndev, ndev, (dm, in_w // ndev), tl)

        small_parts[l] = dict(
            attn_gain=dga, ssm_gain=dgs, ssm_a_re=da_re[:, 0], ssm_a_im=da_im[:, 0], ssm_log_dt=dldt[:, 0, 0], ssm_b_re=dbt_re.transpose(0, 2, 1),
            ssm_b_im=dbt_im.transpose(0, 2, 1), ssm_c_re=dc_re, ssm_c_im=dc_im, ssm_d=dd, b_glu=dbglu, b_out=dbout, ln1_g=dg1,
            ln1_b=db1, b_ff1=dbff1, b_ff2=dbff2, ln2_g=dg2, ln2_b=db2)

        layer_grads = [big_parts[n][l] for n in BIG_NAMES] + [_pack_small([small_parts[l][n] for n in SMALL_NAMES])]
        if distributed:
            grad_handles[l], token = _exchange_start(f"grads_start_l{l}", layer_grads, grad_modes)
        else:
            grad_handles[l], token = layer_grads, jnp.zeros((SUBLANES, LANES), f32)

        wpb = in_w // ndev
        if l > 0:
            prev = saved[l - 1]

            def in_bwd(rv, cr):
                dacc = alpha * rv[4]
                dp = jnp.concatenate([v_.astype(MXU_DTYPE) for v_ in rv[:4]], axis=-1)
                for j in range(ndev):
                    dacc = dacc + _mm_nt(dp[:, j * wpb:(j + 1) * wpb], cr[0][j])
                dpre, dg, db = _layer_norm_bwd(dacc, rv[5], cr[1][...])
                return [dpre], [dg, db, _colsum(dpre)]

            (dpre2,), (dg2, db2, dbff2) = _rows_call("in_bwd", in_bwd, dproj + [dpre1, prev["pre2"]],
                                                     [sv["w_in"], prev["g2"], token], [(dm, f32)], [(1, dm)] * 3, tl)
        else:
            def in_bwd0(rv, cr):
                dacc = alpha * rv[4]
                dp = jnp.concatenate([v_.astype(MXU_DTYPE) for v_ in rv[:4]], axis=-1)
                for j in range(ndev):
                    dacc = dacc + _mm_nt(dp[:, j * wpb:(j + 1) * wpb], cr[0][j])
                return [dacc], []

            (grad_x,), _ = _rows_call("in_bwd0", in_bwd0, dproj + [dpre1], [sv["w_in"], token], [(dm, f32)], [], tl)

    small_shapes = [weights[n].shape[1:] for n in SMALL_NAMES]
    outs = {n: None for n in BIG_NAMES}
    packed = [_pack_small([d[n] for n in SMALL_NAMES], lead=1) for d in (weights, moments_m, moments_v)]
    small_out = None
    after = grad_x
    for l in reversed(range(depth)):
        if distributed:
            recv = _exchange_wait(f"grads_wait_l{l}", grad_handles[l], grad_modes, after)
        else:
            recv = [g_[None] if i == len(BIG_NAMES) else g_ for i, g_ in enumerate(grad_handles[l])]
        for n, parts in zip(BIG_NAMES, recv[:-1], strict=True):
            outs[n] = _adamw_layer("adamw_" + n, parts, weights[n], moments_m[n], moments_v[n], l, outs[n],
                                   min(parts.shape[1], 256))
        small_out = _adamw_layer("adamw_small", recv[-1], *packed, l, small_out, recv[-1].shape[1])
        after = small_out[0]
    for k in range(4):
        row = 0
        for n, shp in zip(SMALL_NAMES, small_shapes, strict=True):
            sz = int(np.prod(shp))
            nrow = _piece_rows(sz)
            piece = small_out[k][:, row:row + nrow].reshape(depth, nrow * LANES)
            outs.setdefault(n, [None] * 4)
            outs[n][k] = piece[:, :sz].reshape((depth,) + tuple(shp))
            row += nrow

    result = [loss, grad_x.reshape(bsz, seq, dm)]
    for k in range(4):
        result += [outs[n][k] for n in WEIGHT_ORDER]
    return tuple(result)


def kernel(x, positions, w_in, attn_gain, ssm_gain, ssm_a_re, ssm_a_im, ssm_log_dt, ssm_b_re, ssm_b_im, ssm_c_re, ssm_c_im, ssm_d, w_glu, b_glu, w_out, b_out, ln1_g, ln1_b, w_ff1, b_ff1, w_ff2, b_ff2, ln2_g, ln2_b, loss_target, m_w_in, m_attn_gain, m_ssm_gain, m_ssm_a_re, m_ssm_a_im, m_ssm_log_dt, m_ssm_b_re, m_ssm_b_im, m_ssm_c_re, m_ssm_c_im, m_ssm_d, m_w_glu, m_b_glu, m_w_out, m_b_out, m_ln1_g, m_ln1_b, m_w_ff1, m_b_ff1, m_w_ff2, m_b_ff2, m_ln2_g, m_ln2_b, v_w_in, v_attn_gain, v_ssm_gain, v_ssm_a_re, v_ssm_a_im, v_ssm_log_dt, v_ssm_b_re, v_ssm_b_im, v_ssm_c_re, v_ssm_c_im, v_ssm_d, v_w_glu, v_b_glu, v_w_out, v_b_out, v_ln1_g, v_ln1_b, v_w_ff1, v_b_ff1, v_w_ff2, v_b_ff2, v_ln2_g, v_ln2_b):
    loc = locals()
    weights = {n: loc[n] for n in WEIGHT_ORDER}
    moments_m = {n: loc["m_" + n] for n in WEIGHT_ORDER}
    moments_v = {n: loc["v_" + n] for n in WEIGHT_ORDER}
    return _step(x, positions, weights, moments_m, moments_v, loss_target, distributed=True)
```

```python
import functools
import math

import jax
import jax.numpy as jnp
import numpy as np
from jax import lax
from jax.experimental import pallas as pl
from jax.experimental.pallas import tpu as pltpu

F32 = jnp.float32
MXU_DTYPE = jnp.bfloat16

HEAD_DIM = 64
DILATION_PAIRS = ((128, 1), (512, 4), (2048, 16))
ROPE_THETA = 10000.0
LN_EPS = 1e-5
RMS_EPS = 1e-6
NEG_INF = -1e30
ADAM_LR, ADAM_B1, ADAM_B2, ADAM_EPS, ADAM_WD, ADAM_STEP = 0.001, 0.9, 0.999, 1e-08, 0.01, 10

LANES = 128
SUBLANES = 8
QBLK = 128
VMEM_LIMIT = 56 * 2**20
MESH_AXES = ("x", "y", "c")
NDEV = 8

SMALL_NAMES = ("attn_gain", "ssm_gain", "ssm_a_re", "ssm_a_im", "ssm_log_dt", "ssm_b_re", "ssm_b_im", "ssm_c_re",
               "ssm_c_im", "ssm_d", "b_glu", "b_out", "ln1_g", "ln1_b", "b_ff1", "b_ff2", "ln2_g", "ln2_b")
BIG_NAMES = ("w_in", "w_glu", "w_out", "w_ff1", "w_ff2")
MIXER_NAMES, FF_NAMES = BIG_NAMES[:3], BIG_NAMES[3:]
WEIGHT_ORDER = ("w_in", "attn_gain", "ssm_gain", "ssm_a_re", "ssm_a_im", "ssm_log_dt", "ssm_b_re", "ssm_b_im", "ssm_c_re",
                "ssm_c_im", "ssm_d", "w_glu", "b_glu", "w_out", "b_out", "ln1_g", "ln1_b", "w_ff1", "b_ff1", "w_ff2",
                "b_ff2", "ln2_g", "ln2_b")


def _cparams(sem=None):
    return pltpu.CompilerParams(dimension_semantics=sem, vmem_limit_bytes=VMEM_LIMIT)


def _mm(a, b):
    return jnp.dot(a.astype(MXU_DTYPE), b.astype(MXU_DTYPE), preferred_element_type=F32)


def _mm_nt(a, b):
    return lax.dot_general(a.astype(MXU_DTYPE), b.astype(MXU_DTYPE), (((1,), (1,)), ((), ())), preferred_element_type=F32)


def _mm_tn(a, b):
    return lax.dot_general(a.astype(MXU_DTYPE), b.astype(MXU_DTYPE), (((0,), (0,)), ((), ())), preferred_element_type=F32)


def _colsum(x):
    return jnp.sum(x, axis=0, keepdims=True)


def _layer_norm(x, g, b):
    mu = jnp.mean(x, axis=-1, keepdims=True)
    xc = x - mu
    var = jnp.mean(xc * xc, axis=-1, keepdims=True)
    return xc * lax.rsqrt(var + LN_EPS) * g + b


def _layer_norm_bwd(dy, pre, g):
    mu = jnp.mean(pre, axis=-1, keepdims=True)
    xc = pre - mu
    var = jnp.mean(xc * xc, axis=-1, keepdims=True)
    r = lax.rsqrt(var + LN_EPS)
    xhat = xc * r
    dyg = dy * g
    dpre = r * (dyg - jnp.mean(dyg, axis=-1, keepdims=True) - xhat * jnp.mean(dyg * xhat, axis=-1, keepdims=True))
    return dpre, _colsum(dy * xhat), _colsum(dy)


def _rms_norm(x, g):
    return x * lax.rsqrt(jnp.mean(x * x, axis=-1, keepdims=True) + RMS_EPS) * g


def _rms_norm_bwd(dy, x, g):
    r = lax.rsqrt(jnp.mean(x * x, axis=-1, keepdims=True) + RMS_EPS)
    dyg = dy * g
    dx = dyg * r - x * (r * r * r) * jnp.mean(dyg * x, axis=-1, keepdims=True)
    return dx, _colsum(dy * x * r)


_GELU_C = math.sqrt(2.0 / math.pi)


def _gelu(x):
    return 0.5 * x * (1.0 + jnp.tanh(_GELU_C * (x + 0.044715 * (x * x * x))))


def _gelu_grad(x):
    t = jnp.tanh(_GELU_C * (x + 0.044715 * (x * x * x)))
    return 0.5 * (1.0 + t) + 0.5 * x * (1.0 - t * t) * (_GELU_C * (1.0 + 3.0 * 0.044715 * x * x))


def _rows_call(name, fn, rows, consts, out_rows, out_accs, tm):
    rows = [r if isinstance(r, tuple) else (r, r.shape[1], 0) for r in rows]
    m = rows[0][0].shape[0]
    assert m % tm == 0
    nr, nc, no, na = len(rows), len(consts), len(out_rows), len(out_accs)

    def body(*refs):
        rr, cr = refs[:nr], refs[nr:nr + nc]
        orr, ar = refs[nr + nc:nr + nc + no], refs[nr + nc + no:]
        outs, accs = fn([r[...] for r in rr], cr)
        for o, v in zip(orr, outs, strict=True):
            o[...] = v.astype(o.dtype)
        if na:
            first = pl.program_id(0) == 0

            @pl.when(first)
            def _():
                for a, v in zip(ar, accs, strict=True):
                    a[...] = v

            @pl.when(jnp.logical_not(first))
            def _():
                for a, v in zip(ar, accs, strict=True):
                    a[...] += v

    def whole(shape):
        return pl.BlockSpec(shape, lambda i, n=len(shape): (0,) * n)

    in_specs = [pl.BlockSpec((tm, w), lambda i, cb=cb: (i, cb)) for _, w, cb in rows] + [whole(c.shape) for c in consts]
    out_specs = [pl.BlockSpec((tm, w), lambda i: (i, 0)) for w, _ in out_rows] + [whole(s) for s in out_accs]
    out_shape = [jax.ShapeDtypeStruct((m, w), dt) for w, dt in out_rows] + [jax.ShapeDtypeStruct(s, F32) for s in out_accs]
    res = pl.pallas_call(body, grid=(m // tm,), in_specs=in_specs, out_specs=out_specs, out_shape=out_shape, name=name,
                         compiler_params=_cparams(("arbitrary",)))(*[r[0] for r in rows], *consts)
    return res[:no], res[no:]


def _wgrad_call(name, x, dy, split, nblk, jb, blk_shape, tm, prologue=None):
    dys = list(dy) if isinstance(dy, (list, tuple)) else [dy]
    m = x.shape[0]
    kk, nn = blk_shape
    assert m % tm == 0 and nblk % jb == 0 and (len(dys) == 1 or (split == "y" and jb == nblk))
    xw = kk * jb if split == "x" else x.shape[1]
    yws = [d.shape[1] for d in dys] if (split == "x" or len(dys) > 1) else [nn * jb]
    nrow = m // tm

    def body(x_ref, *rest):
        dy_refs, o_ref, acc_ref = rest[:len(dys)], rest[-2], rest[-1]
        i = pl.program_id(1)

        @pl.when(i == 0)
        def _():
            acc_ref[...] = jnp.zeros_like(acc_ref)

        xv = x_ref[...]
        if prologue is not None:
            xv = prologue(xv)
        xv = xv.astype(MXU_DTYPE)
        dv = [r[...].astype(MXU_DTYPE) for r in dy_refs]
        dv = dv[0] if len(dv) == 1 else jnp.concatenate(dv, axis=-1)
        for j in range(jb):
            xa = xv[:, j * kk:(j + 1) * kk] if split == "x" else xv
            da = dv[:, j * nn:(j + 1) * nn] if split == "y" else dv
            acc_ref[j] += _mm_tn(xa, da)

        @pl.when(i == nrow - 1)
        def _():
            o_ref[...] = acc_ref[...].astype(o_ref.dtype)

    in_specs = [pl.BlockSpec((tm, xw), (lambda j, i: (i, j)) if split == "x" else (lambda j, i: (i, 0)))]
    in_specs += [pl.BlockSpec((tm, yw), (lambda j, i: (i, j)) if (split == "y" and len(dys) == 1) else (lambda j, i: (i, 0)))
                 for yw in yws]
    return pl.pallas_call(
        body, grid=(nblk // jb, nrow), in_specs=in_specs,
        out_specs=pl.BlockSpec((jb, kk, nn), lambda j, i: (j, 0, 0)),
        out_shape=jax.ShapeDtypeStruct((nblk, kk, nn), MXU_DTYPE),
        scratch_shapes=[pltpu.VMEM((jb, kk, nn), F32)], name=name,
        compiler_params=_cparams(("arbitrary", "arbitrary")))(x, *dys)


_HBM_SPEC = pl.BlockSpec(memory_space=pltpu.HBM)
_SEM_SPEC = pl.BlockSpec(memory_space=pltpu.SEMAPHORE)
_EFFECT = pltpu.SideEffectType.DATAFLOW_SIDE_EFFECTING


def _my_index():
    return 4 * lax.axis_index("x") + 2 * lax.axis_index("y") + lax.axis_index("c")


def _peer_copies(ins, lands, send_sems, recv_sems, modes):
    x, y, c = lax.axis_index("x"), lax.axis_index("y"), lax.axis_index("c")
    me = 4 * x + 2 * y + c
    pairs = []
    for k in range(NDEV - 1):
        fx, fy, fc = ((k + 1) >> 2) & 1, ((k + 1) >> 1) & 1, (k + 1) & 1
        px, py, pc = (x + fx) % 2, (y + fy) % 2, (c + fc) % 2
        idx = 4 * px + 2 * py + pc
        for a, md in enumerate(modes):
            src = ins[a] if md == "gather" else ins[a].at[idx]
            sem = a * (NDEV - 1) + k
            common = dict(src_ref=src, send_sem=send_sems.at[sem], recv_sem=recv_sems.at[sem], device_id=(px, py, pc),
                          device_id_type=pl.DeviceIdType.MESH)
            pairs.append((pltpu.make_async_remote_copy(dst_ref=lands[a].at[me], **common),
                          pltpu.make_async_remote_copy(dst_ref=lands[a].at[idx], **common)))
    return pairs


def _exchange_start(name, arrays, modes, after=None):
    n = len(arrays)
    extra = [] if after is None else [after]
    me = _my_index()
    lands = []
    for a, md in zip(arrays, modes, strict=True):
        piece = a if md == "gather" else lax.dynamic_index_in_dim(a, me, 0, keepdims=False)
        lands.append(lax.dynamic_update_index_in_dim(lax.empty((NDEV,) + piece.shape, a.dtype), piece, me, 0))

    def body(*refs):
        ins, lnd = refs[:n], refs[n:2 * n]
        send_sems, recv_sems = refs[2 * n + len(extra)], refs[2 * n + len(extra) + 1]
        token = refs[-1]
        for out_copy, _ in _peer_copies(ins, lnd, send_sems, recv_sems, modes):
            out_copy.start()
        token[...] = jnp.zeros_like(token)

    sems = pltpu.SemaphoreType.DMA((n * (NDEV - 1),))
    thru = [pltpu.HBM(a.shape, a.dtype) for a in list(arrays) + lands]
    res = pl.pallas_call(
        body, name=name, out_shape=(sems, sems, *thru, jax.ShapeDtypeStruct((SUBLANES, LANES), F32)),
        in_specs=[_HBM_SPEC] * (2 * n) + [pl.BlockSpec(memory_space=pl.ANY)] * len(extra),
        out_specs=(_SEM_SPEC, _SEM_SPEC, *([_HBM_SPEC] * (2 * n)), pl.BlockSpec(memory_space=pltpu.VMEM)),
        input_output_aliases={i: 2 + i for i in range(2 * n)},
        compiler_params=pltpu.CompilerParams(has_side_effects=_EFFECT),
    )(*[pltpu.with_memory_space_constraint(a, pltpu.HBM) for a in list(arrays) + lands], *extra)
    return (res[0], res[1], res[2:2 + n], res[2 + n:2 + 2 * n]), res[-1]


def _exchange_wait(name, handle, modes, after):
    send_sems, recv_sems, ins_thru, lands_thru = handle
    n = len(ins_thru)

    def body(*refs):
        ins, lnd = refs[:n], refs[n:2 * n]
        for out_copy, arrival in _peer_copies(ins, lnd, refs[2 * n], refs[2 * n + 1], modes):
            out_copy.wait_send()
            arrival.wait_recv()

    thru = [pltpu.HBM(a.shape, a.dtype) for a in list(ins_thru) + list(lands_thru)]
    res = pl.pallas_call(
        body, name=name, out_shape=tuple(thru),
        in_specs=[_HBM_SPEC] * (2 * n) + [_SEM_SPEC, _SEM_SPEC, pl.BlockSpec(memory_space=pl.ANY)],
        out_specs=tuple([_HBM_SPEC] * (2 * n)), input_output_aliases={i: i for i in range(2 * n)},
        compiler_params=pltpu.CompilerParams(has_side_effects=_EFFECT),
    )(*ins_thru, *lands_thru, send_sems, recv_sems, after)
    return res[n:]


def _rope_tables(positions3, inv_freq_row, sign_row):
    b, s, _ = positions3.shape

    def body(pos_ref, f_ref, sg_ref, c_ref, s_ref):
        ang = pos_ref[...].astype(F32) * f_ref[...]
        c_ref[...] = jnp.cos(ang)
        s_ref[...] = jnp.sin(ang) * sg_ref[...]

    row = pl.BlockSpec((1, LANES), lambda i: (0, 0))
    blk = pl.BlockSpec((None, s, LANES), lambda i: (i, 0, 0))
    return pl.pallas_call(
        body, grid=(b,), in_specs=[pl.BlockSpec((None, s, 1), lambda i: (i, 0, 0)), row, row], out_specs=[blk, blk],
        out_shape=[jax.ShapeDtypeStruct((b, s, LANES), F32)] * 2, name="rope_tables",
        compiler_params=_cparams(("arbitrary",)))(positions3, inv_freq_row, sign_row)


def _swap_halves(t):
    lane = lax.broadcasted_iota(jnp.int32, t.shape, 1)
    half = HEAD_DIM // 2
    return jnp.where((lane % HEAD_DIM) < half, pltpu.roll(t, LANES - half, 1), pltpu.roll(t, half, 1))


def _segment_rows(r, d, s):
    n = s // d
    return (pl.ds(r, n, stride=d) if d > 1 else pl.ds(0, s)), pl.ds(r * n, n)


def _head_lanes(shape):
    lane = lax.broadcasted_iota(jnp.int32, shape, len(shape) - 1)
    return lane < HEAD_DIM


def _bmm_nt(a, b):
    return lax.dot_general(a.astype(MXU_DTYPE), b.astype(MXU_DTYPE), (((2,), (2,)), ((0,), (0,))), preferred_element_type=F32)


def _bmm(a, b):
    return lax.dot_general(a.astype(MXU_DTYPE), b.astype(MXU_DTYPE), (((2,), (1,)), ((0,), (0,))), preferred_element_type=F32)


def _bmm_tn(a, b):
    return lax.dot_general(a.astype(MXU_DTYPE), b.astype(MXU_DTYPE), (((1,), (1,)), ((0,), (0,))), preferred_element_type=F32)


def _stack_heads(t3):
    head_a = _head_lanes(t3.shape)
    zero = jnp.zeros_like(t3)
    return jnp.concatenate([jnp.where(head_a, t3, zero), jnp.where(head_a, zero, t3)], axis=1)


QUNIT = QBLK // 2


def _with_previous(t3, nprev):
    shifted = [jnp.concatenate([t3[:k], t3[:-k]], axis=0) for k in range(nprev, 0, -1)]
    return jnp.concatenate(shifted + [t3], axis=1)


def _head_columns(t3):
    return jnp.concatenate([t3[:, :, lo:lo + 1] for lo in range(0, LANES, HEAD_DIM)], axis=1)


def _to_own_unit(t, nprev):
    unit = t.shape[1] // (nprev + 1)
    out = t[:, nprev * unit:]
    for k in range(1, nprev + 1):
        part = t[:, (nprev - k) * unit:(nprev - k + 1) * unit]
        out = out + jnp.concatenate([part[k:], jnp.zeros_like(part[:k])], axis=0)
    return out


def _branch_operands(qh, kh, vh, s, nb):
    nh = LANES // HEAD_DIM
    unit, nprev = (QBLK, 0) if nb == 1 else (QUNIT, QBLK // QUNIT)
    g = s // unit
    q3 = _stack_heads(qh[...].reshape(g, unit, LANES))
    k3, v3 = kh[...].reshape(g, unit, LANES), vh[...].reshape(g, unit, LANES)
    if nprev == 0:
        qi = lax.broadcasted_iota(jnp.int32, (1, nh * unit, unit), 1) % unit
        kj = lax.broadcasted_iota(jnp.int32, (1, nh * unit, unit), 2)
        return q3, k3, v3, kj <= qi, nprev
    shape = (g, nh * unit, (nprev + 1) * unit)
    qi = lax.broadcasted_iota(jnp.int32, shape, 1) % unit
    kj = lax.broadcasted_iota(jnp.int32, shape, 2)
    j = lax.broadcasted_iota(jnp.int32, shape, 0)
    per_block = QBLK // unit
    opens = ((j // per_block) % nb) == 0
    mask = (kj >= qi) & (kj <= qi + QBLK) & ((kj >= QBLK - unit * (j % per_block)) | jnp.logical_not(opens))
    return q3, _with_previous(k3, nprev), _with_previous(v3, nprev), mask, nprev


def _attn_fwd(proj3, ctab, stab, aw):
    b, s, _ = proj3.shape
    npair = aw // LANES
    scale = HEAD_DIM ** -0.5
    nbr = len(DILATION_PAIRS)

    def body(q_ref, k_ref, v_ref, c_ref, s_ref, o_ref, l_ref, qf, kf, qh, kh, vh, op, lp, ob, lb):
        cc, ss = c_ref[...], s_ref[...]
        q2, k2 = q_ref[...], k_ref[...]
        qf[...] = (q2 * cc + _swap_halves(q2) * ss) * scale
        kf[...] = k2 * cc + _swap_halves(k2) * ss
        for br, (window, d) in enumerate(DILATION_PAIRS):
            for r in range(d):
                nat, perm = _segment_rows(r, d, s)
                for dst, src in ((qh, qf), (kh, kf), (vh, v_ref)):
                    dst[perm, :] = src[nat, :].astype(MXU_DTYPE)
            o_dst, l_dst = (ob.at[br], lb.at[br]) if d == 1 else (op, lp)
            q3, kk, vv, mask, _ = _branch_operands(qh, kh, vh, s, (s // d) // QBLK)
            unit = q3.shape[1] // (LANES // HEAD_DIM)
            head_a = _head_lanes((q3.shape[0], unit, LANES))
            sc = jnp.where(mask, _bmm_nt(q3, kk), NEG_INF)
            mx = jnp.max(sc, axis=-1, keepdims=True)
            p = jnp.exp(sc - mx)
            den = jnp.sum(p, axis=-1, keepdims=True)
            o2 = _bmm(p, vv) / den
            l2 = mx + jnp.log(den)
            o_dst[...] = jnp.where(head_a, o2[:, :unit], o2[:, unit:]).reshape(s, LANES)
            l_dst[...] = jnp.where(head_a, l2[:, :unit], l2[:, unit:]).reshape(s, LANES)
            if d > 1:
                for r in range(d):
                    nat, perm = _segment_rows(r, d, s)
                    ob[br, nat, :] = op[perm, :]
                    lb[br, nat, :] = lp[perm, :]
        ls = [lb[br] for br in range(nbr)]
        mx = functools.reduce(jnp.maximum, ls)
        ws = [jnp.exp(l - mx) for l in ls]
        tot = functools.reduce(lambda a, b_: a + b_, ws)
        o_ref[...] = functools.reduce(lambda a, b_: a + b_, [(w / tot) * ob[br] for br, w in enumerate(ws)])
        l_ref[...] = mx + jnp.log(tot)

    def col(off):
        return pl.BlockSpec((None, s, LANES), lambda bi, hp, off=off: (bi, 0, off + hp))

    tab = pl.BlockSpec((None, s, LANES), lambda bi, hp: (bi, 0, 0))
    f32s = pltpu.VMEM((s, LANES), F32)
    mxs = pltpu.VMEM((s, LANES), MXU_DTYPE)
    br_s = pltpu.VMEM((nbr, s, LANES), F32)
    return pl.pallas_call(
        body, grid=(b, npair), in_specs=[col(0), col(npair), col(2 * npair), tab, tab], out_specs=[col(0), col(0)],
        out_shape=[jax.ShapeDtypeStruct((b, s, aw), F32)] * 2,
        scratch_shapes=[f32s] * 2 + [mxs] * 3 + [f32s] * 2 + [br_s] * 2,
        name="attn_fwd", compiler_params=_cparams(("arbitrary", "arbitrary")))(proj3, proj3, proj3, ctab, stab)


def _attn_bwd(proj3, ctab, stab, dout3, out3, lse3, aw):
    b, s, _ = proj3.shape
    npair = aw // LANES
    scale = HEAD_DIM ** -0.5
    nheads = LANES // HEAD_DIM

    def body(q_ref, k_ref, v_ref, c_ref, s_ref, do_ref, o_ref, l_ref, dq_ref, dk_ref, dv_ref,
             qf, kf, dlf, qh, kh, vh, doh, lpm, dpm, dqp, dkp, dvp, dqn, dkn, dvn):
        cc, ss = c_ref[...], s_ref[...]
        q2, k2 = q_ref[...], k_ref[...]
        qf[...] = (q2 * cc + _swap_halves(q2) * ss) * scale
        kf[...] = k2 * cc + _swap_halves(k2) * ss
        dd = do_ref[...] * o_ref[...]
        in_a = _head_lanes((s, LANES))
        sum_a = jnp.sum(jnp.where(in_a, dd, 0.0), axis=-1, keepdims=True)
        sum_b = jnp.sum(jnp.where(in_a, 0.0, dd), axis=-1, keepdims=True)
        dlf[...] = jnp.where(in_a, sum_a, sum_b)
        for r_ in (dqn, dkn, dvn):
            r_[...] = jnp.zeros_like(r_)
        for window, d in DILATION_PAIRS:
            for r in range(d):
                nat, perm = _segment_rows(r, d, s)
                for dst, src in ((qh, qf), (kh, kf), (vh, v_ref), (doh, do_ref)):
                    dst[perm, :] = src[nat, :].astype(MXU_DTYPE)
                if d > 1:
                    lpm[perm, :] = l_ref[nat, :]
                    dpm[perm, :] = dlf[nat, :]
            l_src, d_src = (l_ref, dlf) if d == 1 else (lpm, dpm)
            q3, kk, vv, mask, nprev = _branch_operands(qh, kh, vh, s, (s // d) // QBLK)
            g_, unit = q3.shape[0], q3.shape[1] // nheads
            head_a = _head_lanes((g_, unit, LANES))
            do3 = _stack_heads(doh[...].reshape(g_, unit, LANES))
            lcol, dcol = _head_columns(l_src[...].reshape(g_, unit, LANES)), _head_columns(d_src[...].reshape(g_, unit, LANES))
            p = jnp.exp(jnp.where(mask, _bmm_nt(q3, kk), NEG_INF) - lcol)
            ds_ = p * (_bmm_nt(do3, vv) - dcol)
            dq2 = _bmm(ds_, kk)
            dq_new = jnp.where(head_a, dq2[:, :unit], dq2[:, unit:]).reshape(s, LANES)
            dk_new = _to_own_unit(_bmm_tn(ds_, q3), nprev).reshape(s, LANES)
            dv_new = _to_own_unit(_bmm_tn(p, do3), nprev).reshape(s, LANES)
            if d == 1:
                dqn[...] += dq_new
                dkn[...] += dk_new
                dvn[...] += dv_new
            else:
                dqp[...] = dq_new
                dkp[...] = dk_new
                dvp[...] = dv_new
                for r in range(d):
                    nat, perm = _segment_rows(r, d, s)
                    dqn[nat, :] += dqp[perm, :]
                    dkn[nat, :] += dkp[perm, :]
                    dvn[nat, :] += dvp[perm, :]
        g = dqn[...] * scale
        dq_ref[...] = (g * cc + _swap_halves(g * ss)).astype(dq_ref.dtype)
        g = dkn[...]
        dk_ref[...] = (g * cc + _swap_halves(g * ss)).astype(dk_ref.dtype)
        dv_ref[...] = dvn[...].astype(dv_ref.dtype)

    def col(off):
        return pl.BlockSpec((None, s, LANES), lambda bi, hp, off=off: (bi, 0, off + hp))

    tab = pl.BlockSpec((None, s, LANES), lambda bi, hp: (bi, 0, 0))
    f32s = pltpu.VMEM((s, LANES), F32)
    mxs = pltpu.VMEM((s, LANES), MXU_DTYPE)
    return pl.pallas_call(
        body, grid=(b, npair), in_specs=[col(0), col(npair), col(2 * npair), tab, tab, col(0), col(0), col(0)],
        out_specs=[col(0)] * 3, out_shape=[jax.ShapeDtypeStruct((b, s, aw), MXU_DTYPE)] * 3,
        scratch_shapes=[f32s] * 3 + [mxs] * 4 + [f32s] * 8,
        name="attn_bwd", compiler_params=_cparams(("arbitrary", "arbitrary")))(
            proj3, proj3, proj3, ctab, stab, dout3, out3, lse3)


def _s5_discretise(a_re, a_im, log_dt, bt_re, bt_im):
    dt = jnp.exp(log_dt)
    mag = jnp.exp(a_re * dt)
    ang = a_im * dt
    lb_re = mag * jnp.cos(ang)
    lb_im = mag * jnp.sin(ang)
    den = a_re * a_re + a_im * a_im
    nr = lb_re - 1.0
    ni = lb_im
    cr = (nr * a_re + ni * a_im) / den
    ci = (ni * a_re - nr * a_im) / den
    return lb_re, lb_im, cr * bt_re - ci * bt_im, cr * bt_im + ci * bt_re


def _s5_params(a_re, a_im, log_dt, bt_re, bt_im):
    def body(ar, ai, ld, br, bi, o1, o2, o3, o4):
        r = _s5_discretise(ar[...], ai[...], ld[...], br[...], bi[...])
        for o, v in zip((o1, o2, o3, o4), r, strict=True):
            o[...] = v

    sd = jax.ShapeDtypeStruct
    return pl.pallas_call(body, out_shape=[sd(a_re.shape, F32)] * 2 + [sd(bt_re.shape, F32)] * 2, name="s5_params")(
        a_re, a_im, log_dt, bt_re, bt_im)


def _s5_params_bwd(a_re, a_im, log_dt, bt_re, bt_im, cts):
    def body(ar, ai, ld, br, bi, c1, c2, c3, c4, o1, o2, o3, o4, o5):
        _, vjp = jax.vjp(_s5_discretise, ar[...], ai[...], ld[...], br[...], bi[...])
        r = vjp((c1[...], c2[...], c3[...], c4[...]))
        for o, v in zip((o1, o2, o3, o4, o5), r, strict=True):
            o[...] = v

    sd = jax.ShapeDtypeStruct
    return pl.pallas_call(
        body, out_shape=[sd(a_re.shape, F32)] * 2 + [sd(log_dt.shape, F32)] + [sd(bt_re.shape, F32)] * 2, name="s5_params_bwd")(
            a_re, a_im, log_dt, bt_re, bt_im, *cts)


S5_TC = 128


def _time_major(tiles):
    nt, tc, lanes = tiles.shape
    return jnp.swapaxes(tiles, 0, 1).reshape(tc * nt, lanes)


def _tile_major(rows, nt):
    return jnp.swapaxes(rows.astype(MXU_DTYPE).reshape(rows.shape[0] // nt, nt, rows.shape[1]), 0, 1)


def _s5_fwd(proj3, ucol, lam_re, lam_im, bre, bim, cre, cim, dvec, sw):
    b, s, _ = proj3.shape
    nt = lam_re.shape[0]
    ncb = sw // LANES
    tpc = nt // ncb
    tc = S5_TC

    def body(u_ref, lr_ref, li_ref, bre_ref, bim_ref, cre_ref, cim_ref, d_ref, y_ref, xr_ref, xi_ref, sr, si):
        @pl.when(pl.program_id(0) == 0)
        def _():
            sr[...] = jnp.zeros_like(sr)
            si[...] = jnp.zeros_like(si)

        u_all = u_ref[...].reshape(b * tc, sw)
        bur, bui = [], []
        for cb in range(ncb):
            ucb = u_all[:, cb * LANES:(cb + 1) * LANES].astype(MXU_DTYPE)
            for t in range(cb * tpc, (cb + 1) * tpc):
                bur.append(_mm(ucb, bre_ref[t]))
                bui.append(_mm(ucb, bim_ref[t]))
        for bi in range(b):
            rows = slice(bi * tc, (bi + 1) * tc)
            xr_ref[bi] = _time_major(jnp.stack([v_[rows] for v_ in bur]))
            xi_ref[bi] = _time_major(jnp.stack([v_[rows] for v_ in bui]))
        lr, li = lr_ref[...], li_ref[...]

        def step(j, carry):
            off = pl.multiple_of(j * nt, nt)
            new = []
            for bi in range(b):
                pr, pi = carry[2 * bi], carry[2 * bi + 1]
                nr = lr * pr - li * pi + xr_ref[bi, pl.ds(off, nt), :]
                ni = lr * pi + li * pr + xi_ref[bi, pl.ds(off, nt), :]
                xr_ref[bi, pl.ds(off, nt), :] = nr
                xi_ref[bi, pl.ds(off, nt), :] = ni
                new += [nr, ni]
            return tuple(new)

        init = tuple(v for bi in range(b) for v in (sr[bi], si[bi]))
        fin = lax.fori_loop(0, tc, step, init, unroll=4)
        for bi in range(b):
            sr[bi] = fin[2 * bi]
            si[bi] = fin[2 * bi + 1]
        xr_t = [_tile_major(xr_ref[bi], nt) for bi in range(b)]
        xi_t = [_tile_major(xi_ref[bi], nt) for bi in range(b)]
        for cb in range(ncb):
            cols = slice(cb * LANES, (cb + 1) * LANES)
            acc = d_ref[:, cols] * u_all[:, cols]
            for t in range(cb * tpc, (cb + 1) * tpc):
                xr_all = jnp.concatenate([xr_t[bi][t] for bi in range(b)], axis=0)
                xi_all = jnp.concatenate([xi_t[bi][t] for bi in range(b)], axis=0)
                acc = acc + (_mm(xr_all, cre_ref[t]) - _mm(xi_all, cim_ref[t]))
            for bi in range(b):
                y_ref[bi, :, cols] = acc[bi * tc:(bi + 1) * tc]

    def whole(a):
        return pl.BlockSpec(a.shape, lambda c, n=a.ndim: (0,) * n)

    xblk = pl.BlockSpec((b, tc * nt, LANES), lambda c: (0, c, 0))
    return pl.pallas_call(
        body, grid=(s // tc,),
        in_specs=[pl.BlockSpec((b, tc, sw), lambda c: (0, c, ucol))] + [whole(a) for a in (lam_re, lam_im, bre, bim, cre, cim, dvec)],
        out_specs=[pl.BlockSpec((b, tc, sw), lambda c: (0, c, 0)), xblk, xblk],
        out_shape=[jax.ShapeDtypeStruct((b, s, sw), F32)] + [jax.ShapeDtypeStruct((b, s * nt, LANES), F32)] * 2,
        scratch_shapes=[pltpu.VMEM((b, nt, LANES), F32)] * 2, name="s5_fwd",
        compiler_params=_cparams(("arbitrary",)))(proj3, lam_re, lam_im, bre, bim, cre, cim, dvec)


def _s5_bwd(proj3, ucol, dy3, xr3, xi3, lam_re, lam_im, bre, bim, cre, cim, dvec, sw):
    b, s, _ = proj3.shape
    nt = lam_re.shape[0]
    ncb = sw // LANES
    tpc = nt // ncb
    tc = S5_TC
    nchunk = s // tc

    def body(u_ref, dy_ref, xr_ref, xi_ref, pr_ref, pi_ref, lr_ref, li_ref, bre_ref, bim_ref, cre_ref, cim_ref, d_ref,
             du_ref, dbre, dbim, dcre, dcim, dlr, dli, dd, gr, gi, sr, si):
        step_id = pl.program_id(0)

        @pl.when(step_id == 0)
        def _():
            for r in (sr, si, dbre, dbim, dcre, dcim, dlr, dli, dd):
                r[...] = jnp.zeros_like(r)

        u_all = u_ref[...].reshape(b * tc, sw)
        dy_all = dy_ref[...].reshape(b * tc, sw)
        dxr, dxi = [], []
        for cb in range(ncb):
            dycb = dy_all[:, cb * LANES:(cb + 1) * LANES].astype(MXU_DTYPE)
            for t in range(cb * tpc, (cb + 1) * tpc):
                dxr.append(_mm_nt(dycb, cre_ref[t]))
                dxi.append(-_mm_nt(dycb, cim_ref[t]))
        for bi in range(b):
            rows = slice(bi * tc, (bi + 1) * tc)
            gr[bi] = _time_major(jnp.stack([v_[rows] for v_ in dxr]))
            gi[bi] = _time_major(jnp.stack([v_[rows] for v_ in dxi]))
        lr, li = lr_ref[...], li_ref[...]

        has_prev = (step_id != nchunk - 1).astype(F32)
        before = [(pr_ref[bi] * has_prev, pi_ref[bi] * has_prev) for bi in range(b)]

        def step(jj, carry):
            t = tc - 1 - jj
            off = pl.multiple_of(t * nt, nt)
            poff = pl.multiple_of(jnp.maximum(t - 1, 0) * nt, nt)
            new = []
            alr, ali = carry[-2], carry[-1]
            for bi in range(b):
                nr_, ni_ = carry[2 * bi], carry[2 * bi + 1]
                vr = gr[bi, pl.ds(off, nt), :] + lr * nr_ + li * ni_
                vi = gi[bi, pl.ds(off, nt), :] + lr * ni_ - li * nr_
                gr[bi, pl.ds(off, nt), :] = vr
                gi[bi, pl.ds(off, nt), :] = vi
                xpr = jnp.where(t > 0, xr_ref[bi, pl.ds(poff, nt), :], before[bi][0])
                xpi = jnp.where(t > 0, xi_ref[bi, pl.ds(poff, nt), :], before[bi][1])
                alr = alr + (vr * xpr + vi * xpi)
                ali = ali + (vi * xpr - vr * xpi)
                new += [vr, vi]
            return tuple(new) + (alr, ali)

        zero = jnp.zeros((nt, LANES), F32)
        init = tuple(v for bi in range(b) for v in (sr[bi], si[bi])) + (zero, zero)
        fin = lax.fori_loop(0, tc, step, init, unroll=4)
        for bi in range(b):
            sr[bi] = fin[2 * bi]
            si[bi] = fin[2 * bi + 1]
        dlr[...] += fin[-2]
        dli[...] += fin[-1]

        tiles = [[_tile_major(ref[bi], nt) for bi in range(b)] for ref in (gr, gi, xr_ref, xi_ref)]

        def stacked(k, t):
            return jnp.concatenate([tiles[k][bi][t] for bi in range(b)], axis=0)

        for cb in range(ncb):
            cols = slice(cb * LANES, (cb + 1) * LANES)
            ucb32, dycb32 = u_all[:, cols], dy_all[:, cols]
            ucb, dycb = ucb32.astype(MXU_DTYPE), dycb32.astype(MXU_DTYPE)
            acc = d_ref[:, cols] * dycb32
            for t in range(cb * tpc, (cb + 1) * tpc):
                vr, vi = stacked(0, t), stacked(1, t)
                acc = acc + (_mm_nt(vr, bre_ref[t]) + _mm_nt(vi, bim_ref[t]))
                dbre[t] += _mm_tn(ucb, vr)
                dbim[t] += _mm_tn(ucb, vi)
                dcre[t] += _mm_tn(stacked(2, t), dycb)
                dcim[t] -= _mm_tn(stacked(3, t), dycb)
            for bi in range(b):
                du_ref[bi, :, cols] = acc[bi * tc:(bi + 1) * tc].astype(du_ref.dtype)
            dd[:, cols] += _colsum(dycb32 * ucb32)

    def whole(a):
        return pl.BlockSpec(a.shape, lambda c, n=len(a.shape): (0,) * n)

    def rev(c):
        return nchunk - 1 - c

    xblk = pl.BlockSpec((b, tc * nt, LANES), lambda c: (0, rev(c), 0))
    prev = pl.BlockSpec((b, nt, LANES), lambda c: (0, jnp.maximum(rev(c) * tc - 1, 0), 0))
    sd = jax.ShapeDtypeStruct
    blk = sd(bre.shape, F32)
    acc_shapes = [blk, blk, sd(cre.shape, F32), sd(cre.shape, F32), sd(lam_re.shape, F32), sd(lam_re.shape, F32), sd((1, sw), F32)]
    return pl.pallas_call(
        body, grid=(nchunk,),
        in_specs=[pl.BlockSpec((b, tc, sw), lambda c: (0, rev(c), ucol)), pl.BlockSpec((b, tc, sw), lambda c: (0, rev(c), 0)),
                  xblk, xblk, prev, prev] + [whole(a) for a in (lam_re, lam_im, bre, bim, cre, cim, dvec)],
        out_specs=[pl.BlockSpec((b, tc, sw), lambda c: (0, rev(c), 0))] + [whole(a) for a in acc_shapes],
        out_shape=[sd((b, s, sw), MXU_DTYPE)] + acc_shapes,
        scratch_shapes=[pltpu.VMEM((b, tc * nt, LANES), F32)] * 2 + [pltpu.VMEM((b, nt, LANES), F32)] * 2, name="s5_bwd",
        compiler_params=_cparams(("arbitrary",)))(proj3, dy3, xr3, xi3, xr3, xi3, lam_re, lam_im, bre, bim, cre, cim, dvec)


def _s5_blocks(bb, cc_, ncb):
    g, n, p = bb.shape
    mask = jnp.asarray(_s5_tile_mask(g, p, ncb))
    nt, gpt, gpc = mask.shape
    bbt, cct = bb.reshape(nt, gpt, n, p), cc_.reshape(nt, gpt, n, p)
    spread = mask[:, :, :, None, None]
    bblk = (bbt[:, :, None] * spread).transpose(0, 2, 3, 1, 4).reshape(nt, gpc * n, gpt * p)
    cblk = (cct[:, :, None] * spread).transpose(0, 1, 4, 2, 3).reshape(nt, gpt * p, gpc * n)
    return bblk, cblk


def _s5_tile_mask(g, p, ncb):
    nt, gpt, gpc = g * p // LANES, LANES // p, g // ncb
    mask = np.zeros((nt, gpt, gpc), np.float32)
    for t in range(nt):
        for gl in range(gpt):
            mask[t, gl, (t * gpt + gl) % gpc] = 1.0
    return mask


def _s5_unblock(dbblk, dcblk, g, n, p, ncb):
    mask = jnp.asarray(_s5_tile_mask(g, p, ncb))
    nt, gpt, gpc = mask.shape
    db = jnp.sum(dbblk.reshape(nt, gpc, n, gpt, p) * mask.transpose(0, 2, 1)[:, :, None, :, None], axis=1)
    dc = jnp.sum(dcblk.reshape(nt, gpt, p, gpc, n) * mask[:, :, None, :, None], axis=3)
    return db.transpose(0, 2, 1, 3).reshape(g, n, p), dc.transpose(0, 1, 3, 2).reshape(g, n, p)


def _adamw_math(w, g, m, v):
    m = ADAM_B1 * m + (1.0 - ADAM_B1) * g
    v = ADAM_B2 * v + (1.0 - ADAM_B2) * (g * g)
    m_hat = m / (1.0 - ADAM_B1 ** ADAM_STEP)
    v_hat = v / (1.0 - ADAM_B2 ** ADAM_STEP)
    delta = -ADAM_LR * (m_hat / (jnp.sqrt(v_hat) + ADAM_EPS) + ADAM_WD * w)
    return delta, m, v


def _adamw_layer(name, parts, w, m, v, layer, prev, tr):
    nparts, r, c = parts.shape
    assert r % tr == 0
    if prev is None:
        prev = [lax.empty(w.shape, F32) for _ in range(4)]

    def body(p_ref, w_ref, m_ref, v_ref, *rest):
        g_out, d_out, m_out, v_out = rest[4:]
        g = p_ref[0].astype(F32)
        for k in range(1, nparts):
            g = g + p_ref[k].astype(F32)
        delta, mn, vn = _adamw_math(w_ref[...], g, m_ref[...], v_ref[...])
        g_out[...] = g
        d_out[...] = delta
        m_out[...] = mn
        v_out[...] = vn

    blk = pl.BlockSpec((None, tr, c), lambda i: (layer, i, 0))
    kept = pl.BlockSpec(memory_space=pl.ANY)
    return pl.pallas_call(
        body, grid=(r // tr,), in_specs=[pl.BlockSpec((nparts, tr, c), lambda i: (0, i, 0)), blk, blk, blk] + [kept] * 4,
        out_specs=[blk] * 4, out_shape=[jax.ShapeDtypeStruct(w.shape, F32)] * 4, name=name,
        input_output_aliases={4 + k: k for k in range(4)},
        compiler_params=_cparams(("arbitrary",)))(parts, w, m, v, *prev)


def _piece_rows(size):
    unit = SUBLANES * LANES
    return -(-size // unit) * SUBLANES


def _pack_small(pieces, lead=0):
    rows = []
    for p in pieces:
        head = p.shape[:lead]
        flat = p.reshape(head + (-1,))
        nrow = _piece_rows(flat.shape[-1])
        flat = jnp.pad(flat, [(0, 0)] * lead + [(0, nrow * LANES - flat.shape[-1])])
        rows.append(flat.reshape(head + (nrow, LANES)))
    return jnp.concatenate(rows, axis=lead)


def _step(x, positions, weights, moments_m, moments_v, loss_target, distributed):
    f32 = F32
    bsz, seq, dm = x.shape
    depth = weights["w_in"].shape[0]
    aw = weights["attn_gain"].shape[1]
    sw = weights["ssm_gain"].shape[1]
    dff = weights["b_ff1"].shape[1]
    ng, npst = weights["ssm_a_re"].shape[1:]
    gdim = weights["ssm_d"].shape[2]
    in_w = 3 * aw + sw
    t_rows = bsz * seq
    alpha = (2.0 * depth) ** 0.25
    ncb = sw // LANES
    nt = ng * npst // LANES
    ndev = NDEV if distributed else 1
    tm, tl = 512, 1024

    gather_modes = ["gather"] * len(BIG_NAMES)
    no_token = jnp.zeros((SUBLANES, LANES), f32)

    def shards(l, names):
        return [weights[n][l].astype(MXU_DTYPE) for n in names]

    if distributed:
        first, _ = _exchange_start("weights_start_l0_in", shards(0, BIG_NAMES[:1]), gather_modes[:1])
        (g_in0,) = _exchange_wait("weights_wait_l0_in", first, gather_modes[:1], positions)
        rest0, tok_rest0 = _exchange_start("weights_start_l0_rest", shards(0, BIG_NAMES[1:]), gather_modes[1:], after=g_in0)

    half = HEAD_DIM // 2
    inv_freq = ROPE_THETA ** (-jnp.arange(half, dtype=f32) * 2.0 / HEAD_DIM)
    reps = LANES // half
    inv_row = jnp.tile(inv_freq, reps)[None, :]
    sign_row = jnp.tile(jnp.concatenate([-jnp.ones((half,), f32), jnp.ones((half,), f32)]), LANES // HEAD_DIM)[None, :]
    ctab, stab = _rope_tables(positions[..., None], inv_row, sign_row)

    def row(v):
        return v.reshape(1, -1)

    h = x.reshape(t_rows, dm)
    saved = []
    for l in range(depth):
        tok_in = tok_mix = no_token
        if not distributed:
            g_in, g_glu, g_out, g_ff1, g_ff2 = [weights[n][l].astype(MXU_DTYPE)[None] for n in BIG_NAMES]
        elif l == 0:
            g_in, tok_in = g_in0, tok_rest0
        else:
            g_in, g_glu, g_out, g_ff1, g_ff2 = _exchange_wait(f"weights_wait_l{l}", next_gather, gather_modes, h)
            if l + 1 < depth:
                next_gather, tok_in = _exchange_start(f"weights_start_l{l + 1}", shards(l + 1, BIG_NAMES), gather_modes, after=g_in)
        w_in_l = g_in

        def in_proj(rv, cr):
            return [jnp.concatenate([_mm(rv[0], cr[0][j]) for j in range(ndev)], axis=-1)], []

        (proj,), _ = _rows_call("in_proj", in_proj, [h], [w_in_l, tok_in], [(in_w, f32)], [], tl)
        proj3 = proj.reshape(bsz, seq, in_w)
        attn3, lse3 = _attn_fwd(proj3, ctab, stab, aw)
        if distributed and l == 0:
            g_glu, g_out, g_ff1, g_ff2 = _exchange_wait("weights_wait_l0_rest", rest0, gather_modes[1:], attn3)
            if depth > 1:
                next_gather, tok_mix = _exchange_start("weights_start_l1", shards(1, BIG_NAMES), gather_modes, after=g_glu)
        w_glu_l = g_glu.reshape(sw, sw)
        w_out_l = g_out.reshape(dm, dm)
        w_ff1_l = g_ff1
        w_ff2_l = g_ff2.reshape(dff, dm)

        a_re, a_im = weights["ssm_a_re"][l][:, None, :], weights["ssm_a_im"][l][:, None, :]
        log_dt = weights["ssm_log_dt"][l][:, None, None]
        bt_re = weights["ssm_b_re"][l].transpose(0, 2, 1)
        bt_im = weights["ssm_b_im"][l].transpose(0, 2, 1)
        lb_re, lb_im, bb_re, bb_im = _s5_params(a_re, a_im, log_dt, bt_re, bt_im)
        lam_re, lam_im = lb_re.reshape(nt, LANES), lb_im.reshape(nt, LANES)
        bre, cre = _s5_blocks(bb_re, weights["ssm_c_re"][l], ncb)
        bim, cim = _s5_blocks(bb_im, weights["ssm_c_im"][l], ncb)
        s5c = [a_.astype(MXU_DTYPE) for a_ in (bre, bim, cre, cim)]
        dvec = row(weights["ssm_d"][l])
        ypre3, xr3, xi3 = _s5_fwd(proj3, 3 * aw // sw, lam_re, lam_im, *s5c, dvec, sw)
        attn, ypre = attn3.reshape(t_rows, aw), ypre3.reshape(t_rows, sw)

        b_glu, ga, gs = row(weights["b_glu"][l]), row(weights["attn_gain"][l]), row(weights["ssm_gain"][l])

        def mix(rv, cr):
            at, yp = rv
            g = _gelu(yp)
            ssm = g * jax.nn.sigmoid(_mm(g, cr[0][...]) + cr[1][...])
            return [jnp.concatenate([_rms_norm(at, cr[2][...]), _rms_norm(ssm, cr[3][...])], axis=-1)], []

        (mixed,), _ = _rows_call("mix", mix, [attn, ypre], [w_glu_l, b_glu, ga, gs, tok_mix], [(dm, MXU_DTYPE)], [], tl)

        b_out, g1, b1 = row(weights["b_out"][l]), row(weights["ln1_g"][l]), row(weights["ln1_b"][l])

        def out_proj(rv, cr):
            pre = alpha * rv[1] + _mm(rv[0], cr[0][...]) + cr[1][...]
            return [pre, _layer_norm(pre, cr[2][...], cr[3][...])], []

        (pre1, h1), _ = _rows_call("out_proj", out_proj, [mixed, h], [w_out_l, b_out, g1, b1], [(dm, f32), (dm, f32)], [], tl)

        b_ff1 = row(weights["b_ff1"][l])

        def ff1(rv, cr):
            pre_act = jnp.concatenate([_mm(rv[0], cr[0][j]) for j in range(ndev)], axis=-1) + cr[1][...]
            return [jnp.square(jnp.maximum(pre_act, 0.0))], []

        (act,), _ = _rows_call("ff1", ff1, [h1], [w_ff1_l, b_ff1], [(dff, MXU_DTYPE)], [], tm)

        b_ff2, g2, b2 = row(weights["b_ff2"][l]), row(weights["ln2_g"][l]), row(weights["ln2_b"][l])

        def ff2(rv, cr):
            pre = alpha * rv[1] + _mm(rv[0], cr[0][...]) + cr[1][...]
            return [pre, _layer_norm(pre, cr[2][...], cr[3][...])], []

        (pre2, h2), _ = _rows_call("ff2", ff2, [act, h1], [w_ff2_l, b_ff2, g2, b2], [(dm, f32), (dm, f32)], [], tm)

        saved.append(dict(h=h, proj3=proj3, attn=attn, lse3=lse3, attn3=attn3, ypre=ypre, xr3=xr3, xi3=xi3, mixed=mixed, pre1=pre1,
                          h1=h1, act=act, pre2=pre2, w_in=w_in_l, w_glu=w_glu_l, w_out=w_out_l, w_ff1=w_ff1_l, w_ff2=w_ff2_l,
                          lam=(lam_re, lam_im), s5c=s5c, dvec=dvec, b_glu=b_glu, ga=ga, gs=gs, g1=g1, g2=g2,
                          s5in=(a_re, a_im, log_dt, bt_re, bt_im)))
        h = h2

    g2_last = saved[-1]["g2"]

    def loss_fn(rv, cr):
        y, tgt, pre = rv
        err = y - tgt
        part = 0.5 * jnp.sum(jnp.mean(err * err, axis=-1, keepdims=True), axis=0, keepdims=True)
        dpre, dg, db = _layer_norm_bwd(err * (1.0 / dm), pre, cr[0][...])
        return [dpre], [jnp.broadcast_to(part, (1, LANES)), dg, db, _colsum(dpre)]

    (dpre2,), (loss_acc, dg2, db2, dbff2) = _rows_call(
        "loss", loss_fn, [h, loss_target.reshape(t_rows, dm), saved[-1]["pre2"]], [g2_last], [(dm, f32)],
        [(1, LANES), (1, dm), (1, dm), (1, dm)], tl)
    loss = loss_acc[0, 0]
    if distributed:
        loss = lax.psum(loss, MESH_AXES)

    big_parts = {n: [None] * depth for n in BIG_NAMES}
    small_parts = [None] * depth
    grad_handles, ff_handles = [None] * depth, [None] * depth
    grad_modes = ["scatter"] * len(MIXER_NAMES) + ["gather"]
    grad_x = None
    for l in reversed(range(depth)):
        sv = saved[l]

        def ff2_bwd(rv, cr):
            r = rv[1].astype(F32)
            relu2 = jnp.where(r > 0.0, (2.0 * r) * lax.rsqrt(r), 0.0)
            da = _mm_nt(rv[0], cr[0][...]) * relu2
            return [da], [_colsum(da)]

        (dact,), (dbff1,) = _rows_call("ff2_bwd", ff2_bwd, [dpre2, sv["act"]], [sv["w_ff2"]], [(dff, MXU_DTYPE)], [(1, dff)], tm)
        big_parts["w_ff2"][l] = _wgrad_call("wgrad_ff2", sv["act"], dpre2, "x", ndev, ndev,
                                            (dff // ndev, dm), tm)
        big_parts["w_ff1"][l] = _wgrad_call("wgrad_ff1", sv["h1"], dact, "y", ndev, ndev,
                                            (dm, dff // ndev), tm)
        ff_grads = [big_parts[n][l] for n in FF_NAMES]
        if distributed:
            ff_handles[l], ff_token = _exchange_start(f"grads_ff_start_l{l}", ff_grads, ["scatter"] * len(FF_NAMES))
        else:
            ff_handles[l], ff_token = ff_grads, no_token

        def ff1_bwd(rv, cr):
            dacc = alpha * rv[1]
            wpb = dff // ndev
            for j in range(ndev):
                dacc = dacc + _mm_nt(rv[0][:, j * wpb:(j + 1) * wpb], cr[0][j])
            dpre, dg, db = _layer_norm_bwd(dacc, rv[2], cr[1][...])
            return [dpre], [dg, db, _colsum(dpre)]

        (dpre1,), (dg1, db1, dbout) = _rows_call("ff1_bwd", ff1_bwd, [dact, dpre2, sv["pre1"]],
                                                 [sv["w_ff1"], sv["g1"], ff_token], [(dm, f32)], [(1, dm)] * 3, tm)
        big_parts["w_out"][l] = _wgrad_call("wgrad_out", sv["mixed"], dpre1, "x", ndev, ndev, (dm // ndev, dm), tl)

        def mix_bwd(rv, cr):
            dp, at, yp = rv
            w_out_r, w_glu_r, bg, ga_, gs_ = cr
            dmixed = _mm_nt(dp, w_out_r[...])
            g = _gelu(yp)
            sig = jax.nn.sigmoid(_mm(g, w_glu_r[...]) + bg[...])
            ssm = g * sig
            dat, dga = _rms_norm_bwd(dmixed[:, :aw], at, ga_[...])
            dssm, dgs = _rms_norm_bwd(dmixed[:, aw:], ssm, gs_[...])
            dz = dssm * g * sig * (1.0 - sig)
            dg = dssm * sig + _mm_nt(dz, w_glu_r[...])
            return [dat, dg * _gelu_grad(yp), dz], [dga, dgs, _colsum(dz)]

        (dattn, dypre, dz), (dga, dgs, dbglu) = _rows_call(
            "mix_bwd", mix_bwd, [dpre1, sv["attn"], sv["ypre"]], [sv["w_out"], sv["w_glu"], sv["b_glu"], sv["ga"], sv["gs"]],
            [(aw, f32), (sw, f32), (sw, MXU_DTYPE)], [(1, aw), (1, sw), (1, sw)], tl)
        big_parts["w_glu"][l] = _wgrad_call("wgrad_glu", sv["ypre"], dz, "x", 1, 1, (sw, sw), tl, prologue=_gelu).reshape(
            ndev, sw // ndev, sw)

        du3, dbre, dbim, dcre, dcim, dlr, dli, dd = _s5_bwd(
            sv["proj3"], 3 * aw // sw, dypre.reshape(bsz, seq, sw), sv["xr3"], sv["xi3"], *sv["lam"], *sv["s5c"], sv["dvec"], sw)
        dbb_re, dc_re = _s5_unblock(dbre, dcre, ng, gdim, npst, ncb)
        dbb_im, dc_im = _s5_unblock(dbim, dcim, ng, gdim, npst, ncb)
        da_re, da_im, dldt, dbt_re, dbt_im = _s5_params_bwd(
            *sv["s5in"], (dlr.reshape(ng, 1, npst), dli.reshape(ng, 1, npst), dbb_re, dbb_im))

        dq3, dk3, dv3 = _attn_bwd(sv["proj3"], ctab, stab, dattn.reshape(bsz, seq, aw), sv["attn3"], sv["lse3"], aw)
        dproj = [dq3.reshape(t_rows, aw), dk3.reshape(t_rows, aw), dv3.reshape(t_rows, aw), du3.reshape(t_rows, sw)]
        big_parts["w_in"][l] = _wgrad_call("wgrad_in", sv["h"], dproj, "y", ndev, ndev, (dm, in_w // ndev), tl)

        small_parts[l] = dict(
            attn_gain=dga, ssm_gain=dgs, ssm_a_re=da_re[:, 0], ssm_a_im=da_im[:, 0], ssm_log_dt=dldt[:, 0, 0], ssm_b_re=dbt_re.transpose(0, 2, 1),
            ssm_b_im=dbt_im.transpose(0, 2, 1), ssm_c_re=dc_re, ssm_c_im=dc_im, ssm_d=dd, b_glu=dbglu, b_out=dbout, ln1_g=dg1,
            ln1_b=db1, b_ff1=dbff1, b_ff2=dbff2, ln2_g=dg2, ln2_b=db2)

        layer_grads = [big_parts[n][l] for n in MIXER_NAMES] + [_pack_small([small_parts[l][n] for n in SMALL_NAMES])]
        if distributed:
            grad_handles[l], token = _exchange_start(f"grads_start_l{l}", layer_grads, grad_modes)
        else:
            grad_handles[l], token = layer_grads, no_token

        wpb = in_w // ndev
        if l > 0:
            prev = saved[l - 1]

            def in_bwd(rv, cr):
                dacc = alpha * rv[4]
                dp = jnp.concatenate([v_.astype(MXU_DTYPE) for v_ in rv[:4]], axis=-1)
                for j in range(ndev):
                    dacc = dacc + _mm_nt(dp[:, j * wpb:(j + 1) * wpb], cr[0][j])
                dpre, dg, db = _layer_norm_bwd(dacc, rv[5], cr[1][...])
                return [dpre], [dg, db, _colsum(dpre)]

            (dpre2,), (dg2, db2, dbff2) = _rows_call("in_bwd", in_bwd, dproj + [dpre1, prev["pre2"]],
                                                     [sv["w_in"], prev["g2"], token], [(dm, f32)], [(1, dm)] * 3, tl)
        else:
            def in_bwd0(rv, cr):
                dacc = alpha * rv[4]
                dp = jnp.concatenate([v_.astype(MXU_DTYPE) for v_ in rv[:4]], axis=-1)
                for j in range(ndev):
                    dacc = dacc + _mm_nt(dp[:, j * wpb:(j + 1) * wpb], cr[0][j])
                return [dacc], []

            (grad_x,), _ = _rows_call("in_bwd0", in_bwd0, dproj + [dpre1], [sv["w_in"], token], [(dm, f32)], [], tl)

    small_shapes = [weights[n].shape[1:] for n in SMALL_NAMES]
    outs = {n: None for n in BIG_NAMES}
    packed = [_pack_small([d[n] for n in SMALL_NAMES], lead=1) for d in (weights, moments_m, moments_v)]
    small_out = None
    after = grad_x
    for l in reversed(range(depth)):
        if distributed:
            recv_ff = _exchange_wait(f"grads_ff_wait_l{l}", ff_handles[l], ["scatter"] * len(FF_NAMES), after)
            recv = _exchange_wait(f"grads_wait_l{l}", grad_handles[l], grad_modes, recv_ff[0])
        else:
            recv_ff = ff_handles[l]
            recv = [g_[None] if i == len(MIXER_NAMES) else g_ for i, g_ in enumerate(grad_handles[l])]
        for n, parts in zip(FF_NAMES + MIXER_NAMES, list(recv_ff) + list(recv[:-1]), strict=True):
            outs[n] = _adamw_layer("adamw_" + n, parts, weights[n], moments_m[n], moments_v[n], l, outs[n],
                                   min(parts.shape[1], 256))
        small_out = _adamw_layer("adamw_small", recv[-1], *packed, l, small_out, recv[-1].shape[1])
        after = small_out[0]
    for k in range(4):
        row = 0
        for n, shp in zip(SMALL_NAMES, small_shapes, strict=True):
            sz = int(np.prod(shp))
            nrow = _piece_rows(sz)
            piece = small_out[k][:, row:row + nrow].reshape(depth, nrow * LANES)
            outs.setdefault(n, [None] * 4)
            outs[n][k] = piece[:, :sz].reshape((depth,) + tuple(shp))
            row += nrow

    result = [loss, grad_x.reshape(bsz, seq, dm)]
    for k in range(4):
        result += [outs[n][k] for n in WEIGHT_ORDER]
    return tuple(result)


def kernel(x, positions, w_in, attn_gain, ssm_gain, ssm_a_re, ssm_a_im, ssm_log_dt, ssm_b_re, ssm_b_im, ssm_c_re, ssm_c_im, ssm_d, w_glu, b_glu, w_out, b_out, ln1_g, ln1_b, w_ff1, b_ff1, w_ff2, b_ff2, ln2_g, ln2_b, loss_target, m_w_in, m_attn_gain, m_ssm_gain, m_ssm_a_re, m_ssm_a_im, m_ssm_log_dt, m_ssm_b_re, m_ssm_b_im, m_ssm_c_re, m_ssm_c_im, m_ssm_d, m_w_glu, m_b_glu, m_w_out, m_b_out, m_ln1_g, m_ln1_b, m_w_ff1, m_b_ff1, m_w_ff2, m_b_ff2, m_ln2_g, m_ln2_b, v_w_in, v_attn_gain, v_ssm_gain, v_ssm_a_re, v_ssm_a_im, v_ssm_log_dt, v_ssm_b_re, v_ssm_b_im, v_ssm_c_re, v_ssm_c_im, v_ssm_d, v_w_glu, v_b_glu, v_w_out, v_b_out, v_ln1_g, v_ln1_b, v_w_ff1, v_b_ff1, v_w_ff2, v_b_ff2, v_ln2_g, v_ln2_b):
    loc = locals()
    weights = {n: loc[n] for n in WEIGHT_ORDER}
    moments_m = {n: loc["m_" + n] for n in WEIGHT_ORDER}
    moments_v = {n: loc["v_" + n] for n in WEIGHT_ORDER}
    return _step(x, positions, weights, moments_m, moments_v, loss_target, distributed=True)
```

```python
import functools
import math

import jax
import jax.numpy as jnp
import numpy as np
from jax import lax
from jax.experimental import pallas as pl
from jax.experimental.pallas import tpu as pltpu

F32 = jnp.float32
MXU_DTYPE = jnp.bfloat16

HEAD_DIM = 64
DILATION_PAIRS = ((128, 1), (512, 4), (2048, 16))
ROPE_THETA = 10000.0
LN_EPS = 1e-5
RMS_EPS = 1e-6
NEG_INF = -1e30
ADAM_LR, ADAM_B1, ADAM_B2, ADAM_EPS, ADAM_WD, ADAM_STEP = 0.001, 0.9, 0.999, 1e-08, 0.01, 10

LANES = 128
SUBLANES = 8
QBLK = 128
VMEM_LIMIT = 56 * 2**20
MESH_AXES = ("x", "y", "c")
NDEV = 8

SMALL_NAMES = ("attn_gain", "ssm_gain", "ssm_a_re", "ssm_a_im", "ssm_log_dt", "ssm_b_re", "ssm_b_im", "ssm_c_re",
               "ssm_c_im", "ssm_d", "b_glu", "b_out", "ln1_g", "ln1_b", "b_ff1", "b_ff2", "ln2_g", "ln2_b")
BIG_NAMES = ("w_in", "w_glu", "w_out", "w_ff1", "w_ff2")
MIXER_NAMES, FF_NAMES = BIG_NAMES[:3], BIG_NAMES[3:]
WEIGHT_ORDER = ("w_in", "attn_gain", "ssm_gain", "ssm_a_re", "ssm_a_im", "ssm_log_dt", "ssm_b_re", "ssm_b_im", "ssm_c_re",
                "ssm_c_im", "ssm_d", "w_glu", "b_glu", "w_out", "b_out", "ln1_g", "ln1_b", "w_ff1", "b_ff1", "w_ff2",
                "b_ff2", "ln2_g", "ln2_b")


def _cparams(sem=None):
    return pltpu.CompilerParams(dimension_semantics=sem, vmem_limit_bytes=VMEM_LIMIT)


def _mm(a, b):
    return jnp.dot(a.astype(MXU_DTYPE), b.astype(MXU_DTYPE), preferred_element_type=F32)


def _mm_nt(a, b):
    return lax.dot_general(a.astype(MXU_DTYPE), b.astype(MXU_DTYPE), (((1,), (1,)), ((), ())), preferred_element_type=F32)


def _mm_tn(a, b):
    return lax.dot_general(a.astype(MXU_DTYPE), b.astype(MXU_DTYPE), (((0,), (0,)), ((), ())), preferred_element_type=F32)


def _colsum(x):
    return jnp.sum(x, axis=0, keepdims=True)


def _layer_norm(x, g, b):
    mu = jnp.mean(x, axis=-1, keepdims=True)
    xc = x - mu
    var = jnp.mean(xc * xc, axis=-1, keepdims=True)
    return xc * lax.rsqrt(var + LN_EPS) * g + b


def _layer_norm_bwd(dy, pre, g):
    mu = jnp.mean(pre, axis=-1, keepdims=True)
    xc = pre - mu
    var = jnp.mean(xc * xc, axis=-1, keepdims=True)
    r = lax.rsqrt(var + LN_EPS)
    xhat = xc * r
    dyg = dy * g
    dpre = r * (dyg - jnp.mean(dyg, axis=-1, keepdims=True) - xhat * jnp.mean(dyg * xhat, axis=-1, keepdims=True))
    return dpre, _colsum(dy * xhat), _colsum(dy)


def _rms_norm(x, g):
    return x * lax.rsqrt(jnp.mean(x * x, axis=-1, keepdims=True) + RMS_EPS) * g


def _rms_norm_bwd(dy, x, g):
    r = lax.rsqrt(jnp.mean(x * x, axis=-1, keepdims=True) + RMS_EPS)
    dyg = dy * g
    dx = dyg * r - x * (r * r * r) * jnp.mean(dyg * x, axis=-1, keepdims=True)
    return dx, _colsum(dy * x * r)


_GELU_C = math.sqrt(2.0 / math.pi)


def _gelu(x):
    return 0.5 * x * (1.0 + jnp.tanh(_GELU_C * (x + 0.044715 * (x * x * x))))


def _gelu_grad(x):
    t = jnp.tanh(_GELU_C * (x + 0.044715 * (x * x * x)))
    return 0.5 * (1.0 + t) + 0.5 * x * (1.0 - t * t) * (_GELU_C * (1.0 + 3.0 * 0.044715 * x * x))


def _rows_call(name, fn, rows, consts, out_rows, out_accs, tm):
    rows = [r if isinstance(r, tuple) else (r, r.shape[1], 0) for r in rows]
    m = rows[0][0].shape[0]
    assert m % tm == 0
    nr, nc, no, na = len(rows), len(consts), len(out_rows), len(out_accs)

    def body(*refs):
        rr, cr = refs[:nr], refs[nr:nr + nc]
        orr, ar = refs[nr + nc:nr + nc + no], refs[nr + nc + no:]
        outs, accs = fn([r[...] for r in rr], cr)
        for o, v in zip(orr, outs, strict=True):
            o[...] = v.astype(o.dtype)
        if na:
            first = pl.program_id(0) == 0

            @pl.when(first)
            def _():
                for a, v in zip(ar, accs, strict=True):
                    a[...] = v

            @pl.when(jnp.logical_not(first))
            def _():
                for a, v in zip(ar, accs, strict=True):
                    a[...] += v

    def whole(shape):
        return pl.BlockSpec(shape, lambda i, n=len(shape): (0,) * n)

    in_specs = [pl.BlockSpec((tm, w), lambda i, cb=cb: (i, cb)) for _, w, cb in rows] + [whole(c.shape) for c in consts]
    out_specs = [pl.BlockSpec((tm, w), lambda i: (i, 0)) for w, _ in out_rows] + [whole(s) for s in out_accs]
    out_shape = [jax.ShapeDtypeStruct((m, w), dt) for w, dt in out_rows] + [jax.ShapeDtypeStruct(s, F32) for s in out_accs]
    res = pl.pallas_call(body, grid=(m // tm,), in_specs=in_specs, out_specs=out_specs, out_shape=out_shape, name=name,
                         compiler_params=_cparams(("arbitrary",)))(*[r[0] for r in rows], *consts)
    return res[:no], res[no:]


def _wgrad_call(name, x, dy, split, nblk, jb, blk_shape, tm, prologue=None):
    dys = list(dy) if isinstance(dy, (list, tuple)) else [dy]
    m = x.shape[0]
    kk, nn = blk_shape
    assert m % tm == 0 and nblk % jb == 0 and (len(dys) == 1 or (split == "y" and jb == nblk))
    xw = kk * jb if split == "x" else x.shape[1]
    yws = [d.shape[1] for d in dys] if (split == "x" or len(dys) > 1) else [nn * jb]
    nrow = m // tm

    def body(x_ref, *rest):
        dy_refs, o_ref, acc_ref = rest[:len(dys)], rest[-2], rest[-1]
        i = pl.program_id(1)

        @pl.when(i == 0)
        def _():
            acc_ref[...] = jnp.zeros_like(acc_ref)

        xv = x_ref[...]
        if prologue is not None:
            xv = prologue(xv)
        xv = xv.astype(MXU_DTYPE)
        dv = [r[...].astype(MXU_DTYPE) for r in dy_refs]
        dv = dv[0] if len(dv) == 1 else jnp.concatenate(dv, axis=-1)
        for j in range(jb):
            xa = xv[:, j * kk:(j + 1) * kk] if split == "x" else xv
            da = dv[:, j * nn:(j + 1) * nn] if split == "y" else dv
            acc_ref[j] += _mm_tn(xa, da)

        @pl.when(i == nrow - 1)
        def _():
            o_ref[...] = acc_ref[...].astype(o_ref.dtype)

    in_specs = [pl.BlockSpec((tm, xw), (lambda j, i: (i, j)) if split == "x" else (lambda j, i: (i, 0)))]
    in_specs += [pl.BlockSpec((tm, yw), (lambda j, i: (i, j)) if (split == "y" and len(dys) == 1) else (lambda j, i: (i, 0)))
                 for yw in yws]
    return pl.pallas_call(
        body, grid=(nblk // jb, nrow), in_specs=in_specs,
        out_specs=pl.BlockSpec((jb, kk, nn), lambda j, i: (j, 0, 0)),
        out_shape=jax.ShapeDtypeStruct((nblk, kk, nn), MXU_DTYPE),
        scratch_shapes=[pltpu.VMEM((jb, kk, nn), F32)], name=name,
        compiler_params=_cparams(("arbitrary", "arbitrary")))(x, *dys)


_HBM_SPEC = pl.BlockSpec(memory_space=pltpu.HBM)
_SEM_SPEC = pl.BlockSpec(memory_space=pltpu.SEMAPHORE)
_EFFECT = pltpu.SideEffectType.DATAFLOW_SIDE_EFFECTING


def _my_index():
    return 4 * lax.axis_index("x") + 2 * lax.axis_index("y") + lax.axis_index("c")


def _peer_copies(ins, lands, send_sems, recv_sems, modes):
    x, y, c = lax.axis_index("x"), lax.axis_index("y"), lax.axis_index("c")
    me = 4 * x + 2 * y + c
    pairs = []
    for k in range(NDEV - 1):
        fx, fy, fc = ((k + 1) >> 2) & 1, ((k + 1) >> 1) & 1, (k + 1) & 1
        px, py, pc = (x + fx) % 2, (y + fy) % 2, (c + fc) % 2
        idx = 4 * px + 2 * py + pc
        for a, md in enumerate(modes):
            src = ins[a] if md == "gather" else ins[a].at[idx]
            sem = a * (NDEV - 1) + k
            common = dict(src_ref=src, send_sem=send_sems.at[sem], recv_sem=recv_sems.at[sem], device_id=(px, py, pc),
                          device_id_type=pl.DeviceIdType.MESH)
            pairs.append((pltpu.make_async_remote_copy(dst_ref=lands[a].at[me], **common),
                          pltpu.make_async_remote_copy(dst_ref=lands[a].at[idx], **common)))
    return pairs


def _exchange_start(name, arrays, modes, after=None):
    n = len(arrays)
    extra = [] if after is None else [after]
    me = _my_index()
    lands = []
    for a, md in zip(arrays, modes, strict=True):
        piece = a if md == "gather" else lax.dynamic_index_in_dim(a, me, 0, keepdims=False)
        lands.append(lax.dynamic_update_index_in_dim(lax.empty((NDEV,) + piece.shape, a.dtype), piece, me, 0))

    def body(*refs):
        ins, lnd = refs[:n], refs[n:2 * n]
        send_sems, recv_sems = refs[2 * n + len(extra)], refs[2 * n + len(extra) + 1]
        token = refs[-1]
        for out_copy, _ in _peer_copies(ins, lnd, send_sems, recv_sems, modes):
            out_copy.start()
        token[...] = jnp.zeros_like(token)

    sems = pltpu.SemaphoreType.DMA((n * (NDEV - 1),))
    thru = [pltpu.HBM(a.shape, a.dtype) for a in list(arrays) + lands]
    res = pl.pallas_call(
        body, name=name, out_shape=(sems, sems, *thru, jax.ShapeDtypeStruct((SUBLANES, LANES), F32)),
        in_specs=[_HBM_SPEC] * (2 * n) + [pl.BlockSpec(memory_space=pl.ANY)] * len(extra),
        out_specs=(_SEM_SPEC, _SEM_SPEC, *([_HBM_SPEC] * (2 * n)), pl.BlockSpec(memory_space=pltpu.VMEM)),
        input_output_aliases={i: 2 + i for i in range(2 * n)},
        compiler_params=pltpu.CompilerParams(has_side_effects=_EFFECT),
    )(*[pltpu.with_memory_space_constraint(a, pltpu.HBM) for a in list(arrays) + lands], *extra)
    return (res[0], res[1], res[2:2 + n], res[2 + n:2 + 2 * n]), res[-1]


def _exchange_wait(name, handle, modes, after):
    send_sems, recv_sems, ins_thru, lands_thru = handle
    n = len(ins_thru)

    def body(*refs):
        ins, lnd = refs[:n], refs[n:2 * n]
        for out_copy, arrival in _peer_copies(ins, lnd, refs[2 * n], refs[2 * n + 1], modes):
            out_copy.wait_send()
            arrival.wait_recv()

    thru = [pltpu.HBM(a.shape, a.dtype) for a in list(ins_thru) + list(lands_thru)]
    res = pl.pallas_call(
        body, name=name, out_shape=tuple(thru),
        in_specs=[_HBM_SPEC] * (2 * n) + [_SEM_SPEC, _SEM_SPEC, pl.BlockSpec(memory_space=pl.ANY)],
        out_specs=tuple([_HBM_SPEC] * (2 * n)), input_output_aliases={i: i for i in range(2 * n)},
        compiler_params=pltpu.CompilerParams(has_side_effects=_EFFECT),
    )(*ins_thru, *lands_thru, send_sems, recv_sems, after)
    return res[n:]


def _rope_tables(positions3, inv_freq_row, sign_row):
    b, s, _ = positions3.shape

    def body(pos_ref, f_ref, sg_ref, c_ref, s_ref):
        ang = pos_ref[...].astype(F32) * f_ref[...]
        c_ref[...] = jnp.cos(ang)
        s_ref[...] = jnp.sin(ang) * sg_ref[...]

    row = pl.BlockSpec((1, LANES), lambda i: (0, 0))
    blk = pl.BlockSpec((None, s, LANES), lambda i: (i, 0, 0))
    return pl.pallas_call(
        body, grid=(b,), in_specs=[pl.BlockSpec((None, s, 1), lambda i: (i, 0, 0)), row, row], out_specs=[blk, blk],
        out_shape=[jax.ShapeDtypeStruct((b, s, LANES), F32)] * 2, name="rope_tables",
        compiler_params=_cparams(("arbitrary",)))(positions3, inv_freq_row, sign_row)


def _swap_halves(t):
    lane = lax.broadcasted_iota(jnp.int32, t.shape, 1)
    half = HEAD_DIM // 2
    return jnp.where((lane % HEAD_DIM) < half, pltpu.roll(t, LANES - half, 1), pltpu.roll(t, half, 1))


def _segment_rows(r, d, s):
    n = s // d
    return (pl.ds(r, n, stride=d) if d > 1 else pl.ds(0, s)), pl.ds(r * n, n)


def _head_lanes(shape):
    lane = lax.broadcasted_iota(jnp.int32, shape, len(shape) - 1)
    return lane < HEAD_DIM


def _bmm_nt(a, b):
    return lax.dot_general(a.astype(MXU_DTYPE), b.astype(MXU_DTYPE), (((2,), (2,)), ((0,), (0,))), preferred_element_type=F32)


def _bmm(a, b):
    return lax.dot_general(a.astype(MXU_DTYPE), b.astype(MXU_DTYPE), (((2,), (1,)), ((0,), (0,))), preferred_element_type=F32)


def _bmm_tn(a, b):
    return lax.dot_general(a.astype(MXU_DTYPE), b.astype(MXU_DTYPE), (((1,), (1,)), ((0,), (0,))), preferred_element_type=F32)


def _stack_heads(t3):
    head_a = _head_lanes(t3.shape)
    zero = jnp.zeros_like(t3)
    return jnp.concatenate([jnp.where(head_a, t3, zero), jnp.where(head_a, zero, t3)], axis=1)


QUNIT = QBLK // 2


def _with_previous(t3, nprev):
    shifted = [jnp.concatenate([t3[:k], t3[:-k]], axis=0) for k in range(nprev, 0, -1)]
    return jnp.concatenate(shifted + [t3], axis=1)


def _head_columns(t3):
    return jnp.concatenate([t3[:, :, lo:lo + 1] for lo in range(0, LANES, HEAD_DIM)], axis=1)


def _to_own_unit(t, nprev):
    unit = t.shape[1] // (nprev + 1)
    out = t[:, nprev * unit:]
    for k in range(1, nprev + 1):
        part = t[:, (nprev - k) * unit:(nprev - k + 1) * unit]
        out = out + jnp.concatenate([part[k:], jnp.zeros_like(part[:k])], axis=0)
    return out


def _branch_operands(qh, kh, vh, s, nb):
    nh = LANES // HEAD_DIM
    unit, nprev = (QBLK, 0) if nb == 1 else (QUNIT, QBLK // QUNIT)
    g = s // unit
    q3 = _stack_heads(qh[...].reshape(g, unit, LANES))
    k3, v3 = kh[...].reshape(g, unit, LANES), vh[...].reshape(g, unit, LANES)
    if nprev == 0:
        qi = lax.broadcasted_iota(jnp.int32, (1, nh * unit, unit), 1) % unit
        kj = lax.broadcasted_iota(jnp.int32, (1, nh * unit, unit), 2)
        return q3, k3, v3, kj <= qi, nprev
    shape = (g, nh * unit, (nprev + 1) * unit)
    qi = lax.broadcasted_iota(jnp.int32, shape, 1) % unit
    kj = lax.broadcasted_iota(jnp.int32, shape, 2)
    j = lax.broadcasted_iota(jnp.int32, shape, 0)
    per_block = QBLK // unit
    opens = ((j // per_block) % nb) == 0
    mask = (kj >= qi) & (kj <= qi + QBLK) & ((kj >= QBLK - unit * (j % per_block)) | jnp.logical_not(opens))
    return q3, _with_previous(k3, nprev), _with_previous(v3, nprev), mask, nprev


def _attn_fwd(proj3, ctab, stab, aw):
    b, s, _ = proj3.shape
    npair = aw // LANES
    scale = HEAD_DIM ** -0.5
    nbr = len(DILATION_PAIRS)

    def body(q_ref, k_ref, v_ref, c_ref, s_ref, o_ref, l_ref, qf, kf, qh, kh, vh, op, lp, ob, lb):
        cc, ss = c_ref[...], s_ref[...]
        q2, k2 = q_ref[...], k_ref[...]
        qf[...] = (q2 * cc + _swap_halves(q2) * ss) * scale
        kf[...] = k2 * cc + _swap_halves(k2) * ss
        for br, (window, d) in enumerate(DILATION_PAIRS):
            for r in range(d):
                nat, perm = _segment_rows(r, d, s)
                for dst, src in ((qh, qf), (kh, kf), (vh, v_ref)):
                    dst[perm, :] = src[nat, :].astype(MXU_DTYPE)
            o_dst, l_dst = (ob.at[br], lb.at[br]) if d == 1 else (op, lp)
            q3, kk, vv, mask, _ = _branch_operands(qh, kh, vh, s, (s // d) // QBLK)
            unit = q3.shape[1] // (LANES // HEAD_DIM)
            head_a = _head_lanes((q3.shape[0], unit, LANES))
            sc = jnp.where(mask, _bmm_nt(q3, kk), NEG_INF)
            mx = jnp.max(sc, axis=-1, keepdims=True)
            p = jnp.exp(sc - mx)
            den = jnp.sum(p, axis=-1, keepdims=True)
            o2 = _bmm(p, vv) / den
            l2 = mx + jnp.log(den)
            o_dst[...] = jnp.where(head_a, o2[:, :unit], o2[:, unit:]).reshape(s, LANES)
            l_dst[...] = jnp.where(head_a, l2[:, :unit], l2[:, unit:]).reshape(s, LANES)
            if d > 1:
                for r in range(d):
                    nat, perm = _segment_rows(r, d, s)
                    ob[br, nat, :] = op[perm, :]
                    lb[br, nat, :] = lp[perm, :]
        ls = [lb[br] for br in range(nbr)]
        mx = functools.reduce(jnp.maximum, ls)
        ws = [jnp.exp(l - mx) for l in ls]
        tot = functools.reduce(lambda a, b_: a + b_, ws)
        o_ref[...] = functools.reduce(lambda a, b_: a + b_, [(w / tot) * ob[br] for br, w in enumerate(ws)])
        l_ref[...] = mx + jnp.log(tot)

    def col(off):
        return pl.BlockSpec((None, s, LANES), lambda bi, hp, off=off: (bi, 0, off + hp))

    tab = pl.BlockSpec((None, s, LANES), lambda bi, hp: (bi, 0, 0))
    f32s = pltpu.VMEM((s, LANES), F32)
    mxs = pltpu.VMEM((s, LANES), MXU_DTYPE)
    br_s = pltpu.VMEM((nbr, s, LANES), F32)
    return pl.pallas_call(
        body, grid=(b, npair), in_specs=[col(0), col(npair), col(2 * npair), tab, tab], out_specs=[col(0), col(0)],
        out_shape=[jax.ShapeDtypeStruct((b, s, aw), F32)] * 2,
        scratch_shapes=[f32s] * 2 + [mxs] * 3 + [f32s] * 2 + [br_s] * 2,
        name="attn_fwd", compiler_params=_cparams(("arbitrary", "arbitrary")))(proj3, proj3, proj3, ctab, stab)


def _attn_bwd(proj3, ctab, stab, dout3, out3, lse3, aw):
    b, s, _ = proj3.shape
    npair = aw // LANES
    scale = HEAD_DIM ** -0.5
    nheads = LANES // HEAD_DIM

    def body(q_ref, k_ref, v_ref, c_ref, s_ref, do_ref, o_ref, l_ref, dq_ref, dk_ref, dv_ref,
             qf, kf, dlf, qh, kh, vh, doh, lpm, dpm, dqp, dkp, dvp, dqn, dkn, dvn):
        cc, ss = c_ref[...], s_ref[...]
        q2, k2 = q_ref[...], k_ref[...]
        qf[...] = (q2 * cc + _swap_halves(q2) * ss) * scale
        kf[...] = k2 * cc + _swap_halves(k2) * ss
        dd = do_ref[...] * o_ref[...]
        in_a = _head_lanes((s, LANES))
        sum_a = jnp.sum(jnp.where(in_a, dd, 0.0), axis=-1, keepdims=True)
        sum_b = jnp.sum(jnp.where(in_a, 0.0, dd), axis=-1, keepdims=True)
        dlf[...] = jnp.where(in_a, sum_a, sum_b)
        for r_ in (dqn, dkn, dvn):
            r_[...] = jnp.zeros_like(r_)
        for window, d in DILATION_PAIRS:
            for r in range(d):
                nat, perm = _segment_rows(r, d, s)
                for dst, src in ((qh, qf), (kh, kf), (vh, v_ref), (doh, do_ref)):
                    dst[perm, :] = src[nat, :].astype(MXU_DTYPE)
                if d > 1:
                    lpm[perm, :] = l_ref[nat, :]
                    dpm[perm, :] = dlf[nat, :]
            l_src, d_src = (l_ref, dlf) if d == 1 else (lpm, dpm)
            q3, kk, vv, mask, nprev = _branch_operands(qh, kh, vh, s, (s // d) // QBLK)
            g_, unit = q3.shape[0], q3.shape[1] // nheads
            head_a = _head_lanes((g_, unit, LANES))
            do3 = _stack_heads(doh[...].reshape(g_, unit, LANES))
            lcol, dcol = _head_columns(l_src[...].reshape(g_, unit, LANES)), _head_columns(d_src[...].reshape(g_, unit, LANES))
            p = jnp.exp(jnp.where(mask, _bmm_nt(q3, kk), NEG_INF) - lcol)
            ds_ = p * (_bmm_nt(do3, vv) - dcol)
            dq2 = _bmm(ds_, kk)
            dq_new = jnp.where(head_a, dq2[:, :unit], dq2[:, unit:]).reshape(s, LANES)
            dk_new = _to_own_unit(_bmm_tn(ds_, q3), nprev).reshape(s, LANES)
            dv_new = _to_own_unit(_bmm_tn(p, do3), nprev).reshape(s, LANES)
            if d == 1:
                dqn[...] += dq_new
                dkn[...] += dk_new
                dvn[...] += dv_new
            else:
                dqp[...] = dq_new
                dkp[...] = dk_new
                dvp[...] = dv_new
                for r in range(d):
                    nat, perm = _segment_rows(r, d, s)
                    dqn[nat, :] += dqp[perm, :]
                    dkn[nat, :] += dkp[perm, :]
                    dvn[nat, :] += dvp[perm, :]
        g = dqn[...] * scale
        dq_ref[...] = (g * cc + _swap_halves(g * ss)).astype(dq_ref.dtype)
        g = dkn[...]
        dk_ref[...] = (g * cc + _swap_halves(g * ss)).astype(dk_ref.dtype)
        dv_ref[...] = dvn[...].astype(dv_ref.dtype)

    def col(off):
        return pl.BlockSpec((None, s, LANES), lambda bi, hp, off=off: (bi, 0, off + hp))

    tab = pl.BlockSpec((None, s, LANES), lambda bi, hp: (bi, 0, 0))
    f32s = pltpu.VMEM((s, LANES), F32)
    mxs = pltpu.VMEM((s, LANES), MXU_DTYPE)
    return pl.pallas_call(
        body, grid=(b, npair), in_specs=[col(0), col(npair), col(2 * npair), tab, tab, col(0), col(0), col(0)],
        out_specs=[col(0)] * 3, out_shape=[jax.ShapeDtypeStruct((b, s, aw), MXU_DTYPE)] * 3,
        scratch_shapes=[f32s] * 3 + [mxs] * 4 + [f32s] * 8,
        name="attn_bwd", compiler_params=_cparams(("arbitrary", "arbitrary")))(
            proj3, proj3, proj3, ctab, stab, dout3, out3, lse3)


def _s5_discretise(a_re, a_im, log_dt, bt_re, bt_im):
    dt = jnp.exp(log_dt)
    mag = jnp.exp(a_re * dt)
    ang = a_im * dt
    lb_re = mag * jnp.cos(ang)
    lb_im = mag * jnp.sin(ang)
    den = a_re * a_re + a_im * a_im
    nr = lb_re - 1.0
    ni = lb_im
    cr = (nr * a_re + ni * a_im) / den
    ci = (ni * a_re - nr * a_im) / den
    return lb_re, lb_im, cr * bt_re - ci * bt_im, cr * bt_im + ci * bt_re


def _s5_params(a_re, a_im, log_dt, bt_re, bt_im):
    def body(ar, ai, ld, br, bi, o1, o2, o3, o4):
        r = _s5_discretise(ar[...], ai[...], ld[...], br[...], bi[...])
        for o, v in zip((o1, o2, o3, o4), r, strict=True):
            o[...] = v

    sd = jax.ShapeDtypeStruct
    return pl.pallas_call(body, out_shape=[sd(a_re.shape, F32)] * 2 + [sd(bt_re.shape, F32)] * 2, name="s5_params")(
        a_re, a_im, log_dt, bt_re, bt_im)


def _s5_params_bwd(a_re, a_im, log_dt, bt_re, bt_im, cts):
    def body(ar, ai, ld, br, bi, c1, c2, c3, c4, o1, o2, o3, o4, o5):
        _, vjp = jax.vjp(_s5_discretise, ar[...], ai[...], ld[...], br[...], bi[...])
        r = vjp((c1[...], c2[...], c3[...], c4[...]))
        for o, v in zip((o1, o2, o3, o4, o5), r, strict=True):
            o[...] = v

    sd = jax.ShapeDtypeStruct
    return pl.pallas_call(
        body, out_shape=[sd(a_re.shape, F32)] * 2 + [sd(log_dt.shape, F32)] + [sd(bt_re.shape, F32)] * 2, name="s5_params_bwd")(
            a_re, a_im, log_dt, bt_re, bt_im, *cts)


S5_TC = 128


def _time_major(tiles):
    nt, tc, lanes = tiles.shape
    return jnp.swapaxes(tiles, 0, 1).reshape(tc * nt, lanes)


def _tile_major(rows, nt):
    return jnp.swapaxes(rows.astype(MXU_DTYPE).reshape(rows.shape[0] // nt, nt, rows.shape[1]), 0, 1)


def _s5_fwd(proj3, ucol, lam_re, lam_im, bre, bim, cre, cim, dvec, sw):
    b, s, _ = proj3.shape
    nt = lam_re.shape[0]
    ncb = sw // LANES
    tpc = nt // ncb
    tc = S5_TC

    def body(u_ref, lr_ref, li_ref, bre_ref, bim_ref, cre_ref, cim_ref, d_ref, y_ref, xr_ref, xi_ref, sr, si):
        @pl.when(pl.program_id(0) == 0)
        def _():
            sr[...] = jnp.zeros_like(sr)
            si[...] = jnp.zeros_like(si)

        u_all = u_ref[...].reshape(b * tc, sw)
        bur, bui = [], []
        for cb in range(ncb):
            ucb = u_all[:, cb * LANES:(cb + 1) * LANES].astype(MXU_DTYPE)
            for t in range(cb * tpc, (cb + 1) * tpc):
                bur.append(_mm(ucb, bre_ref[t]))
                bui.append(_mm(ucb, bim_ref[t]))
        for bi in range(b):
            rows = slice(bi * tc, (bi + 1) * tc)
            xr_ref[bi] = _time_major(jnp.stack([v_[rows] for v_ in bur]))
            xi_ref[bi] = _time_major(jnp.stack([v_[rows] for v_ in bui]))
        lr, li = lr_ref[...], li_ref[...]

        def step(j, carry):
            off = pl.multiple_of(j * nt, nt)
            new = []
            for bi in range(b):
                pr, pi = carry[2 * bi], carry[2 * bi + 1]
                nr = lr * pr - li * pi + xr_ref[bi, pl.ds(off, nt), :]
                ni = lr * pi + li * pr + xi_ref[bi, pl.ds(off, nt), :]
                xr_ref[bi, pl.ds(off, nt), :] = nr
                xi_ref[bi, pl.ds(off, nt), :] = ni
                new += [nr, ni]
            return tuple(new)

        init = tuple(v for bi in range(b) for v in (sr[bi], si[bi]))
        fin = lax.fori_loop(0, tc, step, init, unroll=4)
        for bi in range(b):
            sr[bi] = fin[2 * bi]
            si[bi] = fin[2 * bi + 1]
        xr_t = [_tile_major(xr_ref[bi], nt) for bi in range(b)]
        xi_t = [_tile_major(xi_ref[bi], nt) for bi in range(b)]
        for cb in range(ncb):
            cols = slice(cb * LANES, (cb + 1) * LANES)
            acc = d_ref[:, cols] * u_all[:, cols]
            for t in range(cb * tpc, (cb + 1) * tpc):
                xr_all = jnp.concatenate([xr_t[bi][t] for bi in range(b)], axis=0)
                xi_all = jnp.concatenate([xi_t[bi][t] for bi in range(b)], axis=0)
                acc = acc + (_mm(xr_all, cre_ref[t]) - _mm(xi_all, cim_ref[t]))
            for bi in range(b):
                y_ref[bi, :, cols] = acc[bi * tc:(bi + 1) * tc]

    def whole(a):
        return pl.BlockSpec(a.shape, lambda c, n=a.ndim: (0,) * n)

    xblk = pl.BlockSpec((b, tc * nt, LANES), lambda c: (0, c, 0))
    return pl.pallas_call(
        body, grid=(s // tc,),
        in_specs=[pl.BlockSpec((b, tc, sw), lambda c: (0, c, ucol))] + [whole(a) for a in (lam_re, lam_im, bre, bim, cre, cim, dvec)],
        out_specs=[pl.BlockSpec((b, tc, sw), lambda c: (0, c, 0)), xblk, xblk],
        out_shape=[jax.ShapeDtypeStruct((b, s, sw), F32)] + [jax.ShapeDtypeStruct((b, s * nt, LANES), F32)] * 2,
        scratch_shapes=[pltpu.VMEM((b, nt, LANES), F32)] * 2, name="s5_fwd",
        compiler_params=_cparams(("arbitrary",)))(proj3, lam_re, lam_im, bre, bim, cre, cim, dvec)


def _s5_bwd(proj3, ucol, dy3, xr3, xi3, lam_re, lam_im, bre, bim, cre, cim, dvec, sw):
    b, s, _ = proj3.shape
    nt = lam_re.shape[0]
    ncb = sw // LANES
    tpc = nt // ncb
    tc = S5_TC
    nchunk = s // tc

    def body(u_ref, dy_ref, xr_ref, xi_ref, pr_ref, pi_ref, lr_ref, li_ref, bre_ref, bim_ref, cre_ref, cim_ref, d_ref,
             du_ref, dbre, dbim, dcre, dcim, dlr, dli, dd, gr, gi, sr, si):
        step_id = pl.program_id(0)

        @pl.when(step_id == 0)
        def _():
            for r in (sr, si, dbre, dbim, dcre, dcim, dlr, dli, dd):
                r[...] = jnp.zeros_like(r)

        u_all = u_ref[...].reshape(b * tc, sw)
        dy_all = dy_ref[...].reshape(b * tc, sw)
        dxr, dxi = [], []
        for cb in range(ncb):
            dycb = dy_all[:, cb * LANES:(cb + 1) * LANES].astype(MXU_DTYPE)
            for t in range(cb * tpc, (cb + 1) * tpc):
                dxr.append(_mm_nt(dycb, cre_ref[t]))
                dxi.append(-_mm_nt(dycb, cim_ref[t]))
        for bi in range(b):
            rows = slice(bi * tc, (bi + 1) * tc)
            gr[bi] = _time_major(jnp.stack([v_[rows] for v_ in dxr]))
            gi[bi] = _time_major(jnp.stack([v_[rows] for v_ in dxi]))
        lr, li = lr_ref[...], li_ref[...]

        has_prev = (step_id != nchunk - 1).astype(F32)
        before = [(pr_ref[bi] * has_prev, pi_ref[bi] * has_prev) for bi in range(b)]

        def step(jj, carry):
            t = tc - 1 - jj
            off = pl.multiple_of(t * nt, nt)
            poff = pl.multiple_of(jnp.maximum(t - 1, 0) * nt, nt)
            new = []
            alr, ali = carry[-2], carry[-1]
            for bi in range(b):
                nr_, ni_ = carry[2 * bi], carry[2 * bi + 1]
                vr = gr[bi, pl.ds(off, nt), :] + lr * nr_ + li * ni_
                vi = gi[bi, pl.ds(off, nt), :] + lr * ni_ - li * nr_
                gr[bi, pl.ds(off, nt), :] = vr
                gi[bi, pl.ds(off, nt), :] = vi
                xpr = jnp.where(t > 0, xr_ref[bi, pl.ds(poff, nt), :], before[bi][0])
                xpi = jnp.where(t > 0, xi_ref[bi, pl.ds(poff, nt), :], before[bi][1])
                alr = alr + (vr * xpr + vi * xpi)
                ali = ali + (vi * xpr - vr * xpi)
                new += [vr, vi]
            return tuple(new) + (alr, ali)

        zero = jnp.zeros((nt, LANES), F32)
        init = tuple(v for bi in range(b) for v in (sr[bi], si[bi])) + (zero, zero)
        fin = lax.fori_loop(0, tc, step, init, unroll=4)
        for bi in range(b):
            sr[bi] = fin[2 * bi]
            si[bi] = fin[2 * bi + 1]
        dlr[...] += fin[-2]
        dli[...] += fin[-1]

        tiles = [[_tile_major(ref[bi], nt) for bi in range(b)] for ref in (gr, gi, xr_ref, xi_ref)]

        def stacked(k, t):
            return jnp.concatenate([tiles[k][bi][t] for bi in range(b)], axis=0)

        for cb in range(ncb):
            cols = slice(cb * LANES, (cb + 1) * LANES)
            ucb32, dycb32 = u_all[:, cols], dy_all[:, cols]
            ucb, dycb = ucb32.astype(MXU_DTYPE), dycb32.astype(MXU_DTYPE)
            acc = d_ref[:, cols] * dycb32
            for t in range(cb * tpc, (cb + 1) * tpc):
                vr, vi = stacked(0, t), stacked(1, t)
                acc = acc + (_mm_nt(vr, bre_ref[t]) + _mm_nt(vi, bim_ref[t]))
                dbre[t] += _mm_tn(ucb, vr)
                dbim[t] += _mm_tn(ucb, vi)
                dcre[t] += _mm_tn(stacked(2, t), dycb)
                dcim[t] -= _mm_tn(stacked(3, t), dycb)
            for bi in range(b):
                du_ref[bi, :, cols] = acc[bi * tc:(bi + 1) * tc].astype(du_ref.dtype)
            dd[:, cols] += _colsum(dycb32 * ucb32)

    def whole(a):
        return pl.BlockSpec(a.shape, lambda c, n=len(a.shape): (0,) * n)

    def rev(c):
        return nchunk - 1 - c

    xblk = pl.BlockSpec((b, tc * nt, LANES), lambda c: (0, rev(c), 0))
    prev = pl.BlockSpec((b, nt, LANES), lambda c: (0, jnp.maximum(rev(c) * tc - 1, 0), 0))
    sd = jax.ShapeDtypeStruct
    blk = sd(bre.shape, F32)
    acc_shapes = [blk, blk, sd(cre.shape, F32), sd(cre.shape, F32), sd(lam_re.shape, F32), sd(lam_re.shape, F32), sd((1, sw), F32)]
    return pl.pallas_call(
        body, grid=(nchunk,),
        in_specs=[pl.BlockSpec((b, tc, sw), lambda c: (0, rev(c), ucol)), pl.BlockSpec((b, tc, sw), lambda c: (0, rev(c), 0)),
                  xblk, xblk, prev, prev] + [whole(a) for a in (lam_re, lam_im, bre, bim, cre, cim, dvec)],
        out_specs=[pl.BlockSpec((b, tc, sw), lambda c: (0, rev(c), 0))] + [whole(a) for a in acc_shapes],
        out_shape=[sd((b, s, sw), MXU_DTYPE)] + acc_shapes,
        scratch_shapes=[pltpu.VMEM((b, tc * nt, LANES), F32)] * 2 + [pltpu.VMEM((b, nt, LANES), F32)] * 2, name="s5_bwd",
        compiler_params=_cparams(("arbitrary",)))(proj3, dy3, xr3, xi3, xr3, xi3, lam_re, lam_im, bre, bim, cre, cim, dvec)


def _s5_blocks(bb, cc_, ncb):
    g, n, p = bb.shape
    mask = jnp.asarray(_s5_tile_mask(g, p, ncb))
    nt, gpt, gpc = mask.shape
    bbt, cct = bb.reshape(nt, gpt, n, p), cc_.reshape(nt, gpt, n, p)
    spread = mask[:, :, :, None, None]
    bblk = (bbt[:, :, None] * spread).transpose(0, 2, 3, 1, 4).reshape(nt, gpc * n, gpt * p)
    cblk = (cct[:, :, None] * spread).transpose(0, 1, 4, 2, 3).reshape(nt, gpt * p, gpc * n)
    return bblk, cblk


def _s5_tile_mask(g, p, ncb):
    nt, gpt, gpc = g * p // LANES, LANES // p, g // ncb
    mask = np.zeros((nt, gpt, gpc), np.float32)
    for t in range(nt):
        for gl in range(gpt):
            mask[t, gl, (t * gpt + gl) % gpc] = 1.0
    return mask


def _s5_unblock(dbblk, dcblk, g, n, p, ncb):
    mask = jnp.asarray(_s5_tile_mask(g, p, ncb))
    nt, gpt, gpc = mask.shape
    db = jnp.sum(dbblk.reshape(nt, gpc, n, gpt, p) * mask.transpose(0, 2, 1)[:, :, None, :, None], axis=1)
    dc = jnp.sum(dcblk.reshape(nt, gpt, p, gpc, n) * mask[:, :, None, :, None], axis=3)
    return db.transpose(0, 2, 1, 3).reshape(g, n, p), dc.transpose(0, 1, 3, 2).reshape(g, n, p)


def _adamw_math(w, g, m, v):
    m = ADAM_B1 * m + (1.0 - ADAM_B1) * g
    v = ADAM_B2 * v + (1.0 - ADAM_B2) * (g * g)
    m_hat = m / (1.0 - ADAM_B1 ** ADAM_STEP)
    v_hat = v / (1.0 - ADAM_B2 ** ADAM_STEP)
    delta = -ADAM_LR * (m_hat / (jnp.sqrt(v_hat) + ADAM_EPS) + ADAM_WD * w)
    return delta, m, v


def _adamw_layer(name, parts, w, m, v, layer, prev, tr):
    nparts, r, c = parts.shape
    assert r % tr == 0
    if prev is None:
        prev = [lax.empty(w.shape, F32) for _ in range(4)]

    def body(p_ref, w_ref, m_ref, v_ref, *rest):
        g_out, d_out, m_out, v_out = rest[4:]
        g = p_ref[0].astype(F32)
        for k in range(1, nparts):
            g = g + p_ref[k].astype(F32)
        delta, mn, vn = _adamw_math(w_ref[...], g, m_ref[...], v_ref[...])
        g_out[...] = g
        d_out[...] = delta
        m_out[...] = mn
        v_out[...] = vn

    blk = pl.BlockSpec((None, tr, c), lambda i: (layer, i, 0))
    kept = pl.BlockSpec(memory_space=pl.ANY)
    return pl.pallas_call(
        body, grid=(r // tr,), in_specs=[pl.BlockSpec((nparts, tr, c), lambda i: (0, i, 0)), blk, blk, blk] + [kept] * 4,
        out_specs=[blk] * 4, out_shape=[jax.ShapeDtypeStruct(w.shape, F32)] * 4, name=name,
        input_output_aliases={4 + k: k for k in range(4)},
        compiler_params=_cparams(("arbitrary",)))(parts, w, m, v, *prev)


def _piece_rows(size):
    unit = SUBLANES * LANES
    return -(-size // unit) * SUBLANES


def _pack_small(pieces, lead=0):
    rows = []
    for p in pieces:
        head = p.shape[:lead]
        flat = p.reshape(head + (-1,))
        nrow = _piece_rows(flat.shape[-1])
        flat = jnp.pad(flat, [(0, 0)] * lead + [(0, nrow * LANES - flat.shape[-1])])
        rows.append(flat.reshape(head + (nrow, LANES)))
    return jnp.concatenate(rows, axis=lead)


def _step(x, positions, weights, moments_m, moments_v, loss_target, distributed):
    f32 = F32
    bsz, seq, dm = x.shape
    depth = weights["w_in"].shape[0]
    aw = weights["attn_gain"].shape[1]
    sw = weights["ssm_gain"].shape[1]
    dff = weights["b_ff1"].shape[1]
    ng, npst = weights["ssm_a_re"].shape[1:]
    gdim = weights["ssm_d"].shape[2]
    in_w = 3 * aw + sw
    t_rows = bsz * seq
    alpha = (2.0 * depth) ** 0.25
    ncb = sw // LANES
    nt = ng * npst // LANES
    ndev = NDEV if distributed else 1
    tm, tl = 512, 1024

    gather_modes = ["gather"] * len(BIG_NAMES)
    no_token = jnp.zeros((SUBLANES, LANES), f32)

    def shards(l, names):
        return [weights[n][l].astype(MXU_DTYPE) for n in names]

    if distributed:
        first, _ = _exchange_start("weights_start_l0_in", shards(0, BIG_NAMES[:1]), gather_modes[:1])

    half = HEAD_DIM // 2
    inv_freq = ROPE_THETA ** (-jnp.arange(half, dtype=f32) * 2.0 / HEAD_DIM)
    reps = LANES // half
    inv_row = jnp.tile(inv_freq, reps)[None, :]
    sign_row = jnp.tile(jnp.concatenate([-jnp.ones((half,), f32), jnp.ones((half,), f32)]), LANES // HEAD_DIM)[None, :]
    ctab, stab = _rope_tables(positions[..., None], inv_row, sign_row)

    def row(v):
        return v.reshape(1, -1)

    s5_prep = []
    for l in range(depth):
        a_re, a_im = weights["ssm_a_re"][l][:, None, :], weights["ssm_a_im"][l][:, None, :]
        log_dt = weights["ssm_log_dt"][l][:, None, None]
        bt_re = weights["ssm_b_re"][l].transpose(0, 2, 1)
        bt_im = weights["ssm_b_im"][l].transpose(0, 2, 1)
        lb_re, lb_im, bb_re, bb_im = _s5_params(a_re, a_im, log_dt, bt_re, bt_im)
        bre, cre = _s5_blocks(bb_re, weights["ssm_c_re"][l], ncb)
        bim, cim = _s5_blocks(bb_im, weights["ssm_c_im"][l], ncb)
        s5_prep.append(dict(lam=(lb_re.reshape(nt, LANES), lb_im.reshape(nt, LANES)),
                            s5c=[a_.astype(MXU_DTYPE) for a_ in (bre, bim, cre, cim)], dvec=row(weights["ssm_d"][l]),
                            s5in=(a_re, a_im, log_dt, bt_re, bt_im)))

    if distributed:
        (g_in0,) = _exchange_wait("weights_wait_l0_in", first, gather_modes[:1], s5_prep[-1]["s5c"][-1])
        rest0, tok_rest0 = _exchange_start("weights_start_l0_rest", shards(0, BIG_NAMES[1:]), gather_modes[1:], after=g_in0)

    h = x.reshape(t_rows, dm)
    saved = []
    for l in range(depth):
        tok_in = tok_mix = no_token
        if not distributed:
            g_in, g_glu, g_out, g_ff1, g_ff2 = [weights[n][l].astype(MXU_DTYPE)[None] for n in BIG_NAMES]
        elif l == 0:
            g_in, tok_in = g_in0, tok_rest0
        else:
            g_in, g_glu, g_out, g_ff1, g_ff2 = _exchange_wait(f"weights_wait_l{l}", next_gather, gather_modes, h)
            if l + 1 < depth:
                next_gather, tok_in = _exchange_start(f"weights_start_l{l + 1}", shards(l + 1, BIG_NAMES), gather_modes, after=g_in)
        w_in_l = g_in

        def in_proj(rv, cr):
            return [jnp.concatenate([_mm(rv[0], cr[0][j]) for j in range(ndev)], axis=-1)], []

        (proj,), _ = _rows_call("in_proj", in_proj, [h], [w_in_l, tok_in], [(in_w, f32)], [], tl)
        proj3 = proj.reshape(bsz, seq, in_w)
        attn3, lse3 = _attn_fwd(proj3, ctab, stab, aw)
        if distributed and l == 0:
            g_glu, g_out, g_ff1, g_ff2 = _exchange_wait("weights_wait_l0_rest", rest0, gather_modes[1:], attn3)
            if depth > 1:
                next_gather, tok_mix = _exchange_start("weights_start_l1", shards(1, BIG_NAMES), gather_modes, after=g_glu)
        w_glu_l = g_glu.reshape(sw, sw)
        w_out_l = g_out.reshape(dm, dm)
        w_ff1_l = g_ff1
        w_ff2_l = g_ff2.reshape(dff, dm)

        (lam_re, lam_im), s5c, dvec = s5_prep[l]["lam"], s5_prep[l]["s5c"], s5_prep[l]["dvec"]
        ypre3, xr3, xi3 = _s5_fwd(proj3, 3 * aw // sw, lam_re, lam_im, *s5c, dvec, sw)
        attn, ypre = attn3.reshape(t_rows, aw), ypre3.reshape(t_rows, sw)

        b_glu, ga, gs = row(weights["b_glu"][l]), row(weights["attn_gain"][l]), row(weights["ssm_gain"][l])

        def mix(rv, cr):
            at, yp = rv
            g = _gelu(yp)
            ssm = g * jax.nn.sigmoid(_mm(g, cr[0][...]) + cr[1][...])
            return [jnp.concatenate([_rms_norm(at, cr[2][...]), _rms_norm(ssm, cr[3][...])], axis=-1)], []

        (mixed,), _ = _rows_call("mix", mix, [attn, ypre], [w_glu_l, b_glu, ga, gs, tok_mix], [(dm, MXU_DTYPE)], [], tl)

        b_out, g1, b1 = row(weights["b_out"][l]), row(weights["ln1_g"][l]), row(weights["ln1_b"][l])

        def out_proj(rv, cr):
            pre = alpha * rv[1] + _mm(rv[0], cr[0][...]) + cr[1][...]
            return [pre, _layer_norm(pre, cr[2][...], cr[3][...])], []

        (pre1, h1), _ = _rows_call("out_proj", out_proj, [mixed, h], [w_out_l, b_out, g1, b1], [(dm, f32), (dm, f32)], [], tl)

        b_ff1 = row(weights["b_ff1"][l])

        def ff1(rv, cr):
            pre_act = jnp.concatenate([_mm(rv[0], cr[0][j]) for j in range(ndev)], axis=-1) + cr[1][...]
            return [jnp.square(jnp.maximum(pre_act, 0.0))], []

        (act,), _ = _rows_call("ff1", ff1, [h1], [w_ff1_l, b_ff1], [(dff, MXU_DTYPE)], [], tm)

        b_ff2, g2, b2 = row(weights["b_ff2"][l]), row(weights["ln2_g"][l]), row(weights["ln2_b"][l])

        def ff2(rv, cr):
            pre = alpha * rv[1] + _mm(rv[0], cr[0][...]) + cr[1][...]
            return [pre, _layer_norm(pre, cr[2][...], cr[3][...])], []

        (pre2, h2), _ = _rows_call("ff2", ff2, [act, h1], [w_ff2_l, b_ff2, g2, b2], [(dm, f32), (dm, f32)], [], tm)

        saved.append(dict(h=h, proj3=proj3, attn=attn, lse3=lse3, attn3=attn3, ypre=ypre, xr3=xr3, xi3=xi3, mixed=mixed, pre1=pre1,
                          h1=h1, act=act, pre2=pre2, w_in=w_in_l, w_glu=w_glu_l, w_out=w_out_l, w_ff1=w_ff1_l, w_ff2=w_ff2_l,
                          lam=(lam_re, lam_im), s5c=s5c, dvec=dvec, b_glu=b_glu, ga=ga, gs=gs, g1=g1, g2=g2,
                          s5in=s5_prep[l]["s5in"]))
        h = h2

    g2_last = saved[-1]["g2"]

    def loss_fn(rv, cr):
        y, tgt, pre = rv
        err = y - tgt
        part = 0.5 * jnp.sum(jnp.mean(err * err, axis=-1, keepdims=True), axis=0, keepdims=True)
        dpre, dg, db = _layer_norm_bwd(err * (1.0 / dm), pre, cr[0][...])
        return [dpre], [jnp.broadcast_to(part, (1, LANES)), dg, db, _colsum(dpre)]

    (dpre2,), (loss_acc, dg2, db2, dbff2) = _rows_call(
        "loss", loss_fn, [h, loss_target.reshape(t_rows, dm), saved[-1]["pre2"]], [g2_last], [(dm, f32)],
        [(1, LANES), (1, dm), (1, dm), (1, dm)], tl)
    loss = loss_acc[0, 0]
    if distributed:
        loss = lax.psum(loss, MESH_AXES)

    big_parts = {n: [None] * depth for n in BIG_NAMES}
    small_parts = [None] * depth
    grad_handles, ff_handles = [None] * depth, [None] * depth
    grad_modes = ["scatter"] * len(MIXER_NAMES) + ["gather"]
    grad_x = None
    for l in reversed(range(depth)):
        sv = saved[l]

        def ff2_bwd(rv, cr):
            r = rv[1].astype(F32)
            relu2 = jnp.where(r > 0.0, (2.0 * r) * lax.rsqrt(r), 0.0)
            da = _mm_nt(rv[0], cr[0][...]) * relu2
            return [da], [_colsum(da)]

        (dact,), (dbff1,) = _rows_call("ff2_bwd", ff2_bwd, [dpre2, sv["act"]], [sv["w_ff2"]], [(dff, MXU_DTYPE)], [(1, dff)], tm)
        big_parts["w_ff2"][l] = _wgrad_call("wgrad_ff2", sv["act"], dpre2, "x", ndev, ndev,
                                            (dff // ndev, dm), tm)
        big_parts["w_ff1"][l] = _wgrad_call("wgrad_ff1", sv["h1"], dact, "y", ndev, ndev,
                                            (dm, dff // ndev), tm)
        ff_grads = [big_parts[n][l] for n in FF_NAMES]
        if distributed:
            ff_handles[l], ff_token = _exchange_start(f"grads_ff_start_l{l}", ff_grads, ["scatter"] * len(FF_NAMES))
        else:
            ff_handles[l], ff_token = ff_grads, no_token

        def ff1_bwd(rv, cr):
            dacc = alpha * rv[1]
            wpb = dff // ndev
            for j in range(ndev):
                dacc = dacc + _mm_nt(rv[0][:, j * wpb:(j + 1) * wpb], cr[0][j])
            dpre, dg, db = _layer_norm_bwd(dacc, rv[2], cr[1][...])
            return [dpre], [dg, db, _colsum(dpre)]

        (dpre1,), (dg1, db1, dbout) = _rows_call("ff1_bwd", ff1_bwd, [dact, dpre2, sv["pre1"]],
                                                 [sv["w_ff1"], sv["g1"], ff_token], [(dm, f32)], [(1, dm)] * 3, tm)
        big_parts["w_out"][l] = _wgrad_call("wgrad_out", sv["mixed"], dpre1, "x", ndev, ndev, (dm // ndev, dm), tl)

        def mix_bwd(rv, cr):
            dp, at, yp = rv
            w_out_r, w_glu_r, bg, ga_, gs_ = cr
            dmixed = _mm_nt(dp, w_out_r[...])
            g = _gelu(yp)
            sig = jax.nn.sigmoid(_mm(g, w_glu_r[...]) + bg[...])
            ssm = g * sig
            dat, dga = _rms_norm_bwd(dmixed[:, :aw], at, ga_[...])
            dssm, dgs = _rms_norm_bwd(dmixed[:, aw:], ssm, gs_[...])
            dz = dssm * g * sig * (1.0 - sig)
            dg = dssm * sig + _mm_nt(dz, w_glu_r[...])
            return [dat, dg * _gelu_grad(yp), dz], [dga, dgs, _colsum(dz)]

        (dattn, dypre, dz), (dga, dgs, dbglu) = _rows_call(
            "mix_bwd", mix_bwd, [dpre1, sv["attn"], sv["ypre"]], [sv["w_out"], sv["w_glu"], sv["b_glu"], sv["ga"], sv["gs"]],
            [(aw, f32), (sw, f32), (sw, MXU_DTYPE)], [(1, aw), (1, sw), (1, sw)], tl)
        big_parts["w_glu"][l] = _wgrad_call("wgrad_glu", sv["ypre"], dz, "x", 1, 1, (sw, sw), tl, prologue=_gelu).reshape(
            ndev, sw // ndev, sw)

        du3, dbre, dbim, dcre, dcim, dlr, dli, dd = _s5_bwd(
            sv["proj3"], 3 * aw // sw, dypre.reshape(bsz, seq, sw), sv["xr3"], sv["xi3"], *sv["lam"], *sv["s5c"], sv["dvec"], sw)
        dbb_re, dc_re = _s5_unblock(dbre, dcre, ng, gdim, npst, ncb)
        dbb_im, dc_im = _s5_unblock(dbim, dcim, ng, gdim, npst, ncb)
        da_re, da_im, dldt, dbt_re, dbt_im = _s5_params_bwd(
            *sv["s5in"], (dlr.reshape(ng, 1, npst), dli.reshape(ng, 1, npst), dbb_re, dbb_im))

        dq3, dk3, dv3 = _attn_bwd(sv["proj3"], ctab, stab, dattn.reshape(bsz, seq, aw), sv["attn3"], sv["lse3"], aw)
        dproj = [dq3.reshape(t_rows, aw), dk3.reshape(t_rows, aw), dv3.reshape(t_rows, aw), du3.reshape(t_rows, sw)]
        big_parts["w_in"][l] = _wgrad_call("wgrad_in", sv["h"], dproj, "y", ndev, ndev, (dm, in_w // ndev), tl)

        small_parts[l] = dict(
            attn_gain=dga, ssm_gain=dgs, ssm_a_re=da_re[:, 0], ssm_a_im=da_im[:, 0], ssm_log_dt=dldt[:, 0, 0], ssm_b_re=dbt_re.transpose(0, 2, 1),
            ssm_b_im=dbt_im.transpose(0, 2, 1), ssm_c_re=dc_re, ssm_c_im=dc_im, ssm_d=dd, b_glu=dbglu, b_out=dbout, ln1_g=dg1,
            ln1_b=db1, b_ff1=dbff1, b_ff2=dbff2, ln2_g=dg2, ln2_b=db2)

        layer_grads = [big_parts[n][l] for n in MIXER_NAMES] + [_pack_small([small_parts[l][n] for n in SMALL_NAMES])]
        if distributed:
            grad_handles[l], token = _exchange_start(f"grads_start_l{l}", layer_grads, grad_modes)
        else:
            grad_handles[l], token = layer_grads, no_token

        wpb = in_w // ndev
        if l > 0:
            prev = saved[l - 1]

            def in_bwd(rv, cr):
                dacc = alpha * rv[4]
                dp = jnp.concatenate([v_.astype(MXU_DTYPE) for v_ in rv[:4]], axis=-1)
                for j in range(ndev):
                    dacc = dacc + _mm_nt(dp[:, j * wpb:(j + 1) * wpb], cr[0][j])
                dpre, dg, db = _layer_norm_bwd(dacc, rv[5], cr[1][...])
                return [dpre], [dg, db, _colsum(dpre)]

            (dpre2,), (dg2, db2, dbff2) = _rows_call("in_bwd", in_bwd, dproj + [dpre1, prev["pre2"]],
                                                     [sv["w_in"], prev["g2"], token], [(dm, f32)], [(1, dm)] * 3, tl)
        else:
            def in_bwd0(rv, cr):
                dacc = alpha * rv[4]
                dp = jnp.concatenate([v_.astype(MXU_DTYPE) for v_ in rv[:4]], axis=-1)
                for j in range(ndev):
                    dacc = dacc + _mm_nt(dp[:, j * wpb:(j + 1) * wpb], cr[0][j])
                return [dacc], []

            (grad_x,), _ = _rows_call("in_bwd0", in_bwd0, dproj + [dpre1], [sv["w_in"], token], [(dm, f32)], [], tl)

    small_shapes = [weights[n].shape[1:] for n in SMALL_NAMES]
    outs = {n: None for n in BIG_NAMES}
    packed = [_pack_small([d[n] for n in SMALL_NAMES], lead=1) for d in (weights, moments_m, moments_v)]
    small_out = None
    after = grad_x
    for l in reversed(range(depth)):
        if distributed:
            recv_ff = _exchange_wait(f"grads_ff_wait_l{l}", ff_handles[l], ["scatter"] * len(FF_NAMES), after)
            recv = _exchange_wait(f"grads_wait_l{l}", grad_handles[l], grad_modes, recv_ff[0])
        else:
            recv_ff = ff_handles[l]
            recv = [g_[None] if i == len(MIXER_NAMES) else g_ for i, g_ in enumerate(grad_handles[l])]
        for n, parts in zip(FF_NAMES + MIXER_NAMES, list(recv_ff) + list(recv[:-1]), strict=True):
            outs[n] = _adamw_layer("adamw_" + n, parts, weights[n], moments_m[n], moments_v[n], l, outs[n],
                                   min(parts.shape[1], 256))
        small_out = _adamw_layer("adamw_small", recv[-1], *packed, l, small_out, recv[-1].shape[1])
        after = small_out[0]
    for k in range(4):
        row = 0
        for n, shp in zip(SMALL_NAMES, small_shapes, strict=True):
            sz = int(np.prod(shp))
            nrow = _piece_rows(sz)
            piece = small_out[k][:, row:row + nrow].reshape(depth, nrow * LANES)
            outs.setdefault(n, [None] * 4)
            outs[n][k] = piece[:, :sz].reshape((depth,) + tuple(shp))
            row += nrow

    result = [loss, grad_x.reshape(bsz, seq, dm)]
    for k in range(4):
        result += [outs[n][k] for n in WEIGHT_ORDER]
    return tuple(result)


def kernel(x, positions, w_in, attn_gain, ssm_gain, ssm_a_re, ssm_a_im, ssm_log_dt, ssm_b_re, ssm_b_im, ssm_c_re, ssm_c_im, ssm_d, w_glu, b_glu, w_out, b_out, ln1_g, ln1_b, w_ff1, b_ff1, w_ff2, b_ff2, ln2_g, ln2_b, loss_target, m_w_in, m_attn_gain, m_ssm_gain, m_ssm_a_re, m_ssm_a_im, m_ssm_log_dt, m_ssm_b_re, m_ssm_b_im, m_ssm_c_re, m_ssm_c_im, m_ssm_d, m_w_glu, m_b_glu, m_w_out, m_b_out, m_ln1_g, m_ln1_b, m_w_ff1, m_b_ff1, m_w_ff2, m_b_ff2, m_ln2_g, m_ln2_b, v_w_in, v_attn_gain, v_ssm_gain, v_ssm_a_re, v_ssm_a_im, v_ssm_log_dt, v_ssm_b_re, v_ssm_b_im, v_ssm_c_re, v_ssm_c_im, v_ssm_d, v_w_glu, v_b_glu, v_w_out, v_b_out, v_ln1_g, v_ln1_b, v_w_ff1, v_b_ff1, v_w_ff2, v_b_ff2, v_ln2_g, v_ln2_b):
    loc = locals()
    weights = {n: loc[n] for n in WEIGHT_ORDER}
    moments_m = {n: loc["m_" + n] for n in WEIGHT_ORDER}
    moments_v = {n: loc["v_" + n] for n in WEIGHT_ORDER}
    return _step(x, positions, weights, moments_m, moments_v, loss_target, distributed=True)
```

```python
import functools
import math

import jax
import jax.numpy as jnp
import numpy as np
from jax import lax
from jax.experimental import pallas as pl
from jax.experimental.pallas import tpu as pltpu

F32 = jnp.float32
MXU_DTYPE = jnp.bfloat16

HEAD_DIM = 64
DILATION_PAIRS = ((128, 1), (512, 4), (2048, 16))
ROPE_THETA = 10000.0
LN_EPS = 1e-5
RMS_EPS = 1e-6
NEG_INF = -1e30
ADAM_LR, ADAM_B1, ADAM_B2, ADAM_EPS, ADAM_WD, ADAM_STEP = 0.001, 0.9, 0.999, 1e-08, 0.01, 10

LANES = 128
SUBLANES = 8
QBLK = 128
VMEM_LIMIT = 56 * 2**20
MESH_AXES = ("x", "y", "c")
NDEV = 8

SMALL_NAMES = ("attn_gain", "ssm_gain", "ssm_a_re", "ssm_a_im", "ssm_log_dt", "ssm_b_re", "ssm_b_im", "ssm_c_re",
               "ssm_c_im", "ssm_d", "b_glu", "b_out", "ln1_g", "ln1_b", "b_ff1", "b_ff2", "ln2_g", "ln2_b")
BIG_NAMES = ("w_in", "w_glu", "w_out", "w_ff1", "w_ff2")
MIXER_NAMES, FF_NAMES = BIG_NAMES[:3], BIG_NAMES[3:]
WEIGHT_ORDER = ("w_in", "attn_gain", "ssm_gain", "ssm_a_re", "ssm_a_im", "ssm_log_dt", "ssm_b_re", "ssm_b_im", "ssm_c_re",
                "ssm_c_im", "ssm_d", "w_glu", "b_glu", "w_out", "b_out", "ln1_g", "ln1_b", "w_ff1", "b_ff1", "w_ff2",
                "b_ff2", "ln2_g", "ln2_b")


def _cparams(sem=None):
    return pltpu.CompilerParams(dimension_semantics=sem, vmem_limit_bytes=VMEM_LIMIT)


def _mm(a, b):
    return jnp.dot(a.astype(MXU_DTYPE), b.astype(MXU_DTYPE), preferred_element_type=F32)


def _mm_nt(a, b):
    return lax.dot_general(a.astype(MXU_DTYPE), b.astype(MXU_DTYPE), (((1,), (1,)), ((), ())), preferred_element_type=F32)


def _mm_tn(a, b):
    return lax.dot_general(a.astype(MXU_DTYPE), b.astype(MXU_DTYPE), (((0,), (0,)), ((), ())), preferred_element_type=F32)


def _colsum(x):
    return jnp.sum(x, axis=0, keepdims=True)


def _layer_norm(x, g, b):
    mu = jnp.mean(x, axis=-1, keepdims=True)
    xc = x - mu
    var = jnp.mean(xc * xc, axis=-1, keepdims=True)
    return xc * lax.rsqrt(var + LN_EPS) * g + b


def _layer_norm_bwd(dy, pre, g):
    mu = jnp.mean(pre, axis=-1, keepdims=True)
    xc = pre - mu
    var = jnp.mean(xc * xc, axis=-1, keepdims=True)
    r = lax.rsqrt(var + LN_EPS)
    xhat = xc * r
    dyg = dy * g
    dpre = r * (dyg - jnp.mean(dyg, axis=-1, keepdims=True) - xhat * jnp.mean(dyg * xhat, axis=-1, keepdims=True))
    return dpre, _colsum(dy * xhat), _colsum(dy)


def _rms_norm(x, g):
    return x * lax.rsqrt(jnp.mean(x * x, axis=-1, keepdims=True) + RMS_EPS) * g


def _rms_norm_bwd(dy, x, g):
    r = lax.rsqrt(jnp.mean(x * x, axis=-1, keepdims=True) + RMS_EPS)
    dyg = dy * g
    dx = dyg * r - x * (r * r * r) * jnp.mean(dyg * x, axis=-1, keepdims=True)
    return dx, _colsum(dy * x * r)


_GELU_C = math.sqrt(2.0 / math.pi)


def _gelu(x):
    return 0.5 * x * (1.0 + jnp.tanh(_GELU_C * (x + 0.044715 * (x * x * x))))


def _gelu_grad(x):
    t = jnp.tanh(_GELU_C * (x + 0.044715 * (x * x * x)))
    return 0.5 * (1.0 + t) + 0.5 * x * (1.0 - t * t) * (_GELU_C * (1.0 + 3.0 * 0.044715 * x * x))


def _rows_call(name, fn, rows, consts, out_rows, out_accs, tm):
    rows = [r if isinstance(r, tuple) else (r, r.shape[1], 0) for r in rows]
    m = rows[0][0].shape[0]
    assert m % tm == 0
    nr, nc, no, na = len(rows), len(consts), len(out_rows), len(out_accs)

    def body(*refs):
        rr, cr = refs[:nr], refs[nr:nr + nc]
        orr, ar = refs[nr + nc:nr + nc + no], refs[nr + nc + no:]
        outs, accs = fn([r[...] for r in rr], cr)
        for o, v in zip(orr, outs, strict=True):
            o[...] = v.astype(o.dtype)
        if na:
            first = pl.program_id(0) == 0

            @pl.when(first)
            def _():
                for a, v in zip(ar, accs, strict=True):
                    a[...] = v

            @pl.when(jnp.logical_not(first))
            def _():
                for a, v in zip(ar, accs, strict=True):
                    a[...] += v

    def whole(shape):
        return pl.BlockSpec(shape, lambda i, n=len(shape): (0,) * n)

    in_specs = [pl.BlockSpec((tm, w), lambda i, cb=cb: (i, cb)) for _, w, cb in rows] + [whole(c.shape) for c in consts]
    out_specs = [pl.BlockSpec((tm, w), lambda i: (i, 0)) for w, _ in out_rows] + [whole(s) for s in out_accs]
    out_shape = [jax.ShapeDtypeStruct((m, w), dt) for w, dt in out_rows] + [jax.ShapeDtypeStruct(s, F32) for s in out_accs]
    res = pl.pallas_call(body, grid=(m // tm,), in_specs=in_specs, out_specs=out_specs, out_shape=out_shape, name=name,
                         compiler_params=_cparams(("arbitrary",)))(*[r[0] for r in rows], *consts)
    return res[:no], res[no:]


def _wgrad_call(name, x, dy, split, nblk, jb, blk_shape, tm, prologue=None):
    dys = list(dy) if isinstance(dy, (list, tuple)) else [dy]
    m = x.shape[0]
    kk, nn = blk_shape
    assert m % tm == 0 and nblk % jb == 0 and (len(dys) == 1 or (split == "y" and jb == nblk))
    xw = kk * jb if split == "x" else x.shape[1]
    yws = [d.shape[1] for d in dys] if (split == "x" or len(dys) > 1) else [nn * jb]
    nrow = m // tm

    def body(x_ref, *rest):
        dy_refs, o_ref, acc_ref = rest[:len(dys)], rest[-2], rest[-1]
        i = pl.program_id(1)

        @pl.when(i == 0)
        def _():
            acc_ref[...] = jnp.zeros_like(acc_ref)

        xv = x_ref[...]
        if prologue is not None:
            xv = prologue(xv)
        xv = xv.astype(MXU_DTYPE)
        dv = [r[...].astype(MXU_DTYPE) for r in dy_refs]
        dv = dv[0] if len(dv) == 1 else jnp.concatenate(dv, axis=-1)
        for j in range(jb):
            xa = xv[:, j * kk:(j + 1) * kk] if split == "x" else xv
            da = dv[:, j * nn:(j + 1) * nn] if split == "y" else dv
            acc_ref[j] += _mm_tn(xa, da)

        @pl.when(i == nrow - 1)
        def _():
            o_ref[...] = acc_ref[...].astype(o_ref.dtype)

    in_specs = [pl.BlockSpec((tm, xw), (lambda j, i: (i, j)) if split == "x" else (lambda j, i: (i, 0)))]
    in_specs += [pl.BlockSpec((tm, yw), (lambda j, i: (i, j)) if (split == "y" and len(dys) == 1) else (lambda j, i: (i, 0)))
                 for yw in yws]
    return pl.pallas_call(
        body, grid=(nblk // jb, nrow), in_specs=in_specs,
        out_specs=pl.BlockSpec((jb, kk, nn), lambda j, i: (j, 0, 0)),
        out_shape=jax.ShapeDtypeStruct((nblk, kk, nn), MXU_DTYPE),
        scratch_shapes=[pltpu.VMEM((jb, kk, nn), F32)], name=name,
        compiler_params=_cparams(("arbitrary", "arbitrary")))(x, *dys)


_HBM_SPEC = pl.BlockSpec(memory_space=pltpu.HBM)
_SEM_SPEC = pl.BlockSpec(memory_space=pltpu.SEMAPHORE)
_EFFECT = pltpu.SideEffectType.DATAFLOW_SIDE_EFFECTING


def _my_index():
    return 4 * lax.axis_index("x") + 2 * lax.axis_index("y") + lax.axis_index("c")


def _peer_copies(ins, lands, send_sems, recv_sems, modes):
    x, y, c = lax.axis_index("x"), lax.axis_index("y"), lax.axis_index("c")
    me = 4 * x + 2 * y + c
    pairs = []
    for k in range(NDEV - 1):
        fx, fy, fc = ((k + 1) >> 2) & 1, ((k + 1) >> 1) & 1, (k + 1) & 1
        px, py, pc = (x + fx) % 2, (y + fy) % 2, (c + fc) % 2
        idx = 4 * px + 2 * py + pc
        for a, md in enumerate(modes):
            src = ins[a] if md == "gather" else ins[a].at[idx]
            sem = a * (NDEV - 1) + k
            common = dict(src_ref=src, send_sem=send_sems.at[sem], recv_sem=recv_sems.at[sem], device_id=(px, py, pc),
                          device_id_type=pl.DeviceIdType.MESH)
            pairs.append((pltpu.make_async_remote_copy(dst_ref=lands[a].at[me], **common),
                          pltpu.make_async_remote_copy(dst_ref=lands[a].at[idx], **common)))
    return pairs


def _exchange_start(name, arrays, modes, after=None):
    n = len(arrays)
    extra = [] if after is None else [after]
    me = _my_index()
    lands = []
    for a, md in zip(arrays, modes, strict=True):
        piece = a if md == "gather" else lax.dynamic_index_in_dim(a, me, 0, keepdims=False)
        lands.append(lax.dynamic_update_index_in_dim(lax.empty((NDEV,) + piece.shape, a.dtype), piece, me, 0))

    def body(*refs):
        ins, lnd = refs[:n], refs[n:2 * n]
        send_sems, recv_sems = refs[2 * n + len(extra)], refs[2 * n + len(extra) + 1]
        token = refs[-1]
        for out_copy, _ in _peer_copies(ins, lnd, send_sems, recv_sems, modes):
            out_copy.start()
        token[...] = jnp.zeros_like(token)

    sems = pltpu.SemaphoreType.DMA((n * (NDEV - 1),))
    thru = [pltpu.HBM(a.shape, a.dtype) for a in list(arrays) + lands]
    res = pl.pallas_call(
        body, name=name, out_shape=(sems, sems, *thru, jax.ShapeDtypeStruct((SUBLANES, LANES), F32)),
        in_specs=[_HBM_SPEC] * (2 * n) + [pl.BlockSpec(memory_space=pl.ANY)] * len(extra),
        out_specs=(_SEM_SPEC, _SEM_SPEC, *([_HBM_SPEC] * (2 * n)), pl.BlockSpec(memory_space=pltpu.VMEM)),
        input_output_aliases={i: 2 + i for i in range(2 * n)},
        compiler_params=pltpu.CompilerParams(has_side_effects=_EFFECT),
    )(*[pltpu.with_memory_space_constraint(a, pltpu.HBM) for a in list(arrays) + lands], *extra)
    return (res[0], res[1], res[2:2 + n], res[2 + n:2 + 2 * n]), res[-1]


def _exchange_wait(name, handle, modes, after):
    send_sems, recv_sems, ins_thru, lands_thru = handle
    n = len(ins_thru)

    def body(*refs):
        ins, lnd = refs[:n], refs[n:2 * n]
        for out_copy, arrival in _peer_copies(ins, lnd, refs[2 * n], refs[2 * n + 1], modes):
            out_copy.wait_send()
            arrival.wait_recv()

    thru = [pltpu.HBM(a.shape, a.dtype) for a in list(ins_thru) + list(lands_thru)]
    res = pl.pallas_call(
        body, name=name, out_shape=tuple(thru),
        in_specs=[_HBM_SPEC] * (2 * n) + [_SEM_SPEC, _SEM_SPEC, pl.BlockSpec(memory_space=pl.ANY)],
        out_specs=tuple([_HBM_SPEC] * (2 * n)), input_output_aliases={i: i for i in range(2 * n)},
        compiler_params=pltpu.CompilerParams(has_side_effects=_EFFECT),
    )(*ins_thru, *lands_thru, send_sems, recv_sems, after)
    return res[n:]


def _rope_tables(positions3, inv_freq_row, sign_row):
    b, s, _ = positions3.shape

    def body(pos_ref, f_ref, sg_ref, c_ref, s_ref):
        ang = pos_ref[...].astype(F32) * f_ref[...]
        c_ref[...] = jnp.cos(ang)
        s_ref[...] = jnp.sin(ang) * sg_ref[...]

    row = pl.BlockSpec((1, LANES), lambda i: (0, 0))
    blk = pl.BlockSpec((None, s, LANES), lambda i: (i, 0, 0))
    return pl.pallas_call(
        body, grid=(b,), in_specs=[pl.BlockSpec((None, s, 1), lambda i: (i, 0, 0)), row, row], out_specs=[blk, blk],
        out_shape=[jax.ShapeDtypeStruct((b, s, LANES), F32)] * 2, name="rope_tables",
        compiler_params=_cparams(("arbitrary",)))(positions3, inv_freq_row, sign_row)


def _swap_halves(t):
    lane = lax.broadcasted_iota(jnp.int32, t.shape, 1)
    half = HEAD_DIM // 2
    return jnp.where((lane % HEAD_DIM) < half, pltpu.roll(t, LANES - half, 1), pltpu.roll(t, half, 1))


def _segment_rows(r, d, s):
    n = s // d
    return (pl.ds(r, n, stride=d) if d > 1 else pl.ds(0, s)), pl.ds(r * n, n)


def _head_lanes(shape):
    lane = lax.broadcasted_iota(jnp.int32, shape, len(shape) - 1)
    return lane < HEAD_DIM


def _bmm_nt(a, b):
    return lax.dot_general(a.astype(MXU_DTYPE), b.astype(MXU_DTYPE), (((2,), (2,)), ((0,), (0,))), preferred_element_type=F32)


def _bmm(a, b):
    return lax.dot_general(a.astype(MXU_DTYPE), b.astype(MXU_DTYPE), (((2,), (1,)), ((0,), (0,))), preferred_element_type=F32)


def _bmm_tn(a, b):
    return lax.dot_general(a.astype(MXU_DTYPE), b.astype(MXU_DTYPE), (((1,), (1,)), ((0,), (0,))), preferred_element_type=F32)


def _stack_heads(t3):
    head_a = _head_lanes(t3.shape)
    zero = jnp.zeros_like(t3)
    return jnp.concatenate([jnp.where(head_a, t3, zero), jnp.where(head_a, zero, t3)], axis=1)


QUNIT = QBLK // 2


def _with_previous(t3, nprev):
    shifted = [jnp.concatenate([t3[:k], t3[:-k]], axis=0) for k in range(nprev, 0, -1)]
    return jnp.concatenate(shifted + [t3], axis=1)


def _head_columns(t3):
    return jnp.concatenate([t3[:, :, lo:lo + 1] for lo in range(0, LANES, HEAD_DIM)], axis=1)


def _to_own_unit(t, nprev):
    unit = t.shape[1] // (nprev + 1)
    out = t[:, nprev * unit:]
    for k in range(1, nprev + 1):
        part = t[:, (nprev - k) * unit:(nprev - k + 1) * unit]
        out = out + jnp.concatenate([part[k:], jnp.zeros_like(part[:k])], axis=0)
    return out


def _branch_operands(qh, kh, vh, s, nb):
    nh = LANES // HEAD_DIM
    unit, nprev = (QBLK, 0) if nb == 1 else (QUNIT, QBLK // QUNIT)
    g = s // unit
    q3 = _stack_heads(qh[...].reshape(g, unit, LANES))
    k3, v3 = kh[...].reshape(g, unit, LANES), vh[...].reshape(g, unit, LANES)
    if nprev == 0:
        qi = lax.broadcasted_iota(jnp.int32, (1, nh * unit, unit), 1) % unit
        kj = lax.broadcasted_iota(jnp.int32, (1, nh * unit, unit), 2)
        return q3, k3, v3, kj <= qi, nprev
    shape = (g, nh * unit, (nprev + 1) * unit)
    qi = lax.broadcasted_iota(jnp.int32, shape, 1) % unit
    kj = lax.broadcasted_iota(jnp.int32, shape, 2)
    j = lax.broadcasted_iota(jnp.int32, shape, 0)
    per_block = QBLK // unit
    opens = ((j // per_block) % nb) == 0
    mask = (kj >= qi) & (kj <= qi + QBLK) & ((kj >= QBLK - unit * (j % per_block)) | jnp.logical_not(opens))
    return q3, _with_previous(k3, nprev), _with_previous(v3, nprev), mask, nprev


def _attn_fwd(proj3, ctab, stab, aw):
    b, s, _ = proj3.shape
    npair = aw // LANES
    scale = HEAD_DIM ** -0.5
    nbr = len(DILATION_PAIRS)

    def body(q_ref, k_ref, v_ref, c_ref, s_ref, o_ref, l_ref, qf, kf, qh, kh, vh, op, lp, ob, lb):
        cc, ss = c_ref[...], s_ref[...]
        q2, k2 = q_ref[...], k_ref[...]
        qf[...] = (q2 * cc + _swap_halves(q2) * ss) * scale
        kf[...] = k2 * cc + _swap_halves(k2) * ss
        for br, (window, d) in enumerate(DILATION_PAIRS):
            for r in range(d):
                nat, perm = _segment_rows(r, d, s)
                for dst, src in ((qh, qf), (kh, kf), (vh, v_ref)):
                    dst[perm, :] = src[nat, :].astype(MXU_DTYPE)
            o_dst, l_dst = (ob.at[br], lb.at[br]) if d == 1 else (op, lp)
            q3, kk, vv, mask, _ = _branch_operands(qh, kh, vh, s, (s // d) // QBLK)
            unit = q3.shape[1] // (LANES // HEAD_DIM)
            head_a = _head_lanes((q3.shape[0], unit, LANES))
            sc = jnp.where(mask, _bmm_nt(q3, kk), NEG_INF)
            mx = jnp.max(sc, axis=-1, keepdims=True)
            p = jnp.exp(sc - mx)
            den = jnp.sum(p, axis=-1, keepdims=True)
            o2 = _bmm(p, vv) / den
            l2 = mx + jnp.log(den)
            o_dst[...] = jnp.where(head_a, o2[:, :unit], o2[:, unit:]).reshape(s, LANES)
            l_dst[...] = jnp.where(head_a, l2[:, :unit], l2[:, unit:]).reshape(s, LANES)
            if d > 1:
                for r in range(d):
                    nat, perm = _segment_rows(r, d, s)
                    ob[br, nat, :] = op[perm, :]
                    lb[br, nat, :] = lp[perm, :]
        ls = [lb[br] for br in range(nbr)]
        mx = functools.reduce(jnp.maximum, ls)
        ws = [jnp.exp(l - mx) for l in ls]
        tot = functools.reduce(lambda a, b_: a + b_, ws)
        o_ref[...] = functools.reduce(lambda a, b_: a + b_, [(w / tot) * ob[br] for br, w in enumerate(ws)])
        l_ref[...] = mx + jnp.log(tot)

    def col(off):
        return pl.BlockSpec((None, s, LANES), lambda bi, hp, off=off: (bi, 0, off + hp))

    tab = pl.BlockSpec((None, s, LANES), lambda bi, hp: (bi, 0, 0))
    f32s = pltpu.VMEM((s, LANES), F32)
    mxs = pltpu.VMEM((s, LANES), MXU_DTYPE)
    br_s = pltpu.VMEM((nbr, s, LANES), F32)
    return pl.pallas_call(
        body, grid=(b, npair), in_specs=[col(0), col(npair), col(2 * npair), tab, tab], out_specs=[col(0), col(0)],
        out_shape=[jax.ShapeDtypeStruct((b, s, aw), F32)] * 2,
        scratch_shapes=[f32s] * 2 + [mxs] * 3 + [f32s] * 2 + [br_s] * 2,
        name="attn_fwd", compiler_params=_cparams(("arbitrary", "arbitrary")))(proj3, proj3, proj3, ctab, stab)


def _attn_bwd(proj3, ctab, stab, dout3, out3, lse3, aw):
    b, s, _ = proj3.shape
    npair = aw // LANES
    scale = HEAD_DIM ** -0.5
    nheads = LANES // HEAD_DIM

    def body(q_ref, k_ref, v_ref, c_ref, s_ref, do_ref, o_ref, l_ref, dq_ref, dk_ref, dv_ref,
             qf, kf, dlf, qh, kh, vh, doh, lpm, dpm, dqp, dkp, dvp, dqn, dkn, dvn):
        cc, ss = c_ref[...], s_ref[...]
        q2, k2 = q_ref[...], k_ref[...]
        qf[...] = (q2 * cc + _swap_halves(q2) * ss) * scale
        kf[...] = k2 * cc + _swap_halves(k2) * ss
        dd = do_ref[...] * o_ref[...]
        in_a = _head_lanes((s, LANES))
        sum_a = jnp.sum(jnp.where(in_a, dd, 0.0), axis=-1, keepdims=True)
        sum_b = jnp.sum(jnp.where(in_a, 0.0, dd), axis=-1, keepdims=True)
        dlf[...] = jnp.where(in_a, sum_a, sum_b)
        for r_ in (dqn, dkn, dvn):
            r_[...] = jnp.zeros_like(r_)
        for window, d in DILATION_PAIRS:
            for r in range(d):
                nat, perm = _segment_rows(r, d, s)
                for dst, src in ((qh, qf), (kh, kf), (vh, v_ref), (doh, do_ref)):
                    dst[perm, :] = src[nat, :].astype(MXU_DTYPE)
                if d > 1:
                    lpm[perm, :] = l_ref[nat, :]
                    dpm[perm, :] = dlf[nat, :]
            l_src, d_src = (l_ref, dlf) if d == 1 else (lpm, dpm)
            q3, kk, vv, mask, nprev = _branch_operands(qh, kh, vh, s, (s // d) // QBLK)
            g_, unit = q3.shape[0], q3.shape[1] // nheads
            head_a = _head_lanes((g_, unit, LANES))
            do3 = _stack_heads(doh[...].reshape(g_, unit, LANES))
            lcol, dcol = _head_columns(l_src[...].reshape(g_, unit, LANES)), _head_columns(d_src[...].reshape(g_, unit, LANES))
            p = jnp.exp(jnp.where(mask, _bmm_nt(q3, kk), NEG_INF) - lcol)
            ds_ = p * (_bmm_nt(do3, vv) - dcol)
            dq2 = _bmm(ds_, kk)
            dq_new = jnp.where(head_a, dq2[:, :unit], dq2[:, unit:]).reshape(s, LANES)
            dk_new = _to_own_unit(_bmm_tn(ds_, q3), nprev).reshape(s, LANES)
            dv_new = _to_own_unit(_bmm_tn(p, do3), nprev).reshape(s, LANES)
            if d == 1:
                dqn[...] += dq_new
                dkn[...] += dk_new
                dvn[...] += dv_new
            else:
                dqp[...] = dq_new
                dkp[...] = dk_new
                dvp[...] = dv_new
                for r in range(d):
                    nat, perm = _segment_rows(r, d, s)
                    dqn[nat, :] += dqp[perm, :]
                    dkn[nat, :] += dkp[perm, :]
                    dvn[nat, :] += dvp[perm, :]
        g = dqn[...] * scale
        dq_ref[...] = (g * cc + _swap_halves(g * ss)).astype(dq_ref.dtype)
        g = dkn[...]
        dk_ref[...] = (g * cc + _swap_halves(g * ss)).astype(dk_ref.dtype)
        dv_ref[...] = dvn[...].astype(dv_ref.dtype)

    def col(off):
        return pl.BlockSpec((None, s, LANES), lambda bi, hp, off=off: (bi, 0, off + hp))

    tab = pl.BlockSpec((None, s, LANES), lambda bi, hp: (bi, 0, 0))
    f32s = pltpu.VMEM((s, LANES), F32)
    mxs = pltpu.VMEM((s, LANES), MXU_DTYPE)
    return pl.pallas_call(
        body, grid=(b, npair), in_specs=[col(0), col(npair), col(2 * npair), tab, tab, col(0), col(0), col(0)],
        out_specs=[col(0)] * 3, out_shape=[jax.ShapeDtypeStruct((b, s, aw), MXU_DTYPE)] * 3,
        scratch_shapes=[f32s] * 3 + [mxs] * 4 + [f32s] * 8,
        name="attn_bwd", compiler_params=_cparams(("arbitrary", "arbitrary")))(
            proj3, proj3, proj3, ctab, stab, dout3, out3, lse3)


def _s5_discretise(a_re, a_im, log_dt, bt_re, bt_im):
    dt = jnp.exp(log_dt)
    mag = jnp.exp(a_re * dt)
    ang = a_im * dt
    lb_re = mag * jnp.cos(ang)
    lb_im = mag * jnp.sin(ang)
    den = a_re * a_re + a_im * a_im
    nr = lb_re - 1.0
    ni = lb_im
    cr = (nr * a_re + ni * a_im) / den
    ci = (ni * a_re - nr * a_im) / den
    return lb_re, lb_im, cr * bt_re - ci * bt_im, cr * bt_im + ci * bt_re


def _s5_params(a_re, a_im, log_dt, bt_re, bt_im):
    def body(ar, ai, ld, br, bi, o1, o2, o3, o4):
        r = _s5_discretise(ar[...], ai[...], ld[...], br[...], bi[...])
        for o, v in zip((o1, o2, o3, o4), r, strict=True):
            o[...] = v

    sd = jax.ShapeDtypeStruct
    return pl.pallas_call(body, out_shape=[sd(a_re.shape, F32)] * 2 + [sd(bt_re.shape, F32)] * 2, name="s5_params")(
        a_re, a_im, log_dt, bt_re, bt_im)


def _s5_params_bwd(a_re, a_im, log_dt, bt_re, bt_im, cts):
    def body(ar, ai, ld, br, bi, c1, c2, c3, c4, o1, o2, o3, o4, o5):
        _, vjp = jax.vjp(_s5_discretise, ar[...], ai[...], ld[...], br[...], bi[...])
        r = vjp((c1[...], c2[...], c3[...], c4[...]))
        for o, v in zip((o1, o2, o3, o4, o5), r, strict=True):
            o[...] = v

    sd = jax.ShapeDtypeStruct
    return pl.pallas_call(
        body, out_shape=[sd(a_re.shape, F32)] * 2 + [sd(log_dt.shape, F32)] + [sd(bt_re.shape, F32)] * 2, name="s5_params_bwd")(
            a_re, a_im, log_dt, bt_re, bt_im, *cts)


S5_TC = 128


def _time_major(tiles):
    nt, tc, lanes = tiles.shape
    return jnp.swapaxes(tiles, 0, 1).reshape(tc * nt, lanes)


def _tile_major(rows, nt):
    return jnp.swapaxes(rows.astype(MXU_DTYPE).reshape(rows.shape[0] // nt, nt, rows.shape[1]), 0, 1)


def _s5_fwd(proj3, ucol, lam_re, lam_im, bre, bim, cre, cim, dvec, sw):
    b, s, _ = proj3.shape
    nt = lam_re.shape[0]
    ncb = sw // LANES
    tpc = nt // ncb
    tc = S5_TC

    def body(u_ref, lr_ref, li_ref, bre_ref, bim_ref, cre_ref, cim_ref, d_ref, y_ref, xr_ref, xi_ref, sr, si):
        @pl.when(pl.program_id(0) == 0)
        def _():
            sr[...] = jnp.zeros_like(sr)
            si[...] = jnp.zeros_like(si)

        u_all = u_ref[...].reshape(b * tc, sw)
        bur, bui = [], []
        for cb in range(ncb):
            ucb = u_all[:, cb * LANES:(cb + 1) * LANES].astype(MXU_DTYPE)
            for t in range(cb * tpc, (cb + 1) * tpc):
                bur.append(_mm(ucb, bre_ref[t]))
                bui.append(_mm(ucb, bim_ref[t]))
        for bi in range(b):
            rows = slice(bi * tc, (bi + 1) * tc)
            xr_ref[bi] = _time_major(jnp.stack([v_[rows] for v_ in bur]))
            xi_ref[bi] = _time_major(jnp.stack([v_[rows] for v_ in bui]))
        lr, li = lr_ref[...], li_ref[...]

        def step(j, carry):
            off = pl.multiple_of(j * nt, nt)
            new = []
            for bi in range(b):
                pr, pi = carry[2 * bi], carry[2 * bi + 1]
                nr = lr * pr - li * pi + xr_ref[bi, pl.ds(off, nt), :]
                ni = lr * pi + li * pr + xi_ref[bi, pl.ds(off, nt), :]
                xr_ref[bi, pl.ds(off, nt), :] = nr
                xi_ref[bi, pl.ds(off, nt), :] = ni
                new += [nr, ni]
            return tuple(new)

        init = tuple(v for bi in range(b) for v in (sr[bi], si[bi]))
        fin = lax.fori_loop(0, tc, step, init, unroll=4)
        for bi in range(b):
            sr[bi] = fin[2 * bi]
            si[bi] = fin[2 * bi + 1]
        xr_t = [_tile_major(xr_ref[bi], nt) for bi in range(b)]
        xi_t = [_tile_major(xi_ref[bi], nt) for bi in range(b)]
        for cb in range(ncb):
            cols = slice(cb * LANES, (cb + 1) * LANES)
            acc = d_ref[:, cols] * u_all[:, cols]
            for t in range(cb * tpc, (cb + 1) * tpc):
                xr_all = jnp.concatenate([xr_t[bi][t] for bi in range(b)], axis=0)
                xi_all = jnp.concatenate([xi_t[bi][t] for bi in range(b)], axis=0)
                acc = acc + (_mm(xr_all, cre_ref[t]) - _mm(xi_all, cim_ref[t]))
            for bi in range(b):
                y_ref[bi, :, cols] = acc[bi * tc:(bi + 1) * tc]

    def whole(a):
        return pl.BlockSpec(a.shape, lambda c, n=a.ndim: (0,) * n)

    xblk = pl.BlockSpec((b, tc * nt, LANES), lambda c: (0, c, 0))
    return pl.pallas_call(
        body, grid=(s // tc,),
        in_specs=[pl.BlockSpec((b, tc, sw), lambda c: (0, c, ucol))] + [whole(a) for a in (lam_re, lam_im, bre, bim, cre, cim, dvec)],
        out_specs=[pl.BlockSpec((b, tc, sw), lambda c: (0, c, 0)), xblk, xblk],
        out_shape=[jax.ShapeDtypeStruct((b, s, sw), F32)] + [jax.ShapeDtypeStruct((b, s * nt, LANES), F32)] * 2,
        scratch_shapes=[pltpu.VMEM((b, nt, LANES), F32)] * 2, name="s5_fwd",
        compiler_params=_cparams(("arbitrary",)))(proj3, lam_re, lam_im, bre, bim, cre, cim, dvec)


def _s5_bwd(proj3, ucol, dy3, xr3, xi3, lam_re, lam_im, bre, bim, cre, cim, dvec, sw):
    b, s, _ = proj3.shape
    nt = lam_re.shape[0]
    ncb = sw // LANES
    tpc = nt // ncb
    tc = S5_TC
    nchunk = s // tc

    def body(u_ref, dy_ref, xr_ref, xi_ref, pr_ref, pi_ref, lr_ref, li_ref, bre_ref, bim_ref, cre_ref, cim_ref, d_ref,
             du_ref, dbre, dbim, dcre, dcim, dlr, dli, dd, gr, gi, sr, si):
        step_id = pl.program_id(0)

        @pl.when(step_id == 0)
        def _():
            for r in (sr, si, dbre, dbim, dcre, dcim, dlr, dli, dd):
                r[...] = jnp.zeros_like(r)

        u_all = u_ref[...].reshape(b * tc, sw)
        dy_all = dy_ref[...].reshape(b * tc, sw)
        dxr, dxi = [], []
        for cb in range(ncb):
            dycb = dy_all[:, cb * LANES:(cb + 1) * LANES].astype(MXU_DTYPE)
            for t in range(cb * tpc, (cb + 1) * tpc):
                dxr.append(_mm_nt(dycb, cre_ref[t]))
                dxi.append(-_mm_nt(dycb, cim_ref[t]))
        for bi in range(b):
            rows = slice(bi * tc, (bi + 1) * tc)
            gr[bi] = _time_major(jnp.stack([v_[rows] for v_ in dxr]))
            gi[bi] = _time_major(jnp.stack([v_[rows] for v_ in dxi]))
        lr, li = lr_ref[...], li_ref[...]

        has_prev = (step_id != nchunk - 1).astype(F32)
        before = [(pr_ref[bi] * has_prev, pi_ref[bi] * has_prev) for bi in range(b)]

        def step(jj, carry):
            t = tc - 1 - jj
            off = pl.multiple_of(t * nt, nt)
            poff = pl.multiple_of(jnp.maximum(t - 1, 0) * nt, nt)
            new = []
            alr, ali = carry[-2], carry[-1]
            for bi in range(b):
                nr_, ni_ = carry[2 * bi], carry[2 * bi + 1]
                vr = gr[bi, pl.ds(off, nt), :] + lr * nr_ + li * ni_
                vi = gi[bi, pl.ds(off, nt), :] + lr * ni_ - li * nr_
                gr[bi, pl.ds(off, nt), :] = vr
                gi[bi, pl.ds(off, nt), :] = vi
                xpr = jnp.where(t > 0, xr_ref[bi, pl.ds(poff, nt), :], before[bi][0])
                xpi = jnp.where(t > 0, xi_ref[bi, pl.ds(poff, nt), :], before[bi][1])
                alr = alr + (vr * xpr + vi * xpi)
                ali = ali + (vi * xpr - vr * xpi)
                new += [vr, vi]
            return tuple(new) + (alr, ali)

        zero = jnp.zeros((nt, LANES), F32)
        init = tuple(v for bi in range(b) for v in (sr[bi], si[bi])) + (zero, zero)
        fin = lax.fori_loop(0, tc, step, init, unroll=4)
        for bi in range(b):
            sr[bi] = fin[2 * bi]
            si[bi] = fin[2 * bi + 1]
        dlr[...] += fin[-2]
        dli[...] += fin[-1]

        tiles = [[_tile_major(ref[bi], nt) for bi in range(b)] for ref in (gr, gi, xr_ref, xi_ref)]

        def stacked(k, t):
            return jnp.concatenate([tiles[k][bi][t] for bi in range(b)], axis=0)

        for cb in range(ncb):
            cols = slice(cb * LANES, (cb + 1) * LANES)
            ucb32, dycb32 = u_all[:, cols], dy_all[:, cols]
            ucb, dycb = ucb32.astype(MXU_DTYPE), dycb32.astype(MXU_DTYPE)
            acc = d_ref[:, cols] * dycb32
            for t in range(cb * tpc, (cb + 1) * tpc):
                vr, vi = stacked(0, t), stacked(1, t)
                acc = acc + (_mm_nt(vr, bre_ref[t]) + _mm_nt(vi, bim_ref[t]))
                dbre[t] += _mm_tn(ucb, vr)
                dbim[t] += _mm_tn(ucb, vi)
                dcre[t] += _mm_tn(stacked(2, t), dycb)
                dcim[t] -= _mm_tn(stacked(3, t), dycb)
            for bi in range(b):
                du_ref[bi, :, cols] = acc[bi * tc:(bi + 1) * tc].astype(du_ref.dtype)
            dd[:, cols] += _colsum(dycb32 * ucb32)

    def whole(a):
        return pl.BlockSpec(a.shape, lambda c, n=len(a.shape): (0,) * n)

    def rev(c):
        return nchunk - 1 - c

    xblk = pl.BlockSpec((b, tc * nt, LANES), lambda c: (0, rev(c), 0))
    prev = pl.BlockSpec((b, nt, LANES), lambda c: (0, jnp.maximum(rev(c) * tc - 1, 0), 0))
    sd = jax.ShapeDtypeStruct
    blk = sd(bre.shape, F32)
    acc_shapes = [blk, blk, sd(cre.shape, F32), sd(cre.shape, F32), sd(lam_re.shape, F32), sd(lam_re.shape, F32), sd((1, sw), F32)]
    return pl.pallas_call(
        body, grid=(nchunk,),
        in_specs=[pl.BlockSpec((b, tc, sw), lambda c: (0, rev(c), ucol)), pl.BlockSpec((b, tc, sw), lambda c: (0, rev(c), 0)),
                  xblk, xblk, prev, prev] + [whole(a) for a in (lam_re, lam_im, bre, bim, cre, cim, dvec)],
        out_specs=[pl.BlockSpec((b, tc, sw), lambda c: (0, rev(c), 0))] + [whole(a) for a in acc_shapes],
        out_shape=[sd((b, s, sw), MXU_DTYPE)] + acc_shapes,
        scratch_shapes=[pltpu.VMEM((b, tc * nt, LANES), F32)] * 2 + [pltpu.VMEM((b, nt, LANES), F32)] * 2, name="s5_bwd",
        compiler_params=_cparams(("arbitrary",)))(proj3, dy3, xr3, xi3, xr3, xi3, lam_re, lam_im, bre, bim, cre, cim, dvec)


def _s5_blocks(bb, cc_, ncb):
    g, n, p = bb.shape
    mask = jnp.asarray(_s5_tile_mask(g, p, ncb))
    nt, gpt, gpc = mask.shape
    bbt, cct = bb.reshape(nt, gpt, n, p), cc_.reshape(nt, gpt, n, p)
    spread = mask[:, :, :, None, None]
    bblk = (bbt[:, :, None] * spread).transpose(0, 2, 3, 1, 4).reshape(nt, gpc * n, gpt * p)
    cblk = (cct[:, :, None] * spread).transpose(0, 1, 4, 2, 3).reshape(nt, gpt * p, gpc * n)
    return bblk, cblk


def _s5_tile_mask(g, p, ncb):
    nt, gpt, gpc = g * p // LANES, LANES // p, g // ncb
    mask = np.zeros((nt, gpt, gpc), np.float32)
    for t in range(nt):
        for gl in range(gpt):
            mask[t, gl, (t * gpt + gl) % gpc] = 1.0
    return mask


def _s5_unblock(dbblk, dcblk, g, n, p, ncb):
    mask = jnp.asarray(_s5_tile_mask(g, p, ncb))
    nt, gpt, gpc = mask.shape
    db = jnp.sum(dbblk.reshape(nt, gpc, n, gpt, p) * mask.transpose(0, 2, 1)[:, :, None, :, None], axis=1)
    dc = jnp.sum(dcblk.reshape(nt, gpt, p, gpc, n) * mask[:, :, None, :, None], axis=3)
    return db.transpose(0, 2, 1, 3).reshape(g, n, p), dc.transpose(0, 1, 3, 2).reshape(g, n, p)


def _adamw_math(w, g, m, v):
    m = ADAM_B1 * m + (1.0 - ADAM_B1) * g
    v = ADAM_B2 * v + (1.0 - ADAM_B2) * (g * g)
    m_hat = m / (1.0 - ADAM_B1 ** ADAM_STEP)
    v_hat = v / (1.0 - ADAM_B2 ** ADAM_STEP)
    delta = -ADAM_LR * (m_hat / (jnp.sqrt(v_hat) + ADAM_EPS) + ADAM_WD * w)
    return delta, m, v


def _adamw_layer(name, parts, w, m, v, layer, prev, tr):
    nparts, r, c = parts.shape
    assert r % tr == 0
    if prev is None:
        prev = [lax.empty(w.shape, F32) for _ in range(4)]

    def body(p_ref, w_ref, m_ref, v_ref, *rest):
        g_out, d_out, m_out, v_out = rest[4:]
        g = p_ref[0].astype(F32)
        for k in range(1, nparts):
            g = g + p_ref[k].astype(F32)
        delta, mn, vn = _adamw_math(w_ref[...], g, m_ref[...], v_ref[...])
        g_out[...] = g
        d_out[...] = delta
        m_out[...] = mn
        v_out[...] = vn

    blk = pl.BlockSpec((None, tr, c), lambda i: (layer, i, 0))
    kept = pl.BlockSpec(memory_space=pl.ANY)
    return pl.pallas_call(
        body, grid=(r // tr,), in_specs=[pl.BlockSpec((nparts, tr, c), lambda i: (0, i, 0)), blk, blk, blk] + [kept] * 4,
        out_specs=[blk] * 4, out_shape=[jax.ShapeDtypeStruct(w.shape, F32)] * 4, name=name,
        input_output_aliases={4 + k: k for k in range(4)},
        compiler_params=_cparams(("arbitrary",)))(parts, w, m, v, *prev)


def _piece_rows(size):
    unit = SUBLANES * LANES
    return -(-size // unit) * SUBLANES


def _pack_small(pieces, lead=0):
    rows = []
    for p in pieces:
        head = p.shape[:lead]
        flat = p.reshape(head + (-1,))
        nrow = _piece_rows(flat.shape[-1])
        flat = jnp.pad(flat, [(0, 0)] * lead + [(0, nrow * LANES - flat.shape[-1])])
        rows.append(flat.reshape(head + (nrow, LANES)))
    return jnp.concatenate(rows, axis=lead)


def _step(x, positions, weights, moments_m, moments_v, loss_target, distributed):
    f32 = F32
    bsz, seq, dm = x.shape
    depth = weights["w_in"].shape[0]
    aw = weights["attn_gain"].shape[1]
    sw = weights["ssm_gain"].shape[1]
    dff = weights["b_ff1"].shape[1]
    ng, npst = weights["ssm_a_re"].shape[1:]
    gdim = weights["ssm_d"].shape[2]
    in_w = 3 * aw + sw
    t_rows = bsz * seq
    alpha = (2.0 * depth) ** 0.25
    ncb = sw // LANES
    nt = ng * npst // LANES
    ndev = NDEV if distributed else 1
    tm, tl = 512, 1024

    gather_modes = ["gather"] * len(BIG_NAMES)
    no_token = jnp.zeros((SUBLANES, LANES), f32)

    def shards(l, names):
        return [weights[n][l].astype(MXU_DTYPE) for n in names]

    if distributed:
        first, _ = _exchange_start("weights_start_l0_in", shards(0, BIG_NAMES[:1]), gather_modes[:1])

    half = HEAD_DIM // 2
    inv_freq = ROPE_THETA ** (-jnp.arange(half, dtype=f32) * 2.0 / HEAD_DIM)
    reps = LANES // half
    inv_row = jnp.tile(inv_freq, reps)[None, :]
    sign_row = jnp.tile(jnp.concatenate([-jnp.ones((half,), f32), jnp.ones((half,), f32)]), LANES // HEAD_DIM)[None, :]
    ctab, stab = _rope_tables(positions[..., None], inv_row, sign_row)

    def row(v):
        return v.reshape(1, -1)

    s5_prep = []
    for l in range(depth):
        a_re, a_im = weights["ssm_a_re"][l][:, None, :], weights["ssm_a_im"][l][:, None, :]
        log_dt = weights["ssm_log_dt"][l][:, None, None]
        bt_re = weights["ssm_b_re"][l].transpose(0, 2, 1)
        bt_im = weights["ssm_b_im"][l].transpose(0, 2, 1)
        lb_re, lb_im, bb_re, bb_im = _s5_params(a_re, a_im, log_dt, bt_re, bt_im)
        bre, cre = _s5_blocks(bb_re, weights["ssm_c_re"][l], ncb)
        bim, cim = _s5_blocks(bb_im, weights["ssm_c_im"][l], ncb)
        s5_prep.append(dict(lam=(lb_re.reshape(nt, LANES), lb_im.reshape(nt, LANES)),
                            s5c=[a_.astype(MXU_DTYPE) for a_ in (bre, bim, cre, cim)], dvec=row(weights["ssm_d"][l]),
                            s5in=(a_re, a_im, log_dt, bt_re, bt_im)))

    if distributed:
        (g_in0,) = _exchange_wait("weights_wait_l0_in", first, gather_modes[:1], s5_prep[-1]["s5c"][-1])
        rest0, tok_rest0 = _exchange_start("weights_start_l0_rest", shards(0, BIG_NAMES[1:]), gather_modes[1:], after=g_in0)

    h = x.reshape(t_rows, dm)
    saved = []
    for l in range(depth):
        tok_in = tok_mix = no_token
        if not distributed:
            g_in, g_glu, g_out, g_ff1, g_ff2 = [weights[n][l].astype(MXU_DTYPE)[None] for n in BIG_NAMES]
        elif l == 0:
            g_in, tok_in = g_in0, tok_rest0
        else:
            g_in, g_glu, g_out, g_ff1, g_ff2 = _exchange_wait(f"weights_wait_l{l}", next_gather, gather_modes, h)
            if l + 1 < depth:
                next_gather, tok_in = _exchange_start(f"weights_start_l{l + 1}", shards(l + 1, BIG_NAMES), gather_modes, after=g_in)
        w_in_l = g_in

        def in_proj(rv, cr):
            return [jnp.concatenate([_mm(rv[0], cr[0][j]) for j in range(ndev)], axis=-1)], []

        (proj,), _ = _rows_call("in_proj", in_proj, [h], [w_in_l, tok_in], [(in_w, f32)], [], tl)
        proj3 = proj.reshape(bsz, seq, in_w)
        attn3, lse3 = _attn_fwd(proj3, ctab, stab, aw)
        if distributed and l == 0:
            g_glu, g_out, g_ff1, g_ff2 = _exchange_wait("weights_wait_l0_rest", rest0, gather_modes[1:], attn3)
            if depth > 1:
                next_gather, tok_mix = _exchange_start("weights_start_l1", shards(1, BIG_NAMES), gather_modes, after=g_glu)
        w_glu_l = g_glu.reshape(sw, sw)
        w_out_l = g_out.reshape(dm, dm)
        w_ff1_l = g_ff1
        w_ff2_l = g_ff2.reshape(dff, dm)

        (lam_re, lam_im), s5c, dvec = s5_prep[l]["lam"], s5_prep[l]["s5c"], s5_prep[l]["dvec"]
        ypre3, xr3, xi3 = _s5_fwd(proj3, 3 * aw // sw, lam_re, lam_im, *s5c, dvec, sw)
        attn, ypre = attn3.reshape(t_rows, aw), ypre3.reshape(t_rows, sw)

        b_glu, ga, gs = row(weights["b_glu"][l]), row(weights["attn_gain"][l]), row(weights["ssm_gain"][l])

        def mix(rv, cr):
            at, yp = rv
            g = _gelu(yp)
            ssm = g * jax.nn.sigmoid(_mm(g, cr[0][...]) + cr[1][...])
            return [jnp.concatenate([_rms_norm(at, cr[2][...]), _rms_norm(ssm, cr[3][...])], axis=-1)], []

        (mixed,), _ = _rows_call("mix", mix, [attn, ypre], [w_glu_l, b_glu, ga, gs, tok_mix], [(dm, MXU_DTYPE)], [], tl)

        b_out, g1, b1 = row(weights["b_out"][l]), row(weights["ln1_g"][l]), row(weights["ln1_b"][l])

        def out_proj(rv, cr):
            pre = alpha * rv[1] + _mm(rv[0], cr[0][...]) + cr[1][...]
            return [pre, _layer_norm(pre, cr[2][...], cr[3][...])], []

        (pre1, h1), _ = _rows_call("out_proj", out_proj, [mixed, h], [w_out_l, b_out, g1, b1], [(dm, f32), (dm, f32)], [], tl)

        b_ff1 = row(weights["b_ff1"][l])

        def ff1(rv, cr):
            pre_act = jnp.concatenate([_mm(rv[0], cr[0][j]) for j in range(ndev)], axis=-1) + cr[1][...]
            return [jnp.square(jnp.maximum(pre_act, 0.0))], []

        (act,), _ = _rows_call("ff1", ff1, [h1], [w_ff1_l, b_ff1], [(dff, MXU_DTYPE)], [], tm)

        b_ff2, g2, b2 = row(weights["b_ff2"][l]), row(weights["ln2_g"][l]), row(weights["ln2_b"][l])

        def ff2(rv, cr):
            pre = alpha * rv[1] + _mm(rv[0], cr[0][...]) + cr[1][...]
            return [pre, _layer_norm(pre, cr[2][...], cr[3][...])], []

        (pre2, h2), _ = _rows_call("ff2", ff2, [act, h1], [w_ff2_l, b_ff2, g2, b2], [(dm, f32), (dm, f32)], [], tm)

        saved.append(dict(h=h, proj3=proj3, attn=attn, lse3=lse3, attn3=attn3, ypre=ypre, xr3=xr3, xi3=xi3, mixed=mixed, pre1=pre1,
                          h1=h1, act=act, pre2=pre2, w_in=w_in_l, w_glu=w_glu_l, w_out=w_out_l, w_ff1=w_ff1_l, w_ff2=w_ff2_l,
                          lam=(lam_re, lam_im), s5c=s5c, dvec=dvec, b_glu=b_glu, ga=ga, gs=gs, g1=g1, g2=g2,
                          s5in=s5_prep[l]["s5in"]))
        h = h2

    g2_last = saved[-1]["g2"]

    def loss_fn(rv, cr):
        y, tgt, pre = rv
        err = y - tgt
        part = 0.5 * jnp.sum(jnp.mean(err * err, axis=-1, keepdims=True), axis=0, keepdims=True)
        dpre, dg, db = _layer_norm_bwd(err * (1.0 / dm), pre, cr[0][...])
        return [dpre], [jnp.broadcast_to(part, (1, LANES)), dg, db, _colsum(dpre)]

    (dpre2,), (loss_acc, dg2, db2, dbff2) = _rows_call(
        "loss", loss_fn, [h, loss_target.reshape(t_rows, dm), saved[-1]["pre2"]], [g2_last], [(dm, f32)],
        [(1, LANES), (1, dm), (1, dm), (1, dm)], tl)
    loss = loss_acc[0, 0]
    if distributed:
        loss = lax.psum(loss, MESH_AXES)

    big_parts = {n: [None] * depth for n in BIG_NAMES}
    small_parts = [None] * depth
    grad_handles, ff_handles = [None] * depth, [None] * depth
    grad_modes = ["scatter"] * len(MIXER_NAMES) + ["gather"]
    grad_x = None
    for l in reversed(range(depth)):
        sv = saved[l]

        def ff2_bwd(rv, cr):
            r = rv[1].astype(F32)
            relu2 = jnp.where(r > 0.0, (2.0 * r) * lax.rsqrt(r), 0.0)
            da = _mm_nt(rv[0], cr[0][...]) * relu2
            return [da], [_colsum(da)]

        (dact,), (dbff1,) = _rows_call("ff2_bwd", ff2_bwd, [dpre2, sv["act"]], [sv["w_ff2"]], [(dff, MXU_DTYPE)], [(1, dff)], tm)
        big_parts["w_ff2"][l] = _wgrad_call("wgrad_ff2", sv["act"], dpre2, "x", ndev, ndev,
                                            (dff // ndev, dm), tm)
        big_parts["w_ff1"][l] = _wgrad_call("wgrad_ff1", sv["h1"], dact, "y", ndev, ndev,
                                            (dm, dff // ndev), tm)
        ff_grads = [big_parts[n][l] for n in FF_NAMES]
        if distributed:
            ff_handles[l], ff_token = _exchange_start(f"grads_ff_start_l{l}", ff_grads, ["scatter"] * len(FF_NAMES))
        else:
            ff_handles[l], ff_token = ff_grads, no_token

        def ff1_bwd(rv, cr):
            dacc = alpha * rv[1]
            wpb = dff // ndev
            for j in range(ndev):
                dacc = dacc + _mm_nt(rv[0][:, j * wpb:(j + 1) * wpb], cr[0][j])
            dpre, dg, db = _layer_norm_bwd(dacc, rv[2], cr[1][...])
            return [dpre], [dg, db, _colsum(dpre)]

        (dpre1,), (dg1, db1, dbout) = _rows_call("ff1_bwd", ff1_bwd, [dact, dpre2, sv["pre1"]],
                                                 [sv["w_ff1"], sv["g1"], ff_token], [(dm, f32)], [(1, dm)] * 3, tm)
        big_parts["w_out"][l] = _wgrad_call("wgrad_out", sv["mixed"], dpre1, "x", ndev, ndev, (dm // ndev, dm), tl)

        def mix_bwd(rv, cr):
            dp, at, yp = rv
            w_out_r, w_glu_r, bg, ga_, gs_ = cr
            dmixed = _mm_nt(dp, w_out_r[...])
            g = _gelu(yp)
            sig = jax.nn.sigmoid(_mm(g, w_glu_r[...]) + bg[...])
            ssm = g * sig
            dat, dga = _rms_norm_bwd(dmixed[:, :aw], at, ga_[...])
            dssm, dgs = _rms_norm_bwd(dmixed[:, aw:], ssm, gs_[...])
            dz = dssm * g * sig * (1.0 - sig)
            dg = dssm * sig + _mm_nt(dz, w_glu_r[...])
            return [dat, dg * _gelu_grad(yp), dz], [dga, dgs, _colsum(dz)]

        (dattn, dypre, dz), (dga, dgs, dbglu) = _rows_call(
            "mix_bwd", mix_bwd, [dpre1, sv["attn"], sv["ypre"]], [sv["w_out"], sv["w_glu"], sv["b_glu"], sv["ga"], sv["gs"]],
            [(aw, f32), (sw, f32), (sw, MXU_DTYPE)], [(1, aw), (1, sw), (1, sw)], tl)
        big_parts["w_glu"][l] = _wgrad_call("wgrad_glu", sv["ypre"], dz, "x", 1, 1, (sw, sw), tl, prologue=_gelu).reshape(
            ndev, sw // ndev, sw)

        du3, dbre, dbim, dcre, dcim, dlr, dli, dd = _s5_bwd(
            sv["proj3"], 3 * aw // sw, dypre.reshape(bsz, seq, sw), sv["xr3"], sv["xi3"], *sv["lam"], *sv["s5c"], sv["dvec"], sw)
        dbb_re, dc_re = _s5_unblock(dbre, dcre, ng, gdim, npst, ncb)
        dbb_im, dc_im = _s5_unblock(dbim, dcim, ng, gdim, npst, ncb)
        da_re, da_im, dldt, dbt_re, dbt_im = _s5_params_bwd(
            *sv["s5in"], (dlr.reshape(ng, 1, npst), dli.reshape(ng, 1, npst), dbb_re, dbb_im))

        dq3, dk3, dv3 = _attn_bwd(sv["proj3"], ctab, stab, dattn.reshape(bsz, seq, aw), sv["attn3"], sv["lse3"], aw)
        dproj = [dq3.reshape(t_rows, aw), dk3.reshape(t_rows, aw), dv3.reshape(t_rows, aw), du3.reshape(t_rows, sw)]
        big_parts["w_in"][l] = _wgrad_call("wgrad_in", sv["h"], dproj, "y", ndev, ndev, (dm, in_w // ndev), tl)

        small_parts[l] = dict(
            attn_gain=dga, ssm_gain=dgs, ssm_a_re=da_re[:, 0], ssm_a_im=da_im[:, 0], ssm_log_dt=dldt[:, 0, 0], ssm_b_re=dbt_re.transpose(0, 2, 1),
            ssm_b_im=dbt_im.transpose(0, 2, 1), ssm_c_re=dc_re, ssm_c_im=dc_im, ssm_d=dd, b_glu=dbglu, b_out=dbout, ln1_g=dg1,
            ln1_b=db1, b_ff1=dbff1, b_ff2=dbff2, ln2_g=dg2, ln2_b=db2)

        layer_grads = [big_parts[n][l] for n in MIXER_NAMES] + [_pack_small([small_parts[l][n] for n in SMALL_NAMES])]
        if distributed:
            grad_handles[l], token = _exchange_start(f"grads_start_l{l}", layer_grads, grad_modes)
        else:
            grad_handles[l], token = layer_grads, no_token

        wpb = in_w // ndev
        if l > 0:
            prev = saved[l - 1]

            def in_bwd(rv, cr):
                dacc = alpha * rv[4]
                dp = jnp.concatenate([v_.astype(MXU_DTYPE) for v_ in rv[:4]], axis=-1)
                for j in range(ndev):
                    dacc = dacc + _mm_nt(dp[:, j * wpb:(j + 1) * wpb], cr[0][j])
                dpre, dg, db = _layer_norm_bwd(dacc, rv[5], cr[1][...])
                return [dpre], [dg, db, _colsum(dpre)]

            (dpre2,), (dg2, db2, dbff2) = _rows_call("in_bwd", in_bwd, dproj + [dpre1, prev["pre2"]],
                                                     [sv["w_in"], prev["g2"], token], [(dm, f32)], [(1, dm)] * 3, tl)
        else:
            def in_bwd0(rv, cr):
                dacc = alpha * rv[4]
                dp = jnp.concatenate([v_.astype(MXU_DTYPE) for v_ in rv[:4]], axis=-1)
                for j in range(ndev):
                    dacc = dacc + _mm_nt(dp[:, j * wpb:(j + 1) * wpb], cr[0][j])
                return [dacc], []

            (grad_x,), _ = _rows_call("in_bwd0", in_bwd0, dproj + [dpre1], [sv["w_in"], token], [(dm, f32)], [], tl)

    small_shapes = [weights[n].shape[1:] for n in SMALL_NAMES]
    outs = {n: None for n in BIG_NAMES}
    packed = [_pack_small([d[n] for n in SMALL_NAMES], lead=1) for d in (weights, moments_m, moments_v)]
    small_out = None
    after = grad_x
    for l in reversed(range(depth)):
        if distributed:
            recv_ff = _exchange_wait(f"grads_ff_wait_l{l}", ff_handles[l], ["scatter"] * len(FF_NAMES), after)
            recv = _exchange_wait(f"grads_wait_l{l}", grad_handles[l], grad_modes, recv_ff[0])
        else:
            recv_ff = ff_handles[l]
            recv = [g_[None] if i == len(MIXER_NAMES) else g_ for i, g_ in enumerate(grad_handles[l])]
        for n, parts in zip(FF_NAMES + MIXER_NAMES, list(recv_ff) + list(recv[:-1]), strict=True):
            outs[n] = _adamw_layer("adamw_" + n, parts, weights[n], moments_m[n], moments_v[n], l, outs[n],
                                   min(parts.shape[1], 512))
        small_out = _adamw_layer("adamw_small", recv[-1], *packed, l, small_out, recv[-1].shape[1])
        after = small_out[0]
    for k in range(4):
        row = 0
        for n, shp in zip(SMALL_NAMES, small_shapes, strict=True):
            sz = int(np.prod(shp))
            nrow = _piece_rows(sz)
            piece = small_out[k][:, row:row + nrow].reshape(depth, nrow * LANES)
            outs.setdefault(n, [None] * 4)
            outs[n][k] = piece[:, :sz].reshape((depth,) + tuple(shp))
            row += nrow

    result = [loss, grad_x.reshape(bsz, seq, dm)]
    for k in range(4):
        result += [outs[n][k] for n in WEIGHT_ORDER]
    return tuple(result)


def kernel(x, positions, w_in, attn_gain, ssm_gain, ssm_a_re, ssm_a_im, ssm_log_dt, ssm_b_re, ssm_b_im, ssm_c_re, ssm_c_im, ssm_d, w_glu, b_glu, w_out, b_out, ln1_g, ln1_b, w_ff1, b_ff1, w_ff2, b_ff2, ln2_g, ln2_b, loss_target, m_w_in, m_attn_gain, m_ssm_gain, m_ssm_a_re, m_ssm_a_im, m_ssm_log_dt, m_ssm_b_re, m_ssm_b_im, m_ssm_c_re, m_ssm_c_im, m_ssm_d, m_w_glu, m_b_glu, m_w_out, m_b_out, m_ln1_g, m_ln1_b, m_w_ff1, m_b_ff1, m_w_ff2, m_b_ff2, m_ln2_g, m_ln2_b, v_w_in, v_attn_gain, v_ssm_gain, v_ssm_a_re, v_ssm_a_im, v_ssm_log_dt, v_ssm_b_re, v_ssm_b_im, v_ssm_c_re, v_ssm_c_im, v_ssm_d, v_w_glu, v_b_glu, v_w_out, v_b_out, v_ln1_g, v_ln1_b, v_w_ff1, v_b_ff1, v_w_ff2, v_b_ff2, v_ln2_g, v_ln2_b):
    loc = locals()
    weights = {n: loc[n] for n in WEIGHT_ORDER}
    moments_m = {n: loc["m_" + n] for n in WEIGHT_ORDER}
    moments_v = {n: loc["v_" + n] for n in WEIGHT_ORDER}
    return _step(x, positions, weights, moments_m, moments_v, loss_target, distributed=True)
```

```python
import functools
import math

import jax
import jax.numpy as jnp
import numpy as np
from jax import lax
from jax.experimental import pallas as pl
from jax.experimental.pallas import tpu as pltpu

F32 = jnp.float32
MXU_DTYPE = jnp.bfloat16

HEAD_DIM = 64
DILATION_PAIRS = ((128, 1), (512, 4), (2048, 16))
ROPE_THETA = 10000.0
LN_EPS = 1e-5
RMS_EPS = 1e-6
NEG_INF = -1e30
ADAM_LR, ADAM_B1, ADAM_B2, ADAM_EPS, ADAM_WD, ADAM_STEP = 0.001, 0.9, 0.999, 1e-08, 0.01, 10

LANES = 128
SUBLANES = 8
QBLK = 128
VMEM_LIMIT = 56 * 2**20
MESH_AXES = ("x", "y", "c")
NDEV = 8

SMALL_NAMES = ("attn_gain", "ssm_gain", "ssm_a_re", "ssm_a_im", "ssm_log_dt", "ssm_b_re", "ssm_b_im", "ssm_c_re",
               "ssm_c_im", "ssm_d", "b_glu", "b_out", "ln1_g", "ln1_b", "b_ff1", "b_ff2", "ln2_g", "ln2_b")
BIG_NAMES = ("w_in", "w_glu", "w_out", "w_ff1", "w_ff2")
MIXER_NAMES, FF_NAMES = BIG_NAMES[:3], BIG_NAMES[3:]
WEIGHT_ORDER = ("w_in", "attn_gain", "ssm_gain", "ssm_a_re", "ssm_a_im", "ssm_log_dt", "ssm_b_re", "ssm_b_im", "ssm_c_re",
                "ssm_c_im", "ssm_d", "w_glu", "b_glu", "w_out", "b_out", "ln1_g", "ln1_b", "w_ff1", "b_ff1", "w_ff2",
                "b_ff2", "ln2_g", "ln2_b")


def _cparams(sem=None):
    return pltpu.CompilerParams(dimension_semantics=sem, vmem_limit_bytes=VMEM_LIMIT)


def _mm(a, b):
    return jnp.dot(a.astype(MXU_DTYPE), b.astype(MXU_DTYPE), preferred_element_type=F32)


def _mm_nt(a, b):
    return lax.dot_general(a.astype(MXU_DTYPE), b.astype(MXU_DTYPE), (((1,), (1,)), ((), ())), preferred_element_type=F32)


def _mm_tn(a, b):
    return lax.dot_general(a.astype(MXU_DTYPE), b.astype(MXU_DTYPE), (((0,), (0,)), ((), ())), preferred_element_type=F32)


def _colsum(x):
    return jnp.sum(x, axis=0, keepdims=True)


def _layer_norm(x, g, b):
    mu = jnp.mean(x, axis=-1, keepdims=True)
    xc = x - mu
    var = jnp.mean(xc * xc, axis=-1, keepdims=True)
    return xc * lax.rsqrt(var + LN_EPS) * g + b


def _layer_norm_bwd(dy, pre, g):
    mu = jnp.mean(pre, axis=-1, keepdims=True)
    xc = pre - mu
    var = jnp.mean(xc * xc, axis=-1, keepdims=True)
    r = lax.rsqrt(var + LN_EPS)
    xhat = xc * r
    dyg = dy * g
    dpre = r * (dyg - jnp.mean(dyg, axis=-1, keepdims=True) - xhat * jnp.mean(dyg * xhat, axis=-1, keepdims=True))
    return dpre, _colsum(dy * xhat), _colsum(dy)


def _rms_norm(x, g):
    return x * lax.rsqrt(jnp.mean(x * x, axis=-1, keepdims=True) + RMS_EPS) * g


def _rms_norm_bwd(dy, x, g):
    r = lax.rsqrt(jnp.mean(x * x, axis=-1, keepdims=True) + RMS_EPS)
    dyg = dy * g
    dx = dyg * r - x * (r * r * r) * jnp.mean(dyg * x, axis=-1, keepdims=True)
    return dx, _colsum(dy * x * r)


_GELU_C = math.sqrt(2.0 / math.pi)


def _gelu(x):
    return 0.5 * x * (1.0 + jnp.tanh(_GELU_C * (x + 0.044715 * (x * x * x))))


def _gelu_grad(x):
    t = jnp.tanh(_GELU_C * (x + 0.044715 * (x * x * x)))
    return 0.5 * (1.0 + t) + 0.5 * x * (1.0 - t * t) * (_GELU_C * (1.0 + 3.0 * 0.044715 * x * x))


def _rows_call(name, fn, rows, consts, out_rows, out_accs, tm):
    rows = [r if isinstance(r, tuple) else (r, r.shape[1], 0) for r in rows]
    m = rows[0][0].shape[0]
    assert m % tm == 0
    nr, nc, no, na = len(rows), len(consts), len(out_rows), len(out_accs)

    def body(*refs):
        rr, cr = refs[:nr], refs[nr:nr + nc]
        orr, ar = refs[nr + nc:nr + nc + no], refs[nr + nc + no:]
        outs, accs = fn([r[...] for r in rr], cr)
        for o, v in zip(orr, outs, strict=True):
            o[...] = v.astype(o.dtype)
        if na:
            first = pl.program_id(0) == 0

            @pl.when(first)
            def _():
                for a, v in zip(ar, accs, strict=True):
                    a[...] = v

            @pl.when(jnp.logical_not(first))
            def _():
                for a, v in zip(ar, accs, strict=True):
                    a[...] += v

    def whole(shape):
        return pl.BlockSpec(shape, lambda i, n=len(shape): (0,) * n)

    in_specs = [pl.BlockSpec((tm, w), lambda i, cb=cb: (i, cb)) for _, w, cb in rows] + [whole(c.shape) for c in consts]
    out_specs = [pl.BlockSpec((tm, w), lambda i: (i, 0)) for w, _ in out_rows] + [whole(s) for s in out_accs]
    out_shape = [jax.ShapeDtypeStruct((m, w), dt) for w, dt in out_rows] + [jax.ShapeDtypeStruct(s, F32) for s in out_accs]
    res = pl.pallas_call(body, grid=(m // tm,), in_specs=in_specs, out_specs=out_specs, out_shape=out_shape, name=name,
                         compiler_params=_cparams(("arbitrary",)))(*[r[0] for r in rows], *consts)
    return res[:no], res[no:]


def _wgrad_call(name, x, dy, split, nblk, jb, blk_shape, tm, prologue=None):
    dys = list(dy) if isinstance(dy, (list, tuple)) else [dy]
    m = x.shape[0]
    kk, nn = blk_shape
    assert m % tm == 0 and nblk % jb == 0 and (len(dys) == 1 or (split == "y" and jb == nblk))
    xw = kk * jb if split == "x" else x.shape[1]
    yws = [d.shape[1] for d in dys] if (split == "x" or len(dys) > 1) else [nn * jb]
    nrow = m // tm

    def body(x_ref, *rest):
        dy_refs, o_ref, acc_ref = rest[:len(dys)], rest[-2], rest[-1]
        i = pl.program_id(1)

        @pl.when(i == 0)
        def _():
            acc_ref[...] = jnp.zeros_like(acc_ref)

        xv = x_ref[...]
        if prologue is not None:
            xv = prologue(xv)
        xv = xv.astype(MXU_DTYPE)
        dv = [r[...].astype(MXU_DTYPE) for r in dy_refs]
        dv = dv[0] if len(dv) == 1 else jnp.concatenate(dv, axis=-1)
        for j in range(jb):
            xa = xv[:, j * kk:(j + 1) * kk] if split == "x" else xv
            da = dv[:, j * nn:(j + 1) * nn] if split == "y" else dv
            acc_ref[j] += _mm_tn(xa, da)

        @pl.when(i == nrow - 1)
        def _():
            o_ref[...] = acc_ref[...].astype(o_ref.dtype)

    in_specs = [pl.BlockSpec((tm, xw), (lambda j, i: (i, j)) if split == "x" else (lambda j, i: (i, 0)))]
    in_specs += [pl.BlockSpec((tm, yw), (lambda j, i: (i, j)) if (split == "y" and len(dys) == 1) else (lambda j, i: (i, 0)))
                 for yw in yws]
    return pl.pallas_call(
        body, grid=(nblk // jb, nrow), in_specs=in_specs,
        out_specs=pl.BlockSpec((jb, kk, nn), lambda j, i: (j, 0, 0)),
        out_shape=jax.ShapeDtypeStruct((nblk, kk, nn), MXU_DTYPE),
        scratch_shapes=[pltpu.VMEM((jb, kk, nn), F32)], name=name,
        compiler_params=_cparams(("arbitrary", "arbitrary")))(x, *dys)


_HBM_SPEC = pl.BlockSpec(memory_space=pltpu.HBM)
_SEM_SPEC = pl.BlockSpec(memory_space=pltpu.SEMAPHORE)
_EFFECT = pltpu.SideEffectType.DATAFLOW_SIDE_EFFECTING


def _my_index():
    return 4 * lax.axis_index("x") + 2 * lax.axis_index("y") + lax.axis_index("c")


def _peer_copies(ins, lands, send_sems, recv_sems, modes):
    x, y, c = lax.axis_index("x"), lax.axis_index("y"), lax.axis_index("c")
    me = 4 * x + 2 * y + c
    pairs = []
    for k in range(NDEV - 1):
        fx, fy, fc = ((k + 1) >> 2) & 1, ((k + 1) >> 1) & 1, (k + 1) & 1
        px, py, pc = (x + fx) % 2, (y + fy) % 2, (c + fc) % 2
        idx = 4 * px + 2 * py + pc
        for a, md in enumerate(modes):
            src = ins[a] if md == "gather" else ins[a].at[idx]
            sem = a * (NDEV - 1) + k
            common = dict(src_ref=src, send_sem=send_sems.at[sem], recv_sem=recv_sems.at[sem], device_id=(px, py, pc),
                          device_id_type=pl.DeviceIdType.MESH)
            pairs.append((pltpu.make_async_remote_copy(dst_ref=lands[a].at[me], **common),
                          pltpu.make_async_remote_copy(dst_ref=lands[a].at[idx], **common)))
    return pairs


def _exchange_start(name, arrays, modes, after=None):
    n = len(arrays)
    extra = [] if after is None else [after]
    me = _my_index()
    lands = []
    for a, md in zip(arrays, modes, strict=True):
        piece = a if md == "gather" else lax.dynamic_index_in_dim(a, me, 0, keepdims=False)
        lands.append(lax.dynamic_update_index_in_dim(lax.empty((NDEV,) + piece.shape, a.dtype), piece, me, 0))

    def body(*refs):
        ins, lnd = refs[:n], refs[n:2 * n]
        send_sems, recv_sems = refs[2 * n + len(extra)], refs[2 * n + len(extra) + 1]
        token = refs[-1]
        for out_copy, _ in _peer_copies(ins, lnd, send_sems, recv_sems, modes):
            out_copy.start()
        token[...] = jnp.zeros_like(token)

    sems = pltpu.SemaphoreType.DMA((n * (NDEV - 1),))
    thru = [pltpu.HBM(a.shape, a.dtype) for a in list(arrays) + lands]
    res = pl.pallas_call(
        body, name=name, out_shape=(sems, sems, *thru, jax.ShapeDtypeStruct((SUBLANES, LANES), F32)),
        in_specs=[_HBM_SPEC] * (2 * n) + [pl.BlockSpec(memory_space=pl.ANY)] * len(extra),
        out_specs=(_SEM_SPEC, _SEM_SPEC, *([_HBM_SPEC] * (2 * n)), pl.BlockSpec(memory_space=pltpu.VMEM)),
        input_output_aliases={i: 2 + i for i in range(2 * n)},
        compiler_params=pltpu.CompilerParams(has_side_effects=_EFFECT),
    )(*[pltpu.with_memory_space_constraint(a, pltpu.HBM) for a in list(arrays) + lands], *extra)
    return (res[0], res[1], res[2:2 + n], res[2 + n:2 + 2 * n]), res[-1]


def _exchange_wait(name, handle, modes, after):
    send_sems, recv_sems, ins_thru, lands_thru = handle
    n = len(ins_thru)

    def body(*refs):
        ins, lnd = refs[:n], refs[n:2 * n]
        for out_copy, arrival in _peer_copies(ins, lnd, refs[2 * n], refs[2 * n + 1], modes):
            out_copy.wait_send()
            arrival.wait_recv()

    thru = [pltpu.HBM(a.shape, a.dtype) for a in list(ins_thru) + list(lands_thru)]
    res = pl.pallas_call(
        body, name=name, out_shape=tuple(thru),
        in_specs=[_HBM_SPEC] * (2 * n) + [_SEM_SPEC, _SEM_SPEC, pl.BlockSpec(memory_space=pl.ANY)],
        out_specs=tuple([_HBM_SPEC] * (2 * n)), input_output_aliases={i: i for i in range(2 * n)},
        compiler_params=pltpu.CompilerParams(has_side_effects=_EFFECT),
    )(*ins_thru, *lands_thru, send_sems, recv_sems, after)
    return res[n:]


def _rope_tables(positions3, inv_freq_row, sign_row):
    b, s, _ = positions3.shape

    def body(pos_ref, f_ref, sg_ref, c_ref, s_ref):
        ang = pos_ref[...].astype(F32) * f_ref[...]
        c_ref[...] = jnp.cos(ang)
        s_ref[...] = jnp.sin(ang) * sg_ref[...]

    row = pl.BlockSpec((1, LANES), lambda i: (0, 0))
    blk = pl.BlockSpec((None, s, LANES), lambda i: (i, 0, 0))
    return pl.pallas_call(
        body, grid=(b,), in_specs=[pl.BlockSpec((None, s, 1), lambda i: (i, 0, 0)), row, row], out_specs=[blk, blk],
        out_shape=[jax.ShapeDtypeStruct((b, s, LANES), F32)] * 2, name="rope_tables",
        compiler_params=_cparams(("arbitrary",)))(positions3, inv_freq_row, sign_row)


def _swap_halves(t):
    lane = lax.broadcasted_iota(jnp.int32, t.shape, 1)
    half = HEAD_DIM // 2
    return jnp.where((lane % HEAD_DIM) < half, pltpu.roll(t, LANES - half, 1), pltpu.roll(t, half, 1))


def _segment_rows(r, d, s):
    n = s // d
    return (pl.ds(r, n, stride=d) if d > 1 else pl.ds(0, s)), pl.ds(r * n, n)


def _head_lanes(shape):
    lane = lax.broadcasted_iota(jnp.int32, shape, len(shape) - 1)
    return lane < HEAD_DIM


def _bmm_nt(a, b):
    return lax.dot_general(a.astype(MXU_DTYPE), b.astype(MXU_DTYPE), (((2,), (2,)), ((0,), (0,))), preferred_element_type=F32)


def _bmm(a, b):
    return lax.dot_general(a.astype(MXU_DTYPE), b.astype(MXU_DTYPE), (((2,), (1,)), ((0,), (0,))), preferred_element_type=F32)


def _bmm_tn(a, b):
    return lax.dot_general(a.astype(MXU_DTYPE), b.astype(MXU_DTYPE), (((1,), (1,)), ((0,), (0,))), preferred_element_type=F32)


def _stack_heads(t3):
    head_a = _head_lanes(t3.shape)
    zero = jnp.zeros_like(t3)
    return jnp.concatenate([jnp.where(head_a, t3, zero), jnp.where(head_a, zero, t3)], axis=1)


QUNIT = QBLK // 2
ATTN_SPLIT = 2


def _with_previous(t3, nprev):
    shifted = [jnp.concatenate([t3[:k], t3[:-k]], axis=0) for k in range(nprev, 0, -1)]
    return jnp.concatenate(shifted + [t3], axis=1)


def _head_columns(t3):
    return jnp.concatenate([t3[:, :, lo:lo + 1] for lo in range(0, LANES, HEAD_DIM)], axis=1)


def _to_own_unit(t, nprev):
    unit = t.shape[1] // (nprev + 1)
    out = t[:, nprev * unit:]
    for k in range(1, nprev + 1):
        part = t[:, (nprev - k) * unit:(nprev - k + 1) * unit]
        out = out + jnp.concatenate([part[k:], jnp.zeros_like(part[:k])], axis=0)
    return out


def _branch_operands(qh, kh, vh, s, nb):
    nh = LANES // HEAD_DIM
    unit, nprev = (QBLK, 0) if nb == 1 else (QUNIT, QBLK // QUNIT)
    g = s // unit
    q3 = _stack_heads(qh[...].reshape(g, unit, LANES))
    k3, v3 = kh[...].reshape(g, unit, LANES), vh[...].reshape(g, unit, LANES)
    if nprev == 0:
        qi = lax.broadcasted_iota(jnp.int32, (1, nh * unit, unit), 1) % unit
        kj = lax.broadcasted_iota(jnp.int32, (1, nh * unit, unit), 2)
        return q3, k3, v3, kj <= qi, nprev
    shape = (g, nh * unit, (nprev + 1) * unit)
    qi = lax.broadcasted_iota(jnp.int32, shape, 1) % unit
    kj = lax.broadcasted_iota(jnp.int32, shape, 2)
    j = lax.broadcasted_iota(jnp.int32, shape, 0)
    per_block = QBLK // unit
    opens = ((j // per_block) % nb) == 0
    mask = (kj >= qi) & (kj <= qi + QBLK) & ((kj >= QBLK - unit * (j % per_block)) | jnp.logical_not(opens))
    return q3, _with_previous(k3, nprev), _with_previous(v3, nprev), mask, nprev


def _attn_fwd(proj3, ctab, stab, aw):
    b, s, _ = proj3.shape
    npair = aw // LANES
    scale = HEAD_DIM ** -0.5
    nbr = len(DILATION_PAIRS)

    def body(q_ref, k_ref, v_ref, c_ref, s_ref, o_ref, l_ref, qf, kf, qh, kh, vh, op, lp, ob, lb):
        cc, ss = c_ref[...], s_ref[...]
        q2, k2 = q_ref[...], k_ref[...]
        qf[...] = (q2 * cc + _swap_halves(q2) * ss) * scale
        kf[...] = k2 * cc + _swap_halves(k2) * ss
        for br, (window, d) in enumerate(DILATION_PAIRS):
            for r in range(d):
                nat, perm = _segment_rows(r, d, s)
                for dst, src in ((qh, qf), (kh, kf), (vh, v_ref)):
                    dst[perm, :] = src[nat, :].astype(MXU_DTYPE)
            o_dst, l_dst = (ob.at[br], lb.at[br]) if d == 1 else (op, lp)
            q3, kk, vv, mask, _ = _branch_operands(qh, kh, vh, s, (s // d) // QBLK)
            unit = q3.shape[1] // (LANES // HEAD_DIM)
            gh = q3.shape[0] // ATTN_SPLIT
            head_a = _head_lanes((gh, unit, LANES))
            for part in range(ATTN_SPLIT):
                sel = slice(part * gh, (part + 1) * gh)
                sc = jnp.where(mask[sel] if mask.shape[0] > 1 else mask, _bmm_nt(q3[sel], kk[sel]), NEG_INF)
                mx = jnp.max(sc, axis=-1, keepdims=True)
                p = jnp.exp(sc - mx)
                den = jnp.sum(p, axis=-1, keepdims=True)
                o2 = _bmm(p, vv[sel]) / den
                l2 = mx + jnp.log(den)
                rows = pl.ds(part * gh * unit, gh * unit)
                o_dst[rows, :] = jnp.where(head_a, o2[:, :unit], o2[:, unit:]).reshape(gh * unit, LANES)
                l_dst[rows, :] = jnp.where(head_a, l2[:, :unit], l2[:, unit:]).reshape(gh * unit, LANES)
            if d > 1:
                for r in range(d):
                    nat, perm = _segment_rows(r, d, s)
                    ob[br, nat, :] = op[perm, :]
                    lb[br, nat, :] = lp[perm, :]
        ls = [lb[br] for br in range(nbr)]
        mx = functools.reduce(jnp.maximum, ls)
        ws = [jnp.exp(l - mx) for l in ls]
        tot = functools.reduce(lambda a, b_: a + b_, ws)
        o_ref[...] = functools.reduce(lambda a, b_: a + b_, [(w / tot) * ob[br] for br, w in enumerate(ws)])
        l_ref[...] = mx + jnp.log(tot)

    def col(off):
        return pl.BlockSpec((None, s, LANES), lambda bi, hp, off=off: (bi, 0, off + hp))

    tab = pl.BlockSpec((None, s, LANES), lambda bi, hp: (bi, 0, 0))
    f32s = pltpu.VMEM((s, LANES), F32)
    mxs = pltpu.VMEM((s, LANES), MXU_DTYPE)
    br_s = pltpu.VMEM((nbr, s, LANES), F32)
    return pl.pallas_call(
        body, grid=(b, npair), in_specs=[col(0), col(npair), col(2 * npair), tab, tab], out_specs=[col(0), col(0)],
        out_shape=[jax.ShapeDtypeStruct((b, s, aw), F32)] * 2,
        scratch_shapes=[f32s] * 2 + [mxs] * 3 + [f32s] * 2 + [br_s] * 2,
        name="attn_fwd", compiler_params=_cparams(("arbitrary", "arbitrary")))(proj3, proj3, proj3, ctab, stab)


def _attn_bwd(proj3, ctab, stab, dout3, out3, lse3, aw):
    b, s, _ = proj3.shape
    npair = aw // LANES
    scale = HEAD_DIM ** -0.5
    nheads = LANES // HEAD_DIM

    def body(q_ref, k_ref, v_ref, c_ref, s_ref, do_ref, o_ref, l_ref, dq_ref, dk_ref, dv_ref,
             qf, kf, dlf, qh, kh, vh, doh, lpm, dpm, dqp, dkp, dvp, dqn, dkn, dvn):
        cc, ss = c_ref[...], s_ref[...]
        q2, k2 = q_ref[...], k_ref[...]
        qf[...] = (q2 * cc + _swap_halves(q2) * ss) * scale
        kf[...] = k2 * cc + _swap_halves(k2) * ss
        dd = do_ref[...] * o_ref[...]
        in_a = _head_lanes((s, LANES))
        sum_a = jnp.sum(jnp.where(in_a, dd, 0.0), axis=-1, keepdims=True)
        sum_b = jnp.sum(jnp.where(in_a, 0.0, dd), axis=-1, keepdims=True)
        dlf[...] = jnp.where(in_a, sum_a, sum_b)
        for r_ in (dqn, dkn, dvn):
            r_[...] = jnp.zeros_like(r_)
        for window, d in DILATION_PAIRS:
            for r in range(d):
                nat, perm = _segment_rows(r, d, s)
                for dst, src in ((qh, qf), (kh, kf), (vh, v_ref), (doh, do_ref)):
                    dst[perm, :] = src[nat, :].astype(MXU_DTYPE)
                if d > 1:
                    lpm[perm, :] = l_ref[nat, :]
                    dpm[perm, :] = dlf[nat, :]
            l_src, d_src = (l_ref, dlf) if d == 1 else (lpm, dpm)
            q3, kk, vv, mask, nprev = _branch_operands(qh, kh, vh, s, (s // d) // QBLK)
            g_, unit = q3.shape[0], q3.shape[1] // nheads
            head_a = _head_lanes((g_, unit, LANES))
            do3 = _stack_heads(doh[...].reshape(g_, unit, LANES))
            lcol, dcol = _head_columns(l_src[...].reshape(g_, unit, LANES)), _head_columns(d_src[...].reshape(g_, unit, LANES))
            p = jnp.exp(jnp.where(mask, _bmm_nt(q3, kk), NEG_INF) - lcol)
            ds_ = p * (_bmm_nt(do3, vv) - dcol)
            dq2 = _bmm(ds_, kk)
            dq_new = jnp.where(head_a, dq2[:, :unit], dq2[:, unit:]).reshape(s, LANES)
            dk_new = _to_own_unit(_bmm_tn(ds_, q3), nprev).reshape(s, LANES)
            dv_new = _to_own_unit(_bmm_tn(p, do3), nprev).reshape(s, LANES)
            if d == 1:
                dqn[...] += dq_new
                dkn[...] += dk_new
                dvn[...] += dv_new
            else:
                dqp[...] = dq_new
                dkp[...] = dk_new
                dvp[...] = dv_new
                for r in range(d):
                    nat, perm = _segment_rows(r, d, s)
                    dqn[nat, :] += dqp[perm, :]
                    dkn[nat, :] += dkp[perm, :]
                    dvn[nat, :] += dvp[perm, :]
        g = dqn[...] * scale
        dq_ref[...] = (g * cc + _swap_halves(g * ss)).astype(dq_ref.dtype)
        g = dkn[...]
        dk_ref[...] = (g * cc + _swap_halves(g * ss)).astype(dk_ref.dtype)
        dv_ref[...] = dvn[...].astype(dv_ref.dtype)

    def col(off):
        return pl.BlockSpec((None, s, LANES), lambda bi, hp, off=off: (bi, 0, off + hp))

    tab = pl.BlockSpec((None, s, LANES), lambda bi, hp: (bi, 0, 0))
    f32s = pltpu.VMEM((s, LANES), F32)
    mxs = pltpu.VMEM((s, LANES), MXU_DTYPE)
    return pl.pallas_call(
        body, grid=(b, npair), in_specs=[col(0), col(npair), col(2 * npair), tab, tab, col(0), col(0), col(0)],
        out_specs=[col(0)] * 3, out_shape=[jax.ShapeDtypeStruct((b, s, aw), MXU_DTYPE)] * 3,
        scratch_shapes=[f32s] * 3 + [mxs] * 4 + [f32s] * 8,
        name="attn_bwd", compiler_params=_cparams(("arbitrary", "arbitrary")))(
            proj3, proj3, proj3, ctab, stab, dout3, out3, lse3)


def _s5_discretise(a_re, a_im, log_dt, bt_re, bt_im):
    dt = jnp.exp(log_dt)
    mag = jnp.exp(a_re * dt)
    ang = a_im * dt
    lb_re = mag * jnp.cos(ang)
    lb_im = mag * jnp.sin(ang)
    den = a_re * a_re + a_im * a_im
    nr = lb_re - 1.0
    ni = lb_im
    cr = (nr * a_re + ni * a_im) / den
    ci = (ni * a_re - nr * a_im) / den
    return lb_re, lb_im, cr * bt_re - ci * bt_im, cr * bt_im + ci * bt_re


def _s5_params(a_re, a_im, log_dt, bt_re, bt_im):
    def body(ar, ai, ld, br, bi, o1, o2, o3, o4):
        r = _s5_discretise(ar[...], ai[...], ld[...], br[...], bi[...])
        for o, v in zip((o1, o2, o3, o4), r, strict=True):
            o[...] = v

    sd = jax.ShapeDtypeStruct
    return pl.pallas_call(body, out_shape=[sd(a_re.shape, F32)] * 2 + [sd(bt_re.shape, F32)] * 2, name="s5_params")(
        a_re, a_im, log_dt, bt_re, bt_im)


def _s5_params_bwd(a_re, a_im, log_dt, bt_re, bt_im, cts):
    def body(ar, ai, ld, br, bi, c1, c2, c3, c4, o1, o2, o3, o4, o5):
        _, vjp = jax.vjp(_s5_discretise, ar[...], ai[...], ld[...], br[...], bi[...])
        r = vjp((c1[...], c2[...], c3[...], c4[...]))
        for o, v in zip((o1, o2, o3, o4, o5), r, strict=True):
            o[...] = v

    sd = jax.ShapeDtypeStruct
    return pl.pallas_call(
        body, out_shape=[sd(a_re.shape, F32)] * 2 + [sd(log_dt.shape, F32)] + [sd(bt_re.shape, F32)] * 2, name="s5_params_bwd")(
            a_re, a_im, log_dt, bt_re, bt_im, *cts)


S5_TC = 128


def _time_major(tiles):
    nt, tc, lanes = tiles.shape
    return jnp.swapaxes(tiles, 0, 1).reshape(tc * nt, lanes)


def _tile_major(rows, nt):
    return jnp.swapaxes(rows.astype(MXU_DTYPE).reshape(rows.shape[0] // nt, nt, rows.shape[1]), 0, 1)


def _s5_fwd(proj3, ucol, lam_re, lam_im, bre, bim, cre, cim, dvec, sw):
    b, s, _ = proj3.shape
    nt = lam_re.shape[0]
    ncb = sw // LANES
    tpc = nt // ncb
    tc = S5_TC

    def body(u_ref, lr_ref, li_ref, bre_ref, bim_ref, cre_ref, cim_ref, d_ref, y_ref, xr_ref, xi_ref, sr, si):
        @pl.when(pl.program_id(0) == 0)
        def _():
            sr[...] = jnp.zeros_like(sr)
            si[...] = jnp.zeros_like(si)

        u_all = u_ref[...].reshape(b * tc, sw)
        bur, bui = [], []
        for cb in range(ncb):
            ucb = u_all[:, cb * LANES:(cb + 1) * LANES].astype(MXU_DTYPE)
            for t in range(cb * tpc, (cb + 1) * tpc):
                bur.append(_mm(ucb, bre_ref[t]))
                bui.append(_mm(ucb, bim_ref[t]))
        for bi in range(b):
            rows = slice(bi * tc, (bi + 1) * tc)
            xr_ref[bi] = _time_major(jnp.stack([v_[rows] for v_ in bur]))
            xi_ref[bi] = _time_major(jnp.stack([v_[rows] for v_ in bui]))
        lr, li = lr_ref[...], li_ref[...]

        def step(j, carry):
            off = pl.multiple_of(j * nt, nt)
            new = []
            for bi in range(b):
                pr, pi = carry[2 * bi], carry[2 * bi + 1]
                nr = lr * pr - li * pi + xr_ref[bi, pl.ds(off, nt), :]
                ni = lr * pi + li * pr + xi_ref[bi, pl.ds(off, nt), :]
                xr_ref[bi, pl.ds(off, nt), :] = nr
                xi_ref[bi, pl.ds(off, nt), :] = ni
                new += [nr, ni]
            return tuple(new)

        init = tuple(v for bi in range(b) for v in (sr[bi], si[bi]))
        fin = lax.fori_loop(0, tc, step, init, unroll=4)
        for bi in range(b):
            sr[bi] = fin[2 * bi]
            si[bi] = fin[2 * bi + 1]
        xr_t = [_tile_major(xr_ref[bi], nt) for bi in range(b)]
        xi_t = [_tile_major(xi_ref[bi], nt) for bi in range(b)]
        for cb in range(ncb):
            cols = slice(cb * LANES, (cb + 1) * LANES)
            acc = d_ref[:, cols] * u_all[:, cols]
            for t in range(cb * tpc, (cb + 1) * tpc):
                xr_all = jnp.concatenate([xr_t[bi][t] for bi in range(b)], axis=0)
                xi_all = jnp.concatenate([xi_t[bi][t] for bi in range(b)], axis=0)
                acc = acc + (_mm(xr_all, cre_ref[t]) - _mm(xi_all, cim_ref[t]))
            for bi in range(b):
                y_ref[bi, :, cols] = acc[bi * tc:(bi + 1) * tc]

    def whole(a):
        return pl.BlockSpec(a.shape, lambda c, n=a.ndim: (0,) * n)

    xblk = pl.BlockSpec((b, tc * nt, LANES), lambda c: (0, c, 0))
    return pl.pallas_call(
        body, grid=(s // tc,),
        in_specs=[pl.BlockSpec((b, tc, sw), lambda c: (0, c, ucol))] + [whole(a) for a in (lam_re, lam_im, bre, bim, cre, cim, dvec)],
        out_specs=[pl.BlockSpec((b, tc, sw), lambda c: (0, c, 0)), xblk, xblk],
        out_shape=[jax.ShapeDtypeStruct((b, s, sw), F32)] + [jax.ShapeDtypeStruct((b, s * nt, LANES), F32)] * 2,
        scratch_shapes=[pltpu.VMEM((b, nt, LANES), F32)] * 2, name="s5_fwd",
        compiler_params=_cparams(("arbitrary",)))(proj3, lam_re, lam_im, bre, bim, cre, cim, dvec)


def _s5_bwd(proj3, ucol, dy3, xr3, xi3, lam_re, lam_im, bre, bim, cre, cim, dvec, sw):
    b, s, _ = proj3.shape
    nt = lam_re.shape[0]
    ncb = sw // LANES
    tpc = nt // ncb
    tc = S5_TC
    nchunk = s // tc

    def body(u_ref, dy_ref, xr_ref, xi_ref, pr_ref, pi_ref, lr_ref, li_ref, bre_ref, bim_ref, cre_ref, cim_ref, d_ref,
             du_ref, dbre, dbim, dcre, dcim, dlr, dli, dd, gr, gi, sr, si):
        step_id = pl.program_id(0)

        @pl.when(step_id == 0)
        def _():
            for r in (sr, si, dbre, dbim, dcre, dcim, dlr, dli, dd):
                r[...] = jnp.zeros_like(r)

        u_all = u_ref[...].reshape(b * tc, sw)
        dy_all = dy_ref[...].reshape(b * tc, sw)
        dxr, dxi = [], []
        for cb in range(ncb):
            dycb = dy_all[:, cb * LANES:(cb + 1) * LANES].astype(MXU_DTYPE)
            for t in range(cb * tpc, (cb + 1) * tpc):
                dxr.append(_mm_nt(dycb, cre_ref[t]))
                dxi.append(-_mm_nt(dycb, cim_ref[t]))
        for bi in range(b):
            rows = slice(bi * tc, (bi + 1) * tc)
            gr[bi] = _time_major(jnp.stack([v_[rows] for v_ in dxr]))
            gi[bi] = _time_major(jnp.stack([v_[rows] for v_ in dxi]))
        lr, li = lr_ref[...], li_ref[...]

        has_prev = (step_id != nchunk - 1).astype(F32)
        before = [(pr_ref[bi] * has_prev, pi_ref[bi] * has_prev) for bi in range(b)]

        def step(jj, carry):
            t = tc - 1 - jj
            off = pl.multiple_of(t * nt, nt)
            poff = pl.multiple_of(jnp.maximum(t - 1, 0) * nt, nt)
            new = []
            alr, ali = carry[-2], carry[-1]
            for bi in range(b):
                nr_, ni_ = carry[2 * bi], carry[2 * bi + 1]
                vr = gr[bi, pl.ds(off, nt), :] + lr * nr_ + li * ni_
                vi = gi[bi, pl.ds(off, nt), :] + lr * ni_ - li * nr_
                gr[bi, pl.ds(off, nt), :] = vr
                gi[bi, pl.ds(off, nt), :] = vi
                xpr = jnp.where(t > 0, xr_ref[bi, pl.ds(poff, nt), :], before[bi][0])
                xpi = jnp.where(t > 0, xi_ref[bi, pl.ds(poff, nt), :], before[bi][1])
                alr = alr + (vr * xpr + vi * xpi)
                ali = ali + (vi * xpr - vr * xpi)
                new += [vr, vi]
            return tuple(new) + (alr, ali)

        zero = jnp.zeros((nt, LANES), F32)
        init = tuple(v for bi in range(b) for v in (sr[bi], si[bi])) + (zero, zero)
        fin = lax.fori_loop(0, tc, step, init, unroll=4)
        for bi in range(b):
            sr[bi] = fin[2 * bi]
            si[bi] = fin[2 * bi + 1]
        dlr[...] += fin[-2]
        dli[...] += fin[-1]

        tiles = [[_tile_major(ref[bi], nt) for bi in range(b)] for ref in (gr, gi, xr_ref, xi_ref)]

        def stacked(k, t):
            return jnp.concatenate([tiles[k][bi][t] for bi in range(b)], axis=0)

        for cb in range(ncb):
            cols = slice(cb * LANES, (cb + 1) * LANES)
            ucb32, dycb32 = u_all[:, cols], dy_all[:, cols]
            ucb, dycb = ucb32.astype(MXU_DTYPE), dycb32.astype(MXU_DTYPE)
            acc = d_ref[:, cols] * dycb32
            for t in range(cb * tpc, (cb + 1) * tpc):
                vr, vi = stacked(0, t), stacked(1, t)
                acc = acc + (_mm_nt(vr, bre_ref[t]) + _mm_nt(vi, bim_ref[t]))
                dbre[t] += _mm_tn(ucb, vr)
                dbim[t] += _mm_tn(ucb, vi)
                dcre[t] += _mm_tn(stacked(2, t), dycb)
                dcim[t] -= _mm_tn(stacked(3, t), dycb)
            for bi in range(b):
                du_ref[bi, :, cols] = acc[bi * tc:(bi + 1) * tc].astype(du_ref.dtype)
            dd[:, cols] += _colsum(dycb32 * ucb32)

    def whole(a):
        return pl.BlockSpec(a.shape, lambda c, n=len(a.shape): (0,) * n)

    def rev(c):
        return nchunk - 1 - c

    xblk = pl.BlockSpec((b, tc * nt, LANES), lambda c: (0, rev(c), 0))
    prev = pl.BlockSpec((b, nt, LANES), lambda c: (0, jnp.maximum(rev(c) * tc - 1, 0), 0))
    sd = jax.ShapeDtypeStruct
    blk = sd(bre.shape, F32)
    acc_shapes = [blk, blk, sd(cre.shape, F32), sd(cre.shape, F32), sd(lam_re.shape, F32), sd(lam_re.shape, F32), sd((1, sw), F32)]
    return pl.pallas_call(
        body, grid=(nchunk,),
        in_specs=[pl.BlockSpec((b, tc, sw), lambda c: (0, rev(c), ucol)), pl.BlockSpec((b, tc, sw), lambda c: (0, rev(c), 0)),
                  xblk, xblk, prev, prev] + [whole(a) for a in (lam_re, lam_im, bre, bim, cre, cim, dvec)],
        out_specs=[pl.BlockSpec((b, tc, sw), lambda c: (0, rev(c), 0))] + [whole(a) for a in acc_shapes],
        out_shape=[sd((b, s, sw), MXU_DTYPE)] + acc_shapes,
        scratch_shapes=[pltpu.VMEM((b, tc * nt, LANES), F32)] * 2 + [pltpu.VMEM((b, nt, LANES), F32)] * 2, name="s5_bwd",
        compiler_params=_cparams(("arbitrary",)))(proj3, dy3, xr3, xi3, xr3, xi3, lam_re, lam_im, bre, bim, cre, cim, dvec)


def _s5_blocks(bb, cc_, ncb):
    g, n, p = bb.shape
    mask = jnp.asarray(_s5_tile_mask(g, p, ncb))
    nt, gpt, gpc = mask.shape
    bbt, cct = bb.reshape(nt, gpt, n, p), cc_.reshape(nt, gpt, n, p)
    spread = mask[:, :, :, None, None]
    bblk = (bbt[:, :, None] * spread).transpose(0, 2, 3, 1, 4).reshape(nt, gpc * n, gpt * p)
    cblk = (cct[:, :, None] * spread).transpose(0, 1, 4, 2, 3).reshape(nt, gpt * p, gpc * n)
    return bblk, cblk


def _s5_tile_mask(g, p, ncb):
    nt, gpt, gpc = g * p // LANES, LANES // p, g // ncb
    mask = np.zeros((nt, gpt, gpc), np.float32)
    for t in range(nt):
        for gl in range(gpt):
            mask[t, gl, (t * gpt + gl) % gpc] = 1.0
    return mask


def _s5_unblock(dbblk, dcblk, g, n, p, ncb):
    mask = jnp.asarray(_s5_tile_mask(g, p, ncb))
    nt, gpt, gpc = mask.shape
    db = jnp.sum(dbblk.reshape(nt, gpc, n, gpt, p) * mask.transpose(0, 2, 1)[:, :, None, :, None], axis=1)
    dc = jnp.sum(dcblk.reshape(nt, gpt, p, gpc, n) * mask[:, :, None, :, None], axis=3)
    return db.transpose(0, 2, 1, 3).reshape(g, n, p), dc.transpose(0, 1, 3, 2).reshape(g, n, p)


def _adamw_math(w, g, m, v):
    m = ADAM_B1 * m + (1.0 - ADAM_B1) * g
    v = ADAM_B2 * v + (1.0 - ADAM_B2) * (g * g)
    m_hat = m / (1.0 - ADAM_B1 ** ADAM_STEP)
    v_hat = v / (1.0 - ADAM_B2 ** ADAM_STEP)
    delta = -ADAM_LR * (m_hat / (jnp.sqrt(v_hat) + ADAM_EPS) + ADAM_WD * w)
    return delta, m, v


def _adamw_layer(name, parts, w, m, v, layer, prev, tr):
    nparts, r, c = parts.shape
    assert r % tr == 0
    if prev is None:
        prev = [lax.empty(w.shape, F32) for _ in range(4)]

    def body(p_ref, w_ref, m_ref, v_ref, *rest):
        g_out, d_out, m_out, v_out = rest[4:]
        g = p_ref[0].astype(F32)
        for k in range(1, nparts):
            g = g + p_ref[k].astype(F32)
        delta, mn, vn = _adamw_math(w_ref[...], g, m_ref[...], v_ref[...])
        g_out[...] = g
        d_out[...] = delta
        m_out[...] = mn
        v_out[...] = vn

    blk = pl.BlockSpec((None, tr, c), lambda i: (layer, i, 0))
    kept = pl.BlockSpec(memory_space=pl.ANY)
    return pl.pallas_call(
        body, grid=(r // tr,), in_specs=[pl.BlockSpec((nparts, tr, c), lambda i: (0, i, 0)), blk, blk, blk] + [kept] * 4,
        out_specs=[blk] * 4, out_shape=[jax.ShapeDtypeStruct(w.shape, F32)] * 4, name=name,
        input_output_aliases={4 + k: k for k in range(4)},
        compiler_params=_cparams(("arbitrary",)))(parts, w, m, v, *prev)


def _piece_rows(size):
    unit = SUBLANES * LANES
    return -(-size // unit) * SUBLANES


def _pack_small(pieces, lead=0):
    rows = []
    for p in pieces:
        head = p.shape[:lead]
        flat = p.reshape(head + (-1,))
        nrow = _piece_rows(flat.shape[-1])
        flat = jnp.pad(flat, [(0, 0)] * lead + [(0, nrow * LANES - flat.shape[-1])])
        rows.append(flat.reshape(head + (nrow, LANES)))
    return jnp.concatenate(rows, axis=lead)


def _step(x, positions, weights, moments_m, moments_v, loss_target, distributed):
    f32 = F32
    bsz, seq, dm = x.shape
    depth = weights["w_in"].shape[0]
    aw = weights["attn_gain"].shape[1]
    sw = weights["ssm_gain"].shape[1]
    dff = weights["b_ff1"].shape[1]
    ng, npst = weights["ssm_a_re"].shape[1:]
    gdim = weights["ssm_d"].shape[2]
    in_w = 3 * aw + sw
    t_rows = bsz * seq
    alpha = (2.0 * depth) ** 0.25
    ncb = sw // LANES
    nt = ng * npst // LANES
    ndev = NDEV if distributed else 1
    tm, tl = 512, 1024

    gather_modes = ["gather"] * len(BIG_NAMES)
    no_token = jnp.zeros((SUBLANES, LANES), f32)

    def shards(l, names):
        return [weights[n][l].astype(MXU_DTYPE) for n in names]

    if distributed:
        first, _ = _exchange_start("weights_start_l0_in", shards(0, BIG_NAMES[:1]), gather_modes[:1])

    half = HEAD_DIM // 2
    inv_freq = ROPE_THETA ** (-jnp.arange(half, dtype=f32) * 2.0 / HEAD_DIM)
    reps = LANES // half
    inv_row = jnp.tile(inv_freq, reps)[None, :]
    sign_row = jnp.tile(jnp.concatenate([-jnp.ones((half,), f32), jnp.ones((half,), f32)]), LANES // HEAD_DIM)[None, :]
    ctab, stab = _rope_tables(positions[..., None], inv_row, sign_row)

    def row(v):
        return v.reshape(1, -1)

    s5_prep = []
    for l in range(depth):
        a_re, a_im = weights["ssm_a_re"][l][:, None, :], weights["ssm_a_im"][l][:, None, :]
        log_dt = weights["ssm_log_dt"][l][:, None, None]
        bt_re = weights["ssm_b_re"][l].transpose(0, 2, 1)
        bt_im = weights["ssm_b_im"][l].transpose(0, 2, 1)
        lb_re, lb_im, bb_re, bb_im = _s5_params(a_re, a_im, log_dt, bt_re, bt_im)
        bre, cre = _s5_blocks(bb_re, weights["ssm_c_re"][l], ncb)
        bim, cim = _s5_blocks(bb_im, weights["ssm_c_im"][l], ncb)
        s5_prep.append(dict(lam=(lb_re.reshape(nt, LANES), lb_im.reshape(nt, LANES)),
                            s5c=[a_.astype(MXU_DTYPE) for a_ in (bre, bim, cre, cim)], dvec=row(weights["ssm_d"][l]),
                            s5in=(a_re, a_im, log_dt, bt_re, bt_im)))

    if distributed:
        (g_in0,) = _exchange_wait("weights_wait_l0_in", first, gather_modes[:1], s5_prep[-1]["s5c"][-1])
        rest0, tok_rest0 = _exchange_start("weights_start_l0_rest", shards(0, BIG_NAMES[1:]), gather_modes[1:], after=g_in0)

    h = x.reshape(t_rows, dm)
    saved = []
    for l in range(depth):
        tok_in = tok_mix = no_token
        if not distributed:
            g_in, g_glu, g_out, g_ff1, g_ff2 = [weights[n][l].astype(MXU_DTYPE)[None] for n in BIG_NAMES]
        elif l == 0:
            g_in, tok_in = g_in0, tok_rest0
        else:
            g_in, g_glu, g_out, g_ff1, g_ff2 = _exchange_wait(f"weights_wait_l{l}", next_gather, gather_modes, h)
            if l + 1 < depth:
                next_gather, tok_in = _exchange_start(f"weights_start_l{l + 1}", shards(l + 1, BIG_NAMES), gather_modes, after=g_in)
        w_in_l = g_in

        def in_proj(rv, cr):
            return [jnp.concatenate([_mm(rv[0], cr[0][j]) for j in range(ndev)], axis=-1)], []

        (proj,), _ = _rows_call("in_proj", in_proj, [h], [w_in_l, tok_in], [(in_w, f32)], [], tl)
        proj3 = proj.reshape(bsz, seq, in_w)
        attn3, lse3 = _attn_fwd(proj3, ctab, stab, aw)
        if distributed and l == 0:
            g_glu, g_out, g_ff1, g_ff2 = _exchange_wait("weights_wait_l0_rest", rest0, gather_modes[1:], attn3)
            if depth > 1:
                next_gather, tok_mix = _exchange_start("weights_start_l1", shards(1, BIG_NAMES), gather_modes, after=g_glu)
        w_glu_l = g_glu.reshape(sw, sw)
        w_out_l = g_out.reshape(dm, dm)
        w_ff1_l = g_ff1
        w_ff2_l = g_ff2.reshape(dff, dm)

        (lam_re, lam_im), s5c, dvec = s5_prep[l]["lam"], s5_prep[l]["s5c"], s5_prep[l]["dvec"]
        ypre3, xr3, xi3 = _s5_fwd(proj3, 3 * aw // sw, lam_re, lam_im, *s5c, dvec, sw)
        attn, ypre = attn3.reshape(t_rows, aw), ypre3.reshape(t_rows, sw)

        b_glu, ga, gs = row(weights["b_glu"][l]), row(weights["attn_gain"][l]), row(weights["ssm_gain"][l])

        def mix(rv, cr):
            at, yp = rv
            g = _gelu(yp)
            ssm = g * jax.nn.sigmoid(_mm(g, cr[0][...]) + cr[1][...])
            return [jnp.concatenate([_rms_norm(at, cr[2][...]), _rms_norm(ssm, cr[3][...])], axis=-1)], []

        (mixed,), _ = _rows_call("mix", mix, [attn, ypre], [w_glu_l, b_glu, ga, gs, tok_mix], [(dm, MXU_DTYPE)], [], tl)

        b_out, g1, b1 = row(weights["b_out"][l]), row(weights["ln1_g"][l]), row(weights["ln1_b"][l])

        def out_proj(rv, cr):
            pre = alpha * rv[1] + _mm(rv[0], cr[0][...]) + cr[1][...]
            return [pre, _layer_norm(pre, cr[2][...], cr[3][...])], []

        (pre1, h1), _ = _rows_call("out_proj", out_proj, [mixed, h], [w_out_l, b_out, g1, b1], [(dm, f32), (dm, f32)], [], tl)

        b_ff1 = row(weights["b_ff1"][l])

        def ff1(rv, cr):
            pre_act = jnp.concatenate([_mm(rv[0], cr[0][j]) for j in range(ndev)], axis=-1) + cr[1][...]
            return [jnp.square(jnp.maximum(pre_act, 0.0))], []

        (act,), _ = _rows_call("ff1", ff1, [h1], [w_ff1_l, b_ff1], [(dff, MXU_DTYPE)], [], tm)

        b_ff2, g2, b2 = row(weights["b_ff2"][l]), row(weights["ln2_g"][l]), row(weights["ln2_b"][l])

        def ff2(rv, cr):
            pre = alpha * rv[1] + _mm(rv[0], cr[0][...]) + cr[1][...]
            return [pre, _layer_norm(pre, cr[2][...], cr[3][...])], []

        (pre2, h2), _ = _rows_call("ff2", ff2, [act, h1], [w_ff2_l, b_ff2, g2, b2], [(dm, f32), (dm, f32)], [], tm)

        saved.append(dict(h=h, proj3=proj3, attn=attn, lse3=lse3, attn3=attn3, ypre=ypre, xr3=xr3, xi3=xi3, mixed=mixed, pre1=pre1,
                          h1=h1, act=act, pre2=pre2, w_in=w_in_l, w_glu=w_glu_l, w_out=w_out_l, w_ff1=w_ff1_l, w_ff2=w_ff2_l,
                          lam=(lam_re, lam_im), s5c=s5c, dvec=dvec, b_glu=b_glu, ga=ga, gs=gs, g1=g1, g2=g2,
                          s5in=s5_prep[l]["s5in"]))
        h = h2

    g2_last = saved[-1]["g2"]

    def loss_fn(rv, cr):
        y, tgt, pre = rv
        err = y - tgt
        part = 0.5 * jnp.sum(jnp.mean(err * err, axis=-1, keepdims=True), axis=0, keepdims=True)
        dpre, dg, db = _layer_norm_bwd(err * (1.0 / dm), pre, cr[0][...])
        return [dpre], [jnp.broadcast_to(part, (1, LANES)), dg, db, _colsum(dpre)]

    (dpre2,), (loss_acc, dg2, db2, dbff2) = _rows_call(
        "loss", loss_fn, [h, loss_target.reshape(t_rows, dm), saved[-1]["pre2"]], [g2_last], [(dm, f32)],
        [(1, LANES), (1, dm), (1, dm), (1, dm)], tl)
    loss = loss_acc[0, 0]
    if distributed:
        loss = lax.psum(loss, MESH_AXES)

    big_parts = {n: [None] * depth for n in BIG_NAMES}
    small_parts = [None] * depth
    grad_handles, ff_handles = [None] * depth, [None] * depth
    grad_modes = ["scatter"] * len(MIXER_NAMES) + ["gather"]
    grad_x = None
    for l in reversed(range(depth)):
        sv = saved[l]

        def ff2_bwd(rv, cr):
            r = rv[1].astype(F32)
            relu2 = jnp.where(r > 0.0, (2.0 * r) * lax.rsqrt(r), 0.0)
            da = _mm_nt(rv[0], cr[0][...]) * relu2
            return [da], [_colsum(da)]

        (dact,), (dbff1,) = _rows_call("ff2_bwd", ff2_bwd, [dpre2, sv["act"]], [sv["w_ff2"]], [(dff, MXU_DTYPE)], [(1, dff)], tm)
        big_parts["w_ff2"][l] = _wgrad_call("wgrad_ff2", sv["act"], dpre2, "x", ndev, ndev,
                                            (dff // ndev, dm), tm)
        big_parts["w_ff1"][l] = _wgrad_call("wgrad_ff1", sv["h1"], dact, "y", ndev, ndev,
                                            (dm, dff // ndev), tm)
        ff_grads = [big_parts[n][l] for n in FF_NAMES]
        if distributed:
            ff_handles[l], ff_token = _exchange_start(f"grads_ff_start_l{l}", ff_grads, ["scatter"] * len(FF_NAMES))
        else:
            ff_handles[l], ff_token = ff_grads, no_token

        def ff1_bwd(rv, cr):
            dacc = alpha * rv[1]
            wpb = dff // ndev
            for j in range(ndev):
                dacc = dacc + _mm_nt(rv[0][:, j * wpb:(j + 1) * wpb], cr[0][j])
            dpre, dg, db = _layer_norm_bwd(dacc, rv[2], cr[1][...])
            return [dpre], [dg, db, _colsum(dpre)]

        (dpre1,), (dg1, db1, dbout) = _rows_call("ff1_bwd", ff1_bwd, [dact, dpre2, sv["pre1"]],
                                                 [sv["w_ff1"], sv["g1"], ff_token], [(dm, f32)], [(1, dm)] * 3, tm)
        big_parts["w_out"][l] = _wgrad_call("wgrad_out", sv["mixed"], dpre1, "x", ndev, ndev, (dm // ndev, dm), tl)

        def mix_bwd(rv, cr):
            dp, at, yp = rv
            w_out_r, w_glu_r, bg, ga_, gs_ = cr
            dmixed = _mm_nt(dp, w_out_r[...])
            g = _gelu(yp)
            sig = jax.nn.sigmoid(_mm(g, w_glu_r[...]) + bg[...])
            ssm = g * sig
            dat, dga = _rms_norm_bwd(dmixed[:, :aw], at, ga_[...])
            dssm, dgs = _rms_norm_bwd(dmixed[:, aw:], ssm, gs_[...])
            dz = dssm * g * sig * (1.0 - sig)
            dg = dssm * sig + _mm_nt(dz, w_glu_r[...])
            return [dat, dg * _gelu_grad(yp), dz], [dga, dgs, _colsum(dz)]

        (dattn, dypre, dz), (dga, dgs, dbglu) = _rows_call(
            "mix_bwd", mix_bwd, [dpre1, sv["attn"], sv["ypre"]], [sv["w_out"], sv["w_glu"], sv["b_glu"], sv["ga"], sv["gs"]],
            [(aw, f32), (sw, f32), (sw, MXU_DTYPE)], [(1, aw), (1, sw), (1, sw)], tl)
        big_parts["w_glu"][l] = _wgrad_call("wgrad_glu", sv["ypre"], dz, "x", 1, 1, (sw, sw), tl, prologue=_gelu).reshape(
            ndev, sw // ndev, sw)

        du3, dbre, dbim, dcre, dcim, dlr, dli, dd = _s5_bwd(
            sv["proj3"], 3 * aw // sw, dypre.reshape(bsz, seq, sw), sv["xr3"], sv["xi3"], *sv["lam"], *sv["s5c"], sv["dvec"], sw)
        dbb_re, dc_re = _s5_unblock(dbre, dcre, ng, gdim, npst, ncb)
        dbb_im, dc_im = _s5_unblock(dbim, dcim, ng, gdim, npst, ncb)
        da_re, da_im, dldt, dbt_re, dbt_im = _s5_params_bwd(
            *sv["s5in"], (dlr.reshape(ng, 1, npst), dli.reshape(ng, 1, npst), dbb_re, dbb_im))

        dq3, dk3, dv3 = _attn_bwd(sv["proj3"], ctab, stab, dattn.reshape(bsz, seq, aw), sv["attn3"], sv["lse3"], aw)
        dproj = [dq3.reshape(t_rows, aw), dk3.reshape(t_rows, aw), dv3.reshape(t_rows, aw), du3.reshape(t_rows, sw)]
        big_parts["w_in"][l] = _wgrad_call("wgrad_in", sv["h"], dproj, "y", ndev, ndev, (dm, in_w // ndev), tl)

        small_parts[l] = dict(
            attn_gain=dga, ssm_gain=dgs, ssm_a_re=da_re[:, 0], ssm_a_im=da_im[:, 0], ssm_log_dt=dldt[:, 0, 0], ssm_b_re=dbt_re.transpose(0, 2, 1),
            ssm_b_im=dbt_im.transpose(0, 2, 1), ssm_c_re=dc_re, ssm_c_im=dc_im, ssm_d=dd, b_glu=dbglu, b_out=dbout, ln1_g=dg1,
            ln1_b=db1, b_ff1=dbff1, b_ff2=dbff2, ln2_g=dg2, ln2_b=db2)

        layer_grads = [big_parts[n][l] for n in MIXER_NAMES] + [_pack_small([small_parts[l][n] for n in SMALL_NAMES])]
        if distributed:
            grad_handles[l], token = _exchange_start(f"grads_start_l{l}", layer_grads, grad_modes)
        else:
            grad_handles[l], token = layer_grads, no_token

        wpb = in_w // ndev
        if l > 0:
            prev = saved[l - 1]

            def in_bwd(rv, cr):
                dacc = alpha * rv[4]
                dp = jnp.concatenate([v_.astype(MXU_DTYPE) for v_ in rv[:4]], axis=-1)
                for j in range(ndev):
                    dacc = dacc + _mm_nt(dp[:, j * wpb:(j + 1) * wpb], cr[0][j])
                dpre, dg, db = _layer_norm_bwd(dacc, rv[5], cr[1][...])
                return [dpre], [dg, db, _colsum(dpre)]

            (dpre2,), (dg2, db2, dbff2) = _rows_call("in_bwd", in_bwd, dproj + [dpre1, prev["pre2"]],
                                                     [sv["w_in"], prev["g2"], token], [(dm, f32)], [(1, dm)] * 3, tl)
        else:
            def in_bwd0(rv, cr):
                dacc = alpha * rv[4]
                dp = jnp.concatenate([v_.astype(MXU_DTYPE) for v_ in rv[:4]], axis=-1)
                for j in range(ndev):
                    dacc = dacc + _mm_nt(dp[:, j * wpb:(j + 1) * wpb], cr[0][j])
                return [dacc], []

            (grad_x,), _ = _rows_call("in_bwd0", in_bwd0, dproj + [dpre1], [sv["w_in"], token], [(dm, f32)], [], tl)

    small_shapes = [weights[n].shape[1:] for n in SMALL_NAMES]
    outs = {n: None for n in BIG_NAMES}
    packed = [_pack_small([d[n] for n in SMALL_NAMES], lead=1) for d in (weights, moments_m, moments_v)]
    small_out = None
    after = grad_x
    for l in reversed(range(depth)):
        if distributed:
            recv_ff = _exchange_wait(f"grads_ff_wait_l{l}", ff_handles[l], ["scatter"] * len(FF_NAMES), after)
            recv = _exchange_wait(f"grads_wait_l{l}", grad_handles[l], grad_modes, recv_ff[0])
        else:
            recv_ff = ff_handles[l]
            recv = [g_[None] if i == len(MIXER_NAMES) else g_ for i, g_ in enumerate(grad_handles[l])]
        for n, parts in zip(FF_NAMES + MIXER_NAMES, list(recv_ff) + list(recv[:-1]), strict=True):
            outs[n] = _adamw_layer("adamw_" + n, parts, weights[n], moments_m[n], moments_v[n], l, outs[n],
                                   min(parts.shape[1], 256))
        small_out = _adamw_layer("adamw_small", recv[-1], *packed, l, small_out, recv[-1].shape[1])
        after = small_out[0]
    for k in range(4):
        row = 0
        for n, shp in zip(SMALL_NAMES, small_shapes, strict=True):
            sz = int(np.prod(shp))
            nrow = _piece_rows(sz)
            piece = small_out[k][:, row:row + nrow].reshape(depth, nrow * LANES)
            outs.setdefault(n, [None] * 4)
            outs[n][k] = piece[:, :sz].reshape((depth,) + tuple(shp))
            row += nrow

    result = [loss, grad_x.reshape(bsz, seq, dm)]
    for k in range(4):
        result += [outs[n][k] for n in WEIGHT_ORDER]
    return tuple(result)


def kernel(x, positions, w_in, attn_gain, ssm_gain, ssm_a_re, ssm_a_im, ssm_log_dt, ssm_b_re, ssm_b_im, ssm_c_re, ssm_c_im, ssm_d, w_glu, b_glu, w_out, b_out, ln1_g, ln1_b, w_ff1, b_ff1, w_ff2, b_ff2, ln2_g, ln2_b, loss_target, m_w_in, m_attn_gain, m_ssm_gain, m_ssm_a_re, m_ssm_a_im, m_ssm_log_dt, m_ssm_b_re, m_ssm_b_im, m_ssm_c_re, m_ssm_c_im, m_ssm_d, m_w_glu, m_b_glu, m_w_out, m_b_out, m_ln1_g, m_ln1_b, m_w_ff1, m_b_ff1, m_w_ff2, m_b_ff2, m_ln2_g, m_ln2_b, v_w_in, v_attn_gain, v_ssm_gain, v_ssm_a_re, v_ssm_a_im, v_ssm_log_dt, v_ssm_b_re, v_ssm_b_im, v_ssm_c_re, v_ssm_c_im, v_ssm_d, v_w_glu, v_b_glu, v_w_out, v_b_out, v_ln1_g, v_ln1_b, v_w_ff1, v_b_ff1, v_w_ff2, v_b_ff2, v_ln2_g, v_ln2_b):
    loc = locals()
    weights = {n: loc[n] for n in WEIGHT_ORDER}
    moments_m = {n: loc["m_" + n] for n in WEIGHT_ORDER}
    moments_v = {n: loc["v_" + n] for n in WEIGHT_ORDER}
    return _step(x, positions, weights, moments_m, moments_v, loss_target, distributed=True)
```

```python
import functools
import math

import jax
import jax.numpy as jnp
import numpy as np
from jax import lax
from jax.experimental import pallas as pl
from jax.experimental.pallas import tpu as pltpu

F32 = jnp.float32
MXU_DTYPE = jnp.bfloat16

HEAD_DIM = 64
DILATION_PAIRS = ((128, 1), (512, 4), (2048, 16))
ROPE_THETA = 10000.0
LN_EPS = 1e-5
RMS_EPS = 1e-6
NEG_INF = -1e30
ADAM_LR, ADAM_B1, ADAM_B2, ADAM_EPS, ADAM_WD, ADAM_STEP = 0.001, 0.9, 0.999, 1e-08, 0.01, 10

LANES = 128
SUBLANES = 8
QBLK = 128
VMEM_LIMIT = 56 * 2**20
MESH_AXES = ("x", "y", "c")
NDEV = 8

SMALL_NAMES = ("attn_gain", "ssm_gain", "ssm_a_re", "ssm_a_im", "ssm_log_dt", "ssm_b_re", "ssm_b_im", "ssm_c_re",
               "ssm_c_im", "ssm_d", "b_glu", "b_out", "ln1_g", "ln1_b", "b_ff1", "b_ff2", "ln2_g", "ln2_b")
BIG_NAMES = ("w_in", "w_glu", "w_out", "w_ff1", "w_ff2")
MIXER_NAMES, FF_NAMES = BIG_NAMES[:3], BIG_NAMES[3:]
WEIGHT_ORDER = ("w_in", "attn_gain", "ssm_gain", "ssm_a_re", "ssm_a_im", "ssm_log_dt", "ssm_b_re", "ssm_b_im", "ssm_c_re",
                "ssm_c_im", "ssm_d", "w_glu", "b_glu", "w_out", "b_out", "ln1_g", "ln1_b", "w_ff1", "b_ff1", "w_ff2",
                "b_ff2", "ln2_g", "ln2_b")


def _cparams(sem=None):
    return pltpu.CompilerParams(dimension_semantics=sem, vmem_limit_bytes=VMEM_LIMIT)


def _mm(a, b):
    return jnp.dot(a.astype(MXU_DTYPE), b.astype(MXU_DTYPE), preferred_element_type=F32)


def _mm_nt(a, b):
    return lax.dot_general(a.astype(MXU_DTYPE), b.astype(MXU_DTYPE), (((1,), (1,)), ((), ())), preferred_element_type=F32)


def _mm_tn(a, b):
    return lax.dot_general(a.astype(MXU_DTYPE), b.astype(MXU_DTYPE), (((0,), (0,)), ((), ())), preferred_element_type=F32)


def _colsum(x):
    return jnp.sum(x, axis=0, keepdims=True)


def _layer_norm(x, g, b):
    mu = jnp.mean(x, axis=-1, keepdims=True)
    xc = x - mu
    var = jnp.mean(xc * xc, axis=-1, keepdims=True)
    return xc * lax.rsqrt(var + LN_EPS) * g + b


def _layer_norm_bwd(dy, pre, g):
    mu = jnp.mean(pre, axis=-1, keepdims=True)
    xc = pre - mu
    var = jnp.mean(xc * xc, axis=-1, keepdims=True)
    r = lax.rsqrt(var + LN_EPS)
    xhat = xc * r
    dyg = dy * g
    dpre = r * (dyg - jnp.mean(dyg, axis=-1, keepdims=True) - xhat * jnp.mean(dyg * xhat, axis=-1, keepdims=True))
    return dpre, _colsum(dy * xhat), _colsum(dy)


def _rms_norm(x, g):
    return x * lax.rsqrt(jnp.mean(x * x, axis=-1, keepdims=True) + RMS_EPS) * g


def _rms_norm_bwd(dy, x, g):
    r = lax.rsqrt(jnp.mean(x * x, axis=-1, keepdims=True) + RMS_EPS)
    dyg = dy * g
    dx = dyg * r - x * (r * r * r) * jnp.mean(dyg * x, axis=-1, keepdims=True)
    return dx, _colsum(dy * x * r)


_GELU_C = math.sqrt(2.0 / math.pi)


def _gelu(x):
    return 0.5 * x * (1.0 + jnp.tanh(_GELU_C * (x + 0.044715 * (x * x * x))))


def _gelu_grad(x):
    t = jnp.tanh(_GELU_C * (x + 0.044715 * (x * x * x)))
    return 0.5 * (1.0 + t) + 0.5 * x * (1.0 - t * t) * (_GELU_C * (1.0 + 3.0 * 0.044715 * x * x))


def _rows_call(name, fn, rows, consts, out_rows, out_accs, tm):
    rows = [r if isinstance(r, tuple) else (r, r.shape[1], 0) for r in rows]
    m = rows[0][0].shape[0]
    assert m % tm == 0
    nr, nc, no, na = len(rows), len(consts), len(out_rows), len(out_accs)

    def body(*refs):
        rr, cr = refs[:nr], refs[nr:nr + nc]
        orr, ar = refs[nr + nc:nr + nc + no], refs[nr + nc + no:]
        outs, accs = fn([r[...] for r in rr], cr)
        for o, v in zip(orr, outs, strict=True):
            o[...] = v.astype(o.dtype)
        if na:
            first = pl.program_id(0) == 0

            @pl.when(first)
            def _():
                for a, v in zip(ar, accs, strict=True):
                    a[...] = v

            @pl.when(jnp.logical_not(first))
            def _():
                for a, v in zip(ar, accs, strict=True):
                    a[...] += v

    def whole(shape):
        return pl.BlockSpec(shape, lambda i, n=len(shape): (0,) * n)

    in_specs = [pl.BlockSpec((tm, w), lambda i, cb=cb: (i, cb)) for _, w, cb in rows] + [whole(c.shape) for c in consts]
    out_specs = [pl.BlockSpec((tm, w), lambda i: (i, 0)) for w, _ in out_rows] + [whole(s) for s in out_accs]
    out_shape = [jax.ShapeDtypeStruct((m, w), dt) for w, dt in out_rows] + [jax.ShapeDtypeStruct(s, F32) for s in out_accs]
    res = pl.pallas_call(body, grid=(m // tm,), in_specs=in_specs, out_specs=out_specs, out_shape=out_shape, name=name,
                         compiler_params=_cparams(("arbitrary",)))(*[r[0] for r in rows], *consts)
    return res[:no], res[no:]


def _wgrad_call(name, x, dy, split, nblk, jb, blk_shape, tm, prologue=None):
    dys = list(dy) if isinstance(dy, (list, tuple)) else [dy]
    m = x.shape[0]
    kk, nn = blk_shape
    assert m % tm == 0 and nblk % jb == 0 and (len(dys) == 1 or (split == "y" and jb == nblk))
    xw = kk * jb if split == "x" else x.shape[1]
    yws = [d.shape[1] for d in dys] if (split == "x" or len(dys) > 1) else [nn * jb]
    nrow = m // tm

    def body(x_ref, *rest):
        dy_refs, o_ref, acc_ref = rest[:len(dys)], rest[-2], rest[-1]
        i = pl.program_id(1)

        @pl.when(i == 0)
        def _():
            acc_ref[...] = jnp.zeros_like(acc_ref)

        xv = x_ref[...]
        if prologue is not None:
            xv = prologue(xv)
        xv = xv.astype(MXU_DTYPE)
        dv = [r[...].astype(MXU_DTYPE) for r in dy_refs]
        dv = dv[0] if len(dv) == 1 else jnp.concatenate(dv, axis=-1)
        for j in range(jb):
            xa = xv[:, j * kk:(j + 1) * kk] if split == "x" else xv
            da = dv[:, j * nn:(j + 1) * nn] if split == "y" else dv
            acc_ref[j] += _mm_tn(xa, da)

        @pl.when(i == nrow - 1)
        def _():
            o_ref[...] = acc_ref[...].astype(o_ref.dtype)

    in_specs = [pl.BlockSpec((tm, xw), (lambda j, i: (i, j)) if split == "x" else (lambda j, i: (i, 0)))]
    in_specs += [pl.BlockSpec((tm, yw), (lambda j, i: (i, j)) if (split == "y" and len(dys) == 1) else (lambda j, i: (i, 0)))
                 for yw in yws]
    return pl.pallas_call(
        body, grid=(nblk // jb, nrow), in_specs=in_specs,
        out_specs=pl.BlockSpec((jb, kk, nn), lambda j, i: (j, 0, 0)),
        out_shape=jax.ShapeDtypeStruct((nblk, kk, nn), MXU_DTYPE),
        scratch_shapes=[pltpu.VMEM((jb, kk, nn), F32)], name=name,
        compiler_params=_cparams(("arbitrary", "arbitrary")))(x, *dys)


_HBM_SPEC = pl.BlockSpec(memory_space=pltpu.HBM)
_SEM_SPEC = pl.BlockSpec(memory_space=pltpu.SEMAPHORE)
_EFFECT = pltpu.SideEffectType.DATAFLOW_SIDE_EFFECTING


def _my_index():
    return 4 * lax.axis_index("x") + 2 * lax.axis_index("y") + lax.axis_index("c")


def _peer_copies(ins, lands, send_sems, recv_sems, modes):
    x, y, c = lax.axis_index("x"), lax.axis_index("y"), lax.axis_index("c")
    me = 4 * x + 2 * y + c
    pairs = []
    for k in range(NDEV - 1):
        fx, fy, fc = ((k + 1) >> 2) & 1, ((k + 1) >> 1) & 1, (k + 1) & 1
        px, py, pc = (x + fx) % 2, (y + fy) % 2, (c + fc) % 2
        idx = 4 * px + 2 * py + pc
        for a, md in enumerate(modes):
            src = ins[a] if md == "gather" else ins[a].at[idx]
            sem = a * (NDEV - 1) + k
            common = dict(src_ref=src, send_sem=send_sems.at[sem], recv_sem=recv_sems.at[sem], device_id=(px, py, pc),
                          device_id_type=pl.DeviceIdType.MESH)
            pairs.append((pltpu.make_async_remote_copy(dst_ref=lands[a].at[me], **common),
                          pltpu.make_async_remote_copy(dst_ref=lands[a].at[idx], **common)))
    return pairs


def _exchange_start(name, arrays, modes, after=None):
    n = len(arrays)
    extra = [] if after is None else [after]
    me = _my_index()
    lands = []
    for a, md in zip(arrays, modes, strict=True):
        piece = a if md == "gather" else lax.dynamic_index_in_dim(a, me, 0, keepdims=False)
        lands.append(lax.dynamic_update_index_in_dim(lax.empty((NDEV,) + piece.shape, a.dtype), piece, me, 0))

    def body(*refs):
        ins, lnd = refs[:n], refs[n:2 * n]
        send_sems, recv_sems = refs[2 * n + len(extra)], refs[2 * n + len(extra) + 1]
        token = refs[-1]
        for out_copy, _ in _peer_copies(ins, lnd, send_sems, recv_sems, modes):
            out_copy.start()
        token[...] = jnp.zeros_like(token)

    sems = pltpu.SemaphoreType.DMA((n * (NDEV - 1),))
    thru = [pltpu.HBM(a.shape, a.dtype) for a in list(arrays) + lands]
    res = pl.pallas_call(
        body, name=name, out_shape=(sems, sems, *thru, jax.ShapeDtypeStruct((SUBLANES, LANES), F32)),
        in_specs=[_HBM_SPEC] * (2 * n) + [pl.BlockSpec(memory_space=pl.ANY)] * len(extra),
        out_specs=(_SEM_SPEC, _SEM_SPEC, *([_HBM_SPEC] * (2 * n)), pl.BlockSpec(memory_space=pltpu.VMEM)),
        input_output_aliases={i: 2 + i for i in range(2 * n)},
        compiler_params=pltpu.CompilerParams(has_side_effects=_EFFECT),
    )(*[pltpu.with_memory_space_constraint(a, pltpu.HBM) for a in list(arrays) + lands], *extra)
    return (res[0], res[1], res[2:2 + n], res[2 + n:2 + 2 * n]), res[-1]


def _exchange_wait(name, handle, modes, after):
    send_sems, recv_sems, ins_thru, lands_thru = handle
    n = len(ins_thru)

    def body(*refs):
        ins, lnd = refs[:n], refs[n:2 * n]
        for out_copy, arrival in _peer_copies(ins, lnd, refs[2 * n], refs[2 * n + 1], modes):
            out_copy.wait_send()
            arrival.wait_recv()

    thru = [pltpu.HBM(a.shape, a.dtype) for a in list(ins_thru) + list(lands_thru)]
    res = pl.pallas_call(
        body, name=name, out_shape=tuple(thru),
        in_specs=[_HBM_SPEC] * (2 * n) + [_SEM_SPEC, _SEM_SPEC, pl.BlockSpec(memory_space=pl.ANY)],
        out_specs=tuple([_HBM_SPEC] * (2 * n)), input_output_aliases={i: i for i in range(2 * n)},
        compiler_params=pltpu.CompilerParams(has_side_effects=_EFFECT),
    )(*ins_thru, *lands_thru, send_sems, recv_sems, after)
    return res[n:]


def _rope_tables(positions3, inv_freq_row, sign_row):
    b, s, _ = positions3.shape

    def body(pos_ref, f_ref, sg_ref, c_ref, s_ref):
        ang = pos_ref[...].astype(F32) * f_ref[...]
        c_ref[...] = jnp.cos(ang)
        s_ref[...] = jnp.sin(ang) * sg_ref[...]

    row = pl.BlockSpec((1, LANES), lambda i: (0, 0))
    blk = pl.BlockSpec((None, s, LANES), lambda i: (i, 0, 0))
    return pl.pallas_call(
        body, grid=(b,), in_specs=[pl.BlockSpec((None, s, 1), lambda i: (i, 0, 0)), row, row], out_specs=[blk, blk],
        out_shape=[jax.ShapeDtypeStruct((b, s, LANES), F32)] * 2, name="rope_tables",
        compiler_params=_cparams(("arbitrary",)))(positions3, inv_freq_row, sign_row)


def _swap_halves(t):
    lane = lax.broadcasted_iota(jnp.int32, t.shape, 1)
    half = HEAD_DIM // 2
    return jnp.where((lane % HEAD_DIM) < half, pltpu.roll(t, LANES - half, 1), pltpu.roll(t, half, 1))


def _segment_rows(r, d, s):
    n = s // d
    return (pl.ds(r, n, stride=d) if d > 1 else pl.ds(0, s)), pl.ds(r * n, n)


def _head_lanes(shape):
    lane = lax.broadcasted_iota(jnp.int32, shape, len(shape) - 1)
    return lane < HEAD_DIM


def _bmm_nt(a, b):
    return lax.dot_general(a.astype(MXU_DTYPE), b.astype(MXU_DTYPE), (((2,), (2,)), ((0,), (0,))), preferred_element_type=F32)


def _bmm(a, b):
    return lax.dot_general(a.astype(MXU_DTYPE), b.astype(MXU_DTYPE), (((2,), (1,)), ((0,), (0,))), preferred_element_type=F32)


def _bmm_tn(a, b):
    return lax.dot_general(a.astype(MXU_DTYPE), b.astype(MXU_DTYPE), (((1,), (1,)), ((0,), (0,))), preferred_element_type=F32)


def _stack_heads(t3):
    head_a = _head_lanes(t3.shape)
    zero = jnp.zeros_like(t3)
    return jnp.concatenate([jnp.where(head_a, t3, zero), jnp.where(head_a, zero, t3)], axis=1)


QUNIT = QBLK // 2


def _with_previous(t3, nprev):
    shifted = [jnp.concatenate([t3[:k], t3[:-k]], axis=0) for k in range(nprev, 0, -1)]
    return jnp.concatenate(shifted + [t3], axis=1)


def _head_columns(t3):
    return jnp.concatenate([t3[:, :, lo:lo + 1] for lo in range(0, LANES, HEAD_DIM)], axis=1)


def _to_own_unit(t, nprev):
    unit = t.shape[1] // (nprev + 1)
    out = t[:, nprev * unit:]
    for k in range(1, nprev + 1):
        part = t[:, (nprev - k) * unit:(nprev - k + 1) * unit]
        out = out + jnp.concatenate([part[k:], jnp.zeros_like(part[:k])], axis=0)
    return out


def _branch_operands(qh, kh, vh, s, nb):
    nh = LANES // HEAD_DIM
    unit, nprev = (QBLK, 0) if nb == 1 else (QUNIT, QBLK // QUNIT)
    g = s // unit
    q3 = _stack_heads(qh[...].reshape(g, unit, LANES))
    k3, v3 = kh[...].reshape(g, unit, LANES), vh[...].reshape(g, unit, LANES)
    if nprev == 0:
        qi = lax.broadcasted_iota(jnp.int32, (1, nh * unit, unit), 1) % unit
        kj = lax.broadcasted_iota(jnp.int32, (1, nh * unit, unit), 2)
        return q3, k3, v3, kj <= qi, nprev
    shape = (g, nh * unit, (nprev + 1) * unit)
    qi = lax.broadcasted_iota(jnp.int32, shape, 1) % unit
    kj = lax.broadcasted_iota(jnp.int32, shape, 2)
    j = lax.broadcasted_iota(jnp.int32, shape, 0)
    per_block = QBLK // unit
    opens = ((j // per_block) % nb) == 0
    mask = (kj >= qi) & (kj <= qi + QBLK) & ((kj >= QBLK - unit * (j % per_block)) | jnp.logical_not(opens))
    return q3, _with_previous(k3, nprev), _with_previous(v3, nprev), mask, nprev


def _attn_fwd(proj3, ctab, stab, aw):
    b, s, _ = proj3.shape
    npair = aw // LANES
    scale = HEAD_DIM ** -0.5
    nbr = len(DILATION_PAIRS)

    def body(q_ref, k_ref, v_ref, c_ref, s_ref, o_ref, l_ref, qf, kf, qh, kh, vh, op, lp, ob, lb):
        cc, ss = c_ref[...], s_ref[...]
        q2, k2 = q_ref[...], k_ref[...]
        qf[...] = (q2 * cc + _swap_halves(q2) * ss) * scale
        kf[...] = k2 * cc + _swap_halves(k2) * ss
        for br, (window, d) in enumerate(DILATION_PAIRS):
            for r in range(d):
                nat, perm = _segment_rows(r, d, s)
                for dst, src in ((qh, qf), (kh, kf), (vh, v_ref)):
                    dst[perm, :] = src[nat, :].astype(MXU_DTYPE)
            o_dst, l_dst = (ob.at[br], lb.at[br]) if d == 1 else (op, lp)
            q3, kk, vv, mask, _ = _branch_operands(qh, kh, vh, s, (s // d) // QBLK)
            unit = q3.shape[1] // (LANES // HEAD_DIM)
            head_a = _head_lanes((q3.shape[0], unit, LANES))
            sc = jnp.where(mask, _bmm_nt(q3, kk), NEG_INF)
            mx = jnp.max(sc, axis=-1, keepdims=True)
            p = jnp.exp(sc - mx)
            den = jnp.sum(p, axis=-1, keepdims=True)
            o2 = _bmm(p, vv) / den
            l2 = mx + jnp.log(den)
            o_dst[...] = jnp.where(head_a, o2[:, :unit], o2[:, unit:]).reshape(s, LANES)
            l_dst[...] = jnp.where(head_a, l2[:, :unit], l2[:, unit:]).reshape(s, LANES)
            if d > 1:
                for r in range(d):
                    nat, perm = _segment_rows(r, d, s)
                    ob[br, nat, :] = op[perm, :]
                    lb[br, nat, :] = lp[perm, :]
        ls = [lb[br] for br in range(nbr)]
        mx = functools.reduce(jnp.maximum, ls)
        ws = [jnp.exp(l - mx) for l in ls]
        tot = functools.reduce(lambda a, b_: a + b_, ws)
        o_ref[...] = functools.reduce(lambda a, b_: a + b_, [(w / tot) * ob[br] for br, w in enumerate(ws)])
        l_ref[...] = mx + jnp.log(tot)

    def col(off):
        return pl.BlockSpec((None, s, LANES), lambda bi, hp, off=off: (bi, 0, off + hp))

    tab = pl.BlockSpec((None, s, LANES), lambda bi, hp: (bi, 0, 0))
    f32s = pltpu.VMEM((s, LANES), F32)
    mxs = pltpu.VMEM((s, LANES), MXU_DTYPE)
    br_s = pltpu.VMEM((nbr, s, LANES), F32)
    return pl.pallas_call(
        body, grid=(b, npair), in_specs=[col(0), col(npair), col(2 * npair), tab, tab], out_specs=[col(0), col(0)],
        out_shape=[jax.ShapeDtypeStruct((b, s, aw), F32)] * 2,
        scratch_shapes=[f32s] * 2 + [mxs] * 3 + [f32s] * 2 + [br_s] * 2,
        name="attn_fwd", compiler_params=_cparams(("arbitrary", "arbitrary")))(proj3, proj3, proj3, ctab, stab)


def _attn_bwd(proj3, ctab, stab, dout3, out3, lse3, aw):
    b, s, _ = proj3.shape
    npair = aw // LANES
    scale = HEAD_DIM ** -0.5
    nheads = LANES // HEAD_DIM

    def body(q_ref, k_ref, v_ref, c_ref, s_ref, do_ref, o_ref, l_ref, dq_ref, dk_ref, dv_ref,
             qf, kf, dlf, qh, kh, vh, doh, lpm, dpm, dqp, dkp, dvp, dqn, dkn, dvn):
        cc, ss = c_ref[...], s_ref[...]
        q2, k2 = q_ref[...], k_ref[...]
        qf[...] = (q2 * cc + _swap_halves(q2) * ss) * scale
        kf[...] = k2 * cc + _swap_halves(k2) * ss
        dd = do_ref[...] * o_ref[...]
        in_a = _head_lanes((s, LANES))
        sum_a = jnp.sum(jnp.where(in_a, dd, 0.0), axis=-1, keepdims=True)
        sum_b = jnp.sum(jnp.where(in_a, 0.0, dd), axis=-1, keepdims=True)
        dlf[...] = jnp.where(in_a, sum_a, sum_b)
        for r_ in (dqn, dkn, dvn):
            r_[...] = jnp.zeros_like(r_)
        for window, d in DILATION_PAIRS:
            for r in range(d):
                nat, perm = _segment_rows(r, d, s)
                for dst, src in ((qh, qf), (kh, kf), (vh, v_ref), (doh, do_ref)):
                    dst[perm, :] = src[nat, :].astype(MXU_DTYPE)
                if d > 1:
                    lpm[perm, :] = l_ref[nat, :]
                    dpm[perm, :] = dlf[nat, :]
            l_src, d_src = (l_ref, dlf) if d == 1 else (lpm, dpm)
            q3, kk, vv, mask, nprev = _branch_operands(qh, kh, vh, s, (s // d) // QBLK)
            g_, unit = q3.shape[0], q3.shape[1] // nheads
            head_a = _head_lanes((g_, unit, LANES))
            do3 = _stack_heads(doh[...].reshape(g_, unit, LANES))
            lcol, dcol = _head_columns(l_src[...].reshape(g_, unit, LANES)), _head_columns(d_src[...].reshape(g_, unit, LANES))
            p = jnp.exp(jnp.where(mask, _bmm_nt(q3, kk), NEG_INF) - lcol)
            ds_ = p * (_bmm_nt(do3, vv) - dcol)
            dq2 = _bmm(ds_, kk)
            dq_new = jnp.where(head_a, dq2[:, :unit], dq2[:, unit:]).reshape(s, LANES)
            dk_new = _to_own_unit(_bmm_tn(ds_, q3), nprev).reshape(s, LANES)
            dv_new = _to_own_unit(_bmm_tn(p, do3), nprev).reshape(s, LANES)
            if d == 1:
                dqn[...] += dq_new
                dkn[...] += dk_new
                dvn[...] += dv_new
            else:
                dqp[...] = dq_new
                dkp[...] = dk_new
                dvp[...] = dv_new
                for r in range(d):
                    nat, perm = _segment_rows(r, d, s)
                    dqn[nat, :] += dqp[perm, :]
                    dkn[nat, :] += dkp[perm, :]
                    dvn[nat, :] += dvp[perm, :]
        g = dqn[...] * scale
        dq_ref[...] = (g * cc + _swap_halves(g * ss)).astype(dq_ref.dtype)
        g = dkn[...]
        dk_ref[...] = (g * cc + _swap_halves(g * ss)).astype(dk_ref.dtype)
        dv_ref[...] = dvn[...].astype(dv_ref.dtype)

    def col(off):
        return pl.BlockSpec((None, s, LANES), lambda bi, hp, off=off: (bi, 0, off + hp))

    tab = pl.BlockSpec((None, s, LANES), lambda bi, hp: (bi, 0, 0))
    f32s = pltpu.VMEM((s, LANES), F32)
    mxs = pltpu.VMEM((s, LANES), MXU_DTYPE)
    return pl.pallas_call(
        body, grid=(b, npair), in_specs=[col(0), col(npair), col(2 * npair), tab, tab, col(0), col(0), col(0)],
        out_specs=[col(0)] * 3, out_shape=[jax.ShapeDtypeStruct((b, s, aw), MXU_DTYPE)] * 3,
        scratch_shapes=[f32s] * 3 + [mxs] * 4 + [f32s] * 8,
        name="attn_bwd", compiler_params=_cparams(("arbitrary", "arbitrary")))(
            proj3, proj3, proj3, ctab, stab, dout3, out3, lse3)


def _s5_discretise(a_re, a_im, log_dt, bt_re, bt_im):
    dt = jnp.exp(log_dt)
    mag = jnp.exp(a_re * dt)
    ang = a_im * dt
    lb_re = mag * jnp.cos(ang)
    lb_im = mag * jnp.sin(ang)
    den = a_re * a_re + a_im * a_im
    nr = lb_re - 1.0
    ni = lb_im
    cr = (nr * a_re + ni * a_im) / den
    ci = (ni * a_re - nr * a_im) / den
    return lb_re, lb_im, cr * bt_re - ci * bt_im, cr * bt_im + ci * bt_re


def _s5_params(a_re, a_im, log_dt, bt_re, bt_im):
    def body(ar, ai, ld, br, bi, o1, o2, o3, o4):
        r = _s5_discretise(ar[...], ai[...], ld[...], br[...], bi[...])
        for o, v in zip((o1, o2, o3, o4), r, strict=True):
            o[...] = v

    sd = jax.ShapeDtypeStruct
    return pl.pallas_call(body, out_shape=[sd(a_re.shape, F32)] * 2 + [sd(bt_re.shape, F32)] * 2, name="s5_params")(
        a_re, a_im, log_dt, bt_re, bt_im)


def _s5_params_bwd(a_re, a_im, log_dt, bt_re, bt_im, cts):
    def body(ar, ai, ld, br, bi, c1, c2, c3, c4, o1, o2, o3, o4, o5):
        _, vjp = jax.vjp(_s5_discretise, ar[...], ai[...], ld[...], br[...], bi[...])
        r = vjp((c1[...], c2[...], c3[...], c4[...]))
        for o, v in zip((o1, o2, o3, o4, o5), r, strict=True):
            o[...] = v

    sd = jax.ShapeDtypeStruct
    return pl.pallas_call(
        body, out_shape=[sd(a_re.shape, F32)] * 2 + [sd(log_dt.shape, F32)] + [sd(bt_re.shape, F32)] * 2, name="s5_params_bwd")(
            a_re, a_im, log_dt, bt_re, bt_im, *cts)


S5_TC = 128


def _time_major(tiles):
    nt, tc, lanes = tiles.shape
    return jnp.swapaxes(tiles, 0, 1).reshape(tc * nt, lanes)


def _tile_major(rows, nt):
    return jnp.swapaxes(rows.astype(MXU_DTYPE).reshape(rows.shape[0] // nt, nt, rows.shape[1]), 0, 1)


def _s5_fwd(proj3, ucol, lam_re, lam_im, bre, bim, cre, cim, dvec, sw):
    b, s, _ = proj3.shape
    nt = lam_re.shape[0]
    ncb = sw // LANES
    tpc = nt // ncb
    tc = S5_TC

    def body(u_ref, lr_ref, li_ref, bre_ref, bim_ref, cre_ref, cim_ref, d_ref, y_ref, xr_ref, xi_ref, sr, si):
        @pl.when(pl.program_id(0) == 0)
        def _():
            sr[...] = jnp.zeros_like(sr)
            si[...] = jnp.zeros_like(si)

        u_all = u_ref[...].reshape(b * tc, sw)
        bur, bui = [], []
        for cb in range(ncb):
            ucb = u_all[:, cb * LANES:(cb + 1) * LANES].astype(MXU_DTYPE)
            for t in range(cb * tpc, (cb + 1) * tpc):
                bur.append(_mm(ucb, bre_ref[t]))
                bui.append(_mm(ucb, bim_ref[t]))
        for bi in range(b):
            rows = slice(bi * tc, (bi + 1) * tc)
            xr_ref[bi] = _time_major(jnp.stack([v_[rows] for v_ in bur]))
            xi_ref[bi] = _time_major(jnp.stack([v_[rows] for v_ in bui]))
        lr, li = lr_ref[...], li_ref[...]

        def step(j, carry):
            off = pl.multiple_of(j * nt, nt)
            new = []
            for bi in range(b):
                pr, pi = carry[2 * bi], carry[2 * bi + 1]
                nr = lr * pr - li * pi + xr_ref[bi, pl.ds(off, nt), :]
                ni = lr * pi + li * pr + xi_ref[bi, pl.ds(off, nt), :]
                xr_ref[bi, pl.ds(off, nt), :] = nr
                xi_ref[bi, pl.ds(off, nt), :] = ni
                new += [nr, ni]
            return tuple(new)

        init = tuple(v for bi in range(b) for v in (sr[bi], si[bi]))
        fin = lax.fori_loop(0, tc, step, init, unroll=8)
        for bi in range(b):
            sr[bi] = fin[2 * bi]
            si[bi] = fin[2 * bi + 1]
        xr_t = [_tile_major(xr_ref[bi], nt) for bi in range(b)]
        xi_t = [_tile_major(xi_ref[bi], nt) for bi in range(b)]
        for cb in range(ncb):
            cols = slice(cb * LANES, (cb + 1) * LANES)
            acc = d_ref[:, cols] * u_all[:, cols]
            for t in range(cb * tpc, (cb + 1) * tpc):
                xr_all = jnp.concatenate([xr_t[bi][t] for bi in range(b)], axis=0)
                xi_all = jnp.concatenate([xi_t[bi][t] for bi in range(b)], axis=0)
                acc = acc + (_mm(xr_all, cre_ref[t]) - _mm(xi_all, cim_ref[t]))
            for bi in range(b):
                y_ref[bi, :, cols] = acc[bi * tc:(bi + 1) * tc]

    def whole(a):
        return pl.BlockSpec(a.shape, lambda c, n=a.ndim: (0,) * n)

    xblk = pl.BlockSpec((b, tc * nt, LANES), lambda c: (0, c, 0))
    return pl.pallas_call(
        body, grid=(s // tc,),
        in_specs=[pl.BlockSpec((b, tc, sw), lambda c: (0, c, ucol))] + [whole(a) for a in (lam_re, lam_im, bre, bim, cre, cim, dvec)],
        out_specs=[pl.BlockSpec((b, tc, sw), lambda c: (0, c, 0)), xblk, xblk],
        out_shape=[jax.ShapeDtypeStruct((b, s, sw), F32)] + [jax.ShapeDtypeStruct((b, s * nt, LANES), F32)] * 2,
        scratch_shapes=[pltpu.VMEM((b, nt, LANES), F32)] * 2, name="s5_fwd",
        compiler_params=_cparams(("arbitrary",)))(proj3, lam_re, lam_im, bre, bim, cre, cim, dvec)


def _s5_bwd(proj3, ucol, dy3, xr3, xi3, lam_re, lam_im, bre, bim, cre, cim, dvec, sw):
    b, s, _ = proj3.shape
    nt = lam_re.shape[0]
    ncb = sw // LANES
    tpc = nt // ncb
    tc = S5_TC
    nchunk = s // tc

    def body(u_ref, dy_ref, xr_ref, xi_ref, pr_ref, pi_ref, lr_ref, li_ref, bre_ref, bim_ref, cre_ref, cim_ref, d_ref,
             du_ref, dbre, dbim, dcre, dcim, dlr, dli, dd, gr, gi, sr, si):
        step_id = pl.program_id(0)

        @pl.when(step_id == 0)
        def _():
            for r in (sr, si, dbre, dbim, dcre, dcim, dlr, dli, dd):
                r[...] = jnp.zeros_like(r)

        u_all = u_ref[...].reshape(b * tc, sw)
        dy_all = dy_ref[...].reshape(b * tc, sw)
        dxr, dxi = [], []
        for cb in range(ncb):
            dycb = dy_all[:, cb * LANES:(cb + 1) * LANES].astype(MXU_DTYPE)
            for t in range(cb * tpc, (cb + 1) * tpc):
                dxr.append(_mm_nt(dycb, cre_ref[t]))
                dxi.append(-_mm_nt(dycb, cim_ref[t]))
        for bi in range(b):
            rows = slice(bi * tc, (bi + 1) * tc)
            gr[bi] = _time_major(jnp.stack([v_[rows] for v_ in dxr]))
            gi[bi] = _time_major(jnp.stack([v_[rows] for v_ in dxi]))
        lr, li = lr_ref[...], li_ref[...]

        has_prev = (step_id != nchunk - 1).astype(F32)
        before = [(pr_ref[bi] * has_prev, pi_ref[bi] * has_prev) for bi in range(b)]

        def step(jj, carry):
            t = tc - 1 - jj
            off = pl.multiple_of(t * nt, nt)
            poff = pl.multiple_of(jnp.maximum(t - 1, 0) * nt, nt)
            new = []
            alr, ali = carry[-2], carry[-1]
            for bi in range(b):
                nr_, ni_ = carry[2 * bi], carry[2 * bi + 1]
                vr = gr[bi, pl.ds(off, nt), :] + lr * nr_ + li * ni_
                vi = gi[bi, pl.ds(off, nt), :] + lr * ni_ - li * nr_
                gr[bi, pl.ds(off, nt), :] = vr
                gi[bi, pl.ds(off, nt), :] = vi
                xpr = jnp.where(t > 0, xr_ref[bi, pl.ds(poff, nt), :], before[bi][0])
                xpi = jnp.where(t > 0, xi_ref[bi, pl.ds(poff, nt), :], before[bi][1])
                alr = alr + (vr * xpr + vi * xpi)
                ali = ali + (vi * xpr - vr * xpi)
                new += [vr, vi]
            return tuple(new) + (alr, ali)

        zero = jnp.zeros((nt, LANES), F32)
        init = tuple(v for bi in range(b) for v in (sr[bi], si[bi])) + (zero, zero)
        fin = lax.fori_loop(0, tc, step, init, unroll=8)
        for bi in range(b):
            sr[bi] = fin[2 * bi]
            si[bi] = fin[2 * bi + 1]
        dlr[...] += fin[-2]
        dli[...] += fin[-1]

        tiles = [[_tile_major(ref[bi], nt) for bi in range(b)] for ref in (gr, gi, xr_ref, xi_ref)]

        def stacked(k, t):
            return jnp.concatenate([tiles[k][bi][t] for bi in range(b)], axis=0)

        for cb in range(ncb):
            cols = slice(cb * LANES, (cb + 1) * LANES)
            ucb32, dycb32 = u_all[:, cols], dy_all[:, cols]
            ucb, dycb = ucb32.astype(MXU_DTYPE), dycb32.astype(MXU_DTYPE)
            acc = d_ref[:, cols] * dycb32
            for t in range(cb * tpc, (cb + 1) * tpc):
                vr, vi = stacked(0, t), stacked(1, t)
                acc = acc + (_mm_nt(vr, bre_ref[t]) + _mm_nt(vi, bim_ref[t]))
                dbre[t] += _mm_tn(ucb, vr)
                dbim[t] += _mm_tn(ucb, vi)
                dcre[t] += _mm_tn(stacked(2, t), dycb)
                dcim[t] -= _mm_tn(stacked(3, t), dycb)
            for bi in range(b):
                du_ref[bi, :, cols] = acc[bi * tc:(bi + 1) * tc].astype(du_ref.dtype)
            dd[:, cols] += _colsum(dycb32 * ucb32)

    def whole(a):
        return pl.BlockSpec(a.shape, lambda c, n=len(a.shape): (0,) * n)

    def rev(c):
        return nchunk - 1 - c

    xblk = pl.BlockSpec((b, tc * nt, LANES), lambda c: (0, rev(c), 0))
    prev = pl.BlockSpec((b, nt, LANES), lambda c: (0, jnp.maximum(rev(c) * tc - 1, 0), 0))
    sd = jax.ShapeDtypeStruct
    blk = sd(bre.shape, F32)
    acc_shapes = [blk, blk, sd(cre.shape, F32), sd(cre.shape, F32), sd(lam_re.shape, F32), sd(lam_re.shape, F32), sd((1, sw), F32)]
    return pl.pallas_call(
        body, grid=(nchunk,),
        in_specs=[pl.BlockSpec((b, tc, sw), lambda c: (0, rev(c), ucol)), pl.BlockSpec((b, tc, sw), lambda c: (0, rev(c), 0)),
                  xblk, xblk, prev, prev] + [whole(a) for a in (lam_re, lam_im, bre, bim, cre, cim, dvec)],
        out_specs=[pl.BlockSpec((b, tc, sw), lambda c: (0, rev(c), 0))] + [whole(a) for a in acc_shapes],
        out_shape=[sd((b, s, sw), MXU_DTYPE)] + acc_shapes,
        scratch_shapes=[pltpu.VMEM((b, tc * nt, LANES), F32)] * 2 + [pltpu.VMEM((b, nt, LANES), F32)] * 2, name="s5_bwd",
        compiler_params=_cparams(("arbitrary",)))(proj3, dy3, xr3, xi3, xr3, xi3, lam_re, lam_im, bre, bim, cre, cim, dvec)


def _s5_blocks(bb, cc_, ncb):
    g, n, p = bb.shape
    mask = jnp.asarray(_s5_tile_mask(g, p, ncb))
    nt, gpt, gpc = mask.shape
    bbt, cct = bb.reshape(nt, gpt, n, p), cc_.reshape(nt, gpt, n, p)
    spread = mask[:, :, :, None, None]
    bblk = (bbt[:, :, None] * spread).transpose(0, 2, 3, 1, 4).reshape(nt, gpc * n, gpt * p)
    cblk = (cct[:, :, None] * spread).transpose(0, 1, 4, 2, 3).reshape(nt, gpt * p, gpc * n)
    return bblk, cblk


def _s5_tile_mask(g, p, ncb):
    nt, gpt, gpc = g * p // LANES, LANES // p, g // ncb
    mask = np.zeros((nt, gpt, gpc), np.float32)
    for t in range(nt):
        for gl in range(gpt):
            mask[t, gl, (t * gpt + gl) % gpc] = 1.0
    return mask


def _s5_unblock(dbblk, dcblk, g, n, p, ncb):
    mask = jnp.asarray(_s5_tile_mask(g, p, ncb))
    nt, gpt, gpc = mask.shape
    db = jnp.sum(dbblk.reshape(nt, gpc, n, gpt, p) * mask.transpose(0, 2, 1)[:, :, None, :, None], axis=1)
    dc = jnp.sum(dcblk.reshape(nt, gpt, p, gpc, n) * mask[:, :, None, :, None], axis=3)
    return db.transpose(0, 2, 1, 3).reshape(g, n, p), dc.transpose(0, 1, 3, 2).reshape(g, n, p)


def _adamw_math(w, g, m, v):
    m = ADAM_B1 * m + (1.0 - ADAM_B1) * g
    v = ADAM_B2 * v + (1.0 - ADAM_B2) * (g * g)
    m_hat = m / (1.0 - ADAM_B1 ** ADAM_STEP)
    v_hat = v / (1.0 - ADAM_B2 ** ADAM_STEP)
    delta = -ADAM_LR * (m_hat / (jnp.sqrt(v_hat) + ADAM_EPS) + ADAM_WD * w)
    return delta, m, v


def _adamw_layer(name, parts, w, m, v, layer, prev, tr):
    nparts, r, c = parts.shape
    assert r % tr == 0
    if prev is None:
        prev = [lax.empty(w.shape, F32) for _ in range(4)]

    def body(p_ref, w_ref, m_ref, v_ref, *rest):
        g_out, d_out, m_out, v_out = rest[4:]
        g = p_ref[0].astype(F32)
        for k in range(1, nparts):
            g = g + p_ref[k].astype(F32)
        delta, mn, vn = _adamw_math(w_ref[...], g, m_ref[...], v_ref[...])
        g_out[...] = g
        d_out[...] = delta
        m_out[...] = mn
        v_out[...] = vn

    blk = pl.BlockSpec((None, tr, c), lambda i: (layer, i, 0))
    kept = pl.BlockSpec(memory_space=pl.ANY)
    return pl.pallas_call(
        body, grid=(r // tr,), in_specs=[pl.BlockSpec((nparts, tr, c), lambda i: (0, i, 0)), blk, blk, blk] + [kept] * 4,
        out_specs=[blk] * 4, out_shape=[jax.ShapeDtypeStruct(w.shape, F32)] * 4, name=name,
        input_output_aliases={4 + k: k for k in range(4)},
        compiler_params=_cparams(("arbitrary",)))(parts, w, m, v, *prev)


def _piece_rows(size):
    unit = SUBLANES * LANES
    return -(-size // unit) * SUBLANES


def _pack_small(pieces, lead=0):
    rows = []
    for p in pieces:
        head = p.shape[:lead]
        flat = p.reshape(head + (-1,))
        nrow = _piece_rows(flat.shape[-1])
        flat = jnp.pad(flat, [(0, 0)] * lead + [(0, nrow * LANES - flat.shape[-1])])
        rows.append(flat.reshape(head + (nrow, LANES)))
    return jnp.concatenate(rows, axis=lead)


def _step(x, positions, weights, moments_m, moments_v, loss_target, distributed):
    f32 = F32
    bsz, seq, dm = x.shape
    depth = weights["w_in"].shape[0]
    aw = weights["attn_gain"].shape[1]
    sw = weights["ssm_gain"].shape[1]
    dff = weights["b_ff1"].shape[1]
    ng, npst = weights["ssm_a_re"].shape[1:]
    gdim = weights["ssm_d"].shape[2]
    in_w = 3 * aw + sw
    t_rows = bsz * seq
    alpha = (2.0 * depth) ** 0.25
    ncb = sw // LANES
    nt = ng * npst // LANES
    ndev = NDEV if distributed else 1
    tm, tl = 512, 1024

    gather_modes = ["gather"] * len(BIG_NAMES)
    no_token = jnp.zeros((SUBLANES, LANES), f32)

    def shards(l, names):
        return [weights[n][l].astype(MXU_DTYPE) for n in names]

    if distributed:
        first, _ = _exchange_start("weights_start_l0_in", shards(0, BIG_NAMES[:1]), gather_modes[:1])

    half = HEAD_DIM // 2
    inv_freq = ROPE_THETA ** (-jnp.arange(half, dtype=f32) * 2.0 / HEAD_DIM)
    reps = LANES // half
    inv_row = jnp.tile(inv_freq, reps)[None, :]
    sign_row = jnp.tile(jnp.concatenate([-jnp.ones((half,), f32), jnp.ones((half,), f32)]), LANES // HEAD_DIM)[None, :]
    ctab, stab = _rope_tables(positions[..., None], inv_row, sign_row)

    def row(v):
        return v.reshape(1, -1)

    s5_prep = []
    for l in range(depth):
        a_re, a_im = weights["ssm_a_re"][l][:, None, :], weights["ssm_a_im"][l][:, None, :]
        log_dt = weights["ssm_log_dt"][l][:, None, None]
        bt_re = weights["ssm_b_re"][l].transpose(0, 2, 1)
        bt_im = weights["ssm_b_im"][l].transpose(0, 2, 1)
        lb_re, lb_im, bb_re, bb_im = _s5_params(a_re, a_im, log_dt, bt_re, bt_im)
        bre, cre = _s5_blocks(bb_re, weights["ssm_c_re"][l], ncb)
        bim, cim = _s5_blocks(bb_im, weights["ssm_c_im"][l], ncb)
        s5_prep.append(dict(lam=(lb_re.reshape(nt, LANES), lb_im.reshape(nt, LANES)),
                            s5c=[a_.astype(MXU_DTYPE) for a_ in (bre, bim, cre, cim)], dvec=row(weights["ssm_d"][l]),
                            s5in=(a_re, a_im, log_dt, bt_re, bt_im)))

    if distributed:
        (g_in0,) = _exchange_wait("weights_wait_l0_in", first, gather_modes[:1], s5_prep[-1]["s5c"][-1])
        rest0, tok_rest0 = _exchange_start("weights_start_l0_rest", shards(0, BIG_NAMES[1:]), gather_modes[1:], after=g_in0)

    h = x.reshape(t_rows, dm)
    saved = []
    for l in range(depth):
        tok_in = tok_mix = no_token
        if not distributed:
            g_in, g_glu, g_out, g_ff1, g_ff2 = [weights[n][l].astype(MXU_DTYPE)[None] for n in BIG_NAMES]
        elif l == 0:
            g_in, tok_in = g_in0, tok_rest0
        else:
            g_in, g_glu, g_out, g_ff1, g_ff2 = _exchange_wait(f"weights_wait_l{l}", next_gather, gather_modes, h)
            if l + 1 < depth:
                next_gather, tok_in = _exchange_start(f"weights_start_l{l + 1}", shards(l + 1, BIG_NAMES), gather_modes, after=g_in)
        w_in_l = g_in

        def in_proj(rv, cr):
            return [jnp.concatenate([_mm(rv[0], cr[0][j]) for j in range(ndev)], axis=-1)], []

        (proj,), _ = _rows_call("in_proj", in_proj, [h], [w_in_l, tok_in], [(in_w, f32)], [], tl)
        proj3 = proj.reshape(bsz, seq, in_w)
        attn3, lse3 = _attn_fwd(proj3, ctab, stab, aw)
        if distributed and l == 0:
            g_glu, g_out, g_ff1, g_ff2 = _exchange_wait("weights_wait_l0_rest", rest0, gather_modes[1:], attn3)
            if depth > 1:
                next_gather, tok_mix = _exchange_start("weights_start_l1", shards(1, BIG_NAMES), gather_modes, after=g_glu)
        w_glu_l = g_glu.reshape(sw, sw)
        w_out_l = g_out.reshape(dm, dm)
        w_ff1_l = g_ff1
        w_ff2_l = g_ff2.reshape(dff, dm)

        (lam_re, lam_im), s5c, dvec = s5_prep[l]["lam"], s5_prep[l]["s5c"], s5_prep[l]["dvec"]
        ypre3, xr3, xi3 = _s5_fwd(proj3, 3 * aw // sw, lam_re, lam_im, *s5c, dvec, sw)
        attn, ypre = attn3.reshape(t_rows, aw), ypre3.reshape(t_rows, sw)

        b_glu, ga, gs = row(weights["b_glu"][l]), row(weights["attn_gain"][l]), row(weights["ssm_gain"][l])

        def mix(rv, cr):
            at, yp = rv
            g = _gelu(yp)
            ssm = g * jax.nn.sigmoid(_mm(g, cr[0][...]) + cr[1][...])
            return [jnp.concatenate([_rms_norm(at, cr[2][...]), _rms_norm(ssm, cr[3][...])], axis=-1)], []

        (mixed,), _ = _rows_call("mix", mix, [attn, ypre], [w_glu_l, b_glu, ga, gs, tok_mix], [(dm, MXU_DTYPE)], [], tl)

        b_out, g1, b1 = row(weights["b_out"][l]), row(weights["ln1_g"][l]), row(weights["ln1_b"][l])

        def out_proj(rv, cr):
            pre = alpha * rv[1] + _mm(rv[0], cr[0][...]) + cr[1][...]
            return [pre, _layer_norm(pre, cr[2][...], cr[3][...])], []

        (pre1, h1), _ = _rows_call("out_proj", out_proj, [mixed, h], [w_out_l, b_out, g1, b1], [(dm, f32), (dm, f32)], [], tl)

        b_ff1 = row(weights["b_ff1"][l])

        def ff1(rv, cr):
            pre_act = jnp.concatenate([_mm(rv[0], cr[0][j]) for j in range(ndev)], axis=-1) + cr[1][...]
            return [jnp.square(jnp.maximum(pre_act, 0.0))], []

        (act,), _ = _rows_call("ff1", ff1, [h1], [w_ff1_l, b_ff1], [(dff, MXU_DTYPE)], [], tm)

        b_ff2, g2, b2 = row(weights["b_ff2"][l]), row(weights["ln2_g"][l]), row(weights["ln2_b"][l])

        def ff2(rv, cr):
            pre = alpha * rv[1] + _mm(rv[0], cr[0][...]) + cr[1][...]
            return [pre, _layer_norm(pre, cr[2][...], cr[3][...])], []

        (pre2, h2), _ = _rows_call("ff2", ff2, [act, h1], [w_ff2_l, b_ff2, g2, b2], [(dm, f32), (dm, f32)], [], tm)

        saved.append(dict(h=h, proj3=proj3, attn=attn, lse3=lse3, attn3=attn3, ypre=ypre, xr3=xr3, xi3=xi3, mixed=mixed, pre1=pre1,
                          h1=h1, act=act, pre2=pre2, w_in=w_in_l, w_glu=w_glu_l, w_out=w_out_l, w_ff1=w_ff1_l, w_ff2=w_ff2_l,
                          lam=(lam_re, lam_im), s5c=s5c, dvec=dvec, b_glu=b_glu, ga=ga, gs=gs, g1=g1, g2=g2,
                          s5in=s5_prep[l]["s5in"]))
        h = h2

    g2_last = saved[-1]["g2"]

    def loss_fn(rv, cr):
        y, tgt, pre = rv
        err = y - tgt
        part = 0.5 * jnp.sum(jnp.mean(err * err, axis=-1, keepdims=True), axis=0, keepdims=True)
        dpre, dg, db = _layer_norm_bwd(err * (1.0 / dm), pre, cr[0][...])
        return [dpre], [jnp.broadcast_to(part, (1, LANES)), dg, db, _colsum(dpre)]

    (dpre2,), (loss_acc, dg2, db2, dbff2) = _rows_call(
        "loss", loss_fn, [h, loss_target.reshape(t_rows, dm), saved[-1]["pre2"]], [g2_last], [(dm, f32)],
        [(1, LANES), (1, dm), (1, dm), (1, dm)], tl)
    loss = loss_acc[0, 0]
    if distributed:
        loss = lax.psum(loss, MESH_AXES)

    big_parts = {n: [None] * depth for n in BIG_NAMES}
    small_parts = [None] * depth
    grad_handles, ff_handles = [None] * depth, [None] * depth
    grad_modes = ["scatter"] * len(MIXER_NAMES) + ["gather"]
    grad_x = None
    for l in reversed(range(depth)):
        sv = saved[l]

        def ff2_bwd(rv, cr):
            r = rv[1].astype(F32)
            relu2 = jnp.where(r > 0.0, (2.0 * r) * lax.rsqrt(r), 0.0)
            da = _mm_nt(rv[0], cr[0][...]) * relu2
            return [da], [_colsum(da)]

        (dact,), (dbff1,) = _rows_call("ff2_bwd", ff2_bwd, [dpre2, sv["act"]], [sv["w_ff2"]], [(dff, MXU_DTYPE)], [(1, dff)], tm)
        big_parts["w_ff2"][l] = _wgrad_call("wgrad_ff2", sv["act"], dpre2, "x", ndev, ndev,
                                            (dff // ndev, dm), tm)
        big_parts["w_ff1"][l] = _wgrad_call("wgrad_ff1", sv["h1"], dact, "y", ndev, ndev,
                                            (dm, dff // ndev), tm)
        ff_grads = [big_parts[n][l] for n in FF_NAMES]
        if distributed:
            ff_handles[l], ff_token = _exchange_start(f"grads_ff_start_l{l}", ff_grads, ["scatter"] * len(FF_NAMES))
        else:
            ff_handles[l], ff_token = ff_grads, no_token

        def ff1_bwd(rv, cr):
            dacc = alpha * rv[1]
            wpb = dff // ndev
            for j in range(ndev):
                dacc = dacc + _mm_nt(rv[0][:, j * wpb:(j + 1) * wpb], cr[0][j])
            dpre, dg, db = _layer_norm_bwd(dacc, rv[2], cr[1][...])
            return [dpre], [dg, db, _colsum(dpre)]

        (dpre1,), (dg1, db1, dbout) = _rows_call("ff1_bwd", ff1_bwd, [dact, dpre2, sv["pre1"]],
                                                 [sv["w_ff1"], sv["g1"], ff_token], [(dm, f32)], [(1, dm)] * 3, tm)
        big_parts["w_out"][l] = _wgrad_call("wgrad_out", sv["mixed"], dpre1, "x", ndev, ndev, (dm // ndev, dm), tl)

        def mix_bwd(rv, cr):
            dp, at, yp = rv
            w_out_r, w_glu_r, bg, ga_, gs_ = cr
            dmixed = _mm_nt(dp, w_out_r[...])
            g = _gelu(yp)
            sig = jax.nn.sigmoid(_mm(g, w_glu_r[...]) + bg[...])
            ssm = g * sig
            dat, dga = _rms_norm_bwd(dmixed[:, :aw], at, ga_[...])
            dssm, dgs = _rms_norm_bwd(dmixed[:, aw:], ssm, gs_[...])
            dz = dssm * g * sig * (1.0 - sig)
            dg = dssm * sig + _mm_nt(dz, w_glu_r[...])
            return [dat, dg * _gelu_grad(yp), dz], [dga, dgs, _colsum(dz)]

        (dattn, dypre, dz), (dga, dgs, dbglu) = _rows_call(
            "mix_bwd", mix_bwd, [dpre1, sv["attn"], sv["ypre"]], [sv["w_out"], sv["w_glu"], sv["b_glu"], sv["ga"], sv["gs"]],
            [(aw, f32), (sw, f32), (sw, MXU_DTYPE)], [(1, aw), (1, sw), (1, sw)], tl)
        big_parts["w_glu"][l] = _wgrad_call("wgrad_glu", sv["ypre"], dz, "x", 1, 1, (sw, sw), tl, prologue=_gelu).reshape(
            ndev, sw // ndev, sw)

        du3, dbre, dbim, dcre, dcim, dlr, dli, dd = _s5_bwd(
            sv["proj3"], 3 * aw // sw, dypre.reshape(bsz, seq, sw), sv["xr3"], sv["xi3"], *sv["lam"], *sv["s5c"], sv["dvec"], sw)
        dbb_re, dc_re = _s5_unblock(dbre, dcre, ng, gdim, npst, ncb)
        dbb_im, dc_im = _s5_unblock(dbim, dcim, ng, gdim, npst, ncb)
        da_re, da_im, dldt, dbt_re, dbt_im = _s5_params_bwd(
            *sv["s5in"], (dlr.reshape(ng, 1, npst), dli.reshape(ng, 1, npst), dbb_re, dbb_im))

        dq3, dk3, dv3 = _attn_bwd(sv["proj3"], ctab, stab, dattn.reshape(bsz, seq, aw), sv["attn3"], sv["lse3"], aw)
        dproj = [dq3.reshape(t_rows, aw), dk3.reshape(t_rows, aw), dv3.reshape(t_rows, aw), du3.reshape(t_rows, sw)]
        big_parts["w_in"][l] = _wgrad_call("wgrad_in", sv["h"], dproj, "y", ndev, ndev, (dm, in_w // ndev), tl)

        small_parts[l] = dict(
            attn_gain=dga, ssm_gain=dgs, ssm_a_re=da_re[:, 0], ssm_a_im=da_im[:, 0], ssm_log_dt=dldt[:, 0, 0], ssm_b_re=dbt_re.transpose(0, 2, 1),
            ssm_b_im=dbt_im.transpose(0, 2, 1), ssm_c_re=dc_re, ssm_c_im=dc_im, ssm_d=dd, b_glu=dbglu, b_out=dbout, ln1_g=dg1,
            ln1_b=db1, b_ff1=dbff1, b_ff2=dbff2, ln2_g=dg2, ln2_b=db2)

        layer_grads = [big_parts[n][l] for n in MIXER_NAMES] + [_pack_small([small_parts[l][n] for n in SMALL_NAMES])]
        if distributed:
            grad_handles[l], token = _exchange_start(f"grads_start_l{l}", layer_grads, grad_modes)
        else:
            grad_handles[l], token = layer_grads, no_token

        wpb = in_w // ndev
        if l > 0:
            prev = saved[l - 1]

            def in_bwd(rv, cr):
                dacc = alpha * rv[4]
                dp = jnp.concatenate([v_.astype(MXU_DTYPE) for v_ in rv[:4]], axis=-1)
                for j in range(ndev):
                    dacc = dacc + _mm_nt(dp[:, j * wpb:(j + 1) * wpb], cr[0][j])
                dpre, dg, db = _layer_norm_bwd(dacc, rv[5], cr[1][...])
                return [dpre], [dg, db, _colsum(dpre)]

            (dpre2,), (dg2, db2, dbff2) = _rows_call("in_bwd", in_bwd, dproj + [dpre1, prev["pre2"]],
                                                     [sv["w_in"], prev["g2"], token], [(dm, f32)], [(1, dm)] * 3, tl)
        else:
            def in_bwd0(rv, cr):
                dacc = alpha * rv[4]
                dp = jnp.concatenate([v_.astype(MXU_DTYPE) for v_ in rv[:4]], axis=-1)
                for j in range(ndev):
                    dacc = dacc + _mm_nt(dp[:, j * wpb:(j + 1) * wpb], cr[0][j])
                return [dacc], []

            (grad_x,), _ = _rows_call("in_bwd0", in_bwd0, dproj + [dpre1], [sv["w_in"], token], [(dm, f32)], [], tl)

    small_shapes = [weights[n].shape[1:] for n in SMALL_NAMES]
    outs = {n: None for n in BIG_NAMES}
    packed = [_pack_small([d[n] for n in SMALL_NAMES], lead=1) for d in (weights, moments_m, moments_v)]
    small_out = None
    after = grad_x
    for l in reversed(range(depth)):
        if distributed:
            recv_ff = _exchange_wait(f"grads_ff_wait_l{l}", ff_handles[l], ["scatter"] * len(FF_NAMES), after)
            recv = _exchange_wait(f"grads_wait_l{l}", grad_handles[l], grad_modes, recv_ff[0])
        else:
            recv_ff = ff_handles[l]
            recv = [g_[None] if i == len(MIXER_NAMES) else g_ for i, g_ in enumerate(grad_handles[l])]
        for n, parts in zip(FF_NAMES + MIXER_NAMES, list(recv_ff) + list(recv[:-1]), strict=True):
            outs[n] = _adamw_layer("adamw_" + n, parts, weights[n], moments_m[n], moments_v[n], l, outs[n],
                                   min(parts.shape[1], 256))
        small_out = _adamw_layer("adamw_small", recv[-1], *packed, l, small_out, recv[-1].shape[1])
        after = small_out[0]
    for k in range(4):
        row = 0
        for n, shp in zip(SMALL_NAMES, small_shapes, strict=True):
            sz = int(np.prod(shp))
            nrow = _piece_rows(sz)
            piece = small_out[k][:, row:row + nrow].reshape(depth, nrow * LANES)
            outs.setdefault(n, [None] * 4)
            outs[n][k] = piece[:, :sz].reshape((depth,) + tuple(shp))
            row += nrow

    result = [loss, grad_x.reshape(bsz, seq, dm)]
    for k in range(4):
        result += [outs[n][k] for n in WEIGHT_ORDER]
    return tuple(result)


def kernel(x, positions, w_in, attn_gain, ssm_gain, ssm_a_re, ssm_a_im, ssm_log_dt, ssm_b_re, ssm_b_im, ssm_c_re, ssm_c_im, ssm_d, w_glu, b_glu, w_out, b_out, ln1_g, ln1_b, w_ff1, b_ff1, w_ff2, b_ff2, ln2_g, ln2_b, loss_target, m_w_in, m_attn_gain, m_ssm_gain, m_ssm_a_re, m_ssm_a_im, m_ssm_log_dt, m_ssm_b_re, m_ssm_b_im, m_ssm_c_re, m_ssm_c_im, m_ssm_d, m_w_glu, m_b_glu, m_w_out, m_b_out, m_ln1_g, m_ln1_b, m_w_ff1, m_b_ff1, m_w_ff2, m_b_ff2, m_ln2_g, m_ln2_b, v_w_in, v_attn_gain, v_ssm_gain, v_ssm_a_re, v_ssm_a_im, v_ssm_log_dt, v_ssm_b_re, v_ssm_b_im, v_ssm_c_re, v_ssm_c_im, v_ssm_d, v_w_glu, v_b_glu, v_w_out, v_b_out, v_ln1_g, v_ln1_b, v_w_ff1, v_b_ff1, v_w_ff2, v_b_ff2, v_ln2_g, v_ln2_b):
    loc = locals()
    weights = {n: loc[n] for n in WEIGHT_ORDER}
    moments_m = {n: loc["m_" + n] for n in WEIGHT_ORDER}
    moments_v = {n: loc["v_" + n] for n in WEIGHT_ORDER}
    return _step(x, positions, weights, moments_m, moments_v, loss_target, distributed=True)
```
